```python
import jax, jax.numpy as jnp
from jax import lax
import numpy as np

D_MODEL = 1024
BATCH = 8
SEQ = 8192
DEPTH = 4

CHUNK = 64
Q_BLOCK = 128
PLE_DIM = 256
GDN_HEADS = 4
GDN_DK = 128
GDN_DV = 128
CONV_WIDTH = 4
MLA_HEADS = 4
MLA_NOPE = 128
MLA_ROPE = 64
MLA_V = 128
Q_LORA = 384
KV_LORA = 256
ROPE_THETA = 10000.0
D_FF = -(-(8 * D_MODEL) // (3 * 256)) * 256

GDN_QK = GDN_HEADS * GDN_DK
GDN_VW = GDN_HEADS * GDN_DV
MIX_WIDTH = GDN_VW + MLA_HEADS * MLA_V
IN_SIZES = (GDN_QK, GDN_QK, GDN_VW, GDN_VW, GDN_HEADS, GDN_HEADS, Q_LORA, KV_LORA, MLA_ROPE)
IN_SPLITS = tuple(int(v) for v in np.cumsum(IN_SIZES)[:-1])
IN_WIDTH = int(sum(IN_SIZES))
ALPHA = (2.0 * DEPTH) ** 0.25
BETA = (8.0 * DEPTH) ** -0.25
LN_EPS = 1e-5
RMS_EPS = 1e-6

kernel_name = 'hymba_gdn_mla_deepnorm_ple_trunk'


def layer_norm(x, g, b):
    xf = x.astype(jnp.float32)
    mu = jnp.mean(xf, -1, keepdims=True)
    var = jnp.mean(jnp.square(xf - mu), -1, keepdims=True)
    return ((xf - mu) * lax.rsqrt(var + LN_EPS) * g + b).astype(x.dtype)


def rms_norm(x, g):
    xf = x.astype(jnp.float32)
    return (xf * lax.rsqrt(jnp.mean(xf * xf, -1, keepdims=True) + RMS_EPS) * g).astype(x.dtype)


def l2_norm(x):
    xf = x.astype(jnp.float32)
    return xf * lax.rsqrt(jnp.sum(xf * xf, -1, keepdims=True) + RMS_EPS)


def rope_tables(positions):
    inv_freq = ROPE_THETA ** (-jnp.arange(0, MLA_ROPE, 2, dtype=jnp.float32) / MLA_ROPE)
    ang = positions.astype(jnp.float32)[..., None] * inv_freq
    return jnp.cos(ang), jnp.sin(ang)


def apply_rope(x, cos, sin):
    x1, x2 = jnp.split(x.astype(jnp.float32), 2, axis=-1)
    return jnp.concatenate([x1 * cos - x2 * sin, x2 * cos + x1 * sin], -1).astype(x.dtype)


def causal_dwconv(x, w):
    return lax.conv_general_dilated(
        x, w[:, None, :].astype(x.dtype), window_strides=(1,),
        padding=[(CONV_WIDTH - 1, 0)], dimension_numbers=('NWC', 'WIO', 'NWC'),
        feature_group_count=x.shape[-1])


def gated_delta_rule(q, k, v, g, beta):
    B, S, H, DK = q.shape
    DV = v.shape[-1]
    N = S // CHUNK
    f32 = jnp.float32

    def chunks(t):
        t = t.astype(f32).reshape((B, N, CHUNK, H) + t.shape[3:])
        return jnp.moveaxis(t, (1, 3), (0, 2))

    q = chunks(q) * DK ** -0.5
    k = chunks(k)
    v = chunks(v)
    beta = chunks(beta)
    g = jnp.cumsum(chunks(g), axis=-1)
    tri_incl = jnp.tril(jnp.ones((CHUNK, CHUNK), bool))
    tri_strict = jnp.tril(jnp.ones((CHUNK, CHUNK), bool), -1)
    decay = jnp.exp(jnp.where(tri_incl, g[..., :, None] - g[..., None, :], -jnp.inf))
    k_beta = k * beta[..., None]
    lower = jnp.where(tri_strict, jnp.einsum('nbhid,nbhjd->nbhij', k_beta, k) * decay, 0.0)
    rhs = jnp.concatenate([v * beta[..., None], k_beta * jnp.exp(g)[..., None]], -1)
    sol = lax.linalg.triangular_solve(lower + jnp.eye(CHUNK, dtype=f32), rhs,
                                      left_side=True, lower=True, unit_diagonal=True)
    u, w = sol[..., :DV], sol[..., DV:]
    attn = jnp.where(tri_incl, jnp.einsum('nbhid,nbhjd->nbhij', q, k) * decay, 0.0)
    g_last = g[..., -1]
    q_dec = q * jnp.exp(g)[..., None]
    k_dec = k * jnp.exp(g_last[..., None] - g)[..., None]

    def step(state, xs):
        q_c, k_c, u_c, w_c, attn_c, gl = xs
        v_new = u_c - jnp.einsum('bhck,bhkv->bhcv', w_c, state)
        o = jnp.einsum('bhck,bhkv->bhcv', q_c, state) + jnp.einsum('bhij,bhjv->bhiv', attn_c, v_new)
        state = state * jnp.exp(gl)[..., None, None] + jnp.einsum('bhck,bhcv->bhkv', k_c, v_new)
        return state, o

    s0 = jnp.zeros((B, H, DK, DV), f32)
    _, o = lax.scan(step, s0, (q_dec, k_dec, u, w, attn, g_last))
    return jnp.moveaxis(o, (0, 2), (1, 3)).reshape(B, S, H, DV)


def mla_attention(q_nope, q_rope, k_nope, k_rope, v):
    S = q_nope.shape[1]
    scale = (MLA_NOPE + MLA_ROPE) ** -0.5
    frame_chunk = jnp.arange(S) // CHUNK
    outs = []
    for blk in range(S // Q_BLOCK):
        q0, q1 = blk * Q_BLOCK, (blk + 1) * Q_BLOCK
        s = (jnp.einsum('bqhd,bkhd->bhqk', q_nope[:, q0:q1], k_nope[:, :q1])
             + jnp.einsum('bqhd,bkd->bhqk', q_rope[:, q0:q1], k_rope[:, :q1])).astype(jnp.float32) * scale
        mask = frame_chunk[None, :q1] <= frame_chunk[q0:q1, None]
        prob = jax.nn.softmax(jnp.where(mask, s, -jnp.inf), axis=-1).astype(v.dtype)
        outs.append(jnp.einsum('bhqk,bkhd->bqhd', prob, v[:, :q1]))
    return jnp.concatenate(outs, axis=1)


def hybrid_layer(x, p_i, cos, sin, w_in, conv_w, a_log, dt_bias, gdn_norm_g, q_norm_g,
                 w_uq, kv_norm_g, w_ukv, w_out, ln1_g, ln1_b, w_gate_up, w_down,
                 ln2_g, ln2_b, w_ple, w_ple_gate):
    B, S, _ = x.shape
    h = x @ w_in
    q, k, v, z, b, a, c_q, c_kv, k_r = jnp.split(h, IN_SPLITS, axis=-1)

    qkv = jax.nn.silu(causal_dwconv(jnp.concatenate([q, k, v], -1), conv_w))
    q, k, v = jnp.split(qkv, [GDN_QK, 2 * GDN_QK], axis=-1)
    q = l2_norm(q.reshape(B, S, GDN_HEADS, GDN_DK))
    k = l2_norm(k.reshape(B, S, GDN_HEADS, GDN_DK))
    v = v.reshape(B, S, GDN_HEADS, GDN_DV)
    beta = jax.nn.sigmoid(b.astype(jnp.float32))
    g = -jnp.exp(a_log.astype(jnp.float32)) * jax.nn.softplus(a.astype(jnp.float32) + dt_bias)
    o_gdn = gated_delta_rule(q, k, v, g, beta)
    o_gdn = rms_norm(o_gdn, gdn_norm_g) * jax.nn.silu(z.reshape(B, S, GDN_HEADS, GDN_DV).astype(jnp.float32))

    qm = (rms_norm(c_q, q_norm_g) @ w_uq).reshape(B, S, MLA_HEADS, MLA_NOPE + MLA_ROPE)
    q_nope, q_rope = jnp.split(qm, [MLA_NOPE], axis=-1)
    q_rope = apply_rope(q_rope, cos[:, :, None, :], sin[:, :, None, :])
    kv = (rms_norm(c_kv, kv_norm_g) @ w_ukv).reshape(B, S, MLA_HEADS, MLA_NOPE + MLA_V)
    k_nope, v_m = jnp.split(kv, [MLA_NOPE], axis=-1)
    k_rope = apply_rope(k_r, cos, sin)
    o_mla = mla_attention(q_nope, q_rope, k_nope, k_rope, v_m)

    mix = jnp.concatenate([o_gdn.reshape(B, S, GDN_VW).astype(x.dtype),
                           o_mla.reshape(B, S, MLA_HEADS * MLA_V)], axis=-1) @ w_out
    x = layer_norm(ALPHA * x + mix, ln1_g, ln1_b)

    gate, up = jnp.split(x @ w_gate_up, 2, axis=-1)
    x = layer_norm(ALPHA * x + (jax.nn.silu(gate) * up) @ w_down, ln2_g, ln2_b)

    return x + jax.nn.sigmoid(x @ w_ple_gate) * (p_i @ w_ple)


def _fwd_setup_inputs(seed: int = 0) -> dict:
    key = jax.random.key(seed)
    ks = jax.random.split(key, 24)
    f32 = jnp.float32
    nrm = lambda k, shape, scale: jax.random.normal(k, shape, f32) * scale
    x = jax.random.normal(ks[0], (BATCH, SEQ, D_MODEL), f32)
    p = jax.random.normal(ks[1], (DEPTH, BATCH, SEQ, PLE_DIM), f32)
    offset = jax.random.randint(ks[2], (BATCH, 1), 0, 64) * CHUNK
    positions = (offset + jnp.arange(SEQ, dtype=jnp.int32)[None, :]).astype(jnp.int32)
    dt = jnp.exp(jax.random.uniform(ks[3], (DEPTH, GDN_HEADS), f32) * (np.log(0.1) - np.log(1e-3)) + np.log(1e-3))
    return {
        'x': x,
        'p': p,
        'positions': positions,
        'w_in': nrm(ks[4], (DEPTH, D_MODEL, IN_WIDTH), D_MODEL ** -0.5),
        'conv_w': nrm(ks[5], (DEPTH, CONV_WIDTH, 2 * GDN_QK + GDN_VW), CONV_WIDTH ** -0.5),
        'a_log': jnp.log(jax.random.uniform(ks[6], (DEPTH, GDN_HEADS), f32, 1.0, 16.0)),
        'dt_bias': dt + jnp.log(-jnp.expm1(-dt)),
        'gdn_norm_g': 1.0 + nrm(ks[7], (DEPTH, GDN_DV), 0.1),
        'q_norm_g': 1.0 + nrm(ks[8], (DEPTH, Q_LORA), 0.1),
        'w_uq': nrm(ks[9], (DEPTH, Q_LORA, MLA_HEADS * (MLA_NOPE + MLA_ROPE)), Q_LORA ** -0.5),
        'kv_norm_g': 1.0 + nrm(ks[10], (DEPTH, KV_LORA), 0.1),
        'w_ukv': nrm(ks[11], (DEPTH, KV_LORA, MLA_HEADS * (MLA_NOPE + MLA_V)), KV_LORA ** -0.5),
        'w_out': nrm(ks[12], (DEPTH, MIX_WIDTH, D_MODEL), MIX_WIDTH ** -0.5 * BETA),
        'ln1_g': 1.0 + nrm(ks[13], (DEPTH, D_MODEL), 0.1),
        'ln1_b': nrm(ks[14], (DEPTH, D_MODEL), 0.02),
        'w_gate_up': nrm(ks[15], (DEPTH, D_MODEL, 2 * D_FF), D_MODEL ** -0.5),
        'w_down': nrm(ks[16], (DEPTH, D_FF, D_MODEL), D_FF ** -0.5 * BETA),
        'ln2_g': 1.0 + nrm(ks[17], (DEPTH, D_MODEL), 0.1),
        'ln2_b': nrm(ks[18], (DEPTH, D_MODEL), 0.02),
        'w_ple': nrm(ks[19], (DEPTH, PLE_DIM, D_MODEL), PLE_DIM ** -0.5),
        'w_ple_gate': nrm(ks[20], (DEPTH, D_MODEL, D_MODEL), D_MODEL ** -0.5),
    }


def _fwd_reference(x, p, positions, w_in, conv_w, a_log, dt_bias, gdn_norm_g, q_norm_g, w_uq,
              kv_norm_g, w_ukv, w_out, ln1_g, ln1_b, w_gate_up, w_down, ln2_g, ln2_b,
              w_ple, w_ple_gate):
    cos, sin = rope_tables(positions)
    for i in range(DEPTH):
        x = hybrid_layer(x, p[i], cos, sin, w_in[i], conv_w[i], a_log[i], dt_bias[i],
                         gdn_norm_g[i], q_norm_g[i], w_uq[i], kv_norm_g[i], w_ukv[i],
                         w_out[i], ln1_g[i], ln1_b[i], w_gate_up[i], w_down[i],
                         ln2_g[i], ln2_b[i], w_ple[i], w_ple_gate[i])
    return x


import jax as _jax
import jax.numpy as _jnp

TWIN_FORMAT = 'train_step'
FWD_PARAMS = ['x', 'p', 'positions', 'w_in', 'conv_w', 'a_log', 'dt_bias', 'gdn_norm_g', 'q_norm_g', 'w_uq', 'kv_norm_g', 'w_ukv', 'w_out', 'ln1_g', 'ln1_b', 'w_gate_up', 'w_down', 'ln2_g', 'ln2_b', 'w_ple', 'w_ple_gate']
TWIN_WEIGHTS = ['w_in', 'conv_w', 'a_log', 'dt_bias', 'gdn_norm_g', 'q_norm_g', 'w_uq', 'kv_norm_g', 'w_ukv', 'w_out', 'ln1_g', 'ln1_b', 'w_gate_up', 'w_down', 'ln2_g', 'ln2_b', 'w_ple', 'w_ple_gate']
TWIN_DIFF_INPUT = 'x'
TWIN_INPUTS = ['x', 'p', 'positions', 'w_in', 'conv_w', 'a_log', 'dt_bias', 'gdn_norm_g', 'q_norm_g', 'w_uq', 'kv_norm_g', 'w_ukv', 'w_out', 'ln1_g', 'ln1_b', 'w_gate_up', 'w_down', 'ln2_g', 'ln2_b', 'w_ple', 'w_ple_gate', 'loss_target', 'm_w_in', 'm_conv_w', 'm_a_log', 'm_dt_bias', 'm_gdn_norm_g', 'm_q_norm_g', 'm_w_uq', 'm_kv_norm_g', 'm_w_ukv', 'm_w_out', 'm_ln1_g', 'm_ln1_b', 'm_w_gate_up', 'm_w_down', 'm_ln2_g', 'm_ln2_b', 'm_w_ple', 'm_w_ple_gate', 'v_w_in', 'v_conv_w', 'v_a_log', 'v_dt_bias', 'v_gdn_norm_g', 'v_q_norm_g', 'v_w_uq', 'v_kv_norm_g', 'v_w_ukv', 'v_w_out', 'v_ln1_g', 'v_ln1_b', 'v_w_gate_up', 'v_w_down', 'v_ln2_g', 'v_ln2_b', 'v_w_ple', 'v_w_ple_gate']
TWIN_OUTPUTS = ['loss', 'grad_x', 'grad_w_in', 'grad_conv_w', 'grad_a_log', 'grad_dt_bias', 'grad_gdn_norm_g', 'grad_q_norm_g', 'grad_w_uq', 'grad_kv_norm_g', 'grad_w_ukv', 'grad_w_out', 'grad_ln1_g', 'grad_ln1_b', 'grad_w_gate_up', 'grad_w_down', 'grad_ln2_g', 'grad_ln2_b', 'grad_w_ple', 'grad_w_ple_gate', 'delta_w_in', 'delta_conv_w', 'delta_a_log', 'delta_dt_bias', 'delta_gdn_norm_g', 'delta_q_norm_g', 'delta_w_uq', 'delta_kv_norm_g', 'delta_w_ukv', 'delta_w_out', 'delta_ln1_g', 'delta_ln1_b', 'delta_w_gate_up', 'delta_w_down', 'delta_ln2_g', 'delta_ln2_b', 'delta_w_ple', 'delta_w_ple_gate', 'new_m_w_in', 'new_m_conv_w', 'new_m_a_log', 'new_m_dt_bias', 'new_m_gdn_norm_g', 'new_m_q_norm_g', 'new_m_w_uq', 'new_m_kv_norm_g', 'new_m_w_ukv', 'new_m_w_out', 'new_m_ln1_g', 'new_m_ln1_b', 'new_m_w_gate_up', 'new_m_w_down', 'new_m_ln2_g', 'new_m_ln2_b', 'new_m_w_ple', 'new_m_w_ple_gate', 'new_v_w_in', 'new_v_conv_w', 'new_v_a_log', 'new_v_dt_bias', 'new_v_gdn_norm_g', 'new_v_q_norm_g', 'new_v_w_uq', 'new_v_kv_norm_g', 'new_v_w_ukv', 'new_v_w_out', 'new_v_ln1_g', 'new_v_ln1_b', 'new_v_w_gate_up', 'new_v_w_down', 'new_v_ln2_g', 'new_v_ln2_b', 'new_v_w_ple', 'new_v_w_ple_gate']
TWIN_LEAF_KINDS = {'loss': 'loss', 'grad_x': 'grad_x', 'grad_w_in': 'grad_w', 'grad_conv_w': 'grad_w', 'grad_a_log': 'grad_w', 'grad_dt_bias': 'grad_w', 'grad_gdn_norm_g': 'grad_w', 'grad_q_norm_g': 'grad_w', 'grad_w_uq': 'grad_w', 'grad_kv_norm_g': 'grad_w', 'grad_w_ukv': 'grad_w', 'grad_w_out': 'grad_w', 'grad_ln1_g': 'grad_w', 'grad_ln1_b': 'grad_w', 'grad_w_gate_up': 'grad_w', 'grad_w_down': 'grad_w', 'grad_ln2_g': 'grad_w', 'grad_ln2_b': 'grad_w', 'grad_w_ple': 'grad_w', 'grad_w_ple_gate': 'grad_w', 'delta_w_in': 'delta_w', 'delta_conv_w': 'delta_w', 'delta_a_log': 'delta_w', 'delta_dt_bias': 'delta_w', 'delta_gdn_norm_g': 'delta_w', 'delta_q_norm_g': 'delta_w', 'delta_w_uq': 'delta_w', 'delta_kv_norm_g': 'delta_w', 'delta_w_ukv': 'delta_w', 'delta_w_out': 'delta_w', 'delta_ln1_g': 'delta_w', 'delta_ln1_b': 'delta_w', 'delta_w_gate_up': 'delta_w', 'delta_w_down': 'delta_w', 'delta_ln2_g': 'delta_w', 'delta_ln2_b': 'delta_w', 'delta_w_ple': 'delta_w', 'delta_w_ple_gate': 'delta_w', 'new_m_w_in': 'new_m', 'new_m_conv_w': 'new_m', 'new_m_a_log': 'new_m', 'new_m_dt_bias': 'new_m', 'new_m_gdn_norm_g': 'new_m', 'new_m_q_norm_g': 'new_m', 'new_m_w_uq': 'new_m', 'new_m_kv_norm_g': 'new_m', 'new_m_w_ukv': 'new_m', 'new_m_w_out': 'new_m', 'new_m_ln1_g': 'new_m', 'new_m_ln1_b': 'new_m', 'new_m_w_gate_up': 'new_m', 'new_m_w_down': 'new_m', 'new_m_ln2_g': 'new_m', 'new_m_ln2_b': 'new_m', 'new_m_w_ple': 'new_m', 'new_m_w_ple_gate': 'new_m', 'new_v_w_in': 'new_v', 'new_v_conv_w': 'new_v', 'new_v_a_log': 'new_v', 'new_v_dt_bias': 'new_v', 'new_v_gdn_norm_g': 'new_v', 'new_v_q_norm_g': 'new_v', 'new_v_w_uq': 'new_v', 'new_v_kv_norm_g': 'new_v', 'new_v_w_ukv': 'new_v', 'new_v_w_out': 'new_v', 'new_v_ln1_g': 'new_v', 'new_v_ln1_b': 'new_v', 'new_v_w_gate_up': 'new_v', 'new_v_w_down': 'new_v', 'new_v_ln2_g': 'new_v', 'new_v_ln2_b': 'new_v', 'new_v_w_ple': 'new_v', 'new_v_w_ple_gate': 'new_v'}


def _forward(args):
    return _fwd_reference(*[args[k] for k in FWD_PARAMS])


def _output_shape():
    def fwd():
        inp = _fwd_setup_inputs(0)
        return _fwd_reference(*[inp[k] for k in FWD_PARAMS])
    out = _jax.eval_shape(fwd)
    return out.shape, out.dtype

N_MICROBATCH = 1
ADAM_LR = 0.001
ADAM_B1 = 0.9
ADAM_B2 = 0.999
ADAM_EPS = 1e-08
ADAM_WD = 0.01
ADAM_STEP = 10
PER_EXAMPLE_BATCH_AXIS = {'x': 0, 'p': 1, 'positions': 0, 'loss_target': 0}
SHARED_INPUTS = []
_WEIGHT_DTYPES = {'w_in': _jnp.float32, 'conv_w': _jnp.float32, 'a_log': _jnp.float32, 'dt_bias': _jnp.float32, 'gdn_norm_g': _jnp.float32, 'q_norm_g': _jnp.float32, 'w_uq': _jnp.float32, 'kv_norm_g': _jnp.float32, 'w_ukv': _jnp.float32, 'w_out': _jnp.float32, 'ln1_g': _jnp.float32, 'ln1_b': _jnp.float32, 'w_gate_up': _jnp.float32, 'w_down': _jnp.float32, 'ln2_g': _jnp.float32, 'ln2_b': _jnp.float32, 'w_ple': _jnp.float32, 'w_ple_gate': _jnp.float32}
MOMENT_SCALE = {'w_in': 3.180182e-02, 'conv_w': 5.903697e-02, 'a_log': 3.722733e-01, 'dt_bias': 3.289565e-01, 'gdn_norm_g': 3.014811e-01, 'q_norm_g': 1.169752e-02, 'w_uq': 8.131599e-03, 'kv_norm_g': 4.738449e-02, 'w_ukv': 2.138862e-02, 'w_out': 2.627154e-01, 'ln1_g': 9.355370e+00, 'ln1_b': 5.385541e+00, 'w_gate_up': 2.627488e-02, 'w_down': 1.045161e-01, 'ln2_g': 3.572688e+01, 'ln2_b': 5.706339e+00, 'w_ple': 4.319802e-01, 'w_ple_gate': 1.518366e-01}


def _to_microbatches(a, axis):
    t = _jnp.moveaxis(a, axis, 0)
    t = t.reshape((N_MICROBATCH, t.shape[0] // N_MICROBATCH) + t.shape[1:])
    return _jnp.moveaxis(t, 1, axis + 1)


def setup_inputs(seed: int = 0) -> dict:
    inp = _fwd_setup_inputs(seed)
    key = _jax.random.fold_in(_jax.random.key(seed), 7919)
    shape, _ = _output_shape()
    out = dict(inp)
    out["loss_target"] = _jax.random.normal(_jax.random.fold_in(key, 0), shape, _jnp.float32)
    for i, name in enumerate(TWIN_WEIGHTS):
        w = inp[name].astype(_jnp.float32)
        if MOMENT_SCALE is None:
            s = _jnp.sqrt(_jnp.mean(_jnp.square(w)) + 1e-30)
        else:
            s = MOMENT_SCALE[name]
        km, kv = _jax.random.split(_jax.random.fold_in(key, i + 1))
        out[name] = w
        out["m_" + name] = s * _jax.random.normal(km, w.shape, _jnp.float32)
        out["v_" + name] = (s * s) * _jax.random.uniform(kv, w.shape, _jnp.float32, 0.5, 1.5)
    if N_MICROBATCH > 1:
        for name, axis in PER_EXAMPLE_BATCH_AXIS.items():
            out[name] = _to_microbatches(out[name], axis)
    return {'x': out['x'], 'p': out['p'], 'positions': out['positions'], 'w_in': out['w_in'], 'conv_w': out['conv_w'], 'a_log': out['a_log'], 'dt_bias': out['dt_bias'], 'gdn_norm_g': out['gdn_norm_g'], 'q_norm_g': out['q_norm_g'], 'w_uq': out['w_uq'], 'kv_norm_g': out['kv_norm_g'], 'w_ukv': out['w_ukv'], 'w_out': out['w_out'], 'ln1_g': out['ln1_g'], 'ln1_b': out['ln1_b'], 'w_gate_up': out['w_gate_up'], 'w_down': out['w_down'], 'ln2_g': out['ln2_g'], 'ln2_b': out['ln2_b'], 'w_ple': out['w_ple'], 'w_ple_gate': out['w_ple_gate'], 'loss_target': out['loss_target'], 'm_w_in': out['m_w_in'], 'm_conv_w': out['m_conv_w'], 'm_a_log': out['m_a_log'], 'm_dt_bias': out['m_dt_bias'], 'm_gdn_norm_g': out['m_gdn_norm_g'], 'm_q_norm_g': out['m_q_norm_g'], 'm_w_uq': out['m_w_uq'], 'm_kv_norm_g': out['m_kv_norm_g'], 'm_w_ukv': out['m_w_ukv'], 'm_w_out': out['m_w_out'], 'm_ln1_g': out['m_ln1_g'], 'm_ln1_b': out['m_ln1_b'], 'm_w_gate_up': out['m_w_gate_up'], 'm_w_down': out['m_w_down'], 'm_ln2_g': out['m_ln2_g'], 'm_ln2_b': out['m_ln2_b'], 'm_w_ple': out['m_w_ple'], 'm_w_ple_gate': out['m_w_ple_gate'], 'v_w_in': out['v_w_in'], 'v_conv_w': out['v_conv_w'], 'v_a_log': out['v_a_log'], 'v_dt_bias': out['v_dt_bias'], 'v_gdn_norm_g': out['v_gdn_norm_g'], 'v_q_norm_g': out['v_q_norm_g'], 'v_w_uq': out['v_w_uq'], 'v_kv_norm_g': out['v_kv_norm_g'], 'v_w_ukv': out['v_w_ukv'], 'v_w_out': out['v_w_out'], 'v_ln1_g': out['v_ln1_g'], 'v_ln1_b': out['v_ln1_b'], 'v_w_gate_up': out['v_w_gate_up'], 'v_w_down': out['v_w_down'], 'v_ln2_g': out['v_ln2_g'], 'v_ln2_b': out['v_ln2_b'], 'v_w_ple': out['v_w_ple'], 'v_w_ple_gate': out['v_w_ple_gate']}


def _loss(weights, diff, rest, loss_target):
    with _jax.named_scope("forward"):
        args = {**rest, TWIN_DIFF_INPUT: diff, **{k: w.astype(_WEIGHT_DTYPES[k]) for k, w in weights.items()}}
        y = _forward(args)
    with _jax.named_scope("loss_head"):
        err = _jnp.square(y.astype(_jnp.float32) - loss_target)
        return 0.5 * _jnp.sum(_jnp.mean(err, axis=-1)) if err.ndim else 0.5 * err


def _adamw(w, g, m, v):
    m = ADAM_B1 * m + (1.0 - ADAM_B1) * g
    v = ADAM_B2 * v + (1.0 - ADAM_B2) * _jnp.square(g)
    m_hat = m / (1.0 - ADAM_B1 ** ADAM_STEP)
    v_hat = v / (1.0 - ADAM_B2 ** ADAM_STEP)
    delta = -ADAM_LR * (m_hat / (_jnp.sqrt(v_hat) + ADAM_EPS) + ADAM_WD * w)
    return delta, m, v


def reference(x, p, positions, w_in, conv_w, a_log, dt_bias, gdn_norm_g, q_norm_g, w_uq, kv_norm_g, w_ukv, w_out, ln1_g, ln1_b, w_gate_up, w_down, ln2_g, ln2_b, w_ple, w_ple_gate, loss_target, m_w_in, m_conv_w, m_a_log, m_dt_bias, m_gdn_norm_g, m_q_norm_g, m_w_uq, m_kv_norm_g, m_w_ukv, m_w_out, m_ln1_g, m_ln1_b, m_w_gate_up, m_w_down, m_ln2_g, m_ln2_b, m_w_ple, m_w_ple_gate, v_w_in, v_conv_w, v_a_log, v_dt_bias, v_gdn_norm_g, v_q_norm_g, v_w_uq, v_kv_norm_g, v_w_ukv, v_w_out, v_ln1_g, v_ln1_b, v_w_gate_up, v_w_down, v_ln2_g, v_ln2_b, v_w_ple, v_w_ple_gate):
    given = dict(x=x, p=p, positions=positions, w_in=w_in, conv_w=conv_w, a_log=a_log, dt_bias=dt_bias, gdn_norm_g=gdn_norm_g, q_norm_g=q_norm_g, w_uq=w_uq, kv_norm_g=kv_norm_g, w_ukv=w_ukv, w_out=w_out, ln1_g=ln1_g, ln1_b=ln1_b, w_gate_up=w_gate_up, w_down=w_down, ln2_g=ln2_g, ln2_b=ln2_b, w_ple=w_ple, w_ple_gate=w_ple_gate, loss_target=loss_target, m_w_in=m_w_in, m_conv_w=m_conv_w, m_a_log=m_a_log, m_dt_bias=m_dt_bias, m_gdn_norm_g=m_gdn_norm_g, m_q_norm_g=m_q_norm_g, m_w_uq=m_w_uq, m_kv_norm_g=m_kv_norm_g, m_w_ukv=m_w_ukv, m_w_out=m_w_out, m_ln1_g=m_ln1_g, m_ln1_b=m_ln1_b, m_w_gate_up=m_w_gate_up, m_w_down=m_w_down, m_ln2_g=m_ln2_g, m_ln2_b=m_ln2_b, m_w_ple=m_w_ple, m_w_ple_gate=m_w_ple_gate, v_w_in=v_w_in, v_conv_w=v_conv_w, v_a_log=v_a_log, v_dt_bias=v_dt_bias, v_gdn_norm_g=v_gdn_norm_g, v_q_norm_g=v_q_norm_g, v_w_uq=v_w_uq, v_kv_norm_g=v_kv_norm_g, v_w_ukv=v_w_ukv, v_w_out=v_w_out, v_ln1_g=v_ln1_g, v_ln1_b=v_ln1_b, v_w_gate_up=v_w_gate_up, v_w_down=v_w_down, v_ln2_g=v_ln2_g, v_ln2_b=v_ln2_b, v_w_ple=v_w_ple, v_w_ple_gate=v_w_ple_gate)
    weights = {n: given[n] for n in TWIN_WEIGHTS}
    shared = {n: given[n] for n in SHARED_INPUTS}
    per_example = {n: given[n] for n in ['x', 'p', 'positions']}
    grad_fn = _jax.value_and_grad(_loss, argnums=(0, 1))

    def one_microbatch(ex, loss_target):
        ex = dict(ex)
        diff = ex.pop(TWIN_DIFF_INPUT)
        return grad_fn(weights, diff, {**shared, **ex}, loss_target)

    if N_MICROBATCH == 1:
        loss, (grad_w, grad_x) = one_microbatch(per_example, given["loss_target"])
    else:
        def body(carry, xs):
            loss_sum, grad_sum = carry
            l_k, (gw_k, gx_k) = one_microbatch(xs[0], xs[1])
            with _jax.named_scope("update"):
                return (loss_sum + l_k, _jax.tree.map(_jnp.add, grad_sum, gw_k)), gx_k

        init = (_jnp.zeros((), _jnp.float32), _jax.tree.map(_jnp.zeros_like, weights))
        (loss, grad_w), grad_x = _jax.lax.scan(body, init, (per_example, given["loss_target"]))
    with _jax.named_scope("update"):
        delta_w, new_m, new_v = {}, {}, {}
        for n in TWIN_WEIGHTS:
            delta_w[n], new_m[n], new_v[n] = _adamw(weights[n], grad_w[n], given["m_" + n], given["v_" + n])
    return (loss, grad_x, *[grad_w[n] for n in TWIN_WEIGHTS], *[delta_w[n] for n in TWIN_WEIGHTS],
            *[new_m[n] for n in TWIN_WEIGHTS], *[new_v[n] for n in TWIN_WEIGHTS])
```

```python
import functools

import jax
import jax.numpy as jnp
from jax import lax
from jax.experimental import pallas as pl
from jax.experimental.pallas import tpu as pltpu

F32 = jnp.float32
BF16 = jnp.bfloat16
HIGHEST = lax.Precision.HIGHEST
MESH = pl.DeviceIdType.MESH

CHUNK = 64
N_HEADS = 4
HEAD_DIM = 128
ROPE_DIM = 64
ROPE_THETA = 10000.0
LN_EPS = 1e-5
RMS_EPS = 1e-6
ADAM_LR, ADAM_B1, ADAM_B2, ADAM_EPS, ADAM_WD, ADAM_STEP = 0.001, 0.9, 0.999, 1e-08, 0.01, 10

LANES = 128
VMEM_LIMIT = 48 * 1024 * 1024
PACK_W = 512
ROW_TILE = 256
SUB_ROWS = 16

MISC_BETA0 = ROPE_DIM
MISC_A0 = ROPE_DIM + N_HEADS

NN = (((1,), (0,)), ((), ()))
NT = (((1,), (1,)), ((), ()))
TN = (((0,), (0,)), ((), ()))


def _params(sem=None):
    return pltpu.CompilerParams(dimension_semantics=sem, vmem_limit_bytes=VMEM_LIMIT)


def _divisor_tile(dim, target, unit):
    best = None
    t = unit
    while t <= min(dim, target):
        if dim % t == 0:
            best = t
        t += unit
    return best if best is not None else dim


def _make_dots(high_precision):
    def raw(a, b, dims):
        if high_precision:
            return lax.dot_general(a, b, dims, precision=HIGHEST, preferred_element_type=F32)
        return lax.dot_general(a.astype(BF16), b.astype(BF16), dims, preferred_element_type=F32)

    @jax.custom_vjp
    def nn(a, b):
        return raw(a, b, NN)

    @jax.custom_vjp
    def nt(a, b):
        return raw(a, b, NT)

    @jax.custom_vjp
    def tn(a, b):
        return raw(a, b, TN)

    nn.defvjp(lambda a, b: (raw(a, b, NN), (a, b)), lambda r, g: (nt(g, r[1]), tn(r[0], g)))
    nt.defvjp(lambda a, b: (raw(a, b, NT), (a, b)), lambda r, g: (nn(g, r[1]), tn(g, r[0])))
    tn.defvjp(lambda a, b: (raw(a, b, TN), (a, b)), lambda r, g: (nt(r[1], g), nn(r[0], g)))
    return nn, nt, tn


_nn, _nt, _tn = _make_dots(False)
_hnn, _hnt, _htn = _make_dots(True)


def _matmul(a, b, *, dims, name, c=None, out_dtype=F32, tm=1024, tn=512, tk=1408):
    if dims == "nn":
        (m, k), (k2, n) = a.shape, b.shape
    elif dims == "nt":
        (m, k), (n, k2) = a.shape, b.shape
    else:
        (k, m), (k2, n) = a.shape, b.shape
    assert k == k2, (a.shape, b.shape, dims)
    tm = _divisor_tile(m, tm, LANES)
    tn = _divisor_tile(n, tn, LANES)
    tk = _divisor_tile(k, tk, LANES)
    nk = k // tk
    dn = {"nn": NN, "nt": NT, "tn": TN}[dims]
    if dims == "tn":
        a_spec = pl.BlockSpec((tk, tm), lambda i, j, kk: (kk, i))
    else:
        a_spec = pl.BlockSpec((tm, tk), lambda i, j, kk: (i, kk))
    if dims == "nt":
        b_spec = pl.BlockSpec((tn, tk), lambda i, j, kk: (j, kk))
    else:
        b_spec = pl.BlockSpec((tk, tn), lambda i, j, kk: (kk, j))
    o_spec = pl.BlockSpec((tm, tn), lambda i, j, kk: (i, j))
    has_c = c is not None

    def body(*refs):
        if has_c:
            a_ref, b_ref, c_ref, o_ref, acc_ref = refs
        else:
            a_ref, b_ref, o_ref, acc_ref = refs
        kk = pl.program_id(2)

        @pl.when(kk == 0)
        def _():
            if has_c:
                acc_ref[...] = c_ref[...].astype(F32)
            else:
                acc_ref[...] = jnp.zeros_like(acc_ref)

        acc_ref[...] += lax.dot_general(a_ref[...].astype(BF16), b_ref[...].astype(BF16), dn,
                                        preferred_element_type=F32)

        @pl.when(kk == nk - 1)
        def _():
            o_ref[...] = acc_ref[...].astype(o_ref.dtype)

    ins = [a, b] + ([c] if has_c else [])
    specs = [a_spec, b_spec] + ([o_spec] if has_c else [])
    return pl.pallas_call(
        body, name=name, grid=(m // tm, n // tn, nk), in_specs=specs, out_specs=o_spec,
        out_shape=jax.ShapeDtypeStruct((m, n), out_dtype),
        scratch_shapes=[pltpu.VMEM((tm, tn), F32)],
        compiler_params=_params(("arbitrary", "arbitrary", "arbitrary")),
    )(*ins)


def _rowwise(fn, rows, params, outs, accs=(), *, name, tm=ROW_TILE, sub=SUB_ROWS):
    t = rows[0][0].shape[0]
    tm = min(tm, t)
    assert t % tm == 0 and tm % sub == 0
    n_rows, n_par, n_out, n_acc = len(rows), len(params), len(outs), len(accs)

    def body(*refs):
        row_refs = refs[:n_rows]
        par_refs = refs[n_rows:n_rows + n_par]
        out_refs = refs[n_rows + n_par:n_rows + n_par + n_out]
        acc_refs = refs[n_rows + n_par + n_out:]
        if n_acc:
            @pl.when(pl.program_id(0) == 0)
            def _():
                for a_ref in acc_refs:
                    a_ref[...] = jnp.zeros_like(a_ref)

        def step(r, carry):
            sl = pl.ds(pl.multiple_of(r * sub, sub), sub)
            vals = [ref[sl, :].astype(F32) for ref in row_refs] + [ref[...] for ref in par_refs]
            res = fn(*vals)
            for o_ref, val in zip(out_refs, res[:n_out]):
                o_ref[sl, :] = val.astype(o_ref.dtype)
            for a_ref, val in zip(acc_refs, res[n_out:]):
                a_ref[...] += val
            return carry

        lax.fori_loop(0, tm // sub, step, 0)

    in_specs = [pl.BlockSpec((tm, w), functools.partial(lambda i, cb: (i, cb), cb=cb)) for _, w, cb in rows]
    in_specs += [pl.BlockSpec(p.shape, lambda i: (0, 0)) for p in params]
    out_specs = [pl.BlockSpec((tm, w), lambda i: (i, 0)) for w, _ in outs]
    out_specs += [pl.BlockSpec(s, lambda i: (0, 0)) for s in accs]
    out_shape = [jax.ShapeDtypeStruct((t, w), d) for w, d in outs]
    out_shape += [jax.ShapeDtypeStruct(s, F32) for s in accs]
    return pl.pallas_call(
        body, name=name, grid=(t // tm,), in_specs=in_specs, out_specs=out_specs, out_shape=out_shape,
        compiler_params=_params(("arbitrary",)),
    )(*[r[0] for r in rows], *params)


def _vjp_fn(fn, n_in, n_out):
    def bwd(*args):
        ins, cts = args[:n_in], args[n_in:]
        _, pull = jax.vjp(fn, *ins)
        return pull(tuple(cts) if n_out > 1 else cts[0])
    return bwd


def _lane(shape):
    return lax.broadcasted_iota(jnp.int32, shape, 1)


def _silu(x):
    return x * jax.nn.sigmoid(x)


def _softplus(x):
    return jnp.maximum(x, 0.0) + jnp.log1p(jnp.exp(-jnp.abs(x)))


def _heads(x, width=HEAD_DIM):
    return [x[:, h * width:(h + 1) * width] for h in range(N_HEADS)]


def _layer_norm(z, g, b):
    mu = jnp.mean(z, -1, keepdims=True)
    zc = z - mu
    var = jnp.mean(zc * zc, -1, keepdims=True)
    return zc * lax.rsqrt(var + LN_EPS) * g + b


def _gdn_act(u, misc, alog_row, dtb_row):
    s = _silu(u)
    w = N_HEADS * HEAD_DIM
    q = jnp.concatenate([t * lax.rsqrt(jnp.sum(t * t, -1, keepdims=True) + RMS_EPS) * HEAD_DIM ** -0.5
                         for t in _heads(s[:, :w])], axis=1)
    k = jnp.concatenate([t * lax.rsqrt(jnp.sum(t * t, -1, keepdims=True) + RMS_EPS)
                         for t in _heads(s[:, w:2 * w])], axis=1)
    v = s[:, 2 * w:]
    lane = _lane(misc.shape)
    beta = jax.nn.sigmoid(misc)
    g = -jnp.exp(alog_row) * _softplus(misc + dtb_row)
    is_beta = (lane >= MISC_BETA0) & (lane < MISC_BETA0 + N_HEADS)
    is_g = (lane >= MISC_A0) & (lane < MISC_A0 + N_HEADS)
    gb = jnp.where(is_beta, beta, jnp.where(is_g, g, 0.0))
    return q, k, v, gb


def _gdn_out(o, z, gn_row):
    outs = []
    for oh, zh in zip(_heads(o), _heads(z)):
        r = oh * lax.rsqrt(jnp.mean(oh * oh, -1, keepdims=True) + RMS_EPS) * gn_row
        outs.append(r * _silu(zh))
    return jnp.concatenate(outs, axis=1)


def _mla_norm(ckv, cq, kvg_row, qg_row):
    cqn = cq * lax.rsqrt(jnp.mean(cq * cq, -1, keepdims=True) + RMS_EPS) * qg_row
    ckvn = ckv * lax.rsqrt(jnp.mean(ckv * ckv, -1, keepdims=True) + RMS_EPS) * kvg_row
    return cqn, ckvn


def _swap_halves(x):
    half = ROPE_DIM // 2
    return jnp.where(_lane(x.shape) < half, pltpu.roll(x, LANES - half, 1), pltpu.roll(x, half, 1))


@jax.custom_vjp
def _rope(x, cos_t, sin_t):
    return x * cos_t + _swap_halves(x) * sin_t


def _rope_fwd(x, cos_t, sin_t):
    return _rope(x, cos_t, sin_t), (cos_t, sin_t)


def _rope_bwd(res, g):
    cos_t, sin_t = res
    return g * cos_t - _swap_halves(g) * sin_t, jnp.zeros_like(cos_t), jnp.zeros_like(sin_t)


_rope.defvjp(_rope_fwd, _rope_bwd)


def _mla_qk(scale, qm, kv, misc, cos_t, sin_t):
    krope = _rope(misc, cos_t, sin_t)
    qs, ks = [], []
    for h in range(N_HEADS):
        base = 2 * HEAD_DIM * h
        qs += [qm[:, base:base + HEAD_DIM], _rope(qm[:, base + HEAD_DIM:base + 2 * HEAD_DIM], cos_t, sin_t)]
        ks += [kv[:, HEAD_DIM * h:HEAD_DIM * (h + 1)], krope]
    return jnp.concatenate(qs, axis=1) * scale, jnp.concatenate(ks, axis=1), kv[:, N_HEADS * HEAD_DIM:]


def _swiglu(gu):
    f = gu.shape[1] // 2
    return _silu(gu[:, :f]) * gu[:, f:]


def _ple_out(x2, pg, pe):
    return x2 + jax.nn.sigmoid(pg) * pe


CONV_W = 4
HALO = 8


def _conv_fwd(h, conv_w, width, *, name, tm=ROW_TILE, sub=32):
    t = h.shape[0]
    tm = min(tm, t)
    nb = tm // HALO

    def body(x_ref, halo_ref, w_ref, u_ref, buf):
        i = pl.program_id(0)
        buf[pl.ds(0, HALO), :] = jnp.where(i > 0, halo_ref[...], 0.0)
        buf[pl.ds(HALO, tm), :] = x_ref[...]
        w = w_ref[...]
        for r0 in range(0, tm, sub):
            acc = jnp.zeros((sub, width), F32)
            for j in range(CONV_W):
                acc = acc + w[j:j + 1, :] * buf[pl.ds(HALO + r0 - (CONV_W - 1) + j, sub), :]
            u_ref[pl.ds(r0, sub), :] = acc

    return pl.pallas_call(
        body, name=name, grid=(t // tm,),
        in_specs=[pl.BlockSpec((tm, width), lambda i: (i, 0)),
                  pl.BlockSpec((HALO, width), lambda i: (jnp.maximum(i * nb - 1, 0), 0)),
                  pl.BlockSpec(conv_w.shape, lambda i: (0, 0))],
        out_specs=pl.BlockSpec((tm, width), lambda i: (i, 0)),
        out_shape=jax.ShapeDtypeStruct((t, width), F32),
        scratch_shapes=[pltpu.VMEM((tm + HALO, width), F32)],
        compiler_params=_params(("arbitrary",)),
    )(h, h, conv_w)


def _conv_bwd(du, h, conv_w, width, *, name, tm=ROW_TILE, sub=32):
    t = h.shape[0]
    tm = min(tm, t)
    nb = tm // HALO
    n_tiles = t // tm

    def body(du_ref, du_halo, x_ref, x_halo, w_ref, dx_ref, dw_ref, dbuf, xbuf):
        i = pl.program_id(0)

        @pl.when(i == 0)
        def _():
            dw_ref[...] = jnp.zeros_like(dw_ref)

        dbuf[pl.ds(0, tm), :] = du_ref[...]
        dbuf[pl.ds(tm, HALO), :] = jnp.where(i < n_tiles - 1, du_halo[...], 0.0)
        xbuf[pl.ds(0, HALO), :] = jnp.where(i > 0, x_halo[...], 0.0)
        xbuf[pl.ds(HALO, tm), :] = x_ref[...]
        w = w_ref[...]
        dws = [jnp.zeros((1, width), F32) for _ in range(CONV_W)]
        for r0 in range(0, tm, sub):
            acc = jnp.zeros((sub, width), F32)
            d_here = dbuf[pl.ds(r0, sub), :]
            for j in range(CONV_W):
                acc = acc + w[j:j + 1, :] * dbuf[pl.ds(r0 + (CONV_W - 1) - j, sub), :]
                xs = xbuf[pl.ds(HALO + r0 - (CONV_W - 1) + j, sub), :]
                dws[j] = dws[j] + jnp.sum(d_here * xs, axis=0, keepdims=True)
            dx_ref[pl.ds(r0, sub), :] = acc.astype(dx_ref.dtype)
        for j in range(CONV_W):
            dw_ref[pl.ds(j, 1), :] += dws[j]

    return pl.pallas_call(
        body, name=name, grid=(n_tiles,),
        in_specs=[pl.BlockSpec((tm, width), lambda i: (i, 0)),
                  pl.BlockSpec((HALO, width), lambda i: (jnp.minimum((i + 1) * nb, t // HALO - 1), 0)),
                  pl.BlockSpec((tm, width), lambda i: (i, 0)),
                  pl.BlockSpec((HALO, width), lambda i: (jnp.maximum(i * nb - 1, 0), 0)),
                  pl.BlockSpec(conv_w.shape, lambda i: (0, 0))],
        out_specs=[pl.BlockSpec((tm, width), lambda i: (i, 0)),
                   pl.BlockSpec((HALO, width), lambda i: (0, 0))],
        out_shape=[jax.ShapeDtypeStruct((t, width), BF16), jax.ShapeDtypeStruct((HALO, width), F32)],
        scratch_shapes=[pltpu.VMEM((tm + HALO, width), F32), pltpu.VMEM((tm + HALO, width), F32)],
        compiler_params=_params(("arbitrary",)),
    )(du, du, h, h, conv_w)


@jax.custom_vjp
def _inv_unit_lower(low):
    n = low.shape[0]
    eye = (lax.broadcasted_iota(jnp.int32, (n, n), 0) == lax.broadcasted_iota(jnp.int32, (n, n), 1)).astype(F32)
    x = eye - low
    p = low
    span = 2
    while span < n:
        p = _hnn(p, p)
        x = x + _hnn(x, p)
        span *= 2
    return x


def _inv_fwd(low):
    x = _inv_unit_lower(low)
    return x, x


def _inv_bwd(x, g):
    return (-_htn(x, _hnt(g, x)),)


_inv_unit_lower.defvjp(_inv_fwd, _inv_bwd)


def _gdn_chunk(state, q, k, v, gb):
    c = q.shape[0]
    gbt = gb.T
    row = lax.broadcasted_iota(jnp.int32, (c, c), 0)
    col = lax.broadcasted_iota(jnp.int32, (c, c), 1)
    tri_incl = row >= col
    tri_strict = row > col
    lane = _lane(gb.shape)
    sub = lax.broadcasted_iota(jnp.int32, gbt.shape, 0)
    last = lax.broadcasted_iota(jnp.int32, (c, 1), 0) == c - 1
    outs, states = [], []
    for h in range(N_HEADS):
        g_col = jnp.sum(jnp.where(lane == MISC_A0 + h, gb, 0.0), axis=1, keepdims=True)
        b_col = jnp.sum(jnp.where(lane == MISC_BETA0 + h, gb, 0.0), axis=1, keepdims=True)
        g_row = jnp.sum(jnp.where(sub == MISC_A0 + h, gbt, 0.0), axis=0, keepdims=True)
        gc_col = jnp.sum(jnp.where(tri_incl, g_row, 0.0), axis=1, keepdims=True)
        gc_row = jnp.sum(jnp.where(row <= col, g_col, 0.0), axis=0, keepdims=True)
        decay = jnp.where(tri_incl, jnp.exp(jnp.where(tri_incl, gc_col - gc_row, 0.0)), 0.0)
        g_last = jnp.sum(jnp.where(last, gc_col, 0.0), axis=0, keepdims=True)
        hs = slice(h * HEAD_DIM, (h + 1) * HEAD_DIM)
        qh, kh, vh, sh = q[:, hs], k[:, hs], v[:, hs], state[hs, :]
        kb = kh * b_col
        low = jnp.where(tri_strict, _nt(kb, kh) * decay, 0.0)
        tinv = _inv_unit_lower(low)
        eg = jnp.exp(gc_col)
        sol = _hnn(tinv, jnp.concatenate([vh * b_col, kb * eg], axis=1))
        u, w = sol[:, :HEAD_DIM], sol[:, HEAD_DIM:]
        attn = jnp.where(tri_incl, _nt(qh, kh) * decay, 0.0)
        v_new = u - _nn(w, sh)
        outs.append(_nn(qh * eg, sh) + _nn(attn, v_new))
        states.append(sh * jnp.exp(g_last) + _tn(kh * jnp.exp(g_last - gc_col), v_new))
    return jnp.concatenate(outs, axis=1), jnp.concatenate(states, axis=0)


def _gdn_fwd(q, k, v, gb, *, name):
    t, w = q.shape
    n = t // CHUNK

    def body(q_ref, k_ref, v_ref, gb_ref, o_ref, sall_ref, s_scr):
        @pl.when(pl.program_id(0) == 0)
        def _():
            s_scr[...] = jnp.zeros_like(s_scr)

        s = s_scr[...]
        sall_ref[...] = s
        o, s_new = _gdn_chunk(s, q_ref[...], k_ref[...], v_ref[...], gb_ref[...])
        o_ref[...] = o
        s_scr[...] = s_new

    row = pl.BlockSpec((CHUNK, w), lambda i: (i, 0))
    return pl.pallas_call(
        body, name=name, grid=(n,),
        in_specs=[row, row, row, pl.BlockSpec((CHUNK, LANES), lambda i: (i, 0))],
        out_specs=[row, pl.BlockSpec((None, w, HEAD_DIM), lambda i: (i, 0, 0))],
        out_shape=[jax.ShapeDtypeStruct((t, w), F32), jax.ShapeDtypeStruct((n, w, HEAD_DIM), F32)],
        scratch_shapes=[pltpu.VMEM((w, HEAD_DIM), F32)],
        compiler_params=_params(("arbitrary",)),
    )(q, k, v, gb)


def _gdn_bwd(q, k, v, gb, s_all, do, *, name):
    t, w = q.shape
    n = t // CHUNK

    def body(q_ref, k_ref, v_ref, gb_ref, sall_ref, do_ref, dq_ref, dk_ref, dv_ref, dgb_ref, ds_scr):
        @pl.when(pl.program_id(0) == 0)
        def _():
            ds_scr[...] = jnp.zeros_like(ds_scr)

        _, pull = jax.vjp(_gdn_chunk, sall_ref[...], q_ref[...], k_ref[...], v_ref[...], gb_ref[...])
        ds, dq, dk, dv, dgb = pull((do_ref[...], ds_scr[...]))
        dq_ref[...] = dq
        dk_ref[...] = dk
        dv_ref[...] = dv
        dgb_ref[...] = dgb
        ds_scr[...] = ds

    row = pl.BlockSpec((CHUNK, w), lambda i: (n - 1 - i, 0))
    gate = pl.BlockSpec((CHUNK, LANES), lambda i: (n - 1 - i, 0))
    return pl.pallas_call(
        body, name=name, grid=(n,),
        in_specs=[row, row, row, gate, pl.BlockSpec((None, w, HEAD_DIM), lambda i: (n - 1 - i, 0, 0)), row],
        out_specs=[row, row, row, gate],
        out_shape=[jax.ShapeDtypeStruct((t, w), F32)] * 3 + [jax.ShapeDtypeStruct((t, LANES), F32)],
        scratch_shapes=[pltpu.VMEM((w, HEAD_DIM), F32)],
        compiler_params=_params(("arbitrary",)),
    )(q, k, v, gb, s_all, do)


QK_DIM = 2 * HEAD_DIM
ATT_TILE = 512
NEG = -1e30


def _chunk_mask(tq, tk, key_major):
    r = lax.broadcasted_iota(jnp.int32, (tq, tk), 0) // CHUNK
    c = lax.broadcasted_iota(jnp.int32, (tq, tk), 1) // CHUNK
    return (r <= c) if key_major else (c <= r)


def _dot_nt(a, b):
    return lax.dot_general(a, b, NT, preferred_element_type=F32)


def _dot_nn(a, b):
    return lax.dot_general(a, b, NN, preferred_element_type=F32)


def _attn_fwd(q, k, v, *, name):
    t = q.shape[0]
    tq = min(ATT_TILE, t)
    nq = t // tq

    def body(q_ref, k_ref, v_ref, o_ref, lse_ref, m_scr, l_scr, acc_scr):
        qi = pl.program_id(1)
        m_scr[...] = jnp.full_like(m_scr, NEG)
        l_scr[...] = jnp.zeros_like(l_scr)
        acc_scr[...] = jnp.zeros_like(acc_scr)
        qv = q_ref[...]

        def step(kj, masked):
            rows = pl.ds(pl.multiple_of(kj * tq, tq), tq)
            s = _dot_nt(qv, k_ref[rows, :])
            if masked:
                s = jnp.where(_chunk_mask(tq, tq, False), s, NEG)
            m_old = m_scr[...]
            m_new = jnp.maximum(m_old, jnp.max(s, axis=1, keepdims=True))
            p = jnp.exp(s - m_new)
            alpha = jnp.exp(m_old - m_new)
            l_scr[...] = alpha * l_scr[...] + jnp.sum(p, axis=1, keepdims=True)
            acc_scr[...] = alpha * acc_scr[...] + _dot_nn(p.astype(BF16), v_ref[rows, :])
            m_scr[...] = m_new

        def loop_body(kj, carry):
            step(kj, False)
            return carry

        lax.fori_loop(0, qi, loop_body, 0)
        step(qi, True)
        o_ref[...] = (acc_scr[...] / l_scr[...]).astype(o_ref.dtype)
        lse_ref[...] = m_scr[...] + jnp.log(l_scr[...])

    return pl.pallas_call(
        body, name=name, grid=(N_HEADS, nq),
        in_specs=[pl.BlockSpec((tq, QK_DIM), lambda h, i: (i, h)),
                  pl.BlockSpec((t, QK_DIM), lambda h, i: (0, h)),
                  pl.BlockSpec((t, HEAD_DIM), lambda h, i: (0, h))],
        out_specs=[pl.BlockSpec((tq, HEAD_DIM), lambda h, i: (i, h)),
                   pl.BlockSpec((None, tq, 1), lambda h, i: (h, i, 0))],
        out_shape=[jax.ShapeDtypeStruct((t, N_HEADS * HEAD_DIM), BF16),
                   jax.ShapeDtypeStruct((N_HEADS, t, 1), F32)],
        scratch_shapes=[pltpu.VMEM((tq, 1), F32), pltpu.VMEM((tq, 1), F32), pltpu.VMEM((tq, HEAD_DIM), F32)],
        compiler_params=_params(("arbitrary", "arbitrary")),
    )(q, k, v)


def _attn_bwd_dq(q, k, v, o, lse, dom, do_col0, *, name):
    t = q.shape[0]
    tq = min(ATT_TILE, t)
    nq = t // tq

    def body(q_ref, k_ref, v_ref, o_ref, lse_ref, do_ref, dq_ref, delta_ref, acc_scr):
        qi = pl.program_id(1)
        acc_scr[...] = jnp.zeros_like(acc_scr)
        qv = q_ref[...]
        do = do_ref[...]
        delta = jnp.sum(do * o_ref[...].astype(F32), axis=1, keepdims=True)
        delta_ref[...] = delta
        do_b = do.astype(BF16)
        lse_v = lse_ref[...]

        def step(kj, masked):
            rows = pl.ds(pl.multiple_of(kj * tq, tq), tq)
            kv_ = k_ref[rows, :]
            p = jnp.exp(_dot_nt(qv, kv_) - lse_v)
            if masked:
                p = jnp.where(_chunk_mask(tq, tq, False), p, 0.0)
            ds = p * (_dot_nt(do_b, v_ref[rows, :]) - delta)
            acc_scr[...] += _dot_nn(ds.astype(BF16), kv_)

        def loop_body(kj, carry):
            step(kj, False)
            return carry

        lax.fori_loop(0, qi, loop_body, 0)
        step(qi, True)
        dq_ref[...] = acc_scr[...].astype(dq_ref.dtype)

    return pl.pallas_call(
        body, name=name, grid=(N_HEADS, nq),
        in_specs=[pl.BlockSpec((tq, QK_DIM), lambda h, i: (i, h)),
                  pl.BlockSpec((t, QK_DIM), lambda h, i: (0, h)),
                  pl.BlockSpec((t, HEAD_DIM), lambda h, i: (0, h)),
                  pl.BlockSpec((tq, HEAD_DIM), lambda h, i: (i, h)),
                  pl.BlockSpec((None, tq, 1), lambda h, i: (h, i, 0)),
                  pl.BlockSpec((tq, HEAD_DIM), lambda h, i: (i, do_col0 + h))],
        out_specs=[pl.BlockSpec((tq, QK_DIM), lambda h, i: (i, h)),
                   pl.BlockSpec((None, tq, 1), lambda h, i: (h, i, 0))],
        out_shape=[jax.ShapeDtypeStruct((t, N_HEADS * QK_DIM), BF16),
                   jax.ShapeDtypeStruct((N_HEADS, t, 1), F32)],
        scratch_shapes=[pltpu.VMEM((tq, QK_DIM), F32)],
        compiler_params=_params(("arbitrary", "arbitrary")),
    )(q, k, v, o, lse, dom)


def _attn_bwd_dkv(q, k, v, lse_row, delta_row, dom, do_col0, *, name):
    t = q.shape[0]
    tk = min(ATT_TILE, t)
    nk = t // tk

    def body(q_ref, k_ref, v_ref, lse_ref, delta_ref, do_ref, dk_ref, dv_ref, dk_scr, dv_scr):
        kj = pl.program_id(1)
        dk_scr[...] = jnp.zeros_like(dk_scr)
        dv_scr[...] = jnp.zeros_like(dv_scr)
        kv_ = k_ref[...]
        vv = v_ref[...]

        def step(qi, masked):
            rows = pl.ds(pl.multiple_of(qi * tk, tk), tk)
            qv = q_ref[rows, :]
            do_b = do_ref[rows, :].astype(BF16)
            p = jnp.exp(_dot_nt(kv_, qv) - lse_ref[qi])
            if masked:
                p = jnp.where(_chunk_mask(tk, tk, True), p, 0.0)
            dv_scr[...] += _dot_nn(p.astype(BF16), do_b)
            ds = p * (_dot_nt(vv, do_b) - delta_ref[qi])
            dk_scr[...] += _dot_nn(ds.astype(BF16), qv)

        step(kj, True)

        def loop_body(qi, carry):
            step(qi, False)
            return carry

        lax.fori_loop(kj + 1, nk, loop_body, 0)
        dk_ref[...] = dk_scr[...].astype(dk_ref.dtype)
        dv_ref[...] = dv_scr[...].astype(dv_ref.dtype)

    stat = pl.BlockSpec((None, nk, 1, tk), lambda h, j: (h, 0, 0, 0))
    return pl.pallas_call(
        body, name=name, grid=(N_HEADS, nk),
        in_specs=[pl.BlockSpec((t, QK_DIM), lambda h, j: (0, h)),
                  pl.BlockSpec((tk, QK_DIM), lambda h, j: (j, h)),
                  pl.BlockSpec((tk, HEAD_DIM), lambda h, j: (j, h)),
                  stat, stat,
                  pl.BlockSpec((t, HEAD_DIM), lambda h, j: (0, do_col0 + h))],
        out_specs=[pl.BlockSpec((tk, QK_DIM), lambda h, j: (j, h)),
                   pl.BlockSpec((tk, HEAD_DIM), lambda h, j: (j, h))],
        out_shape=[jax.ShapeDtypeStruct((t, N_HEADS * QK_DIM), BF16),
                   jax.ShapeDtypeStruct((t, N_HEADS * HEAD_DIM), BF16)],
        scratch_shapes=[pltpu.VMEM((tk, QK_DIM), F32), pltpu.VMEM((tk, HEAD_DIM), F32)],
        compiler_params=_params(("arbitrary", "arbitrary")),
    )(q, k, v, lse_row, delta_row, dom)


def _rope_tables(pos_col, inv_freq_row, *, name):
    t = pos_col.shape[0]
    tm = min(ROW_TILE, t)

    def body(p_ref, f_ref, c_ref, s_ref):
        ang = p_ref[...].astype(F32) * f_ref[...]
        lane = _lane(ang.shape)
        c_ref[...] = jnp.where(lane < ROPE_DIM, jnp.cos(ang), 0.0)
        sn = jnp.sin(ang)
        s_ref[...] = jnp.where(lane < ROPE_DIM // 2, -sn, jnp.where(lane < ROPE_DIM, sn, 0.0))

    out = pl.BlockSpec((tm, LANES), lambda i: (i, 0))
    return pl.pallas_call(
        body, name=name, grid=(t // tm,),
        in_specs=[pl.BlockSpec((tm, 1), lambda i: (i, 0)), pl.BlockSpec((1, LANES), lambda i: (0, 0))],
        out_specs=[out, out], out_shape=[jax.ShapeDtypeStruct((t, LANES), F32)] * 2,
        compiler_params=_params(("arbitrary",)),
    )(pos_col, inv_freq_row)


def _loss_head(y, target):
    width = y.shape[1]

    def fn(yv, tv):
        e = yv - tv
        part = 0.5 * jnp.sum(jnp.mean(e * e, axis=1, keepdims=True), axis=0, keepdims=True)
        return e * (1.0 / width), jnp.broadcast_to(part, (HALO, LANES))

    return _rowwise(fn, [(y, width, 0), (target, width, 0)], [], [(width, F32)], [(HALO, LANES)], name="loss_head")


def _adamw(w, g, m, v, *, name):
    shape = w.shape
    w2, g2, m2, v2 = (a.reshape(-1, shape[-1]) for a in (w, g, m, v))
    rows, width = w2.shape
    tr = _divisor_tile(rows, max(8, (1 << 19) // max(width, 1)), 8)
    bc1 = 1.0 - ADAM_B1 ** ADAM_STEP
    bc2 = 1.0 - ADAM_B2 ** ADAM_STEP

    def body(w_ref, g_ref, m_ref, v_ref, d_ref, mo_ref, vo_ref):
        gv = g_ref[...]
        mn = ADAM_B1 * m_ref[...] + (1.0 - ADAM_B1) * gv
        vn = ADAM_B2 * v_ref[...] + (1.0 - ADAM_B2) * (gv * gv)
        d_ref[...] = -ADAM_LR * ((mn / bc1) / (jnp.sqrt(vn / bc2) + ADAM_EPS) + ADAM_WD * w_ref[...])
        mo_ref[...] = mn
        vo_ref[...] = vn

    spec = pl.BlockSpec((tr, width), lambda i: (i, 0))
    outs = pl.pallas_call(
        body, name=name, grid=(rows // tr,), in_specs=[spec] * 4, out_specs=[spec] * 3,
        out_shape=[jax.ShapeDtypeStruct((rows, width), F32)] * 3,
        compiler_params=_params(("arbitrary",)),
    )(w2, g2, m2, v2)
    return tuple(o.reshape(shape) for o in outs)


HBM_SPEC = pl.BlockSpec(memory_space=pltpu.HBM)


def _position():
    return lax.axis_index("x"), lax.axis_index("y"), lax.axis_index("c")


def _other_chips(x, y):
    return [(1 - x, y), (x, 1 - y), (1 - x, 1 - y)]


def _gather_chips(shard, *, name):
    r, w = shard.shape
    half = r // 2
    assert r % 32 == 0

    def body(x_ref, out_ref, send_sems, recv_sems, local_sem):
        x, y, c = _position()
        sibling = (x, y, 1 - c)
        chips = _other_chips(x, y)

        def blk(cx, cy, cc):
            return out_ref.at[2 * cx + cy, pl.ds(cc * half, half), :]

        def copy(kk, src, dst, to):
            return pltpu.make_async_remote_copy(src_ref=src, dst_ref=dst, send_sem=send_sems.at[kk],
                                                recv_sem=recv_sems.at[kk], device_id=to, device_id_type=MESH)

        mine = pltpu.make_async_copy(x_ref, out_ref.at[2 * x + y], local_sem)
        mine.start()
        first = [copy(j, x_ref.at[pl.ds(c * half, half), :], blk(x, y, c), (cx, cy, c))
                 for j, (cx, cy) in enumerate(chips)]
        for cp in first:
            cp.start()
        passed = [copy(3 + j, blk(cx, cy, c), blk(cx, cy, c), sibling) for j, (cx, cy) in enumerate(chips)]
        for j, (cx, cy) in enumerate(chips):
            copy(j, blk(cx, cy, c), blk(cx, cy, c), (x, y, c)).wait_recv()
            passed[j].start()
        for j, (cx, cy) in enumerate(chips):
            copy(3 + j, blk(cx, cy, 1 - c), blk(cx, cy, 1 - c), (x, y, c)).wait_recv()
        for cp in first + passed:
            cp.wait_send()
        mine.wait()

    return pl.pallas_call(
        body, name=name, in_specs=[HBM_SPEC], out_specs=HBM_SPEC,
        out_shape=jax.ShapeDtypeStruct((4, r, w), shard.dtype),
        scratch_shapes=[pltpu.SemaphoreType.DMA((6,)), pltpu.SemaphoreType.DMA((6,)), pltpu.SemaphoreType.DMA],
        compiler_params=pltpu.CompilerParams(has_side_effects=True),
    )(shard)


def _sibling_take_other_half(g, *, name):
    n_own, _, r, w = g.shape

    def body(g_ref, out_ref, send_sems, recv_sems):
        x, y, c = _position()
        copies = [pltpu.make_async_remote_copy(src_ref=g_ref.at[o, 1 - c], dst_ref=out_ref.at[o],
                                               send_sem=send_sems.at[o], recv_sem=recv_sems.at[o],
                                               device_id=(x, y, 1 - c), device_id_type=MESH)
                  for o in range(n_own)]
        for cp in copies:
            cp.start()
        for cp in copies:
            cp.wait()

    return pl.pallas_call(
        body, name=name, in_specs=[HBM_SPEC], out_specs=HBM_SPEC,
        out_shape=jax.ShapeDtypeStruct((n_own, r, w), g.dtype),
        scratch_shapes=[pltpu.SemaphoreType.DMA((n_own,)), pltpu.SemaphoreType.DMA((n_own,))],
        compiler_params=pltpu.CompilerParams(has_side_effects=True),
    )(g)


def _chips_exchange(p, *, name):
    _, r, w = p.shape

    def body(p_ref, out_ref, send_sems, recv_sems):
        x, y, c = _position()
        copies = [pltpu.make_async_remote_copy(src_ref=p_ref.at[2 * cx + cy], dst_ref=out_ref.at[j],
                                               send_sem=send_sems.at[j], recv_sem=recv_sems.at[j],
                                               device_id=(cx, cy, c), device_id_type=MESH)
                  for j, (cx, cy) in enumerate(_other_chips(x, y))]
        for cp in copies:
            cp.start()
        for cp in copies:
            cp.wait()

    return pl.pallas_call(
        body, name=name, in_specs=[HBM_SPEC], out_specs=HBM_SPEC,
        out_shape=jax.ShapeDtypeStruct((3, r, w), p.dtype),
        scratch_shapes=[pltpu.SemaphoreType.DMA((3,)), pltpu.SemaphoreType.DMA((3,))],
        compiler_params=pltpu.CompilerParams(has_side_effects=True),
    )(p)


def _sibling_join_halves(mine, *, name):
    r, w = mine.shape

    def body(m_ref, out_ref, send_sem, recv_sem, local_sem):
        x, y, c = _position()
        own = pltpu.make_async_copy(m_ref, out_ref.at[c], local_sem)
        own.start()
        cp = pltpu.make_async_remote_copy(src_ref=m_ref, dst_ref=out_ref.at[c], send_sem=send_sem,
                                          recv_sem=recv_sem, device_id=(x, y, 1 - c), device_id_type=MESH)
        cp.start()
        pltpu.make_async_remote_copy(src_ref=m_ref, dst_ref=out_ref.at[1 - c], send_sem=send_sem,
                                     recv_sem=recv_sem, device_id=(x, y, 1 - c), device_id_type=MESH).wait_recv()
        cp.wait_send()
        own.wait()

    return pl.pallas_call(
        body, name=name, in_specs=[HBM_SPEC], out_specs=HBM_SPEC,
        out_shape=jax.ShapeDtypeStruct((2, r, w), mine.dtype),
        scratch_shapes=[pltpu.SemaphoreType.DMA, pltpu.SemaphoreType.DMA, pltpu.SemaphoreType.DMA],
        compiler_params=pltpu.CompilerParams(has_side_effects=True),
    )(mine)


def _add_own_half(g, got, c_idx, *, name):
    n_own, _, r, w = g.shape
    tr = _divisor_tile(r, 512, 8)

    def body(c_ref, g_ref, a_ref, o_ref):
        o_ref[...] = g_ref[...] + a_ref[...]

    return pl.pallas_call(
        body, name=name,
        grid_spec=pltpu.PrefetchScalarGridSpec(
            num_scalar_prefetch=1, grid=(n_own, r // tr),
            in_specs=[pl.BlockSpec((None, None, tr, w), lambda o, i, c_ref: (o, c_ref[0], i, 0)),
                      pl.BlockSpec((None, tr, w), lambda o, i, c_ref: (o, i, 0))],
            out_specs=pl.BlockSpec((None, tr, w), lambda o, i, c_ref: (o, i, 0))),
        out_shape=jax.ShapeDtypeStruct((n_own, r, w), F32),
        compiler_params=_params(("arbitrary", "arbitrary")),
    )(c_idx, g, got)


def _sum_chips(p, got, chip_idx, *, name):
    _, r, w = p.shape
    tr = _divisor_tile(r, 512, 8)

    def body(k_ref, p_ref, fx_ref, fy_ref, fxy_ref, o_ref):
        o_ref[...] = (p_ref[...] + fy_ref[...]) + (fx_ref[...] + fxy_ref[...])

    def rel(j):
        return pl.BlockSpec((None, tr, w), functools.partial(lambda i, k_ref, j: (j, i, 0), j=j))

    return pl.pallas_call(
        body, name=name,
        grid_spec=pltpu.PrefetchScalarGridSpec(
            num_scalar_prefetch=1, grid=(r // tr,),
            in_specs=[pl.BlockSpec((None, tr, w), lambda i, k_ref: (k_ref[0], i, 0)), rel(0), rel(1), rel(2)],
            out_specs=pl.BlockSpec((tr, w), lambda i, k_ref: (i, 0))),
        out_shape=jax.ShapeDtypeStruct((r, w), F32),
        compiler_params=_params(("arbitrary",)),
    )(chip_idx, p, got, got, got)


MATRICES = ("w_in", "w_uq", "w_ukv", "w_out", "w_gate_up", "w_down", "w_ple", "w_ple_gate", "conv_w")
VECTORS = ("a_log", "dt_bias", "gdn_norm_g", "q_norm_g", "kv_norm_g", "ln1_g", "ln1_b", "ln2_g", "ln2_b")
WEIGHTS = ("w_in", "conv_w", "a_log", "dt_bias", "gdn_norm_g", "q_norm_g", "w_uq", "kv_norm_g", "w_ukv", "w_out",
           "ln1_g", "ln1_b", "w_gate_up", "w_down", "ln2_g", "ln2_b", "w_ple", "w_ple_gate")
ROW_SHARDED = ("w_out", "w_down", "w_ple_gate")
N_CHIPS = 4


class _Layout:
    def __init__(self, shard_shapes):
        self.shapes = {n: tuple(shard_shapes[n]) for n in MATRICES + VECTORS}
        self.offsets = {}
        off = 0
        for n in MATRICES + VECTORS:
            self.offsets[n] = off
            size = 1
            for d in self.shapes[n]:
                size *= d
            off += size
        unit = 16 * PACK_W
        self.size = -(-off // unit) * unit
        self.rows = self.size // PACK_W
        self.used = off

    def pack(self, tensors, depth, dtype):
        parts = [tensors[n].reshape(depth, -1).astype(dtype) for n in MATRICES + VECTORS]
        parts.append(jnp.zeros((depth, self.size - self.used), dtype))
        return jnp.concatenate(parts, axis=1).reshape(depth, self.rows, PACK_W)

    def unpack(self, flat, name):
        lead = flat.shape[:-2]
        v = flat.reshape(lead + (self.size,))
        size = 1
        for d in self.shapes[name]:
            size *= d
        return v[..., self.offsets[name]:self.offsets[name] + size].reshape(lead + self.shapes[name])


def _shard_axis(name):
    return 0 if name in ROW_SHARDED else 1


class _Dims:
    def __init__(self, d_model, in_width, q_lora, kv_lora, d_ff2, ple_dim):
        self.d = d_model
        self.hw = N_HEADS * HEAD_DIM
        self.in_width = in_width
        self.q_lora, self.kv_lora = q_lora, kv_lora
        self.ff2 = d_ff2
        self.ple = ple_dim
        self.c_kv0 = 4 * self.hw
        self.c_q0 = self.c_kv0 + kv_lora
        self.misc0 = self.c_q0 + q_lora
        self.h_width = self.misc0 + LANES
        assert self.c_kv0 % kv_lora == 0 and self.c_q0 % q_lora == 0 and self.misc0 % LANES == 0
        self.g_beta = 4 * self.hw
        self.g_a = self.g_beta + N_HEADS
        self.g_cq = self.g_a + N_HEADS
        self.g_ckv = self.g_cq + q_lora
        self.g_kr = self.g_ckv + kv_lora
        assert self.g_kr + ROPE_DIM == in_width

    def w_in_local(self, w):
        pad = jnp.zeros((w.shape[0], self.h_width - self.in_width), w.dtype)
        return jnp.concatenate([w[:, :self.g_beta], w[:, self.g_ckv:self.g_kr], w[:, self.g_cq:self.g_ckv],
                                w[:, self.g_kr:], w[:, self.g_beta:self.g_cq], pad], axis=1)

    def w_in_global(self, d):
        m = self.misc0
        return jnp.concatenate([d[:, :self.c_kv0], d[:, m + MISC_BETA0:m + MISC_A0 + N_HEADS],
                                d[:, self.c_q0:self.misc0], d[:, self.c_kv0:self.c_q0], d[:, m:m + ROPE_DIM]], axis=1)

    def w_uq_local(self, w):
        r = w.reshape(w.shape[0], N_HEADS, HEAD_DIM + ROPE_DIM)
        r = jnp.pad(r, ((0, 0), (0, 0), (0, QK_DIM - HEAD_DIM - ROPE_DIM)))
        return r.reshape(w.shape[0], N_HEADS * QK_DIM)

    def w_uq_global(self, d):
        r = d.reshape(d.shape[0], N_HEADS, QK_DIM)[:, :, :HEAD_DIM + ROPE_DIM]
        return r.reshape(d.shape[0], N_HEADS * (HEAD_DIM + ROPE_DIM))

    def w_ukv_local(self, w):
        return w.reshape(w.shape[0], N_HEADS, 2, HEAD_DIM).transpose(0, 2, 1, 3).reshape(w.shape[0], -1)

    def w_ukv_global(self, d):
        return d.reshape(d.shape[0], 2, N_HEADS, HEAD_DIM).transpose(0, 2, 1, 3).reshape(d.shape[0], -1)


def _lane_row(vec, lane0):
    pad = LANES - lane0 - vec.shape[0]
    return jnp.concatenate([jnp.zeros((lane0,), F32), vec.astype(F32), jnp.zeros((pad,), F32)])[None, :]


def _layer_fwd(dm, alpha, x, xb, p_i, cos_t, sin_t, wl, tag):
    d, hw = dm.d, dm.hw
    nm = lambda s: f"{s}_{tag}"
    h = _matmul(xb, wl["w_in"], dims="nn", name=nm("f_in"), tn=_divisor_tile(dm.h_width, 1408, LANES))
    misc_cb = dm.misc0 // LANES

    u = _conv_fwd(h, wl["conv_w"], 3 * hw, name=nm("f_conv"))
    qn, kn, vg, gb = _rowwise(_gdn_act, [(u, 3 * hw, 0), (h, LANES, misc_cb)], [wl["alog_row"], wl["dtb_row"]],
                              [(hw, F32), (hw, F32), (hw, F32), (LANES, F32)], name=nm("f_gdn_act"))
    o_gdn, s_all = _gdn_fwd(qn, kn, vg, gb, name=nm("f_gdn"))
    (og,) = _rowwise(lambda o, z, g: (_gdn_out(o, z, g),), [(o_gdn, hw, 0), (h, hw, 3)], [wl["gn_row"]],
                     [(hw, BF16)], name=nm("f_gdn_out"))

    cqn, ckvn = _rowwise(_mla_norm, [(h, dm.kv_lora, dm.c_kv0 // dm.kv_lora), (h, dm.q_lora, dm.c_q0 // dm.q_lora)],
                         [wl["kvg_row"], wl["qg_row"]], [(dm.q_lora, BF16), (dm.kv_lora, BF16)], name=nm("f_mla_norm"))
    qm = _matmul(cqn, wl["w_uq"], dims="nn", name=nm("f_uq"))
    kvm = _matmul(ckvn, wl["w_ukv"], dims="nn", name=nm("f_ukv"))
    scale = (HEAD_DIM + ROPE_DIM) ** -0.5
    qk_fn = functools.partial(_mla_qk, scale)
    qa, ka, va = _rowwise(qk_fn, [(qm, N_HEADS * QK_DIM, 0), (kvm, 2 * hw, 0), (h, LANES, misc_cb),
                                  (cos_t, LANES, 0), (sin_t, LANES, 0)], [],
                          [(N_HEADS * QK_DIM, BF16), (N_HEADS * QK_DIM, BF16), (hw, BF16)], name=nm("f_mla_qk"))
    o_mla, lse = _attn_fwd(qa, ka, va, name=nm("f_attn"))

    om = jnp.concatenate([og, o_mla], axis=1)
    mix = _matmul(om, wl["w_out"], dims="nn", name=nm("f_out"))
    ln1 = lambda xv, yv, g, b: (_layer_norm(alpha * xv + yv, g, b),) * 2
    x1, x1b = _rowwise(ln1, [(x, d, 0), (mix, d, 0)], [wl["ln1_g"], wl["ln1_b"]], [(d, F32), (d, BF16)], name=nm("f_ln1"))

    gu = _matmul(x1b, wl["w_gate_up"], dims="nn", name=nm("f_gate_up"))
    (act,) = _rowwise(lambda g_: (_swiglu(g_),), [(gu, dm.ff2, 0)], [], [(dm.ff2 // 2, BF16)], name=nm("f_swiglu"))
    dn = _matmul(act, wl["w_down"], dims="nn", name=nm("f_down"))
    x2, x2b = _rowwise(ln1, [(x1, d, 0), (dn, d, 0)], [wl["ln2_g"], wl["ln2_b"]], [(d, F32), (d, BF16)], name=nm("f_ln2"))

    pg = _matmul(x2b, wl["w_ple_gate"], dims="nn", name=nm("f_ple_gate"))
    pe = _matmul(p_i, wl["w_ple"], dims="nn", name=nm("f_ple"))
    out, outb = _rowwise(lambda a, b, c_: (_ple_out(a, b, c_),) * 2, [(x2, d, 0), (pg, d, 0), (pe, d, 0)], [],
                         [(d, F32), (d, BF16)], name=nm("f_ple_out"))
    saved = dict(x=x, xb=xb, p_i=p_i, h=h, u=u, qn=qn, kn=kn, vg=vg, gb=gb, s_all=s_all, o_gdn=o_gdn, cqn=cqn, ckvn=ckvn,
                 qm=qm, kvm=kvm, qa=qa, ka=ka, va=va, o_mla=o_mla, lse=lse, om=om, mix=mix, x1=x1, x1b=x1b, gu=gu,
                 act=act, dn=dn, x2=x2, x2b=x2b, pg=pg, pe=pe)
    return out, outb, saved


def _layer_bwd(dm, alpha, dout, sv, cos_t, sin_t, wl, tag):
    d, hw = dm.d, dm.hw
    t = dout.shape[0]
    nm = lambda s: f"{s}_{tag}"
    gr = {}
    misc_cb = dm.misc0 // LANES
    big = dict(tm=1408, tn=1408, tk=1024)

    dx2_a, dpg, dpe = _rowwise(_vjp_fn(_ple_out, 3, 1), [(sv["x2"], d, 0), (sv["pg"], d, 0), (sv["pe"], d, 0), (dout, d, 0)],
                               [], [(d, F32), (d, BF16), (d, BF16)], name=nm("b_ple_out"))
    gr["w_ple"] = _matmul(sv["p_i"], dpe, dims="tn", name=nm("b_w_ple"), **big)
    gr["w_ple_gate"] = _matmul(sv["x2b"], dpg, dims="tn", name=nm("b_w_ple_gate"), **big)
    dx2 = _matmul(dpg, wl["w_ple_gate"], dims="nt", c=dx2_a, name=nm("b_x2"))

    def ln_bwd(xv, yv, ct, g, b):
        _, pull = jax.vjp(lambda a_, b_, c_, d_: _layer_norm(alpha * a_ + b_, c_, d_), xv, yv, g, b)
        return pull(ct)

    dx1_a, ddn, gr["ln2_g"], gr["ln2_b"] = _rowwise(
        ln_bwd, [(sv["x1"], d, 0), (sv["dn"], d, 0), (dx2, d, 0)], [wl["ln2_g"], wl["ln2_b"]],
        [(d, F32), (d, BF16)], [(1, d), (1, d)], name=nm("b_ln2"))
    gr["w_down"] = _matmul(sv["act"], ddn, dims="tn", name=nm("b_w_down"), **big)
    dact = _matmul(ddn, wl["w_down"], dims="nt", name=nm("b_act"), tn=1408)
    (dgu,) = _rowwise(_vjp_fn(_swiglu, 1, 1), [(sv["gu"], dm.ff2, 0), (dact, dm.ff2 // 2, 0)], [], [(dm.ff2, BF16)],
                      name=nm("b_swiglu"))
    gr["w_gate_up"] = _matmul(sv["x1b"], dgu, dims="tn", name=nm("b_w_gate_up"), **big)
    dx1 = _matmul(dgu, wl["w_gate_up"], dims="nt", c=dx1_a, name=nm("b_x1"))

    dx_a, dmix, gr["ln1_g"], gr["ln1_b"] = _rowwise(
        ln_bwd, [(sv["x"], d, 0), (sv["mix"], d, 0), (dx1, d, 0)], [wl["ln1_g"], wl["ln1_b"]],
        [(d, F32), (d, BF16)], [(1, d), (1, d)], name=nm("b_ln1"))
    gr["w_out"] = _matmul(sv["om"], dmix, dims="tn", name=nm("b_w_out"), **big)
    dom = _matmul(dmix, wl["w_out"], dims="nt", name=nm("b_om"))

    nq = t // min(ATT_TILE, t)
    dqa, delta = _attn_bwd_dq(sv["qa"], sv["ka"], sv["va"], sv["o_mla"], sv["lse"], dom, hw // HEAD_DIM, name=nm("b_attn_dq"))
    lse_row = sv["lse"].reshape(N_HEADS, nq, 1, t // nq)
    delta_row = delta.reshape(N_HEADS, nq, 1, t // nq)
    dka, dva = _attn_bwd_dkv(sv["qa"], sv["ka"], sv["va"], lse_row, delta_row, dom, hw // HEAD_DIM, name=nm("b_attn_dkv"))
    scale = (HEAD_DIM + ROPE_DIM) ** -0.5
    qk_fn = functools.partial(_mla_qk, scale)

    def qk_bwd(qm, kvm, misc, cs, sn, g_q, g_k, g_v):
        _, pull = jax.vjp(lambda a, b, c_: qk_fn(a, b, c_, cs, sn), qm, kvm, misc)
        return pull((g_q, g_k, g_v))

    dqm, dkvm, dmisc_rope = _rowwise(
        qk_bwd, [(sv["qm"], N_HEADS * QK_DIM, 0), (sv["kvm"], 2 * hw, 0), (sv["h"], LANES, misc_cb), (cos_t, LANES, 0),
                 (sin_t, LANES, 0), (dqa, N_HEADS * QK_DIM, 0), (dka, N_HEADS * QK_DIM, 0), (dva, hw, 0)], [],
        [(N_HEADS * QK_DIM, BF16), (2 * hw, BF16), (LANES, F32)], name=nm("b_mla_qk"))
    gr["w_uq"] = _matmul(sv["cqn"], dqm, dims="tn", name=nm("b_w_uq"), **big)
    gr["w_ukv"] = _matmul(sv["ckvn"], dkvm, dims="tn", name=nm("b_w_ukv"), **big)
    dcqn = _matmul(dqm, wl["w_uq"], dims="nt", name=nm("b_cqn"))
    dckvn = _matmul(dkvm, wl["w_ukv"], dims="nt", name=nm("b_ckvn"))

    def norm_bwd(ckv, cq, g_q, g_kv, kvg, qg):
        _, pull = jax.vjp(_mla_norm, ckv, cq, kvg, qg)
        return pull((g_q, g_kv))

    dckv, dcq, gr["kvg_row"], gr["qg_row"] = _rowwise(
        norm_bwd, [(sv["h"], dm.kv_lora, dm.c_kv0 // dm.kv_lora), (sv["h"], dm.q_lora, dm.c_q0 // dm.q_lora),
                   (dcqn, dm.q_lora, 0), (dckvn, dm.kv_lora, 0)], [wl["kvg_row"], wl["qg_row"]],
        [(dm.kv_lora, BF16), (dm.q_lora, BF16)], [(1, dm.kv_lora), (1, dm.q_lora)], name=nm("b_mla_norm"))

    def gout_bwd(o, z, g_o, gn):
        _, pull = jax.vjp(_gdn_out, o, z, gn)
        return pull(g_o)

    do_gdn, dz, gr["gn_row"] = _rowwise(gout_bwd, [(sv["o_gdn"], hw, 0), (sv["h"], hw, 3), (dom, hw, 0)], [wl["gn_row"]],
                                        [(hw, F32), (hw, BF16)], [(1, HEAD_DIM)], name=nm("b_gdn_out"))
    dqn, dkn, dvg, dgb = _gdn_bwd(sv["qn"], sv["kn"], sv["vg"], sv["gb"], sv["s_all"], do_gdn, name=nm("b_gdn"))

    def act_bwd(u, misc, g_q, g_k, g_v, g_gb, g_rope, alog, dtb):
        _, pull = jax.vjp(_gdn_act, u, misc, alog, dtb)
        du_, dmisc_, dalog_, ddtb_ = pull((g_q, g_k, g_v, g_gb))
        return du_, dmisc_ + g_rope, dalog_, ddtb_

    du, dmisc, gr["alog_row"], gr["dtb_row"] = _rowwise(
        act_bwd, [(sv["u"], 3 * hw, 0), (sv["h"], LANES, misc_cb), (dqn, hw, 0), (dkn, hw, 0), (dvg, hw, 0),
                  (dgb, LANES, 0), (dmisc_rope, LANES, 0)], [wl["alog_row"], wl["dtb_row"]],
        [(3 * hw, F32), (LANES, BF16)], [(1, LANES), (1, LANES)], name=nm("b_gdn_act"))
    dqkv, dconv = _conv_bwd(du, sv["h"], wl["conv_w"], 3 * hw, name=nm("b_conv"))
    gr["conv_w"] = dconv[:CONV_W]

    dh = jnp.concatenate([dqkv, dz, dckv, dcq, dmisc], axis=1)
    gr["w_in"] = _matmul(sv["xb"], dh, dims="tn", name=nm("b_w_in"), **big)
    dx = _matmul(dh, wl["w_in"], dims="nt", c=dx_a, name=nm("b_x"), tk=1408)
    return dx, gr


def _local_weights(dm, full, layer):
    wl = {}
    wl["w_in"] = dm.w_in_local(full["w_in"][layer])
    wl["w_uq"] = dm.w_uq_local(full["w_uq"][layer])
    wl["w_ukv"] = dm.w_ukv_local(full["w_ukv"][layer])
    for n in ("w_out", "w_gate_up", "w_down", "w_ple", "w_ple_gate", "conv_w"):
        wl[n] = full[n][layer]
    wl["alog_row"] = _lane_row(full["a_log"][layer], MISC_A0)
    wl["dtb_row"] = _lane_row(full["dt_bias"][layer], MISC_A0)
    wl["gn_row"] = full["gdn_norm_g"][layer][None, :]
    wl["qg_row"] = full["q_norm_g"][layer][None, :]
    wl["kvg_row"] = full["kv_norm_g"][layer][None, :]
    for n in ("ln1_g", "ln1_b", "ln2_g", "ln2_b"):
        wl[n] = full[n][layer][None, :]
    return wl


def _global_grads(dm, gr):
    out = {"w_in": dm.w_in_global(gr["w_in"]), "w_uq": dm.w_uq_global(gr["w_uq"]), "w_ukv": dm.w_ukv_global(gr["w_ukv"])}
    for n in ("w_out", "w_gate_up", "w_down", "w_ple", "w_ple_gate", "conv_w"):
        out[n] = gr[n]
    out["a_log"] = gr["alog_row"][0, MISC_A0:MISC_A0 + N_HEADS]
    out["dt_bias"] = gr["dtb_row"][0, MISC_A0:MISC_A0 + N_HEADS]
    out["gdn_norm_g"] = gr["gn_row"][0]
    out["q_norm_g"] = gr["qg_row"][0]
    out["kv_norm_g"] = gr["kvg_row"][0]
    for n in ("ln1_g", "ln1_b", "ln2_g", "ln2_b"):
        out[n] = gr[n][0]
    return out


def _local_step(x, p, positions, target, full):
    depth = p.shape[0]
    dm = _Dims(x.shape[1], full["w_in"].shape[2], full["w_uq"].shape[1], full["w_ukv"].shape[1],
               full["w_gate_up"].shape[2], p.shape[2])
    alpha = (2.0 * depth) ** 0.25
    freq = ROPE_THETA ** (-jnp.arange(0, ROPE_DIM, 2, dtype=F32) / ROPE_DIM)
    inv_freq_row = _lane_row(jnp.concatenate([freq, freq]), 0)
    cos_t, sin_t = _rope_tables(positions.reshape(-1, 1), inv_freq_row, name="rope_tables")

    wls = [_local_weights(dm, full, i) for i in range(depth)]
    saved = []
    cur, cur_b = x, x
    for i in range(depth):
        cur, cur_b, sv = _layer_fwd(dm, alpha, cur, cur_b, p[i], cos_t, sin_t, wls[i], f"l{i}")
        saved.append(sv)
    dy, loss_blk = _loss_head(cur, target)
    grads = [None] * depth
    for i in reversed(range(depth)):
        dy, gr = _layer_bwd(dm, alpha, dy, saved[i], cos_t, sin_t, wls[i], f"l{i}")
        grads[i] = _global_grads(dm, gr)
    return loss_blk[0, 0], dy, grads


def kernel(x, p, positions, w_in, conv_w, a_log, dt_bias, gdn_norm_g, q_norm_g, w_uq, kv_norm_g, w_ukv, w_out, ln1_g, ln1_b, w_gate_up, w_down, ln2_g, ln2_b, w_ple, w_ple_gate, loss_target, m_w_in, m_conv_w, m_a_log, m_dt_bias, m_gdn_norm_g, m_q_norm_g, m_w_uq, m_kv_norm_g, m_w_ukv, m_w_out, m_ln1_g, m_ln1_b, m_w_gate_up, m_w_down, m_ln2_g, m_ln2_b, m_w_ple, m_w_ple_gate, v_w_in, v_conv_w, v_a_log, v_dt_bias, v_gdn_norm_g, v_q_norm_g, v_w_uq, v_kv_norm_g, v_w_ukv, v_w_out, v_ln1_g, v_ln1_b, v_w_gate_up, v_w_down, v_ln2_g, v_ln2_b, v_w_ple, v_w_ple_gate):
    w = dict(w_in=w_in, conv_w=conv_w, a_log=a_log, dt_bias=dt_bias, gdn_norm_g=gdn_norm_g, q_norm_g=q_norm_g, w_uq=w_uq,
             kv_norm_g=kv_norm_g, w_ukv=w_ukv, w_out=w_out, ln1_g=ln1_g, ln1_b=ln1_b, w_gate_up=w_gate_up, w_down=w_down,
             ln2_g=ln2_g, ln2_b=ln2_b, w_ple=w_ple, w_ple_gate=w_ple_gate)
    m = dict(w_in=m_w_in, conv_w=m_conv_w, a_log=m_a_log, dt_bias=m_dt_bias, gdn_norm_g=m_gdn_norm_g, q_norm_g=m_q_norm_g,
             w_uq=m_w_uq, kv_norm_g=m_kv_norm_g, w_ukv=m_w_ukv, w_out=m_w_out, ln1_g=m_ln1_g, ln1_b=m_ln1_b,
             w_gate_up=m_w_gate_up, w_down=m_w_down, ln2_g=m_ln2_g, ln2_b=m_ln2_b, w_ple=m_w_ple, w_ple_gate=m_w_ple_gate)
    v = dict(w_in=v_w_in, conv_w=v_conv_w, a_log=v_a_log, dt_bias=v_dt_bias, gdn_norm_g=v_gdn_norm_g, q_norm_g=v_q_norm_g,
             w_uq=v_w_uq, kv_norm_g=v_kv_norm_g, w_ukv=v_w_ukv, w_out=v_w_out, ln1_g=v_ln1_g, ln1_b=v_ln1_b,
             w_gate_up=v_w_gate_up, w_down=v_w_down, ln2_g=v_ln2_g, ln2_b=v_ln2_b, w_ple=v_w_ple, w_ple_gate=v_w_ple_gate)
    depth = w_in.shape[0]
    assert depth % 2 == 0
    lay = _Layout({n: w[n].shape[1:] for n in MATRICES + VECTORS})
    cx, cy, cc = lax.axis_index("x"), lax.axis_index("y"), lax.axis_index("c")
    chip = 2 * cx + cy

    packed = lay.pack(w, depth, BF16).reshape(depth * lay.rows, PACK_W)
    gathered = _gather_chips(packed, name="gather_weights").reshape(N_CHIPS, depth, lay.rows, PACK_W)
    conv_rows = 16
    conv_flat = conv_w.reshape(depth, -1)
    conv_pad = jnp.zeros((depth, conv_rows * PACK_W - conv_flat.shape[1]), F32)
    conv_packed = jnp.concatenate([conv_flat, conv_pad], axis=1).reshape(depth * conv_rows, PACK_W)
    conv_all = _gather_chips(conv_packed, name="gather_conv").reshape(N_CHIPS, depth, conv_rows * PACK_W)
    full = {}
    for n in MATRICES:
        if n == "conv_w":
            size = conv_flat.shape[1]
            parts = conv_all[:, :, :size].reshape((N_CHIPS, depth) + lay.shapes[n])
        else:
            parts = lay.unpack(gathered, n)
        full[n] = jnp.concatenate([parts[j] for j in range(N_CHIPS)], axis=1 + _shard_axis(n))
    for n in VECTORS:
        full[n] = w[n]

    loss_local, grad_x, grads = _local_step(x[0], p[:, 0], positions[0], loss_target[0], full)
    loss = lax.psum(loss_local, ("x", "y", "c"))

    per_owner = []
    for j in range(N_CHIPS):
        tensors = {}
        for n in MATRICES:
            stacked = jnp.stack([grads[i][n] for i in range(depth)])
            size = lay.shapes[n][_shard_axis(n)]
            tensors[n] = lax.slice_in_dim(stacked, j * size, (j + 1) * size, axis=1 + _shard_axis(n))
        for n in VECTORS:
            tensors[n] = jnp.stack([grads[i][n] for i in range(depth)])
        per_owner.append(lay.pack(tensors, depth, F32))
    half_rows = depth // 2 * lay.rows
    g_all = jnp.stack(per_owner).reshape(N_CHIPS, 2, half_rows, PACK_W)
    c_idx = cc.reshape(1).astype(jnp.int32)
    chip_idx = chip.reshape(1).astype(jnp.int32)
    from_sibling = _sibling_take_other_half(g_all, name="reduce_sibling")
    chip_sum = _add_own_half(g_all, from_sibling, c_idx, name="reduce_add_sibling")
    from_chips = _chips_exchange(chip_sum, name="reduce_chips")
    my_half = _sum_chips(chip_sum, from_chips, chip_idx, name="reduce_sum_chips")
    reduced = _sibling_join_halves(my_half, name="reduce_join").reshape(depth, lay.rows, PACK_W)

    grad_w, delta_w, new_m, new_v = {}, {}, {}, {}
    for n in WEIGHTS:
        grad_w[n] = lay.unpack(reduced, n)
        delta_w[n], new_m[n], new_v[n] = _adamw(w[n], grad_w[n], m[n], v[n], name=f"adamw_{n}")
    return (loss, grad_x[None], *[grad_w[n] for n in WEIGHTS], *[delta_w[n] for n in WEIGHTS],
            *[new_m[n] for n in WEIGHTS], *[new_v[n] for n in WEIGHTS])
```

```python
import functools

import jax
import jax.numpy as jnp
from jax import lax
from jax.experimental import pallas as pl
from jax.experimental.pallas import tpu as pltpu

F32 = jnp.float32
BF16 = jnp.bfloat16
HIGH = lax.Precision.HIGH
MESH = pl.DeviceIdType.MESH

CHUNK = 64
N_HEADS = 4
HEAD_DIM = 128
ROPE_DIM = 64
ROPE_THETA = 10000.0
LN_EPS = 1e-5
RMS_EPS = 1e-6
ADAM_LR, ADAM_B1, ADAM_B2, ADAM_EPS, ADAM_WD, ADAM_STEP = 0.001, 0.9, 0.999, 1e-08, 0.01, 10

LANES = 128
VMEM_LIMIT = 48 * 1024 * 1024
PACK_W = 512
ROW_TILE = 256
SUB_ROWS = 16

MISC_BETA0 = ROPE_DIM
MISC_A0 = ROPE_DIM + N_HEADS

NN = (((1,), (0,)), ((), ()))
NT = (((1,), (1,)), ((), ()))
TN = (((0,), (0,)), ((), ()))


def _params(sem=None):
    return pltpu.CompilerParams(dimension_semantics=sem, vmem_limit_bytes=VMEM_LIMIT)


def _divisor_tile(dim, target, unit):
    best = None
    t = unit
    while t <= min(dim, target):
        if dim % t == 0:
            best = t
        t += unit
    return best if best is not None else dim


def _make_dots(high_precision):
    def raw(a, b, dims):
        if high_precision:
            return lax.dot_general(a, b, dims, precision=HIGH, preferred_element_type=F32)
        return lax.dot_general(a.astype(BF16), b.astype(BF16), dims, preferred_element_type=F32)

    @jax.custom_vjp
    def nn(a, b):
        return raw(a, b, NN)

    @jax.custom_vjp
    def nt(a, b):
        return raw(a, b, NT)

    @jax.custom_vjp
    def tn(a, b):
        return raw(a, b, TN)

    nn.defvjp(lambda a, b: (raw(a, b, NN), (a, b)), lambda r, g: (nt(g, r[1]), tn(r[0], g)))
    nt.defvjp(lambda a, b: (raw(a, b, NT), (a, b)), lambda r, g: (nn(g, r[1]), tn(g, r[0])))
    tn.defvjp(lambda a, b: (raw(a, b, TN), (a, b)), lambda r, g: (nt(r[1], g), nn(r[0], g)))
    return nn, nt, tn


_nn, _nt, _tn = _make_dots(False)
_hnn, _hnt, _htn = _make_dots(True)


def _matmul(a, b, *, dims, name, c=None, out_dtype=F32, tm=1024, tn=512, tk=1408, layer=None, into=None):
    b_shape = b.shape[-2:]
    if dims == "nn":
        (m, k), (k2, n) = a.shape, b_shape
    elif dims == "nt":
        (m, k), (n, k2) = a.shape, b_shape
    else:
        (k, m), (k2, n) = a.shape, b_shape
    assert k == k2, (a.shape, b.shape, dims)
    tm = _divisor_tile(m, tm, LANES)
    tn = _divisor_tile(n, tn, LANES)
    tk = _divisor_tile(k, tk, LANES)
    nk = k // tk
    dn = {"nn": NN, "nt": NT, "tn": TN}[dims]
    if dims == "tn":
        a_spec = pl.BlockSpec((tk, tm), lambda i, j, kk: (kk, i))
    else:
        a_spec = pl.BlockSpec((tm, tk), lambda i, j, kk: (i, kk))
    b_blk, b_idx = ((tn, tk), lambda i, j, kk: (j, kk)) if dims == "nt" else ((tk, tn), lambda i, j, kk: (kk, j))
    if b.ndim == 3:
        b_spec = pl.BlockSpec((None,) + b_blk, lambda i, j, kk: (layer,) + b_idx(i, j, kk))
    else:
        b_spec = pl.BlockSpec(b_blk, b_idx)
    c_spec = pl.BlockSpec((tm, tn), lambda i, j, kk: (i, j))
    if into is not None:
        assert into.shape[1:] == (m, n) and into.dtype == out_dtype
        o_spec = pl.BlockSpec((None, tm, tn), lambda i, j, kk: (layer, i, j))
        out_shape = jax.ShapeDtypeStruct(into.shape, into.dtype)
    else:
        o_spec = c_spec
        out_shape = jax.ShapeDtypeStruct((m, n), out_dtype)
    has_c = c is not None

    def body(*refs):
        a_ref, b_ref = refs[:2]
        c_ref = refs[2] if has_c else None
        o_ref, acc_ref = refs[-2:]
        kk = pl.program_id(2)

        @pl.when(kk == 0)
        def _():
            if has_c:
                acc_ref[...] = c_ref[...].astype(F32)
            else:
                acc_ref[...] = jnp.zeros_like(acc_ref)

        acc_ref[...] += lax.dot_general(a_ref[...].astype(BF16), b_ref[...].astype(BF16), dn,
                                        preferred_element_type=F32)

        @pl.when(kk == nk - 1)
        def _():
            o_ref[...] = acc_ref[...].astype(o_ref.dtype)

    ins = [a, b] + ([c] if has_c else [])
    specs = [a_spec, b_spec] + ([c_spec] if has_c else [])
    aliases = {}
    if into is not None:
        aliases = {len(ins): 0}
        ins.append(into)
        specs.append(pl.BlockSpec(memory_space=pl.ANY))
    return pl.pallas_call(
        body, name=name, grid=(m // tm, n // tn, nk), in_specs=specs, out_specs=o_spec, out_shape=out_shape,
        scratch_shapes=[pltpu.VMEM((tm, tn), F32)], input_output_aliases=aliases,
        compiler_params=_params(("arbitrary", "arbitrary", "arbitrary")),
    )(*ins)


def _rowwise(fn, rows, params, outs, accs=(), *, name, tm=ROW_TILE, sub=SUB_ROWS):
    t = rows[0][0].shape[0]
    tm = min(tm, t)
    assert t % tm == 0 and tm % sub == 0
    n_rows, n_par, n_out, n_acc = len(rows), len(params), len(outs), len(accs)

    def body(*refs):
        row_refs = refs[:n_rows]
        par_refs = refs[n_rows:n_rows + n_par]
        out_refs = refs[n_rows + n_par:n_rows + n_par + n_out]
        acc_refs = refs[n_rows + n_par + n_out:]
        if n_acc:
            @pl.when(pl.program_id(0) == 0)
            def _():
                for a_ref in acc_refs:
                    a_ref[...] = jnp.zeros_like(a_ref)

        def step(r, carry):
            sl = pl.ds(pl.multiple_of(r * sub, sub), sub)
            vals = [ref[sl, :].astype(F32) for ref in row_refs] + [ref[...] for ref in par_refs]
            res = fn(*vals)
            for o_ref, val in zip(out_refs, res[:n_out]):
                o_ref[sl, :] = val.astype(o_ref.dtype)
            for a_ref, val in zip(acc_refs, res[n_out:]):
                a_ref[...] += val
            return carry

        lax.fori_loop(0, tm // sub, step, 0)

    in_specs = [pl.BlockSpec((tm, w), functools.partial(lambda i, cb: (i, cb), cb=cb)) for _, w, cb in rows]
    in_specs += [pl.BlockSpec(p.shape, lambda i: (0, 0)) for p in params]
    out_specs = [pl.BlockSpec((tm, w), lambda i: (i, 0)) for w, _ in outs]
    out_specs += [pl.BlockSpec(s, lambda i: (0, 0)) for s in accs]
    out_shape = [jax.ShapeDtypeStruct((t, w), d) for w, d in outs]
    out_shape += [jax.ShapeDtypeStruct(s, F32) for s in accs]
    return pl.pallas_call(
        body, name=name, grid=(t // tm,), in_specs=in_specs, out_specs=out_specs, out_shape=out_shape,
        compiler_params=_params(("arbitrary",)),
    )(*[r[0] for r in rows], *params)


def _vjp_fn(fn, n_in, n_out):
    def bwd(*args):
        ins, cts = args[:n_in], args[n_in:]
        _, pull = jax.vjp(fn, *ins)
        return pull(tuple(cts) if n_out > 1 else cts[0])
    return bwd


def _lane(shape):
    return lax.broadcasted_iota(jnp.int32, shape, 1)


def _silu(x):
    return x * jax.nn.sigmoid(x)


def _softplus(x):
    return jnp.maximum(x, 0.0) + jnp.log1p(jnp.exp(-jnp.abs(x)))


def _heads(x, width=HEAD_DIM):
    return [x[:, h * width:(h + 1) * width] for h in range(N_HEADS)]


def _layer_norm(z, g, b):
    mu = jnp.mean(z, -1, keepdims=True)
    zc = z - mu
    var = jnp.mean(zc * zc, -1, keepdims=True)
    return zc * lax.rsqrt(var + LN_EPS) * g + b


def _gdn_act(u, misc, alog_row, dtb_row):
    s = _silu(u)
    w = N_HEADS * HEAD_DIM
    q = jnp.concatenate([t * lax.rsqrt(jnp.sum(t * t, -1, keepdims=True) + RMS_EPS) * HEAD_DIM ** -0.5
                         for t in _heads(s[:, :w])], axis=1)
    k = jnp.concatenate([t * lax.rsqrt(jnp.sum(t * t, -1, keepdims=True) + RMS_EPS)
                         for t in _heads(s[:, w:2 * w])], axis=1)
    v = s[:, 2 * w:]
    lane = _lane(misc.shape)
    beta = jax.nn.sigmoid(misc)
    g = -jnp.exp(alog_row) * _softplus(misc + dtb_row)
    is_beta = (lane >= MISC_BETA0) & (lane < MISC_BETA0 + N_HEADS)
    is_g = (lane >= MISC_A0) & (lane < MISC_A0 + N_HEADS)
    gb = jnp.where(is_beta, beta, jnp.where(is_g, g, 0.0))
    return q, k, v, gb


def _gdn_out(o, z, gn_row):
    outs = []
    for oh, zh in zip(_heads(o), _heads(z)):
        r = oh * lax.rsqrt(jnp.mean(oh * oh, -1, keepdims=True) + RMS_EPS) * gn_row
        outs.append(r * _silu(zh))
    return jnp.concatenate(outs, axis=1)


def _mla_norm(ckv, cq, kvg_row, qg_row):
    cqn = cq * lax.rsqrt(jnp.mean(cq * cq, -1, keepdims=True) + RMS_EPS) * qg_row
    ckvn = ckv * lax.rsqrt(jnp.mean(ckv * ckv, -1, keepdims=True) + RMS_EPS) * kvg_row
    return cqn, ckvn


def _swap_halves(x):
    half = ROPE_DIM // 2
    return jnp.where(_lane(x.shape) < half, pltpu.roll(x, LANES - half, 1), pltpu.roll(x, half, 1))


@jax.custom_vjp
def _rope(x, cos_t, sin_t):
    return x * cos_t + _swap_halves(x) * sin_t


def _rope_fwd(x, cos_t, sin_t):
    return _rope(x, cos_t, sin_t), (cos_t, sin_t)


def _rope_bwd(res, g):
    cos_t, sin_t = res
    return g * cos_t - _swap_halves(g) * sin_t, jnp.zeros_like(cos_t), jnp.zeros_like(sin_t)


_rope.defvjp(_rope_fwd, _rope_bwd)


def _mla_qk(scale, qm, kv, misc, cos_t, sin_t):
    krope = _rope(misc, cos_t, sin_t)
    qs, ks = [], []
    for h in range(N_HEADS):
        base = 2 * HEAD_DIM * h
        qs += [qm[:, base:base + HEAD_DIM], _rope(qm[:, base + HEAD_DIM:base + 2 * HEAD_DIM], cos_t, sin_t)]
        ks += [kv[:, HEAD_DIM * h:HEAD_DIM * (h + 1)], krope]
    return jnp.concatenate(qs, axis=1) * scale, jnp.concatenate(ks, axis=1), kv[:, N_HEADS * HEAD_DIM:]


def _swiglu(gu):
    f = gu.shape[1] // 2
    return _silu(gu[:, :f]) * gu[:, f:]


def _ple_out(x2, pg, pe):
    return x2 + jax.nn.sigmoid(pg) * pe


CONV_W = 4
HALO = 8


def _conv_fwd(h, conv_w, width, *, name, tm=ROW_TILE, sub=32):
    t = h.shape[0]
    tm = min(tm, t)
    nb = tm // HALO

    def body(x_ref, halo_ref, w_ref, u_ref, buf):
        i = pl.program_id(0)
        buf[pl.ds(0, HALO), :] = jnp.where(i > 0, halo_ref[...], 0.0)
        buf[pl.ds(HALO, tm), :] = x_ref[...]
        w = w_ref[...]
        for r0 in range(0, tm, sub):
            acc = jnp.zeros((sub, width), F32)
            for j in range(CONV_W):
                acc = acc + w[j:j + 1, :] * buf[pl.ds(HALO + r0 - (CONV_W - 1) + j, sub), :]
            u_ref[pl.ds(r0, sub), :] = acc

    return pl.pallas_call(
        body, name=name, grid=(t // tm,),
        in_specs=[pl.BlockSpec((tm, width), lambda i: (i, 0)),
                  pl.BlockSpec((HALO, width), lambda i: (jnp.maximum(i * nb - 1, 0), 0)),
                  pl.BlockSpec(conv_w.shape, lambda i: (0, 0))],
        out_specs=pl.BlockSpec((tm, width), lambda i: (i, 0)),
        out_shape=jax.ShapeDtypeStruct((t, width), F32),
        scratch_shapes=[pltpu.VMEM((tm + HALO, width), F32)],
        compiler_params=_params(("arbitrary",)),
    )(h, h, conv_w)


def _conv_bwd(du, h, conv_w, width, *, name, tm=ROW_TILE, sub=32):
    t = h.shape[0]
    tm = min(tm, t)
    nb = tm // HALO
    n_tiles = t // tm

    def body(du_ref, du_halo, x_ref, x_halo, w_ref, dx_ref, dw_ref, dbuf, xbuf):
        i = pl.program_id(0)

        @pl.when(i == 0)
        def _():
            dw_ref[...] = jnp.zeros_like(dw_ref)

        dbuf[pl.ds(0, tm), :] = du_ref[...]
        dbuf[pl.ds(tm, HALO), :] = jnp.where(i < n_tiles - 1, du_halo[...], 0.0)
        xbuf[pl.ds(0, HALO), :] = jnp.where(i > 0, x_halo[...], 0.0)
        xbuf[pl.ds(HALO, tm), :] = x_ref[...]
        w = w_ref[...]
        dws = [jnp.zeros((1, width), F32) for _ in range(CONV_W)]
        for r0 in range(0, tm, sub):
            acc = jnp.zeros((sub, width), F32)
            d_here = dbuf[pl.ds(r0, sub), :]
            for j in range(CONV_W):
                acc = acc + w[j:j + 1, :] * dbuf[pl.ds(r0 + (CONV_W - 1) - j, sub), :]
                xs = xbuf[pl.ds(HALO + r0 - (CONV_W - 1) + j, sub), :]
                dws[j] = dws[j] + jnp.sum(d_here * xs, axis=0, keepdims=True)
            dx_ref[pl.ds(r0, sub), :] = acc.astype(dx_ref.dtype)
        for j in range(CONV_W):
            dw_ref[pl.ds(j, 1), :] += dws[j]

    return pl.pallas_call(
        body, name=name, grid=(n_tiles,),
        in_specs=[pl.BlockSpec((tm, width), lambda i: (i, 0)),
                  pl.BlockSpec((HALO, width), lambda i: (jnp.minimum((i + 1) * nb, t // HALO - 1), 0)),
                  pl.BlockSpec((tm, width), lambda i: (i, 0)),
                  pl.BlockSpec((HALO, width), lambda i: (jnp.maximum(i * nb - 1, 0), 0)),
                  pl.BlockSpec(conv_w.shape, lambda i: (0, 0))],
        out_specs=[pl.BlockSpec((tm, width), lambda i: (i, 0)),
                   pl.BlockSpec((HALO, width), lambda i: (0, 0))],
        out_shape=[jax.ShapeDtypeStruct((t, width), BF16), jax.ShapeDtypeStruct((HALO, width), F32)],
        scratch_shapes=[pltpu.VMEM((tm + HALO, width), F32), pltpu.VMEM((tm + HALO, width), F32)],
        compiler_params=_params(("arbitrary",)),
    )(du, du, h, h, conv_w)


@jax.custom_vjp
def _inv_unit_lower(low):
    n = low.shape[0]
    eye = (lax.broadcasted_iota(jnp.int32, (n, n), 0) == lax.broadcasted_iota(jnp.int32, (n, n), 1)).astype(F32)
    x = eye - low
    p = low
    span = 2
    while span < n:
        p = _hnn(p, p)
        x = x + _hnn(x, p)
        span *= 2
    return x


def _inv_fwd(low):
    x = _inv_unit_lower(low)
    return x, x


def _inv_bwd(x, g):
    return (-_htn(x, _hnt(g, x)),)


_inv_unit_lower.defvjp(_inv_fwd, _inv_bwd)


def _gdn_prep(q, k, v, gb):
    c = q.shape[0]
    gbt = gb.T
    row = lax.broadcasted_iota(jnp.int32, (c, c), 0)
    col = lax.broadcasted_iota(jnp.int32, (c, c), 1)
    tri_incl = row >= col
    tri_strict = row > col
    lane = _lane(gb.shape)
    sub = lax.broadcasted_iota(jnp.int32, gbt.shape, 0)
    last = lax.broadcasted_iota(jnp.int32, (c, 1), 0) == c - 1
    us, ws, qds, kds, attns = [], [], [], [], []
    glb = jnp.zeros(gb.shape, F32)
    for h in range(N_HEADS):
        g_col = jnp.sum(jnp.where(lane == MISC_A0 + h, gb, 0.0), axis=1, keepdims=True)
        b_col = jnp.sum(jnp.where(lane == MISC_BETA0 + h, gb, 0.0), axis=1, keepdims=True)
        g_row = jnp.sum(jnp.where(sub == MISC_A0 + h, gbt, 0.0), axis=0, keepdims=True)
        gc_col = jnp.sum(jnp.where(tri_incl, g_row, 0.0), axis=1, keepdims=True)
        gc_row = jnp.sum(jnp.where(row <= col, g_col, 0.0), axis=0, keepdims=True)
        decay = jnp.where(tri_incl, jnp.exp(jnp.where(tri_incl, gc_col - gc_row, 0.0)), 0.0)
        g_last = jnp.sum(jnp.where(last, gc_col, 0.0), axis=0, keepdims=True)
        hs = slice(h * HEAD_DIM, (h + 1) * HEAD_DIM)
        qh, kh, vh = q[:, hs], k[:, hs], v[:, hs]
        kb = kh * b_col
        low = jnp.where(tri_strict, _nt(kb, kh) * decay, 0.0)
        tinv = _inv_unit_lower(low)
        eg = jnp.exp(gc_col)
        sol = _hnn(tinv, jnp.concatenate([vh * b_col, kb * eg], axis=1))
        us.append(sol[:, :HEAD_DIM])
        ws.append(sol[:, HEAD_DIM:])
        attns.append(jnp.where(tri_incl, _nt(qh, kh) * decay, 0.0))
        qds.append(qh * eg)
        kds.append(kh * jnp.exp(g_last - gc_col))
        glb = glb + jnp.where(lane == h, g_last, 0.0)
    cat = lambda xs: jnp.concatenate(xs, axis=1)
    return cat(us), cat(ws), cat(qds), cat(kds), cat(attns), glb


def _gdn_seq(state, u, w, qd, kd, attn, glb):
    c = u.shape[0]
    first = lax.broadcasted_iota(jnp.int32, glb.shape, 0) == 0
    lane = _lane(glb.shape)
    outs, states = [], []
    for h in range(N_HEADS):
        hs = slice(h * HEAD_DIM, (h + 1) * HEAD_DIM)
        g_last = jnp.sum(jnp.sum(jnp.where(first & (lane == h), glb, 0.0), axis=1, keepdims=True), axis=0, keepdims=True)
        sh = state[hs, :]
        v_new = u[:, hs] - _nn(w[:, hs], sh)
        outs.append(_nn(qd[:, hs], sh) + _nn(attn[:, h * c:(h + 1) * c], v_new))
        states.append(sh * jnp.exp(g_last) + _tn(kd[:, hs], v_new))
    return jnp.concatenate(outs, axis=1), jnp.concatenate(states, axis=0)


PREP_CHUNKS = 2
SEQ_CHUNKS = 8


def _gdn_prep_fwd(q, k, v, gb, *, name):
    t, w = q.shape
    rows = min(PREP_CHUNKS * CHUNK, t)

    def body(q_ref, k_ref, v_ref, gb_ref, *out_refs):
        for r0 in range(0, rows, CHUNK):
            sl = pl.ds(r0, CHUNK)
            res = _gdn_prep(q_ref[sl, :], k_ref[sl, :], v_ref[sl, :], gb_ref[sl, :])
            for o_ref, val in zip(out_refs, res):
                o_ref[sl, :] = val

    spec = lambda width: pl.BlockSpec((rows, width), lambda i: (i, 0))
    widths = [w, w, w, w, N_HEADS * CHUNK, LANES]
    return pl.pallas_call(
        body, name=name, grid=(t // rows,),
        in_specs=[spec(w), spec(w), spec(w), spec(LANES)],
        out_specs=[spec(x) for x in widths],
        out_shape=[jax.ShapeDtypeStruct((t, x), F32) for x in widths],
        compiler_params=_params(("arbitrary",)),
    )(q, k, v, gb)


def _gdn_prep_bwd(q, k, v, gb, cts, *, name):
    t, w = q.shape
    rows = min(PREP_CHUNKS * CHUNK, t)

    def body(q_ref, k_ref, v_ref, gb_ref, du, dw, dqd, dkd, dattn, dglb, dq_ref, dk_ref, dv_ref, dgb_ref):
        for r0 in range(0, rows, CHUNK):
            sl = pl.ds(r0, CHUNK)
            _, pull = jax.vjp(_gdn_prep, q_ref[sl, :], k_ref[sl, :], v_ref[sl, :], gb_ref[sl, :])
            dq, dk, dv, dgb = pull(tuple(r[sl, :] for r in (du, dw, dqd, dkd, dattn, dglb)))
            dq_ref[sl, :] = dq
            dk_ref[sl, :] = dk
            dv_ref[sl, :] = dv
            dgb_ref[sl, :] = dgb

    spec = lambda width: pl.BlockSpec((rows, width), lambda i: (i, 0))
    widths = [w, w, w, w, N_HEADS * CHUNK, LANES]
    return pl.pallas_call(
        body, name=name, grid=(t // rows,),
        in_specs=[spec(w), spec(w), spec(w), spec(LANES)] + [spec(x) for x in widths],
        out_specs=[spec(w), spec(w), spec(w), spec(LANES)],
        out_shape=[jax.ShapeDtypeStruct((t, w), F32)] * 3 + [jax.ShapeDtypeStruct((t, LANES), F32)],
        compiler_params=_params(("arbitrary",)),
    )(q, k, v, gb, *cts)


def _gdn_seq_fwd(prep, *, name):
    t, w = prep[0].shape
    rows = min(SEQ_CHUNKS * CHUNK, t)
    per = rows // CHUNK

    def body(u_ref, w_ref, qd_ref, kd_ref, at_ref, gl_ref, o_ref, sall_ref, s_scr):
        @pl.when(pl.program_id(0) == 0)
        def _():
            s_scr[...] = jnp.zeros_like(s_scr)

        def step(j, carry):
            sl = pl.ds(pl.multiple_of(j * CHUNK, CHUNK), CHUNK)
            s = s_scr[...]
            sall_ref[j] = s
            o, s_new = _gdn_seq(s, u_ref[sl, :], w_ref[sl, :], qd_ref[sl, :], kd_ref[sl, :], at_ref[sl, :], gl_ref[sl, :])
            o_ref[sl, :] = o
            s_scr[...] = s_new
            return carry

        lax.fori_loop(0, per, step, 0)

    spec = lambda width: pl.BlockSpec((rows, width), lambda i: (i, 0))
    widths = [w, w, w, w, N_HEADS * CHUNK, LANES]
    return pl.pallas_call(
        body, name=name, grid=(t // rows,),
        in_specs=[spec(x) for x in widths],
        out_specs=[spec(w), pl.BlockSpec((per, w, HEAD_DIM), lambda i: (i, 0, 0))],
        out_shape=[jax.ShapeDtypeStruct((t, w), F32), jax.ShapeDtypeStruct((t // CHUNK, w, HEAD_DIM), F32)],
        scratch_shapes=[pltpu.VMEM((w, HEAD_DIM), F32)],
        compiler_params=_params(("arbitrary",)),
    )(*prep)


def _gdn_seq_bwd(prep, s_all, do, *, name):
    t, w = prep[0].shape
    rows = min(SEQ_CHUNKS * CHUNK, t)
    per = rows // CHUNK
    n = t // rows

    def body(u_ref, w_ref, qd_ref, kd_ref, at_ref, gl_ref, sall_ref, do_ref, du, dw, dqd, dkd, dat, dgl, ds_scr):
        @pl.when(pl.program_id(0) == 0)
        def _():
            ds_scr[...] = jnp.zeros_like(ds_scr)

        def step(jj, carry):
            j = per - 1 - jj
            sl = pl.ds(pl.multiple_of(j * CHUNK, CHUNK), CHUNK)
            _, pull = jax.vjp(_gdn_seq, sall_ref[j], u_ref[sl, :], w_ref[sl, :], qd_ref[sl, :], kd_ref[sl, :],
                              at_ref[sl, :], gl_ref[sl, :])
            res = pull((do_ref[sl, :], ds_scr[...]))
            ds_scr[...] = res[0]
            for o_ref, val in zip((du, dw, dqd, dkd, dat, dgl), res[1:]):
                o_ref[sl, :] = val
            return carry

        lax.fori_loop(0, per, step, 0)

    spec = lambda width: pl.BlockSpec((rows, width), lambda i: (n - 1 - i, 0))
    widths = [w, w, w, w, N_HEADS * CHUNK, LANES]
    return pl.pallas_call(
        body, name=name, grid=(n,),
        in_specs=[spec(x) for x in widths] + [pl.BlockSpec((per, w, HEAD_DIM), lambda i: (n - 1 - i, 0, 0)), spec(w)],
        out_specs=[spec(x) for x in widths],
        out_shape=[jax.ShapeDtypeStruct((t, x), F32) for x in widths],
        scratch_shapes=[pltpu.VMEM((w, HEAD_DIM), F32)],
        compiler_params=_params(("arbitrary",)),
    )(*prep, s_all, do)


QK_DIM = 2 * HEAD_DIM
ATT_TILE = 512
NEG = -1e30


def _chunk_mask(tq, tk, key_major):
    r = lax.broadcasted_iota(jnp.int32, (tq, tk), 0) // CHUNK
    c = lax.broadcasted_iota(jnp.int32, (tq, tk), 1) // CHUNK
    return (r <= c) if key_major else (c <= r)


def _dot_nt(a, b):
    return lax.dot_general(a, b, NT, preferred_element_type=F32)


def _dot_nn(a, b):
    return lax.dot_general(a, b, NN, preferred_element_type=F32)


def _attn_fwd(q, k, v, *, name):
    t = q.shape[0]
    tq = min(ATT_TILE, t)
    nq = t // tq

    def body(q_ref, k_ref, v_ref, o_ref, lse_ref, m_scr, l_scr, acc_scr):
        qi = pl.program_id(1)
        m_scr[...] = jnp.full_like(m_scr, NEG)
        l_scr[...] = jnp.zeros_like(l_scr)
        acc_scr[...] = jnp.zeros_like(acc_scr)
        qv = q_ref[...]

        def step(kj, masked):
            rows = pl.ds(pl.multiple_of(kj * tq, tq), tq)
            s = _dot_nt(qv, k_ref[rows, :])
            if masked:
                s = jnp.where(_chunk_mask(tq, tq, False), s, NEG)
            m_old = m_scr[...]
            m_new = jnp.maximum(m_old, jnp.max(s, axis=1, keepdims=True))
            p = jnp.exp(s - m_new)
            alpha = jnp.exp(m_old - m_new)
            l_scr[...] = alpha * l_scr[...] + jnp.sum(p, axis=1, keepdims=True)
            acc_scr[...] = alpha * acc_scr[...] + _dot_nn(p.astype(BF16), v_ref[rows, :])
            m_scr[...] = m_new

        def loop_body(kj, carry):
            step(kj, False)
            return carry

        lax.fori_loop(0, qi, loop_body, 0)
        step(qi, True)
        o_ref[...] = (acc_scr[...] / l_scr[...]).astype(o_ref.dtype)
        lse_ref[...] = m_scr[...] + jnp.log(l_scr[...])

    return pl.pallas_call(
        body, name=name, grid=(N_HEADS, nq),
        in_specs=[pl.BlockSpec((tq, QK_DIM), lambda h, i: (i, h)),
                  pl.BlockSpec((t, QK_DIM), lambda h, i: (0, h)),
                  pl.BlockSpec((t, HEAD_DIM), lambda h, i: (0, h))],
        out_specs=[pl.BlockSpec((tq, HEAD_DIM), lambda h, i: (i, h)),
                   pl.BlockSpec((None, tq, 1), lambda h, i: (h, i, 0))],
        out_shape=[jax.ShapeDtypeStruct((t, N_HEADS * HEAD_DIM), BF16),
                   jax.ShapeDtypeStruct((N_HEADS, t, 1), F32)],
        scratch_shapes=[pltpu.VMEM((tq, 1), F32), pltpu.VMEM((tq, 1), F32), pltpu.VMEM((tq, HEAD_DIM), F32)],
        compiler_params=_params(("arbitrary", "arbitrary")),
    )(q, k, v)


def _attn_bwd_dq(q, k, v, o, lse, dom, do_col0, *, name):
    t = q.shape[0]
    tq = min(ATT_TILE, t)
    nq = t // tq

    def body(q_ref, k_ref, v_ref, o_ref, lse_ref, do_ref, dq_ref, delta_ref, acc_scr):
        qi = pl.program_id(1)
        acc_scr[...] = jnp.zeros_like(acc_scr)
        qv = q_ref[...]
        do = do_ref[...]
        delta = jnp.sum(do * o_ref[...].astype(F32), axis=1, keepdims=True)
        delta_ref[...] = delta
        do_b = do.astype(BF16)
        lse_v = lse_ref[...]

        def step(kj, masked):
            rows = pl.ds(pl.multiple_of(kj * tq, tq), tq)
            kv_ = k_ref[rows, :]
            p = jnp.exp(_dot_nt(qv, kv_) - lse_v)
            if masked:
                p = jnp.where(_chunk_mask(tq, tq, False), p, 0.0)
            ds = p * (_dot_nt(do_b, v_ref[rows, :]) - delta)
            acc_scr[...] += _dot_nn(ds.astype(BF16), kv_)

        def loop_body(kj, carry):
            step(kj, False)
            return carry

        lax.fori_loop(0, qi, loop_body, 0)
        step(qi, True)
        dq_ref[...] = acc_scr[...].astype(dq_ref.dtype)

    return pl.pallas_call(
        body, name=name, grid=(N_HEADS, nq),
        in_specs=[pl.BlockSpec((tq, QK_DIM), lambda h, i: (i, h)),
                  pl.BlockSpec((t, QK_DIM), lambda h, i: (0, h)),
                  pl.BlockSpec((t, HEAD_DIM), lambda h, i: (0, h)),
                  pl.BlockSpec((tq, HEAD_DIM), lambda h, i: (i, h)),
                  pl.BlockSpec((None, tq, 1), lambda h, i: (h, i, 0)),
                  pl.BlockSpec((tq, HEAD_DIM), lambda h, i: (i, do_col0 + h))],
        out_specs=[pl.BlockSpec((tq, QK_DIM), lambda h, i: (i, h)),
                   pl.BlockSpec((None, tq, 1), lambda h, i: (h, i, 0))],
        out_shape=[jax.ShapeDtypeStruct((t, N_HEADS * QK_DIM), BF16),
                   jax.ShapeDtypeStruct((N_HEADS, t, 1), F32)],
        scratch_shapes=[pltpu.VMEM((tq, QK_DIM), F32)],
        compiler_params=_params(("arbitrary", "arbitrary")),
    )(q, k, v, o, lse, dom)


def _attn_bwd_dkv(q, k, v, lse_row, delta_row, dom, do_col0, *, name):
    t = q.shape[0]
    tk = min(ATT_TILE, t)
    nk = t // tk

    def body(q_ref, k_ref, v_ref, lse_ref, delta_ref, do_ref, dk_ref, dv_ref, dk_scr, dv_scr):
        kj = pl.program_id(1)
        dk_scr[...] = jnp.zeros_like(dk_scr)
        dv_scr[...] = jnp.zeros_like(dv_scr)
        kv_ = k_ref[...]
        vv = v_ref[...]

        def step(qi, masked):
            rows = pl.ds(pl.multiple_of(qi * tk, tk), tk)
            qv = q_ref[rows, :]
            do_b = do_ref[rows, :].astype(BF16)
            p = jnp.exp(_dot_nt(kv_, qv) - lse_ref[qi])
            if masked:
                p = jnp.where(_chunk_mask(tk, tk, True), p, 0.0)
            dv_scr[...] += _dot_nn(p.astype(BF16), do_b)
            ds = p * (_dot_nt(vv, do_b) - delta_ref[qi])
            dk_scr[...] += _dot_nn(ds.astype(BF16), qv)

        step(kj, True)

        def loop_body(qi, carry):
            step(qi, False)
            return carry

        lax.fori_loop(kj + 1, nk, loop_body, 0)
        dk_ref[...] = dk_scr[...].astype(dk_ref.dtype)
        dv_ref[...] = dv_scr[...].astype(dv_ref.dtype)

    stat = pl.BlockSpec((None, nk, 1, tk), lambda h, j: (h, 0, 0, 0))
    return pl.pallas_call(
        body, name=name, grid=(N_HEADS, nk),
        in_specs=[pl.BlockSpec((t, QK_DIM), lambda h, j: (0, h)),
                  pl.BlockSpec((tk, QK_DIM), lambda h, j: (j, h)),
                  pl.BlockSpec((tk, HEAD_DIM), lambda h, j: (j, h)),
                  stat, stat,
                  pl.BlockSpec((t, HEAD_DIM), lambda h, j: (0, do_col0 + h))],
        out_specs=[pl.BlockSpec((tk, QK_DIM), lambda h, j: (j, h)),
                   pl.BlockSpec((tk, HEAD_DIM), lambda h, j: (j, h))],
        out_shape=[jax.ShapeDtypeStruct((t, N_HEADS * QK_DIM), BF16),
                   jax.ShapeDtypeStruct((t, N_HEADS * HEAD_DIM), BF16)],
        scratch_shapes=[pltpu.VMEM((tk, QK_DIM), F32), pltpu.VMEM((tk, HEAD_DIM), F32)],
        compiler_params=_params(("arbitrary", "arbitrary")),
    )(q, k, v, lse_row, delta_row, dom)


def _rope_tables(pos_col, inv_freq_row, *, name):
    t = pos_col.shape[0]
    tm = min(ROW_TILE, t)

    def body(p_ref, f_ref, c_ref, s_ref):
        ang = p_ref[...].astype(F32) * f_ref[...]
        lane = _lane(ang.shape)
        c_ref[...] = jnp.where(lane < ROPE_DIM, jnp.cos(ang), 0.0)
        sn = jnp.sin(ang)
        s_ref[...] = jnp.where(lane < ROPE_DIM // 2, -sn, jnp.where(lane < ROPE_DIM, sn, 0.0))

    out = pl.BlockSpec((tm, LANES), lambda i: (i, 0))
    return pl.pallas_call(
        body, name=name, grid=(t // tm,),
        in_specs=[pl.BlockSpec((tm, 1), lambda i: (i, 0)), pl.BlockSpec((1, LANES), lambda i: (0, 0))],
        out_specs=[out, out], out_shape=[jax.ShapeDtypeStruct((t, LANES), F32)] * 2,
        compiler_params=_params(("arbitrary",)),
    )(pos_col, inv_freq_row)


def _loss_head(y, target):
    width = y.shape[1]

    def fn(yv, tv):
        e = yv - tv
        part = 0.5 * jnp.sum(jnp.mean(e * e, axis=1, keepdims=True), axis=0, keepdims=True)
        return e * (1.0 / width), jnp.broadcast_to(part, (HALO, LANES))

    return _rowwise(fn, [(y, width, 0), (target, width, 0)], [], [(width, F32)], [(HALO, LANES)], name="loss_head")


def _adamw(w, g, m, v, *, name):
    shape = w.shape
    w2, g2, m2, v2 = (a.reshape(-1, shape[-1]) for a in (w, g, m, v))
    rows, width = w2.shape
    tr = _divisor_tile(rows, max(8, (1 << 19) // max(width, 1)), 8)
    bc1 = 1.0 - ADAM_B1 ** ADAM_STEP
    bc2 = 1.0 - ADAM_B2 ** ADAM_STEP

    def body(w_ref, g_ref, m_ref, v_ref, d_ref, mo_ref, vo_ref):
        gv = g_ref[...]
        mn = ADAM_B1 * m_ref[...] + (1.0 - ADAM_B1) * gv
        vn = ADAM_B2 * v_ref[...] + (1.0 - ADAM_B2) * (gv * gv)
        d_ref[...] = -ADAM_LR * ((mn / bc1) / (jnp.sqrt(vn / bc2) + ADAM_EPS) + ADAM_WD * w_ref[...])
        mo_ref[...] = mn
        vo_ref[...] = vn

    spec = pl.BlockSpec((tr, width), lambda i: (i, 0))
    outs = pl.pallas_call(
        body, name=name, grid=(rows // tr,), in_specs=[spec] * 4, out_specs=[spec] * 3,
        out_shape=[jax.ShapeDtypeStruct((rows, width), F32)] * 3,
        compiler_params=_params(("arbitrary",)),
    )(w2, g2, m2, v2)
    return tuple(o.reshape(shape) for o in outs)


HBM_SPEC = pl.BlockSpec(memory_space=pltpu.HBM)


def _position():
    return lax.axis_index("x"), lax.axis_index("y"), lax.axis_index("c")


def _other_chips(x, y):
    return [(1 - x, y), (x, 1 - y), (1 - x, 1 - y)]


class _Stream:
    def __init__(self, kind, size=0):
        self.kind, self.size = kind, size
        self.parts = 2 if kind == "heads" else 1

    def local(self, ref, k, part):
        if self.kind == "rows":
            return ref.at[:, pl.ds(k * self.size, self.size), :]
        if self.kind == "cols":
            return ref.at[:, :, pl.ds(k * self.size, self.size)]
        if self.kind == "heads":
            return ref.at[:, :, pl.ds(part * N_HEADS * HEAD_DIM + k * HEAD_DIM, HEAD_DIM)]
        if self.kind == "piece":
            return ref.at[k]
        return ref

    def shard(self, ref, part):
        if self.kind == "heads":
            return ref.at[:, :, pl.ds(part * HEAD_DIM, HEAD_DIM)]
        return ref

    def half_local(self, ref, k, part, cc, hd):
        if self.kind == "piece":
            return ref.at[k, pl.ds(cc * hd, hd)]
        return self.local(ref.at[pl.ds(cc * hd, hd)], k, part)


def _remote(src, dst, send_sems, recv_sems, idx, to):
    return pltpu.make_async_remote_copy(src_ref=src, dst_ref=dst, send_sem=send_sems.at[idx],
                                        recv_sem=recv_sems.at[idx], device_id=to, device_id_type=MESH)


def _comm_call(body, ins, out_shapes, n_remote, n_local, *, name):
    scratch = [pltpu.SemaphoreType.DMA((n_remote,)), pltpu.SemaphoreType.DMA((n_remote,))]
    if n_local:
        scratch.append(pltpu.SemaphoreType.DMA((n_local,)))
    return pl.pallas_call(
        body, name=name, in_specs=[HBM_SPEC] * len(ins), out_specs=[HBM_SPEC] * len(out_shapes), out_shape=out_shapes,
        scratch_shapes=scratch, compiler_params=pltpu.CompilerParams(has_side_effects=True),
    )(*ins)


def _gather_chips(shards, streams, out_shapes, *, name):
    n = len(shards)
    hd = shards[0].shape[0] // 2
    flat = [(t, part) for t in range(n) for part in range(streams[t].parts)]
    ns = len(flat)

    def body(*refs):
        s_refs, o_refs = refs[:n], refs[n:2 * n]
        send_sems, recv_sems, local_sems = refs[2 * n:]
        x, y, c = _position()
        sibling = (x, y, 1 - c)
        chips = _other_chips(x, y)
        me = 2 * x + y
        own = []
        for s, (t, part) in enumerate(flat):
            st = streams[t]
            own.append(pltpu.make_async_copy(st.shard(s_refs[t], part), st.local(o_refs[t], me, part), local_sems.at[s]))
            own[-1].start()
        sent = []
        for s, (t, part) in enumerate(flat):
            st = streams[t]
            src = st.shard(s_refs[t].at[pl.ds(c * hd, hd)], part)
            for j, (cx, cy) in enumerate(chips):
                sent.append(_remote(src, st.half_local(o_refs[t], me, part, c, hd), send_sems, recv_sems,
                                    3 * s + j, (cx, cy, c)))
                sent[-1].start()
        for s, (t, part) in enumerate(flat):
            st = streams[t]
            for j, (cx, cy) in enumerate(chips):
                blk = st.half_local(o_refs[t], 2 * cx + cy, part, c, hd)
                _remote(blk, blk, send_sems, recv_sems, 3 * s + j, (x, y, c)).wait_recv()
                sent.append(_remote(blk, blk, send_sems, recv_sems, 3 * ns + 3 * s + j, sibling))
                sent[-1].start()
        for s, (t, part) in enumerate(flat):
            st = streams[t]
            for j, (cx, cy) in enumerate(chips):
                blk = st.half_local(o_refs[t], 2 * cx + cy, part, 1 - c, hd)
                _remote(blk, blk, send_sems, recv_sems, 3 * ns + 3 * s + j, (x, y, c)).wait_recv()
        for cp in sent:
            cp.wait_send()
        for cp in own:
            cp.wait()

    return _comm_call(body, shards, out_shapes, 6 * ns, ns, name=name)


def _sibling_take_other_half(gs, *, name):
    n = len(gs)
    hd = gs[0].shape[0] // 2

    def body(*refs):
        g_refs, o_refs = refs[:n], refs[n:2 * n]
        send_sems, recv_sems = refs[2 * n:]
        x, y, c = _position()
        copies = [_remote(g_refs[t].at[pl.ds((1 - c) * hd, hd)], o_refs[t], send_sems, recv_sems, t, (x, y, 1 - c))
                  for t in range(n)]
        for cp in copies:
            cp.start()
        for cp in copies:
            cp.wait()

    outs = [jax.ShapeDtypeStruct((hd,) + g.shape[1:], g.dtype) for g in gs]
    return _comm_call(body, gs, outs, n, 0, name=name)


def _chips_exchange(ps, streams, shard_shapes, *, name):
    n = len(ps)
    flat = [(t, part) for t in range(n) for part in range(streams[t].parts)]

    def body(*refs):
        p_refs, o_refs = refs[:n], refs[n:2 * n]
        send_sems, recv_sems = refs[2 * n:]
        x, y, c = _position()
        copies = []
        for s, (t, part) in enumerate(flat):
            st = streams[t]
            for j, (cx, cy) in enumerate(_other_chips(x, y)):
                copies.append(_remote(st.local(p_refs[t], 2 * cx + cy, part), st.shard(o_refs[t].at[j], part),
                                      send_sems, recv_sems, 3 * s + j, (cx, cy, c)))
        for cp in copies:
            cp.start()
        for cp in copies:
            cp.wait()

    outs = [jax.ShapeDtypeStruct((3,) + tuple(shp), p.dtype) for p, shp in zip(ps, shard_shapes)]
    return _comm_call(body, ps, outs, 3 * len(flat), 0, name=name)


def _sibling_join_halves(halves, *, name):
    n = len(halves)
    hd = halves[0].shape[0]

    def body(*refs):
        m_refs, o_refs = refs[:n], refs[n:2 * n]
        send_sems, recv_sems, local_sems = refs[2 * n:]
        x, y, c = _position()
        own, sent = [], []
        for t in range(n):
            own.append(pltpu.make_async_copy(m_refs[t], o_refs[t].at[pl.ds(c * hd, hd)], local_sems.at[t]))
            own[-1].start()
            sent.append(_remote(m_refs[t], o_refs[t].at[pl.ds(c * hd, hd)], send_sems, recv_sems, t, (x, y, 1 - c)))
            sent[-1].start()
        for t in range(n):
            _remote(m_refs[t], o_refs[t].at[pl.ds((1 - c) * hd, hd)], send_sems, recv_sems, t, (x, y, c)).wait_recv()
        for cp in sent:
            cp.wait_send()
        for cp in own:
            cp.wait()

    outs = [jax.ShapeDtypeStruct((2 * hd,) + h.shape[1:], h.dtype) for h in halves]
    return _comm_call(body, halves, outs, n, n, name=name)


def _row_tile(rows, width):
    return _divisor_tile(rows, max(16, (1 << 19) // width), 16)


def _add_own_half(g, got, c_idx, out_dtype, *, name):
    hd, r, w = got.shape
    tr = _row_tile(r, w)

    def body(c_ref, g_ref, a_ref, o_ref):
        o_ref[...] = (g_ref[...] + a_ref[...]).astype(o_ref.dtype)

    return pl.pallas_call(
        body, name=name,
        grid_spec=pltpu.PrefetchScalarGridSpec(
            num_scalar_prefetch=1, grid=(hd, r // tr),
            in_specs=[pl.BlockSpec((None, None, tr, w), lambda l, i, c_ref: (c_ref[0], l, i, 0)),
                      pl.BlockSpec((None, tr, w), lambda l, i, c_ref: (l, i, 0))],
            out_specs=pl.BlockSpec((None, tr, w), lambda l, i, c_ref: (l, i, 0))),
        out_shape=jax.ShapeDtypeStruct((hd, r, w), out_dtype),
        compiler_params=_params(("arbitrary", "arbitrary")),
    )(c_idx, g.reshape((2, hd) + g.shape[1:]), got)


def _sum_chips(p, got, chip_idx, stream, *, name):
    _, hd, rs, cs = got.shape
    wb = HEAD_DIM if stream.kind == "heads" else cs
    tr = _row_tile(rs, wb)
    kind, size = stream.kind, stream.size

    def own_index(l, i, g, k_ref):
        k = k_ref[0]
        if kind == "rows":
            return (l, k * (size // tr) + i, 0)
        if kind == "cols":
            return (l, i, k)
        if kind == "heads":
            return (l, i, g * N_HEADS + k)
        if kind == "piece":
            return (k, l, i, 0)
        return (l, i, 0)

    own_blk = (None, None, tr, wb) if kind == "piece" else (None, tr, wb)

    def body(k_ref, p_ref, fx_ref, fy_ref, fxy_ref, o_ref):
        f = lambda r: r[...].astype(F32)
        o_ref[...] = (f(p_ref) + f(fy_ref)) + (f(fx_ref) + f(fxy_ref))

    def rel(j):
        return pl.BlockSpec((None, None, tr, wb), functools.partial(lambda l, i, g, k_ref, j: (j, l, i, g), j=j))

    return pl.pallas_call(
        body, name=name,
        grid_spec=pltpu.PrefetchScalarGridSpec(
            num_scalar_prefetch=1, grid=(hd, rs // tr, stream.parts),
            in_specs=[pl.BlockSpec(own_blk, own_index), rel(0), rel(1), rel(2)],
            out_specs=pl.BlockSpec((None, tr, wb), lambda l, i, g, k_ref: (l, i, g))),
        out_shape=jax.ShapeDtypeStruct((hd, rs, cs), F32),
        compiler_params=_params(("arbitrary", "arbitrary", "arbitrary")),
    )(chip_idx, p, got, got, got)


MATRICES = ("w_in", "w_uq", "w_ukv", "w_out", "w_gate_up", "w_down", "w_ple", "w_ple_gate", "conv_w")
VECTORS = ("a_log", "dt_bias", "gdn_norm_g", "q_norm_g", "kv_norm_g", "ln1_g", "ln1_b", "ln2_g", "ln2_b")
WEIGHTS = ("w_in", "conv_w", "a_log", "dt_bias", "gdn_norm_g", "q_norm_g", "w_uq", "kv_norm_g", "w_ukv", "w_out",
           "ln1_g", "ln1_b", "w_gate_up", "w_down", "ln2_g", "ln2_b", "w_ple", "w_ple_gate")
ROW_SHARDED = ("w_out", "w_down", "w_ple_gate")
N_CHIPS = 4


def _stream_of(name, shard_shape):
    if name in ("w_in", "w_uq"):
        return _Stream("piece")
    if name == "w_ukv":
        return _Stream("heads")
    if name in ROW_SHARDED:
        return _Stream("rows", shard_shape[0])
    return _Stream("cols", shard_shape[1])


def _pack_vectors(vecs, depth):
    flat = jnp.concatenate([vecs[n].reshape(depth, -1) for n in VECTORS], axis=1)
    pad = jnp.zeros((depth, VEC_ROWS * LANES - flat.shape[1]), F32)
    return jnp.concatenate([flat, pad], axis=1).reshape(depth, VEC_ROWS, LANES)


def _unpack_vectors(packed, shapes):
    depth = packed.shape[0]
    flat = packed.reshape(depth, VEC_ROWS * LANES)
    out, off = {}, 0
    for n in VECTORS:
        out[n] = flat[:, off:off + shapes[n][1]]
        off += shapes[n][1]
    return out


VEC_ROWS = 40


class _Dims:
    def __init__(self, d_model, in_width, q_lora, kv_lora, d_ff2, ple_dim):
        self.d = d_model
        self.hw = N_HEADS * HEAD_DIM
        self.in_width = in_width
        self.q_lora, self.kv_lora = q_lora, kv_lora
        self.ff2 = d_ff2
        self.ple = ple_dim
        self.c_kv0 = 4 * self.hw
        self.c_q0 = self.c_kv0 + kv_lora
        self.misc0 = self.c_q0 + q_lora
        self.h_width = self.misc0 + LANES
        assert self.c_kv0 % kv_lora == 0 and self.c_q0 % q_lora == 0 and self.misc0 % LANES == 0
        self.g_beta = 4 * self.hw
        self.g_a = self.g_beta + N_HEADS
        self.g_cq = self.g_a + N_HEADS
        self.g_ckv = self.g_cq + q_lora
        self.g_kr = self.g_ckv + kv_lora
        assert self.g_kr + ROPE_DIM == in_width

    def w_in_local(self, w):
        pad = jnp.zeros(w.shape[:-1] + (self.h_width - self.in_width,), w.dtype)
        return jnp.concatenate([w[..., :self.g_beta], w[..., self.g_ckv:self.g_kr], w[..., self.g_cq:self.g_ckv],
                                w[..., self.g_kr:], w[..., self.g_beta:self.g_cq], pad], axis=-1)

    def w_in_global(self, d):
        m = self.misc0
        return jnp.concatenate([d[..., :self.c_kv0], d[..., m + MISC_BETA0:m + MISC_A0 + N_HEADS],
                                d[..., self.c_q0:self.misc0], d[..., self.c_kv0:self.c_q0], d[..., m:m + ROPE_DIM]],
                               axis=-1)

    def w_uq_local(self, w):
        r = w.reshape(w.shape[:-1] + (N_HEADS, HEAD_DIM + ROPE_DIM))
        r = jnp.pad(r, [(0, 0)] * (r.ndim - 1) + [(0, QK_DIM - HEAD_DIM - ROPE_DIM)])
        return r.reshape(w.shape[:-1] + (N_HEADS * QK_DIM,))

    def w_uq_global(self, d):
        r = d.reshape(d.shape[:-1] + (N_HEADS, QK_DIM))[..., :HEAD_DIM + ROPE_DIM]
        return r.reshape(d.shape[:-1] + (N_HEADS * (HEAD_DIM + ROPE_DIM),))


def _lane_row(vec, lane0):
    pad = LANES - lane0 - vec.shape[0]
    return jnp.concatenate([jnp.zeros((lane0,), F32), vec.astype(F32), jnp.zeros((pad,), F32)])[None, :]


def _layer_fwd(dm, alpha, x, xb, p_i, cos_t, sin_t, wl, tag):
    d, hw = dm.d, dm.hw
    nm = lambda s: f"{s}_{tag}"
    mm = functools.partial(_matmul, layer=wl["layer"])
    h = mm(xb, wl["w_in"], dims="nn", name=nm("f_in"), tn=_divisor_tile(dm.h_width, 1408, LANES))
    misc_cb = dm.misc0 // LANES

    u = _conv_fwd(h, wl["conv_w"], 3 * hw, name=nm("f_conv"))
    qn, kn, vg, gb = _rowwise(_gdn_act, [(u, 3 * hw, 0), (h, LANES, misc_cb)], [wl["alog_row"], wl["dtb_row"]],
                              [(hw, F32), (hw, F32), (hw, F32), (LANES, F32)], name=nm("f_gdn_act"))
    prep = _gdn_prep_fwd(qn, kn, vg, gb, name=nm("f_gdn_prep"))
    o_gdn, s_all = _gdn_seq_fwd(prep, name=nm("f_gdn_seq"))
    (og,) = _rowwise(lambda o, z, g: (_gdn_out(o, z, g),), [(o_gdn, hw, 0), (h, hw, 3)], [wl["gn_row"]],
                     [(hw, BF16)], name=nm("f_gdn_out"))

    cqn, ckvn = _rowwise(_mla_norm, [(h, dm.kv_lora, dm.c_kv0 // dm.kv_lora), (h, dm.q_lora, dm.c_q0 // dm.q_lora)],
                         [wl["kvg_row"], wl["qg_row"]], [(dm.q_lora, BF16), (dm.kv_lora, BF16)], name=nm("f_mla_norm"))
    qm = mm(cqn, wl["w_uq"], dims="nn", name=nm("f_uq"))
    kvm = mm(ckvn, wl["w_ukv"], dims="nn", name=nm("f_ukv"))
    scale = (HEAD_DIM + ROPE_DIM) ** -0.5
    qk_fn = functools.partial(_mla_qk, scale)
    qa, ka, va = _rowwise(qk_fn, [(qm, N_HEADS * QK_DIM, 0), (kvm, 2 * hw, 0), (h, LANES, misc_cb),
                                  (cos_t, LANES, 0), (sin_t, LANES, 0)], [],
                          [(N_HEADS * QK_DIM, BF16), (N_HEADS * QK_DIM, BF16), (hw, BF16)], name=nm("f_mla_qk"))
    o_mla, lse = _attn_fwd(qa, ka, va, name=nm("f_attn"))

    om = jnp.concatenate([og, o_mla], axis=1)
    mix = mm(om, wl["w_out"], dims="nn", name=nm("f_out"))
    ln1 = lambda xv, yv, g, b: (_layer_norm(alpha * xv + yv, g, b),) * 2
    x1, x1b = _rowwise(ln1, [(x, d, 0), (mix, d, 0)], [wl["ln1_g"], wl["ln1_b"]], [(d, F32), (d, BF16)], name=nm("f_ln1"))

    gu = mm(x1b, wl["w_gate_up"], dims="nn", name=nm("f_gate_up"))
    (act,) = _rowwise(lambda g_: (_swiglu(g_),), [(gu, dm.ff2, 0)], [], [(dm.ff2 // 2, BF16)], name=nm("f_swiglu"))
    dn = mm(act, wl["w_down"], dims="nn", name=nm("f_down"))
    x2, x2b = _rowwise(ln1, [(x1, d, 0), (dn, d, 0)], [wl["ln2_g"], wl["ln2_b"]], [(d, F32), (d, BF16)], name=nm("f_ln2"))

    pg = mm(x2b, wl["w_ple_gate"], dims="nn", name=nm("f_ple_gate"))
    pe = mm(p_i, wl["w_ple"], dims="nn", name=nm("f_ple"))
    out, outb = _rowwise(lambda a, b, c_: (_ple_out(a, b, c_),) * 2, [(x2, d, 0), (pg, d, 0), (pe, d, 0)], [],
                         [(d, F32), (d, BF16)], name=nm("f_ple_out"))
    saved = dict(x=x, xb=xb, p_i=p_i, h=h, u=u, qn=qn, kn=kn, vg=vg, gb=gb, prep=prep, s_all=s_all, o_gdn=o_gdn, cqn=cqn, ckvn=ckvn,
                 qm=qm, kvm=kvm, qa=qa, ka=ka, va=va, o_mla=o_mla, lse=lse, om=om, mix=mix, x1=x1, x1b=x1b, gu=gu,
                 act=act, dn=dn, x2=x2, x2b=x2b, pg=pg, pe=pe)
    return out, outb, saved


def _layer_bwd(dm, alpha, dout, sv, cos_t, sin_t, wl, gbuf, tag):
    d, hw = dm.d, dm.hw
    t = dout.shape[0]
    nm = lambda s: f"{s}_{tag}"
    gr = {}
    gbuf = dict(gbuf)
    misc_cb = dm.misc0 // LANES
    mm = functools.partial(_matmul, layer=wl["layer"])

    def wgrad(name_, a, g):
        gbuf[name_] = mm(a, g, dims="tn", name=nm("b_" + name_), into=gbuf[name_], tm=1408, tn=1408, tk=1024)

    dx2_a, dpg, dpe = _rowwise(_vjp_fn(_ple_out, 3, 1), [(sv["x2"], d, 0), (sv["pg"], d, 0), (sv["pe"], d, 0), (dout, d, 0)],
                               [], [(d, F32), (d, BF16), (d, BF16)], name=nm("b_ple_out"))
    wgrad("w_ple", sv["p_i"], dpe)
    wgrad("w_ple_gate", sv["x2b"], dpg)
    dx2 = mm(dpg, wl["w_ple_gate"], dims="nt", c=dx2_a, name=nm("b_x2"))

    def ln_bwd(xv, yv, ct, g, b):
        _, pull = jax.vjp(lambda a_, b_, c_, d_: _layer_norm(alpha * a_ + b_, c_, d_), xv, yv, g, b)
        return pull(ct)

    dx1_a, ddn, gr["ln2_g"], gr["ln2_b"] = _rowwise(
        ln_bwd, [(sv["x1"], d, 0), (sv["dn"], d, 0), (dx2, d, 0)], [wl["ln2_g"], wl["ln2_b"]],
        [(d, F32), (d, BF16)], [(1, d), (1, d)], name=nm("b_ln2"))
    wgrad("w_down", sv["act"], ddn)
    dact = mm(ddn, wl["w_down"], dims="nt", name=nm("b_act"), tn=1408)
    (dgu,) = _rowwise(_vjp_fn(_swiglu, 1, 1), [(sv["gu"], dm.ff2, 0), (dact, dm.ff2 // 2, 0)], [], [(dm.ff2, BF16)],
                      name=nm("b_swiglu"))
    wgrad("w_gate_up", sv["x1b"], dgu)
    dx1 = mm(dgu, wl["w_gate_up"], dims="nt", c=dx1_a, name=nm("b_x1"))

    dx_a, dmix, gr["ln1_g"], gr["ln1_b"] = _rowwise(
        ln_bwd, [(sv["x"], d, 0), (sv["mix"], d, 0), (dx1, d, 0)], [wl["ln1_g"], wl["ln1_b"]],
        [(d, F32), (d, BF16)], [(1, d), (1, d)], name=nm("b_ln1"))
    wgrad("w_out", sv["om"], dmix)
    dom = mm(dmix, wl["w_out"], dims="nt", name=nm("b_om"))

    nq = t // min(ATT_TILE, t)
    dqa, delta = _attn_bwd_dq(sv["qa"], sv["ka"], sv["va"], sv["o_mla"], sv["lse"], dom, hw // HEAD_DIM, name=nm("b_attn_dq"))
    lse_row = sv["lse"].reshape(N_HEADS, nq, 1, t // nq)
    delta_row = delta.reshape(N_HEADS, nq, 1, t // nq)
    dka, dva = _attn_bwd_dkv(sv["qa"], sv["ka"], sv["va"], lse_row, delta_row, dom, hw // HEAD_DIM, name=nm("b_attn_dkv"))
    scale = (HEAD_DIM + ROPE_DIM) ** -0.5
    qk_fn = functools.partial(_mla_qk, scale)

    def qk_bwd(qm, kvm, misc, cs, sn, g_q, g_k, g_v):
        _, pull = jax.vjp(lambda a, b, c_: qk_fn(a, b, c_, cs, sn), qm, kvm, misc)
        return pull((g_q, g_k, g_v))

    dqm, dkvm, dmisc_rope = _rowwise(
        qk_bwd, [(sv["qm"], N_HEADS * QK_DIM, 0), (sv["kvm"], 2 * hw, 0), (sv["h"], LANES, misc_cb), (cos_t, LANES, 0),
                 (sin_t, LANES, 0), (dqa, N_HEADS * QK_DIM, 0), (dka, N_HEADS * QK_DIM, 0), (dva, hw, 0)], [],
        [(N_HEADS * QK_DIM, BF16), (2 * hw, BF16), (LANES, F32)], name=nm("b_mla_qk"))
    wgrad("w_uq", sv["cqn"], dqm)
    wgrad("w_ukv", sv["ckvn"], dkvm)
    dcqn = mm(dqm, wl["w_uq"], dims="nt", name=nm("b_cqn"))
    dckvn = mm(dkvm, wl["w_ukv"], dims="nt", name=nm("b_ckvn"))

    def norm_bwd(ckv, cq, g_q, g_kv, kvg, qg):
        _, pull = jax.vjp(_mla_norm, ckv, cq, kvg, qg)
        return pull((g_q, g_kv))

    dckv, dcq, gr["kvg_row"], gr["qg_row"] = _rowwise(
        norm_bwd, [(sv["h"], dm.kv_lora, dm.c_kv0 // dm.kv_lora), (sv["h"], dm.q_lora, dm.c_q0 // dm.q_lora),
                   (dcqn, dm.q_lora, 0), (dckvn, dm.kv_lora, 0)], [wl["kvg_row"], wl["qg_row"]],
        [(dm.kv_lora, BF16), (dm.q_lora, BF16)], [(1, dm.kv_lora), (1, dm.q_lora)], name=nm("b_mla_norm"))

    def gout_bwd(o, z, g_o, gn):
        _, pull = jax.vjp(_gdn_out, o, z, gn)
        return pull(g_o)

    do_gdn, dz, gr["gn_row"] = _rowwise(gout_bwd, [(sv["o_gdn"], hw, 0), (sv["h"], hw, 3), (dom, hw, 0)], [wl["gn_row"]],
                                        [(hw, F32), (hw, BF16)], [(1, HEAD_DIM)], name=nm("b_gdn_out"))
    dprep = _gdn_seq_bwd(sv["prep"], sv["s_all"], do_gdn, name=nm("b_gdn_seq"))
    dqn, dkn, dvg, dgb = _gdn_prep_bwd(sv["qn"], sv["kn"], sv["vg"], sv["gb"], dprep, name=nm("b_gdn_prep"))

    def act_bwd(u, misc, g_q, g_k, g_v, g_gb, g_rope, alog, dtb):
        _, pull = jax.vjp(_gdn_act, u, misc, alog, dtb)
        du_, dmisc_, dalog_, ddtb_ = pull((g_q, g_k, g_v, g_gb))
        return du_, dmisc_ + g_rope, dalog_, ddtb_

    du, dmisc, gr["alog_row"], gr["dtb_row"] = _rowwise(
        act_bwd, [(sv["u"], 3 * hw, 0), (sv["h"], LANES, misc_cb), (dqn, hw, 0), (dkn, hw, 0), (dvg, hw, 0),
                  (dgb, LANES, 0), (dmisc_rope, LANES, 0)], [wl["alog_row"], wl["dtb_row"]],
        [(3 * hw, F32), (LANES, BF16)], [(1, LANES), (1, LANES)], name=nm("b_gdn_act"))
    dqkv, dconv = _conv_bwd(du, sv["h"], wl["conv_w"], 3 * hw, name=nm("b_conv"))
    gr["conv_w"] = dconv[:CONV_W]

    dh = jnp.concatenate([dqkv, dz, dckv, dcq, dmisc], axis=1)
    wgrad("w_in", sv["xb"], dh)
    dx = mm(dh, wl["w_in"], dims="nt", c=dx_a, name=nm("b_x"), tk=1408)
    return dx, gbuf, gr


LOCAL_MATRICES = ("w_in", "w_uq", "w_ukv", "w_out", "w_gate_up", "w_down", "w_ple", "w_ple_gate")


def _layer_weights(mats, vecs, layer):
    wl = {n: mats[n] for n in LOCAL_MATRICES}
    wl["layer"] = layer
    wl["conv_w"] = mats["conv_w"][layer]
    wl["alog_row"] = _lane_row(vecs["a_log"][layer], MISC_A0)
    wl["dtb_row"] = _lane_row(vecs["dt_bias"][layer], MISC_A0)
    wl["gn_row"] = vecs["gdn_norm_g"][layer][None, :]
    wl["qg_row"] = vecs["q_norm_g"][layer][None, :]
    wl["kvg_row"] = vecs["kv_norm_g"][layer][None, :]
    for n in ("ln1_g", "ln1_b", "ln2_g", "ln2_b"):
        wl[n] = vecs[n][layer][None, :]
    return wl


def _vector_grads(gr):
    out = {"a_log": gr["alog_row"][0, MISC_A0:MISC_A0 + N_HEADS], "dt_bias": gr["dtb_row"][0, MISC_A0:MISC_A0 + N_HEADS],
           "gdn_norm_g": gr["gn_row"][0], "q_norm_g": gr["qg_row"][0], "kv_norm_g": gr["kvg_row"][0]}
    for n in ("ln1_g", "ln1_b", "ln2_g", "ln2_b"):
        out[n] = gr[n][0]
    return out


def _local_step(dm, x, p, positions, target, mats, vecs):
    depth = p.shape[0]
    alpha = (2.0 * depth) ** 0.25
    freq = ROPE_THETA ** (-jnp.arange(0, ROPE_DIM, 2, dtype=F32) / ROPE_DIM)
    inv_freq_row = _lane_row(jnp.concatenate([freq, freq]), 0)
    cos_t, sin_t = _rope_tables(positions.reshape(-1, 1), inv_freq_row, name="rope_tables")

    wls = [_layer_weights(mats, vecs, i) for i in range(depth)]
    saved = []
    cur, cur_b = x, x
    for i in range(depth):
        cur, cur_b, sv = _layer_fwd(dm, alpha, cur, cur_b, p[i], cos_t, sin_t, wls[i], f"l{i}")
        saved.append(sv)
    dy, loss_blk = _loss_head(cur, target)
    gbuf = {n: jnp.zeros(mats[n].shape, F32) for n in LOCAL_MATRICES}
    conv_g, vec_g = [None] * depth, [None] * depth
    for i in reversed(range(depth)):
        dy, gbuf, gr = _layer_bwd(dm, alpha, dy, saved[i], cos_t, sin_t, wls[i], gbuf, f"l{i}")
        conv_g[i] = gr["conv_w"]
        vec_g[i] = _vector_grads(gr)
    vec_grads = {n: jnp.stack([vec_g[i][n] for i in range(depth)]) for n in VECTORS}
    return loss_blk[0, 0], dy, gbuf, jnp.stack(conv_g), vec_grads


def kernel(x, p, positions, w_in, conv_w, a_log, dt_bias, gdn_norm_g, q_norm_g, w_uq, kv_norm_g, w_ukv, w_out, ln1_g, ln1_b, w_gate_up, w_down, ln2_g, ln2_b, w_ple, w_ple_gate, loss_target, m_w_in, m_conv_w, m_a_log, m_dt_bias, m_gdn_norm_g, m_q_norm_g, m_w_uq, m_kv_norm_g, m_w_ukv, m_w_out, m_ln1_g, m_ln1_b, m_w_gate_up, m_w_down, m_ln2_g, m_ln2_b, m_w_ple, m_w_ple_gate, v_w_in, v_conv_w, v_a_log, v_dt_bias, v_gdn_norm_g, v_q_norm_g, v_w_uq, v_kv_norm_g, v_w_ukv, v_w_out, v_ln1_g, v_ln1_b, v_w_gate_up, v_w_down, v_ln2_g, v_ln2_b, v_w_ple, v_w_ple_gate):
    w = dict(w_in=w_in, conv_w=conv_w, a_log=a_log, dt_bias=dt_bias, gdn_norm_g=gdn_norm_g, q_norm_g=q_norm_g, w_uq=w_uq,
             kv_norm_g=kv_norm_g, w_ukv=w_ukv, w_out=w_out, ln1_g=ln1_g, ln1_b=ln1_b, w_gate_up=w_gate_up, w_down=w_down,
             ln2_g=ln2_g, ln2_b=ln2_b, w_ple=w_ple, w_ple_gate=w_ple_gate)
    m = dict(w_in=m_w_in, conv_w=m_conv_w, a_log=m_a_log, dt_bias=m_dt_bias, gdn_norm_g=m_gdn_norm_g, q_norm_g=m_q_norm_g,
             w_uq=m_w_uq, kv_norm_g=m_kv_norm_g, w_ukv=m_w_ukv, w_out=m_w_out, ln1_g=m_ln1_g, ln1_b=m_ln1_b,
             w_gate_up=m_w_gate_up, w_down=m_w_down, ln2_g=m_ln2_g, ln2_b=m_ln2_b, w_ple=m_w_ple, w_ple_gate=m_w_ple_gate)
    v = dict(w_in=v_w_in, conv_w=v_conv_w, a_log=v_a_log, dt_bias=v_dt_bias, gdn_norm_g=v_gdn_norm_g, q_norm_g=v_q_norm_g,
             w_uq=v_w_uq, kv_norm_g=v_kv_norm_g, w_ukv=v_w_ukv, w_out=v_w_out, ln1_g=v_ln1_g, ln1_b=v_ln1_b,
             w_gate_up=v_w_gate_up, w_down=v_w_down, ln2_g=v_ln2_g, ln2_b=v_ln2_b, w_ple=v_w_ple, w_ple_gate=v_w_ple_gate)
    depth = w_in.shape[0]
    assert depth % 2 == 0
    hd = depth // 2
    dm = _Dims(x.shape[2], N_CHIPS * w_in.shape[2], w_uq.shape[1], w_ukv.shape[1], N_CHIPS * w_gate_up.shape[2], p.shape[3])
    cx, cy, cc = lax.axis_index("x"), lax.axis_index("y"), lax.axis_index("c")
    chip = 2 * cx + cy

    shards = [w[n] if n == "conv_w" else w[n].astype(BF16) for n in MATRICES]
    g_streams = [_stream_of(n, w[n].shape[1:]) for n in MATRICES]
    g_shapes = []
    for s, st in zip(shards, g_streams):
        if st.kind == "piece":
            shape = (N_CHIPS,) + s.shape
        elif st.kind == "rows":
            shape = (depth, N_CHIPS * s.shape[1], s.shape[2])
        else:
            shape = (depth, s.shape[1], N_CHIPS * s.shape[2])
        g_shapes.append(jax.ShapeDtypeStruct(shape, s.dtype))
    mats = dict(zip(MATRICES, _gather_chips(shards, g_streams, g_shapes, name="gather_weights")))
    for n, to_local in (("w_in", dm.w_in_local), ("w_uq", dm.w_uq_local)):
        pieces = jnp.moveaxis(mats[n], 0, 2)
        mats[n] = to_local(pieces.reshape(pieces.shape[:2] + (-1,)))
    vecs = {n: w[n] for n in VECTORS}

    loss_local, grad_x, gbuf, conv_g, vec_g = _local_step(dm, x[0], p[:, 0], positions[0], loss_target[0], mats, vecs)
    loss = lax.psum(loss_local, ("x", "y", "c"))

    names = list(LOCAL_MATRICES) + ["conv_w", "vectors"]
    gs = [gbuf[n] for n in LOCAL_MATRICES] + [conv_g, _pack_vectors(vec_g, depth)]
    wire = [BF16] * len(LOCAL_MATRICES) + [F32, F32]
    r_streams = [_stream_of(n, w[n].shape[1:]) for n in LOCAL_MATRICES]
    r_streams += [_stream_of("conv_w", w["conv_w"].shape[1:]), _Stream("whole")]
    shard_shapes = [(hd,) + w[n].shape[1:] for n in LOCAL_MATRICES] + [(hd,) + w["conv_w"].shape[1:], (hd, VEC_ROWS, LANES)]
    c_idx = cc.reshape(1).astype(jnp.int32)
    chip_idx = chip.reshape(1).astype(jnp.int32)
    from_sibling = _sibling_take_other_half(gs, name="reduce_sibling")
    chip_sum = [_add_own_half(g, a, c_idx, dt, name=f"reduce_add_{n}")
                for g, a, dt, n in zip(gs, from_sibling, wire, names)]
    for i, n in enumerate(names):
        if r_streams[i].kind == "piece":
            glob = dm.w_in_global(chip_sum[i]) if n == "w_in" else dm.w_uq_global(chip_sum[i])
            glob = glob.reshape(glob.shape[:2] + (N_CHIPS, glob.shape[2] // N_CHIPS))
            chip_sum[i] = jnp.moveaxis(glob, 2, 0)
    from_chips = _chips_exchange(chip_sum, r_streams, shard_shapes, name="reduce_chips")
    halves = [_sum_chips(ps, got, chip_idx, st, name=f"reduce_sum_{n}")
              for ps, got, st, n in zip(chip_sum, from_chips, r_streams, names)]
    joined = dict(zip(names, _sibling_join_halves(halves, name="reduce_join")))
    joined.update(_unpack_vectors(joined.pop("vectors"), {n: w[n].shape for n in VECTORS}))

    grad_w, delta_w, new_m, new_v = {}, {}, {}, {}
    for n in WEIGHTS:
        grad_w[n] = joined[n]
        delta_w[n], new_m[n], new_v[n] = _adamw(w[n], grad_w[n], m[n], v[n], name=f"adamw_{n}")
    return (loss, grad_x[None], *[grad_w[n] for n in WEIGHTS], *[delta_w[n] for n in WEIGHTS],
            *[new_m[n] for n in WEIGHTS], *[new_v[n] for n in WEIGHTS])
```

```python
import functools

import jax
import jax.numpy as jnp
from jax import lax
from jax.experimental import pallas as pl
from jax.experimental.pallas import tpu as pltpu

F32 = jnp.float32
BF16 = jnp.bfloat16
HIGH = lax.Precision.HIGH
MESH = pl.DeviceIdType.MESH

CHUNK = 64
N_HEADS = 4
HEAD_DIM = 128
ROPE_DIM = 64
ROPE_THETA = 10000.0
LN_EPS = 1e-5
RMS_EPS = 1e-6
ADAM_LR, ADAM_B1, ADAM_B2, ADAM_EPS, ADAM_WD, ADAM_STEP = 0.001, 0.9, 0.999, 1e-08, 0.01, 10

LANES = 128
VMEM_LIMIT = 48 * 1024 * 1024
PACK_W = 512
ROW_TILE = 256
SUB_ROWS = 16
MAX_SUB_ROWS = 64
VREG_FILE_ELEMS = 64 * 8 * LANES

MISC_BETA0 = ROPE_DIM
MISC_A0 = ROPE_DIM + N_HEADS

NN = (((1,), (0,)), ((), ()))
NT = (((1,), (1,)), ((), ()))
TN = (((0,), (0,)), ((), ()))


def _params(sem=None):
    return pltpu.CompilerParams(dimension_semantics=sem, vmem_limit_bytes=VMEM_LIMIT)


def _divisor_tile(dim, target, unit):
    best = None
    t = unit
    while t <= min(dim, target):
        if dim % t == 0:
            best = t
        t += unit
    return best if best is not None else dim


BATCHED = {NN: (((2,), (1,)), ((0,), (0,))), NT: (((2,), (2,)), ((0,), (0,))), TN: (((1,), (1,)), ((0,), (0,)))}


def _make_dots(high_precision):
    def raw(a, b, dims):
        if a.ndim == 3:
            dims = BATCHED[dims]
        if high_precision:
            return lax.dot_general(a, b, dims, precision=HIGH, preferred_element_type=F32)
        return lax.dot_general(a.astype(BF16), b.astype(BF16), dims, preferred_element_type=F32)

    @jax.custom_vjp
    def nn(a, b):
        return raw(a, b, NN)

    @jax.custom_vjp
    def nt(a, b):
        return raw(a, b, NT)

    @jax.custom_vjp
    def tn(a, b):
        return raw(a, b, TN)

    nn.defvjp(lambda a, b: (raw(a, b, NN), (a, b)), lambda r, g: (nt(g, r[1]), tn(r[0], g)))
    nt.defvjp(lambda a, b: (raw(a, b, NT), (a, b)), lambda r, g: (nn(g, r[1]), tn(g, r[0])))
    tn.defvjp(lambda a, b: (raw(a, b, TN), (a, b)), lambda r, g: (nt(r[1], g), nn(r[0], g)))
    return nn, nt, tn


_nn, _nt, _tn = _make_dots(False)
_hnn, _hnt, _htn = _make_dots(True)


def _matmul(a, b, *, dims, name, c=None, out_dtype=F32, tm=1024, tn=512, tk=1408, layer=None, into=None):
    b_shape = b.shape[-2:]
    if dims == "nn":
        (m, k), (k2, n) = a.shape, b_shape
    elif dims == "nt":
        (m, k), (n, k2) = a.shape, b_shape
    else:
        (k, m), (k2, n) = a.shape, b_shape
    assert k == k2, (a.shape, b.shape, dims)
    tm = _divisor_tile(m, tm, LANES)
    tn = _divisor_tile(n, tn, LANES)
    tk = _divisor_tile(k, tk, LANES)
    nk = k // tk
    dn = {"nn": NN, "nt": NT, "tn": TN}[dims]
    if dims == "tn":
        a_spec = pl.BlockSpec((tk, tm), lambda i, j, kk: (kk, i))
    else:
        a_spec = pl.BlockSpec((tm, tk), lambda i, j, kk: (i, kk))
    b_blk, b_idx = ((tn, tk), lambda i, j, kk: (j, kk)) if dims == "nt" else ((tk, tn), lambda i, j, kk: (kk, j))
    if b.ndim == 3:
        b_spec = pl.BlockSpec((None,) + b_blk, lambda i, j, kk: (layer,) + b_idx(i, j, kk))
    else:
        b_spec = pl.BlockSpec(b_blk, b_idx)
    c_spec = pl.BlockSpec((tm, tn), lambda i, j, kk: (i, j))
    if into is not None:
        assert into.shape[1:] == (m, n) and into.dtype == out_dtype
        o_spec = pl.BlockSpec((None, tm, tn), lambda i, j, kk: (layer, i, j))
        out_shape = jax.ShapeDtypeStruct(into.shape, into.dtype)
    else:
        o_spec = c_spec
        out_shape = jax.ShapeDtypeStruct((m, n), out_dtype)
    has_c = c is not None

    def body(*refs):
        a_ref, b_ref = refs[:2]
        c_ref = refs[2] if has_c else None
        o_ref, acc_ref = refs[-2:]
        kk = pl.program_id(2)

        @pl.when(kk == 0)
        def _():
            if has_c:
                acc_ref[...] = c_ref[...].astype(F32)
            else:
                acc_ref[...] = jnp.zeros_like(acc_ref)

        acc_ref[...] += lax.dot_general(a_ref[...].astype(BF16), b_ref[...].astype(BF16), dn,
                                        preferred_element_type=F32)

        @pl.when(kk == nk - 1)
        def _():
            o_ref[...] = acc_ref[...].astype(o_ref.dtype)

    ins = [a, b] + ([c] if has_c else [])
    specs = [a_spec, b_spec] + ([c_spec] if has_c else [])
    aliases = {}
    if into is not None:
        aliases = {len(ins): 0}
        ins.append(into)
        specs.append(pl.BlockSpec(memory_space=pl.ANY))
    return pl.pallas_call(
        body, name=name, grid=(m // tm, n // tn, nk), in_specs=specs, out_specs=o_spec, out_shape=out_shape,
        scratch_shapes=[pltpu.VMEM((tm, tn), F32)], input_output_aliases=aliases,
        compiler_params=_params(("arbitrary", "arbitrary", "arbitrary")),
    )(*ins)


def _rowwise(fn, rows, params, outs, accs=(), *, name, tm=ROW_TILE):
    t = rows[0][0].shape[0]
    tm = min(tm, t)
    widest = max([w for _, w, _ in rows] + [w for w, _ in outs])
    sub = SUB_ROWS
    while sub < MAX_SUB_ROWS and 2 * sub * widest <= VREG_FILE_ELEMS:
        sub *= 2
    assert t % tm == 0 and tm % sub == 0
    n_rows, n_par, n_out, n_acc = len(rows), len(params), len(outs), len(accs)

    def body(*refs):
        row_refs = refs[:n_rows]
        par_refs = refs[n_rows:n_rows + n_par]
        out_refs = refs[n_rows + n_par:n_rows + n_par + n_out]
        acc_refs = refs[n_rows + n_par + n_out:]
        if n_acc:
            @pl.when(pl.program_id(0) == 0)
            def _():
                for a_ref in acc_refs:
                    a_ref[...] = jnp.zeros_like(a_ref)

        def step(r, carry):
            sl = pl.ds(pl.multiple_of(r * sub, sub), sub)
            vals = [ref[sl, :].astype(F32) for ref in row_refs] + [ref[...] for ref in par_refs]
            res = fn(*vals)
            for o_ref, val in zip(out_refs, res[:n_out]):
                o_ref[sl, :] = val.astype(o_ref.dtype)
            for a_ref, val in zip(acc_refs, res[n_out:]):
                a_ref[...] += val
            return carry

        lax.fori_loop(0, tm // sub, step, 0)

    in_specs = [pl.BlockSpec((tm, w), functools.partial(lambda i, cb: (i, cb), cb=cb)) for _, w, cb in rows]
    in_specs += [pl.BlockSpec(p.shape, lambda i: (0, 0)) for p in params]
    out_specs = [pl.BlockSpec((tm, w), lambda i: (i, 0)) for w, _ in outs]
    out_specs += [pl.BlockSpec(s, lambda i: (0, 0)) for s in accs]
    out_shape = [jax.ShapeDtypeStruct((t, w), d) for w, d in outs]
    out_shape += [jax.ShapeDtypeStruct(s, F32) for s in accs]
    return pl.pallas_call(
        body, name=name, grid=(t // tm,), in_specs=in_specs, out_specs=out_specs, out_shape=out_shape,
        compiler_params=_params(("arbitrary",)),
    )(*[r[0] for r in rows], *params)


def _vjp_fn(fn, n_in, n_out):
    def bwd(*args):
        ins, cts = args[:n_in], args[n_in:]
        _, pull = jax.vjp(fn, *ins)
        return pull(tuple(cts) if n_out > 1 else cts[0])
    return bwd


def _lane(shape):
    return lax.broadcasted_iota(jnp.int32, shape, 1)


def _silu(x):
    return x * jax.nn.sigmoid(x)


def _softplus(x):
    return jnp.maximum(x, 0.0) + jnp.log1p(jnp.exp(-jnp.abs(x)))


def _heads(x, width=HEAD_DIM):
    return [x[:, h * width:(h + 1) * width] for h in range(N_HEADS)]


def _layer_norm(z, g, b):
    mu = jnp.mean(z, -1, keepdims=True)
    zc = z - mu
    var = jnp.mean(zc * zc, -1, keepdims=True)
    return zc * lax.rsqrt(var + LN_EPS) * g + b


def _gdn_act(u, misc, alog_row, dtb_row):
    s = _silu(u)
    w = N_HEADS * HEAD_DIM
    q = jnp.concatenate([t * lax.rsqrt(jnp.sum(t * t, -1, keepdims=True) + RMS_EPS) * HEAD_DIM ** -0.5
                         for t in _heads(s[:, :w])], axis=1)
    k = jnp.concatenate([t * lax.rsqrt(jnp.sum(t * t, -1, keepdims=True) + RMS_EPS)
                         for t in _heads(s[:, w:2 * w])], axis=1)
    v = s[:, 2 * w:]
    lane = _lane(misc.shape)
    beta = jax.nn.sigmoid(misc)
    g = -jnp.exp(alog_row) * _softplus(misc + dtb_row)
    is_beta = (lane >= MISC_BETA0) & (lane < MISC_BETA0 + N_HEADS)
    is_g = (lane >= MISC_A0) & (lane < MISC_A0 + N_HEADS)
    gb = jnp.where(is_beta, beta, jnp.where(is_g, g, 0.0))
    return q, k, v, gb


def _gdn_out(o, z, gn_row):
    outs = []
    for oh, zh in zip(_heads(o), _heads(z)):
        r = oh * lax.rsqrt(jnp.mean(oh * oh, -1, keepdims=True) + RMS_EPS) * gn_row
        outs.append(r * _silu(zh))
    return jnp.concatenate(outs, axis=1)


def _mla_norm(ckv, cq, kvg_row, qg_row):
    cqn = cq * lax.rsqrt(jnp.mean(cq * cq, -1, keepdims=True) + RMS_EPS) * qg_row
    ckvn = ckv * lax.rsqrt(jnp.mean(ckv * ckv, -1, keepdims=True) + RMS_EPS) * kvg_row
    return cqn, ckvn


def _swap_halves(x):
    half = ROPE_DIM // 2
    return jnp.where(_lane(x.shape) < half, pltpu.roll(x, LANES - half, 1), pltpu.roll(x, half, 1))


@jax.custom_vjp
def _rope(x, cos_t, sin_t):
    return x * cos_t + _swap_halves(x) * sin_t


def _rope_fwd(x, cos_t, sin_t):
    return _rope(x, cos_t, sin_t), (cos_t, sin_t)


def _rope_bwd(res, g):
    cos_t, sin_t = res
    return g * cos_t - _swap_halves(g) * sin_t, jnp.zeros_like(cos_t), jnp.zeros_like(sin_t)


_rope.defvjp(_rope_fwd, _rope_bwd)


def _mla_qk(scale, qm, kv, misc, cos_t, sin_t):
    krope = _rope(misc, cos_t, sin_t)
    qs, ks = [], []
    for h in range(N_HEADS):
        base = 2 * HEAD_DIM * h
        qs += [qm[:, base:base + HEAD_DIM], _rope(qm[:, base + HEAD_DIM:base + 2 * HEAD_DIM], cos_t, sin_t)]
        ks += [kv[:, HEAD_DIM * h:HEAD_DIM * (h + 1)], krope]
    return jnp.concatenate(qs, axis=1) * scale, jnp.concatenate(ks, axis=1), kv[:, N_HEADS * HEAD_DIM:]


def _swiglu(gu):
    f = gu.shape[1] // 2
    return _silu(gu[:, :f]) * gu[:, f:]


def _ple_out(x2, pg, pe):
    return x2 + jax.nn.sigmoid(pg) * pe


CONV_W = 4
HALO = 8


def _conv_fwd(h, conv_w, width, *, name, tm=ROW_TILE, sub=32):
    t = h.shape[0]
    tm = min(tm, t)
    nb = tm // HALO

    def body(x_ref, halo_ref, w_ref, u_ref, buf):
        i = pl.program_id(0)
        buf[pl.ds(0, HALO), :] = jnp.where(i > 0, halo_ref[...], 0.0)
        buf[pl.ds(HALO, tm), :] = x_ref[...]
        w = w_ref[...]
        for r0 in range(0, tm, sub):
            acc = jnp.zeros((sub, width), F32)
            for j in range(CONV_W):
                acc = acc + w[j:j + 1, :] * buf[pl.ds(HALO + r0 - (CONV_W - 1) + j, sub), :]
            u_ref[pl.ds(r0, sub), :] = acc

    return pl.pallas_call(
        body, name=name, grid=(t // tm,),
        in_specs=[pl.BlockSpec((tm, width), lambda i: (i, 0)),
                  pl.BlockSpec((HALO, width), lambda i: (jnp.maximum(i * nb - 1, 0), 0)),
                  pl.BlockSpec(conv_w.shape, lambda i: (0, 0))],
        out_specs=pl.BlockSpec((tm, width), lambda i: (i, 0)),
        out_shape=jax.ShapeDtypeStruct((t, width), F32),
        scratch_shapes=[pltpu.VMEM((tm + HALO, width), F32)],
        compiler_params=_params(("arbitrary",)),
    )(h, h, conv_w)


def _conv_bwd(du, h, conv_w, width, *, name, tm=ROW_TILE, sub=32):
    t = h.shape[0]
    tm = min(tm, t)
    nb = tm // HALO
    n_tiles = t // tm

    def body(du_ref, du_halo, x_ref, x_halo, w_ref, dx_ref, dw_ref, dbuf, xbuf):
        i = pl.program_id(0)

        @pl.when(i == 0)
        def _():
            dw_ref[...] = jnp.zeros_like(dw_ref)

        dbuf[pl.ds(0, tm), :] = du_ref[...]
        dbuf[pl.ds(tm, HALO), :] = jnp.where(i < n_tiles - 1, du_halo[...], 0.0)
        xbuf[pl.ds(0, HALO), :] = jnp.where(i > 0, x_halo[...], 0.0)
        xbuf[pl.ds(HALO, tm), :] = x_ref[...]
        w = w_ref[...]
        dws = [jnp.zeros((1, width), F32) for _ in range(CONV_W)]
        for r0 in range(0, tm, sub):
            acc = jnp.zeros((sub, width), F32)
            d_here = dbuf[pl.ds(r0, sub), :]
            for j in range(CONV_W):
                acc = acc + w[j:j + 1, :] * dbuf[pl.ds(r0 + (CONV_W - 1) - j, sub), :]
                xs = xbuf[pl.ds(HALO + r0 - (CONV_W - 1) + j, sub), :]
                dws[j] = dws[j] + jnp.sum(d_here * xs, axis=0, keepdims=True)
            dx_ref[pl.ds(r0, sub), :] = acc.astype(dx_ref.dtype)
        for j in range(CONV_W):
            dw_ref[pl.ds(j, 1), :] += dws[j]

    return pl.pallas_call(
        body, name=name, grid=(n_tiles,),
        in_specs=[pl.BlockSpec((tm, width), lambda i: (i, 0)),
                  pl.BlockSpec((HALO, width), lambda i: (jnp.minimum((i + 1) * nb, t // HALO - 1), 0)),
                  pl.BlockSpec((tm, width), lambda i: (i, 0)),
                  pl.BlockSpec((HALO, width), lambda i: (jnp.maximum(i * nb - 1, 0), 0)),
                  pl.BlockSpec(conv_w.shape, lambda i: (0, 0))],
        out_specs=[pl.BlockSpec((tm, width), lambda i: (i, 0)),
                   pl.BlockSpec((HALO, width), lambda i: (0, 0))],
        out_shape=[jax.ShapeDtypeStruct((t, width), BF16), jax.ShapeDtypeStruct((HALO, width), F32)],
        scratch_shapes=[pltpu.VMEM((tm + HALO, width), F32), pltpu.VMEM((tm + HALO, width), F32)],
        compiler_params=_params(("arbitrary",)),
    )(du, du, h, h, conv_w)


@jax.custom_vjp
def _inv_unit_lower(low):
    n = low.shape[-1]
    eye = (lax.broadcasted_iota(jnp.int32, (n, n), 0) == lax.broadcasted_iota(jnp.int32, (n, n), 1)).astype(F32)
    x = eye - low
    p = low
    span = 2
    while span < n:
        p = _hnn(p, p)
        x = x + _hnn(x, p)
        span *= 2
    return x


def _inv_fwd(low):
    x = _inv_unit_lower(low)
    return x, x


def _inv_bwd(x, g):
    return (-_htn(x, _hnt(g, x)),)


_inv_unit_lower.defvjp(_inv_fwd, _inv_bwd)


def _gdn_prep(q, k, v, gb):
    c = CHUNK
    n = q.shape[0] // c
    pairs = [(g, h) for g in range(n) for h in range(N_HEADS)]
    row = lax.broadcasted_iota(jnp.int32, (c, c), 0)
    col = lax.broadcasted_iota(jnp.int32, (c, c), 1)
    tri_incl = row >= col
    tri_strict = row > col
    lane = _lane((c, LANES))
    sub = lax.broadcasted_iota(jnp.int32, (LANES, c), 0)
    last = lax.broadcasted_iota(jnp.int32, (c, 1), 0) == c - 1

    def split(x):
        return jnp.stack([x[g * c:(g + 1) * c, h * HEAD_DIM:(h + 1) * HEAD_DIM] for g, h in pairs])

    gbs = [gb[g * c:(g + 1) * c, :] for g in range(n)]
    gbts = [x.T for x in gbs]
    g_col = jnp.stack([jnp.sum(jnp.where(lane == MISC_A0 + h, gbs[g], 0.0), axis=1, keepdims=True) for g, h in pairs])
    b_col = jnp.stack([jnp.sum(jnp.where(lane == MISC_BETA0 + h, gbs[g], 0.0), axis=1, keepdims=True) for g, h in pairs])
    g_row = jnp.stack([jnp.sum(jnp.where(sub == MISC_A0 + h, gbts[g], 0.0), axis=0, keepdims=True) for g, h in pairs])
    gc_col = jnp.sum(jnp.where(tri_incl, g_row, 0.0), axis=2, keepdims=True)
    gc_row = jnp.sum(jnp.where(row <= col, g_col, 0.0), axis=1, keepdims=True)
    decay = jnp.where(tri_incl, jnp.exp(jnp.where(tri_incl, gc_col - gc_row, 0.0)), 0.0)
    g_last = jnp.sum(jnp.where(last, gc_col, 0.0), axis=1, keepdims=True)
    qs, ks, vs = split(q), split(k), split(v)
    kb = ks * b_col
    low = jnp.where(tri_strict, _nt(kb, ks) * decay, 0.0)
    tinv = _inv_unit_lower(low)
    eg = jnp.exp(gc_col)
    sol = _hnn(tinv, jnp.concatenate([vs * b_col, kb * eg], axis=2))
    attn = jnp.where(tri_incl, _nt(qs, ks) * decay, 0.0)
    qd = qs * eg
    kd = ks * jnp.exp(g_last - gc_col)

    def merge(x):
        return jnp.concatenate([jnp.concatenate([x[g * N_HEADS + h] for h in range(N_HEADS)], axis=1)
                                for g in range(n)], axis=0)

    glb = jnp.concatenate([sum(jnp.where(lane == h, g_last[g * N_HEADS + h], 0.0) for h in range(N_HEADS))
                           for g in range(n)], axis=0)
    return merge(sol[:, :, :HEAD_DIM]), merge(sol[:, :, HEAD_DIM:]), merge(qd), merge(kd), merge(attn), glb


def _gdn_seq(state, u, w, qd, kd, attn, glb):
    c = u.shape[0]
    first = lax.broadcasted_iota(jnp.int32, glb.shape, 0) == 0
    lane = _lane(glb.shape)
    outs, states = [], []
    for h in range(N_HEADS):
        hs = slice(h * HEAD_DIM, (h + 1) * HEAD_DIM)
        g_last = jnp.sum(jnp.sum(jnp.where(first & (lane == h), glb, 0.0), axis=1, keepdims=True), axis=0, keepdims=True)
        sh = state[hs, :]
        v_new = u[:, hs] - _nn(w[:, hs], sh)
        outs.append(_nn(qd[:, hs], sh) + _nn(attn[:, h * c:(h + 1) * c], v_new))
        states.append(sh * jnp.exp(g_last) + _tn(kd[:, hs], v_new))
    return jnp.concatenate(outs, axis=1), jnp.concatenate(states, axis=0)


PREP_CHUNKS = 2
SEQ_CHUNKS = 8


def _gdn_prep_fwd(q, k, v, gb, *, name):
    t, w = q.shape
    rows = min(PREP_CHUNKS * CHUNK, t)

    def body(q_ref, k_ref, v_ref, gb_ref, *out_refs):
        res = _gdn_prep(q_ref[...], k_ref[...], v_ref[...], gb_ref[...])
        for o_ref, val in zip(out_refs, res):
            o_ref[...] = val

    spec = lambda width: pl.BlockSpec((rows, width), lambda i: (i, 0))
    widths = [w, w, w, w, N_HEADS * CHUNK, LANES]
    return pl.pallas_call(
        body, name=name, grid=(t // rows,),
        in_specs=[spec(w), spec(w), spec(w), spec(LANES)],
        out_specs=[spec(x) for x in widths],
        out_shape=[jax.ShapeDtypeStruct((t, x), F32) for x in widths],
        compiler_params=_params(("arbitrary",)),
    )(q, k, v, gb)


def _gdn_prep_bwd(q, k, v, gb, cts, *, name):
    t, w = q.shape
    rows = min(PREP_CHUNKS * CHUNK, t)

    def body(q_ref, k_ref, v_ref, gb_ref, du, dw, dqd, dkd, dattn, dglb, dq_ref, dk_ref, dv_ref, dgb_ref):
        _, pull = jax.vjp(_gdn_prep, q_ref[...], k_ref[...], v_ref[...], gb_ref[...])
        dq, dk, dv, dgb = pull(tuple(r[...] for r in (du, dw, dqd, dkd, dattn, dglb)))
        dq_ref[...] = dq
        dk_ref[...] = dk
        dv_ref[...] = dv
        dgb_ref[...] = dgb

    spec = lambda width: pl.BlockSpec((rows, width), lambda i: (i, 0))
    widths = [w, w, w, w, N_HEADS * CHUNK, LANES]
    return pl.pallas_call(
        body, name=name, grid=(t // rows,),
        in_specs=[spec(w), spec(w), spec(w), spec(LANES)] + [spec(x) for x in widths],
        out_specs=[spec(w), spec(w), spec(w), spec(LANES)],
        out_shape=[jax.ShapeDtypeStruct((t, w), F32)] * 3 + [jax.ShapeDtypeStruct((t, LANES), F32)],
        compiler_params=_params(("arbitrary",)),
    )(q, k, v, gb, *cts)


def _gdn_seq_fwd(prep, *, name):
    t, w = prep[0].shape
    rows = min(SEQ_CHUNKS * CHUNK, t)
    per = rows // CHUNK

    def body(u_ref, w_ref, qd_ref, kd_ref, at_ref, gl_ref, o_ref, sall_ref, s_scr):
        @pl.when(pl.program_id(0) == 0)
        def _():
            s_scr[...] = jnp.zeros_like(s_scr)

        def step(j, carry):
            sl = pl.ds(pl.multiple_of(j * CHUNK, CHUNK), CHUNK)
            s = s_scr[...]
            sall_ref[j] = s
            o, s_new = _gdn_seq(s, u_ref[sl, :], w_ref[sl, :], qd_ref[sl, :], kd_ref[sl, :], at_ref[sl, :], gl_ref[sl, :])
            o_ref[sl, :] = o
            s_scr[...] = s_new
            return carry

        lax.fori_loop(0, per, step, 0)

    spec = lambda width: pl.BlockSpec((rows, width), lambda i: (i, 0))
    widths = [w, w, w, w, N_HEADS * CHUNK, LANES]
    return pl.pallas_call(
        body, name=name, grid=(t // rows,),
        in_specs=[spec(x) for x in widths],
        out_specs=[spec(w), pl.BlockSpec((per, w, HEAD_DIM), lambda i: (i, 0, 0))],
        out_shape=[jax.ShapeDtypeStruct((t, w), F32), jax.ShapeDtypeStruct((t // CHUNK, w, HEAD_DIM), F32)],
        scratch_shapes=[pltpu.VMEM((w, HEAD_DIM), F32)],
        compiler_params=_params(("arbitrary",)),
    )(*prep)


def _gdn_seq_bwd(prep, s_all, do, *, name):
    t, w = prep[0].shape
    rows = min(SEQ_CHUNKS * CHUNK, t)
    per = rows // CHUNK
    n = t // rows

    def body(u_ref, w_ref, qd_ref, kd_ref, at_ref, gl_ref, sall_ref, do_ref, du, dw, dqd, dkd, dat, dgl, ds_scr):
        @pl.when(pl.program_id(0) == 0)
        def _():
            ds_scr[...] = jnp.zeros_like(ds_scr)

        def step(jj, carry):
            j = per - 1 - jj
            sl = pl.ds(pl.multiple_of(j * CHUNK, CHUNK), CHUNK)
            _, pull = jax.vjp(_gdn_seq, sall_ref[j], u_ref[sl, :], w_ref[sl, :], qd_ref[sl, :], kd_ref[sl, :],
                              at_ref[sl, :], gl_ref[sl, :])
            res = pull((do_ref[sl, :], ds_scr[...]))
            ds_scr[...] = res[0]
            for o_ref, val in zip((du, dw, dqd, dkd, dat, dgl), res[1:]):
                o_ref[sl, :] = val
            return carry

        lax.fori_loop(0, per, step, 0)

    spec = lambda width: pl.BlockSpec((rows, width), lambda i: (n - 1 - i, 0))
    widths = [w, w, w, w, N_HEADS * CHUNK, LANES]
    return pl.pallas_call(
        body, name=name, grid=(n,),
        in_specs=[spec(x) for x in widths] + [pl.BlockSpec((per, w, HEAD_DIM), lambda i: (n - 1 - i, 0, 0)), spec(w)],
        out_specs=[spec(x) for x in widths],
        out_shape=[jax.ShapeDtypeStruct((t, x), F32) for x in widths],
        scratch_shapes=[pltpu.VMEM((w, HEAD_DIM), F32)],
        compiler_params=_params(("arbitrary",)),
    )(*prep, s_all, do)


QK_DIM = 2 * HEAD_DIM
ATT_TILE = 512
NEG = -1e30


ATT_SPLIT = 2


def _chunk_mask(n_rows, n_cols, key_major, query_offset):
    r = lax.broadcasted_iota(jnp.int32, (n_rows, n_cols), 0)
    c = lax.broadcasted_iota(jnp.int32, (n_rows, n_cols), 1)
    if key_major:
        return r // CHUNK <= (c + query_offset) // CHUNK
    return c // CHUNK <= (r + query_offset) // CHUNK


def _dot_nt(a, b):
    return lax.dot_general(a, b, NT, preferred_element_type=F32)


def _dot_nn(a, b):
    return lax.dot_general(a, b, NN, preferred_element_type=F32)


def _blocked_transpose(x, width):
    t = x.shape[0]
    tile = min(ATT_TILE, t)
    return x.reshape(t // tile, tile, N_HEADS, width).transpose(2, 0, 3, 1)


def _attn_fwd(q, kt, v, *, name):
    t = q.shape[0]
    tq = min(ATT_TILE, t)
    nq = t // tq

    def body(q_ref, kt_ref, v_ref, o_ref, lse_ref, m_scr, l_scr, acc_scr):
        qi = pl.program_id(1)
        m_scr[...] = jnp.full_like(m_scr, NEG)
        l_scr[...] = jnp.zeros_like(l_scr)
        acc_scr[...] = jnp.zeros_like(acc_scr)
        hq = tq // ATT_SPLIT
        parts = [pl.ds(a * hq, hq) for a in range(ATT_SPLIT)]
        qs = [q_ref[sl, :] for sl in parts]

        def step(kj, masked):
            rows = pl.ds(pl.multiple_of(kj * tq, tq), tq)
            kt_blk, vv = kt_ref[kj], v_ref[rows, :]
            ss = [_dot_nn(qv, kt_blk) for qv in qs]
            for a, sl in enumerate(parts):
                s = ss[a]
                if masked:
                    s = jnp.where(_chunk_mask(hq, tq, False, a * hq), s, NEG)
                m_old = m_scr[sl, :]
                m_new = jnp.maximum(m_old, jnp.max(s, axis=1, keepdims=True))
                p = jnp.exp(s - m_new)
                alpha = jnp.exp(m_old - m_new)
                l_scr[sl, :] = alpha * l_scr[sl, :] + jnp.sum(p, axis=1, keepdims=True)
                acc_scr[sl, :] = alpha * acc_scr[sl, :] + _dot_nn(p.astype(BF16), vv)
                m_scr[sl, :] = m_new

        def loop_body(kj, carry):
            step(kj, False)
            return carry

        lax.fori_loop(0, qi, loop_body, 0)
        step(qi, True)
        o_ref[...] = (acc_scr[...] / l_scr[...]).astype(o_ref.dtype)
        lse_ref[...] = m_scr[...] + jnp.log(l_scr[...])

    return pl.pallas_call(
        body, name=name, grid=(N_HEADS, nq),
        in_specs=[pl.BlockSpec((tq, QK_DIM), lambda h, i: (i, h)),
                  pl.BlockSpec((None, nq, QK_DIM, tq), lambda h, i: (h, 0, 0, 0)),
                  pl.BlockSpec((t, HEAD_DIM), lambda h, i: (0, h))],
        out_specs=[pl.BlockSpec((tq, HEAD_DIM), lambda h, i: (i, h)),
                   pl.BlockSpec((None, tq, 1), lambda h, i: (h, i, 0))],
        out_shape=[jax.ShapeDtypeStruct((t, N_HEADS * HEAD_DIM), BF16),
                   jax.ShapeDtypeStruct((N_HEADS, t, 1), F32)],
        scratch_shapes=[pltpu.VMEM((tq, 1), F32), pltpu.VMEM((tq, 1), F32), pltpu.VMEM((tq, HEAD_DIM), F32)],
        compiler_params=_params(("arbitrary", "arbitrary")),
    )(q, kt, v)


def _attn_bwd_dq(q, k, kt, vt, o, lse, dom, do_col0, *, name):
    t = q.shape[0]
    tq = min(ATT_TILE, t)
    nq = t // tq

    def body(q_ref, k_ref, kt_ref, vt_ref, o_ref, lse_ref, do_ref, dq_ref, delta_ref, acc_scr):
        qi = pl.program_id(1)
        acc_scr[...] = jnp.zeros_like(acc_scr)
        do = do_ref[...]
        delta = jnp.sum(do * o_ref[...].astype(F32), axis=1, keepdims=True)
        delta_ref[...] = delta
        hq = tq // ATT_SPLIT
        parts = [pl.ds(a * hq, hq) for a in range(ATT_SPLIT)]
        qs = [q_ref[sl, :] for sl in parts]
        dos = [do_ref[sl, :].astype(BF16) for sl in parts]
        lses = [lse_ref[sl, :] for sl in parts]
        deltas = [delta[a * hq:(a + 1) * hq, :] for a in range(ATT_SPLIT)]

        def step(kj, masked):
            rows = pl.ds(pl.multiple_of(kj * tq, tq), tq)
            kt_blk, vt_blk = kt_ref[kj], vt_ref[kj]
            ss = [_dot_nn(qv, kt_blk) for qv in qs]
            dps = [_dot_nn(do_b, vt_blk) for do_b in dos]
            kv_ = k_ref[rows, :]
            for a, sl in enumerate(parts):
                p = jnp.exp(ss[a] - lses[a])
                if masked:
                    p = jnp.where(_chunk_mask(hq, tq, False, a * hq), p, 0.0)
                ds = p * (dps[a] - deltas[a])
                acc_scr[sl, :] += _dot_nn(ds.astype(BF16), kv_)

        def loop_body(kj, carry):
            step(kj, False)
            return carry

        lax.fori_loop(0, qi, loop_body, 0)
        step(qi, True)
        dq_ref[...] = acc_scr[...].astype(dq_ref.dtype)

    return pl.pallas_call(
        body, name=name, grid=(N_HEADS, nq),
        in_specs=[pl.BlockSpec((tq, QK_DIM), lambda h, i: (i, h)),
                  pl.BlockSpec((t, QK_DIM), lambda h, i: (0, h)),
                  pl.BlockSpec((None, nq, QK_DIM, tq), lambda h, i: (h, 0, 0, 0)),
                  pl.BlockSpec((None, nq, HEAD_DIM, tq), lambda h, i: (h, 0, 0, 0)),
                  pl.BlockSpec((tq, HEAD_DIM), lambda h, i: (i, h)),
                  pl.BlockSpec((None, tq, 1), lambda h, i: (h, i, 0)),
                  pl.BlockSpec((tq, HEAD_DIM), lambda h, i: (i, do_col0 + h))],
        out_specs=[pl.BlockSpec((tq, QK_DIM), lambda h, i: (i, h)),
                   pl.BlockSpec((None, tq, 1), lambda h, i: (h, i, 0))],
        out_shape=[jax.ShapeDtypeStruct((t, N_HEADS * QK_DIM), BF16),
                   jax.ShapeDtypeStruct((N_HEADS, t, 1), F32)],
        scratch_shapes=[pltpu.VMEM((tq, QK_DIM), F32)],
        compiler_params=_params(("arbitrary", "arbitrary")),
    )(q, k, kt, vt, o, lse, dom)


def _attn_bwd_dkv(q, qt, k, v, lse_row, delta_row, dom, dot, do_col0, *, name):
    t = q.shape[0]
    tk = min(ATT_TILE, t)
    nk = t // tk

    def body(q_ref, qt_ref, k_ref, v_ref, lse_ref, delta_ref, do_ref, dot_ref, dk_ref, dv_ref, dk_scr, dv_scr):
        kj = pl.program_id(1)
        dk_scr[...] = jnp.zeros_like(dk_scr)
        dv_scr[...] = jnp.zeros_like(dv_scr)
        kv_ = k_ref[...]
        vv = v_ref[...]

        hq = tk // ATT_SPLIT

        def step(qi, masked):
            lse_v, delta_v = lse_ref[qi], delta_ref[qi]
            qt_blk, dot_blk = qt_ref[qi], dot_ref[qi]
            qs, dos = [], []
            for a in range(ATT_SPLIT):
                rows = pl.ds(pl.multiple_of(qi * tk + a * hq, hq), hq)
                qs.append(q_ref[rows, :])
                dos.append(do_ref[rows, :].astype(BF16))
            ss = [_dot_nn(kv_, qt_blk[:, a * hq:(a + 1) * hq]) for a in range(ATT_SPLIT)]
            dps = [_dot_nn(vv, dot_blk[:, a * hq:(a + 1) * hq]) for a in range(ATT_SPLIT)]
            for a in range(ATT_SPLIT):
                cols = slice(a * hq, (a + 1) * hq)
                p = jnp.exp(ss[a] - lse_v[:, cols])
                if masked:
                    p = jnp.where(_chunk_mask(tk, hq, True, a * hq), p, 0.0)
                dv_scr[...] += _dot_nn(p.astype(BF16), dos[a])
                ds = p * (dps[a] - delta_v[:, cols])
                dk_scr[...] += _dot_nn(ds.astype(BF16), qs[a])

        step(kj, True)

        def loop_body(qi, carry):
            step(qi, False)
            return carry

        lax.fori_loop(kj + 1, nk, loop_body, 0)
        dk_ref[...] = dk_scr[...].astype(dk_ref.dtype)
        dv_ref[...] = dv_scr[...].astype(dv_ref.dtype)

    stat = pl.BlockSpec((None, nk, 1, tk), lambda h, j: (h, 0, 0, 0))
    return pl.pallas_call(
        body, name=name, grid=(N_HEADS, nk),
        in_specs=[pl.BlockSpec((t, QK_DIM), lambda h, j: (0, h)),
                  pl.BlockSpec((None, nk, QK_DIM, tk), lambda h, j: (h, 0, 0, 0)),
                  pl.BlockSpec((tk, QK_DIM), lambda h, j: (j, h)),
                  pl.BlockSpec((tk, HEAD_DIM), lambda h, j: (j, h)),
                  stat, stat,
                  pl.BlockSpec((t, HEAD_DIM), lambda h, j: (0, do_col0 + h)),
                  pl.BlockSpec((None, nk, HEAD_DIM, tk), lambda h, j: (h, 0, 0, 0))],
        out_specs=[pl.BlockSpec((tk, QK_DIM), lambda h, j: (j, h)),
                   pl.BlockSpec((tk, HEAD_DIM), lambda h, j: (j, h))],
        out_shape=[jax.ShapeDtypeStruct((t, N_HEADS * QK_DIM), BF16),
                   jax.ShapeDtypeStruct((t, N_HEADS * HEAD_DIM), BF16)],
        scratch_shapes=[pltpu.VMEM((tk, QK_DIM), F32), pltpu.VMEM((tk, HEAD_DIM), F32)],
        compiler_params=_params(("arbitrary", "arbitrary")),
    )(q, qt, k, v, lse_row, delta_row, dom, dot)


def _rope_tables(pos_col, inv_freq_row, *, name):
    t = pos_col.shape[0]
    tm = min(ROW_TILE, t)

    def body(p_ref, f_ref, c_ref, s_ref):
        ang = p_ref[...].astype(F32) * f_ref[...]
        lane = _lane(ang.shape)
        c_ref[...] = jnp.where(lane < ROPE_DIM, jnp.cos(ang), 0.0)
        sn = jnp.sin(ang)
        s_ref[...] = jnp.where(lane < ROPE_DIM // 2, -sn, jnp.where(lane < ROPE_DIM, sn, 0.0))

    out = pl.BlockSpec((tm, LANES), lambda i: (i, 0))
    return pl.pallas_call(
        body, name=name, grid=(t // tm,),
        in_specs=[pl.BlockSpec((tm, 1), lambda i: (i, 0)), pl.BlockSpec((1, LANES), lambda i: (0, 0))],
        out_specs=[out, out], out_shape=[jax.ShapeDtypeStruct((t, LANES), F32)] * 2,
        compiler_params=_params(("arbitrary",)),
    )(pos_col, inv_freq_row)


def _loss_head(y, target):
    width = y.shape[1]

    def fn(yv, tv):
        e = yv - tv
        part = 0.5 * jnp.sum(jnp.mean(e * e, axis=1, keepdims=True), axis=0, keepdims=True)
        return e * (1.0 / width), jnp.broadcast_to(part, (HALO, LANES))

    return _rowwise(fn, [(y, width, 0), (target, width, 0)], [], [(width, F32)], [(HALO, LANES)], name="loss_head")


def _adamw(w, g, m, v, *, name):
    shape = w.shape
    w2, g2, m2, v2 = (a.reshape(-1, shape[-1]) for a in (w, g, m, v))
    rows, width = w2.shape
    tr = _divisor_tile(rows, max(8, (1 << 19) // max(width, 1)), 8)
    bc1 = 1.0 - ADAM_B1 ** ADAM_STEP
    bc2 = 1.0 - ADAM_B2 ** ADAM_STEP

    def body(w_ref, g_ref, m_ref, v_ref, d_ref, mo_ref, vo_ref):
        gv = g_ref[...]
        mn = ADAM_B1 * m_ref[...] + (1.0 - ADAM_B1) * gv
        vn = ADAM_B2 * v_ref[...] + (1.0 - ADAM_B2) * (gv * gv)
        d_ref[...] = -ADAM_LR * ((mn / bc1) / (jnp.sqrt(vn / bc2) + ADAM_EPS) + ADAM_WD * w_ref[...])
        mo_ref[...] = mn
        vo_ref[...] = vn

    spec = pl.BlockSpec((tr, width), lambda i: (i, 0))
    outs = pl.pallas_call(
        body, name=name, grid=(rows // tr,), in_specs=[spec] * 4, out_specs=[spec] * 3,
        out_shape=[jax.ShapeDtypeStruct((rows, width), F32)] * 3,
        compiler_params=_params(("arbitrary",)),
    )(w2, g2, m2, v2)
    return tuple(o.reshape(shape) for o in outs)


HBM_SPEC = pl.BlockSpec(memory_space=pltpu.HBM)


def _position():
    return lax.axis_index("x"), lax.axis_index("y"), lax.axis_index("c")


def _other_chips(x, y):
    return [(1 - x, y), (x, 1 - y), (1 - x, 1 - y)]


class _Stream:
    def __init__(self, kind, size=0):
        self.kind, self.size = kind, size
        self.parts = 2 if kind == "heads" else 1

    def local(self, ref, k, part):
        if self.kind == "rows":
            return ref.at[:, pl.ds(k * self.size, self.size), :]
        if self.kind == "cols":
            return ref.at[:, :, pl.ds(k * self.size, self.size)]
        if self.kind == "heads":
            return ref.at[:, :, pl.ds(part * N_HEADS * HEAD_DIM + k * HEAD_DIM, HEAD_DIM)]
        if self.kind == "piece":
            return ref.at[k]
        return ref

    def shard(self, ref, part):
        if self.kind == "heads":
            return ref.at[:, :, pl.ds(part * HEAD_DIM, HEAD_DIM)]
        return ref

    def half_local(self, ref, k, part, cc, hd):
        if self.kind == "piece":
            return ref.at[k, pl.ds(cc * hd, hd)]
        return self.local(ref.at[pl.ds(cc * hd, hd)], k, part)


def _remote(src, dst, send_sems, recv_sems, idx, to):
    return pltpu.make_async_remote_copy(src_ref=src, dst_ref=dst, send_sem=send_sems.at[idx],
                                        recv_sem=recv_sems.at[idx], device_id=to, device_id_type=MESH)


def _comm_call(body, ins, out_shapes, n_remote, n_local, *, name):
    scratch = [pltpu.SemaphoreType.DMA((n_remote,)), pltpu.SemaphoreType.DMA((n_remote,))]
    if n_local:
        scratch.append(pltpu.SemaphoreType.DMA((n_local,)))
    return pl.pallas_call(
        body, name=name, in_specs=[HBM_SPEC] * len(ins), out_specs=[HBM_SPEC] * len(out_shapes), out_shape=out_shapes,
        scratch_shapes=scratch, compiler_params=pltpu.CompilerParams(has_side_effects=True),
    )(*ins)


def _gather_chips(shards, streams, out_shapes, *, name):
    n = len(shards)
    hd = shards[0].shape[0] // 2
    flat = [(t, part) for t in range(n) for part in range(streams[t].parts)]
    ns = len(flat)

    def body(*refs):
        s_refs, o_refs = refs[:n], refs[n:2 * n]
        send_sems, recv_sems, local_sems = refs[2 * n:]
        x, y, c = _position()
        sibling = (x, y, 1 - c)
        chips = _other_chips(x, y)
        me = 2 * x + y
        own = []
        for s, (t, part) in enumerate(flat):
            st = streams[t]
            own.append(pltpu.make_async_copy(st.shard(s_refs[t], part), st.local(o_refs[t], me, part), local_sems.at[s]))
            own[-1].start()
        sent = []
        for s, (t, part) in enumerate(flat):
            st = streams[t]
            src = st.shard(s_refs[t].at[pl.ds(c * hd, hd)], part)
            for j, (cx, cy) in enumerate(chips):
                sent.append(_remote(src, st.half_local(o_refs[t], me, part, c, hd), send_sems, recv_sems,
                                    3 * s + j, (cx, cy, c)))
                sent[-1].start()
        for s, (t, part) in enumerate(flat):
            st = streams[t]
            for j, (cx, cy) in enumerate(chips):
                blk = st.half_local(o_refs[t], 2 * cx + cy, part, c, hd)
                _remote(blk, blk, send_sems, recv_sems, 3 * s + j, (x, y, c)).wait_recv()
                sent.append(_remote(blk, blk, send_sems, recv_sems, 3 * ns + 3 * s + j, sibling))
                sent[-1].start()
        for s, (t, part) in enumerate(flat):
            st = streams[t]
            for j, (cx, cy) in enumerate(chips):
                blk = st.half_local(o_refs[t], 2 * cx + cy, part, 1 - c, hd)
                _remote(blk, blk, send_sems, recv_sems, 3 * ns + 3 * s + j, (x, y, c)).wait_recv()
        for cp in sent:
            cp.wait_send()
        for cp in own:
            cp.wait()

    return _comm_call(body, shards, out_shapes, 6 * ns, ns, name=name)


def _sibling_take_other_half(gs, *, name):
    n = len(gs)
    hd = gs[0].shape[0] // 2

    def body(*refs):
        g_refs, o_refs = refs[:n], refs[n:2 * n]
        send_sems, recv_sems = refs[2 * n:]
        x, y, c = _position()
        copies = [_remote(g_refs[t].at[pl.ds((1 - c) * hd, hd)], o_refs[t], send_sems, recv_sems, t, (x, y, 1 - c))
                  for t in range(n)]
        for cp in copies:
            cp.start()
        for cp in copies:
            cp.wait()

    outs = [jax.ShapeDtypeStruct((hd,) + g.shape[1:], g.dtype) for g in gs]
    return _comm_call(body, gs, outs, n, 0, name=name)


def _chips_exchange(ps, streams, shard_shapes, *, name):
    n = len(ps)
    flat = [(t, part) for t in range(n) for part in range(streams[t].parts)]

    def body(*refs):
        p_refs, o_refs = refs[:n], refs[n:2 * n]
        send_sems, recv_sems = refs[2 * n:]
        x, y, c = _position()
        copies = []
        for s, (t, part) in enumerate(flat):
            st = streams[t]
            for j, (cx, cy) in enumerate(_other_chips(x, y)):
                copies.append(_remote(st.local(p_refs[t], 2 * cx + cy, part), st.shard(o_refs[t].at[j], part),
                                      send_sems, recv_sems, 3 * s + j, (cx, cy, c)))
        for cp in copies:
            cp.start()
        for cp in copies:
            cp.wait()

    outs = [jax.ShapeDtypeStruct((3,) + tuple(shp), p.dtype) for p, shp in zip(ps, shard_shapes)]
    return _comm_call(body, ps, outs, 3 * len(flat), 0, name=name)


def _sibling_join_halves(halves, *, name):
    n = len(halves)
    hd = halves[0].shape[0]

    def body(*refs):
        m_refs, o_refs = refs[:n], refs[n:2 * n]
        send_sems, recv_sems, local_sems = refs[2 * n:]
        x, y, c = _position()
        own, sent = [], []
        for t in range(n):
            own.append(pltpu.make_async_copy(m_refs[t], o_refs[t].at[pl.ds(c * hd, hd)], local_sems.at[t]))
            own[-1].start()
            sent.append(_remote(m_refs[t], o_refs[t].at[pl.ds(c * hd, hd)], send_sems, recv_sems, t, (x, y, 1 - c)))
            sent[-1].start()
        for t in range(n):
            _remote(m_refs[t], o_refs[t].at[pl.ds((1 - c) * hd, hd)], send_sems, recv_sems, t, (x, y, c)).wait_recv()
        for cp in sent:
            cp.wait_send()
        for cp in own:
            cp.wait()

    outs = [jax.ShapeDtypeStruct((2 * hd,) + h.shape[1:], h.dtype) for h in halves]
    return _comm_call(body, halves, outs, n, n, name=name)


def _row_tile(rows, width):
    return _divisor_tile(rows, max(16, (1 << 19) // width), 16)


def _add_own_half(g, got, c_idx, out_dtype, *, name):
    hd, r, w = got.shape
    tr = _row_tile(r, w)

    def body(c_ref, g_ref, a_ref, o_ref):
        o_ref[...] = (g_ref[...] + a_ref[...]).astype(o_ref.dtype)

    return pl.pallas_call(
        body, name=name,
        grid_spec=pltpu.PrefetchScalarGridSpec(
            num_scalar_prefetch=1, grid=(hd, r // tr),
            in_specs=[pl.BlockSpec((None, None, tr, w), lambda l, i, c_ref: (c_ref[0], l, i, 0)),
                      pl.BlockSpec((None, tr, w), lambda l, i, c_ref: (l, i, 0))],
            out_specs=pl.BlockSpec((None, tr, w), lambda l, i, c_ref: (l, i, 0))),
        out_shape=jax.ShapeDtypeStruct((hd, r, w), out_dtype),
        compiler_params=_params(("arbitrary", "arbitrary")),
    )(c_idx, g.reshape((2, hd) + g.shape[1:]), got)


def _sum_chips(p, got, chip_idx, stream, *, name):
    _, hd, rs, cs = got.shape
    wb = HEAD_DIM if stream.kind == "heads" else cs
    tr = _row_tile(rs, wb)
    kind, size = stream.kind, stream.size

    def own_index(l, i, g, k_ref):
        k = k_ref[0]
        if kind == "rows":
            return (l, k * (size // tr) + i, 0)
        if kind == "cols":
            return (l, i, k)
        if kind == "heads":
            return (l, i, g * N_HEADS + k)
        if kind == "piece":
            return (k, l, i, 0)
        return (l, i, 0)

    own_blk = (None, None, tr, wb) if kind == "piece" else (None, tr, wb)

    def body(k_ref, p_ref, fx_ref, fy_ref, fxy_ref, o_ref):
        f = lambda r: r[...].astype(F32)
        o_ref[...] = (f(p_ref) + f(fy_ref)) + (f(fx_ref) + f(fxy_ref))

    def rel(j):
        return pl.BlockSpec((None, None, tr, wb), functools.partial(lambda l, i, g, k_ref, j: (j, l, i, g), j=j))

    return pl.pallas_call(
        body, name=name,
        grid_spec=pltpu.PrefetchScalarGridSpec(
            num_scalar_prefetch=1, grid=(hd, rs // tr, stream.parts),
            in_specs=[pl.BlockSpec(own_blk, own_index), rel(0), rel(1), rel(2)],
            out_specs=pl.BlockSpec((None, tr, wb), lambda l, i, g, k_ref: (l, i, g))),
        out_shape=jax.ShapeDtypeStruct((hd, rs, cs), F32),
        compiler_params=_params(("arbitrary", "arbitrary", "arbitrary")),
    )(chip_idx, p, got, got, got)


MATRICES = ("w_in", "w_uq", "w_ukv", "w_out", "w_gate_up", "w_down", "w_ple", "w_ple_gate", "conv_w")
VECTORS = ("a_log", "dt_bias", "gdn_norm_g", "q_norm_g", "kv_norm_g", "ln1_g", "ln1_b", "ln2_g", "ln2_b")
WEIGHTS = ("w_in", "conv_w", "a_log", "dt_bias", "gdn_norm_g", "q_norm_g", "w_uq", "kv_norm_g", "w_ukv", "w_out",
           "ln1_g", "ln1_b", "w_gate_up", "w_down", "ln2_g", "ln2_b", "w_ple", "w_ple_gate")
ROW_SHARDED = ("w_out", "w_down", "w_ple_gate")
N_CHIPS = 4


def _stream_of(name, shard_shape):
    if name in ("w_in", "w_uq"):
        return _Stream("piece")
    if name == "w_ukv":
        return _Stream("heads")
    if name in ROW_SHARDED:
        return _Stream("rows", shard_shape[0])
    return _Stream("cols", shard_shape[1])


def _pack_vectors(vecs, depth):
    flat = jnp.concatenate([vecs[n].reshape(depth, -1) for n in VECTORS], axis=1)
    pad = jnp.zeros((depth, VEC_ROWS * LANES - flat.shape[1]), F32)
    return jnp.concatenate([flat, pad], axis=1).reshape(depth, VEC_ROWS, LANES)


def _unpack_vectors(packed, shapes):
    depth = packed.shape[0]
    flat = packed.reshape(depth, VEC_ROWS * LANES)
    out, off = {}, 0
    for n in VECTORS:
        out[n] = flat[:, off:off + shapes[n][1]]
        off += shapes[n][1]
    return out


VEC_ROWS = 40


class _Dims:
    def __init__(self, d_model, in_width, q_lora, kv_lora, d_ff2, ple_dim):
        self.d = d_model
        self.hw = N_HEADS * HEAD_DIM
        self.in_width = in_width
        self.q_lora, self.kv_lora = q_lora, kv_lora
        self.ff2 = d_ff2
        self.ple = ple_dim
        self.c_kv0 = 4 * self.hw
        self.c_q0 = self.c_kv0 + kv_lora
        self.misc0 = self.c_q0 + q_lora
        self.h_width = self.misc0 + LANES
        assert self.c_kv0 % kv_lora == 0 and self.c_q0 % q_lora == 0 and self.misc0 % LANES == 0
        self.g_beta = 4 * self.hw
        self.g_a = self.g_beta + N_HEADS
        self.g_cq = self.g_a + N_HEADS
        self.g_ckv = self.g_cq + q_lora
        self.g_kr = self.g_ckv + kv_lora
        assert self.g_kr + ROPE_DIM == in_width

    def w_in_local(self, w):
        pad = jnp.zeros(w.shape[:-1] + (self.h_width - self.in_width,), w.dtype)
        return jnp.concatenate([w[..., :self.g_beta], w[..., self.g_ckv:self.g_kr], w[..., self.g_cq:self.g_ckv],
                                w[..., self.g_kr:], w[..., self.g_beta:self.g_cq], pad], axis=-1)

    def w_in_global(self, d):
        m = self.misc0
        return jnp.concatenate([d[..., :self.c_kv0], d[..., m + MISC_BETA0:m + MISC_A0 + N_HEADS],
                                d[..., self.c_q0:self.misc0], d[..., self.c_kv0:self.c_q0], d[..., m:m + ROPE_DIM]],
                               axis=-1)

    def w_uq_local(self, w):
        r = w.reshape(w.shape[:-1] + (N_HEADS, HEAD_DIM + ROPE_DIM))
        r = jnp.pad(r, [(0, 0)] * (r.ndim - 1) + [(0, QK_DIM - HEAD_DIM - ROPE_DIM)])
        return r.reshape(w.shape[:-1] + (N_HEADS * QK_DIM,))

    def w_uq_global(self, d):
        r = d.reshape(d.shape[:-1] + (N_HEADS, QK_DIM))[..., :HEAD_DIM + ROPE_DIM]
        return r.reshape(d.shape[:-1] + (N_HEADS * (HEAD_DIM + ROPE_DIM),))


def _lane_padded(n):
    return -(-n // LANES) * LANES


def _pad_lanes(a):
    pad = _lane_padded(a.shape[-1]) - a.shape[-1]
    return a if pad == 0 else jnp.pad(a, [(0, 0)] * (a.ndim - 1) + [(0, pad)])


def _lane_row(vec, lane0):
    pad = LANES - lane0 - vec.shape[0]
    return jnp.concatenate([jnp.zeros((lane0,), F32), vec.astype(F32), jnp.zeros((pad,), F32)])[None, :]


def _layer_fwd(dm, alpha, x, xb, p_i, cos_t, sin_t, wl, tag):
    d, hw = dm.d, dm.hw
    nm = lambda s: f"{s}_{tag}"
    mm = functools.partial(_matmul, layer=wl["layer"])
    h = mm(xb, wl["w_in"], dims="nn", name=nm("f_in"), tn=_divisor_tile(dm.h_width, 1408, LANES))
    misc_cb = dm.misc0 // LANES

    u = _conv_fwd(h, wl["conv_w"], 3 * hw, name=nm("f_conv"))
    qn, kn, vg, gb = _rowwise(_gdn_act, [(u, 3 * hw, 0), (h, LANES, misc_cb)], [wl["alog_row"], wl["dtb_row"]],
                              [(hw, F32), (hw, F32), (hw, F32), (LANES, F32)], name=nm("f_gdn_act"))
    prep = _gdn_prep_fwd(qn, kn, vg, gb, name=nm("f_gdn_prep"))
    o_gdn, s_all = _gdn_seq_fwd(prep, name=nm("f_gdn_seq"))
    (og,) = _rowwise(lambda o, z, g: (_gdn_out(o, z, g),), [(o_gdn, hw, 0), (h, hw, 3)], [wl["gn_row"]],
                     [(hw, BF16)], name=nm("f_gdn_out"))

    cqn, ckvn = _rowwise(_mla_norm, [(h, dm.kv_lora, dm.c_kv0 // dm.kv_lora), (h, dm.q_lora, dm.c_q0 // dm.q_lora)],
                         [wl["kvg_row"], wl["qg_row"]], [(dm.q_lora, BF16), (dm.kv_lora, BF16)], name=nm("f_mla_norm"))
    qm = mm(cqn, wl["w_uq"], dims="nn", name=nm("f_uq"))
    kvm = mm(ckvn, wl["w_ukv"], dims="nn", name=nm("f_ukv"))
    scale = (HEAD_DIM + ROPE_DIM) ** -0.5
    qk_fn = functools.partial(_mla_qk, scale)
    qa, ka, va = _rowwise(qk_fn, [(qm, N_HEADS * QK_DIM, 0), (kvm, 2 * hw, 0), (h, LANES, misc_cb),
                                  (cos_t, LANES, 0), (sin_t, LANES, 0)], [],
                          [(N_HEADS * QK_DIM, BF16), (N_HEADS * QK_DIM, BF16), (hw, BF16)], name=nm("f_mla_qk"))
    kt = _blocked_transpose(ka, QK_DIM)
    o_mla, lse = _attn_fwd(qa, kt, va, name=nm("f_attn"))

    om = jnp.concatenate([og, o_mla], axis=1)
    mix = mm(om, wl["w_out"], dims="nn", name=nm("f_out"))
    ln1 = lambda xv, yv, g, b: (_layer_norm(alpha * xv + yv, g, b),) * 2
    x1, x1b = _rowwise(ln1, [(x, d, 0), (mix, d, 0)], [wl["ln1_g"], wl["ln1_b"]], [(d, F32), (d, BF16)], name=nm("f_ln1"))

    gu = mm(x1b, wl["w_gate_up"], dims="nn", name=nm("f_gate_up"))
    (act,) = _rowwise(lambda g_: (_swiglu(g_),), [(gu, dm.ff2, 0)], [], [(dm.ff2 // 2, BF16)], name=nm("f_swiglu"))
    dn = mm(act, wl["w_down"], dims="nn", name=nm("f_down"))
    x2, x2b = _rowwise(ln1, [(x1, d, 0), (dn, d, 0)], [wl["ln2_g"], wl["ln2_b"]], [(d, F32), (d, BF16)], name=nm("f_ln2"))

    pg = mm(x2b, wl["w_ple_gate"], dims="nn", name=nm("f_ple_gate"))
    pe = mm(p_i, wl["w_ple"], dims="nn", name=nm("f_ple"))
    out, outb = _rowwise(lambda a, b, c_: (_ple_out(a, b, c_),) * 2, [(x2, d, 0), (pg, d, 0), (pe, d, 0)], [],
                         [(d, F32), (d, BF16)], name=nm("f_ple_out"))
    saved = dict(x=x, xb=xb, p_i=p_i, h=h, u=u, qn=qn, kn=kn, vg=vg, gb=gb, prep=prep, s_all=s_all, o_gdn=o_gdn, cqn=cqn, ckvn=ckvn,
                 qm=qm, kvm=kvm, qa=qa, ka=ka, kt=kt, va=va, o_mla=o_mla, lse=lse, om=om, mix=mix, x1=x1, x1b=x1b, gu=gu,
                 act=act, dn=dn, x2=x2, x2b=x2b, pg=pg, pe=pe)
    return out, outb, saved


def _layer_bwd(dm, alpha, dout, sv, cos_t, sin_t, wl, gbuf, tag):
    d, hw = dm.d, dm.hw
    t = dout.shape[0]
    nm = lambda s: f"{s}_{tag}"
    gr = {}
    gbuf = dict(gbuf)
    misc_cb = dm.misc0 // LANES
    mm = functools.partial(_matmul, layer=wl["layer"])

    def wgrad(name_, a, g):
        gbuf[name_] = mm(a, g, dims="tn", name=nm("b_" + name_), into=gbuf[name_], tm=1408, tn=1408, tk=1024)

    dx2_a, dpg, dpe = _rowwise(_vjp_fn(_ple_out, 3, 1), [(sv["x2"], d, 0), (sv["pg"], d, 0), (sv["pe"], d, 0), (dout, d, 0)],
                               [], [(d, F32), (d, BF16), (d, BF16)], name=nm("b_ple_out"))
    wgrad("w_ple", sv["p_i"], dpe)
    wgrad("w_ple_gate", sv["x2b"], dpg)
    dx2 = mm(dpg, wl["w_ple_gate"], dims="nt", c=dx2_a, name=nm("b_x2"))

    def ln_bwd(xv, yv, ct, g, b):
        _, pull = jax.vjp(lambda a_, b_, c_, d_: _layer_norm(alpha * a_ + b_, c_, d_), xv, yv, g, b)
        return pull(ct)

    dx1_a, ddn, gr["ln2_g"], gr["ln2_b"] = _rowwise(
        ln_bwd, [(sv["x1"], d, 0), (sv["dn"], d, 0), (dx2, d, 0)], [wl["ln2_g"], wl["ln2_b"]],
        [(d, F32), (d, BF16)], [(1, d), (1, d)], name=nm("b_ln2"))
    wgrad("w_down", sv["act"], ddn)
    dact = mm(ddn, wl["w_down"], dims="nt", name=nm("b_act"), tn=1408)
    (dgu,) = _rowwise(_vjp_fn(_swiglu, 1, 1), [(sv["gu"], dm.ff2, 0), (dact, dm.ff2 // 2, 0)], [], [(dm.ff2, BF16)],
                      name=nm("b_swiglu"))
    wgrad("w_gate_up", sv["x1b"], dgu)
    dx1 = mm(dgu, wl["w_gate_up"], dims="nt", c=dx1_a, name=nm("b_x1"))

    dx_a, dmix, gr["ln1_g"], gr["ln1_b"] = _rowwise(
        ln_bwd, [(sv["x"], d, 0), (sv["mix"], d, 0), (dx1, d, 0)], [wl["ln1_g"], wl["ln1_b"]],
        [(d, F32), (d, BF16)], [(1, d), (1, d)], name=nm("b_ln1"))
    wgrad("w_out", sv["om"], dmix)
    dom = mm(dmix, wl["w_out"], dims="nt", name=nm("b_om"))

    nq = t // min(ATT_TILE, t)
    dqa, delta = _attn_bwd_dq(sv["qa"], sv["ka"], sv["kt"], _blocked_transpose(sv["va"], HEAD_DIM), sv["o_mla"], sv["lse"],
                              dom, hw // HEAD_DIM, name=nm("b_attn_dq"))
    lse_row = sv["lse"].reshape(N_HEADS, nq, 1, t // nq)
    delta_row = delta.reshape(N_HEADS, nq, 1, t // nq)
    dot = _blocked_transpose(dom[:, hw:].astype(BF16), HEAD_DIM)
    dka, dva = _attn_bwd_dkv(sv["qa"], _blocked_transpose(sv["qa"], QK_DIM), sv["ka"], sv["va"], lse_row, delta_row, dom, dot,
                             hw // HEAD_DIM, name=nm("b_attn_dkv"))
    scale = (HEAD_DIM + ROPE_DIM) ** -0.5
    qk_fn = functools.partial(_mla_qk, scale)

    def qk_bwd(qm, kvm, misc, cs, sn, g_q, g_k, g_v):
        _, pull = jax.vjp(lambda a, b, c_: qk_fn(a, b, c_, cs, sn), qm, kvm, misc)
        return pull((g_q, g_k, g_v))

    dqm, dkvm, dmisc_rope = _rowwise(
        qk_bwd, [(sv["qm"], N_HEADS * QK_DIM, 0), (sv["kvm"], 2 * hw, 0), (sv["h"], LANES, misc_cb), (cos_t, LANES, 0),
                 (sin_t, LANES, 0), (dqa, N_HEADS * QK_DIM, 0), (dka, N_HEADS * QK_DIM, 0), (dva, hw, 0)], [],
        [(N_HEADS * QK_DIM, BF16), (2 * hw, BF16), (LANES, F32)], name=nm("b_mla_qk"))
    wgrad("w_uq", sv["cqn"], dqm)
    wgrad("w_ukv", sv["ckvn"], dkvm)
    dcqn = mm(dqm, wl["w_uq"], dims="nt", name=nm("b_cqn"))
    dckvn = mm(dkvm, wl["w_ukv"], dims="nt", name=nm("b_ckvn"))

    def norm_bwd(ckv, cq, g_q, g_kv, kvg, qg):
        _, pull = jax.vjp(_mla_norm, ckv, cq, kvg, qg)
        return pull((g_q, g_kv))

    dckv, dcq, gr["kvg_row"], gr["qg_row"] = _rowwise(
        norm_bwd, [(sv["h"], dm.kv_lora, dm.c_kv0 // dm.kv_lora), (sv["h"], dm.q_lora, dm.c_q0 // dm.q_lora),
                   (dcqn, dm.q_lora, 0), (dckvn, dm.kv_lora, 0)], [wl["kvg_row"], wl["qg_row"]],
        [(dm.kv_lora, BF16), (dm.q_lora, BF16)], [(1, dm.kv_lora), (1, dm.q_lora)], name=nm("b_mla_norm"))

    def gout_bwd(o, z, g_o, gn):
        _, pull = jax.vjp(_gdn_out, o, z, gn)
        return pull(g_o)

    do_gdn, dz, gr["gn_row"] = _rowwise(gout_bwd, [(sv["o_gdn"], hw, 0), (sv["h"], hw, 3), (dom, hw, 0)], [wl["gn_row"]],
                                        [(hw, F32), (hw, BF16)], [(1, HEAD_DIM)], name=nm("b_gdn_out"))
    dprep = _gdn_seq_bwd(sv["prep"], sv["s_all"], do_gdn, name=nm("b_gdn_seq"))
    dqn, dkn, dvg, dgb = _gdn_prep_bwd(sv["qn"], sv["kn"], sv["vg"], sv["gb"], dprep, name=nm("b_gdn_prep"))

    def act_bwd(u, misc, g_q, g_k, g_v, g_gb, g_rope, alog, dtb):
        _, pull = jax.vjp(_gdn_act, u, misc, alog, dtb)
        du_, dmisc_, dalog_, ddtb_ = pull((g_q, g_k, g_v, g_gb))
        return du_, dmisc_ + g_rope, dalog_, ddtb_

    du, dmisc, gr["alog_row"], gr["dtb_row"] = _rowwise(
        act_bwd, [(sv["u"], 3 * hw, 0), (sv["h"], LANES, misc_cb), (dqn, hw, 0), (dkn, hw, 0), (dvg, hw, 0),
                  (dgb, LANES, 0), (dmisc_rope, LANES, 0)], [wl["alog_row"], wl["dtb_row"]],
        [(3 * hw, F32), (LANES, BF16)], [(1, LANES), (1, LANES)], name=nm("b_gdn_act"))
    dqkv, dconv = _conv_bwd(du, sv["h"], wl["conv_w"], 3 * hw, name=nm("b_conv"))
    gr["conv_w"] = dconv[:CONV_W]

    dh = jnp.concatenate([dqkv, dz, dckv, dcq, dmisc], axis=1)
    wgrad("w_in", sv["xb"], dh)
    dx = mm(dh, wl["w_in"], dims="nt", c=dx_a, name=nm("b_x"), tk=1408)
    return dx, gbuf, gr


LOCAL_MATRICES = ("w_in", "w_uq", "w_ukv", "w_out", "w_gate_up", "w_down", "w_ple", "w_ple_gate")


def _layer_weights(mats, vecs, layer):
    wl = {n: mats[n] for n in LOCAL_MATRICES}
    wl["layer"] = layer
    wl["conv_w"] = mats["conv_w"][layer]
    wl["alog_row"] = _lane_row(vecs["a_log"][layer], MISC_A0)
    wl["dtb_row"] = _lane_row(vecs["dt_bias"][layer], MISC_A0)
    wl["gn_row"] = vecs["gdn_norm_g"][layer][None, :]
    wl["qg_row"] = vecs["q_norm_g"][layer][None, :]
    wl["kvg_row"] = vecs["kv_norm_g"][layer][None, :]
    for n in ("ln1_g", "ln1_b", "ln2_g", "ln2_b"):
        wl[n] = vecs[n][layer][None, :]
    return wl


def _vector_grads(gr):
    out = {"a_log": gr["alog_row"][0, MISC_A0:MISC_A0 + N_HEADS], "dt_bias": gr["dtb_row"][0, MISC_A0:MISC_A0 + N_HEADS],
           "gdn_norm_g": gr["gn_row"][0], "q_norm_g": gr["qg_row"][0], "kv_norm_g": gr["kvg_row"][0]}
    for n in ("ln1_g", "ln1_b", "ln2_g", "ln2_b"):
        out[n] = gr[n][0]
    return out


def _local_step(dm, x, p, positions, target, mats, vecs):
    depth = p.shape[0]
    alpha = (2.0 * depth) ** 0.25
    freq = ROPE_THETA ** (-jnp.arange(0, ROPE_DIM, 2, dtype=F32) / ROPE_DIM)
    inv_freq_row = _lane_row(jnp.concatenate([freq, freq]), 0)
    cos_t, sin_t = _rope_tables(positions.reshape(-1, 1), inv_freq_row, name="rope_tables")

    wls = [_layer_weights(mats, vecs, i) for i in range(depth)]
    saved = []
    cur, cur_b = x, x
    for i in range(depth):
        cur, cur_b, sv = _layer_fwd(dm, alpha, cur, cur_b, p[i], cos_t, sin_t, wls[i], f"l{i}")
        saved.append(sv)
    dy, loss_blk = _loss_head(cur, target)
    gbuf = {n: jnp.zeros(mats[n].shape, F32) for n in LOCAL_MATRICES}
    conv_g, vec_g = [None] * depth, [None] * depth
    for i in reversed(range(depth)):
        dy, gbuf, gr = _layer_bwd(dm, alpha, dy, saved[i], cos_t, sin_t, wls[i], gbuf, f"l{i}")
        conv_g[i] = gr["conv_w"]
        vec_g[i] = _vector_grads(gr)
    vec_grads = {n: jnp.stack([vec_g[i][n] for i in range(depth)]) for n in VECTORS}
    return loss_blk[0, 0], dy, gbuf, jnp.stack(conv_g), vec_grads


def kernel(x, p, positions, w_in, conv_w, a_log, dt_bias, gdn_norm_g, q_norm_g, w_uq, kv_norm_g, w_ukv, w_out, ln1_g, ln1_b, w_gate_up, w_down, ln2_g, ln2_b, w_ple, w_ple_gate, loss_target, m_w_in, m_conv_w, m_a_log, m_dt_bias, m_gdn_norm_g, m_q_norm_g, m_w_uq, m_kv_norm_g, m_w_ukv, m_w_out, m_ln1_g, m_ln1_b, m_w_gate_up, m_w_down, m_ln2_g, m_ln2_b, m_w_ple, m_w_ple_gate, v_w_in, v_conv_w, v_a_log, v_dt_bias, v_gdn_norm_g, v_q_norm_g, v_w_uq, v_kv_norm_g, v_w_ukv, v_w_out, v_ln1_g, v_ln1_b, v_w_gate_up, v_w_down, v_ln2_g, v_ln2_b, v_w_ple, v_w_ple_gate):
    w = dict(w_in=w_in, conv_w=conv_w, a_log=a_log, dt_bias=dt_bias, gdn_norm_g=gdn_norm_g, q_norm_g=q_norm_g, w_uq=w_uq,
             kv_norm_g=kv_norm_g, w_ukv=w_ukv, w_out=w_out, ln1_g=ln1_g, ln1_b=ln1_b, w_gate_up=w_gate_up, w_down=w_down,
             ln2_g=ln2_g, ln2_b=ln2_b, w_ple=w_ple, w_ple_gate=w_ple_gate)
    m = dict(w_in=m_w_in, conv_w=m_conv_w, a_log=m_a_log, dt_bias=m_dt_bias, gdn_norm_g=m_gdn_norm_g, q_norm_g=m_q_norm_g,
             w_uq=m_w_uq, kv_norm_g=m_kv_norm_g, w_ukv=m_w_ukv, w_out=m_w_out, ln1_g=m_ln1_g, ln1_b=m_ln1_b,
             w_gate_up=m_w_gate_up, w_down=m_w_down, ln2_g=m_ln2_g, ln2_b=m_ln2_b, w_ple=m_w_ple, w_ple_gate=m_w_ple_gate)
    v = dict(w_in=v_w_in, conv_w=v_conv_w, a_log=v_a_log, dt_bias=v_dt_bias, gdn_norm_g=v_gdn_norm_g, q_norm_g=v_q_norm_g,
             w_uq=v_w_uq, kv_norm_g=v_kv_norm_g, w_ukv=v_w_ukv, w_out=v_w_out, ln1_g=v_ln1_g, ln1_b=v_ln1_b,
             w_gate_up=v_w_gate_up, w_down=v_w_down, ln2_g=v_ln2_g, ln2_b=v_ln2_b, w_ple=v_w_ple, w_ple_gate=v_w_ple_gate)
    depth = w_in.shape[0]
    assert depth % 2 == 0
    hd = depth // 2
    dm = _Dims(x.shape[2], N_CHIPS * w_in.shape[2], w_uq.shape[1], w_ukv.shape[1], N_CHIPS * w_gate_up.shape[2], p.shape[3])
    cx, cy, cc = lax.axis_index("x"), lax.axis_index("y"), lax.axis_index("c")
    chip = 2 * cx + cy

    g_streams = [_stream_of(n, w[n].shape[1:]) for n in MATRICES]
    shards = [w[n] if n == "conv_w" else w[n].astype(BF16) for n in MATRICES]
    shards = [_pad_lanes(s) if st.kind == "piece" else s for s, st in zip(shards, g_streams)]
    g_shapes = []
    for s, st in zip(shards, g_streams):
        if st.kind == "piece":
            shape = (N_CHIPS,) + s.shape
        elif st.kind == "rows":
            shape = (depth, N_CHIPS * s.shape[1], s.shape[2])
        else:
            shape = (depth, s.shape[1], N_CHIPS * s.shape[2])
        g_shapes.append(jax.ShapeDtypeStruct(shape, s.dtype))
    mats = dict(zip(MATRICES, _gather_chips(shards, g_streams, g_shapes, name="gather_weights")))
    for n, to_local in (("w_in", dm.w_in_local), ("w_uq", dm.w_uq_local)):
        pieces = jnp.moveaxis(mats[n][..., :w[n].shape[2]], 0, 2)
        mats[n] = to_local(pieces.reshape(pieces.shape[:2] + (-1,)))
    vecs = {n: w[n] for n in VECTORS}

    loss_local, grad_x, gbuf, conv_g, vec_g = _local_step(dm, x[0], p[:, 0], positions[0], loss_target[0], mats, vecs)
    loss = lax.psum(loss_local, ("x", "y", "c"))

    names = list(LOCAL_MATRICES) + ["conv_w", "vectors"]
    gs = [gbuf[n] for n in LOCAL_MATRICES] + [conv_g, _pack_vectors(vec_g, depth)]
    wire = [BF16] * len(LOCAL_MATRICES) + [F32, F32]
    r_streams = [_stream_of(n, w[n].shape[1:]) for n in LOCAL_MATRICES]
    r_streams += [_stream_of("conv_w", w["conv_w"].shape[1:]), _Stream("whole")]
    shard_shapes = [(hd, w[n].shape[1], _lane_padded(w[n].shape[2])) if st.kind == "piece" else (hd,) + w[n].shape[1:]
                    for n, st in zip(LOCAL_MATRICES, r_streams)]
    shard_shapes += [(hd,) + w["conv_w"].shape[1:], (hd, VEC_ROWS, LANES)]
    c_idx = cc.reshape(1).astype(jnp.int32)
    chip_idx = chip.reshape(1).astype(jnp.int32)
    from_sibling = _sibling_take_other_half(gs, name="reduce_sibling")
    chip_sum = [_add_own_half(g, a, c_idx, dt, name=f"reduce_add_{n}")
                for g, a, dt, n in zip(gs, from_sibling, wire, names)]
    for i, n in enumerate(names):
        if r_streams[i].kind == "piece":
            glob = dm.w_in_global(chip_sum[i]) if n == "w_in" else dm.w_uq_global(chip_sum[i])
            glob = glob.reshape(glob.shape[:2] + (N_CHIPS, glob.shape[2] // N_CHIPS))
            chip_sum[i] = jnp.moveaxis(_pad_lanes(glob), 2, 0)
    from_chips = _chips_exchange(chip_sum, r_streams, shard_shapes, name="reduce_chips")
    halves = [_sum_chips(ps, got, chip_idx, st, name=f"reduce_sum_{n}")
              for ps, got, st, n in zip(chip_sum, from_chips, r_streams, names)]
    joined = dict(zip(names, _sibling_join_halves(halves, name="reduce_join")))
    joined.update(_unpack_vectors(joined.pop("vectors"), {n: w[n].shape for n in VECTORS}))

    grad_w, delta_w, new_m, new_v = {}, {}, {}, {}
    for n in WEIGHTS:
        grad_w[n] = joined[n][..., :w[n].shape[-1]]
        delta_w[n], new_m[n], new_v[n] = _adamw(w[n], grad_w[n], m[n], v[n], name=f"adamw_{n}")
    return (loss, grad_x[None], *[grad_w[n] for n in WEIGHTS], *[delta_w[n] for n in WEIGHTS],
            *[new_m[n] for n in WEIGHTS], *[new_v[n] for n in WEIGHTS])
```

```python
import functools

import jax
import jax.numpy as jnp
from jax import lax
from jax.experimental import pallas as pl
from jax.experimental.pallas import tpu as pltpu

F32 = jnp.float32
BF16 = jnp.bfloat16
HIGH = lax.Precision.HIGH
MESH = pl.DeviceIdType.MESH

CHUNK = 64
N_HEADS = 4
HEAD_DIM = 128
ROPE_DIM = 64
ROPE_THETA = 10000.0
LN_EPS = 1e-5
RMS_EPS = 1e-6
ADAM_LR, ADAM_B1, ADAM_B2, ADAM_EPS, ADAM_WD, ADAM_STEP = 0.001, 0.9, 0.999, 1e-08, 0.01, 10

LANES = 128
VMEM_LIMIT = 48 * 1024 * 1024
PACK_W = 512
ROW_TILE = 256
SUB_ROWS = 16
MAX_SUB_ROWS = 64
VREG_FILE_ELEMS = 64 * 8 * LANES

MISC_BETA0 = ROPE_DIM
MISC_A0 = ROPE_DIM + N_HEADS

NN = (((1,), (0,)), ((), ()))
NT = (((1,), (1,)), ((), ()))
TN = (((0,), (0,)), ((), ()))


def _params(sem=None):
    return pltpu.CompilerParams(dimension_semantics=sem, vmem_limit_bytes=VMEM_LIMIT)


def _divisor_tile(dim, target, unit):
    best = None
    t = unit
    while t <= min(dim, target):
        if dim % t == 0:
            best = t
        t += unit
    return best if best is not None else dim


BATCHED = {NN: (((2,), (1,)), ((0,), (0,))), NT: (((2,), (2,)), ((0,), (0,))), TN: (((1,), (1,)), ((0,), (0,)))}


def _make_dots(high_precision):
    def raw(a, b, dims):
        if a.ndim == 3:
            dims = BATCHED[dims]
        if high_precision:
            return lax.dot_general(a, b, dims, precision=HIGH, preferred_element_type=F32)
        return lax.dot_general(a.astype(BF16), b.astype(BF16), dims, preferred_element_type=F32)

    @jax.custom_vjp
    def nn(a, b):
        return raw(a, b, NN)

    @jax.custom_vjp
    def nt(a, b):
        return raw(a, b, NT)

    @jax.custom_vjp
    def tn(a, b):
        return raw(a, b, TN)

    nn.defvjp(lambda a, b: (raw(a, b, NN), (a, b)), lambda r, g: (nt(g, r[1]), tn(r[0], g)))
    nt.defvjp(lambda a, b: (raw(a, b, NT), (a, b)), lambda r, g: (nn(g, r[1]), tn(g, r[0])))
    tn.defvjp(lambda a, b: (raw(a, b, TN), (a, b)), lambda r, g: (nt(r[1], g), nn(r[0], g)))
    return nn, nt, tn


_nn, _nt, _tn = _make_dots(False)
_hnn, _hnt, _htn = _make_dots(True)


def _matmul(a, b, *, dims, name, c=None, out_dtype=F32, tm=1024, tn=512, tk=1408, layer=None, into=None):
    b_shape = b.shape[-2:]
    if dims == "nn":
        (m, k), (k2, n) = a.shape, b_shape
    elif dims == "nt":
        (m, k), (n, k2) = a.shape, b_shape
    else:
        (k, m), (k2, n) = a.shape, b_shape
    assert k == k2, (a.shape, b.shape, dims)
    tm = _divisor_tile(m, tm, LANES)
    tn = _divisor_tile(n, tn, LANES)
    tk = _divisor_tile(k, tk, LANES)
    nk = k // tk
    dn = {"nn": NN, "nt": NT, "tn": TN}[dims]
    if dims == "tn":
        a_spec = pl.BlockSpec((tk, tm), lambda i, j, kk: (kk, i))
    else:
        a_spec = pl.BlockSpec((tm, tk), lambda i, j, kk: (i, kk))
    b_blk, b_idx = ((tn, tk), lambda i, j, kk: (j, kk)) if dims == "nt" else ((tk, tn), lambda i, j, kk: (kk, j))
    if b.ndim == 3:
        b_spec = pl.BlockSpec((None,) + b_blk, lambda i, j, kk: (layer,) + b_idx(i, j, kk))
    else:
        b_spec = pl.BlockSpec(b_blk, b_idx)
    c_spec = pl.BlockSpec((tm, tn), lambda i, j, kk: (i, j))
    if into is not None:
        assert into.shape[1:] == (m, n) and into.dtype == out_dtype
        o_spec = pl.BlockSpec((None, tm, tn), lambda i, j, kk: (layer, i, j))
        out_shape = jax.ShapeDtypeStruct(into.shape, into.dtype)
    else:
        o_spec = c_spec
        out_shape = jax.ShapeDtypeStruct((m, n), out_dtype)
    has_c = c is not None

    def body(*refs):
        a_ref, b_ref = refs[:2]
        c_ref = refs[2] if has_c else None
        o_ref, acc_ref = refs[-2:]
        kk = pl.program_id(2)

        @pl.when(kk == 0)
        def _():
            if has_c:
                acc_ref[...] = c_ref[...].astype(F32)
            else:
                acc_ref[...] = jnp.zeros_like(acc_ref)

        acc_ref[...] += lax.dot_general(a_ref[...].astype(BF16), b_ref[...].astype(BF16), dn,
                                        preferred_element_type=F32)

        @pl.when(kk == nk - 1)
        def _():
            o_ref[...] = acc_ref[...].astype(o_ref.dtype)

    ins = [a, b] + ([c] if has_c else [])
    specs = [a_spec, b_spec] + ([c_spec] if has_c else [])
    aliases = {}
    if into is not None:
        aliases = {len(ins): 0}
        ins.append(into)
        specs.append(pl.BlockSpec(memory_space=pl.ANY))
    return pl.pallas_call(
        body, name=name, grid=(m // tm, n // tn, nk), in_specs=specs, out_specs=o_spec, out_shape=out_shape,
        scratch_shapes=[pltpu.VMEM((tm, tn), F32)], input_output_aliases=aliases,
        compiler_params=_params(("arbitrary", "arbitrary", "arbitrary")),
    )(*ins)


def _rowwise(fn, rows, params, outs, accs=(), *, name, tm=ROW_TILE):
    t = rows[0][0].shape[0]
    tm = min(tm, t)
    widest = max([w for _, w, _ in rows] + [w for w, _ in outs])
    sub = SUB_ROWS
    while sub < MAX_SUB_ROWS and 2 * sub * widest <= VREG_FILE_ELEMS:
        sub *= 2
    assert t % tm == 0 and tm % sub == 0
    n_rows, n_par, n_out, n_acc = len(rows), len(params), len(outs), len(accs)

    def body(*refs):
        row_refs = refs[:n_rows]
        par_refs = refs[n_rows:n_rows + n_par]
        out_refs = refs[n_rows + n_par:n_rows + n_par + n_out]
        acc_refs = refs[n_rows + n_par + n_out:]
        if n_acc:
            @pl.when(pl.program_id(0) == 0)
            def _():
                for a_ref in acc_refs:
                    a_ref[...] = jnp.zeros_like(a_ref)

        def step(r, carry):
            sl = pl.ds(pl.multiple_of(r * sub, sub), sub)
            vals = [ref[sl, :].astype(F32) for ref in row_refs] + [ref[...] for ref in par_refs]
            res = fn(*vals)
            for o_ref, val in zip(out_refs, res[:n_out]):
                o_ref[sl, :] = val.astype(o_ref.dtype)
            for a_ref, val in zip(acc_refs, res[n_out:]):
                a_ref[...] += val
            return carry

        lax.fori_loop(0, tm // sub, step, 0)

    in_specs = [pl.BlockSpec((tm, w), functools.partial(lambda i, cb: (i, cb), cb=cb)) for _, w, cb in rows]
    in_specs += [pl.BlockSpec(p.shape, lambda i: (0, 0)) for p in params]
    out_specs = [pl.BlockSpec((tm, w), lambda i: (i, 0)) for w, _ in outs]
    out_specs += [pl.BlockSpec(s, lambda i: (0, 0)) for s in accs]
    out_shape = [jax.ShapeDtypeStruct((t, w), d) for w, d in outs]
    out_shape += [jax.ShapeDtypeStruct(s, F32) for s in accs]
    return pl.pallas_call(
        body, name=name, grid=(t // tm,), in_specs=in_specs, out_specs=out_specs, out_shape=out_shape,
        compiler_params=_params(("arbitrary",)),
    )(*[r[0] for r in rows], *params)


def _vjp_fn(fn, n_in, n_out):
    def bwd(*args):
        ins, cts = args[:n_in], args[n_in:]
        _, pull = jax.vjp(fn, *ins)
        return pull(tuple(cts) if n_out > 1 else cts[0])
    return bwd


def _lane(shape):
    return lax.broadcasted_iota(jnp.int32, shape, 1)


def _silu(x):
    return x * jax.nn.sigmoid(x)


def _softplus(x):
    return jnp.maximum(x, 0.0) + jnp.log1p(jnp.exp(-jnp.abs(x)))


def _heads(x, width=HEAD_DIM):
    return [x[:, h * width:(h + 1) * width] for h in range(N_HEADS)]


def _layer_norm(z, g, b):
    mu = jnp.mean(z, -1, keepdims=True)
    zc = z - mu
    var = jnp.mean(zc * zc, -1, keepdims=True)
    return zc * lax.rsqrt(var + LN_EPS) * g + b


def _gdn_act(u, misc, alog_row, dtb_row):
    s = _silu(u)
    w = N_HEADS * HEAD_DIM
    q = jnp.concatenate([t * lax.rsqrt(jnp.sum(t * t, -1, keepdims=True) + RMS_EPS) * HEAD_DIM ** -0.5
                         for t in _heads(s[:, :w])], axis=1)
    k = jnp.concatenate([t * lax.rsqrt(jnp.sum(t * t, -1, keepdims=True) + RMS_EPS)
                         for t in _heads(s[:, w:2 * w])], axis=1)
    v = s[:, 2 * w:]
    lane = _lane(misc.shape)
    beta = jax.nn.sigmoid(misc)
    g = -jnp.exp(alog_row) * _softplus(misc + dtb_row)
    is_beta = (lane >= MISC_BETA0) & (lane < MISC_BETA0 + N_HEADS)
    is_g = (lane >= MISC_A0) & (lane < MISC_A0 + N_HEADS)
    gb = jnp.where(is_beta, beta, jnp.where(is_g, g, 0.0))
    return q, k, v, gb


def _gdn_out(o, z, gn_row):
    outs = []
    for oh, zh in zip(_heads(o), _heads(z)):
        r = oh * lax.rsqrt(jnp.mean(oh * oh, -1, keepdims=True) + RMS_EPS) * gn_row
        outs.append(r * _silu(zh))
    return jnp.concatenate(outs, axis=1)


def _mla_norm(ckv, cq, kvg_row, qg_row):
    cqn = cq * lax.rsqrt(jnp.mean(cq * cq, -1, keepdims=True) + RMS_EPS) * qg_row
    ckvn = ckv * lax.rsqrt(jnp.mean(ckv * ckv, -1, keepdims=True) + RMS_EPS) * kvg_row
    return cqn, ckvn


def _swap_halves(x):
    half = ROPE_DIM // 2
    return jnp.where(_lane(x.shape) < half, pltpu.roll(x, LANES - half, 1), pltpu.roll(x, half, 1))


@jax.custom_vjp
def _rope(x, cos_t, sin_t):
    return x * cos_t + _swap_halves(x) * sin_t


def _rope_fwd(x, cos_t, sin_t):
    return _rope(x, cos_t, sin_t), (cos_t, sin_t)


def _rope_bwd(res, g):
    cos_t, sin_t = res
    return g * cos_t - _swap_halves(g) * sin_t, jnp.zeros_like(cos_t), jnp.zeros_like(sin_t)


_rope.defvjp(_rope_fwd, _rope_bwd)


def _mla_qk(scale, qm, kv, misc, cos_t, sin_t):
    krope = _rope(misc, cos_t, sin_t)
    qs, ks = [], []
    for h in range(N_HEADS):
        base = 2 * HEAD_DIM * h
        qs += [qm[:, base:base + HEAD_DIM], _rope(qm[:, base + HEAD_DIM:base + 2 * HEAD_DIM], cos_t, sin_t)]
        ks += [kv[:, HEAD_DIM * h:HEAD_DIM * (h + 1)], krope]
    return jnp.concatenate(qs, axis=1) * scale, jnp.concatenate(ks, axis=1), kv[:, N_HEADS * HEAD_DIM:]


def _swiglu(gu):
    f = gu.shape[1] // 2
    return _silu(gu[:, :f]) * gu[:, f:]


def _ple_out(x2, pg, pe):
    return x2 + jax.nn.sigmoid(pg) * pe


CONV_W = 4
HALO = 8


def _conv_fwd(h, conv_w, width, *, name, tm=ROW_TILE, sub=32):
    t = h.shape[0]
    tm = min(tm, t)
    nb = tm // HALO

    def body(x_ref, halo_ref, w_ref, u_ref, buf):
        i = pl.program_id(0)
        buf[pl.ds(0, HALO), :] = jnp.where(i > 0, halo_ref[...], 0.0)
        buf[pl.ds(HALO, tm), :] = x_ref[...]
        w = w_ref[...]
        for r0 in range(0, tm, sub):
            acc = jnp.zeros((sub, width), F32)
            for j in range(CONV_W):
                acc = acc + w[j:j + 1, :] * buf[pl.ds(HALO + r0 - (CONV_W - 1) + j, sub), :]
            u_ref[pl.ds(r0, sub), :] = acc

    return pl.pallas_call(
        body, name=name, grid=(t // tm,),
        in_specs=[pl.BlockSpec((tm, width), lambda i: (i, 0)),
                  pl.BlockSpec((HALO, width), lambda i: (jnp.maximum(i * nb - 1, 0), 0)),
                  pl.BlockSpec(conv_w.shape, lambda i: (0, 0))],
        out_specs=pl.BlockSpec((tm, width), lambda i: (i, 0)),
        out_shape=jax.ShapeDtypeStruct((t, width), F32),
        scratch_shapes=[pltpu.VMEM((tm + HALO, width), F32)],
        compiler_params=_params(("arbitrary",)),
    )(h, h, conv_w)


def _conv_bwd(du, h, conv_w, width, *, name, tm=ROW_TILE, sub=32):
    t = h.shape[0]
    tm = min(tm, t)
    nb = tm // HALO
    n_tiles = t // tm

    def body(du_ref, du_halo, x_ref, x_halo, w_ref, dx_ref, dw_ref, dbuf, xbuf):
        i = pl.program_id(0)

        @pl.when(i == 0)
        def _():
            dw_ref[...] = jnp.zeros_like(dw_ref)

        dbuf[pl.ds(0, tm), :] = du_ref[...]
        dbuf[pl.ds(tm, HALO), :] = jnp.where(i < n_tiles - 1, du_halo[...], 0.0)
        xbuf[pl.ds(0, HALO), :] = jnp.where(i > 0, x_halo[...], 0.0)
        xbuf[pl.ds(HALO, tm), :] = x_ref[...]
        w = w_ref[...]
        dws = [jnp.zeros((1, width), F32) for _ in range(CONV_W)]
        for r0 in range(0, tm, sub):
            acc = jnp.zeros((sub, width), F32)
            d_here = dbuf[pl.ds(r0, sub), :]
            for j in range(CONV_W):
                acc = acc + w[j:j + 1, :] * dbuf[pl.ds(r0 + (CONV_W - 1) - j, sub), :]
                xs = xbuf[pl.ds(HALO + r0 - (CONV_W - 1) + j, sub), :]
                dws[j] = dws[j] + jnp.sum(d_here * xs, axis=0, keepdims=True)
            dx_ref[pl.ds(r0, sub), :] = acc.astype(dx_ref.dtype)
        for j in range(CONV_W):
            dw_ref[pl.ds(j, 1), :] += dws[j]

    return pl.pallas_call(
        body, name=name, grid=(n_tiles,),
        in_specs=[pl.BlockSpec((tm, width), lambda i: (i, 0)),
                  pl.BlockSpec((HALO, width), lambda i: (jnp.minimum((i + 1) * nb, t // HALO - 1), 0)),
                  pl.BlockSpec((tm, width), lambda i: (i, 0)),
                  pl.BlockSpec((HALO, width), lambda i: (jnp.maximum(i * nb - 1, 0), 0)),
                  pl.BlockSpec(conv_w.shape, lambda i: (0, 0))],
        out_specs=[pl.BlockSpec((tm, width), lambda i: (i, 0)),
                   pl.BlockSpec((HALO, width), lambda i: (0, 0))],
        out_shape=[jax.ShapeDtypeStruct((t, width), BF16), jax.ShapeDtypeStruct((HALO, width), F32)],
        scratch_shapes=[pltpu.VMEM((tm + HALO, width), F32), pltpu.VMEM((tm + HALO, width), F32)],
        compiler_params=_params(("arbitrary",)),
    )(du, du, h, h, conv_w)


@jax.custom_vjp
def _inv_unit_lower(low):
    n = low.shape[-1]
    eye = (lax.broadcasted_iota(jnp.int32, (n, n), 0) == lax.broadcasted_iota(jnp.int32, (n, n), 1)).astype(F32)
    x = eye - low
    p = low
    span = 2
    while span < n:
        p = _hnn(p, p)
        x = x + _hnn(x, p)
        span *= 2
    return x


def _inv_fwd(low):
    x = _inv_unit_lower(low)
    return x, x


def _inv_bwd(x, g):
    return (-_htn(x, _hnt(g, x)),)


_inv_unit_lower.defvjp(_inv_fwd, _inv_bwd)


def _gdn_prep(q, k, v, gb):
    c = CHUNK
    n = q.shape[0] // c
    pairs = [(g, h) for g in range(n) for h in range(N_HEADS)]
    row = lax.broadcasted_iota(jnp.int32, (c, c), 0)
    col = lax.broadcasted_iota(jnp.int32, (c, c), 1)
    tri_incl = row >= col
    tri_strict = row > col
    lane = _lane((c, LANES))
    sub = lax.broadcasted_iota(jnp.int32, (LANES, c), 0)
    last = lax.broadcasted_iota(jnp.int32, (c, 1), 0) == c - 1

    def split(x):
        return jnp.stack([x[g * c:(g + 1) * c, h * HEAD_DIM:(h + 1) * HEAD_DIM] for g, h in pairs])

    gbs = [gb[g * c:(g + 1) * c, :] for g in range(n)]
    gbts = [x.T for x in gbs]
    g_col = jnp.stack([jnp.sum(jnp.where(lane == MISC_A0 + h, gbs[g], 0.0), axis=1, keepdims=True) for g, h in pairs])
    b_col = jnp.stack([jnp.sum(jnp.where(lane == MISC_BETA0 + h, gbs[g], 0.0), axis=1, keepdims=True) for g, h in pairs])
    g_row = jnp.stack([jnp.sum(jnp.where(sub == MISC_A0 + h, gbts[g], 0.0), axis=0, keepdims=True) for g, h in pairs])
    gc_col = jnp.sum(jnp.where(tri_incl, g_row, 0.0), axis=2, keepdims=True)
    gc_row = jnp.sum(jnp.where(row <= col, g_col, 0.0), axis=1, keepdims=True)
    decay = jnp.where(tri_incl, jnp.exp(jnp.where(tri_incl, gc_col - gc_row, 0.0)), 0.0)
    g_last = jnp.sum(jnp.where(last, gc_col, 0.0), axis=1, keepdims=True)
    qs, ks, vs = split(q), split(k), split(v)
    kb = ks * b_col
    low = jnp.where(tri_strict, _nt(kb, ks) * decay, 0.0)
    tinv = _inv_unit_lower(low)
    eg = jnp.exp(gc_col)
    sol = _hnn(tinv, jnp.concatenate([vs * b_col, kb * eg], axis=2))
    attn = jnp.where(tri_incl, _nt(qs, ks) * decay, 0.0)
    qd = qs * eg
    kd = ks * jnp.exp(g_last - gc_col)

    def merge(x):
        return jnp.concatenate([jnp.concatenate([x[g * N_HEADS + h] for h in range(N_HEADS)], axis=1)
                                for g in range(n)], axis=0)

    glb = jnp.concatenate([sum(jnp.where(lane == h, g_last[g * N_HEADS + h], 0.0) for h in range(N_HEADS))
                           for g in range(n)], axis=0)
    return merge(sol[:, :, :HEAD_DIM]), merge(sol[:, :, HEAD_DIM:]), merge(qd), merge(kd), merge(attn), glb


def _gdn_seq(state, u, w, qd, kd, attn, glb):
    c = u.shape[0]
    first = lax.broadcasted_iota(jnp.int32, glb.shape, 0) == 0
    lane = _lane(glb.shape)
    heads = lambda x: jnp.stack([x[:, h * HEAD_DIM:(h + 1) * HEAD_DIM] for h in range(N_HEADS)])
    g_last = jnp.stack([jnp.sum(jnp.sum(jnp.where(first & (lane == h), glb, 0.0), axis=1, keepdims=True),
                                axis=0, keepdims=True) for h in range(N_HEADS)])
    s = jnp.stack([state[h * HEAD_DIM:(h + 1) * HEAD_DIM, :] for h in range(N_HEADS)])
    at = jnp.stack([attn[:, h * c:(h + 1) * c] for h in range(N_HEADS)])
    v_new = heads(u) - _nn(heads(w), s)
    o = _nn(heads(qd), s) + _nn(at, v_new)
    s_new = s * jnp.exp(g_last) + _tn(heads(kd), v_new)
    return (jnp.concatenate([o[h] for h in range(N_HEADS)], axis=1),
            jnp.concatenate([s_new[h] for h in range(N_HEADS)], axis=0))


PREP_CHUNKS = 2
SEQ_CHUNKS = 8


def _gdn_prep_fwd(q, k, v, gb, *, name):
    t, w = q.shape
    rows = min(PREP_CHUNKS * CHUNK, t)

    def body(q_ref, k_ref, v_ref, gb_ref, *out_refs):
        res = _gdn_prep(q_ref[...], k_ref[...], v_ref[...], gb_ref[...])
        for o_ref, val in zip(out_refs, res):
            o_ref[...] = val

    spec = lambda width: pl.BlockSpec((rows, width), lambda i: (i, 0))
    widths = [w, w, w, w, N_HEADS * CHUNK, LANES]
    return pl.pallas_call(
        body, name=name, grid=(t // rows,),
        in_specs=[spec(w), spec(w), spec(w), spec(LANES)],
        out_specs=[spec(x) for x in widths],
        out_shape=[jax.ShapeDtypeStruct((t, x), F32) for x in widths],
        compiler_params=_params(("arbitrary",)),
    )(q, k, v, gb)


def _gdn_prep_bwd(q, k, v, gb, cts, *, name):
    t, w = q.shape
    rows = min(PREP_CHUNKS * CHUNK, t)

    def body(q_ref, k_ref, v_ref, gb_ref, du, dw, dqd, dkd, dattn, dglb, dq_ref, dk_ref, dv_ref, dgb_ref):
        _, pull = jax.vjp(_gdn_prep, q_ref[...], k_ref[...], v_ref[...], gb_ref[...])
        dq, dk, dv, dgb = pull(tuple(r[...] for r in (du, dw, dqd, dkd, dattn, dglb)))
        dq_ref[...] = dq
        dk_ref[...] = dk
        dv_ref[...] = dv
        dgb_ref[...] = dgb

    spec = lambda width: pl.BlockSpec((rows, width), lambda i: (i, 0))
    widths = [w, w, w, w, N_HEADS * CHUNK, LANES]
    return pl.pallas_call(
        body, name=name, grid=(t // rows,),
        in_specs=[spec(w), spec(w), spec(w), spec(LANES)] + [spec(x) for x in widths],
        out_specs=[spec(w), spec(w), spec(w), spec(LANES)],
        out_shape=[jax.ShapeDtypeStruct((t, w), F32)] * 3 + [jax.ShapeDtypeStruct((t, LANES), F32)],
        compiler_params=_params(("arbitrary",)),
    )(q, k, v, gb, *cts)


def _gdn_seq_fwd(prep, *, name):
    t, w = prep[0].shape
    rows = min(SEQ_CHUNKS * CHUNK, t)
    per = rows // CHUNK

    def body(u_ref, w_ref, qd_ref, kd_ref, at_ref, gl_ref, o_ref, sall_ref, s_scr):
        @pl.when(pl.program_id(0) == 0)
        def _():
            s_scr[...] = jnp.zeros_like(s_scr)

        def step(j, carry):
            sl = pl.ds(pl.multiple_of(j * CHUNK, CHUNK), CHUNK)
            s = s_scr[...]
            sall_ref[j] = s
            o, s_new = _gdn_seq(s, u_ref[sl, :], w_ref[sl, :], qd_ref[sl, :], kd_ref[sl, :], at_ref[sl, :], gl_ref[sl, :])
            o_ref[sl, :] = o
            s_scr[...] = s_new
            return carry

        lax.fori_loop(0, per, step, 0)

    spec = lambda width: pl.BlockSpec((rows, width), lambda i: (i, 0))
    widths = [w, w, w, w, N_HEADS * CHUNK, LANES]
    return pl.pallas_call(
        body, name=name, grid=(t // rows,),
        in_specs=[spec(x) for x in widths],
        out_specs=[spec(w), pl.BlockSpec((per, w, HEAD_DIM), lambda i: (i, 0, 0))],
        out_shape=[jax.ShapeDtypeStruct((t, w), F32), jax.ShapeDtypeStruct((t // CHUNK, w, HEAD_DIM), F32)],
        scratch_shapes=[pltpu.VMEM((w, HEAD_DIM), F32)],
        compiler_params=_params(("arbitrary",)),
    )(*prep)


def _gdn_seq_bwd(prep, s_all, do, *, name):
    t, w = prep[0].shape
    rows = min(SEQ_CHUNKS * CHUNK, t)
    per = rows // CHUNK
    n = t // rows

    def body(u_ref, w_ref, qd_ref, kd_ref, at_ref, gl_ref, sall_ref, do_ref, du, dw, dqd, dkd, dat, dgl, ds_scr):
        @pl.when(pl.program_id(0) == 0)
        def _():
            ds_scr[...] = jnp.zeros_like(ds_scr)

        def step(jj, carry):
            j = per - 1 - jj
            sl = pl.ds(pl.multiple_of(j * CHUNK, CHUNK), CHUNK)
            _, pull = jax.vjp(_gdn_seq, sall_ref[j], u_ref[sl, :], w_ref[sl, :], qd_ref[sl, :], kd_ref[sl, :],
                              at_ref[sl, :], gl_ref[sl, :])
            res = pull((do_ref[sl, :], ds_scr[...]))
            ds_scr[...] = res[0]
            for o_ref, val in zip((du, dw, dqd, dkd, dat, dgl), res[1:]):
                o_ref[sl, :] = val
            return carry

        lax.fori_loop(0, per, step, 0)

    spec = lambda width: pl.BlockSpec((rows, width), lambda i: (n - 1 - i, 0))
    widths = [w, w, w, w, N_HEADS * CHUNK, LANES]
    return pl.pallas_call(
        body, name=name, grid=(n,),
        in_specs=[spec(x) for x in widths] + [pl.BlockSpec((per, w, HEAD_DIM), lambda i: (n - 1 - i, 0, 0)), spec(w)],
        out_specs=[spec(x) for x in widths],
        out_shape=[jax.ShapeDtypeStruct((t, x), F32) for x in widths],
        scratch_shapes=[pltpu.VMEM((w, HEAD_DIM), F32)],
        compiler_params=_params(("arbitrary",)),
    )(*prep, s_all, do)


QK_DIM = 2 * HEAD_DIM
ATT_TILE = 512
NEG = -1e30


ATT_SPLIT = 2


def _chunk_mask(n_rows, n_cols, key_major, query_offset):
    r = lax.broadcasted_iota(jnp.int32, (n_rows, n_cols), 0)
    c = lax.broadcasted_iota(jnp.int32, (n_rows, n_cols), 1)
    if key_major:
        return r // CHUNK <= (c + query_offset) // CHUNK
    return c // CHUNK <= (r + query_offset) // CHUNK


def _dot_nt(a, b):
    return lax.dot_general(a, b, NT, preferred_element_type=F32)


def _dot_nn(a, b):
    return lax.dot_general(a, b, NN, preferred_element_type=F32)


def _blocked_transpose(x, width):
    t = x.shape[0]
    tile = min(ATT_TILE, t)
    return x.reshape(t // tile, tile, N_HEADS, width).transpose(2, 0, 3, 1)


def _attn_fwd(q, kt, v1, *, name):
    t = q.shape[0]
    tq = min(ATT_TILE, t)
    nq = t // tq

    def body(q_ref, kt_ref, v_ref, o_ref, lse_ref, m_scr, acc_scr):
        qi = pl.program_id(1)
        m_scr[...] = jnp.full_like(m_scr, NEG)
        acc_scr[...] = jnp.zeros_like(acc_scr)
        hq = tq // ATT_SPLIT
        parts = [pl.ds(a * hq, hq) for a in range(ATT_SPLIT)]
        qs = [q_ref[sl, :] for sl in parts]

        def step(kj, masked):
            rows = pl.ds(pl.multiple_of(kj * tq, tq), tq)
            kt_blk, vv = kt_ref[kj], v_ref[rows, :]
            ss = [_dot_nn(qv, kt_blk) for qv in qs]
            for a, sl in enumerate(parts):
                s = ss[a]
                if masked:
                    s = jnp.where(_chunk_mask(hq, tq, False, a * hq), s, NEG)
                m_old = m_scr[sl, :]
                m_new = jnp.maximum(m_old, jnp.max(s, axis=1, keepdims=True))
                p = jnp.exp(s - m_new)
                acc_scr[sl, :] = jnp.exp(m_old - m_new) * acc_scr[sl, :] + _dot_nn(p.astype(BF16), vv)
                m_scr[sl, :] = m_new

        def loop_body(kj, carry):
            step(kj, False)
            return carry

        lax.fori_loop(0, qi, loop_body, 0)
        step(qi, True)
        acc = acc_scr[...]
        o_ref[...] = (acc[:, :HEAD_DIM] / acc[:, HEAD_DIM:]).astype(o_ref.dtype)
        lse_ref[...] = m_scr[...] + jnp.log(acc[:, HEAD_DIM:HEAD_DIM + 1])

    return pl.pallas_call(
        body, name=name, grid=(N_HEADS, nq),
        in_specs=[pl.BlockSpec((tq, QK_DIM), lambda h, i: (i, h)),
                  pl.BlockSpec((None, nq, QK_DIM, tq), lambda h, i: (h, 0, 0, 0)),
                  pl.BlockSpec((t, 2 * HEAD_DIM), lambda h, i: (0, h))],
        out_specs=[pl.BlockSpec((tq, HEAD_DIM), lambda h, i: (i, h)),
                   pl.BlockSpec((None, tq, 1), lambda h, i: (h, i, 0))],
        out_shape=[jax.ShapeDtypeStruct((t, N_HEADS * HEAD_DIM), BF16),
                   jax.ShapeDtypeStruct((N_HEADS, t, 1), F32)],
        scratch_shapes=[pltpu.VMEM((tq, 1), F32), pltpu.VMEM((tq, 2 * HEAD_DIM), F32)],
        compiler_params=_params(("arbitrary", "arbitrary")),
    )(q, kt, v1)


def _attn_bwd_dq(q, k, kt, vt, o, lse, dom, do_col0, *, name):
    t = q.shape[0]
    tq = min(ATT_TILE, t)
    nq = t // tq

    def body(q_ref, k_ref, kt_ref, vt_ref, o_ref, lse_ref, do_ref, dq_ref, delta_ref, acc_scr):
        qi = pl.program_id(1)
        acc_scr[...] = jnp.zeros_like(acc_scr)
        do = do_ref[...]
        delta = jnp.sum(do * o_ref[...].astype(F32), axis=1, keepdims=True)
        delta_ref[...] = delta
        hq = tq // ATT_SPLIT
        parts = [pl.ds(a * hq, hq) for a in range(ATT_SPLIT)]
        qs = [q_ref[sl, :] for sl in parts]
        dos = [do_ref[sl, :].astype(BF16) for sl in parts]
        lses = [lse_ref[sl, :] for sl in parts]
        deltas = [delta[a * hq:(a + 1) * hq, :] for a in range(ATT_SPLIT)]

        def step(kj, masked):
            rows = pl.ds(pl.multiple_of(kj * tq, tq), tq)
            kt_blk, vt_blk = kt_ref[kj], vt_ref[kj]
            ss = [_dot_nn(qv, kt_blk) for qv in qs]
            dps = [_dot_nn(do_b, vt_blk) for do_b in dos]
            kv_ = k_ref[rows, :]
            for a, sl in enumerate(parts):
                p = jnp.exp(ss[a] - lses[a])
                if masked:
                    p = jnp.where(_chunk_mask(hq, tq, False, a * hq), p, 0.0)
                ds = p * (dps[a] - deltas[a])
                acc_scr[sl, :] += _dot_nn(ds.astype(BF16), kv_)

        def loop_body(kj, carry):
            step(kj, False)
            return carry

        lax.fori_loop(0, qi, loop_body, 0)
        step(qi, True)
        dq_ref[...] = acc_scr[...].astype(dq_ref.dtype)

    return pl.pallas_call(
        body, name=name, grid=(N_HEADS, nq),
        in_specs=[pl.BlockSpec((tq, QK_DIM), lambda h, i: (i, h)),
                  pl.BlockSpec((t, QK_DIM), lambda h, i: (0, h)),
                  pl.BlockSpec((None, nq, QK_DIM, tq), lambda h, i: (h, 0, 0, 0)),
                  pl.BlockSpec((None, nq, HEAD_DIM, tq), lambda h, i: (h, 0, 0, 0)),
                  pl.BlockSpec((tq, HEAD_DIM), lambda h, i: (i, h)),
                  pl.BlockSpec((None, tq, 1), lambda h, i: (h, i, 0)),
                  pl.BlockSpec((tq, HEAD_DIM), lambda h, i: (i, do_col0 + h))],
        out_specs=[pl.BlockSpec((tq, QK_DIM), lambda h, i: (i, h)),
                   pl.BlockSpec((None, tq, 1), lambda h, i: (h, i, 0))],
        out_shape=[jax.ShapeDtypeStruct((t, N_HEADS * QK_DIM), BF16),
                   jax.ShapeDtypeStruct((N_HEADS, t, 1), F32)],
        scratch_shapes=[pltpu.VMEM((tq, QK_DIM), F32)],
        compiler_params=_params(("arbitrary", "arbitrary")),
    )(q, k, kt, vt, o, lse, dom)


def _attn_bwd_dkv(q, qt, k, v, lse_row, delta_row, dom, dot, do_col0, *, name):
    t = q.shape[0]
    tk = min(ATT_TILE, t)
    nk = t // tk

    def body(q_ref, qt_ref, k_ref, v_ref, lse_ref, delta_ref, do_ref, dot_ref, dk_ref, dv_ref, dk_scr, dv_scr):
        kj = pl.program_id(1)
        dk_scr[...] = jnp.zeros_like(dk_scr)
        dv_scr[...] = jnp.zeros_like(dv_scr)
        kv_ = k_ref[...]
        vv = v_ref[...]

        hq = tk // ATT_SPLIT

        def step(qi, masked):
            lse_v, delta_v = lse_ref[qi], delta_ref[qi]
            qt_blk, dot_blk = qt_ref[qi], dot_ref[qi]
            qs, dos = [], []
            for a in range(ATT_SPLIT):
                rows = pl.ds(pl.multiple_of(qi * tk + a * hq, hq), hq)
                qs.append(q_ref[rows, :])
                dos.append(do_ref[rows, :].astype(BF16))
            ss = [_dot_nn(kv_, qt_blk[:, a * hq:(a + 1) * hq]) for a in range(ATT_SPLIT)]
            dps = [_dot_nn(vv, dot_blk[:, a * hq:(a + 1) * hq]) for a in range(ATT_SPLIT)]
            for a in range(ATT_SPLIT):
                cols = slice(a * hq, (a + 1) * hq)
                p = jnp.exp(ss[a] - lse_v[:, cols])
                if masked:
                    p = jnp.where(_chunk_mask(tk, hq, True, a * hq), p, 0.0)
                dv_scr[...] += _dot_nn(p.astype(BF16), dos[a])
                ds = p * (dps[a] - delta_v[:, cols])
                dk_scr[...] += _dot_nn(ds.astype(BF16), qs[a])

        step(kj, True)

        def loop_body(qi, carry):
            step(qi, False)
            return carry

        lax.fori_loop(kj + 1, nk, loop_body, 0)
        dk_ref[...] = dk_scr[...].astype(dk_ref.dtype)
        dv_ref[...] = dv_scr[...].astype(dv_ref.dtype)

    stat = pl.BlockSpec((None, nk, 1, tk), lambda h, j: (h, 0, 0, 0))
    return pl.pallas_call(
        body, name=name, grid=(N_HEADS, nk),
        in_specs=[pl.BlockSpec((t, QK_DIM), lambda h, j: (0, h)),
                  pl.BlockSpec((None, nk, QK_DIM, tk), lambda h, j: (h, 0, 0, 0)),
                  pl.BlockSpec((tk, QK_DIM), lambda h, j: (j, h)),
                  pl.BlockSpec((tk, HEAD_DIM), lambda h, j: (j, h)),
                  stat, stat,
                  pl.BlockSpec((t, HEAD_DIM), lambda h, j: (0, do_col0 + h)),
                  pl.BlockSpec((None, nk, HEAD_DIM, tk), lambda h, j: (h, 0, 0, 0))],
        out_specs=[pl.BlockSpec((tk, QK_DIM), lambda h, j: (j, h)),
                   pl.BlockSpec((tk, HEAD_DIM), lambda h, j: (j, h))],
        out_shape=[jax.ShapeDtypeStruct((t, N_HEADS * QK_DIM), BF16),
                   jax.ShapeDtypeStruct((t, N_HEADS * HEAD_DIM), BF16)],
        scratch_shapes=[pltpu.VMEM((tk, QK_DIM), F32), pltpu.VMEM((tk, HEAD_DIM), F32)],
        compiler_params=_params(("arbitrary", "arbitrary")),
    )(q, qt, k, v, lse_row, delta_row, dom, dot)


def _rope_tables(pos_col, inv_freq_row, *, name):
    t = pos_col.shape[0]
    tm = min(ROW_TILE, t)

    def body(p_ref, f_ref, c_ref, s_ref):
        ang = p_ref[...].astype(F32) * f_ref[...]
        lane = _lane(ang.shape)
        c_ref[...] = jnp.where(lane < ROPE_DIM, jnp.cos(ang), 0.0)
        sn = jnp.sin(ang)
        s_ref[...] = jnp.where(lane < ROPE_DIM // 2, -sn, jnp.where(lane < ROPE_DIM, sn, 0.0))

    out = pl.BlockSpec((tm, LANES), lambda i: (i, 0))
    return pl.pallas_call(
        body, name=name, grid=(t // tm,),
        in_specs=[pl.BlockSpec((tm, 1), lambda i: (i, 0)), pl.BlockSpec((1, LANES), lambda i: (0, 0))],
        out_specs=[out, out], out_shape=[jax.ShapeDtypeStruct((t, LANES), F32)] * 2,
        compiler_params=_params(("arbitrary",)),
    )(pos_col, inv_freq_row)


def _loss_head(y, target):
    width = y.shape[1]

    def fn(yv, tv):
        e = yv - tv
        part = 0.5 * jnp.sum(jnp.mean(e * e, axis=1, keepdims=True), axis=0, keepdims=True)
        return e * (1.0 / width), jnp.broadcast_to(part, (HALO, LANES))

    return _rowwise(fn, [(y, width, 0), (target, width, 0)], [], [(width, F32)], [(HALO, LANES)], name="loss_head")


def _adamw(w, g, m, v, *, name):
    shape = w.shape
    w2, g2, m2, v2 = (a.reshape(-1, shape[-1]) for a in (w, g, m, v))
    rows, width = w2.shape
    tr = _divisor_tile(rows, max(8, (1 << 19) // max(width, 1)), 8)
    bc1 = 1.0 - ADAM_B1 ** ADAM_STEP
    bc2 = 1.0 - ADAM_B2 ** ADAM_STEP

    def body(w_ref, g_ref, m_ref, v_ref, d_ref, mo_ref, vo_ref):
        gv = g_ref[...]
        mn = ADAM_B1 * m_ref[...] + (1.0 - ADAM_B1) * gv
        vn = ADAM_B2 * v_ref[...] + (1.0 - ADAM_B2) * (gv * gv)
        d_ref[...] = -ADAM_LR * ((mn / bc1) / (jnp.sqrt(vn / bc2) + ADAM_EPS) + ADAM_WD * w_ref[...])
        mo_ref[...] = mn
        vo_ref[...] = vn

    spec = pl.BlockSpec((tr, width), lambda i: (i, 0))
    outs = pl.pallas_call(
        body, name=name, grid=(rows // tr,), in_specs=[spec] * 4, out_specs=[spec] * 3,
        out_shape=[jax.ShapeDtypeStruct((rows, width), F32)] * 3,
        compiler_params=_params(("arbitrary",)),
    )(w2, g2, m2, v2)
    return tuple(o.reshape(shape) for o in outs)


HBM_SPEC = pl.BlockSpec(memory_space=pltpu.HBM)


def _position():
    return lax.axis_index("x"), lax.axis_index("y"), lax.axis_index("c")


def _other_chips(x, y):
    return [(1 - x, y), (x, 1 - y), (1 - x, 1 - y)]


class _Stream:
    def __init__(self, kind, size=0):
        self.kind, self.size = kind, size
        self.parts = 2 if kind == "heads" else 1

    def local(self, ref, k, part):
        if self.kind == "rows":
            return ref.at[:, pl.ds(k * self.size, self.size), :]
        if self.kind == "cols":
            return ref.at[:, :, pl.ds(k * self.size, self.size)]
        if self.kind == "heads":
            return ref.at[:, :, pl.ds(part * N_HEADS * HEAD_DIM + k * HEAD_DIM, HEAD_DIM)]
        if self.kind == "piece":
            return ref.at[k]
        return ref

    def shard(self, ref, part):
        if self.kind == "heads":
            return ref.at[:, :, pl.ds(part * HEAD_DIM, HEAD_DIM)]
        return ref

    def half_local(self, ref, k, part, cc, hd):
        if self.kind == "piece":
            return ref.at[k, pl.ds(cc * hd, hd)]
        return self.local(ref.at[pl.ds(cc * hd, hd)], k, part)


def _remote(src, dst, send_sems, recv_sems, idx, to):
    return pltpu.make_async_remote_copy(src_ref=src, dst_ref=dst, send_sem=send_sems.at[idx],
                                        recv_sem=recv_sems.at[idx], device_id=to, device_id_type=MESH)


def _comm_call(body, ins, out_shapes, n_remote, n_local, *, name):
    scratch = [pltpu.SemaphoreType.DMA((n_remote,)), pltpu.SemaphoreType.DMA((n_remote,))]
    if n_local:
        scratch.append(pltpu.SemaphoreType.DMA((n_local,)))
    return pl.pallas_call(
        body, name=name, in_specs=[HBM_SPEC] * len(ins), out_specs=[HBM_SPEC] * len(out_shapes), out_shape=out_shapes,
        scratch_shapes=scratch, compiler_params=pltpu.CompilerParams(has_side_effects=True),
    )(*ins)


def _gather_chips(shards, streams, out_shapes, *, name):
    n = len(shards)
    hd = shards[0].shape[0] // 2
    flat = [(t, part) for t in range(n) for part in range(streams[t].parts)]
    ns = len(flat)

    def body(*refs):
        s_refs, o_refs = refs[:n], refs[n:2 * n]
        send_sems, recv_sems = refs[2 * n:]
        x, y, c = _position()
        sibling = (x, y, 1 - c)
        chips = _other_chips(x, y)
        me = 2 * x + y
        sent = []
        for s, (t, part) in enumerate(flat):
            st = streams[t]
            src = st.shard(s_refs[t].at[pl.ds(c * hd, hd)], part)
            for j, (cx, cy) in enumerate(chips):
                sent.append(_remote(src, st.half_local(o_refs[t], me, part, c, hd), send_sems, recv_sems,
                                    3 * s + j, (cx, cy, c)))
                sent[-1].start()
        for s, (t, part) in enumerate(flat):
            st = streams[t]
            for j, (cx, cy) in enumerate(chips):
                blk = st.half_local(o_refs[t], 2 * cx + cy, part, c, hd)
                _remote(blk, blk, send_sems, recv_sems, 3 * s + j, (x, y, c)).wait_recv()
                sent.append(_remote(blk, blk, send_sems, recv_sems, 3 * ns + 3 * s + j, sibling))
                sent[-1].start()
        for s, (t, part) in enumerate(flat):
            st = streams[t]
            for j, (cx, cy) in enumerate(chips):
                blk = st.half_local(o_refs[t], 2 * cx + cy, part, 1 - c, hd)
                _remote(blk, blk, send_sems, recv_sems, 3 * ns + 3 * s + j, (x, y, c)).wait_recv()
        for cp in sent:
            cp.wait_send()

    return _comm_call(body, shards, out_shapes, 6 * ns, 0, name=name)


def _sibling_take_other_half(gs, *, name):
    n = len(gs)
    hd = gs[0].shape[0] // 2

    def body(*refs):
        g_refs, o_refs = refs[:n], refs[n:2 * n]
        send_sems, recv_sems = refs[2 * n:]
        x, y, c = _position()
        copies = [_remote(g_refs[t].at[pl.ds((1 - c) * hd, hd)], o_refs[t], send_sems, recv_sems, t, (x, y, 1 - c))
                  for t in range(n)]
        for cp in copies:
            cp.start()
        for cp in copies:
            cp.wait()

    outs = [jax.ShapeDtypeStruct((hd,) + g.shape[1:], g.dtype) for g in gs]
    return _comm_call(body, gs, outs, n, 0, name=name)


def _chips_exchange(ps, streams, shard_shapes, *, name):
    n = len(ps)
    flat = [(t, part) for t in range(n) for part in range(streams[t].parts)]

    def body(*refs):
        p_refs, o_refs = refs[:n], refs[n:2 * n]
        send_sems, recv_sems = refs[2 * n:]
        x, y, c = _position()
        copies = []
        for s, (t, part) in enumerate(flat):
            st = streams[t]
            for j, (cx, cy) in enumerate(_other_chips(x, y)):
                copies.append(_remote(st.local(p_refs[t], 2 * cx + cy, part), st.shard(o_refs[t].at[j], part),
                                      send_sems, recv_sems, 3 * s + j, (cx, cy, c)))
        for cp in copies:
            cp.start()
        for cp in copies:
            cp.wait()

    outs = [jax.ShapeDtypeStruct((3,) + tuple(shp), p.dtype) for p, shp in zip(ps, shard_shapes)]
    return _comm_call(body, ps, outs, 3 * len(flat), 0, name=name)


def _sibling_join_halves(bufs, *, name):
    n = len(bufs)
    hd = bufs[0].shape[0] // 2

    def body(*refs):
        o_refs = refs[n:2 * n]
        send_sems, recv_sems = refs[2 * n:]
        x, y, c = _position()
        sent = []
        for t in range(n):
            mine = o_refs[t].at[pl.ds(c * hd, hd)]
            sent.append(_remote(mine, mine, send_sems, recv_sems, t, (x, y, 1 - c)))
            sent[-1].start()
        for t in range(n):
            theirs = o_refs[t].at[pl.ds((1 - c) * hd, hd)]
            _remote(theirs, theirs, send_sems, recv_sems, t, (x, y, c)).wait_recv()
        for cp in sent:
            cp.wait_send()

    return pl.pallas_call(
        body, name=name, in_specs=[HBM_SPEC] * n, out_specs=[HBM_SPEC] * n,
        out_shape=[jax.ShapeDtypeStruct(b.shape, b.dtype) for b in bufs],
        scratch_shapes=[pltpu.SemaphoreType.DMA((n,)), pltpu.SemaphoreType.DMA((n,))],
        input_output_aliases={t: t for t in range(n)},
        compiler_params=pltpu.CompilerParams(has_side_effects=True),
    )(*bufs)


def _row_tile(rows, width):
    return _divisor_tile(rows, max(16, (1 << 19) // width), 16)


def _add_own_half(g, got, c_idx, out_dtype, *, name):
    hd, r, w = got.shape
    tr = _row_tile(r, w)

    def body(c_ref, g_ref, a_ref, o_ref):
        o_ref[...] = (g_ref[...] + a_ref[...]).astype(o_ref.dtype)

    return pl.pallas_call(
        body, name=name,
        grid_spec=pltpu.PrefetchScalarGridSpec(
            num_scalar_prefetch=1, grid=(hd, r // tr),
            in_specs=[pl.BlockSpec((None, None, tr, w), lambda l, i, c_ref: (c_ref[0], l, i, 0)),
                      pl.BlockSpec((None, tr, w), lambda l, i, c_ref: (l, i, 0))],
            out_specs=pl.BlockSpec((None, tr, w), lambda l, i, c_ref: (l, i, 0))),
        out_shape=jax.ShapeDtypeStruct((hd, r, w), out_dtype),
        compiler_params=_params(("arbitrary", "arbitrary")),
    )(c_idx, g.reshape((2, hd) + g.shape[1:]), got)


def _sum_chips(p, got, place, stream, *, name):
    _, hd, rs, cs = got.shape
    wb = HEAD_DIM if stream.kind == "heads" else cs
    tr = _row_tile(rs, wb)
    kind, size = stream.kind, stream.size

    def own_index(l, i, g, k_ref, c_ref):
        k = k_ref[0]
        if kind == "rows":
            return (l, k * (size // tr) + i, 0)
        if kind == "cols":
            return (l, i, k)
        if kind == "heads":
            return (l, i, g * N_HEADS + k)
        if kind == "piece":
            return (k, l, i, 0)
        return (l, i, 0)

    own_blk = (None, None, tr, wb) if kind == "piece" else (None, tr, wb)

    def body(k_ref, c_ref, p_ref, fx_ref, fy_ref, fxy_ref, o_ref):
        f = lambda r: r[...].astype(F32)
        o_ref[...] = (f(p_ref) + f(fy_ref)) + (f(fx_ref) + f(fxy_ref))

    def rel(j):
        return pl.BlockSpec((None, None, tr, wb), functools.partial(lambda l, i, g, k_ref, c_ref, j: (j, l, i, g), j=j))

    return pl.pallas_call(
        body, name=name,
        grid_spec=pltpu.PrefetchScalarGridSpec(
            num_scalar_prefetch=2, grid=(hd, rs // tr, stream.parts),
            in_specs=[pl.BlockSpec(own_blk, own_index), rel(0), rel(1), rel(2)],
            out_specs=pl.BlockSpec((None, tr, wb), lambda l, i, g, k_ref, c_ref: (c_ref[0] * hd + l, i, g))),
        out_shape=jax.ShapeDtypeStruct((2 * hd, rs, cs), F32),
        compiler_params=_params(("arbitrary", "arbitrary", "arbitrary")),
    )(place[0], place[1], p, got, got, got)


MATRICES = ("w_in", "w_uq", "w_ukv", "w_out", "w_gate_up", "w_down", "w_ple", "w_ple_gate", "conv_w")
VECTORS = ("a_log", "dt_bias", "gdn_norm_g", "q_norm_g", "kv_norm_g", "ln1_g", "ln1_b", "ln2_g", "ln2_b")
WEIGHTS = ("w_in", "conv_w", "a_log", "dt_bias", "gdn_norm_g", "q_norm_g", "w_uq", "kv_norm_g", "w_ukv", "w_out",
           "ln1_g", "ln1_b", "w_gate_up", "w_down", "ln2_g", "ln2_b", "w_ple", "w_ple_gate")
ROW_SHARDED = ("w_out", "w_down", "w_ple_gate")
N_CHIPS = 4


def _stream_of(name, shard_shape):
    if name in ("w_in", "w_uq"):
        return _Stream("piece")
    if name == "w_ukv":
        return _Stream("heads")
    if name in ROW_SHARDED:
        return _Stream("rows", shard_shape[0])
    return _Stream("cols", shard_shape[1])


def _place_own_block(local, shard, stream, chip):
    z = jnp.zeros((), jnp.int32)
    if stream.kind == "piece":
        return lax.dynamic_update_slice(local, shard[None], (chip, z, z, z))
    if stream.kind == "rows":
        return lax.dynamic_update_slice(local, shard, (z, chip * stream.size, z))
    if stream.kind == "cols":
        return lax.dynamic_update_slice(local, shard, (z, z, chip * stream.size))
    for part in range(stream.parts):
        piece = shard[:, :, part * HEAD_DIM:(part + 1) * HEAD_DIM]
        local = lax.dynamic_update_slice(local, piece, (z, z, part * N_HEADS * HEAD_DIM + chip * HEAD_DIM))
    return local


def _pack_vectors(vecs, depth):
    flat = jnp.concatenate([vecs[n].reshape(depth, -1) for n in VECTORS], axis=1)
    pad = jnp.zeros((depth, VEC_ROWS * LANES - flat.shape[1]), F32)
    return jnp.concatenate([flat, pad], axis=1).reshape(depth, VEC_ROWS, LANES)


def _unpack_vectors(packed, shapes):
    depth = packed.shape[0]
    flat = packed.reshape(depth, VEC_ROWS * LANES)
    out, off = {}, 0
    for n in VECTORS:
        out[n] = flat[:, off:off + shapes[n][1]]
        off += shapes[n][1]
    return out


VEC_ROWS = 40


class _Dims:
    def __init__(self, d_model, in_width, q_lora, kv_lora, d_ff2, ple_dim):
        self.d = d_model
        self.hw = N_HEADS * HEAD_DIM
        self.in_width = in_width
        self.q_lora, self.kv_lora = q_lora, kv_lora
        self.ff2 = d_ff2
        self.ple = ple_dim
        self.c_kv0 = 4 * self.hw
        self.c_q0 = self.c_kv0 + kv_lora
        self.misc0 = self.c_q0 + q_lora
        self.h_width = self.misc0 + LANES
        assert self.c_kv0 % kv_lora == 0 and self.c_q0 % q_lora == 0 and self.misc0 % LANES == 0
        self.g_beta = 4 * self.hw
        self.g_a = self.g_beta + N_HEADS
        self.g_cq = self.g_a + N_HEADS
        self.g_ckv = self.g_cq + q_lora
        self.g_kr = self.g_ckv + kv_lora
        assert self.g_kr + ROPE_DIM == in_width

    def w_in_local(self, w):
        pad = jnp.zeros(w.shape[:-1] + (self.h_width - self.in_width,), w.dtype)
        return jnp.concatenate([w[..., :self.g_beta], w[..., self.g_ckv:self.g_kr], w[..., self.g_cq:self.g_ckv],
                                w[..., self.g_kr:], w[..., self.g_beta:self.g_cq], pad], axis=-1)

    def w_in_global(self, d):
        m = self.misc0
        return jnp.concatenate([d[..., :self.c_kv0], d[..., m + MISC_BETA0:m + MISC_A0 + N_HEADS],
                                d[..., self.c_q0:self.misc0], d[..., self.c_kv0:self.c_q0], d[..., m:m + ROPE_DIM]],
                               axis=-1)

    def w_uq_local(self, w):
        r = w.reshape(w.shape[:-1] + (N_HEADS, HEAD_DIM + ROPE_DIM))
        r = jnp.pad(r, [(0, 0)] * (r.ndim - 1) + [(0, QK_DIM - HEAD_DIM - ROPE_DIM)])
        return r.reshape(w.shape[:-1] + (N_HEADS * QK_DIM,))

    def w_uq_global(self, d):
        r = d.reshape(d.shape[:-1] + (N_HEADS, QK_DIM))[..., :HEAD_DIM + ROPE_DIM]
        return r.reshape(d.shape[:-1] + (N_HEADS * (HEAD_DIM + ROPE_DIM),))


def _lane_padded(n):
    return -(-n // LANES) * LANES


def _pad_lanes(a):
    pad = _lane_padded(a.shape[-1]) - a.shape[-1]
    return a if pad == 0 else jnp.pad(a, [(0, 0)] * (a.ndim - 1) + [(0, pad)])


def _lane_row(vec, lane0):
    pad = LANES - lane0 - vec.shape[0]
    return jnp.concatenate([jnp.zeros((lane0,), F32), vec.astype(F32), jnp.zeros((pad,), F32)])[None, :]


def _layer_fwd(dm, alpha, x, xb, p_i, cos_t, sin_t, wl, tag):
    d, hw = dm.d, dm.hw
    nm = lambda s: f"{s}_{tag}"
    mm = functools.partial(_matmul, layer=wl["layer"])
    h = mm(xb, wl["w_in"], dims="nn", name=nm("f_in"), tn=_divisor_tile(dm.h_width, 1408, LANES))
    misc_cb = dm.misc0 // LANES

    u = _conv_fwd(h, wl["conv_w"], 3 * hw, name=nm("f_conv"))
    qn, kn, vg, gb = _rowwise(_gdn_act, [(u, 3 * hw, 0), (h, LANES, misc_cb)], [wl["alog_row"], wl["dtb_row"]],
                              [(hw, F32), (hw, F32), (hw, F32), (LANES, F32)], name=nm("f_gdn_act"))
    prep = _gdn_prep_fwd(qn, kn, vg, gb, name=nm("f_gdn_prep"))
    o_gdn, s_all = _gdn_seq_fwd(prep, name=nm("f_gdn_seq"))
    (og,) = _rowwise(lambda o, z, g: (_gdn_out(o, z, g),), [(o_gdn, hw, 0), (h, hw, 3)], [wl["gn_row"]],
                     [(hw, BF16)], name=nm("f_gdn_out"))

    cqn, ckvn = _rowwise(_mla_norm, [(h, dm.kv_lora, dm.c_kv0 // dm.kv_lora), (h, dm.q_lora, dm.c_q0 // dm.q_lora)],
                         [wl["kvg_row"], wl["qg_row"]], [(dm.q_lora, BF16), (dm.kv_lora, BF16)], name=nm("f_mla_norm"))
    qm = mm(cqn, wl["w_uq"], dims="nn", name=nm("f_uq"))
    kvm = mm(ckvn, wl["w_ukv"], dims="nn", name=nm("f_ukv"))
    scale = (HEAD_DIM + ROPE_DIM) ** -0.5
    qk_fn = functools.partial(_mla_qk, scale)
    qa, ka, va = _rowwise(qk_fn, [(qm, N_HEADS * QK_DIM, 0), (kvm, 2 * hw, 0), (h, LANES, misc_cb),
                                  (cos_t, LANES, 0), (sin_t, LANES, 0)], [],
                          [(N_HEADS * QK_DIM, BF16), (N_HEADS * QK_DIM, BF16), (hw, BF16)], name=nm("f_mla_qk"))
    kt = _blocked_transpose(ka, QK_DIM)
    v_heads = va.reshape(va.shape[0], N_HEADS, HEAD_DIM)
    v1 = jnp.concatenate([v_heads, jnp.ones_like(v_heads)], axis=2).reshape(va.shape[0], 2 * hw)
    o_mla, lse = _attn_fwd(qa, kt, v1, name=nm("f_attn"))

    om = jnp.concatenate([og, o_mla], axis=1)
    mix = mm(om, wl["w_out"], dims="nn", name=nm("f_out"))
    ln1 = lambda xv, yv, g, b: (_layer_norm(alpha * xv + yv, g, b),) * 2
    x1, x1b = _rowwise(ln1, [(x, d, 0), (mix, d, 0)], [wl["ln1_g"], wl["ln1_b"]], [(d, F32), (d, BF16)], name=nm("f_ln1"))

    gu = mm(x1b, wl["w_gate_up"], dims="nn", name=nm("f_gate_up"))
    (act,) = _rowwise(lambda g_: (_swiglu(g_),), [(gu, dm.ff2, 0)], [], [(dm.ff2 // 2, BF16)], name=nm("f_swiglu"))
    dn = mm(act, wl["w_down"], dims="nn", name=nm("f_down"))
    x2, x2b = _rowwise(ln1, [(x1, d, 0), (dn, d, 0)], [wl["ln2_g"], wl["ln2_b"]], [(d, F32), (d, BF16)], name=nm("f_ln2"))

    pg = mm(x2b, wl["w_ple_gate"], dims="nn", name=nm("f_ple_gate"))
    pe = mm(p_i, wl["w_ple"], dims="nn", name=nm("f_ple"))
    out, outb = _rowwise(lambda a, b, c_: (_ple_out(a, b, c_),) * 2, [(x2, d, 0), (pg, d, 0), (pe, d, 0)], [],
                         [(d, F32), (d, BF16)], name=nm("f_ple_out"))
    saved = dict(x=x, xb=xb, p_i=p_i, h=h, u=u, qn=qn, kn=kn, vg=vg, gb=gb, prep=prep, s_all=s_all, o_gdn=o_gdn, cqn=cqn, ckvn=ckvn,
                 qm=qm, kvm=kvm, qa=qa, ka=ka, kt=kt, va=va, o_mla=o_mla, lse=lse, om=om, mix=mix, x1=x1, x1b=x1b, gu=gu,
                 act=act, dn=dn, x2=x2, x2b=x2b, pg=pg, pe=pe)
    return out, outb, saved


def _layer_bwd(dm, alpha, dout, sv, cos_t, sin_t, wl, gbuf, tag):
    d, hw = dm.d, dm.hw
    t = dout.shape[0]
    nm = lambda s: f"{s}_{tag}"
    gr = {}
    gbuf = dict(gbuf)
    misc_cb = dm.misc0 // LANES
    mm = functools.partial(_matmul, layer=wl["layer"])

    def wgrad(name_, a, g):
        gbuf[name_] = mm(a, g, dims="tn", name=nm("b_" + name_), into=gbuf[name_], tm=1408, tn=1408, tk=1024)

    dx2_a, dpg, dpe = _rowwise(_vjp_fn(_ple_out, 3, 1), [(sv["x2"], d, 0), (sv["pg"], d, 0), (sv["pe"], d, 0), (dout, d, 0)],
                               [], [(d, F32), (d, BF16), (d, BF16)], name=nm("b_ple_out"))
    wgrad("w_ple", sv["p_i"], dpe)
    wgrad("w_ple_gate", sv["x2b"], dpg)
    dx2 = mm(dpg, wl["w_ple_gate"], dims="nt", c=dx2_a, name=nm("b_x2"))

    def ln_bwd(xv, yv, ct, g, b):
        _, pull = jax.vjp(lambda a_, b_, c_, d_: _layer_norm(alpha * a_ + b_, c_, d_), xv, yv, g, b)
        return pull(ct)

    dx1_a, ddn, gr["ln2_g"], gr["ln2_b"] = _rowwise(
        ln_bwd, [(sv["x1"], d, 0), (sv["dn"], d, 0), (dx2, d, 0)], [wl["ln2_g"], wl["ln2_b"]],
        [(d, F32), (d, BF16)], [(1, d), (1, d)], name=nm("b_ln2"))
    wgrad("w_down", sv["act"], ddn)
    dact = mm(ddn, wl["w_down"], dims="nt", name=nm("b_act"), tn=1408)
    (dgu,) = _rowwise(_vjp_fn(_swiglu, 1, 1), [(sv["gu"], dm.ff2, 0), (dact, dm.ff2 // 2, 0)], [], [(dm.ff2, BF16)],
                      name=nm("b_swiglu"))
    wgrad("w_gate_up", sv["x1b"], dgu)
    dx1 = mm(dgu, wl["w_gate_up"], dims="nt", c=dx1_a, name=nm("b_x1"))

    dx_a, dmix, gr["ln1_g"], gr["ln1_b"] = _rowwise(
        ln_bwd, [(sv["x"], d, 0), (sv["mix"], d, 0), (dx1, d, 0)], [wl["ln1_g"], wl["ln1_b"]],
        [(d, F32), (d, BF16)], [(1, d), (1, d)], name=nm("b_ln1"))
    wgrad("w_out", sv["om"], dmix)
    dom = mm(dmix, wl["w_out"], dims="nt", name=nm("b_om"))

    nq = t // min(ATT_TILE, t)
    dqa, delta = _attn_bwd_dq(sv["qa"], sv["ka"], sv["kt"], _blocked_transpose(sv["va"], HEAD_DIM), sv["o_mla"], sv["lse"],
                              dom, hw // HEAD_DIM, name=nm("b_attn_dq"))
    lse_row = sv["lse"].reshape(N_HEADS, nq, 1, t // nq)
    delta_row = delta.reshape(N_HEADS, nq, 1, t // nq)
    dot = _blocked_transpose(dom[:, hw:].astype(BF16), HEAD_DIM)
    dka, dva = _attn_bwd_dkv(sv["qa"], _blocked_transpose(sv["qa"], QK_DIM), sv["ka"], sv["va"], lse_row, delta_row, dom, dot,
                             hw // HEAD_DIM, name=nm("b_attn_dkv"))
    scale = (HEAD_DIM + ROPE_DIM) ** -0.5
    qk_fn = functools.partial(_mla_qk, scale)

    def qk_bwd(qm, kvm, misc, cs, sn, g_q, g_k, g_v):
        _, pull = jax.vjp(lambda a, b, c_: qk_fn(a, b, c_, cs, sn), qm, kvm, misc)
        return pull((g_q, g_k, g_v))

    dqm, dkvm, dmisc_rope = _rowwise(
        qk_bwd, [(sv["qm"], N_HEADS * QK_DIM, 0), (sv["kvm"], 2 * hw, 0), (sv["h"], LANES, misc_cb), (cos_t, LANES, 0),
                 (sin_t, LANES, 0), (dqa, N_HEADS * QK_DIM, 0), (dka, N_HEADS * QK_DIM, 0), (dva, hw, 0)], [],
        [(N_HEADS * QK_DIM, BF16), (2 * hw, BF16), (LANES, F32)], name=nm("b_mla_qk"))
    wgrad("w_uq", sv["cqn"], dqm)
    wgrad("w_ukv", sv["ckvn"], dkvm)
    dcqn = mm(dqm, wl["w_uq"], dims="nt", name=nm("b_cqn"))
    dckvn = mm(dkvm, wl["w_ukv"], dims="nt", name=nm("b_ckvn"))

    def norm_bwd(ckv, cq, g_q, g_kv, kvg, qg):
        _, pull = jax.vjp(_mla_norm, ckv, cq, kvg, qg)
        return pull((g_q, g_kv))

    dckv, dcq, gr["kvg_row"], gr["qg_row"] = _rowwise(
        norm_bwd, [(sv["h"], dm.kv_lora, dm.c_kv0 // dm.kv_lora), (sv["h"], dm.q_lora, dm.c_q0 // dm.q_lora),
                   (dcqn, dm.q_lora, 0), (dckvn, dm.kv_lora, 0)], [wl["kvg_row"], wl["qg_row"]],
        [(dm.kv_lora, BF16), (dm.q_lora, BF16)], [(1, dm.kv_lora), (1, dm.q_lora)], name=nm("b_mla_norm"))

    def gout_bwd(o, z, g_o, gn):
        _, pull = jax.vjp(_gdn_out, o, z, gn)
        return pull(g_o)

    do_gdn, dz, gr["gn_row"] = _rowwise(gout_bwd, [(sv["o_gdn"], hw, 0), (sv["h"], hw, 3), (dom, hw, 0)], [wl["gn_row"]],
                                        [(hw, F32), (hw, BF16)], [(1, HEAD_DIM)], name=nm("b_gdn_out"))
    dprep = _gdn_seq_bwd(sv["prep"], sv["s_all"], do_gdn, name=nm("b_gdn_seq"))
    dqn, dkn, dvg, dgb = _gdn_prep_bwd(sv["qn"], sv["kn"], sv["vg"], sv["gb"], dprep, name=nm("b_gdn_prep"))

    def act_bwd(u, misc, g_q, g_k, g_v, g_gb, g_rope, alog, dtb):
        _, pull = jax.vjp(_gdn_act, u, misc, alog, dtb)
        du_, dmisc_, dalog_, ddtb_ = pull((g_q, g_k, g_v, g_gb))
        return du_, dmisc_ + g_rope, dalog_, ddtb_

    du, dmisc, gr["alog_row"], gr["dtb_row"] = _rowwise(
        act_bwd, [(sv["u"], 3 * hw, 0), (sv["h"], LANES, misc_cb), (dqn, hw, 0), (dkn, hw, 0), (dvg, hw, 0),
                  (dgb, LANES, 0), (dmisc_rope, LANES, 0)], [wl["alog_row"], wl["dtb_row"]],
        [(3 * hw, F32), (LANES, BF16)], [(1, LANES), (1, LANES)], name=nm("b_gdn_act"))
    dqkv, dconv = _conv_bwd(du, sv["h"], wl["conv_w"], 3 * hw, name=nm("b_conv"))
    gr["conv_w"] = dconv[:CONV_W]

    dh = jnp.concatenate([dqkv, dz, dckv, dcq, dmisc], axis=1)
    wgrad("w_in", sv["xb"], dh)
    dx = mm(dh, wl["w_in"], dims="nt", c=dx_a, name=nm("b_x"), tk=1408)
    return dx, gbuf, gr


LOCAL_MATRICES = ("w_in", "w_uq", "w_ukv", "w_out", "w_gate_up", "w_down", "w_ple", "w_ple_gate")


def _layer_weights(mats, vecs, layer):
    wl = {n: mats[n] for n in LOCAL_MATRICES}
    wl["layer"] = layer
    wl["conv_w"] = mats["conv_w"][layer]
    wl["alog_row"] = _lane_row(vecs["a_log"][layer], MISC_A0)
    wl["dtb_row"] = _lane_row(vecs["dt_bias"][layer], MISC_A0)
    wl["gn_row"] = vecs["gdn_norm_g"][layer][None, :]
    wl["qg_row"] = vecs["q_norm_g"][layer][None, :]
    wl["kvg_row"] = vecs["kv_norm_g"][layer][None, :]
    for n in ("ln1_g", "ln1_b", "ln2_g", "ln2_b"):
        wl[n] = vecs[n][layer][None, :]
    return wl


def _vector_grads(gr):
    out = {"a_log": gr["alog_row"][0, MISC_A0:MISC_A0 + N_HEADS], "dt_bias": gr["dtb_row"][0, MISC_A0:MISC_A0 + N_HEADS],
           "gdn_norm_g": gr["gn_row"][0], "q_norm_g": gr["qg_row"][0], "kv_norm_g": gr["kvg_row"][0]}
    for n in ("ln1_g", "ln1_b", "ln2_g", "ln2_b"):
        out[n] = gr[n][0]
    return out


def _local_step(dm, x, p, positions, target, mats, vecs):
    depth = p.shape[0]
    alpha = (2.0 * depth) ** 0.25
    freq = ROPE_THETA ** (-jnp.arange(0, ROPE_DIM, 2, dtype=F32) / ROPE_DIM)
    inv_freq_row = _lane_row(jnp.concatenate([freq, freq]), 0)
    cos_t, sin_t = _rope_tables(positions.reshape(-1, 1), inv_freq_row, name="rope_tables")

    wls = [_layer_weights(mats, vecs, i) for i in range(depth)]
    saved = []
    cur, cur_b = x, x
    for i in range(depth):
        cur, cur_b, sv = _layer_fwd(dm, alpha, cur, cur_b, p[i], cos_t, sin_t, wls[i], f"l{i}")
        saved.append(sv)
    dy, loss_blk = _loss_head(cur, target)
    gbuf = {n: jnp.zeros(mats[n].shape, F32) for n in LOCAL_MATRICES}
    conv_g, vec_g = [None] * depth, [None] * depth
    for i in reversed(range(depth)):
        dy, gbuf, gr = _layer_bwd(dm, alpha, dy, saved[i], cos_t, sin_t, wls[i], gbuf, f"l{i}")
        conv_g[i] = gr["conv_w"]
        vec_g[i] = _vector_grads(gr)
    vec_grads = {n: jnp.stack([vec_g[i][n] for i in range(depth)]) for n in VECTORS}
    return loss_blk[0, 0], dy, gbuf, jnp.stack(conv_g), vec_grads


def kernel(x, p, positions, w_in, conv_w, a_log, dt_bias, gdn_norm_g, q_norm_g, w_uq, kv_norm_g, w_ukv, w_out, ln1_g, ln1_b, w_gate_up, w_down, ln2_g, ln2_b, w_ple, w_ple_gate, loss_target, m_w_in, m_conv_w, m_a_log, m_dt_bias, m_gdn_norm_g, m_q_norm_g, m_w_uq, m_kv_norm_g, m_w_ukv, m_w_out, m_ln1_g, m_ln1_b, m_w_gate_up, m_w_down, m_ln2_g, m_ln2_b, m_w_ple, m_w_ple_gate, v_w_in, v_conv_w, v_a_log, v_dt_bias, v_gdn_norm_g, v_q_norm_g, v_w_uq, v_kv_norm_g, v_w_ukv, v_w_out, v_ln1_g, v_ln1_b, v_w_gate_up, v_w_down, v_ln2_g, v_ln2_b, v_w_ple, v_w_ple_gate):
    w = dict(w_in=w_in, conv_w=conv_w, a_log=a_log, dt_bias=dt_bias, gdn_norm_g=gdn_norm_g, q_norm_g=q_norm_g, w_uq=w_uq,
             kv_norm_g=kv_norm_g, w_ukv=w_ukv, w_out=w_out, ln1_g=ln1_g, ln1_b=ln1_b, w_gate_up=w_gate_up, w_down=w_down,
             ln2_g=ln2_g, ln2_b=ln2_b, w_ple=w_ple, w_ple_gate=w_ple_gate)
    m = dict(w_in=m_w_in, conv_w=m_conv_w, a_log=m_a_log, dt_bias=m_dt_bias, gdn_norm_g=m_gdn_norm_g, q_norm_g=m_q_norm_g,
             w_uq=m_w_uq, kv_norm_g=m_kv_norm_g, w_ukv=m_w_ukv, w_out=m_w_out, ln1_g=m_ln1_g, ln1_b=m_ln1_b,
             w_gate_up=m_w_gate_up, w_down=m_w_down, ln2_g=m_ln2_g, ln2_b=m_ln2_b, w_ple=m_w_ple, w_ple_gate=m_w_ple_gate)
    v = dict(w_in=v_w_in, conv_w=v_conv_w, a_log=v_a_log, dt_bias=v_dt_bias, gdn_norm_g=v_gdn_norm_g, q_norm_g=v_q_norm_g,
             w_uq=v_w_uq, kv_norm_g=v_kv_norm_g, w_ukv=v_w_ukv, w_out=v_w_out, ln1_g=v_ln1_g, ln1_b=v_ln1_b,
             w_gate_up=v_w_gate_up, w_down=v_w_down, ln2_g=v_ln2_g, ln2_b=v_ln2_b, w_ple=v_w_ple, w_ple_gate=v_w_ple_gate)
    depth = w_in.shape[0]
    assert depth % 2 == 0
    hd = depth // 2
    dm = _Dims(x.shape[2], N_CHIPS * w_in.shape[2], w_uq.shape[1], w_ukv.shape[1], N_CHIPS * w_gate_up.shape[2], p.shape[3])
    cx, cy, cc = lax.axis_index("x"), lax.axis_index("y"), lax.axis_index("c")
    chip = 2 * cx + cy

    g_streams = [_stream_of(n, w[n].shape[1:]) for n in MATRICES]
    shards = [w[n] if n == "conv_w" else w[n].astype(BF16) for n in MATRICES]
    shards = [_pad_lanes(s) if st.kind == "piece" else s for s, st in zip(shards, g_streams)]
    g_shapes = []
    for s, st in zip(shards, g_streams):
        if st.kind == "piece":
            shape = (N_CHIPS,) + s.shape
        elif st.kind == "rows":
            shape = (depth, N_CHIPS * s.shape[1], s.shape[2])
        else:
            shape = (depth, s.shape[1], N_CHIPS * s.shape[2])
        g_shapes.append(jax.ShapeDtypeStruct(shape, s.dtype))
    gathered = _gather_chips(shards, g_streams, g_shapes, name="gather_weights")
    mats = {n: _place_own_block(g, s, st, chip) for n, g, s, st in zip(MATRICES, gathered, shards, g_streams)}
    for n, to_local in (("w_in", dm.w_in_local), ("w_uq", dm.w_uq_local)):
        pieces = jnp.moveaxis(mats[n][..., :w[n].shape[2]], 0, 2)
        mats[n] = to_local(pieces.reshape(pieces.shape[:2] + (-1,)))
    vecs = {n: w[n] for n in VECTORS}

    loss_local, grad_x, gbuf, conv_g, vec_g = _local_step(dm, x[0], p[:, 0], positions[0], loss_target[0], mats, vecs)
    loss = lax.psum(loss_local, ("x", "y", "c"))

    names = list(LOCAL_MATRICES) + ["conv_w", "vectors"]
    gs = [gbuf[n] for n in LOCAL_MATRICES] + [conv_g, _pack_vectors(vec_g, depth)]
    wire = [BF16] * len(LOCAL_MATRICES) + [F32, F32]
    r_streams = [_stream_of(n, w[n].shape[1:]) for n in LOCAL_MATRICES]
    r_streams += [_stream_of("conv_w", w["conv_w"].shape[1:]), _Stream("whole")]
    shard_shapes = [(hd, w[n].shape[1], _lane_padded(w[n].shape[2])) if st.kind == "piece" else (hd,) + w[n].shape[1:]
                    for n, st in zip(LOCAL_MATRICES, r_streams)]
    shard_shapes += [(hd,) + w["conv_w"].shape[1:], (hd, VEC_ROWS, LANES)]
    c_idx = cc.reshape(1).astype(jnp.int32)
    place = (chip.reshape(1).astype(jnp.int32), c_idx)
    from_sibling = _sibling_take_other_half(gs, name="reduce_sibling")
    chip_sum = [_add_own_half(g, a, c_idx, dt, name=f"reduce_add_{n}")
                for g, a, dt, n in zip(gs, from_sibling, wire, names)]
    for i, n in enumerate(names):
        if r_streams[i].kind == "piece":
            glob = dm.w_in_global(chip_sum[i]) if n == "w_in" else dm.w_uq_global(chip_sum[i])
            glob = glob.reshape(glob.shape[:2] + (N_CHIPS, glob.shape[2] // N_CHIPS))
            chip_sum[i] = jnp.moveaxis(_pad_lanes(glob), 2, 0)
    from_chips = _chips_exchange(chip_sum, r_streams, shard_shapes, name="reduce_chips")
    halves = [_sum_chips(ps, got, place, st, name=f"reduce_sum_{n}")
              for ps, got, st, n in zip(chip_sum, from_chips, r_streams, names)]
    joined = dict(zip(names, _sibling_join_halves(halves, name="reduce_join")))
    joined.update(_unpack_vectors(joined.pop("vectors"), {n: w[n].shape for n in VECTORS}))

    grad_w, delta_w, new_m, new_v = {}, {}, {}, {}
    for n in WEIGHTS:
        grad_w[n] = joined[n][..., :w[n].shape[-1]]
        delta_w[n], new_m[n], new_v[n] = _adamw(w[n], grad_w[n], m[n], v[n], name=f"adamw_{n}")
    return (loss, grad_x[None], *[grad_w[n] for n in WEIGHTS], *[delta_w[n] for n in WEIGHTS],
            *[new_m[n] for n in WEIGHTS], *[new_v[n] for n in WEIGHTS])
```

```python
import functools

import jax
import jax.numpy as jnp
from jax import lax
from jax.experimental import pallas as pl
from jax.experimental.pallas import tpu as pltpu

F32 = jnp.float32
BF16 = jnp.bfloat16
HIGH = lax.Precision.HIGH
MESH = pl.DeviceIdType.MESH

CHUNK = 64
N_HEADS = 4
HEAD_DIM = 128
ROPE_DIM = 64
ROPE_THETA = 10000.0
LN_EPS = 1e-5
RMS_EPS = 1e-6
ADAM_LR, ADAM_B1, ADAM_B2, ADAM_EPS, ADAM_WD, ADAM_STEP = 0.001, 0.9, 0.999, 1e-08, 0.01, 10

LANES = 128
VMEM_LIMIT = 48 * 1024 * 1024
PACK_W = 512
ROW_TILE = 256
SUB_ROWS = 16
MAX_SUB_ROWS = 64
VREG_FILE_ELEMS = 64 * 8 * LANES

MISC_BETA0 = ROPE_DIM
MISC_A0 = ROPE_DIM + N_HEADS

NN = (((1,), (0,)), ((), ()))
NT = (((1,), (1,)), ((), ()))
TN = (((0,), (0,)), ((), ()))


def _params(sem=None):
    return pltpu.CompilerParams(dimension_semantics=sem, vmem_limit_bytes=VMEM_LIMIT)


def _divisor_tile(dim, target, unit):
    best = None
    t = unit
    while t <= min(dim, target):
        if dim % t == 0:
            best = t
        t += unit
    return best if best is not None else dim


BATCHED = {NN: (((2,), (1,)), ((0,), (0,))), NT: (((2,), (2,)), ((0,), (0,))), TN: (((1,), (1,)), ((0,), (0,)))}


def _make_dots(high_precision):
    def raw(a, b, dims):
        if a.ndim == 3:
            dims = BATCHED[dims]
        if high_precision:
            return lax.dot_general(a, b, dims, precision=HIGH, preferred_element_type=F32)
        return lax.dot_general(a.astype(BF16), b.astype(BF16), dims, preferred_element_type=F32)

    @jax.custom_vjp
    def nn(a, b):
        return raw(a, b, NN)

    @jax.custom_vjp
    def nt(a, b):
        return raw(a, b, NT)

    @jax.custom_vjp
    def tn(a, b):
        return raw(a, b, TN)

    nn.defvjp(lambda a, b: (raw(a, b, NN), (a, b)), lambda r, g: (nt(g, r[1]), tn(r[0], g)))
    nt.defvjp(lambda a, b: (raw(a, b, NT), (a, b)), lambda r, g: (nn(g, r[1]), tn(g, r[0])))
    tn.defvjp(lambda a, b: (raw(a, b, TN), (a, b)), lambda r, g: (nt(r[1], g), nn(r[0], g)))
    return nn, nt, tn


_nn, _nt, _tn = _make_dots(False)
_hnn, _hnt, _htn = _make_dots(True)


def _matmul(a, b, *, dims, name, c=None, out_dtype=F32, tm=1024, tn=1408, tk=1408, layer=None, into=None):
    b_shape = b.shape[-2:]
    if dims == "nn":
        (m, k), (k2, n) = a.shape, b_shape
    elif dims == "nt":
        (m, k), (n, k2) = a.shape, b_shape
    else:
        (k, m), (k2, n) = a.shape, b_shape
    assert k == k2, (a.shape, b.shape, dims)
    tm = _divisor_tile(m, tm, LANES)
    tn = _divisor_tile(n, tn, LANES)
    tk = _divisor_tile(k, tk, LANES)
    nk = k // tk
    dn = {"nn": NN, "nt": NT, "tn": TN}[dims]
    if dims == "tn":
        a_spec = pl.BlockSpec((tk, tm), lambda i, j, kk: (kk, i))
    else:
        a_spec = pl.BlockSpec((tm, tk), lambda i, j, kk: (i, kk))
    b_blk, b_idx = ((tn, tk), lambda i, j, kk: (j, kk)) if dims == "nt" else ((tk, tn), lambda i, j, kk: (kk, j))
    if b.ndim == 3:
        b_spec = pl.BlockSpec((None,) + b_blk, lambda i, j, kk: (layer,) + b_idx(i, j, kk))
    else:
        b_spec = pl.BlockSpec(b_blk, b_idx)
    c_spec = pl.BlockSpec((tm, tn), lambda i, j, kk: (i, j))
    if into is not None:
        assert into.shape[1:] == (m, n) and into.dtype == out_dtype
        o_spec = pl.BlockSpec((None, tm, tn), lambda i, j, kk: (layer, i, j))
        out_shape = jax.ShapeDtypeStruct(into.shape, into.dtype)
    else:
        o_spec = c_spec
        out_shape = jax.ShapeDtypeStruct((m, n), out_dtype)
    has_c = c is not None

    def body(*refs):
        a_ref, b_ref = refs[:2]
        c_ref = refs[2] if has_c else None
        o_ref, acc_ref = refs[-2:]
        kk = pl.program_id(2)

        @pl.when(kk == 0)
        def _():
            if has_c:
                acc_ref[...] = c_ref[...].astype(F32)
            else:
                acc_ref[...] = jnp.zeros_like(acc_ref)

        acc_ref[...] += lax.dot_general(a_ref[...].astype(BF16), b_ref[...].astype(BF16), dn,
                                        preferred_element_type=F32)

        @pl.when(kk == nk - 1)
        def _():
            o_ref[...] = acc_ref[...].astype(o_ref.dtype)

    ins = [a, b] + ([c] if has_c else [])
    specs = [a_spec, b_spec] + ([c_spec] if has_c else [])
    aliases = {}
    if into is not None:
        aliases = {len(ins): 0}
        ins.append(into)
        specs.append(pl.BlockSpec(memory_space=pl.ANY))
    return pl.pallas_call(
        body, name=name, grid=(m // tm, n // tn, nk), in_specs=specs, out_specs=o_spec, out_shape=out_shape,
        scratch_shapes=[pltpu.VMEM((tm, tn), F32)], input_output_aliases=aliases,
        compiler_params=_params(("arbitrary", "arbitrary", "arbitrary")),
    )(*ins)


def _rowwise(fn, rows, params, outs, accs=(), *, name, tm=ROW_TILE):
    t = rows[0][0].shape[0]
    tm = min(tm, t)
    widest = max([w for _, w, _ in rows] + [w for w, _ in outs])
    sub = SUB_ROWS
    while sub < MAX_SUB_ROWS and 2 * sub * widest <= VREG_FILE_ELEMS:
        sub *= 2
    assert t % tm == 0 and tm % sub == 0
    n_rows, n_par, n_out, n_acc = len(rows), len(params), len(outs), len(accs)

    def body(*refs):
        row_refs = refs[:n_rows]
        par_refs = refs[n_rows:n_rows + n_par]
        out_refs = refs[n_rows + n_par:n_rows + n_par + n_out]
        acc_refs = refs[n_rows + n_par + n_out:]
        if n_acc:
            @pl.when(pl.program_id(0) == 0)
            def _():
                for a_ref in acc_refs:
                    a_ref[...] = jnp.zeros_like(a_ref)

        def step(r, carry):
            sl = pl.ds(pl.multiple_of(r * sub, sub), sub)
            vals = [ref[sl, :].astype(F32) for ref in row_refs] + [ref[...] for ref in par_refs]
            res = fn(*vals)
            for o_ref, val in zip(out_refs, res[:n_out]):
                o_ref[sl, :] = val.astype(o_ref.dtype)
            for a_ref, val in zip(acc_refs, res[n_out:]):
                a_ref[...] += val
            return carry

        lax.fori_loop(0, tm // sub, step, 0)

    in_specs = [pl.BlockSpec((tm, w), functools.partial(lambda i, cb: (i, cb), cb=cb)) for _, w, cb in rows]
    in_specs += [pl.BlockSpec(p.shape, lambda i: (0, 0)) for p in params]
    out_specs = [pl.BlockSpec((tm, w), lambda i: (i, 0)) for w, _ in outs]
    out_specs += [pl.BlockSpec(s, lambda i: (0, 0)) for s in accs]
    out_shape = [jax.ShapeDtypeStruct((t, w), d) for w, d in outs]
    out_shape += [jax.ShapeDtypeStruct(s, F32) for s in accs]
    return pl.pallas_call(
        body, name=name, grid=(t // tm,), in_specs=in_specs, out_specs=out_specs, out_shape=out_shape,
        compiler_params=_params(("arbitrary",)),
    )(*[r[0] for r in rows], *params)


def _vjp_fn(fn, n_in, n_out):
    def bwd(*args):
        ins, cts = args[:n_in], args[n_in:]
        _, pull = jax.vjp(fn, *ins)
        return pull(tuple(cts) if n_out > 1 else cts[0])
    return bwd


def _lane(shape):
    return lax.broadcasted_iota(jnp.int32, shape, 1)


def _silu(x):
    return x * jax.nn.sigmoid(x)


def _softplus(x):
    return jnp.maximum(x, 0.0) + jnp.log1p(jnp.exp(-jnp.abs(x)))


def _heads(x, width=HEAD_DIM):
    return [x[:, h * width:(h + 1) * width] for h in range(N_HEADS)]


def _layer_norm(z, g, b):
    mu = jnp.mean(z, -1, keepdims=True)
    zc = z - mu
    var = jnp.mean(zc * zc, -1, keepdims=True)
    return zc * lax.rsqrt(var + LN_EPS) * g + b


def _gdn_act(u, misc, alog_row, dtb_row):
    s = _silu(u)
    w = N_HEADS * HEAD_DIM
    q = jnp.concatenate([t * lax.rsqrt(jnp.sum(t * t, -1, keepdims=True) + RMS_EPS) * HEAD_DIM ** -0.5
                         for t in _heads(s[:, :w])], axis=1)
    k = jnp.concatenate([t * lax.rsqrt(jnp.sum(t * t, -1, keepdims=True) + RMS_EPS)
                         for t in _heads(s[:, w:2 * w])], axis=1)
    v = s[:, 2 * w:]
    lane = _lane(misc.shape)
    beta = jax.nn.sigmoid(misc)
    g = -jnp.exp(alog_row) * _softplus(misc + dtb_row)
    is_beta = (lane >= MISC_BETA0) & (lane < MISC_BETA0 + N_HEADS)
    is_g = (lane >= MISC_A0) & (lane < MISC_A0 + N_HEADS)
    gb = jnp.where(is_beta, beta, jnp.where(is_g, g, 0.0))
    return q, k, v, gb


def _gdn_out(o, z, gn_row):
    outs = []
    for oh, zh in zip(_heads(o), _heads(z)):
        r = oh * lax.rsqrt(jnp.mean(oh * oh, -1, keepdims=True) + RMS_EPS) * gn_row
        outs.append(r * _silu(zh))
    return jnp.concatenate(outs, axis=1)


def _mla_norm(ckv, cq, kvg_row, qg_row):
    cqn = cq * lax.rsqrt(jnp.mean(cq * cq, -1, keepdims=True) + RMS_EPS) * qg_row
    ckvn = ckv * lax.rsqrt(jnp.mean(ckv * ckv, -1, keepdims=True) + RMS_EPS) * kvg_row
    return cqn, ckvn


def _swap_halves(x):
    half = ROPE_DIM // 2
    return jnp.where(_lane(x.shape) < half, pltpu.roll(x, LANES - half, 1), pltpu.roll(x, half, 1))


@jax.custom_vjp
def _rope(x, cos_t, sin_t):
    return x * cos_t + _swap_halves(x) * sin_t


def _rope_fwd(x, cos_t, sin_t):
    return _rope(x, cos_t, sin_t), (cos_t, sin_t)


def _rope_bwd(res, g):
    cos_t, sin_t = res
    return g * cos_t - _swap_halves(g) * sin_t, jnp.zeros_like(cos_t), jnp.zeros_like(sin_t)


_rope.defvjp(_rope_fwd, _rope_bwd)


def _mla_qk(scale, qm, kv, misc, cos_t, sin_t):
    krope = _rope(misc, cos_t, sin_t)
    qs, ks = [], []
    for h in range(N_HEADS):
        base = 2 * HEAD_DIM * h
        qs += [qm[:, base:base + HEAD_DIM], _rope(qm[:, base + HEAD_DIM:base + 2 * HEAD_DIM], cos_t, sin_t)]
        ks += [kv[:, HEAD_DIM * h:HEAD_DIM * (h + 1)], krope]
    return jnp.concatenate(qs, axis=1) * scale, jnp.concatenate(ks, axis=1), kv[:, N_HEADS * HEAD_DIM:]


def _swiglu(gu):
    f = gu.shape[1] // 2
    return _silu(gu[:, :f]) * gu[:, f:]


def _ple_out(x2, pg, pe):
    return x2 + jax.nn.sigmoid(pg) * pe


CONV_W = 4
HALO = 8


def _conv_fwd(h, conv_w, width, *, name, tm=ROW_TILE, sub=32):
    t = h.shape[0]
    tm = min(tm, t)
    nb = tm // HALO

    def body(x_ref, halo_ref, w_ref, u_ref, buf):
        i = pl.program_id(0)
        buf[pl.ds(0, HALO), :] = jnp.where(i > 0, halo_ref[...], 0.0)
        buf[pl.ds(HALO, tm), :] = x_ref[...]
        w = w_ref[...]
        for r0 in range(0, tm, sub):
            acc = jnp.zeros((sub, width), F32)
            for j in range(CONV_W):
                acc = acc + w[j:j + 1, :] * buf[pl.ds(HALO + r0 - (CONV_W - 1) + j, sub), :]
            u_ref[pl.ds(r0, sub), :] = acc

    return pl.pallas_call(
        body, name=name, grid=(t // tm,),
        in_specs=[pl.BlockSpec((tm, width), lambda i: (i, 0)),
                  pl.BlockSpec((HALO, width), lambda i: (jnp.maximum(i * nb - 1, 0), 0)),
                  pl.BlockSpec(conv_w.shape, lambda i: (0, 0))],
        out_specs=pl.BlockSpec((tm, width), lambda i: (i, 0)),
        out_shape=jax.ShapeDtypeStruct((t, width), F32),
        scratch_shapes=[pltpu.VMEM((tm + HALO, width), F32)],
        compiler_params=_params(("arbitrary",)),
    )(h, h, conv_w)


def _conv_bwd(du, h, conv_w, width, *, name, tm=ROW_TILE, sub=32):
    t = h.shape[0]
    tm = min(tm, t)
    nb = tm // HALO
    n_tiles = t // tm

    def body(du_ref, du_halo, x_ref, x_halo, w_ref, dx_ref, dw_ref, dbuf, xbuf):
        i = pl.program_id(0)

        @pl.when(i == 0)
        def _():
            dw_ref[...] = jnp.zeros_like(dw_ref)

        dbuf[pl.ds(0, tm), :] = du_ref[...]
        dbuf[pl.ds(tm, HALO), :] = jnp.where(i < n_tiles - 1, du_halo[...], 0.0)
        xbuf[pl.ds(0, HALO), :] = jnp.where(i > 0, x_halo[...], 0.0)
        xbuf[pl.ds(HALO, tm), :] = x_ref[...]
        w = w_ref[...]
        dws = [jnp.zeros((1, width), F32) for _ in range(CONV_W)]
        for r0 in range(0, tm, sub):
            acc = jnp.zeros((sub, width), F32)
            d_here = dbuf[pl.ds(r0, sub), :]
            for j in range(CONV_W):
                acc = acc + w[j:j + 1, :] * dbuf[pl.ds(r0 + (CONV_W - 1) - j, sub), :]
                xs = xbuf[pl.ds(HALO + r0 - (CONV_W - 1) + j, sub), :]
                dws[j] = dws[j] + jnp.sum(d_here * xs, axis=0, keepdims=True)
            dx_ref[pl.ds(r0, sub), :] = acc.astype(dx_ref.dtype)
        for j in range(CONV_W):
            dw_ref[pl.ds(j, 1), :] += dws[j]

    return pl.pallas_call(
        body, name=name, grid=(n_tiles,),
        in_specs=[pl.BlockSpec((tm, width), lambda i: (i, 0)),
                  pl.BlockSpec((HALO, width), lambda i: (jnp.minimum((i + 1) * nb, t // HALO - 1), 0)),
                  pl.BlockSpec((tm, width), lambda i: (i, 0)),
                  pl.BlockSpec((HALO, width), lambda i: (jnp.maximum(i * nb - 1, 0), 0)),
                  pl.BlockSpec(conv_w.shape, lambda i: (0, 0))],
        out_specs=[pl.BlockSpec((tm, width), lambda i: (i, 0)),
                   pl.BlockSpec((HALO, width), lambda i: (0, 0))],
        out_shape=[jax.ShapeDtypeStruct((t, width), BF16), jax.ShapeDtypeStruct((HALO, width), F32)],
        scratch_shapes=[pltpu.VMEM((tm + HALO, width), F32), pltpu.VMEM((tm + HALO, width), F32)],
        compiler_params=_params(("arbitrary",)),
    )(du, du, h, h, conv_w)


@jax.custom_vjp
def _inv_unit_lower(low):
    n = low.shape[-1]
    eye = (lax.broadcasted_iota(jnp.int32, (n, n), 0) == lax.broadcasted_iota(jnp.int32, (n, n), 1)).astype(F32)
    x = eye - low
    p = low
    span = 2
    while span < n:
        p = _hnn(p, p)
        x = x + _hnn(x, p)
        span *= 2
    return x


def _inv_fwd(low):
    x = _inv_unit_lower(low)
    return x, x


def _inv_bwd(x, g):
    return (-_htn(x, _hnt(g, x)),)


_inv_unit_lower.defvjp(_inv_fwd, _inv_bwd)


def _gdn_prep(q, k, v, gb):
    c = CHUNK
    n = q.shape[0] // c
    pairs = [(g, h) for g in range(n) for h in range(N_HEADS)]
    row = lax.broadcasted_iota(jnp.int32, (c, c), 0)
    col = lax.broadcasted_iota(jnp.int32, (c, c), 1)
    tri_incl = row >= col
    tri_strict = row > col
    lane = _lane((c, LANES))
    sub = lax.broadcasted_iota(jnp.int32, (LANES, c), 0)
    last = lax.broadcasted_iota(jnp.int32, (c, 1), 0) == c - 1

    def split(x):
        return jnp.stack([x[g * c:(g + 1) * c, h * HEAD_DIM:(h + 1) * HEAD_DIM] for g, h in pairs])

    gbs = [gb[g * c:(g + 1) * c, :] for g in range(n)]
    gbts = [x.T for x in gbs]
    g_col = jnp.stack([jnp.sum(jnp.where(lane == MISC_A0 + h, gbs[g], 0.0), axis=1, keepdims=True) for g, h in pairs])
    b_col = jnp.stack([jnp.sum(jnp.where(lane == MISC_BETA0 + h, gbs[g], 0.0), axis=1, keepdims=True) for g, h in pairs])
    g_row = jnp.stack([jnp.sum(jnp.where(sub == MISC_A0 + h, gbts[g], 0.0), axis=0, keepdims=True) for g, h in pairs])
    gc_col = jnp.sum(jnp.where(tri_incl, g_row, 0.0), axis=2, keepdims=True)
    gc_row = jnp.sum(jnp.where(row <= col, g_col, 0.0), axis=1, keepdims=True)
    decay = jnp.where(tri_incl, jnp.exp(jnp.where(tri_incl, gc_col - gc_row, 0.0)), 0.0)
    g_last = jnp.sum(jnp.where(last, gc_col, 0.0), axis=1, keepdims=True)
    qs, ks, vs = split(q), split(k), split(v)
    kb = ks * b_col
    low = jnp.where(tri_strict, _nt(kb, ks) * decay, 0.0)
    tinv = _inv_unit_lower(low)
    eg = jnp.exp(gc_col)
    sol = _hnn(tinv, jnp.concatenate([vs * b_col, kb * eg], axis=2))
    attn = jnp.where(tri_incl, _nt(qs, ks) * decay, 0.0)
    qd = qs * eg
    kd = ks * jnp.exp(g_last - gc_col)

    def merge(x):
        return jnp.concatenate([jnp.concatenate([x[g * N_HEADS + h] for h in range(N_HEADS)], axis=1)
                                for g in range(n)], axis=0)

    glb = jnp.concatenate([sum(jnp.where(lane == h, g_last[g * N_HEADS + h], 0.0) for h in range(N_HEADS))
                           for g in range(n)], axis=0)
    return merge(sol[:, :, :HEAD_DIM]), merge(sol[:, :, HEAD_DIM:]), merge(qd), merge(kd), merge(attn), glb


def _gdn_seq(state, u, w, qd, kd, attn, glb):
    c = u.shape[0]
    first = lax.broadcasted_iota(jnp.int32, glb.shape, 0) == 0
    lane = _lane(glb.shape)
    heads = lambda x: jnp.stack([x[:, h * HEAD_DIM:(h + 1) * HEAD_DIM] for h in range(N_HEADS)])
    g_last = jnp.stack([jnp.sum(jnp.sum(jnp.where(first & (lane == h), glb, 0.0), axis=1, keepdims=True),
                                axis=0, keepdims=True) for h in range(N_HEADS)])
    s = jnp.stack([state[h * HEAD_DIM:(h + 1) * HEAD_DIM, :] for h in range(N_HEADS)])
    at = jnp.stack([attn[:, h * c:(h + 1) * c] for h in range(N_HEADS)])
    v_new = heads(u) - _nn(heads(w), s)
    o = _nn(heads(qd), s) + _nn(at, v_new)
    s_new = s * jnp.exp(g_last) + _tn(heads(kd), v_new)
    return (jnp.concatenate([o[h] for h in range(N_HEADS)], axis=1),
            jnp.concatenate([s_new[h] for h in range(N_HEADS)], axis=0))


PREP_CHUNKS = 2
SEQ_CHUNKS = 8


def _gdn_prep_fwd(q, k, v, gb, *, name):
    t, w = q.shape
    rows = min(PREP_CHUNKS * CHUNK, t)

    def body(q_ref, k_ref, v_ref, gb_ref, *out_refs):
        res = _gdn_prep(q_ref[...], k_ref[...], v_ref[...], gb_ref[...])
        for o_ref, val in zip(out_refs, res):
            o_ref[...] = val

    spec = lambda width: pl.BlockSpec((rows, width), lambda i: (i, 0))
    widths = [w, w, w, w, N_HEADS * CHUNK, LANES]
    return pl.pallas_call(
        body, name=name, grid=(t // rows,),
        in_specs=[spec(w), spec(w), spec(w), spec(LANES)],
        out_specs=[spec(x) for x in widths],
        out_shape=[jax.ShapeDtypeStruct((t, x), F32) for x in widths],
        compiler_params=_params(("arbitrary",)),
    )(q, k, v, gb)


def _gdn_prep_bwd(q, k, v, gb, cts, *, name):
    t, w = q.shape
    rows = min(PREP_CHUNKS * CHUNK, t)

    def body(q_ref, k_ref, v_ref, gb_ref, du, dw, dqd, dkd, dattn, dglb, dq_ref, dk_ref, dv_ref, dgb_ref):
        _, pull = jax.vjp(_gdn_prep, q_ref[...], k_ref[...], v_ref[...], gb_ref[...])
        dq, dk, dv, dgb = pull(tuple(r[...] for r in (du, dw, dqd, dkd, dattn, dglb)))
        dq_ref[...] = dq
        dk_ref[...] = dk
        dv_ref[...] = dv
        dgb_ref[...] = dgb

    spec = lambda width: pl.BlockSpec((rows, width), lambda i: (i, 0))
    widths = [w, w, w, w, N_HEADS * CHUNK, LANES]
    return pl.pallas_call(
        body, name=name, grid=(t // rows,),
        in_specs=[spec(w), spec(w), spec(w), spec(LANES)] + [spec(x) for x in widths],
        out_specs=[spec(w), spec(w), spec(w), spec(LANES)],
        out_shape=[jax.ShapeDtypeStruct((t, w), F32)] * 3 + [jax.ShapeDtypeStruct((t, LANES), F32)],
        compiler_params=_params(("arbitrary",)),
    )(q, k, v, gb, *cts)


def _gdn_seq_fwd(prep, *, name):
    t, w = prep[0].shape
    rows = min(SEQ_CHUNKS * CHUNK, t)
    per = rows // CHUNK

    def body(u_ref, w_ref, qd_ref, kd_ref, at_ref, gl_ref, o_ref, sall_ref, s_scr):
        @pl.when(pl.program_id(0) == 0)
        def _():
            s_scr[...] = jnp.zeros_like(s_scr)

        def step(j, carry):
            sl = pl.ds(pl.multiple_of(j * CHUNK, CHUNK), CHUNK)
            s = s_scr[...]
            sall_ref[j] = s
            o, s_new = _gdn_seq(s, u_ref[sl, :], w_ref[sl, :], qd_ref[sl, :], kd_ref[sl, :], at_ref[sl, :], gl_ref[sl, :])
            o_ref[sl, :] = o
            s_scr[...] = s_new
            return carry

        lax.fori_loop(0, per, step, 0)

    spec = lambda width: pl.BlockSpec((rows, width), lambda i: (i, 0))
    widths = [w, w, w, w, N_HEADS * CHUNK, LANES]
    return pl.pallas_call(
        body, name=name, grid=(t // rows,),
        in_specs=[spec(x) for x in widths],
        out_specs=[spec(w), pl.BlockSpec((per, w, HEAD_DIM), lambda i: (i, 0, 0))],
        out_shape=[jax.ShapeDtypeStruct((t, w), F32), jax.ShapeDtypeStruct((t // CHUNK, w, HEAD_DIM), F32)],
        scratch_shapes=[pltpu.VMEM((w, HEAD_DIM), F32)],
        compiler_params=_params(("arbitrary",)),
    )(*prep)


def _gdn_seq_bwd(prep, s_all, do, *, name):
    t, w = prep[0].shape
    rows = min(SEQ_CHUNKS * CHUNK, t)
    per = rows // CHUNK
    n = t // rows

    def body(u_ref, w_ref, qd_ref, kd_ref, at_ref, gl_ref, sall_ref, do_ref, du, dw, dqd, dkd, dat, dgl, ds_scr):
        @pl.when(pl.program_id(0) == 0)
        def _():
            ds_scr[...] = jnp.zeros_like(ds_scr)

        def step(jj, carry):
            j = per - 1 - jj
            sl = pl.ds(pl.multiple_of(j * CHUNK, CHUNK), CHUNK)
            _, pull = jax.vjp(_gdn_seq, sall_ref[j], u_ref[sl, :], w_ref[sl, :], qd_ref[sl, :], kd_ref[sl, :],
                              at_ref[sl, :], gl_ref[sl, :])
            res = pull((do_ref[sl, :], ds_scr[...]))
            ds_scr[...] = res[0]
            for o_ref, val in zip((du, dw, dqd, dkd, dat, dgl), res[1:]):
                o_ref[sl, :] = val
            return carry

        lax.fori_loop(0, per, step, 0)

    spec = lambda width: pl.BlockSpec((rows, width), lambda i: (n - 1 - i, 0))
    widths = [w, w, w, w, N_HEADS * CHUNK, LANES]
    return pl.pallas_call(
        body, name=name, grid=(n,),
        in_specs=[spec(x) for x in widths] + [pl.BlockSpec((per, w, HEAD_DIM), lambda i: (n - 1 - i, 0, 0)), spec(w)],
        out_specs=[spec(x) for x in widths],
        out_shape=[jax.ShapeDtypeStruct((t, x), F32) for x in widths],
        scratch_shapes=[pltpu.VMEM((w, HEAD_DIM), F32)],
        compiler_params=_params(("arbitrary",)),
    )(*prep, s_all, do)


QK_DIM = 2 * HEAD_DIM
ATT_TILE = 1024
NEG = -1e30


ATT_SPLIT = 4


def _chunk_mask(n_rows, n_cols, key_major, query_offset):
    r = lax.broadcasted_iota(jnp.int32, (n_rows, n_cols), 0)
    c = lax.broadcasted_iota(jnp.int32, (n_rows, n_cols), 1)
    if key_major:
        return r // CHUNK <= (c + query_offset) // CHUNK
    return c // CHUNK <= (r + query_offset) // CHUNK


def _dot_nt(a, b):
    return lax.dot_general(a, b, NT, preferred_element_type=F32)


def _dot_nn(a, b):
    return lax.dot_general(a, b, NN, preferred_element_type=F32)


def _blocked_transpose(x, width):
    t = x.shape[0]
    tile = min(ATT_TILE, t)
    return x.reshape(t // tile, tile, N_HEADS, width).transpose(2, 0, 3, 1)


def _attn_fwd(q, kt, v1, *, name):
    t = q.shape[0]
    tq = min(ATT_TILE, t)
    nq = t // tq

    def body(q_ref, kt_ref, v_ref, o_ref, lse_ref, m_scr, acc_scr):
        qi = pl.program_id(1)
        m_scr[...] = jnp.full_like(m_scr, NEG)
        acc_scr[...] = jnp.zeros_like(acc_scr)
        hq = tq // ATT_SPLIT
        parts = [pl.ds(a * hq, hq) for a in range(ATT_SPLIT)]
        qs = [q_ref[sl, :] for sl in parts]

        def step(kj, masked):
            rows = pl.ds(pl.multiple_of(kj * tq, tq), tq)
            kt_blk, vv = kt_ref[kj], v_ref[rows, :]
            ss = [_dot_nn(qv, kt_blk) for qv in qs]
            for a, sl in enumerate(parts):
                s = ss[a]
                if masked:
                    s = jnp.where(_chunk_mask(hq, tq, False, a * hq), s, NEG)
                m_old = m_scr[sl, :]
                m_new = jnp.maximum(m_old, jnp.max(s, axis=1, keepdims=True))
                p = jnp.exp(s - m_new)
                acc_scr[sl, :] = jnp.exp(m_old - m_new) * acc_scr[sl, :] + _dot_nn(p.astype(BF16), vv)
                m_scr[sl, :] = m_new

        def loop_body(kj, carry):
            step(kj, False)
            return carry

        lax.fori_loop(0, qi, loop_body, 0)
        step(qi, True)
        acc = acc_scr[...]
        o_ref[...] = (acc[:, :HEAD_DIM] / acc[:, HEAD_DIM:]).astype(o_ref.dtype)
        lse_ref[...] = m_scr[...] + jnp.log(acc[:, HEAD_DIM:HEAD_DIM + 1])

    return pl.pallas_call(
        body, name=name, grid=(N_HEADS, nq),
        in_specs=[pl.BlockSpec((tq, QK_DIM), lambda h, i: (i, h)),
                  pl.BlockSpec((None, nq, QK_DIM, tq), lambda h, i: (h, 0, 0, 0)),
                  pl.BlockSpec((t, 2 * HEAD_DIM), lambda h, i: (0, h))],
        out_specs=[pl.BlockSpec((tq, HEAD_DIM), lambda h, i: (i, h)),
                   pl.BlockSpec((None, tq, 1), lambda h, i: (h, i, 0))],
        out_shape=[jax.ShapeDtypeStruct((t, N_HEADS * HEAD_DIM), BF16),
                   jax.ShapeDtypeStruct((N_HEADS, t, 1), F32)],
        scratch_shapes=[pltpu.VMEM((tq, 1), F32), pltpu.VMEM((tq, 2 * HEAD_DIM), F32)],
        compiler_params=_params(("arbitrary", "arbitrary")),
    )(q, kt, v1)


def _attn_bwd_dq(q, k, kt, vt, o, lse, dom, do_col0, *, name):
    t = q.shape[0]
    tq = min(ATT_TILE, t)
    nq = t // tq

    def body(q_ref, k_ref, kt_ref, vt_ref, o_ref, lse_ref, do_ref, dq_ref, delta_ref, acc_scr):
        qi = pl.program_id(1)
        acc_scr[...] = jnp.zeros_like(acc_scr)
        do = do_ref[...]
        delta = jnp.sum(do * o_ref[...].astype(F32), axis=1, keepdims=True)
        delta_ref[...] = delta
        hq = tq // ATT_SPLIT
        parts = [pl.ds(a * hq, hq) for a in range(ATT_SPLIT)]
        qs = [q_ref[sl, :] for sl in parts]
        dos = [do_ref[sl, :].astype(BF16) for sl in parts]
        lses = [lse_ref[sl, :] for sl in parts]
        deltas = [delta[a * hq:(a + 1) * hq, :] for a in range(ATT_SPLIT)]

        def step(kj, masked):
            rows = pl.ds(pl.multiple_of(kj * tq, tq), tq)
            kt_blk, vt_blk = kt_ref[kj], vt_ref[kj]
            ss = [_dot_nn(qv, kt_blk) for qv in qs]
            dps = [_dot_nn(do_b, vt_blk) for do_b in dos]
            kv_ = k_ref[rows, :]
            for a, sl in enumerate(parts):
                p = jnp.exp(ss[a] - lses[a])
                if masked:
                    p = jnp.where(_chunk_mask(hq, tq, False, a * hq), p, 0.0)
                ds = p * (dps[a] - deltas[a])
                acc_scr[sl, :] += _dot_nn(ds.astype(BF16), kv_)

        def loop_body(kj, carry):
            step(kj, False)
            return carry

        lax.fori_loop(0, qi, loop_body, 0)
        step(qi, True)
        dq_ref[...] = acc_scr[...].astype(dq_ref.dtype)

    return pl.pallas_call(
        body, name=name, grid=(N_HEADS, nq),
        in_specs=[pl.BlockSpec((tq, QK_DIM), lambda h, i: (i, h)),
                  pl.BlockSpec((t, QK_DIM), lambda h, i: (0, h)),
                  pl.BlockSpec((None, nq, QK_DIM, tq), lambda h, i: (h, 0, 0, 0)),
                  pl.BlockSpec((None, nq, HEAD_DIM, tq), lambda h, i: (h, 0, 0, 0)),
                  pl.BlockSpec((tq, HEAD_DIM), lambda h, i: (i, h)),
                  pl.BlockSpec((None, tq, 1), lambda h, i: (h, i, 0)),
                  pl.BlockSpec((tq, HEAD_DIM), lambda h, i: (i, do_col0 + h))],
        out_specs=[pl.BlockSpec((tq, QK_DIM), lambda h, i: (i, h)),
                   pl.BlockSpec((None, tq, 1), lambda h, i: (h, i, 0))],
        out_shape=[jax.ShapeDtypeStruct((t, N_HEADS * QK_DIM), BF16),
                   jax.ShapeDtypeStruct((N_HEADS, t, 1), F32)],
        scratch_shapes=[pltpu.VMEM((tq, QK_DIM), F32)],
        compiler_params=_params(("arbitrary", "arbitrary")),
    )(q, k, kt, vt, o, lse, dom)


def _attn_bwd_dkv(q, qt, k, v, lse_row, delta_row, dom, dot, do_col0, *, name):
    t = q.shape[0]
    tk = min(ATT_TILE, t)
    nk = t // tk

    def body(q_ref, qt_ref, k_ref, v_ref, lse_ref, delta_ref, do_ref, dot_ref, dk_ref, dv_ref, dk_scr, dv_scr):
        kj = pl.program_id(1)
        dk_scr[...] = jnp.zeros_like(dk_scr)
        dv_scr[...] = jnp.zeros_like(dv_scr)
        kv_ = k_ref[...]
        vv = v_ref[...]

        hq = tk // ATT_SPLIT

        def step(qi, masked):
            lse_v, delta_v = lse_ref[qi], delta_ref[qi]
            qt_blk, dot_blk = qt_ref[qi], dot_ref[qi]
            qs, dos = [], []
            for a in range(ATT_SPLIT):
                rows = pl.ds(pl.multiple_of(qi * tk + a * hq, hq), hq)
                qs.append(q_ref[rows, :])
                dos.append(do_ref[rows, :].astype(BF16))
            ss = [_dot_nn(kv_, qt_blk[:, a * hq:(a + 1) * hq]) for a in range(ATT_SPLIT)]
            dps = [_dot_nn(vv, dot_blk[:, a * hq:(a + 1) * hq]) for a in range(ATT_SPLIT)]
            for a in range(ATT_SPLIT):
                cols = slice(a * hq, (a + 1) * hq)
                p = jnp.exp(ss[a] - lse_v[:, cols])
                if masked:
                    p = jnp.where(_chunk_mask(tk, hq, True, a * hq), p, 0.0)
                dv_scr[...] += _dot_nn(p.astype(BF16), dos[a])
                ds = p * (dps[a] - delta_v[:, cols])
                dk_scr[...] += _dot_nn(ds.astype(BF16), qs[a])

        step(kj, True)

        def loop_body(qi, carry):
            step(qi, False)
            return carry

        lax.fori_loop(kj + 1, nk, loop_body, 0)
        dk_ref[...] = dk_scr[...].astype(dk_ref.dtype)
        dv_ref[...] = dv_scr[...].astype(dv_ref.dtype)

    stat = pl.BlockSpec((None, nk, 1, tk), lambda h, j: (h, 0, 0, 0))
    return pl.pallas_call(
        body, name=name, grid=(N_HEADS, nk),
        in_specs=[pl.BlockSpec((t, QK_DIM), lambda h, j: (0, h)),
                  pl.BlockSpec((None, nk, QK_DIM, tk), lambda h, j: (h, 0, 0, 0)),
                  pl.BlockSpec((tk, QK_DIM), lambda h, j: (j, h)),
                  pl.BlockSpec((tk, HEAD_DIM), lambda h, j: (j, h)),
                  stat, stat,
                  pl.BlockSpec((t, HEAD_DIM), lambda h, j: (0, do_col0 + h)),
                  pl.BlockSpec((None, nk, HEAD_DIM, tk), lambda h, j: (h, 0, 0, 0))],
        out_specs=[pl.BlockSpec((tk, QK_DIM), lambda h, j: (j, h)),
                   pl.BlockSpec((tk, HEAD_DIM), lambda h, j: (j, h))],
        out_shape=[jax.ShapeDtypeStruct((t, N_HEADS * QK_DIM), BF16),
                   jax.ShapeDtypeStruct((t, N_HEADS * HEAD_DIM), BF16)],
        scratch_shapes=[pltpu.VMEM((tk, QK_DIM), F32), pltpu.VMEM((tk, HEAD_DIM), F32)],
        compiler_params=_params(("arbitrary", "arbitrary")),
    )(q, qt, k, v, lse_row, delta_row, dom, dot)


def _rope_tables(pos_col, inv_freq_row, *, name):
    t = pos_col.shape[0]
    tm = min(ROW_TILE, t)

    def body(p_ref, f_ref, c_ref, s_ref):
        ang = p_ref[...].astype(F32) * f_ref[...]
        lane = _lane(ang.shape)
        c_ref[...] = jnp.where(lane < ROPE_DIM, jnp.cos(ang), 0.0)
        sn = jnp.sin(ang)
        s_ref[...] = jnp.where(lane < ROPE_DIM // 2, -sn, jnp.where(lane < ROPE_DIM, sn, 0.0))

    out = pl.BlockSpec((tm, LANES), lambda i: (i, 0))
    return pl.pallas_call(
        body, name=name, grid=(t // tm,),
        in_specs=[pl.BlockSpec((tm, 1), lambda i: (i, 0)), pl.BlockSpec((1, LANES), lambda i: (0, 0))],
        out_specs=[out, out], out_shape=[jax.ShapeDtypeStruct((t, LANES), F32)] * 2,
        compiler_params=_params(("arbitrary",)),
    )(pos_col, inv_freq_row)


def _loss_head(y, target):
    width = y.shape[1]

    def fn(yv, tv):
        e = yv - tv
        part = 0.5 * jnp.sum(jnp.mean(e * e, axis=1, keepdims=True), axis=0, keepdims=True)
        return e * (1.0 / width), jnp.broadcast_to(part, (HALO, LANES))

    return _rowwise(fn, [(y, width, 0), (target, width, 0)], [], [(width, F32)], [(HALO, LANES)], name="loss_head")


def _adamw(w, g, m, v, *, name):
    shape = w.shape
    w2, g2, m2, v2 = (a.reshape(-1, shape[-1]) for a in (w, g, m, v))
    rows, width = w2.shape
    tr = _divisor_tile(rows, max(8, (1 << 19) // max(width, 1)), 8)
    bc1 = 1.0 - ADAM_B1 ** ADAM_STEP
    bc2 = 1.0 - ADAM_B2 ** ADAM_STEP

    def body(w_ref, g_ref, m_ref, v_ref, d_ref, mo_ref, vo_ref):
        gv = g_ref[...]
        mn = ADAM_B1 * m_ref[...] + (1.0 - ADAM_B1) * gv
        vn = ADAM_B2 * v_ref[...] + (1.0 - ADAM_B2) * (gv * gv)
        d_ref[...] = -ADAM_LR * ((mn / bc1) / (jnp.sqrt(vn / bc2) + ADAM_EPS) + ADAM_WD * w_ref[...])
        mo_ref[...] = mn
        vo_ref[...] = vn

    spec = pl.BlockSpec((tr, width), lambda i: (i, 0))
    outs = pl.pallas_call(
        body, name=name, grid=(rows // tr,), in_specs=[spec] * 4, out_specs=[spec] * 3,
        out_shape=[jax.ShapeDtypeStruct((rows, width), F32)] * 3,
        compiler_params=_params(("arbitrary",)),
    )(w2, g2, m2, v2)
    return tuple(o.reshape(shape) for o in outs)


HBM_SPEC = pl.BlockSpec(memory_space=pltpu.HBM)


def _position():
    return lax.axis_index("x"), lax.axis_index("y"), lax.axis_index("c")


def _other_chips(x, y):
    return [(1 - x, y), (x, 1 - y), (1 - x, 1 - y)]


class _Stream:
    def __init__(self, kind, size=0):
        self.kind, self.size = kind, size
        self.parts = 2 if kind == "heads" else 1

    def local(self, ref, k, part):
        if self.kind == "rows":
            return ref.at[:, pl.ds(k * self.size, self.size), :]
        if self.kind == "cols":
            return ref.at[:, :, pl.ds(k * self.size, self.size)]
        if self.kind == "heads":
            return ref.at[:, :, pl.ds(part * N_HEADS * HEAD_DIM + k * HEAD_DIM, HEAD_DIM)]
        if self.kind == "piece":
            return ref.at[k]
        return ref

    def shard(self, ref, part):
        if self.kind == "heads":
            return ref.at[:, :, pl.ds(part * HEAD_DIM, HEAD_DIM)]
        return ref

    def half_local(self, ref, k, part, cc, hd):
        if self.kind == "piece":
            return ref.at[k, pl.ds(cc * hd, hd)]
        return self.local(ref.at[pl.ds(cc * hd, hd)], k, part)


def _remote(src, dst, send_sems, recv_sems, idx, to):
    return pltpu.make_async_remote_copy(src_ref=src, dst_ref=dst, send_sem=send_sems.at[idx],
                                        recv_sem=recv_sems.at[idx], device_id=to, device_id_type=MESH)


def _comm_call(body, ins, out_shapes, n_remote, n_local, *, name):
    scratch = [pltpu.SemaphoreType.DMA((n_remote,)), pltpu.SemaphoreType.DMA((n_remote,))]
    if n_local:
        scratch.append(pltpu.SemaphoreType.DMA((n_local,)))
    return pl.pallas_call(
        body, name=name, in_specs=[HBM_SPEC] * len(ins), out_specs=[HBM_SPEC] * len(out_shapes), out_shape=out_shapes,
        scratch_shapes=scratch, compiler_params=pltpu.CompilerParams(has_side_effects=True),
    )(*ins)


def _gather_chips(shards, streams, out_shapes, *, name):
    n = len(shards)
    hd = shards[0].shape[0] // 2
    flat = [(t, part) for t in range(n) for part in range(streams[t].parts)]
    ns = len(flat)

    def body(*refs):
        s_refs, o_refs = refs[:n], refs[n:2 * n]
        send_sems, recv_sems = refs[2 * n:]
        x, y, c = _position()
        sibling = (x, y, 1 - c)
        chips = _other_chips(x, y)
        me = 2 * x + y
        sent = []
        for s, (t, part) in enumerate(flat):
            st = streams[t]
            src = st.shard(s_refs[t].at[pl.ds(c * hd, hd)], part)
            for j, (cx, cy) in enumerate(chips):
                sent.append(_remote(src, st.half_local(o_refs[t], me, part, c, hd), send_sems, recv_sems,
                                    3 * s + j, (cx, cy, c)))
                sent[-1].start()
        for s, (t, part) in enumerate(flat):
            st = streams[t]
            for j, (cx, cy) in enumerate(chips):
                blk = st.half_local(o_refs[t], 2 * cx + cy, part, c, hd)
                _remote(blk, blk, send_sems, recv_sems, 3 * s + j, (x, y, c)).wait_recv()
                sent.append(_remote(blk, blk, send_sems, recv_sems, 3 * ns + 3 * s + j, sibling))
                sent[-1].start()
        for s, (t, part) in enumerate(flat):
            st = streams[t]
            for j, (cx, cy) in enumerate(chips):
                blk = st.half_local(o_refs[t], 2 * cx + cy, part, 1 - c, hd)
                _remote(blk, blk, send_sems, recv_sems, 3 * ns + 3 * s + j, (x, y, c)).wait_recv()
        for cp in sent:
            cp.wait_send()

    return _comm_call(body, shards, out_shapes, 6 * ns, 0, name=name)


def _sibling_take_other_half(gs, *, name):
    n = len(gs)
    hd = gs[0].shape[0] // 2

    def body(*refs):
        g_refs, o_refs = refs[:n], refs[n:2 * n]
        send_sems, recv_sems = refs[2 * n:]
        x, y, c = _position()
        copies = [_remote(g_refs[t].at[pl.ds((1 - c) * hd, hd)], o_refs[t], send_sems, recv_sems, t, (x, y, 1 - c))
                  for t in range(n)]
        for cp in copies:
            cp.start()
        for cp in copies:
            cp.wait()

    outs = [jax.ShapeDtypeStruct((hd,) + g.shape[1:], g.dtype) for g in gs]
    return _comm_call(body, gs, outs, n, 0, name=name)


def _chips_exchange(ps, streams, shard_shapes, *, name):
    n = len(ps)
    flat = [(t, part) for t in range(n) for part in range(streams[t].parts)]

    def body(*refs):
        p_refs, o_refs = refs[:n], refs[n:2 * n]
        send_sems, recv_sems = refs[2 * n:]
        x, y, c = _position()
        copies = []
        for s, (t, part) in enumerate(flat):
            st = streams[t]
            for j, (cx, cy) in enumerate(_other_chips(x, y)):
                copies.append(_remote(st.local(p_refs[t], 2 * cx + cy, part), st.shard(o_refs[t].at[j], part),
                                      send_sems, recv_sems, 3 * s + j, (cx, cy, c)))
        for cp in copies:
            cp.start()
        for cp in copies:
            cp.wait()

    outs = [jax.ShapeDtypeStruct((3,) + tuple(shp), p.dtype) for p, shp in zip(ps, shard_shapes)]
    return _comm_call(body, ps, outs, 3 * len(flat), 0, name=name)


def _sibling_join_halves(bufs, *, name):
    n = len(bufs)
    hd = bufs[0].shape[0] // 2

    def body(*refs):
        o_refs = refs[n:2 * n]
        send_sems, recv_sems = refs[2 * n:]
        x, y, c = _position()
        sent = []
        for t in range(n):
            mine = o_refs[t].at[pl.ds(c * hd, hd)]
            sent.append(_remote(mine, mine, send_sems, recv_sems, t, (x, y, 1 - c)))
            sent[-1].start()
        for t in range(n):
            theirs = o_refs[t].at[pl.ds((1 - c) * hd, hd)]
            _remote(theirs, theirs, send_sems, recv_sems, t, (x, y, c)).wait_recv()
        for cp in sent:
            cp.wait_send()

    return pl.pallas_call(
        body, name=name, in_specs=[HBM_SPEC] * n, out_specs=[HBM_SPEC] * n,
        out_shape=[jax.ShapeDtypeStruct(b.shape, b.dtype) for b in bufs],
        scratch_shapes=[pltpu.SemaphoreType.DMA((n,)), pltpu.SemaphoreType.DMA((n,))],
        input_output_aliases={t: t for t in range(n)},
        compiler_params=pltpu.CompilerParams(has_side_effects=True),
    )(*bufs)


def _row_tile(rows, width):
    return _divisor_tile(rows, max(16, (1 << 19) // width), 16)


def _add_own_half(g, got, c_idx, out_dtype, *, name):
    hd, r, w = got.shape
    tr = _row_tile(r, w)

    def body(c_ref, g_ref, a_ref, o_ref):
        o_ref[...] = (g_ref[...] + a_ref[...]).astype(o_ref.dtype)

    return pl.pallas_call(
        body, name=name,
        grid_spec=pltpu.PrefetchScalarGridSpec(
            num_scalar_prefetch=1, grid=(hd, r // tr),
            in_specs=[pl.BlockSpec((None, None, tr, w), lambda l, i, c_ref: (c_ref[0], l, i, 0)),
                      pl.BlockSpec((None, tr, w), lambda l, i, c_ref: (l, i, 0))],
            out_specs=pl.BlockSpec((None, tr, w), lambda l, i, c_ref: (l, i, 0))),
        out_shape=jax.ShapeDtypeStruct((hd, r, w), out_dtype),
        compiler_params=_params(("arbitrary", "arbitrary")),
    )(c_idx, g.reshape((2, hd) + g.shape[1:]), got)


def _sum_chips(p, got, place, stream, *, name):
    _, hd, rs, cs = got.shape
    wb = HEAD_DIM if stream.kind == "heads" else cs
    tr = _row_tile(rs, wb)
    kind, size = stream.kind, stream.size

    def own_index(l, i, g, k_ref, c_ref):
        k = k_ref[0]
        if kind == "rows":
            return (l, k * (size // tr) + i, 0)
        if kind == "cols":
            return (l, i, k)
        if kind == "heads":
            return (l, i, g * N_HEADS + k)
        if kind == "piece":
            return (k, l, i, 0)
        return (l, i, 0)

    own_blk = (None, None, tr, wb) if kind == "piece" else (None, tr, wb)

    def body(k_ref, c_ref, p_ref, fx_ref, fy_ref, fxy_ref, o_ref):
        f = lambda r: r[...].astype(F32)
        o_ref[...] = (f(p_ref) + f(fy_ref)) + (f(fx_ref) + f(fxy_ref))

    def rel(j):
        return pl.BlockSpec((None, None, tr, wb), functools.partial(lambda l, i, g, k_ref, c_ref, j: (j, l, i, g), j=j))

    return pl.pallas_call(
        body, name=name,
        grid_spec=pltpu.PrefetchScalarGridSpec(
            num_scalar_prefetch=2, grid=(hd, rs // tr, stream.parts),
            in_specs=[pl.BlockSpec(own_blk, own_index), rel(0), rel(1), rel(2)],
            out_specs=pl.BlockSpec((None, tr, wb), lambda l, i, g, k_ref, c_ref: (c_ref[0] * hd + l, i, g))),
        out_shape=jax.ShapeDtypeStruct((2 * hd, rs, cs), F32),
        compiler_params=_params(("arbitrary", "arbitrary", "arbitrary")),
    )(place[0], place[1], p, got, got, got)


MATRICES = ("w_in", "w_uq", "w_ukv", "w_out", "w_gate_up", "w_down", "w_ple", "w_ple_gate", "conv_w")
VECTORS = ("a_log", "dt_bias", "gdn_norm_g", "q_norm_g", "kv_norm_g", "ln1_g", "ln1_b", "ln2_g", "ln2_b")
WEIGHTS = ("w_in", "conv_w", "a_log", "dt_bias", "gdn_norm_g", "q_norm_g", "w_uq", "kv_norm_g", "w_ukv", "w_out",
           "ln1_g", "ln1_b", "w_gate_up", "w_down", "ln2_g", "ln2_b", "w_ple", "w_ple_gate")
ROW_SHARDED = ("w_out", "w_down", "w_ple_gate")
N_CHIPS = 4


def _stream_of(name, shard_shape):
    if name in ("w_in", "w_uq"):
        return _Stream("piece")
    if name == "w_ukv":
        return _Stream("heads")
    if name in ROW_SHARDED:
        return _Stream("rows", shard_shape[0])
    return _Stream("cols", shard_shape[1])


def _place_own_block(local, shard, stream, chip):
    z = jnp.zeros((), jnp.int32)
    if stream.kind == "piece":
        return lax.dynamic_update_slice(local, shard[None], (chip, z, z, z))
    if stream.kind == "rows":
        return lax.dynamic_update_slice(local, shard, (z, chip * stream.size, z))
    if stream.kind == "cols":
        return lax.dynamic_update_slice(local, shard, (z, z, chip * stream.size))
    for part in range(stream.parts):
        piece = shard[:, :, part * HEAD_DIM:(part + 1) * HEAD_DIM]
        local = lax.dynamic_update_slice(local, piece, (z, z, part * N_HEADS * HEAD_DIM + chip * HEAD_DIM))
    return local


def _pack_vectors(vecs, depth):
    flat = jnp.concatenate([vecs[n].reshape(depth, -1) for n in VECTORS], axis=1)
    pad = jnp.zeros((depth, VEC_ROWS * LANES - flat.shape[1]), F32)
    return jnp.concatenate([flat, pad], axis=1).reshape(depth, VEC_ROWS, LANES)


def _unpack_vectors(packed, shapes):
    depth = packed.shape[0]
    flat = packed.reshape(depth, VEC_ROWS * LANES)
    out, off = {}, 0
    for n in VECTORS:
        out[n] = flat[:, off:off + shapes[n][1]]
        off += shapes[n][1]
    return out


VEC_ROWS = 40


class _Dims:
    def __init__(self, d_model, in_width, q_lora, kv_lora, d_ff2, ple_dim):
        self.d = d_model
        self.hw = N_HEADS * HEAD_DIM
        self.in_width = in_width
        self.q_lora, self.kv_lora = q_lora, kv_lora
        self.ff2 = d_ff2
        self.ple = ple_dim
        self.c_kv0 = 4 * self.hw
        self.c_q0 = self.c_kv0 + kv_lora
        self.misc0 = self.c_q0 + q_lora
        self.h_width = self.misc0 + LANES
        assert self.c_kv0 % kv_lora == 0 and self.c_q0 % q_lora == 0 and self.misc0 % LANES == 0
        self.g_beta = 4 * self.hw
        self.g_a = self.g_beta + N_HEADS
        self.g_cq = self.g_a + N_HEADS
        self.g_ckv = self.g_cq + q_lora
        self.g_kr = self.g_ckv + kv_lora
        assert self.g_kr + ROPE_DIM == in_width

    def w_in_local(self, w):
        pad = jnp.zeros(w.shape[:-1] + (self.h_width - self.in_width,), w.dtype)
        return jnp.concatenate([w[..., :self.g_beta], w[..., self.g_ckv:self.g_kr], w[..., self.g_cq:self.g_ckv],
                                w[..., self.g_kr:], w[..., self.g_beta:self.g_cq], pad], axis=-1)

    def w_in_global(self, d):
        m = self.misc0
        return jnp.concatenate([d[..., :self.c_kv0], d[..., m + MISC_BETA0:m + MISC_A0 + N_HEADS],
                                d[..., self.c_q0:self.misc0], d[..., self.c_kv0:self.c_q0], d[..., m:m + ROPE_DIM]],
                               axis=-1)

    def w_uq_local(self, w):
        r = w.reshape(w.shape[:-1] + (N_HEADS, HEAD_DIM + ROPE_DIM))
        r = jnp.pad(r, [(0, 0)] * (r.ndim - 1) + [(0, QK_DIM - HEAD_DIM - ROPE_DIM)])
        return r.reshape(w.shape[:-1] + (N_HEADS * QK_DIM,))

    def w_uq_global(self, d):
        r = d.reshape(d.shape[:-1] + (N_HEADS, QK_DIM))[..., :HEAD_DIM + ROPE_DIM]
        return r.reshape(d.shape[:-1] + (N_HEADS * (HEAD_DIM + ROPE_DIM),))


def _lane_padded(n):
    return -(-n // LANES) * LANES


def _pad_lanes(a):
    pad = _lane_padded(a.shape[-1]) - a.shape[-1]
    return a if pad == 0 else jnp.pad(a, [(0, 0)] * (a.ndim - 1) + [(0, pad)])


def _lane_row(vec, lane0):
    pad = LANES - lane0 - vec.shape[0]
    return jnp.concatenate([jnp.zeros((lane0,), F32), vec.astype(F32), jnp.zeros((pad,), F32)])[None, :]


def _layer_fwd(dm, alpha, x, xb, p_i, cos_t, sin_t, wl, tag):
    d, hw = dm.d, dm.hw
    nm = lambda s: f"{s}_{tag}"
    mm = functools.partial(_matmul, layer=wl["layer"])
    h = mm(xb, wl["w_in"], dims="nn", name=nm("f_in"))
    misc_cb = dm.misc0 // LANES

    u = _conv_fwd(h, wl["conv_w"], 3 * hw, name=nm("f_conv"))
    qn, kn, vg, gb = _rowwise(_gdn_act, [(u, 3 * hw, 0), (h, LANES, misc_cb)], [wl["alog_row"], wl["dtb_row"]],
                              [(hw, F32), (hw, F32), (hw, F32), (LANES, F32)], name=nm("f_gdn_act"))
    prep = _gdn_prep_fwd(qn, kn, vg, gb, name=nm("f_gdn_prep"))
    o_gdn, s_all = _gdn_seq_fwd(prep, name=nm("f_gdn_seq"))
    (og,) = _rowwise(lambda o, z, g: (_gdn_out(o, z, g),), [(o_gdn, hw, 0), (h, hw, 3)], [wl["gn_row"]],
                     [(hw, BF16)], name=nm("f_gdn_out"))

    cqn, ckvn = _rowwise(_mla_norm, [(h, dm.kv_lora, dm.c_kv0 // dm.kv_lora), (h, dm.q_lora, dm.c_q0 // dm.q_lora)],
                         [wl["kvg_row"], wl["qg_row"]], [(dm.q_lora, BF16), (dm.kv_lora, BF16)], name=nm("f_mla_norm"))
    qm = mm(cqn, wl["w_uq"], dims="nn", name=nm("f_uq"))
    kvm = mm(ckvn, wl["w_ukv"], dims="nn", name=nm("f_ukv"))
    scale = (HEAD_DIM + ROPE_DIM) ** -0.5
    qk_fn = functools.partial(_mla_qk, scale)
    qa, ka, va = _rowwise(qk_fn, [(qm, N_HEADS * QK_DIM, 0), (kvm, 2 * hw, 0), (h, LANES, misc_cb),
                                  (cos_t, LANES, 0), (sin_t, LANES, 0)], [],
                          [(N_HEADS * QK_DIM, BF16), (N_HEADS * QK_DIM, BF16), (hw, BF16)], name=nm("f_mla_qk"))
    kt = _blocked_transpose(ka, QK_DIM)
    v_heads = va.reshape(va.shape[0], N_HEADS, HEAD_DIM)
    v1 = jnp.concatenate([v_heads, jnp.ones_like(v_heads)], axis=2).reshape(va.shape[0], 2 * hw)
    o_mla, lse = _attn_fwd(qa, kt, v1, name=nm("f_attn"))

    om = jnp.concatenate([og, o_mla], axis=1)
    mix = mm(om, wl["w_out"], dims="nn", name=nm("f_out"))
    ln1 = lambda xv, yv, g, b: (_layer_norm(alpha * xv + yv, g, b),) * 2
    x1, x1b = _rowwise(ln1, [(x, d, 0), (mix, d, 0)], [wl["ln1_g"], wl["ln1_b"]], [(d, F32), (d, BF16)], name=nm("f_ln1"))

    gu = mm(x1b, wl["w_gate_up"], dims="nn", name=nm("f_gate_up"), out_dtype=BF16)
    (act,) = _rowwise(lambda g_: (_swiglu(g_),), [(gu, dm.ff2, 0)], [], [(dm.ff2 // 2, BF16)], name=nm("f_swiglu"))
    dn = mm(act, wl["w_down"], dims="nn", name=nm("f_down"))
    x2, x2b = _rowwise(ln1, [(x1, d, 0), (dn, d, 0)], [wl["ln2_g"], wl["ln2_b"]], [(d, F32), (d, BF16)], name=nm("f_ln2"))

    pg = mm(x2b, wl["w_ple_gate"], dims="nn", name=nm("f_ple_gate"))
    pe = mm(p_i, wl["w_ple"], dims="nn", name=nm("f_ple"))
    out, outb = _rowwise(lambda a, b, c_: (_ple_out(a, b, c_),) * 2, [(x2, d, 0), (pg, d, 0), (pe, d, 0)], [],
                         [(d, F32), (d, BF16)], name=nm("f_ple_out"))
    saved = dict(x=x, xb=xb, p_i=p_i, h=h, u=u, qn=qn, kn=kn, vg=vg, gb=gb, prep=prep, s_all=s_all, o_gdn=o_gdn, cqn=cqn, ckvn=ckvn,
                 qm=qm, kvm=kvm, qa=qa, ka=ka, kt=kt, va=va, o_mla=o_mla, lse=lse, om=om, mix=mix, x1=x1, x1b=x1b, gu=gu,
                 act=act, dn=dn, x2=x2, x2b=x2b, pg=pg, pe=pe)
    return out, outb, saved


def _layer_bwd(dm, alpha, dout, sv, cos_t, sin_t, wl, gbuf, tag):
    d, hw = dm.d, dm.hw
    t = dout.shape[0]
    nm = lambda s: f"{s}_{tag}"
    gr = {}
    gbuf = dict(gbuf)
    misc_cb = dm.misc0 // LANES
    mm = functools.partial(_matmul, layer=wl["layer"])

    def wgrad(name_, a, g):
        gbuf[name_] = mm(a, g, dims="tn", name=nm("b_" + name_), into=gbuf[name_], tm=1408, tn=1408, tk=1024)

    dx2_a, dpg, dpe = _rowwise(_vjp_fn(_ple_out, 3, 1), [(sv["x2"], d, 0), (sv["pg"], d, 0), (sv["pe"], d, 0), (dout, d, 0)],
                               [], [(d, F32), (d, BF16), (d, BF16)], name=nm("b_ple_out"))
    wgrad("w_ple", sv["p_i"], dpe)
    wgrad("w_ple_gate", sv["x2b"], dpg)
    dx2 = mm(dpg, wl["w_ple_gate"], dims="nt", c=dx2_a, name=nm("b_x2"))

    def ln_bwd(xv, yv, ct, g, b):
        _, pull = jax.vjp(lambda a_, b_, c_, d_: _layer_norm(alpha * a_ + b_, c_, d_), xv, yv, g, b)
        return pull(ct)

    dx1_a, ddn, gr["ln2_g"], gr["ln2_b"] = _rowwise(
        ln_bwd, [(sv["x1"], d, 0), (sv["dn"], d, 0), (dx2, d, 0)], [wl["ln2_g"], wl["ln2_b"]],
        [(d, F32), (d, BF16)], [(1, d), (1, d)], name=nm("b_ln2"))
    wgrad("w_down", sv["act"], ddn)
    dact = mm(ddn, wl["w_down"], dims="nt", name=nm("b_act"), out_dtype=BF16)
    (dgu,) = _rowwise(_vjp_fn(_swiglu, 1, 1), [(sv["gu"], dm.ff2, 0), (dact, dm.ff2 // 2, 0)], [], [(dm.ff2, BF16)],
                      name=nm("b_swiglu"))
    wgrad("w_gate_up", sv["x1b"], dgu)
    dx1 = mm(dgu, wl["w_gate_up"], dims="nt", c=dx1_a, name=nm("b_x1"))

    dx_a, dmix, gr["ln1_g"], gr["ln1_b"] = _rowwise(
        ln_bwd, [(sv["x"], d, 0), (sv["mix"], d, 0), (dx1, d, 0)], [wl["ln1_g"], wl["ln1_b"]],
        [(d, F32), (d, BF16)], [(1, d), (1, d)], name=nm("b_ln1"))
    wgrad("w_out", sv["om"], dmix)
    dom = mm(dmix, wl["w_out"], dims="nt", name=nm("b_om"))

    nq = t // min(ATT_TILE, t)
    dqa, delta = _attn_bwd_dq(sv["qa"], sv["ka"], sv["kt"], _blocked_transpose(sv["va"], HEAD_DIM), sv["o_mla"], sv["lse"],
                              dom, hw // HEAD_DIM, name=nm("b_attn_dq"))
    lse_row = sv["lse"].reshape(N_HEADS, nq, 1, t // nq)
    delta_row = delta.reshape(N_HEADS, nq, 1, t // nq)
    dot = _blocked_transpose(dom[:, hw:].astype(BF16), HEAD_DIM)
    dka, dva = _attn_bwd_dkv(sv["qa"], _blocked_transpose(sv["qa"], QK_DIM), sv["ka"], sv["va"], lse_row, delta_row, dom, dot,
                             hw // HEAD_DIM, name=nm("b_attn_dkv"))
    scale = (HEAD_DIM + ROPE_DIM) ** -0.5
    qk_fn = functools.partial(_mla_qk, scale)

    def qk_bwd(qm, kvm, misc, cs, sn, g_q, g_k, g_v):
        _, pull = jax.vjp(lambda a, b, c_: qk_fn(a, b, c_, cs, sn), qm, kvm, misc)
        return pull((g_q, g_k, g_v))

    dqm, dkvm, dmisc_rope = _rowwise(
        qk_bwd, [(sv["qm"], N_HEADS * QK_DIM, 0), (sv["kvm"], 2 * hw, 0), (sv["h"], LANES, misc_cb), (cos_t, LANES, 0),
                 (sin_t, LANES, 0), (dqa, N_HEADS * QK_DIM, 0), (dka, N_HEADS * QK_DIM, 0), (dva, hw, 0)], [],
        [(N_HEADS * QK_DIM, BF16), (2 * hw, BF16), (LANES, F32)], name=nm("b_mla_qk"))
    wgrad("w_uq", sv["cqn"], dqm)
    wgrad("w_ukv", sv["ckvn"], dkvm)
    dcqn = mm(dqm, wl["w_uq"], dims="nt", name=nm("b_cqn"))
    dckvn = mm(dkvm, wl["w_ukv"], dims="nt", name=nm("b_ckvn"))

    def norm_bwd(ckv, cq, g_q, g_kv, kvg, qg):
        _, pull = jax.vjp(_mla_norm, ckv, cq, kvg, qg)
        return pull((g_q, g_kv))

    dckv, dcq, gr["kvg_row"], gr["qg_row"] = _rowwise(
        norm_bwd, [(sv["h"], dm.kv_lora, dm.c_kv0 // dm.kv_lora), (sv["h"], dm.q_lora, dm.c_q0 // dm.q_lora),
                   (dcqn, dm.q_lora, 0), (dckvn, dm.kv_lora, 0)], [wl["kvg_row"], wl["qg_row"]],
        [(dm.kv_lora, BF16), (dm.q_lora, BF16)], [(1, dm.kv_lora), (1, dm.q_lora)], name=nm("b_mla_norm"))

    def gout_bwd(o, z, g_o, gn):
        _, pull = jax.vjp(_gdn_out, o, z, gn)
        return pull(g_o)

    do_gdn, dz, gr["gn_row"] = _rowwise(gout_bwd, [(sv["o_gdn"], hw, 0), (sv["h"], hw, 3), (dom, hw, 0)], [wl["gn_row"]],
                                        [(hw, F32), (hw, BF16)], [(1, HEAD_DIM)], name=nm("b_gdn_out"))
    dprep = _gdn_seq_bwd(sv["prep"], sv["s_all"], do_gdn, name=nm("b_gdn_seq"))
    dqn, dkn, dvg, dgb = _gdn_prep_bwd(sv["qn"], sv["kn"], sv["vg"], sv["gb"], dprep, name=nm("b_gdn_prep"))

    def act_bwd(u, misc, g_q, g_k, g_v, g_gb, g_rope, alog, dtb):
        _, pull = jax.vjp(_gdn_act, u, misc, alog, dtb)
        du_, dmisc_, dalog_, ddtb_ = pull((g_q, g_k, g_v, g_gb))
        return du_, dmisc_ + g_rope, dalog_, ddtb_

    du, dmisc, gr["alog_row"], gr["dtb_row"] = _rowwise(
        act_bwd, [(sv["u"], 3 * hw, 0), (sv["h"], LANES, misc_cb), (dqn, hw, 0), (dkn, hw, 0), (dvg, hw, 0),
                  (dgb, LANES, 0), (dmisc_rope, LANES, 0)], [wl["alog_row"], wl["dtb_row"]],
        [(3 * hw, F32), (LANES, BF16)], [(1, LANES), (1, LANES)], name=nm("b_gdn_act"))
    dqkv, dconv = _conv_bwd(du, sv["h"], wl["conv_w"], 3 * hw, name=nm("b_conv"))
    gr["conv_w"] = dconv[:CONV_W]

    dh = jnp.concatenate([dqkv, dz, dckv, dcq, dmisc], axis=1)
    wgrad("w_in", sv["xb"], dh)
    dx = mm(dh, wl["w_in"], dims="nt", c=dx_a, name=nm("b_x"), tk=1408)
    return dx, gbuf, gr


LOCAL_MATRICES = ("w_in", "w_uq", "w_ukv", "w_out", "w_gate_up", "w_down", "w_ple", "w_ple_gate")


def _layer_weights(mats, vecs, layer):
    wl = {n: mats[n] for n in LOCAL_MATRICES}
    wl["layer"] = layer
    wl["conv_w"] = mats["conv_w"][layer]
    wl["alog_row"] = _lane_row(vecs["a_log"][layer], MISC_A0)
    wl["dtb_row"] = _lane_row(vecs["dt_bias"][layer], MISC_A0)
    wl["gn_row"] = vecs["gdn_norm_g"][layer][None, :]
    wl["qg_row"] = vecs["q_norm_g"][layer][None, :]
    wl["kvg_row"] = vecs["kv_norm_g"][layer][None, :]
    for n in ("ln1_g", "ln1_b", "ln2_g", "ln2_b"):
        wl[n] = vecs[n][layer][None, :]
    return wl


def _vector_grads(gr):
    out = {"a_log": gr["alog_row"][0, MISC_A0:MISC_A0 + N_HEADS], "dt_bias": gr["dtb_row"][0, MISC_A0:MISC_A0 + N_HEADS],
           "gdn_norm_g": gr["gn_row"][0], "q_norm_g": gr["qg_row"][0], "kv_norm_g": gr["kvg_row"][0]}
    for n in ("ln1_g", "ln1_b", "ln2_g", "ln2_b"):
        out[n] = gr[n][0]
    return out


def _local_step(dm, x, p, positions, target, mats, vecs):
    depth = p.shape[0]
    alpha = (2.0 * depth) ** 0.25
    freq = ROPE_THETA ** (-jnp.arange(0, ROPE_DIM, 2, dtype=F32) / ROPE_DIM)
    inv_freq_row = _lane_row(jnp.concatenate([freq, freq]), 0)
    cos_t, sin_t = _rope_tables(positions.reshape(-1, 1), inv_freq_row, name="rope_tables")

    wls = [_layer_weights(mats, vecs, i) for i in range(depth)]
    saved = []
    cur, cur_b = x, x
    for i in range(depth):
        cur, cur_b, sv = _layer_fwd(dm, alpha, cur, cur_b, p[i], cos_t, sin_t, wls[i], f"l{i}")
        saved.append(sv)
    dy, loss_blk = _loss_head(cur, target)
    gbuf = {n: jnp.zeros(mats[n].shape, F32) for n in LOCAL_MATRICES}
    conv_g, vec_g = [None] * depth, [None] * depth
    for i in reversed(range(depth)):
        dy, gbuf, gr = _layer_bwd(dm, alpha, dy, saved[i], cos_t, sin_t, wls[i], gbuf, f"l{i}")
        conv_g[i] = gr["conv_w"]
        vec_g[i] = _vector_grads(gr)
    vec_grads = {n: jnp.stack([vec_g[i][n] for i in range(depth)]) for n in VECTORS}
    return loss_blk[0, 0], dy, gbuf, jnp.stack(conv_g), vec_grads


def kernel(x, p, positions, w_in, conv_w, a_log, dt_bias, gdn_norm_g, q_norm_g, w_uq, kv_norm_g, w_ukv, w_out, ln1_g, ln1_b, w_gate_up, w_down, ln2_g, ln2_b, w_ple, w_ple_gate, loss_target, m_w_in, m_conv_w, m_a_log, m_dt_bias, m_gdn_norm_g, m_q_norm_g, m_w_uq, m_kv_norm_g, m_w_ukv, m_w_out, m_ln1_g, m_ln1_b, m_w_gate_up, m_w_down, m_ln2_g, m_ln2_b, m_w_ple, m_w_ple_gate, v_w_in, v_conv_w, v_a_log, v_dt_bias, v_gdn_norm_g, v_q_norm_g, v_w_uq, v_kv_norm_g, v_w_ukv, v_w_out, v_ln1_g, v_ln1_b, v_w_gate_up, v_w_down, v_ln2_g, v_ln2_b, v_w_ple, v_w_ple_gate):
    w = dict(w_in=w_in, conv_w=conv_w, a_log=a_log, dt_bias=dt_bias, gdn_norm_g=gdn_norm_g, q_norm_g=q_norm_g, w_uq=w_uq,
             kv_norm_g=kv_norm_g, w_ukv=w_ukv, w_out=w_out, ln1_g=ln1_g, ln1_b=ln1_b, w_gate_up=w_gate_up, w_down=w_down,
             ln2_g=ln2_g, ln2_b=ln2_b, w_ple=w_ple, w_ple_gate=w_ple_gate)
    m = dict(w_in=m_w_in, conv_w=m_conv_w, a_log=m_a_log, dt_bias=m_dt_bias, gdn_norm_g=m_gdn_norm_g, q_norm_g=m_q_norm_g,
             w_uq=m_w_uq, kv_norm_g=m_kv_norm_g, w_ukv=m_w_ukv, w_out=m_w_out, ln1_g=m_ln1_g, ln1_b=m_ln1_b,
             w_gate_up=m_w_gate_up, w_down=m_w_down, ln2_g=m_ln2_g, ln2_b=m_ln2_b, w_ple=m_w_ple, w_ple_gate=m_w_ple_gate)
    v = dict(w_in=v_w_in, conv_w=v_conv_w, a_log=v_a_log, dt_bias=v_dt_bias, gdn_norm_g=v_gdn_norm_g, q_norm_g=v_q_norm_g,
             w_uq=v_w_uq, kv_norm_g=v_kv_norm_g, w_ukv=v_w_ukv, w_out=v_w_out, ln1_g=v_ln1_g, ln1_b=v_ln1_b,
             w_gate_up=v_w_gate_up, w_down=v_w_down, ln2_g=v_ln2_g, ln2_b=v_ln2_b, w_ple=v_w_ple, w_ple_gate=v_w_ple_gate)
    depth = w_in.shape[0]
    assert depth % 2 == 0
    hd = depth // 2
    dm = _Dims(x.shape[2], N_CHIPS * w_in.shape[2], w_uq.shape[1], w_ukv.shape[1], N_CHIPS * w_gate_up.shape[2], p.shape[3])
    cx, cy, cc = lax.axis_index("x"), lax.axis_index("y"), lax.axis_index("c")
    chip = 2 * cx + cy

    g_streams = [_stream_of(n, w[n].shape[1:]) for n in MATRICES]
    shards = [w[n] if n == "conv_w" else w[n].astype(BF16) for n in MATRICES]
    shards = [_pad_lanes(s) if st.kind == "piece" else s for s, st in zip(shards, g_streams)]
    g_shapes = []
    for s, st in zip(shards, g_streams):
        if st.kind == "piece":
            shape = (N_CHIPS,) + s.shape
        elif st.kind == "rows":
            shape = (depth, N_CHIPS * s.shape[1], s.shape[2])
        else:
            shape = (depth, s.shape[1], N_CHIPS * s.shape[2])
        g_shapes.append(jax.ShapeDtypeStruct(shape, s.dtype))
    gathered = _gather_chips(shards, g_streams, g_shapes, name="gather_weights")
    mats = {n: _place_own_block(g, s, st, chip) for n, g, s, st in zip(MATRICES, gathered, shards, g_streams)}
    for n, to_local in (("w_in", dm.w_in_local), ("w_uq", dm.w_uq_local)):
        pieces = jnp.moveaxis(mats[n][..., :w[n].shape[2]], 0, 2)
        mats[n] = to_local(pieces.reshape(pieces.shape[:2] + (-1,)))
    vecs = {n: w[n] for n in VECTORS}

    loss_local, grad_x, gbuf, conv_g, vec_g = _local_step(dm, x[0], p[:, 0], positions[0], loss_target[0], mats, vecs)
    loss = lax.psum(loss_local, ("x", "y", "c"))

    names = list(LOCAL_MATRICES) + ["conv_w", "vectors"]
    gs = [gbuf[n] for n in LOCAL_MATRICES] + [conv_g, _pack_vectors(vec_g, depth)]
    wire = [BF16] * len(LOCAL_MATRICES) + [F32, F32]
    r_streams = [_stream_of(n, w[n].shape[1:]) for n in LOCAL_MATRICES]
    r_streams += [_stream_of("conv_w", w["conv_w"].shape[1:]), _Stream("whole")]
    shard_shapes = [(hd, w[n].shape[1], _lane_padded(w[n].shape[2])) if st.kind == "piece" else (hd,) + w[n].shape[1:]
                    for n, st in zip(LOCAL_MATRICES, r_streams)]
    shard_shapes += [(hd,) + w["conv_w"].shape[1:], (hd, VEC_ROWS, LANES)]
    c_idx = cc.reshape(1).astype(jnp.int32)
    place = (chip.reshape(1).astype(jnp.int32), c_idx)
    from_sibling = _sibling_take_other_half(gs, name="reduce_sibling")
    chip_sum = [_add_own_half(g, a, c_idx, dt, name=f"reduce_add_{n}")
                for g, a, dt, n in zip(gs, from_sibling, wire, names)]
    for i, n in enumerate(names):
        if r_streams[i].kind == "piece":
            glob = dm.w_in_global(chip_sum[i]) if n == "w_in" else dm.w_uq_global(chip_sum[i])
            glob = glob.reshape(glob.shape[:2] + (N_CHIPS, glob.shape[2] // N_CHIPS))
            chip_sum[i] = jnp.moveaxis(_pad_lanes(glob), 2, 0)
    from_chips = _chips_exchange(chip_sum, r_streams, shard_shapes, name="reduce_chips")
    halves = [_sum_chips(ps, got, place, st, name=f"reduce_sum_{n}")
              for ps, got, st, n in zip(chip_sum, from_chips, r_streams, names)]
    joined = dict(zip(names, _sibling_join_halves(halves, name="reduce_join")))
    joined.update(_unpack_vectors(joined.pop("vectors"), {n: w[n].shape for n in VECTORS}))

    grad_w, delta_w, new_m, new_v = {}, {}, {}, {}
    for n in WEIGHTS:
        grad_w[n] = joined[n][..., :w[n].shape[-1]]
        delta_w[n], new_m[n], new_v[n] = _adamw(w[n], grad_w[n], m[n], v[n], name=f"adamw_{n}")
    return (loss, grad_x[None], *[grad_w[n] for n in WEIGHTS], *[delta_w[n] for n in WEIGHTS],
            *[new_m[n] for n in WEIGHTS], *[new_v[n] for n in WEIGHTS])
```

```python
import functools

import jax
import jax.numpy as jnp
from jax import lax
from jax.experimental import pallas as pl
from jax.experimental.pallas import tpu as pltpu

F32 = jnp.float32
BF16 = jnp.bfloat16
HIGH = lax.Precision.HIGH
MESH = pl.DeviceIdType.MESH

CHUNK = 64
N_HEADS = 4
HEAD_DIM = 128
ROPE_DIM = 64
ROPE_THETA = 10000.0
LN_EPS = 1e-5
RMS_EPS = 1e-6
ADAM_LR, ADAM_B1, ADAM_B2, ADAM_EPS, ADAM_WD, ADAM_STEP = 0.001, 0.9, 0.999, 1e-08, 0.01, 10

LANES = 128
VMEM_LIMIT = 48 * 1024 * 1024
ROW_TILE = 256
SUB_ROWS = 16
MAX_SUB_ROWS = 64
VREG_FILE_ELEMS = 64 * 8 * LANES

MISC_BETA0 = ROPE_DIM
MISC_A0 = ROPE_DIM + N_HEADS

NN = (((1,), (0,)), ((), ()))
NT = (((1,), (1,)), ((), ()))
TN = (((0,), (0,)), ((), ()))


def _params(sem=None):
    return pltpu.CompilerParams(dimension_semantics=sem, vmem_limit_bytes=VMEM_LIMIT)


def _divisor_tile(dim, target, unit):
    best = None
    t = unit
    while t <= min(dim, target):
        if dim % t == 0:
            best = t
        t += unit
    return best if best is not None else dim


BATCHED = {NN: (((2,), (1,)), ((0,), (0,))), NT: (((2,), (2,)), ((0,), (0,))), TN: (((1,), (1,)), ((0,), (0,)))}


def _make_dots(high_precision):
    def raw(a, b, dims):
        if a.ndim == 3:
            dims = BATCHED[dims]
        if high_precision:
            return lax.dot_general(a, b, dims, precision=HIGH, preferred_element_type=F32)
        return lax.dot_general(a.astype(BF16), b.astype(BF16), dims, preferred_element_type=F32)

    @jax.custom_vjp
    def nn(a, b):
        return raw(a, b, NN)

    @jax.custom_vjp
    def nt(a, b):
        return raw(a, b, NT)

    @jax.custom_vjp
    def tn(a, b):
        return raw(a, b, TN)

    nn.defvjp(lambda a, b: (raw(a, b, NN), (a, b)), lambda r, g: (nt(g, r[1]), tn(r[0], g)))
    nt.defvjp(lambda a, b: (raw(a, b, NT), (a, b)), lambda r, g: (nn(g, r[1]), tn(g, r[0])))
    tn.defvjp(lambda a, b: (raw(a, b, TN), (a, b)), lambda r, g: (nt(r[1], g), nn(r[0], g)))
    return nn, nt, tn


_nn, _nt, _tn = _make_dots(False)
_hnn, _hnt, _htn = _make_dots(True)


def _matmul(a, b, *, dims, name, c=None, out_dtype=F32, tm=1024, tn=1408, tk=1408, layer=None, into=None):
    b_shape = b.shape[-2:]
    if dims == "nn":
        (m, k), (k2, n) = a.shape, b_shape
    elif dims == "nt":
        (m, k), (n, k2) = a.shape, b_shape
    else:
        (k, m), (k2, n) = a.shape, b_shape
    assert k == k2, (a.shape, b.shape, dims)
    tm = _divisor_tile(m, tm, LANES)
    tn = _divisor_tile(n, tn, LANES)
    tk = _divisor_tile(k, tk, LANES)
    nk = k // tk
    dn = {"nn": NN, "nt": NT, "tn": TN}[dims]
    if dims == "tn":
        a_spec = pl.BlockSpec((tk, tm), lambda i, j, kk: (kk, i))
    else:
        a_spec = pl.BlockSpec((tm, tk), lambda i, j, kk: (i, kk))
    b_blk, b_idx = ((tn, tk), lambda i, j, kk: (j, kk)) if dims == "nt" else ((tk, tn), lambda i, j, kk: (kk, j))
    if b.ndim == 3:
        b_spec = pl.BlockSpec((None,) + b_blk, lambda i, j, kk: (layer,) + b_idx(i, j, kk))
    else:
        b_spec = pl.BlockSpec(b_blk, b_idx)
    c_spec = pl.BlockSpec((tm, tn), lambda i, j, kk: (i, j))
    if into is not None:
        assert into.shape[1:] == (m, n) and into.dtype == out_dtype
        o_spec = pl.BlockSpec((None, tm, tn), lambda i, j, kk: (layer, i, j))
        out_shape = jax.ShapeDtypeStruct(into.shape, into.dtype)
    else:
        o_spec = c_spec
        out_shape = jax.ShapeDtypeStruct((m, n), out_dtype)
    has_c = c is not None

    def body(*refs):
        a_ref, b_ref = refs[:2]
        c_ref = refs[2] if has_c else None
        o_ref, acc_ref = refs[-2:]
        kk = pl.program_id(2)

        @pl.when(kk == 0)
        def _():
            if has_c:
                acc_ref[...] = c_ref[...].astype(F32)
            else:
                acc_ref[...] = jnp.zeros_like(acc_ref)

        acc_ref[...] += lax.dot_general(a_ref[...].astype(BF16), b_ref[...].astype(BF16), dn,
                                        preferred_element_type=F32)

        @pl.when(kk == nk - 1)
        def _():
            o_ref[...] = acc_ref[...].astype(o_ref.dtype)

    ins = [a, b] + ([c] if has_c else [])
    specs = [a_spec, b_spec] + ([c_spec] if has_c else [])
    aliases = {}
    if into is not None:
        aliases = {len(ins): 0}
        ins.append(into)
        specs.append(pl.BlockSpec(memory_space=pl.ANY))
    return pl.pallas_call(
        body, name=name, grid=(m // tm, n // tn, nk), in_specs=specs, out_specs=o_spec, out_shape=out_shape,
        scratch_shapes=[pltpu.VMEM((tm, tn), F32)], input_output_aliases=aliases,
        compiler_params=_params(("arbitrary", "arbitrary", "arbitrary")),
    )(*ins)


def _rowwise(fn, rows, params, outs, accs=(), *, name, tm=ROW_TILE):
    t = rows[0][0].shape[0]
    tm = min(tm, t)
    widest = max([w for _, w, _ in rows] + [w for w, _ in outs])
    sub = SUB_ROWS
    while sub < MAX_SUB_ROWS and 2 * sub * widest <= VREG_FILE_ELEMS:
        sub *= 2
    assert t % tm == 0 and tm % sub == 0
    n_rows, n_par, n_out, n_acc = len(rows), len(params), len(outs), len(accs)

    def body(*refs):
        row_refs = refs[:n_rows]
        par_refs = refs[n_rows:n_rows + n_par]
        out_refs = refs[n_rows + n_par:n_rows + n_par + n_out]
        acc_refs = refs[n_rows + n_par + n_out:]
        if n_acc:
            @pl.when(pl.program_id(0) == 0)
            def _():
                for a_ref in acc_refs:
                    a_ref[...] = jnp.zeros_like(a_ref)

        def step(r, carry):
            sl = pl.ds(pl.multiple_of(r * sub, sub), sub)
            vals = [ref[sl, :].astype(F32) for ref in row_refs] + [ref[...] for ref in par_refs]
            res = fn(*vals)
            for o_ref, val in zip(out_refs, res[:n_out]):
                o_ref[sl, :] = val.astype(o_ref.dtype)
            for a_ref, val in zip(acc_refs, res[n_out:]):
                a_ref[...] += val
            return carry

        lax.fori_loop(0, tm // sub, step, 0)

    in_specs = [pl.BlockSpec((tm, w), functools.partial(lambda i, cb: (i, cb), cb=cb)) for _, w, cb in rows]
    in_specs += [pl.BlockSpec(p.shape, lambda i: (0, 0)) for p in params]
    out_specs = [pl.BlockSpec((tm, w), lambda i: (i, 0)) for w, _ in outs]
    out_specs += [pl.BlockSpec(s, lambda i: (0, 0)) for s in accs]
    out_shape = [jax.ShapeDtypeStruct((t, w), d) for w, d in outs]
    out_shape += [jax.ShapeDtypeStruct(s, F32) for s in accs]
    return pl.pallas_call(
        body, name=name, grid=(t // tm,), in_specs=in_specs, out_specs=out_specs, out_shape=out_shape,
        compiler_params=_params(("arbitrary",)),
    )(*[r[0] for r in rows], *params)


def _vjp_fn(fn, n_in, n_out):
    def bwd(*args):
        ins, cts = args[:n_in], args[n_in:]
        _, pull = jax.vjp(fn, *ins)
        return pull(tuple(cts) if n_out > 1 else cts[0])
    return bwd


def _lane(shape):
    return lax.broadcasted_iota(jnp.int32, shape, 1)


def _silu(x):
    return x * jax.nn.sigmoid(x)


def _softplus(x):
    return jnp.maximum(x, 0.0) + jnp.log1p(jnp.exp(-jnp.abs(x)))


def _heads(x, width=HEAD_DIM):
    return [x[:, h * width:(h + 1) * width] for h in range(N_HEADS)]


def _layer_norm(z, g, b):
    mu = jnp.mean(z, -1, keepdims=True)
    zc = z - mu
    var = jnp.mean(zc * zc, -1, keepdims=True)
    return zc * lax.rsqrt(var + LN_EPS) * g + b


def _gdn_act(u, misc, alog_row, dtb_row):
    s = _silu(u)
    w = N_HEADS * HEAD_DIM
    q = jnp.concatenate([t * lax.rsqrt(jnp.sum(t * t, -1, keepdims=True) + RMS_EPS) * HEAD_DIM ** -0.5
                         for t in _heads(s[:, :w])], axis=1)
    k = jnp.concatenate([t * lax.rsqrt(jnp.sum(t * t, -1, keepdims=True) + RMS_EPS)
                         for t in _heads(s[:, w:2 * w])], axis=1)
    v = s[:, 2 * w:]
    lane = _lane(misc.shape)
    beta = jax.nn.sigmoid(misc)
    g = -jnp.exp(alog_row) * _softplus(misc + dtb_row)
    is_beta = (lane >= MISC_BETA0) & (lane < MISC_BETA0 + N_HEADS)
    is_g = (lane >= MISC_A0) & (lane < MISC_A0 + N_HEADS)
    gb = jnp.where(is_beta, beta, jnp.where(is_g, g, 0.0))
    return q, k, v, gb


def _gdn_out(o, z, gn_row):
    outs = []
    for oh, zh in zip(_heads(o), _heads(z)):
        r = oh * lax.rsqrt(jnp.mean(oh * oh, -1, keepdims=True) + RMS_EPS) * gn_row
        outs.append(r * _silu(zh))
    return jnp.concatenate(outs, axis=1)


def _mla_norm(ckv, cq, kvg_row, qg_row):
    cqn = cq * lax.rsqrt(jnp.mean(cq * cq, -1, keepdims=True) + RMS_EPS) * qg_row
    ckvn = ckv * lax.rsqrt(jnp.mean(ckv * ckv, -1, keepdims=True) + RMS_EPS) * kvg_row
    return cqn, ckvn


def _swap_halves(x):
    half = ROPE_DIM // 2
    return jnp.where(_lane(x.shape) < half, pltpu.roll(x, LANES - half, 1), pltpu.roll(x, half, 1))


@jax.custom_vjp
def _rope(x, cos_t, sin_t):
    return x * cos_t + _swap_halves(x) * sin_t


def _rope_fwd(x, cos_t, sin_t):
    return _rope(x, cos_t, sin_t), (cos_t, sin_t)


def _rope_bwd(res, g):
    cos_t, sin_t = res
    return g * cos_t - _swap_halves(g) * sin_t, jnp.zeros_like(cos_t), jnp.zeros_like(sin_t)


_rope.defvjp(_rope_fwd, _rope_bwd)


def _mla_qk(scale, qm, kv, misc, cos_t, sin_t):
    krope = _rope(misc, cos_t, sin_t)
    qs, ks = [], []
    for h in range(N_HEADS):
        base = 2 * HEAD_DIM * h
        qs += [qm[:, base:base + HEAD_DIM], _rope(qm[:, base + HEAD_DIM:base + 2 * HEAD_DIM], cos_t, sin_t)]
        ks += [kv[:, HEAD_DIM * h:HEAD_DIM * (h + 1)], krope]
    return jnp.concatenate(qs, axis=1) * scale, jnp.concatenate(ks, axis=1), kv[:, N_HEADS * HEAD_DIM:]


def _swiglu(gu):
    f = gu.shape[1] // 2
    return _silu(gu[:, :f]) * gu[:, f:]


def _ple_out(x2, pg, pe):
    return x2 + jax.nn.sigmoid(pg) * pe


CONV_W = 4
HALO = 8
CONV_STRIP = 512


def _conv_fwd(h, conv_w, width, *, name, tm=ROW_TILE, sub=32):
    t = h.shape[0]
    tm = min(tm, t)
    nb = tm // HALO

    def body(x_ref, halo_ref, w_ref, u_ref, buf):
        i = pl.program_id(0)
        buf[pl.ds(0, HALO), :] = jnp.where(i > 0, halo_ref[...], 0.0)
        buf[pl.ds(HALO, tm), :] = x_ref[...]
        for c0 in range(0, width, CONV_STRIP):
            cols = pl.ds(c0, CONV_STRIP)
            w = w_ref[:, cols]
            for r0 in range(0, tm, sub):
                acc = jnp.zeros((sub, CONV_STRIP), F32)
                for j in range(CONV_W):
                    acc = acc + w[j:j + 1, :] * buf[pl.ds(HALO + r0 - (CONV_W - 1) + j, sub), cols]
                u_ref[pl.ds(r0, sub), cols] = acc

    return pl.pallas_call(
        body, name=name, grid=(t // tm,),
        in_specs=[pl.BlockSpec((tm, width), lambda i: (i, 0)),
                  pl.BlockSpec((HALO, width), lambda i: (jnp.maximum(i * nb - 1, 0), 0)),
                  pl.BlockSpec(conv_w.shape, lambda i: (0, 0))],
        out_specs=pl.BlockSpec((tm, width), lambda i: (i, 0)),
        out_shape=jax.ShapeDtypeStruct((t, width), F32),
        scratch_shapes=[pltpu.VMEM((tm + HALO, width), F32)],
        compiler_params=_params(("arbitrary",)),
    )(h, h, conv_w)


def _conv_bwd(du, h, conv_w, width, *, name, tm=ROW_TILE, sub=32):
    t = h.shape[0]
    tm = min(tm, t)
    nb = tm // HALO
    n_tiles = t // tm

    def body(du_ref, du_halo, x_ref, x_halo, w_ref, dx_ref, dw_ref, dbuf, xbuf):
        i = pl.program_id(0)

        @pl.when(i == 0)
        def _():
            dw_ref[...] = jnp.zeros_like(dw_ref)

        dbuf[pl.ds(0, tm), :] = du_ref[...]
        dbuf[pl.ds(tm, HALO), :] = jnp.where(i < n_tiles - 1, du_halo[...], 0.0)
        xbuf[pl.ds(0, HALO), :] = jnp.where(i > 0, x_halo[...], 0.0)
        xbuf[pl.ds(HALO, tm), :] = x_ref[...]
        for c0 in range(0, width, CONV_STRIP):
            cols = pl.ds(c0, CONV_STRIP)
            w = w_ref[:, cols]
            dws = [jnp.zeros((HALO, CONV_STRIP), F32) for _ in range(CONV_W)]
            for r0 in range(0, tm, sub):
                acc = jnp.zeros((sub, CONV_STRIP), F32)
                d_here = dbuf[pl.ds(r0, sub), cols]
                for j in range(CONV_W):
                    acc = acc + w[j:j + 1, :] * dbuf[pl.ds(r0 + (CONV_W - 1) - j, sub), cols]
                    prod = d_here * xbuf[pl.ds(HALO + r0 - (CONV_W - 1) + j, sub), cols]
                    for g0 in range(0, sub, HALO):
                        dws[j] = dws[j] + prod[g0:g0 + HALO, :]
                dx_ref[pl.ds(r0, sub), cols] = acc.astype(dx_ref.dtype)
            for j in range(CONV_W):
                dw_ref[pl.ds(j, 1), cols] += jnp.sum(dws[j], axis=0, keepdims=True)

    return pl.pallas_call(
        body, name=name, grid=(n_tiles,),
        in_specs=[pl.BlockSpec((tm, width), lambda i: (i, 0)),
                  pl.BlockSpec((HALO, width), lambda i: (jnp.minimum((i + 1) * nb, t // HALO - 1), 0)),
                  pl.BlockSpec((tm, width), lambda i: (i, 0)),
                  pl.BlockSpec((HALO, width), lambda i: (jnp.maximum(i * nb - 1, 0), 0)),
                  pl.BlockSpec(conv_w.shape, lambda i: (0, 0))],
        out_specs=[pl.BlockSpec((tm, width), lambda i: (i, 0)),
                   pl.BlockSpec((HALO, width), lambda i: (0, 0))],
        out_shape=[jax.ShapeDtypeStruct((t, width), BF16), jax.ShapeDtypeStruct((HALO, width), F32)],
        scratch_shapes=[pltpu.VMEM((tm + HALO, width), F32), pltpu.VMEM((tm + HALO, width), F32)],
        compiler_params=_params(("arbitrary",)),
    )(du, du, h, h, conv_w)


@jax.custom_vjp
def _inv_unit_lower(low):
    n = low.shape[-1]
    eye = (lax.broadcasted_iota(jnp.int32, (n, n), 0) == lax.broadcasted_iota(jnp.int32, (n, n), 1)).astype(F32)
    x = eye - low
    p = low
    span = 2
    while span < n:
        p = _hnn(p, p)
        x = x + _hnn(x, p)
        span *= 2
    return x


def _inv_fwd(low):
    x = _inv_unit_lower(low)
    return x, x


def _inv_bwd(x, g):
    return (-_htn(x, _hnt(g, x)),)


_inv_unit_lower.defvjp(_inv_fwd, _inv_bwd)


@jax.custom_vjp
def _inv_known(low, inverse):
    return inverse


_inv_known.defvjp(lambda low, inverse: (inverse, inverse), lambda x, g: (_inv_bwd(x, g)[0], jnp.zeros_like(x)))


def _gdn_prep(q, k, v, gb, known_inverse=None):
    c = CHUNK
    n = q.shape[0] // c
    pairs = [(g, h) for g in range(n) for h in range(N_HEADS)]
    row = lax.broadcasted_iota(jnp.int32, (c, c), 0)
    col = lax.broadcasted_iota(jnp.int32, (c, c), 1)
    tri_incl = row >= col
    tri_strict = row > col
    lane = _lane((c, LANES))
    sub = lax.broadcasted_iota(jnp.int32, (LANES, c), 0)
    last = lax.broadcasted_iota(jnp.int32, (c, 1), 0) == c - 1

    def split(x):
        return jnp.stack([x[g * c:(g + 1) * c, h * HEAD_DIM:(h + 1) * HEAD_DIM] for g, h in pairs])

    gbs = [gb[g * c:(g + 1) * c, :] for g in range(n)]
    gbts = [x.T for x in gbs]
    g_col = jnp.stack([jnp.sum(jnp.where(lane == MISC_A0 + h, gbs[g], 0.0), axis=1, keepdims=True) for g, h in pairs])
    b_col = jnp.stack([jnp.sum(jnp.where(lane == MISC_BETA0 + h, gbs[g], 0.0), axis=1, keepdims=True) for g, h in pairs])
    g_row = jnp.stack([jnp.sum(jnp.where(sub == MISC_A0 + h, gbts[g], 0.0), axis=0, keepdims=True) for g, h in pairs])
    gc_col = jnp.sum(jnp.where(tri_incl, g_row, 0.0), axis=2, keepdims=True)
    gc_row = jnp.sum(jnp.where(row <= col, g_col, 0.0), axis=1, keepdims=True)
    decay = jnp.where(tri_incl, jnp.exp(jnp.where(tri_incl, gc_col - gc_row, 0.0)), 0.0)
    g_last = jnp.sum(jnp.where(last, gc_col, 0.0), axis=1, keepdims=True)
    qs, ks, vs = split(q), split(k), split(v)
    kb = ks * b_col
    low = jnp.where(tri_strict, _nt(kb, ks) * decay, 0.0)
    if known_inverse is None:
        tinv = _inv_unit_lower(low)
    else:
        tinv = _inv_known(low, jnp.stack([known_inverse[g * c:(g + 1) * c, h * c:(h + 1) * c] for g, h in pairs]))
    eg = jnp.exp(gc_col)
    sol = _hnn(tinv, jnp.concatenate([vs * b_col, kb * eg], axis=2))
    attn = jnp.where(tri_incl, _nt(qs, ks) * decay, 0.0)
    qd = qs * eg
    kd = ks * jnp.exp(g_last - gc_col)

    def merge(x):
        return jnp.concatenate([jnp.concatenate([x[g * N_HEADS + h] for h in range(N_HEADS)], axis=1)
                                for g in range(n)], axis=0)

    glb = jnp.concatenate([sum(jnp.where(lane == h, g_last[g * N_HEADS + h], 0.0) for h in range(N_HEADS))
                           for g in range(n)], axis=0)
    outs = (merge(sol[:, :, :HEAD_DIM]), merge(sol[:, :, HEAD_DIM:]), merge(qd), merge(kd), merge(attn), glb)
    return outs, merge(tinv)


def _gdn_seq(state, u, w, qd, kd, attn, glb):
    c = u.shape[0]
    first = lax.broadcasted_iota(jnp.int32, glb.shape, 0) == 0
    lane = _lane(glb.shape)
    heads = lambda x: jnp.stack([x[:, h * HEAD_DIM:(h + 1) * HEAD_DIM] for h in range(N_HEADS)])
    g_last = jnp.stack([jnp.sum(jnp.sum(jnp.where(first & (lane == h), glb, 0.0), axis=1, keepdims=True),
                                axis=0, keepdims=True) for h in range(N_HEADS)])
    s = jnp.stack([state[h * HEAD_DIM:(h + 1) * HEAD_DIM, :] for h in range(N_HEADS)])
    at = jnp.stack([attn[:, h * c:(h + 1) * c] for h in range(N_HEADS)])
    v_new = heads(u) - _nn(heads(w), s)
    o = _nn(heads(qd), s) + _nn(at, v_new)
    s_new = s * jnp.exp(g_last) + _tn(heads(kd), v_new)
    return (jnp.concatenate([o[h] for h in range(N_HEADS)], axis=1),
            jnp.concatenate([s_new[h] for h in range(N_HEADS)], axis=0))


PREP_CHUNKS = 2
SEQ_CHUNKS = 8


def _gdn_prep_fwd(q, k, v, gb, *, name):
    t, w = q.shape
    rows = min(PREP_CHUNKS * CHUNK, t)

    def body(q_ref, k_ref, v_ref, gb_ref, *out_refs):
        outs, inverse = _gdn_prep(q_ref[...], k_ref[...], v_ref[...], gb_ref[...])
        for o_ref, val in zip(out_refs, outs + (inverse,)):
            o_ref[...] = val

    spec = lambda width: pl.BlockSpec((rows, width), lambda i: (i, 0))
    widths = [w, w, w, w, N_HEADS * CHUNK, LANES, N_HEADS * CHUNK]
    res = pl.pallas_call(
        body, name=name, grid=(t // rows,),
        in_specs=[spec(w), spec(w), spec(w), spec(LANES)],
        out_specs=[spec(x) for x in widths],
        out_shape=[jax.ShapeDtypeStruct((t, x), F32) for x in widths],
        compiler_params=_params(("arbitrary",)),
    )(q, k, v, gb)
    return tuple(res[:6]), res[6]


def _gdn_prep_bwd(q, k, v, gb, inverse, cts, *, name):
    t, w = q.shape
    rows = min(PREP_CHUNKS * CHUNK, t)

    def body(q_ref, k_ref, v_ref, gb_ref, inv_ref, du, dw, dqd, dkd, dattn, dglb, dq_ref, dk_ref, dv_ref, dgb_ref):
        known = inv_ref[...]
        _, pull = jax.vjp(lambda a, b, c_, d_: _gdn_prep(a, b, c_, d_, known)[0],
                          q_ref[...], k_ref[...], v_ref[...], gb_ref[...])
        dq, dk, dv, dgb = pull(tuple(r[...] for r in (du, dw, dqd, dkd, dattn, dglb)))
        dq_ref[...] = dq
        dk_ref[...] = dk
        dv_ref[...] = dv
        dgb_ref[...] = dgb

    spec = lambda width: pl.BlockSpec((rows, width), lambda i: (i, 0))
    widths = [w, w, w, w, N_HEADS * CHUNK, LANES]
    return pl.pallas_call(
        body, name=name, grid=(t // rows,),
        in_specs=[spec(w), spec(w), spec(w), spec(LANES), spec(N_HEADS * CHUNK)] + [spec(x) for x in widths],
        out_specs=[spec(w), spec(w), spec(w), spec(LANES)],
        out_shape=[jax.ShapeDtypeStruct((t, w), F32)] * 3 + [jax.ShapeDtypeStruct((t, LANES), F32)],
        compiler_params=_params(("arbitrary",)),
    )(q, k, v, gb, inverse, *cts)


def _gdn_seq_fwd(prep, *, name):
    t, w = prep[0].shape
    rows = min(SEQ_CHUNKS * CHUNK, t)
    per = rows // CHUNK

    def body(u_ref, w_ref, qd_ref, kd_ref, at_ref, gl_ref, o_ref, sall_ref, s_scr):
        @pl.when(pl.program_id(0) == 0)
        def _():
            s_scr[...] = jnp.zeros_like(s_scr)

        def step(j, carry):
            sl = pl.ds(pl.multiple_of(j * CHUNK, CHUNK), CHUNK)
            s = s_scr[...]
            sall_ref[j] = s
            o, s_new = _gdn_seq(s, u_ref[sl, :], w_ref[sl, :], qd_ref[sl, :], kd_ref[sl, :], at_ref[sl, :], gl_ref[sl, :])
            o_ref[sl, :] = o
            s_scr[...] = s_new
            return carry

        lax.fori_loop(0, per, step, 0)

    spec = lambda width: pl.BlockSpec((rows, width), lambda i: (i, 0))
    widths = [w, w, w, w, N_HEADS * CHUNK, LANES]
    return pl.pallas_call(
        body, name=name, grid=(t // rows,),
        in_specs=[spec(x) for x in widths],
        out_specs=[spec(w), pl.BlockSpec((per, w, HEAD_DIM), lambda i: (i, 0, 0))],
        out_shape=[jax.ShapeDtypeStruct((t, w), F32), jax.ShapeDtypeStruct((t // CHUNK, w, HEAD_DIM), F32)],
        scratch_shapes=[pltpu.VMEM((w, HEAD_DIM), F32)],
        compiler_params=_params(("arbitrary",)),
    )(*prep)


def _gdn_seq_bwd(prep, s_all, do, *, name):
    t, w = prep[0].shape
    rows = min(SEQ_CHUNKS * CHUNK, t)
    per = rows // CHUNK
    n = t // rows

    def body(u_ref, w_ref, qd_ref, kd_ref, at_ref, gl_ref, sall_ref, do_ref, du, dw, dqd, dkd, dat, dgl, ds_scr):
        @pl.when(pl.program_id(0) == 0)
        def _():
            ds_scr[...] = jnp.zeros_like(ds_scr)

        def step(jj, carry):
            j = per - 1 - jj
            sl = pl.ds(pl.multiple_of(j * CHUNK, CHUNK), CHUNK)
            _, pull = jax.vjp(_gdn_seq, sall_ref[j], u_ref[sl, :], w_ref[sl, :], qd_ref[sl, :], kd_ref[sl, :],
                              at_ref[sl, :], gl_ref[sl, :])
            res = pull((do_ref[sl, :], ds_scr[...]))
            ds_scr[...] = res[0]
            for o_ref, val in zip((du, dw, dqd, dkd, dat, dgl), res[1:]):
                o_ref[sl, :] = val
            return carry

        lax.fori_loop(0, per, step, 0)

    spec = lambda width: pl.BlockSpec((rows, width), lambda i: (n - 1 - i, 0))
    widths = [w, w, w, w, N_HEADS * CHUNK, LANES]
    return pl.pallas_call(
        body, name=name, grid=(n,),
        in_specs=[spec(x) for x in widths] + [pl.BlockSpec((per, w, HEAD_DIM), lambda i: (n - 1 - i, 0, 0)), spec(w)],
        out_specs=[spec(x) for x in widths],
        out_shape=[jax.ShapeDtypeStruct((t, x), F32) for x in widths],
        scratch_shapes=[pltpu.VMEM((w, HEAD_DIM), F32)],
        compiler_params=_params(("arbitrary",)),
    )(*prep, s_all, do)


QK_DIM = 2 * HEAD_DIM
ATT_TILE = 1024
NEG = -1e30


ATT_SPLIT = 4


def _chunk_mask(n_rows, n_cols, key_major, query_offset):
    r = lax.broadcasted_iota(jnp.int32, (n_rows, n_cols), 0)
    c = lax.broadcasted_iota(jnp.int32, (n_rows, n_cols), 1)
    if key_major:
        return r // CHUNK <= (c + query_offset) // CHUNK
    return c // CHUNK <= (r + query_offset) // CHUNK


def _visible_keys(tile, diagonal):
    hq = tile // ATT_SPLIT
    return [(a + 1) * hq if diagonal else tile for a in range(ATT_SPLIT)]


def _dot_nn(a, b):
    return lax.dot_general(a, b, NN, preferred_element_type=F32)


def _blocked_transpose(x, width):
    t = x.shape[0]
    tile = min(ATT_TILE, t)
    return x.reshape(t // tile, tile, N_HEADS, width).transpose(2, 0, 3, 1)


def _attn_fwd(q, kt, v1, *, name):
    t = q.shape[0]
    tq = min(ATT_TILE, t)
    nq = t // tq

    def body(q_ref, kt_ref, v_ref, o_ref, lse_ref, m_scr, acc_scr):
        qi = pl.program_id(1)
        m_scr[...] = jnp.full_like(m_scr, NEG)
        acc_scr[...] = jnp.zeros_like(acc_scr)
        hq = tq // ATT_SPLIT
        parts = [pl.ds(a * hq, hq) for a in range(ATT_SPLIT)]
        qs = [q_ref[sl, :] for sl in parts]

        def step(kj, masked):
            rows = pl.ds(pl.multiple_of(kj * tq, tq), tq)
            kt_blk, vv = kt_ref[kj], v_ref[rows, :]
            seen = _visible_keys(tq, masked)
            ss = [_dot_nn(qv, kt_blk[:, :w]) for qv, w in zip(qs, seen)]
            for a, sl in enumerate(parts):
                s = ss[a]
                if masked:
                    s = jnp.where(_chunk_mask(hq, seen[a], False, a * hq), s, NEG)
                m_old = m_scr[sl, :]
                m_new = jnp.maximum(m_old, jnp.max(s, axis=1, keepdims=True))
                p = jnp.exp(s - m_new)
                acc_scr[sl, :] = jnp.exp(m_old - m_new) * acc_scr[sl, :] + _dot_nn(p.astype(BF16), vv[:seen[a], :])
                m_scr[sl, :] = m_new

        def loop_body(kj, carry):
            step(kj, False)
            return carry

        lax.fori_loop(0, qi, loop_body, 0)
        step(qi, True)
        acc = acc_scr[...]
        o_ref[...] = (acc[:, :HEAD_DIM] / acc[:, HEAD_DIM:]).astype(o_ref.dtype)
        lse_ref[...] = m_scr[...] + jnp.log(acc[:, HEAD_DIM:HEAD_DIM + 1])

    return pl.pallas_call(
        body, name=name, grid=(N_HEADS, nq),
        in_specs=[pl.BlockSpec((tq, QK_DIM), lambda h, i: (i, h)),
                  pl.BlockSpec((None, nq, QK_DIM, tq), lambda h, i: (h, 0, 0, 0)),
                  pl.BlockSpec((t, 2 * HEAD_DIM), lambda h, i: (0, h))],
        out_specs=[pl.BlockSpec((tq, HEAD_DIM), lambda h, i: (i, h)),
                   pl.BlockSpec((None, tq, 1), lambda h, i: (h, i, 0))],
        out_shape=[jax.ShapeDtypeStruct((t, N_HEADS * HEAD_DIM), BF16),
                   jax.ShapeDtypeStruct((N_HEADS, t, 1), F32)],
        scratch_shapes=[pltpu.VMEM((tq, 1), F32), pltpu.VMEM((tq, 2 * HEAD_DIM), F32)],
        compiler_params=_params(("arbitrary", "arbitrary")),
    )(q, kt, v1)


def _attn_bwd_dq(q, k, kt, vt, o, lse, dom, do_col0, *, name):
    t = q.shape[0]
    tq = min(ATT_TILE, t)
    nq = t // tq

    def body(q_ref, k_ref, kt_ref, vt_ref, o_ref, lse_ref, do_ref, dq_ref, delta_ref, acc_scr):
        qi = pl.program_id(1)
        acc_scr[...] = jnp.zeros_like(acc_scr)
        do = do_ref[...]
        delta = jnp.sum(do * o_ref[...].astype(F32), axis=1, keepdims=True)
        delta_ref[...] = delta
        hq = tq // ATT_SPLIT
        parts = [pl.ds(a * hq, hq) for a in range(ATT_SPLIT)]
        qs = [q_ref[sl, :] for sl in parts]
        dos = [do_ref[sl, :].astype(BF16) for sl in parts]
        lses = [lse_ref[sl, :] for sl in parts]
        deltas = [delta[a * hq:(a + 1) * hq, :] for a in range(ATT_SPLIT)]

        def step(kj, masked):
            rows = pl.ds(pl.multiple_of(kj * tq, tq), tq)
            kt_blk, vt_blk = kt_ref[kj], vt_ref[kj]
            seen = _visible_keys(tq, masked)
            ss = [_dot_nn(qv, kt_blk[:, :w]) for qv, w in zip(qs, seen)]
            dps = [_dot_nn(do_b, vt_blk[:, :w]) for do_b, w in zip(dos, seen)]
            kv_ = k_ref[rows, :]
            for a, sl in enumerate(parts):
                p = jnp.exp(ss[a] - lses[a])
                if masked:
                    p = jnp.where(_chunk_mask(hq, seen[a], False, a * hq), p, 0.0)
                ds = p * (dps[a] - deltas[a])
                acc_scr[sl, :] += _dot_nn(ds.astype(BF16), kv_[:seen[a], :])

        def loop_body(kj, carry):
            step(kj, False)
            return carry

        lax.fori_loop(0, qi, loop_body, 0)
        step(qi, True)
        dq_ref[...] = acc_scr[...].astype(dq_ref.dtype)

    return pl.pallas_call(
        body, name=name, grid=(N_HEADS, nq),
        in_specs=[pl.BlockSpec((tq, QK_DIM), lambda h, i: (i, h)),
                  pl.BlockSpec((t, QK_DIM), lambda h, i: (0, h)),
                  pl.BlockSpec((None, nq, QK_DIM, tq), lambda h, i: (h, 0, 0, 0)),
                  pl.BlockSpec((None, nq, HEAD_DIM, tq), lambda h, i: (h, 0, 0, 0)),
                  pl.BlockSpec((tq, HEAD_DIM), lambda h, i: (i, h)),
                  pl.BlockSpec((None, tq, 1), lambda h, i: (h, i, 0)),
                  pl.BlockSpec((tq, HEAD_DIM), lambda h, i: (i, do_col0 + h))],
        out_specs=[pl.BlockSpec((tq, QK_DIM), lambda h, i: (i, h)),
                   pl.BlockSpec((None, tq, 1), lambda h, i: (h, i, 0))],
        out_shape=[jax.ShapeDtypeStruct((t, N_HEADS * QK_DIM), BF16),
                   jax.ShapeDtypeStruct((N_HEADS, t, 1), F32)],
        scratch_shapes=[pltpu.VMEM((tq, QK_DIM), F32)],
        compiler_params=_params(("arbitrary", "arbitrary")),
    )(q, k, kt, vt, o, lse, dom)


def _attn_bwd_dkv(q, qt, k, v, lse_row, delta_row, dom, dot, do_col0, *, name):
    t = q.shape[0]
    tk = min(ATT_TILE, t)
    nk = t // tk

    def body(q_ref, qt_ref, k_ref, v_ref, lse_ref, delta_ref, do_ref, dot_ref, dk_ref, dv_ref, dk_scr, dv_scr):
        kj = pl.program_id(1)
        dk_scr[...] = jnp.zeros_like(dk_scr)
        dv_scr[...] = jnp.zeros_like(dv_scr)
        kv_ = k_ref[...]
        vv = v_ref[...]

        hq = tk // ATT_SPLIT

        def step(qi, masked):
            lse_v, delta_v = lse_ref[qi], delta_ref[qi]
            qt_blk, dot_blk = qt_ref[qi], dot_ref[qi]
            qs, dos = [], []
            for a in range(ATT_SPLIT):
                rows = pl.ds(pl.multiple_of(qi * tk + a * hq, hq), hq)
                qs.append(q_ref[rows, :])
                dos.append(do_ref[rows, :].astype(BF16))
            seen = _visible_keys(tk, masked)
            ss = [_dot_nn(kv_[:seen[a], :], qt_blk[:, a * hq:(a + 1) * hq]) for a in range(ATT_SPLIT)]
            dps = [_dot_nn(vv[:seen[a], :], dot_blk[:, a * hq:(a + 1) * hq]) for a in range(ATT_SPLIT)]
            for a in range(ATT_SPLIT):
                cols = slice(a * hq, (a + 1) * hq)
                keys = pl.ds(0, seen[a])
                p = jnp.exp(ss[a] - lse_v[:, cols])
                if masked:
                    p = jnp.where(_chunk_mask(seen[a], hq, True, a * hq), p, 0.0)
                dv_scr[keys, :] += _dot_nn(p.astype(BF16), dos[a])
                ds = p * (dps[a] - delta_v[:, cols])
                dk_scr[keys, :] += _dot_nn(ds.astype(BF16), qs[a])

        step(kj, True)

        def loop_body(qi, carry):
            step(qi, False)
            return carry

        lax.fori_loop(kj + 1, nk, loop_body, 0)
        dk_ref[...] = dk_scr[...].astype(dk_ref.dtype)
        dv_ref[...] = dv_scr[...].astype(dv_ref.dtype)

    stat = pl.BlockSpec((None, nk, 1, tk), lambda h, j: (h, 0, 0, 0))
    return pl.pallas_call(
        body, name=name, grid=(N_HEADS, nk),
        in_specs=[pl.BlockSpec((t, QK_DIM), lambda h, j: (0, h)),
                  pl.BlockSpec((None, nk, QK_DIM, tk), lambda h, j: (h, 0, 0, 0)),
                  pl.BlockSpec((tk, QK_DIM), lambda h, j: (j, h)),
                  pl.BlockSpec((tk, HEAD_DIM), lambda h, j: (j, h)),
                  stat, stat,
                  pl.BlockSpec((t, HEAD_DIM), lambda h, j: (0, do_col0 + h)),
                  pl.BlockSpec((None, nk, HEAD_DIM, tk), lambda h, j: (h, 0, 0, 0))],
        out_specs=[pl.BlockSpec((tk, QK_DIM), lambda h, j: (j, h)),
                   pl.BlockSpec((tk, HEAD_DIM), lambda h, j: (j, h))],
        out_shape=[jax.ShapeDtypeStruct((t, N_HEADS * QK_DIM), BF16),
                   jax.ShapeDtypeStruct((t, N_HEADS * HEAD_DIM), BF16)],
        scratch_shapes=[pltpu.VMEM((tk, QK_DIM), F32), pltpu.VMEM((tk, HEAD_DIM), F32)],
        compiler_params=_params(("arbitrary", "arbitrary")),
    )(q, qt, k, v, lse_row, delta_row, dom, dot)


def _rope_tables(pos_col, inv_freq_row, *, name):
    t = pos_col.shape[0]
    tm = min(ROW_TILE, t)

    def body(p_ref, f_ref, c_ref, s_ref):
        ang = p_ref[...].astype(F32) * f_ref[...]
        lane = _lane(ang.shape)
        c_ref[...] = jnp.where(lane < ROPE_DIM, jnp.cos(ang), 0.0)
        sn = jnp.sin(ang)
        s_ref[...] = jnp.where(lane < ROPE_DIM // 2, -sn, jnp.where(lane < ROPE_DIM, sn, 0.0))

    out = pl.BlockSpec((tm, LANES), lambda i: (i, 0))
    return pl.pallas_call(
        body, name=name, grid=(t // tm,),
        in_specs=[pl.BlockSpec((tm, 1), lambda i: (i, 0)), pl.BlockSpec((1, LANES), lambda i: (0, 0))],
        out_specs=[out, out], out_shape=[jax.ShapeDtypeStruct((t, LANES), F32)] * 2,
        compiler_params=_params(("arbitrary",)),
    )(pos_col, inv_freq_row)


def _loss_head(y, target):
    width = y.shape[1]

    def fn(yv, tv):
        e = yv - tv
        part = 0.5 * jnp.sum(jnp.mean(e * e, axis=1, keepdims=True), axis=0, keepdims=True)
        return e * (1.0 / width), jnp.broadcast_to(part, (HALO, LANES))

    return _rowwise(fn, [(y, width, 0), (target, width, 0)], [], [(width, F32)], [(HALO, LANES)], name="loss_head")


def _adamw(w, g, m, v, *, name):
    shape = w.shape
    w2, g2, m2, v2 = (a.reshape(-1, shape[-1]) for a in (w, g, m, v))
    rows, width = w2.shape
    tr = _divisor_tile(rows, max(8, (1 << 19) // max(width, 1)), 8)
    bc1 = 1.0 - ADAM_B1 ** ADAM_STEP
    bc2 = 1.0 - ADAM_B2 ** ADAM_STEP

    def body(w_ref, g_ref, m_ref, v_ref, d_ref, mo_ref, vo_ref):
        gv = g_ref[...]
        mn = ADAM_B1 * m_ref[...] + (1.0 - ADAM_B1) * gv
        vn = ADAM_B2 * v_ref[...] + (1.0 - ADAM_B2) * (gv * gv)
        d_ref[...] = -ADAM_LR * ((mn / bc1) / (jnp.sqrt(vn / bc2) + ADAM_EPS) + ADAM_WD * w_ref[...])
        mo_ref[...] = mn
        vo_ref[...] = vn

    spec = pl.BlockSpec((tr, width), lambda i: (i, 0))
    outs = pl.pallas_call(
        body, name=name, grid=(rows // tr,), in_specs=[spec] * 4, out_specs=[spec] * 3,
        out_shape=[jax.ShapeDtypeStruct((rows, width), F32)] * 3,
        compiler_params=_params(("arbitrary",)),
    )(w2, g2, m2, v2)
    return tuple(o.reshape(shape) for o in outs)


HBM_SPEC = pl.BlockSpec(memory_space=pltpu.HBM)


def _position():
    return lax.axis_index("x"), lax.axis_index("y"), lax.axis_index("c")


def _other_chips(x, y):
    return [(1 - x, y), (x, 1 - y), (1 - x, 1 - y)]


class _Stream:
    def __init__(self, kind, size=0):
        self.kind, self.size = kind, size
        self.parts = 2 if kind == "heads" else 1

    def local(self, ref, k, part):
        if self.kind == "rows":
            return ref.at[:, pl.ds(k * self.size, self.size), :]
        if self.kind == "cols":
            return ref.at[:, :, pl.ds(k * self.size, self.size)]
        if self.kind == "heads":
            return ref.at[:, :, pl.ds(part * N_HEADS * HEAD_DIM + k * HEAD_DIM, HEAD_DIM)]
        if self.kind == "piece":
            return ref.at[k]
        return ref

    def shard(self, ref, part):
        if self.kind == "heads":
            return ref.at[:, :, pl.ds(part * HEAD_DIM, HEAD_DIM)]
        return ref

    def half_local(self, ref, k, part, cc, hd):
        if self.kind == "piece":
            return ref.at[k, pl.ds(cc * hd, hd)]
        return self.local(ref.at[pl.ds(cc * hd, hd)], k, part)


def _remote(src, dst, send_sems, recv_sems, idx, to):
    return pltpu.make_async_remote_copy(src_ref=src, dst_ref=dst, send_sem=send_sems.at[idx],
                                        recv_sem=recv_sems.at[idx], device_id=to, device_id_type=MESH)


def _comm_call(body, ins, out_shapes, n_remote, n_local, *, name):
    scratch = [pltpu.SemaphoreType.DMA((n_remote,)), pltpu.SemaphoreType.DMA((n_remote,))]
    if n_local:
        scratch.append(pltpu.SemaphoreType.DMA((n_local,)))
    return pl.pallas_call(
        body, name=name, in_specs=[HBM_SPEC] * len(ins), out_specs=[HBM_SPEC] * len(out_shapes), out_shape=out_shapes,
        scratch_shapes=scratch, compiler_params=pltpu.CompilerParams(has_side_effects=True),
    )(*ins)


def _gather_chips(shards, streams, out_shapes, *, name):
    n = len(shards)
    hd = shards[0].shape[0] // 2
    flat = [(t, part) for t in range(n) for part in range(streams[t].parts)]
    ns = len(flat)

    def body(*refs):
        s_refs, o_refs = refs[:n], refs[n:2 * n]
        send_sems, recv_sems = refs[2 * n:]
        x, y, c = _position()
        sibling = (x, y, 1 - c)
        chips = _other_chips(x, y)
        me = 2 * x + y
        sent = []
        for s, (t, part) in enumerate(flat):
            st = streams[t]
            sent.append(_remote(st.shard(s_refs[t], part), st.local(o_refs[t], me, part), send_sems, recv_sems,
                                6 * ns + s, sibling))
            sent[-1].start()
            src = st.shard(s_refs[t].at[pl.ds(c * hd, hd)], part)
            for j, (cx, cy) in enumerate(chips):
                sent.append(_remote(src, st.half_local(o_refs[t], me, part, c, hd), send_sems, recv_sems,
                                    3 * s + j, (cx, cy, c)))
                sent[-1].start()
        for s, (t, part) in enumerate(flat):
            st = streams[t]
            for j, (cx, cy) in enumerate(chips):
                blk = st.half_local(o_refs[t], 2 * cx + cy, part, c, hd)
                _remote(blk, blk, send_sems, recv_sems, 3 * s + j, (x, y, c)).wait_recv()
                sent.append(_remote(blk, blk, send_sems, recv_sems, 3 * ns + 3 * s + j, sibling))
                sent[-1].start()
        for s, (t, part) in enumerate(flat):
            st = streams[t]
            for j, (cx, cy) in enumerate(chips):
                blk = st.half_local(o_refs[t], 2 * cx + cy, part, 1 - c, hd)
                _remote(blk, blk, send_sems, recv_sems, 3 * ns + 3 * s + j, (x, y, c)).wait_recv()
            own = st.local(o_refs[t], me, part)
            _remote(own, own, send_sems, recv_sems, 6 * ns + s, (x, y, c)).wait_recv()
        for cp in sent:
            cp.wait_send()

    return _comm_call(body, shards, out_shapes, 7 * ns, 0, name=name)


def _sibling_take_other_half(gs, *, name):
    n = len(gs)
    hd = gs[0].shape[0] // 2

    def body(*refs):
        g_refs, o_refs = refs[:n], refs[n:2 * n]
        send_sems, recv_sems = refs[2 * n:]
        x, y, c = _position()
        copies = [_remote(g_refs[t].at[pl.ds((1 - c) * hd, hd)], o_refs[t], send_sems, recv_sems, t, (x, y, 1 - c))
                  for t in range(n)]
        for cp in copies:
            cp.start()
        for cp in copies:
            cp.wait()

    outs = [jax.ShapeDtypeStruct((hd,) + g.shape[1:], g.dtype) for g in gs]
    return _comm_call(body, gs, outs, n, 0, name=name)


def _chips_exchange(ps, streams, shard_shapes, *, name):
    n = len(ps)
    flat = [(t, part) for t in range(n) for part in range(streams[t].parts)]

    def body(*refs):
        p_refs, o_refs = refs[:n], refs[n:2 * n]
        send_sems, recv_sems = refs[2 * n:]
        x, y, c = _position()
        copies = []
        for s, (t, part) in enumerate(flat):
            st = streams[t]
            for j, (cx, cy) in enumerate(_other_chips(x, y)):
                copies.append(_remote(st.local(p_refs[t], 2 * cx + cy, part), st.shard(o_refs[t].at[j], part),
                                      send_sems, recv_sems, 3 * s + j, (cx, cy, c)))
        for cp in copies:
            cp.start()
        for cp in copies:
            cp.wait()

    outs = [jax.ShapeDtypeStruct((3,) + tuple(shp), p.dtype) for p, shp in zip(ps, shard_shapes)]
    return _comm_call(body, ps, outs, 3 * len(flat), 0, name=name)


def _sibling_join_halves(bufs, *, name):
    n = len(bufs)
    hd = bufs[0].shape[0] // 2

    def body(*refs):
        o_refs = refs[n:2 * n]
        send_sems, recv_sems = refs[2 * n:]
        x, y, c = _position()
        sent = []
        for t in range(n):
            mine = o_refs[t].at[pl.ds(c * hd, hd)]
            sent.append(_remote(mine, mine, send_sems, recv_sems, t, (x, y, 1 - c)))
            sent[-1].start()
        for t in range(n):
            theirs = o_refs[t].at[pl.ds((1 - c) * hd, hd)]
            _remote(theirs, theirs, send_sems, recv_sems, t, (x, y, c)).wait_recv()
        for cp in sent:
            cp.wait_send()

    return pl.pallas_call(
        body, name=name, in_specs=[HBM_SPEC] * n, out_specs=[HBM_SPEC] * n,
        out_shape=[jax.ShapeDtypeStruct(b.shape, b.dtype) for b in bufs],
        scratch_shapes=[pltpu.SemaphoreType.DMA((n,)), pltpu.SemaphoreType.DMA((n,))],
        input_output_aliases={t: t for t in range(n)},
        compiler_params=pltpu.CompilerParams(has_side_effects=True),
    )(*bufs)


def _row_tile(rows, width):
    return _divisor_tile(rows, max(16, (1 << 19) // width), 16)


def _add_own_half(g, got, c_idx, out_dtype, *, name):
    hd, r, w = got.shape
    tr = _row_tile(r, w)

    def body(c_ref, g_ref, a_ref, o_ref):
        o_ref[...] = (g_ref[...] + a_ref[...]).astype(o_ref.dtype)

    return pl.pallas_call(
        body, name=name,
        grid_spec=pltpu.PrefetchScalarGridSpec(
            num_scalar_prefetch=1, grid=(hd, r // tr),
            in_specs=[pl.BlockSpec((None, None, tr, w), lambda l, i, c_ref: (c_ref[0], l, i, 0)),
                      pl.BlockSpec((None, tr, w), lambda l, i, c_ref: (l, i, 0))],
            out_specs=pl.BlockSpec((None, tr, w), lambda l, i, c_ref: (l, i, 0))),
        out_shape=jax.ShapeDtypeStruct((hd, r, w), out_dtype),
        compiler_params=_params(("arbitrary", "arbitrary")),
    )(c_idx, g.reshape((2, hd) + g.shape[1:]), got)


def _sum_chips(p, got, place, stream, *, name):
    _, hd, rs, cs = got.shape
    wb = HEAD_DIM if stream.kind == "heads" else cs
    tr = _row_tile(rs, wb)
    kind, size = stream.kind, stream.size

    def own_index(l, i, g, k_ref, c_ref):
        k = k_ref[0]
        if kind == "rows":
            return (l, k * (size // tr) + i, 0)
        if kind == "cols":
            return (l, i, k)
        if kind == "heads":
            return (l, i, g * N_HEADS + k)
        if kind == "piece":
            return (k, l, i, 0)
        return (l, i, 0)

    own_blk = (None, None, tr, wb) if kind == "piece" else (None, tr, wb)

    def body(k_ref, c_ref, p_ref, fx_ref, fy_ref, fxy_ref, o_ref):
        f = lambda r: r[...].astype(F32)
        o_ref[...] = (f(p_ref) + f(fy_ref)) + (f(fx_ref) + f(fxy_ref))

    def rel(j):
        return pl.BlockSpec((None, None, tr, wb), functools.partial(lambda l, i, g, k_ref, c_ref, j: (j, l, i, g), j=j))

    return pl.pallas_call(
        body, name=name,
        grid_spec=pltpu.PrefetchScalarGridSpec(
            num_scalar_prefetch=2, grid=(hd, rs // tr, stream.parts),
            in_specs=[pl.BlockSpec(own_blk, own_index), rel(0), rel(1), rel(2)],
            out_specs=pl.BlockSpec((None, tr, wb), lambda l, i, g, k_ref, c_ref: (c_ref[0] * hd + l, i, g))),
        out_shape=jax.ShapeDtypeStruct((2 * hd, rs, cs), F32),
        compiler_params=_params(("arbitrary", "arbitrary", "arbitrary")),
    )(place[0], place[1], p, got, got, got)


MATRICES = ("w_in", "w_uq", "w_ukv", "w_out", "w_gate_up", "w_down", "w_ple", "w_ple_gate", "conv_w")
VECTORS = ("a_log", "dt_bias", "gdn_norm_g", "q_norm_g", "kv_norm_g", "ln1_g", "ln1_b", "ln2_g", "ln2_b")
WEIGHTS = ("w_in", "conv_w", "a_log", "dt_bias", "gdn_norm_g", "q_norm_g", "w_uq", "kv_norm_g", "w_ukv", "w_out",
           "ln1_g", "ln1_b", "w_gate_up", "w_down", "ln2_g", "ln2_b", "w_ple", "w_ple_gate")
ROW_SHARDED = ("w_out", "w_down", "w_ple_gate")
N_CHIPS = 4


def _stream_of(name, shard_shape):
    if name in ("w_in", "w_uq"):
        return _Stream("piece")
    if name == "w_ukv":
        return _Stream("heads")
    if name in ROW_SHARDED:
        return _Stream("rows", shard_shape[0])
    return _Stream("cols", shard_shape[1])


def _pack_vectors(vecs, depth):
    flat = jnp.concatenate([vecs[n].reshape(depth, -1) for n in VECTORS], axis=1)
    pad = jnp.zeros((depth, VEC_ROWS * LANES - flat.shape[1]), F32)
    return jnp.concatenate([flat, pad], axis=1).reshape(depth, VEC_ROWS, LANES)


def _unpack_vectors(packed, shapes):
    depth = packed.shape[0]
    flat = packed.reshape(depth, VEC_ROWS * LANES)
    out, off = {}, 0
    for n in VECTORS:
        out[n] = flat[:, off:off + shapes[n][1]]
        off += shapes[n][1]
    return out


VEC_ROWS = 40


class _Dims:
    def __init__(self, d_model, in_width, q_lora, kv_lora, d_ff2, ple_dim):
        self.d = d_model
        self.hw = N_HEADS * HEAD_DIM
        self.in_width = in_width
        self.q_lora, self.kv_lora = q_lora, kv_lora
        self.ff2 = d_ff2
        self.ple = ple_dim
        self.c_kv0 = 4 * self.hw
        self.c_q0 = self.c_kv0 + kv_lora
        self.misc0 = self.c_q0 + q_lora
        self.h_width = self.misc0 + LANES
        assert self.c_kv0 % kv_lora == 0 and self.c_q0 % q_lora == 0 and self.misc0 % LANES == 0
        self.g_beta = 4 * self.hw
        self.g_a = self.g_beta + N_HEADS
        self.g_cq = self.g_a + N_HEADS
        self.g_ckv = self.g_cq + q_lora
        self.g_kr = self.g_ckv + kv_lora
        assert self.g_kr + ROPE_DIM == in_width

    def w_in_local(self, w):
        pad = jnp.zeros(w.shape[:-1] + (self.h_width - self.in_width,), w.dtype)
        return jnp.concatenate([w[..., :self.g_beta], w[..., self.g_ckv:self.g_kr], w[..., self.g_cq:self.g_ckv],
                                w[..., self.g_kr:], w[..., self.g_beta:self.g_cq], pad], axis=-1)

    def w_in_global(self, d):
        m = self.misc0
        return jnp.concatenate([d[..., :self.c_kv0], d[..., m + MISC_BETA0:m + MISC_A0 + N_HEADS],
                                d[..., self.c_q0:self.misc0], d[..., self.c_kv0:self.c_q0], d[..., m:m + ROPE_DIM]],
                               axis=-1)

    def w_uq_local(self, w):
        r = w.reshape(w.shape[:-1] + (N_HEADS, HEAD_DIM + ROPE_DIM))
        r = jnp.pad(r, [(0, 0)] * (r.ndim - 1) + [(0, QK_DIM - HEAD_DIM - ROPE_DIM)])
        return r.reshape(w.shape[:-1] + (N_HEADS * QK_DIM,))

    def w_uq_global(self, d):
        r = d.reshape(d.shape[:-1] + (N_HEADS, QK_DIM))[..., :HEAD_DIM + ROPE_DIM]
        return r.reshape(d.shape[:-1] + (N_HEADS * (HEAD_DIM + ROPE_DIM),))


def _lane_padded(n):
    return -(-n // LANES) * LANES


def _pad_lanes(a):
    pad = _lane_padded(a.shape[-1]) - a.shape[-1]
    return a if pad == 0 else jnp.pad(a, [(0, 0)] * (a.ndim - 1) + [(0, pad)])


def _lane_row(vec, lane0):
    pad = LANES - lane0 - vec.shape[0]
    return jnp.concatenate([jnp.zeros((lane0,), F32), vec.astype(F32), jnp.zeros((pad,), F32)])[None, :]


def _layer_fwd(dm, alpha, x, xb, p_i, cos_t, sin_t, wl, tag):
    d, hw = dm.d, dm.hw
    nm = lambda s: f"{s}_{tag}"
    mm = functools.partial(_matmul, layer=wl["layer"])
    h = mm(xb, wl["w_in"], dims="nn", name=nm("f_in"))
    misc_cb = dm.misc0 // LANES

    u = _conv_fwd(h, wl["conv_w"], 3 * hw, name=nm("f_conv"))
    qn, kn, vg, gb = _rowwise(_gdn_act, [(u, 3 * hw, 0), (h, LANES, misc_cb)], [wl["alog_row"], wl["dtb_row"]],
                              [(hw, F32), (hw, F32), (hw, F32), (LANES, F32)], name=nm("f_gdn_act"))
    prep, tinv = _gdn_prep_fwd(qn, kn, vg, gb, name=nm("f_gdn_prep"))
    o_gdn, s_all = _gdn_seq_fwd(prep, name=nm("f_gdn_seq"))
    (og,) = _rowwise(lambda o, z, g: (_gdn_out(o, z, g),), [(o_gdn, hw, 0), (h, hw, 3)], [wl["gn_row"]],
                     [(hw, BF16)], name=nm("f_gdn_out"))

    cqn, ckvn = _rowwise(_mla_norm, [(h, dm.kv_lora, dm.c_kv0 // dm.kv_lora), (h, dm.q_lora, dm.c_q0 // dm.q_lora)],
                         [wl["kvg_row"], wl["qg_row"]], [(dm.q_lora, BF16), (dm.kv_lora, BF16)], name=nm("f_mla_norm"))
    qm = mm(cqn, wl["w_uq"], dims="nn", name=nm("f_uq"))
    kvm = mm(ckvn, wl["w_ukv"], dims="nn", name=nm("f_ukv"))
    scale = (HEAD_DIM + ROPE_DIM) ** -0.5
    qk_fn = functools.partial(_mla_qk, scale)
    qa, ka, va = _rowwise(qk_fn, [(qm, N_HEADS * QK_DIM, 0), (kvm, 2 * hw, 0), (h, LANES, misc_cb),
                                  (cos_t, LANES, 0), (sin_t, LANES, 0)], [],
                          [(N_HEADS * QK_DIM, BF16), (N_HEADS * QK_DIM, BF16), (hw, BF16)], name=nm("f_mla_qk"))
    kt = _blocked_transpose(ka, QK_DIM)
    v_heads = va.reshape(va.shape[0], N_HEADS, HEAD_DIM)
    v1 = jnp.concatenate([v_heads, jnp.ones_like(v_heads)], axis=2).reshape(va.shape[0], 2 * hw)
    o_mla, lse = _attn_fwd(qa, kt, v1, name=nm("f_attn"))

    om = jnp.concatenate([og, o_mla], axis=1)
    mix = mm(om, wl["w_out"], dims="nn", name=nm("f_out"))
    ln1 = lambda xv, yv, g, b: (_layer_norm(alpha * xv + yv, g, b),) * 2
    x1, x1b = _rowwise(ln1, [(x, d, 0), (mix, d, 0)], [wl["ln1_g"], wl["ln1_b"]], [(d, F32), (d, BF16)], name=nm("f_ln1"))

    gu = mm(x1b, wl["w_gate_up"], dims="nn", name=nm("f_gate_up"), out_dtype=BF16)
    (act,) = _rowwise(lambda g_: (_swiglu(g_),), [(gu, dm.ff2, 0)], [], [(dm.ff2 // 2, BF16)], name=nm("f_swiglu"))
    dn = mm(act, wl["w_down"], dims="nn", name=nm("f_down"))
    x2, x2b = _rowwise(ln1, [(x1, d, 0), (dn, d, 0)], [wl["ln2_g"], wl["ln2_b"]], [(d, F32), (d, BF16)], name=nm("f_ln2"))

    pg = mm(x2b, wl["w_ple_gate"], dims="nn", name=nm("f_ple_gate"))
    pe = mm(p_i, wl["w_ple"], dims="nn", name=nm("f_ple"))
    out, outb = _rowwise(lambda a, b, c_: (_ple_out(a, b, c_),) * 2, [(x2, d, 0), (pg, d, 0), (pe, d, 0)], [],
                         [(d, F32), (d, BF16)], name=nm("f_ple_out"))
    saved = dict(x=x, xb=xb, p_i=p_i, h=h, u=u, qn=qn, kn=kn, vg=vg, gb=gb, prep=prep, tinv=tinv, s_all=s_all, o_gdn=o_gdn, cqn=cqn, ckvn=ckvn,
                 qm=qm, kvm=kvm, qa=qa, ka=ka, kt=kt, va=va, o_mla=o_mla, lse=lse, om=om, mix=mix, x1=x1, x1b=x1b, gu=gu,
                 act=act, dn=dn, x2=x2, x2b=x2b, pg=pg, pe=pe)
    return out, outb, saved


def _layer_bwd(dm, alpha, dout, sv, cos_t, sin_t, wl, gbuf, tag):
    d, hw = dm.d, dm.hw
    t = dout.shape[0]
    nm = lambda s: f"{s}_{tag}"
    gr = {}
    gbuf = dict(gbuf)
    misc_cb = dm.misc0 // LANES
    mm = functools.partial(_matmul, layer=wl["layer"])

    def wgrad(name_, a, g):
        gbuf[name_] = mm(a, g, dims="tn", name=nm("b_" + name_), into=gbuf[name_], tm=1408, tn=1408, tk=1024)

    dx2_a, dpg, dpe = _rowwise(_vjp_fn(_ple_out, 3, 1), [(sv["x2"], d, 0), (sv["pg"], d, 0), (sv["pe"], d, 0), (dout, d, 0)],
                               [], [(d, F32), (d, BF16), (d, BF16)], name=nm("b_ple_out"))
    wgrad("w_ple", sv["p_i"], dpe)
    wgrad("w_ple_gate", sv["x2b"], dpg)
    dx2 = mm(dpg, wl["w_ple_gate"], dims="nt", c=dx2_a, name=nm("b_x2"))

    def ln_bwd(xv, yv, ct, g, b):
        _, pull = jax.vjp(lambda a_, b_, c_, d_: _layer_norm(alpha * a_ + b_, c_, d_), xv, yv, g, b)
        return pull(ct)

    dx1_a, ddn, gr["ln2_g"], gr["ln2_b"] = _rowwise(
        ln_bwd, [(sv["x1"], d, 0), (sv["dn"], d, 0), (dx2, d, 0)], [wl["ln2_g"], wl["ln2_b"]],
        [(d, F32), (d, BF16)], [(1, d), (1, d)], name=nm("b_ln2"))
    wgrad("w_down", sv["act"], ddn)
    dact = mm(ddn, wl["w_down"], dims="nt", name=nm("b_act"), out_dtype=BF16)
    (dgu,) = _rowwise(_vjp_fn(_swiglu, 1, 1), [(sv["gu"], dm.ff2, 0), (dact, dm.ff2 // 2, 0)], [], [(dm.ff2, BF16)],
                      name=nm("b_swiglu"))
    wgrad("w_gate_up", sv["x1b"], dgu)
    dx1 = mm(dgu, wl["w_gate_up"], dims="nt", c=dx1_a, name=nm("b_x1"))

    dx_a, dmix, gr["ln1_g"], gr["ln1_b"] = _rowwise(
        ln_bwd, [(sv["x"], d, 0), (sv["mix"], d, 0), (dx1, d, 0)], [wl["ln1_g"], wl["ln1_b"]],
        [(d, F32), (d, BF16)], [(1, d), (1, d)], name=nm("b_ln1"))
    wgrad("w_out", sv["om"], dmix)
    dom = mm(dmix, wl["w_out"], dims="nt", name=nm("b_om"))

    nq = t // min(ATT_TILE, t)
    dqa, delta = _attn_bwd_dq(sv["qa"], sv["ka"], sv["kt"], _blocked_transpose(sv["va"], HEAD_DIM), sv["o_mla"], sv["lse"],
                              dom, hw // HEAD_DIM, name=nm("b_attn_dq"))
    lse_row = sv["lse"].reshape(N_HEADS, nq, 1, t // nq)
    delta_row = delta.reshape(N_HEADS, nq, 1, t // nq)
    dot = _blocked_transpose(dom[:, hw:].astype(BF16), HEAD_DIM)
    dka, dva = _attn_bwd_dkv(sv["qa"], _blocked_transpose(sv["qa"], QK_DIM), sv["ka"], sv["va"], lse_row, delta_row, dom, dot,
                             hw // HEAD_DIM, name=nm("b_attn_dkv"))
    scale = (HEAD_DIM + ROPE_DIM) ** -0.5
    qk_fn = functools.partial(_mla_qk, scale)

    def qk_bwd(qm, kvm, misc, cs, sn, g_q, g_k, g_v):
        _, pull = jax.vjp(lambda a, b, c_: qk_fn(a, b, c_, cs, sn), qm, kvm, misc)
        return pull((g_q, g_k, g_v))

    dqm, dkvm, dmisc_rope = _rowwise(
        qk_bwd, [(sv["qm"], N_HEADS * QK_DIM, 0), (sv["kvm"], 2 * hw, 0), (sv["h"], LANES, misc_cb), (cos_t, LANES, 0),
                 (sin_t, LANES, 0), (dqa, N_HEADS * QK_DIM, 0), (dka, N_HEADS * QK_DIM, 0), (dva, hw, 0)], [],
        [(N_HEADS * QK_DIM, BF16), (2 * hw, BF16), (LANES, F32)], name=nm("b_mla_qk"))
    wgrad("w_uq", sv["cqn"], dqm)
    wgrad("w_ukv", sv["ckvn"], dkvm)
    dcqn = mm(dqm, wl["w_uq"], dims="nt", name=nm("b_cqn"))
    dckvn = mm(dkvm, wl["w_ukv"], dims="nt", name=nm("b_ckvn"))

    def norm_bwd(ckv, cq, g_q, g_kv, kvg, qg):
        _, pull = jax.vjp(_mla_norm, ckv, cq, kvg, qg)
        return pull((g_q, g_kv))

    dckv, dcq, gr["kvg_row"], gr["qg_row"] = _rowwise(
        norm_bwd, [(sv["h"], dm.kv_lora, dm.c_kv0 // dm.kv_lora), (sv["h"], dm.q_lora, dm.c_q0 // dm.q_lora),
                   (dcqn, dm.q_lora, 0), (dckvn, dm.kv_lora, 0)], [wl["kvg_row"], wl["qg_row"]],
        [(dm.kv_lora, BF16), (dm.q_lora, BF16)], [(1, dm.kv_lora), (1, dm.q_lora)], name=nm("b_mla_norm"))

    def gout_bwd(o, z, g_o, gn):
        _, pull = jax.vjp(_gdn_out, o, z, gn)
        return pull(g_o)

    do_gdn, dz, gr["gn_row"] = _rowwise(gout_bwd, [(sv["o_gdn"], hw, 0), (sv["h"], hw, 3), (dom, hw, 0)], [wl["gn_row"]],
                                        [(hw, F32), (hw, BF16)], [(1, HEAD_DIM)], name=nm("b_gdn_out"))
    dprep = _gdn_seq_bwd(sv["prep"], sv["s_all"], do_gdn, name=nm("b_gdn_seq"))
    dqn, dkn, dvg, dgb = _gdn_prep_bwd(sv["qn"], sv["kn"], sv["vg"], sv["gb"], sv["tinv"], dprep, name=nm("b_gdn_prep"))

    def act_bwd(u, misc, g_q, g_k, g_v, g_gb, g_rope, alog, dtb):
        _, pull = jax.vjp(_gdn_act, u, misc, alog, dtb)
        du_, dmisc_, dalog_, ddtb_ = pull((g_q, g_k, g_v, g_gb))
        return du_, dmisc_ + g_rope, dalog_, ddtb_

    du, dmisc, gr["alog_row"], gr["dtb_row"] = _rowwise(
        act_bwd, [(sv["u"], 3 * hw, 0), (sv["h"], LANES, misc_cb), (dqn, hw, 0), (dkn, hw, 0), (dvg, hw, 0),
                  (dgb, LANES, 0), (dmisc_rope, LANES, 0)], [wl["alog_row"], wl["dtb_row"]],
        [(3 * hw, F32), (LANES, BF16)], [(1, LANES), (1, LANES)], name=nm("b_gdn_act"))
    dqkv, dconv = _conv_bwd(du, sv["h"], wl["conv_w"], 3 * hw, name=nm("b_conv"))
    gr["conv_w"] = dconv[:CONV_W]

    dh = jnp.concatenate([dqkv, dz, dckv, dcq, dmisc], axis=1)
    wgrad("w_in", sv["xb"], dh)
    dx = mm(dh, wl["w_in"], dims="nt", c=dx_a, name=nm("b_x"), tk=1408)
    return dx, gbuf, gr


LOCAL_MATRICES = ("w_in", "w_uq", "w_ukv", "w_out", "w_gate_up", "w_down", "w_ple", "w_ple_gate")


def _layer_weights(mats, vecs, layer):
    wl = {n: mats[n] for n in LOCAL_MATRICES}
    wl["layer"] = layer
    wl["conv_w"] = mats["conv_w"][layer]
    wl["alog_row"] = _lane_row(vecs["a_log"][layer], MISC_A0)
    wl["dtb_row"] = _lane_row(vecs["dt_bias"][layer], MISC_A0)
    wl["gn_row"] = vecs["gdn_norm_g"][layer][None, :]
    wl["qg_row"] = vecs["q_norm_g"][layer][None, :]
    wl["kvg_row"] = vecs["kv_norm_g"][layer][None, :]
    for n in ("ln1_g", "ln1_b", "ln2_g", "ln2_b"):
        wl[n] = vecs[n][layer][None, :]
    return wl


def _vector_grads(gr):
    out = {"a_log": gr["alog_row"][0, MISC_A0:MISC_A0 + N_HEADS], "dt_bias": gr["dtb_row"][0, MISC_A0:MISC_A0 + N_HEADS],
           "gdn_norm_g": gr["gn_row"][0], "q_norm_g": gr["qg_row"][0], "kv_norm_g": gr["kvg_row"][0]}
    for n in ("ln1_g", "ln1_b", "ln2_g", "ln2_b"):
        out[n] = gr[n][0]
    return out


def _local_step(dm, x, p, positions, target, mats, vecs):
    depth = p.shape[0]
    alpha = (2.0 * depth) ** 0.25
    freq = ROPE_THETA ** (-jnp.arange(0, ROPE_DIM, 2, dtype=F32) / ROPE_DIM)
    inv_freq_row = _lane_row(jnp.concatenate([freq, freq]), 0)
    cos_t, sin_t = _rope_tables(positions.reshape(-1, 1), inv_freq_row, name="rope_tables")

    wls = [_layer_weights(mats, vecs, i) for i in range(depth)]
    saved = []
    cur, cur_b = x, x
    for i in range(depth):
        cur, cur_b, sv = _layer_fwd(dm, alpha, cur, cur_b, p[i], cos_t, sin_t, wls[i], f"l{i}")
        saved.append(sv)
    dy, loss_blk = _loss_head(cur, target)
    gbuf = {n: jnp.zeros(mats[n].shape, F32) for n in LOCAL_MATRICES}
    conv_g, vec_g = [None] * depth, [None] * depth
    for i in reversed(range(depth)):
        dy, gbuf, gr = _layer_bwd(dm, alpha, dy, saved[i], cos_t, sin_t, wls[i], gbuf, f"l{i}")
        conv_g[i] = gr["conv_w"]
        vec_g[i] = _vector_grads(gr)
    vec_grads = {n: jnp.stack([vec_g[i][n] for i in range(depth)]) for n in VECTORS}
    return loss_blk[0, 0], dy, gbuf, jnp.stack(conv_g), vec_grads


def kernel(x, p, positions, w_in, conv_w, a_log, dt_bias, gdn_norm_g, q_norm_g, w_uq, kv_norm_g, w_ukv, w_out, ln1_g, ln1_b, w_gate_up, w_down, ln2_g, ln2_b, w_ple, w_ple_gate, loss_target, m_w_in, m_conv_w, m_a_log, m_dt_bias, m_gdn_norm_g, m_q_norm_g, m_w_uq, m_kv_norm_g, m_w_ukv, m_w_out, m_ln1_g, m_ln1_b, m_w_gate_up, m_w_down, m_ln2_g, m_ln2_b, m_w_ple, m_w_ple_gate, v_w_in, v_conv_w, v_a_log, v_dt_bias, v_gdn_norm_g, v_q_norm_g, v_w_uq, v_kv_norm_g, v_w_ukv, v_w_out, v_ln1_g, v_ln1_b, v_w_gate_up, v_w_down, v_ln2_g, v_ln2_b, v_w_ple, v_w_ple_gate):
    w = dict(w_in=w_in, conv_w=conv_w, a_log=a_log, dt_bias=dt_bias, gdn_norm_g=gdn_norm_g, q_norm_g=q_norm_g, w_uq=w_uq,
             kv_norm_g=kv_norm_g, w_ukv=w_ukv, w_out=w_out, ln1_g=ln1_g, ln1_b=ln1_b, w_gate_up=w_gate_up, w_down=w_down,
             ln2_g=ln2_g, ln2_b=ln2_b, w_ple=w_ple, w_ple_gate=w_ple_gate)
    m = dict(w_in=m_w_in, conv_w=m_conv_w, a_log=m_a_log, dt_bias=m_dt_bias, gdn_norm_g=m_gdn_norm_g, q_norm_g=m_q_norm_g,
             w_uq=m_w_uq, kv_norm_g=m_kv_norm_g, w_ukv=m_w_ukv, w_out=m_w_out, ln1_g=m_ln1_g, ln1_b=m_ln1_b,
             w_gate_up=m_w_gate_up, w_down=m_w_down, ln2_g=m_ln2_g, ln2_b=m_ln2_b, w_ple=m_w_ple, w_ple_gate=m_w_ple_gate)
    v = dict(w_in=v_w_in, conv_w=v_conv_w, a_log=v_a_log, dt_bias=v_dt_bias, gdn_norm_g=v_gdn_norm_g, q_norm_g=v_q_norm_g,
             w_uq=v_w_uq, kv_norm_g=v_kv_norm_g, w_ukv=v_w_ukv, w_out=v_w_out, ln1_g=v_ln1_g, ln1_b=v_ln1_b,
             w_gate_up=v_w_gate_up, w_down=v_w_down, ln2_g=v_ln2_g, ln2_b=v_ln2_b, w_ple=v_w_ple, w_ple_gate=v_w_ple_gate)
    depth = w_in.shape[0]
    assert depth % 2 == 0
    hd = depth // 2
    dm = _Dims(x.shape[2], N_CHIPS * w_in.shape[2], w_uq.shape[1], w_ukv.shape[1], N_CHIPS * w_gate_up.shape[2], p.shape[3])
    cx, cy, cc = lax.axis_index("x"), lax.axis_index("y"), lax.axis_index("c")
    chip = 2 * cx + cy

    g_streams = [_stream_of(n, w[n].shape[1:]) for n in MATRICES]
    shards = [w[n] if n == "conv_w" else w[n].astype(BF16) for n in MATRICES]
    shards = [_pad_lanes(s) if st.kind == "piece" else s for s, st in zip(shards, g_streams)]
    g_shapes = []
    for s, st in zip(shards, g_streams):
        if st.kind == "piece":
            shape = (N_CHIPS,) + s.shape
        elif st.kind == "rows":
            shape = (depth, N_CHIPS * s.shape[1], s.shape[2])
        else:
            shape = (depth, s.shape[1], N_CHIPS * s.shape[2])
        g_shapes.append(jax.ShapeDtypeStruct(shape, s.dtype))
    mats = dict(zip(MATRICES, _gather_chips(shards, g_streams, g_shapes, name="gather_weights")))
    for n, to_local in (("w_in", dm.w_in_local), ("w_uq", dm.w_uq_local)):
        pieces = jnp.moveaxis(mats[n][..., :w[n].shape[2]], 0, 2)
        mats[n] = to_local(pieces.reshape(pieces.shape[:2] + (-1,)))
    vecs = {n: w[n] for n in VECTORS}

    loss_local, grad_x, gbuf, conv_g, vec_g = _local_step(dm, x[0], p[:, 0], positions[0], loss_target[0], mats, vecs)
    loss = lax.psum(loss_local, ("x", "y", "c"))

    names = list(LOCAL_MATRICES) + ["conv_w", "vectors"]
    gs = [gbuf[n] for n in LOCAL_MATRICES] + [conv_g, _pack_vectors(vec_g, depth)]
    wire = [BF16] * len(LOCAL_MATRICES) + [F32, F32]
    r_streams = [_stream_of(n, w[n].shape[1:]) for n in LOCAL_MATRICES]
    r_streams += [_stream_of("conv_w", w["conv_w"].shape[1:]), _Stream("whole")]
    shard_shapes = [(hd, w[n].shape[1], _lane_padded(w[n].shape[2])) if st.kind == "piece" else (hd,) + w[n].shape[1:]
                    for n, st in zip(LOCAL_MATRICES, r_streams)]
    shard_shapes += [(hd,) + w["conv_w"].shape[1:], (hd, VEC_ROWS, LANES)]
    c_idx = cc.reshape(1).astype(jnp.int32)
    place = (chip.reshape(1).astype(jnp.int32), c_idx)
    from_sibling = _sibling_take_other_half(gs, name="reduce_sibling")
    chip_sum = [_add_own_half(g, a, c_idx, dt, name=f"reduce_add_{n}")
                for g, a, dt, n in zip(gs, from_sibling, wire, names)]
    for i, n in enumerate(names):
        if r_streams[i].kind == "piece":
            glob = dm.w_in_global(chip_sum[i]) if n == "w_in" else dm.w_uq_global(chip_sum[i])
            glob = glob.reshape(glob.shape[:2] + (N_CHIPS, glob.shape[2] // N_CHIPS))
            chip_sum[i] = jnp.moveaxis(_pad_lanes(glob), 2, 0)
    from_chips = _chips_exchange(chip_sum, r_streams, shard_shapes, name="reduce_chips")
    halves = [_sum_chips(ps, got, place, st, name=f"reduce_sum_{n}")
              for ps, got, st, n in zip(chip_sum, from_chips, r_streams, names)]
    joined = dict(zip(names, _sibling_join_halves(halves, name="reduce_join")))
    joined.update(_unpack_vectors(joined.pop("vectors"), {n: w[n].shape for n in VECTORS}))

    grad_w, delta_w, new_m, new_v = {}, {}, {}, {}
    for n in WEIGHTS:
        grad_w[n] = joined[n][..., :w[n].shape[-1]]
        delta_w[n], new_m[n], new_v[n] = _adamw(w[n], grad_w[n], m[n], v[n], name=f"adamw_{n}")
    return (loss, grad_x[None], *[grad_w[n] for n in WEIGHTS], *[delta_w[n] for n in WEIGHTS],
            *[new_m[n] for n in WEIGHTS], *[new_v[n] for n in WEIGHTS])
```

```python
import functools

import jax
import jax.numpy as jnp
from jax import lax
from jax.experimental import pallas as pl
from jax.experimental.pallas import tpu as pltpu

F32 = jnp.float32
BF16 = jnp.bfloat16
HIGH = lax.Precision.HIGH
MESH = pl.DeviceIdType.MESH

CHUNK = 64
N_HEADS = 4
HEAD_DIM = 128
ROPE_DIM = 64
ROPE_THETA = 10000.0
LN_EPS = 1e-5
RMS_EPS = 1e-6
ADAM_LR, ADAM_B1, ADAM_B2, ADAM_EPS, ADAM_WD, ADAM_STEP = 0.001, 0.9, 0.999, 1e-08, 0.01, 10

LANES = 128
VMEM_LIMIT = 48 * 1024 * 1024
ROW_TILE = 512
WIDE_ROW_TILE = 256
WIDE_COLS = 2048
SUB_ROWS = 16
MAX_SUB_ROWS = 64
VREG_FILE_ELEMS = 64 * 8 * LANES

MISC_BETA0 = ROPE_DIM
MISC_A0 = ROPE_DIM + N_HEADS

NN = (((1,), (0,)), ((), ()))
NT = (((1,), (1,)), ((), ()))
TN = (((0,), (0,)), ((), ()))


def _params(sem=None):
    return pltpu.CompilerParams(dimension_semantics=sem, vmem_limit_bytes=VMEM_LIMIT)


def _divisor_tile(dim, target, unit):
    best = None
    t = unit
    while t <= min(dim, target):
        if dim % t == 0:
            best = t
        t += unit
    return best if best is not None else dim


BATCHED = {NN: (((2,), (1,)), ((0,), (0,))), NT: (((2,), (2,)), ((0,), (0,))), TN: (((1,), (1,)), ((0,), (0,)))}


def _make_dots(high_precision):
    def raw(a, b, dims):
        if a.ndim == 3:
            dims = BATCHED[dims]
        if high_precision:
            return lax.dot_general(a, b, dims, precision=HIGH, preferred_element_type=F32)
        return lax.dot_general(a.astype(BF16), b.astype(BF16), dims, preferred_element_type=F32)

    @jax.custom_vjp
    def nn(a, b):
        return raw(a, b, NN)

    @jax.custom_vjp
    def nt(a, b):
        return raw(a, b, NT)

    @jax.custom_vjp
    def tn(a, b):
        return raw(a, b, TN)

    nn.defvjp(lambda a, b: (raw(a, b, NN), (a, b)), lambda r, g: (nt(g, r[1]), tn(r[0], g)))
    nt.defvjp(lambda a, b: (raw(a, b, NT), (a, b)), lambda r, g: (nn(g, r[1]), tn(g, r[0])))
    tn.defvjp(lambda a, b: (raw(a, b, TN), (a, b)), lambda r, g: (nt(r[1], g), nn(r[0], g)))
    return nn, nt, tn


_nn, _nt, _tn = _make_dots(False)
_hnn, _hnt, _htn = _make_dots(True)


def _matmul(a, b, *, dims, name, c=None, out_dtype=F32, tm=1024, tn=1408, tk=1408, layer=None, into=None):
    b_shape = b.shape[-2:]
    if dims == "nn":
        (m, k), (k2, n) = a.shape, b_shape
    elif dims == "nt":
        (m, k), (n, k2) = a.shape, b_shape
    else:
        (k, m), (k2, n) = a.shape, b_shape
    assert k == k2, (a.shape, b.shape, dims)
    tm = _divisor_tile(m, tm, LANES)
    tn = _divisor_tile(n, tn, LANES)
    tk = _divisor_tile(k, tk, LANES)
    nk = k // tk
    dn = {"nn": NN, "nt": NT, "tn": TN}[dims]
    if dims == "tn":
        a_spec = pl.BlockSpec((tk, tm), lambda i, j, kk: (kk, i))
    else:
        a_spec = pl.BlockSpec((tm, tk), lambda i, j, kk: (i, kk))
    b_blk, b_idx = ((tn, tk), lambda i, j, kk: (j, kk)) if dims == "nt" else ((tk, tn), lambda i, j, kk: (kk, j))
    if b.ndim == 3:
        b_spec = pl.BlockSpec((None,) + b_blk, lambda i, j, kk: (layer,) + b_idx(i, j, kk))
    else:
        b_spec = pl.BlockSpec(b_blk, b_idx)
    c_spec = pl.BlockSpec((tm, tn), lambda i, j, kk: (i, j))
    if isinstance(into, int):
        o_spec = pl.BlockSpec((None, tm, tn), lambda i, j, kk: (layer, i, j))
        out_shape = jax.ShapeDtypeStruct((into, m, n), out_dtype)
        into = None
    elif into is not None:
        assert into.shape[1:] == (m, n) and into.dtype == out_dtype
        o_spec = pl.BlockSpec((None, tm, tn), lambda i, j, kk: (layer, i, j))
        out_shape = jax.ShapeDtypeStruct(into.shape, into.dtype)
    else:
        o_spec = c_spec
        out_shape = jax.ShapeDtypeStruct((m, n), out_dtype)
    has_c = c is not None

    def body(*refs):
        a_ref, b_ref = refs[:2]
        c_ref = refs[2] if has_c else None
        o_ref, acc_ref = refs[-2:]
        kk = pl.program_id(2)

        @pl.when(kk == 0)
        def _():
            if has_c:
                acc_ref[...] = c_ref[...].astype(F32)
            else:
                acc_ref[...] = jnp.zeros_like(acc_ref)

        acc_ref[...] += lax.dot_general(a_ref[...].astype(BF16), b_ref[...].astype(BF16), dn,
                                        preferred_element_type=F32)

        @pl.when(kk == nk - 1)
        def _():
            o_ref[...] = acc_ref[...].astype(o_ref.dtype)

    ins = [a, b] + ([c] if has_c else [])
    specs = [a_spec, b_spec] + ([c_spec] if has_c else [])
    aliases = {}
    if into is not None:
        aliases = {len(ins): 0}
        ins.append(into)
        specs.append(pl.BlockSpec(memory_space=pl.ANY))
    return pl.pallas_call(
        body, name=name, grid=(m // tm, n // tn, nk), in_specs=specs, out_specs=o_spec, out_shape=out_shape,
        scratch_shapes=[pltpu.VMEM((tm, tn), F32)], input_output_aliases=aliases,
        compiler_params=_params(("arbitrary", "arbitrary", "arbitrary")),
    )(*ins)


def _rowwise(fn, rows, params, outs, accs=(), *, name):
    t = rows[0][0].shape[0]
    widest = max([w for _, w, _ in rows] + [w for w, _ in outs])
    tm = min(WIDE_ROW_TILE if widest > WIDE_COLS else ROW_TILE, t)
    sub = SUB_ROWS
    while sub < MAX_SUB_ROWS and 2 * sub * widest <= VREG_FILE_ELEMS:
        sub *= 2
    assert t % tm == 0 and tm % sub == 0
    n_rows, n_par, n_out, n_acc = len(rows), len(params), len(outs), len(accs)

    def body(*refs):
        row_refs = refs[:n_rows]
        par_refs = refs[n_rows:n_rows + n_par]
        out_refs = refs[n_rows + n_par:n_rows + n_par + n_out]
        acc_refs = refs[n_rows + n_par + n_out:]
        if n_acc:
            @pl.when(pl.program_id(0) == 0)
            def _():
                for a_ref in acc_refs:
                    a_ref[...] = jnp.zeros_like(a_ref)

        def step(r, carry):
            sl = pl.ds(pl.multiple_of(r * sub, sub), sub)
            vals = [ref[sl, :].astype(F32) for ref in row_refs] + [ref[...] for ref in par_refs]
            res = fn(*vals)
            for o_ref, val in zip(out_refs, res[:n_out]):
                o_ref[sl, :] = val.astype(o_ref.dtype)
            for a_ref, val in zip(acc_refs, res[n_out:]):
                a_ref[...] += val
            return carry

        lax.fori_loop(0, tm // sub, step, 0)

    in_specs = [pl.BlockSpec((tm, w), functools.partial(lambda i, cb: (i, cb), cb=cb)) for _, w, cb in rows]
    in_specs += [pl.BlockSpec(p.shape, lambda i: (0, 0)) for p in params]
    out_specs = [pl.BlockSpec((tm, w), lambda i: (i, 0)) for w, _ in outs]
    out_specs += [pl.BlockSpec(s, lambda i: (0, 0)) for s in accs]
    out_shape = [jax.ShapeDtypeStruct((t, w), d) for w, d in outs]
    out_shape += [jax.ShapeDtypeStruct(s, F32) for s in accs]
    return pl.pallas_call(
        body, name=name, grid=(t // tm,), in_specs=in_specs, out_specs=out_specs, out_shape=out_shape,
        compiler_params=_params(("arbitrary",)),
    )(*[r[0] for r in rows], *params)


def _vjp_fn(fn, n_in, n_out):
    def bwd(*args):
        ins, cts = args[:n_in], args[n_in:]
        _, pull = jax.vjp(fn, *ins)
        return pull(tuple(cts) if n_out > 1 else cts[0])
    return bwd


def _lane(shape):
    return lax.broadcasted_iota(jnp.int32, shape, 1)


def _silu(x):
    return x * jax.nn.sigmoid(x)


def _softplus(x):
    return jnp.maximum(x, 0.0) + jnp.log1p(jnp.exp(-jnp.abs(x)))


def _heads(x, width=HEAD_DIM):
    return [x[:, h * width:(h + 1) * width] for h in range(N_HEADS)]


def _layer_norm(z, g, b):
    mu = jnp.mean(z, -1, keepdims=True)
    zc = z - mu
    var = jnp.mean(zc * zc, -1, keepdims=True)
    return zc * lax.rsqrt(var + LN_EPS) * g + b


def _gdn_act(u, misc, alog_row, dtb_row):
    s = _silu(u)
    w = N_HEADS * HEAD_DIM
    q = jnp.concatenate([t * lax.rsqrt(jnp.sum(t * t, -1, keepdims=True) + RMS_EPS) * HEAD_DIM ** -0.5
                         for t in _heads(s[:, :w])], axis=1)
    k = jnp.concatenate([t * lax.rsqrt(jnp.sum(t * t, -1, keepdims=True) + RMS_EPS)
                         for t in _heads(s[:, w:2 * w])], axis=1)
    v = s[:, 2 * w:]
    lane = _lane(misc.shape)
    beta = jax.nn.sigmoid(misc)
    g = -jnp.exp(alog_row) * _softplus(misc + dtb_row)
    is_beta = (lane >= MISC_BETA0) & (lane < MISC_BETA0 + N_HEADS)
    is_g = (lane >= MISC_A0) & (lane < MISC_A0 + N_HEADS)
    gb = jnp.where(is_beta, beta, jnp.where(is_g, g, 0.0))
    return q, k, v, gb


def _gdn_out(o, z, gn_row):
    outs = []
    for oh, zh in zip(_heads(o), _heads(z)):
        r = oh * lax.rsqrt(jnp.mean(oh * oh, -1, keepdims=True) + RMS_EPS) * gn_row
        outs.append(r * _silu(zh))
    return jnp.concatenate(outs, axis=1)


def _mla_norm(ckv, cq, kvg_row, qg_row):
    cqn = cq * lax.rsqrt(jnp.mean(cq * cq, -1, keepdims=True) + RMS_EPS) * qg_row
    ckvn = ckv * lax.rsqrt(jnp.mean(ckv * ckv, -1, keepdims=True) + RMS_EPS) * kvg_row
    return cqn, ckvn


def _swap_halves(x):
    half = ROPE_DIM // 2
    return jnp.where(_lane(x.shape) < half, pltpu.roll(x, LANES - half, 1), pltpu.roll(x, half, 1))


@jax.custom_vjp
def _rope(x, cos_t, sin_t):
    return x * cos_t + _swap_halves(x) * sin_t


def _rope_fwd(x, cos_t, sin_t):
    return _rope(x, cos_t, sin_t), (cos_t, sin_t)


def _rope_bwd(res, g):
    cos_t, sin_t = res
    return g * cos_t - _swap_halves(g) * sin_t, jnp.zeros_like(cos_t), jnp.zeros_like(sin_t)


_rope.defvjp(_rope_fwd, _rope_bwd)


def _mla_qk(scale, qm, kv, misc, cos_t, sin_t):
    krope = _rope(misc, cos_t, sin_t)
    qs, ks = [], []
    for h in range(N_HEADS):
        base = 2 * HEAD_DIM * h
        qs += [qm[:, base:base + HEAD_DIM], _rope(qm[:, base + HEAD_DIM:base + 2 * HEAD_DIM], cos_t, sin_t)]
        ks += [kv[:, HEAD_DIM * h:HEAD_DIM * (h + 1)], krope]
    return jnp.concatenate(qs, axis=1) * scale, jnp.concatenate(ks, axis=1), kv[:, N_HEADS * HEAD_DIM:]


def _swiglu(gu):
    f = gu.shape[1] // 2
    return _silu(gu[:, :f]) * gu[:, f:]


def _ple_out(x2, pg, pe):
    return x2 + jax.nn.sigmoid(pg) * pe


CONV_W = 4
HALO = 8
CONV_STRIP = 512


def _conv_fwd(h, conv_w, width, *, name, tm=ROW_TILE, sub=32):
    t = h.shape[0]
    tm = min(tm, t)
    nb = tm // HALO

    def body(x_ref, halo_ref, w_ref, u_ref, buf):
        i = pl.program_id(0)
        buf[pl.ds(0, HALO), :] = jnp.where(i > 0, halo_ref[...], 0.0)
        buf[pl.ds(HALO, tm), :] = x_ref[...]
        for c0 in range(0, width, CONV_STRIP):
            cols = pl.ds(c0, CONV_STRIP)
            w = w_ref[:, cols]
            for r0 in range(0, tm, sub):
                acc = jnp.zeros((sub, CONV_STRIP), F32)
                for j in range(CONV_W):
                    acc = acc + w[j:j + 1, :] * buf[pl.ds(HALO + r0 - (CONV_W - 1) + j, sub), cols]
                u_ref[pl.ds(r0, sub), cols] = acc

    return pl.pallas_call(
        body, name=name, grid=(t // tm,),
        in_specs=[pl.BlockSpec((tm, width), lambda i: (i, 0)),
                  pl.BlockSpec((HALO, width), lambda i: (jnp.maximum(i * nb - 1, 0), 0)),
                  pl.BlockSpec(conv_w.shape, lambda i: (0, 0))],
        out_specs=pl.BlockSpec((tm, width), lambda i: (i, 0)),
        out_shape=jax.ShapeDtypeStruct((t, width), F32),
        scratch_shapes=[pltpu.VMEM((tm + HALO, width), F32)],
        compiler_params=_params(("arbitrary",)),
    )(h, h, conv_w)


def _conv_bwd(du, h, conv_w, width, *, name, tm=ROW_TILE, sub=32):
    t = h.shape[0]
    tm = min(tm, t)
    nb = tm // HALO
    n_tiles = t // tm

    def body(du_ref, du_halo, x_ref, x_halo, w_ref, dx_ref, dw_ref, dbuf, xbuf):
        i = pl.program_id(0)

        @pl.when(i == 0)
        def _():
            dw_ref[...] = jnp.zeros_like(dw_ref)

        dbuf[pl.ds(0, tm), :] = du_ref[...]
        dbuf[pl.ds(tm, HALO), :] = jnp.where(i < n_tiles - 1, du_halo[...], 0.0)
        xbuf[pl.ds(0, HALO), :] = jnp.where(i > 0, x_halo[...], 0.0)
        xbuf[pl.ds(HALO, tm), :] = x_ref[...]
        for c0 in range(0, width, CONV_STRIP):
            cols = pl.ds(c0, CONV_STRIP)
            w = w_ref[:, cols]
            dws = [jnp.zeros((HALO, CONV_STRIP), F32) for _ in range(CONV_W)]
            for r0 in range(0, tm, sub):
                acc = jnp.zeros((sub, CONV_STRIP), F32)
                d_here = dbuf[pl.ds(r0, sub), cols]
                for j in range(CONV_W):
                    acc = acc + w[j:j + 1, :] * dbuf[pl.ds(r0 + (CONV_W - 1) - j, sub), cols]
                    prod = d_here * xbuf[pl.ds(HALO + r0 - (CONV_W - 1) + j, sub), cols]
                    for g0 in range(0, sub, HALO):
                        dws[j] = dws[j] + prod[g0:g0 + HALO, :]
                dx_ref[pl.ds(r0, sub), cols] = acc.astype(dx_ref.dtype)
            for j in range(CONV_W):
                dw_ref[pl.ds(j, 1), cols] += jnp.sum(dws[j], axis=0, keepdims=True)

    return pl.pallas_call(
        body, name=name, grid=(n_tiles,),
        in_specs=[pl.BlockSpec((tm, width), lambda i: (i, 0)),
                  pl.BlockSpec((HALO, width), lambda i: (jnp.minimum((i + 1) * nb, t // HALO - 1), 0)),
                  pl.BlockSpec((tm, width), lambda i: (i, 0)),
                  pl.BlockSpec((HALO, width), lambda i: (jnp.maximum(i * nb - 1, 0), 0)),
                  pl.BlockSpec(conv_w.shape, lambda i: (0, 0))],
        out_specs=[pl.BlockSpec((tm, width), lambda i: (i, 0)),
                   pl.BlockSpec((HALO, width), lambda i: (0, 0))],
        out_shape=[jax.ShapeDtypeStruct((t, width), BF16), jax.ShapeDtypeStruct((HALO, width), F32)],
        scratch_shapes=[pltpu.VMEM((tm + HALO, width), F32), pltpu.VMEM((tm + HALO, width), F32)],
        compiler_params=_params(("arbitrary",)),
    )(du, du, h, h, conv_w)


@jax.custom_vjp
def _inv_unit_lower(low):
    n = low.shape[-1]
    eye = (lax.broadcasted_iota(jnp.int32, (n, n), 0) == lax.broadcasted_iota(jnp.int32, (n, n), 1)).astype(F32)
    x = eye - low
    p = low
    span = 2
    while span < n:
        p = _hnn(p, p)
        x = x + _hnn(x, p)
        span *= 2
    return x


def _inv_fwd(low):
    x = _inv_unit_lower(low)
    return x, x


def _inv_bwd(x, g):
    return (-_htn(x, _hnt(g, x)),)


_inv_unit_lower.defvjp(_inv_fwd, _inv_bwd)


@jax.custom_vjp
def _inv_known(low, inverse):
    return inverse


_inv_known.defvjp(lambda low, inverse: (inverse, inverse), lambda x, g: (_inv_bwd(x, g)[0], jnp.zeros_like(x)))


def _gdn_prep(q, k, v, gb, known_inverse=None):
    c = CHUNK
    n = q.shape[0] // c
    pairs = [(g, h) for g in range(n) for h in range(N_HEADS)]
    row = lax.broadcasted_iota(jnp.int32, (c, c), 0)
    col = lax.broadcasted_iota(jnp.int32, (c, c), 1)
    tri_incl = row >= col
    tri_strict = row > col
    lane = _lane((c, LANES))
    sub = lax.broadcasted_iota(jnp.int32, (LANES, c), 0)
    last = lax.broadcasted_iota(jnp.int32, (c, 1), 0) == c - 1

    def split(x):
        return jnp.stack([x[g * c:(g + 1) * c, h * HEAD_DIM:(h + 1) * HEAD_DIM] for g, h in pairs])

    gbs = [gb[g * c:(g + 1) * c, :] for g in range(n)]
    gbts = [x.T for x in gbs]
    g_col = jnp.stack([jnp.sum(jnp.where(lane == MISC_A0 + h, gbs[g], 0.0), axis=1, keepdims=True) for g, h in pairs])
    b_col = jnp.stack([jnp.sum(jnp.where(lane == MISC_BETA0 + h, gbs[g], 0.0), axis=1, keepdims=True) for g, h in pairs])
    g_row = jnp.stack([jnp.sum(jnp.where(sub == MISC_A0 + h, gbts[g], 0.0), axis=0, keepdims=True) for g, h in pairs])
    gc_col = jnp.sum(jnp.where(tri_incl, g_row, 0.0), axis=2, keepdims=True)
    gc_row = jnp.sum(jnp.where(row <= col, g_col, 0.0), axis=1, keepdims=True)
    decay = jnp.where(tri_incl, jnp.exp(jnp.where(tri_incl, gc_col - gc_row, 0.0)), 0.0)
    g_last = jnp.sum(jnp.where(last, gc_col, 0.0), axis=1, keepdims=True)
    qs, ks, vs = split(q), split(k), split(v)
    kb = ks * b_col
    low = jnp.where(tri_strict, _nt(kb, ks) * decay, 0.0)
    if known_inverse is None:
        tinv = _inv_unit_lower(low)
    else:
        tinv = _inv_known(low, jnp.stack([known_inverse[g * c:(g + 1) * c, h * c:(h + 1) * c] for g, h in pairs]))
    eg = jnp.exp(gc_col)
    sol = _hnn(tinv, jnp.concatenate([vs * b_col, kb * eg], axis=2))
    attn = jnp.where(tri_incl, _nt(qs, ks) * decay, 0.0)
    qd = qs * eg
    kd = ks * jnp.exp(g_last - gc_col)

    def merge(x):
        return jnp.concatenate([jnp.concatenate([x[g * N_HEADS + h] for h in range(N_HEADS)], axis=1)
                                for g in range(n)], axis=0)

    glb = jnp.concatenate([sum(jnp.where(lane == h, g_last[g * N_HEADS + h], 0.0) for h in range(N_HEADS))
                           for g in range(n)], axis=0)
    outs = (merge(sol[:, :, :HEAD_DIM]), merge(sol[:, :, HEAD_DIM:]), merge(qd), merge(kd), merge(attn), glb)
    return outs, merge(tinv)


def _gdn_seq(state, u, w, qd, kd, attn, glb):
    c = u.shape[0]
    first = lax.broadcasted_iota(jnp.int32, glb.shape, 0) == 0
    lane = _lane(glb.shape)
    heads = lambda x: jnp.stack([x[:, h * HEAD_DIM:(h + 1) * HEAD_DIM] for h in range(N_HEADS)])
    g_last = jnp.stack([jnp.sum(jnp.sum(jnp.where(first & (lane == h), glb, 0.0), axis=1, keepdims=True),
                                axis=0, keepdims=True) for h in range(N_HEADS)])
    s = jnp.stack([state[h * HEAD_DIM:(h + 1) * HEAD_DIM, :] for h in range(N_HEADS)])
    at = jnp.stack([attn[:, h * c:(h + 1) * c] for h in range(N_HEADS)])
    v_new = heads(u) - _nn(heads(w), s)
    o = _nn(heads(qd), s) + _nn(at, v_new)
    s_new = s * jnp.exp(g_last) + _tn(heads(kd), v_new)
    return (jnp.concatenate([o[h] for h in range(N_HEADS)], axis=1),
            jnp.concatenate([s_new[h] for h in range(N_HEADS)], axis=0))


PREP_CHUNKS = 2
SEQ_CHUNKS = 8


def _gdn_prep_fwd(q, k, v, gb, *, name):
    t, w = q.shape
    rows = min(PREP_CHUNKS * CHUNK, t)

    def body(q_ref, k_ref, v_ref, gb_ref, *out_refs):
        outs, inverse = _gdn_prep(q_ref[...], k_ref[...], v_ref[...], gb_ref[...])
        for o_ref, val in zip(out_refs, outs + (inverse,)):
            o_ref[...] = val

    spec = lambda width: pl.BlockSpec((rows, width), lambda i: (i, 0))
    widths = [w, w, w, w, N_HEADS * CHUNK, LANES, N_HEADS * CHUNK]
    res = pl.pallas_call(
        body, name=name, grid=(t // rows,),
        in_specs=[spec(w), spec(w), spec(w), spec(LANES)],
        out_specs=[spec(x) for x in widths],
        out_shape=[jax.ShapeDtypeStruct((t, x), F32) for x in widths],
        compiler_params=_params(("arbitrary",)),
    )(q, k, v, gb)
    return tuple(res[:6]), res[6]


def _gdn_prep_bwd(q, k, v, gb, inverse, cts, *, name):
    t, w = q.shape
    rows = min(PREP_CHUNKS * CHUNK, t)

    def body(q_ref, k_ref, v_ref, gb_ref, inv_ref, du, dw, dqd, dkd, dattn, dglb, dq_ref, dk_ref, dv_ref, dgb_ref):
        known = inv_ref[...]
        _, pull = jax.vjp(lambda a, b, c_, d_: _gdn_prep(a, b, c_, d_, known)[0],
                          q_ref[...], k_ref[...], v_ref[...], gb_ref[...])
        dq, dk, dv, dgb = pull(tuple(r[...] for r in (du, dw, dqd, dkd, dattn, dglb)))
        dq_ref[...] = dq
        dk_ref[...] = dk
        dv_ref[...] = dv
        dgb_ref[...] = dgb

    spec = lambda width: pl.BlockSpec((rows, width), lambda i: (i, 0))
    widths = [w, w, w, w, N_HEADS * CHUNK, LANES]
    return pl.pallas_call(
        body, name=name, grid=(t // rows,),
        in_specs=[spec(w), spec(w), spec(w), spec(LANES), spec(N_HEADS * CHUNK)] + [spec(x) for x in widths],
        out_specs=[spec(w), spec(w), spec(w), spec(LANES)],
        out_shape=[jax.ShapeDtypeStruct((t, w), F32)] * 3 + [jax.ShapeDtypeStruct((t, LANES), F32)],
        compiler_params=_params(("arbitrary",)),
    )(q, k, v, gb, inverse, *cts)


def _gdn_seq_fwd(prep, *, name):
    t, w = prep[0].shape
    rows = min(SEQ_CHUNKS * CHUNK, t)
    per = rows // CHUNK

    def body(u_ref, w_ref, qd_ref, kd_ref, at_ref, gl_ref, o_ref, sall_ref, s_scr):
        @pl.when(pl.program_id(0) == 0)
        def _():
            s_scr[...] = jnp.zeros_like(s_scr)

        def step(j, carry):
            sl = pl.ds(pl.multiple_of(j * CHUNK, CHUNK), CHUNK)
            s = s_scr[...]
            sall_ref[j] = s
            o, s_new = _gdn_seq(s, u_ref[sl, :], w_ref[sl, :], qd_ref[sl, :], kd_ref[sl, :], at_ref[sl, :], gl_ref[sl, :])
            o_ref[sl, :] = o
            s_scr[...] = s_new
            return carry

        lax.fori_loop(0, per, step, 0)

    spec = lambda width: pl.BlockSpec((rows, width), lambda i: (i, 0))
    widths = [w, w, w, w, N_HEADS * CHUNK, LANES]
    return pl.pallas_call(
        body, name=name, grid=(t // rows,),
        in_specs=[spec(x) for x in widths],
        out_specs=[spec(w), pl.BlockSpec((per, w, HEAD_DIM), lambda i: (i, 0, 0))],
        out_shape=[jax.ShapeDtypeStruct((t, w), F32), jax.ShapeDtypeStruct((t // CHUNK, w, HEAD_DIM), F32)],
        scratch_shapes=[pltpu.VMEM((w, HEAD_DIM), F32)],
        compiler_params=_params(("arbitrary",)),
    )(*prep)


def _gdn_seq_bwd(prep, s_all, do, *, name):
    t, w = prep[0].shape
    rows = min(SEQ_CHUNKS * CHUNK, t)
    per = rows // CHUNK
    n = t // rows

    def body(u_ref, w_ref, qd_ref, kd_ref, at_ref, gl_ref, sall_ref, do_ref, du, dw, dqd, dkd, dat, dgl, ds_scr):
        @pl.when(pl.program_id(0) == 0)
        def _():
            ds_scr[...] = jnp.zeros_like(ds_scr)

        def step(jj, carry):
            j = per - 1 - jj
            sl = pl.ds(pl.multiple_of(j * CHUNK, CHUNK), CHUNK)
            _, pull = jax.vjp(_gdn_seq, sall_ref[j], u_ref[sl, :], w_ref[sl, :], qd_ref[sl, :], kd_ref[sl, :],
                              at_ref[sl, :], gl_ref[sl, :])
            res = pull((do_ref[sl, :], ds_scr[...]))
            ds_scr[...] = res[0]
            for o_ref, val in zip((du, dw, dqd, dkd, dat, dgl), res[1:]):
                o_ref[sl, :] = val
            return carry

        lax.fori_loop(0, per, step, 0)

    spec = lambda width: pl.BlockSpec((rows, width), lambda i: (n - 1 - i, 0))
    widths = [w, w, w, w, N_HEADS * CHUNK, LANES]
    return pl.pallas_call(
        body, name=name, grid=(n,),
        in_specs=[spec(x) for x in widths] + [pl.BlockSpec((per, w, HEAD_DIM), lambda i: (n - 1 - i, 0, 0)), spec(w)],
        out_specs=[spec(x) for x in widths],
        out_shape=[jax.ShapeDtypeStruct((t, x), F32) for x in widths],
        scratch_shapes=[pltpu.VMEM((w, HEAD_DIM), F32)],
        compiler_params=_params(("arbitrary",)),
    )(*prep, s_all, do)


QK_DIM = 2 * HEAD_DIM
ATT_TILE = 1024
NEG = -1e30


ATT_SPLIT = 4


def _chunk_mask(n_rows, n_cols, key_major, query_offset):
    r = lax.broadcasted_iota(jnp.int32, (n_rows, n_cols), 0)
    c = lax.broadcasted_iota(jnp.int32, (n_rows, n_cols), 1)
    if key_major:
        return r // CHUNK <= (c + query_offset) // CHUNK
    return c // CHUNK <= (r + query_offset) // CHUNK


def _visible_keys(tile, diagonal):
    hq = tile // ATT_SPLIT
    return [(a + 1) * hq if diagonal else tile for a in range(ATT_SPLIT)]


def _dot_nn(a, b):
    return lax.dot_general(a, b, NN, preferred_element_type=F32)


def _blocked_transpose(x, width):
    t = x.shape[0]
    tile = min(ATT_TILE, t)
    return x.reshape(t // tile, tile, N_HEADS, width).transpose(2, 0, 3, 1)


def _attn_fwd(q, kt, v1, *, name):
    t = q.shape[0]
    tq = min(ATT_TILE, t)
    nq = t // tq

    def body(q_ref, kt_ref, v_ref, o_ref, lse_ref, m_scr, acc_scr):
        qi = pl.program_id(1)
        m_scr[...] = jnp.full_like(m_scr, NEG)
        acc_scr[...] = jnp.zeros_like(acc_scr)
        hq = tq // ATT_SPLIT
        parts = [pl.ds(a * hq, hq) for a in range(ATT_SPLIT)]
        qs = [q_ref[sl, :] for sl in parts]

        def step(kj, masked):
            rows = pl.ds(pl.multiple_of(kj * tq, tq), tq)
            kt_blk, vv = kt_ref[kj], v_ref[rows, :]
            seen = _visible_keys(tq, masked)
            ss = [_dot_nn(qv, kt_blk[:, :w]) for qv, w in zip(qs, seen)]
            for a, sl in enumerate(parts):
                s = ss[a]
                if masked:
                    s = jnp.where(_chunk_mask(hq, seen[a], False, a * hq), s, NEG)
                m_old = m_scr[sl, :]
                m_new = jnp.maximum(m_old, jnp.max(s, axis=1, keepdims=True))
                p = jnp.exp(s - m_new)
                acc_scr[sl, :] = jnp.exp(m_old - m_new) * acc_scr[sl, :] + _dot_nn(p.astype(BF16), vv[:seen[a], :])
                m_scr[sl, :] = m_new

        def loop_body(kj, carry):
            step(kj, False)
            return carry

        lax.fori_loop(0, qi, loop_body, 0)
        step(qi, True)
        acc = acc_scr[...]
        o_ref[...] = (acc[:, :HEAD_DIM] / acc[:, HEAD_DIM:]).astype(o_ref.dtype)
        lse_ref[...] = m_scr[...] + jnp.log(acc[:, HEAD_DIM:HEAD_DIM + 1])

    return pl.pallas_call(
        body, name=name, grid=(N_HEADS, nq),
        in_specs=[pl.BlockSpec((tq, QK_DIM), lambda h, i: (i, h)),
                  pl.BlockSpec((None, nq, QK_DIM, tq), lambda h, i: (h, 0, 0, 0)),
                  pl.BlockSpec((t, 2 * HEAD_DIM), lambda h, i: (0, h))],
        out_specs=[pl.BlockSpec((tq, HEAD_DIM), lambda h, i: (i, h)),
                   pl.BlockSpec((None, tq, 1), lambda h, i: (h, i, 0))],
        out_shape=[jax.ShapeDtypeStruct((t, N_HEADS * HEAD_DIM), BF16),
                   jax.ShapeDtypeStruct((N_HEADS, t, 1), F32)],
        scratch_shapes=[pltpu.VMEM((tq, 1), F32), pltpu.VMEM((tq, 2 * HEAD_DIM), F32)],
        compiler_params=_params(("arbitrary", "arbitrary")),
    )(q, kt, v1)


def _attn_bwd_dq(q, k, kt, vt, o, lse, dom, do_col0, *, name):
    t = q.shape[0]
    tq = min(ATT_TILE, t)
    nq = t // tq

    def body(q_ref, k_ref, kt_ref, vt_ref, o_ref, lse_ref, do_ref, dq_ref, delta_ref, acc_scr):
        qi = pl.program_id(1)
        acc_scr[...] = jnp.zeros_like(acc_scr)
        do = do_ref[...]
        delta = jnp.sum(do * o_ref[...].astype(F32), axis=1, keepdims=True)
        delta_ref[...] = delta
        hq = tq // ATT_SPLIT
        parts = [pl.ds(a * hq, hq) for a in range(ATT_SPLIT)]
        qs = [q_ref[sl, :] for sl in parts]
        dos = [do_ref[sl, :].astype(BF16) for sl in parts]
        lses = [lse_ref[sl, :] for sl in parts]
        deltas = [delta[a * hq:(a + 1) * hq, :] for a in range(ATT_SPLIT)]

        def step(kj, masked):
            rows = pl.ds(pl.multiple_of(kj * tq, tq), tq)
            kt_blk, vt_blk = kt_ref[kj], vt_ref[kj]
            seen = _visible_keys(tq, masked)
            ss = [_dot_nn(qv, kt_blk[:, :w]) for qv, w in zip(qs, seen)]
            dps = [_dot_nn(do_b, vt_blk[:, :w]) for do_b, w in zip(dos, seen)]
            kv_ = k_ref[rows, :]
            for a, sl in enumerate(parts):
                p = jnp.exp(ss[a] - lses[a])
                if masked:
                    p = jnp.where(_chunk_mask(hq, seen[a], False, a * hq), p, 0.0)
                ds = p * (dps[a] - deltas[a])
                acc_scr[sl, :] += _dot_nn(ds.astype(BF16), kv_[:seen[a], :])

        def loop_body(kj, carry):
            step(kj, False)
            return carry

        lax.fori_loop(0, qi, loop_body, 0)
        step(qi, True)
        dq_ref[...] = acc_scr[...].astype(dq_ref.dtype)

    return pl.pallas_call(
        body, name=name, grid=(N_HEADS, nq),
        in_specs=[pl.BlockSpec((tq, QK_DIM), lambda h, i: (i, h)),
                  pl.BlockSpec((t, QK_DIM), lambda h, i: (0, h)),
                  pl.BlockSpec((None, nq, QK_DIM, tq), lambda h, i: (h, 0, 0, 0)),
                  pl.BlockSpec((None, nq, HEAD_DIM, tq), lambda h, i: (h, 0, 0, 0)),
                  pl.BlockSpec((tq, HEAD_DIM), lambda h, i: (i, h)),
                  pl.BlockSpec((None, tq, 1), lambda h, i: (h, i, 0)),
                  pl.BlockSpec((tq, HEAD_DIM), lambda h, i: (i, do_col0 + h))],
        out_specs=[pl.BlockSpec((tq, QK_DIM), lambda h, i: (i, h)),
                   pl.BlockSpec((None, tq, 1), lambda h, i: (h, i, 0))],
        out_shape=[jax.ShapeDtypeStruct((t, N_HEADS * QK_DIM), BF16),
                   jax.ShapeDtypeStruct((N_HEADS, t, 1), F32)],
        scratch_shapes=[pltpu.VMEM((tq, QK_DIM), F32)],
        compiler_params=_params(("arbitrary", "arbitrary")),
    )(q, k, kt, vt, o, lse, dom)


def _attn_bwd_dkv(q, qt, k, v, lse_row, delta_row, dom, dot, do_col0, *, name):
    t = q.shape[0]
    tk = min(ATT_TILE, t)
    nk = t // tk

    def body(q_ref, qt_ref, k_ref, v_ref, lse_ref, delta_ref, do_ref, dot_ref, dk_ref, dv_ref, dk_scr, dv_scr):
        kj = pl.program_id(1)
        dk_scr[...] = jnp.zeros_like(dk_scr)
        dv_scr[...] = jnp.zeros_like(dv_scr)
        kv_ = k_ref[...]
        vv = v_ref[...]

        hq = tk // ATT_SPLIT

        def step(qi, masked):
            lse_v, delta_v = lse_ref[qi], delta_ref[qi]
            qt_blk, dot_blk = qt_ref[qi], dot_ref[qi]
            qs, dos = [], []
            for a in range(ATT_SPLIT):
                rows = pl.ds(pl.multiple_of(qi * tk + a * hq, hq), hq)
                qs.append(q_ref[rows, :])
                dos.append(do_ref[rows, :].astype(BF16))
            seen = _visible_keys(tk, masked)
            ss = [_dot_nn(kv_[:seen[a], :], qt_blk[:, a * hq:(a + 1) * hq]) for a in range(ATT_SPLIT)]
            dps = [_dot_nn(vv[:seen[a], :], dot_blk[:, a * hq:(a + 1) * hq]) for a in range(ATT_SPLIT)]
            for a in range(ATT_SPLIT):
                cols = slice(a * hq, (a + 1) * hq)
                keys = pl.ds(0, seen[a])
                p = jnp.exp(ss[a] - lse_v[:, cols])
                if masked:
                    p = jnp.where(_chunk_mask(seen[a], hq, True, a * hq), p, 0.0)
                dv_scr[keys, :] += _dot_nn(p.astype(BF16), dos[a])
                ds = p * (dps[a] - delta_v[:, cols])
                dk_scr[keys, :] += _dot_nn(ds.astype(BF16), qs[a])

        step(kj, True)

        def loop_body(qi, carry):
            step(qi, False)
            return carry

        lax.fori_loop(kj + 1, nk, loop_body, 0)
        dk_ref[...] = dk_scr[...].astype(dk_ref.dtype)
        dv_ref[...] = dv_scr[...].astype(dv_ref.dtype)

    stat = pl.BlockSpec((None, nk, 1, tk), lambda h, j: (h, 0, 0, 0))
    return pl.pallas_call(
        body, name=name, grid=(N_HEADS, nk),
        in_specs=[pl.BlockSpec((t, QK_DIM), lambda h, j: (0, h)),
                  pl.BlockSpec((None, nk, QK_DIM, tk), lambda h, j: (h, 0, 0, 0)),
                  pl.BlockSpec((tk, QK_DIM), lambda h, j: (j, h)),
                  pl.BlockSpec((tk, HEAD_DIM), lambda h, j: (j, h)),
                  stat, stat,
                  pl.BlockSpec((t, HEAD_DIM), lambda h, j: (0, do_col0 + h)),
                  pl.BlockSpec((None, nk, HEAD_DIM, tk), lambda h, j: (h, 0, 0, 0))],
        out_specs=[pl.BlockSpec((tk, QK_DIM), lambda h, j: (j, h)),
                   pl.BlockSpec((tk, HEAD_DIM), lambda h, j: (j, h))],
        out_shape=[jax.ShapeDtypeStruct((t, N_HEADS * QK_DIM), BF16),
                   jax.ShapeDtypeStruct((t, N_HEADS * HEAD_DIM), BF16)],
        scratch_shapes=[pltpu.VMEM((tk, QK_DIM), F32), pltpu.VMEM((tk, HEAD_DIM), F32)],
        compiler_params=_params(("arbitrary", "arbitrary")),
    )(q, qt, k, v, lse_row, delta_row, dom, dot)


def _rope_tables(pos_col, inv_freq_row, *, name):
    t = pos_col.shape[0]
    tm = min(ROW_TILE, t)

    def body(p_ref, f_ref, c_ref, s_ref):
        ang = p_ref[...].astype(F32) * f_ref[...]
        lane = _lane(ang.shape)
        c_ref[...] = jnp.where(lane < ROPE_DIM, jnp.cos(ang), 0.0)
        sn = jnp.sin(ang)
        s_ref[...] = jnp.where(lane < ROPE_DIM // 2, -sn, jnp.where(lane < ROPE_DIM, sn, 0.0))

    out = pl.BlockSpec((tm, LANES), lambda i: (i, 0))
    return pl.pallas_call(
        body, name=name, grid=(t // tm,),
        in_specs=[pl.BlockSpec((tm, 1), lambda i: (i, 0)), pl.BlockSpec((1, LANES), lambda i: (0, 0))],
        out_specs=[out, out], out_shape=[jax.ShapeDtypeStruct((t, LANES), F32)] * 2,
        compiler_params=_params(("arbitrary",)),
    )(pos_col, inv_freq_row)


def _loss_head(y, target):
    width = y.shape[1]

    def fn(yv, tv):
        e = yv - tv
        part = 0.5 * jnp.sum(jnp.mean(e * e, axis=1, keepdims=True), axis=0, keepdims=True)
        return e * (1.0 / width), jnp.broadcast_to(part, (HALO, LANES))

    return _rowwise(fn, [(y, width, 0), (target, width, 0)], [], [(width, F32)], [(HALO, LANES)], name="loss_head")


def _adamw(w, g, m, v, *, name):
    shape = w.shape
    w2, g2, m2, v2 = (a.reshape(-1, shape[-1]) for a in (w, g, m, v))
    rows, width = w2.shape
    tr = _divisor_tile(rows, max(8, (1 << 19) // max(width, 1)), 8)
    bc1 = 1.0 - ADAM_B1 ** ADAM_STEP
    bc2 = 1.0 - ADAM_B2 ** ADAM_STEP

    def body(w_ref, g_ref, m_ref, v_ref, d_ref, mo_ref, vo_ref):
        gv = g_ref[...]
        mn = ADAM_B1 * m_ref[...] + (1.0 - ADAM_B1) * gv
        vn = ADAM_B2 * v_ref[...] + (1.0 - ADAM_B2) * (gv * gv)
        d_ref[...] = -ADAM_LR * ((mn / bc1) / (jnp.sqrt(vn / bc2) + ADAM_EPS) + ADAM_WD * w_ref[...])
        mo_ref[...] = mn
        vo_ref[...] = vn

    spec = pl.BlockSpec((tr, width), lambda i: (i, 0))
    outs = pl.pallas_call(
        body, name=name, grid=(rows // tr,), in_specs=[spec] * 4, out_specs=[spec] * 3,
        out_shape=[jax.ShapeDtypeStruct((rows, width), F32)] * 3,
        compiler_params=_params(("arbitrary",)),
    )(w2, g2, m2, v2)
    return tuple(o.reshape(shape) for o in outs)


HBM_SPEC = pl.BlockSpec(memory_space=pltpu.HBM)


def _position():
    return lax.axis_index("x"), lax.axis_index("y"), lax.axis_index("c")


def _other_chips(x, y):
    return [(1 - x, y), (x, 1 - y), (1 - x, 1 - y)]


class _Stream:
    def __init__(self, kind, size=0):
        self.kind, self.size = kind, size
        self.parts = 2 if kind == "heads" else 1

    def local(self, ref, k, part):
        if self.kind == "rows":
            return ref.at[:, pl.ds(k * self.size, self.size), :]
        if self.kind == "cols":
            return ref.at[:, :, pl.ds(k * self.size, self.size)]
        if self.kind == "heads":
            return ref.at[:, :, pl.ds(part * N_HEADS * HEAD_DIM + k * HEAD_DIM, HEAD_DIM)]
        if self.kind == "piece":
            return ref.at[k]
        return ref

    def shard(self, ref, part):
        if self.kind == "heads":
            return ref.at[:, :, pl.ds(part * HEAD_DIM, HEAD_DIM)]
        return ref

    def half_local(self, ref, k, part, cc, hd):
        if self.kind == "piece":
            return ref.at[k, pl.ds(cc * hd, hd)]
        return self.local(ref.at[pl.ds(cc * hd, hd)], k, part)


def _remote(src, dst, send_sems, recv_sems, idx, to):
    return pltpu.make_async_remote_copy(src_ref=src, dst_ref=dst, send_sem=send_sems.at[idx],
                                        recv_sem=recv_sems.at[idx], device_id=to, device_id_type=MESH)


def _comm_call(body, ins, out_shapes, n_remote, n_local, *, name):
    scratch = [pltpu.SemaphoreType.DMA((n_remote,)), pltpu.SemaphoreType.DMA((n_remote,))]
    if n_local:
        scratch.append(pltpu.SemaphoreType.DMA((n_local,)))
    return pl.pallas_call(
        body, name=name, in_specs=[HBM_SPEC] * len(ins), out_specs=[HBM_SPEC] * len(out_shapes), out_shape=out_shapes,
        scratch_shapes=scratch, compiler_params=pltpu.CompilerParams(has_side_effects=True),
    )(*ins)


def _gather_chips(shards, streams, out_shapes, *, name):
    n = len(shards)
    hd = shards[0].shape[0] // 2
    flat = [(t, part) for t in range(n) for part in range(streams[t].parts)]
    ns = len(flat)

    def body(*refs):
        s_refs, o_refs = refs[:n], refs[n:2 * n]
        send_sems, recv_sems = refs[2 * n:]
        x, y, c = _position()
        sibling = (x, y, 1 - c)
        chips = _other_chips(x, y)
        me = 2 * x + y
        sent = []
        for s, (t, part) in enumerate(flat):
            st = streams[t]
            sent.append(_remote(st.shard(s_refs[t], part), st.local(o_refs[t], me, part), send_sems, recv_sems,
                                6 * ns + s, sibling))
            sent[-1].start()
            src = st.shard(s_refs[t].at[pl.ds(c * hd, hd)], part)
            for j, (cx, cy) in enumerate(chips):
                sent.append(_remote(src, st.half_local(o_refs[t], me, part, c, hd), send_sems, recv_sems,
                                    3 * s + j, (cx, cy, c)))
                sent[-1].start()
        for s, (t, part) in enumerate(flat):
            st = streams[t]
            for j, (cx, cy) in enumerate(chips):
                blk = st.half_local(o_refs[t], 2 * cx + cy, part, c, hd)
                _remote(blk, blk, send_sems, recv_sems, 3 * s + j, (x, y, c)).wait_recv()
                sent.append(_remote(blk, blk, send_sems, recv_sems, 3 * ns + 3 * s + j, sibling))
                sent[-1].start()
        for s, (t, part) in enumerate(flat):
            st = streams[t]
            for j, (cx, cy) in enumerate(chips):
                blk = st.half_local(o_refs[t], 2 * cx + cy, part, 1 - c, hd)
                _remote(blk, blk, send_sems, recv_sems, 3 * ns + 3 * s + j, (x, y, c)).wait_recv()
            own = st.local(o_refs[t], me, part)
            _remote(own, own, send_sems, recv_sems, 6 * ns + s, (x, y, c)).wait_recv()
        for cp in sent:
            cp.wait_send()

    return _comm_call(body, shards, out_shapes, 7 * ns, 0, name=name)


def _sibling_take_other_half(gs, *, name):
    n = len(gs)
    hd = gs[0].shape[0] // 2

    def body(*refs):
        g_refs, o_refs = refs[:n], refs[n:2 * n]
        send_sems, recv_sems = refs[2 * n:]
        x, y, c = _position()
        copies = [_remote(g_refs[t].at[pl.ds((1 - c) * hd, hd)], o_refs[t], send_sems, recv_sems, t, (x, y, 1 - c))
                  for t in range(n)]
        for cp in copies:
            cp.start()
        for cp in copies:
            cp.wait()

    outs = [jax.ShapeDtypeStruct((hd,) + g.shape[1:], g.dtype) for g in gs]
    return _comm_call(body, gs, outs, n, 0, name=name)


def _chips_exchange(ps, streams, shard_shapes, *, name):
    n = len(ps)
    flat = [(t, part) for t in range(n) for part in range(streams[t].parts)]

    def body(*refs):
        p_refs, o_refs = refs[:n], refs[n:2 * n]
        send_sems, recv_sems = refs[2 * n:]
        x, y, c = _position()
        copies = []
        for s, (t, part) in enumerate(flat):
            st = streams[t]
            for j, (cx, cy) in enumerate(_other_chips(x, y)):
                copies.append(_remote(st.local(p_refs[t], 2 * cx + cy, part), st.shard(o_refs[t].at[j], part),
                                      send_sems, recv_sems, 3 * s + j, (cx, cy, c)))
        for cp in copies:
            cp.start()
        for cp in copies:
            cp.wait()

    outs = [jax.ShapeDtypeStruct((3,) + tuple(shp), p.dtype) for p, shp in zip(ps, shard_shapes)]
    return _comm_call(body, ps, outs, 3 * len(flat), 0, name=name)


def _sibling_join_halves(bufs, *, name):
    n = len(bufs)
    hd = bufs[0].shape[0] // 2

    def body(*refs):
        o_refs = refs[n:2 * n]
        send_sems, recv_sems = refs[2 * n:]
        x, y, c = _position()
        sent = []
        for t in range(n):
            mine = o_refs[t].at[pl.ds(c * hd, hd)]
            sent.append(_remote(mine, mine, send_sems, recv_sems, t, (x, y, 1 - c)))
            sent[-1].start()
        for t in range(n):
            theirs = o_refs[t].at[pl.ds((1 - c) * hd, hd)]
            _remote(theirs, theirs, send_sems, recv_sems, t, (x, y, c)).wait_recv()
        for cp in sent:
            cp.wait_send()

    return pl.pallas_call(
        body, name=name, in_specs=[HBM_SPEC] * n, out_specs=[HBM_SPEC] * n,
        out_shape=[jax.ShapeDtypeStruct(b.shape, b.dtype) for b in bufs],
        scratch_shapes=[pltpu.SemaphoreType.DMA((n,)), pltpu.SemaphoreType.DMA((n,))],
        input_output_aliases={t: t for t in range(n)},
        compiler_params=pltpu.CompilerParams(has_side_effects=True),
    )(*bufs)


def _row_tile(rows, width):
    return _divisor_tile(rows, max(16, (1 << 19) // width), 16)


def _add_own_half(g, got, c_idx, out_dtype, *, name):
    hd, r, w = got.shape
    tr = _row_tile(r, w)

    def body(c_ref, g_ref, a_ref, o_ref):
        o_ref[...] = (g_ref[...] + a_ref[...]).astype(o_ref.dtype)

    return pl.pallas_call(
        body, name=name,
        grid_spec=pltpu.PrefetchScalarGridSpec(
            num_scalar_prefetch=1, grid=(hd, r // tr),
            in_specs=[pl.BlockSpec((None, None, tr, w), lambda l, i, c_ref: (c_ref[0], l, i, 0)),
                      pl.BlockSpec((None, tr, w), lambda l, i, c_ref: (l, i, 0))],
            out_specs=pl.BlockSpec((None, tr, w), lambda l, i, c_ref: (l, i, 0))),
        out_shape=jax.ShapeDtypeStruct((hd, r, w), out_dtype),
        compiler_params=_params(("arbitrary", "arbitrary")),
    )(c_idx, g.reshape((2, hd) + g.shape[1:]), got)


def _sum_chips(p, got, place, stream, *, name):
    _, hd, rs, cs = got.shape
    wb = HEAD_DIM if stream.kind == "heads" else cs
    tr = _row_tile(rs, wb)
    kind, size = stream.kind, stream.size

    def own_index(l, i, g, k_ref, c_ref):
        k = k_ref[0]
        if kind == "rows":
            return (l, k * (size // tr) + i, 0)
        if kind == "cols":
            return (l, i, k)
        if kind == "heads":
            return (l, i, g * N_HEADS + k)
        if kind == "piece":
            return (k, l, i, 0)
        return (l, i, 0)

    own_blk = (None, None, tr, wb) if kind == "piece" else (None, tr, wb)

    def body(k_ref, c_ref, p_ref, fx_ref, fy_ref, fxy_ref, o_ref):
        f = lambda r: r[...].astype(F32)
        o_ref[...] = (f(p_ref) + f(fy_ref)) + (f(fx_ref) + f(fxy_ref))

    def rel(j):
        return pl.BlockSpec((None, None, tr, wb), functools.partial(lambda l, i, g, k_ref, c_ref, j: (j, l, i, g), j=j))

    return pl.pallas_call(
        body, name=name,
        grid_spec=pltpu.PrefetchScalarGridSpec(
            num_scalar_prefetch=2, grid=(hd, rs // tr, stream.parts),
            in_specs=[pl.BlockSpec(own_blk, own_index), rel(0), rel(1), rel(2)],
            out_specs=pl.BlockSpec((None, tr, wb), lambda l, i, g, k_ref, c_ref: (c_ref[0] * hd + l, i, g))),
        out_shape=jax.ShapeDtypeStruct((2 * hd, rs, cs), F32),
        compiler_params=_params(("arbitrary", "arbitrary", "arbitrary")),
    )(place[0], place[1], p, got, got, got)


MATRICES = ("w_in", "w_uq", "w_ukv", "w_out", "w_gate_up", "w_down", "w_ple", "w_ple_gate", "conv_w")
VECTORS = ("a_log", "dt_bias", "gdn_norm_g", "q_norm_g", "kv_norm_g", "ln1_g", "ln1_b", "ln2_g", "ln2_b")
WEIGHTS = ("w_in", "conv_w", "a_log", "dt_bias", "gdn_norm_g", "q_norm_g", "w_uq", "kv_norm_g", "w_ukv", "w_out",
           "ln1_g", "ln1_b", "w_gate_up", "w_down", "ln2_g", "ln2_b", "w_ple", "w_ple_gate")
ROW_SHARDED = ("w_out", "w_down", "w_ple_gate")
N_CHIPS = 4


def _stream_of(name, shard_shape):
    if name in ("w_in", "w_uq"):
        return _Stream("piece")
    if name == "w_ukv":
        return _Stream("heads")
    if name in ROW_SHARDED:
        return _Stream("rows", shard_shape[0])
    return _Stream("cols", shard_shape[1])


def _pack_vectors(vecs, depth):
    flat = jnp.concatenate([vecs[n].reshape(depth, -1) for n in VECTORS], axis=1)
    pad = jnp.zeros((depth, VEC_ROWS * LANES - flat.shape[1]), F32)
    return jnp.concatenate([flat, pad], axis=1).reshape(depth, VEC_ROWS, LANES)


def _unpack_vectors(packed, shapes):
    depth = packed.shape[0]
    flat = packed.reshape(depth, VEC_ROWS * LANES)
    out, off = {}, 0
    for n in VECTORS:
        out[n] = flat[:, off:off + shapes[n][1]]
        off += shapes[n][1]
    return out


VEC_ROWS = 40


class _Dims:
    def __init__(self, d_model, in_width, q_lora, kv_lora, d_ff2, ple_dim):
        self.d = d_model
        self.hw = N_HEADS * HEAD_DIM
        self.in_width = in_width
        self.q_lora, self.kv_lora = q_lora, kv_lora
        self.ff2 = d_ff2
        self.ple = ple_dim
        self.c_kv0 = 4 * self.hw
        self.c_q0 = self.c_kv0 + kv_lora
        self.misc0 = self.c_q0 + q_lora
        self.h_width = self.misc0 + LANES
        assert self.c_kv0 % kv_lora == 0 and self.c_q0 % q_lora == 0 and self.misc0 % LANES == 0
        self.g_beta = 4 * self.hw
        self.g_a = self.g_beta + N_HEADS
        self.g_cq = self.g_a + N_HEADS
        self.g_ckv = self.g_cq + q_lora
        self.g_kr = self.g_ckv + kv_lora
        assert self.g_kr + ROPE_DIM == in_width

    def w_in_local(self, w):
        pad = jnp.zeros(w.shape[:-1] + (self.h_width - self.in_width,), w.dtype)
        return jnp.concatenate([w[..., :self.g_beta], w[..., self.g_ckv:self.g_kr], w[..., self.g_cq:self.g_ckv],
                                w[..., self.g_kr:], w[..., self.g_beta:self.g_cq], pad], axis=-1)

    def w_in_global(self, d):
        m = self.misc0
        return jnp.concatenate([d[..., :self.c_kv0], d[..., m + MISC_BETA0:m + MISC_A0 + N_HEADS],
                                d[..., self.c_q0:self.misc0], d[..., self.c_kv0:self.c_q0], d[..., m:m + ROPE_DIM]],
                               axis=-1)

    def w_uq_local(self, w):
        r = w.reshape(w.shape[:-1] + (N_HEADS, HEAD_DIM + ROPE_DIM))
        r = jnp.pad(r, [(0, 0)] * (r.ndim - 1) + [(0, QK_DIM - HEAD_DIM - ROPE_DIM)])
        return r.reshape(w.shape[:-1] + (N_HEADS * QK_DIM,))

    def w_uq_global(self, d):
        r = d.reshape(d.shape[:-1] + (N_HEADS, QK_DIM))[..., :HEAD_DIM + ROPE_DIM]
        return r.reshape(d.shape[:-1] + (N_HEADS * (HEAD_DIM + ROPE_DIM),))


def _lane_padded(n):
    return -(-n // LANES) * LANES


def _pad_lanes(a):
    pad = _lane_padded(a.shape[-1]) - a.shape[-1]
    return a if pad == 0 else jnp.pad(a, [(0, 0)] * (a.ndim - 1) + [(0, pad)])


def _lane_row(vec, lane0):
    pad = LANES - lane0 - vec.shape[0]
    return jnp.concatenate([jnp.zeros((lane0,), F32), vec.astype(F32), jnp.zeros((pad,), F32)])[None, :]


def _layer_fwd(dm, alpha, x, xb, p_i, cos_t, sin_t, wl, tag):
    d, hw = dm.d, dm.hw
    nm = lambda s: f"{s}_{tag}"
    mm = functools.partial(_matmul, layer=wl["layer"])
    h = mm(xb, wl["w_in"], dims="nn", name=nm("f_in"))
    misc_cb = dm.misc0 // LANES

    u = _conv_fwd(h, wl["conv_w"], 3 * hw, name=nm("f_conv"))
    qn, kn, vg, gb = _rowwise(_gdn_act, [(u, 3 * hw, 0), (h, LANES, misc_cb)], [wl["alog_row"], wl["dtb_row"]],
                              [(hw, F32), (hw, F32), (hw, F32), (LANES, F32)], name=nm("f_gdn_act"))
    prep, tinv = _gdn_prep_fwd(qn, kn, vg, gb, name=nm("f_gdn_prep"))
    o_gdn, s_all = _gdn_seq_fwd(prep, name=nm("f_gdn_seq"))
    (og,) = _rowwise(lambda o, z, g: (_gdn_out(o, z, g),), [(o_gdn, hw, 0), (h, hw, 3)], [wl["gn_row"]],
                     [(hw, BF16)], name=nm("f_gdn_out"))

    cqn, ckvn = _rowwise(_mla_norm, [(h, dm.kv_lora, dm.c_kv0 // dm.kv_lora), (h, dm.q_lora, dm.c_q0 // dm.q_lora)],
                         [wl["kvg_row"], wl["qg_row"]], [(dm.q_lora, BF16), (dm.kv_lora, BF16)], name=nm("f_mla_norm"))
    qm = mm(cqn, wl["w_uq"], dims="nn", name=nm("f_uq"))
    kvm = mm(ckvn, wl["w_ukv"], dims="nn", name=nm("f_ukv"))
    scale = (HEAD_DIM + ROPE_DIM) ** -0.5
    qk_fn = functools.partial(_mla_qk, scale)
    qa, ka, va = _rowwise(qk_fn, [(qm, N_HEADS * QK_DIM, 0), (kvm, 2 * hw, 0), (h, LANES, misc_cb),
                                  (cos_t, LANES, 0), (sin_t, LANES, 0)], [],
                          [(N_HEADS * QK_DIM, BF16), (N_HEADS * QK_DIM, BF16), (hw, BF16)], name=nm("f_mla_qk"))
    kt = _blocked_transpose(ka, QK_DIM)
    v_heads = va.reshape(va.shape[0], N_HEADS, HEAD_DIM)
    v1 = jnp.concatenate([v_heads, jnp.ones_like(v_heads)], axis=2).reshape(va.shape[0], 2 * hw)
    o_mla, lse = _attn_fwd(qa, kt, v1, name=nm("f_attn"))

    om = jnp.concatenate([og, o_mla], axis=1)
    mix = mm(om, wl["w_out"], dims="nn", name=nm("f_out"))
    ln1 = lambda xv, yv, g, b: (_layer_norm(alpha * xv + yv, g, b),) * 2
    x1, x1b = _rowwise(ln1, [(x, d, 0), (mix, d, 0)], [wl["ln1_g"], wl["ln1_b"]], [(d, F32), (d, BF16)], name=nm("f_ln1"))

    gu = mm(x1b, wl["w_gate_up"], dims="nn", name=nm("f_gate_up"), out_dtype=BF16)
    (act,) = _rowwise(lambda g_: (_swiglu(g_),), [(gu, dm.ff2, 0)], [], [(dm.ff2 // 2, BF16)], name=nm("f_swiglu"))
    dn = mm(act, wl["w_down"], dims="nn", name=nm("f_down"))
    x2, x2b = _rowwise(ln1, [(x1, d, 0), (dn, d, 0)], [wl["ln2_g"], wl["ln2_b"]], [(d, F32), (d, BF16)], name=nm("f_ln2"))

    pg = mm(x2b, wl["w_ple_gate"], dims="nn", name=nm("f_ple_gate"))
    pe = mm(p_i, wl["w_ple"], dims="nn", name=nm("f_ple"))
    out, outb = _rowwise(lambda a, b, c_: (_ple_out(a, b, c_),) * 2, [(x2, d, 0), (pg, d, 0), (pe, d, 0)], [],
                         [(d, F32), (d, BF16)], name=nm("f_ple_out"))
    saved = dict(x=x, xb=xb, p_i=p_i, h=h, u=u, qn=qn, kn=kn, vg=vg, gb=gb, prep=prep, tinv=tinv, s_all=s_all, o_gdn=o_gdn, cqn=cqn, ckvn=ckvn,
                 qm=qm, kvm=kvm, qa=qa, ka=ka, kt=kt, va=va, o_mla=o_mla, lse=lse, om=om, mix=mix, x1=x1, x1b=x1b, gu=gu,
                 act=act, dn=dn, x2=x2, x2b=x2b, pg=pg, pe=pe)
    return out, outb, saved


def _layer_bwd(dm, alpha, dout, sv, cos_t, sin_t, wl, gbuf, tag):
    d, hw = dm.d, dm.hw
    t = dout.shape[0]
    nm = lambda s: f"{s}_{tag}"
    gr = {}
    gbuf = dict(gbuf)
    misc_cb = dm.misc0 // LANES
    mm = functools.partial(_matmul, layer=wl["layer"])

    def wgrad(name_, a, g):
        gbuf[name_] = mm(a, g, dims="tn", name=nm("b_" + name_), into=gbuf[name_], tm=1408, tn=1408, tk=1024)

    dx2_a, dpg, dpe = _rowwise(_vjp_fn(_ple_out, 3, 1), [(sv["x2"], d, 0), (sv["pg"], d, 0), (sv["pe"], d, 0), (dout, d, 0)],
                               [], [(d, F32), (d, BF16), (d, BF16)], name=nm("b_ple_out"))
    wgrad("w_ple", sv["p_i"], dpe)
    wgrad("w_ple_gate", sv["x2b"], dpg)
    dx2 = mm(dpg, wl["w_ple_gate"], dims="nt", c=dx2_a, name=nm("b_x2"))

    def ln_bwd(xv, yv, ct, g, b):
        _, pull = jax.vjp(lambda a_, b_, c_, d_: _layer_norm(alpha * a_ + b_, c_, d_), xv, yv, g, b)
        return pull(ct)

    dx1_a, ddn, gr["ln2_g"], gr["ln2_b"] = _rowwise(
        ln_bwd, [(sv["x1"], d, 0), (sv["dn"], d, 0), (dx2, d, 0)], [wl["ln2_g"], wl["ln2_b"]],
        [(d, F32), (d, BF16)], [(1, d), (1, d)], name=nm("b_ln2"))
    wgrad("w_down", sv["act"], ddn)
    dact = mm(ddn, wl["w_down"], dims="nt", name=nm("b_act"), out_dtype=BF16)
    (dgu,) = _rowwise(_vjp_fn(_swiglu, 1, 1), [(sv["gu"], dm.ff2, 0), (dact, dm.ff2 // 2, 0)], [], [(dm.ff2, BF16)],
                      name=nm("b_swiglu"))
    wgrad("w_gate_up", sv["x1b"], dgu)
    dx1 = mm(dgu, wl["w_gate_up"], dims="nt", c=dx1_a, name=nm("b_x1"))

    dx_a, dmix, gr["ln1_g"], gr["ln1_b"] = _rowwise(
        ln_bwd, [(sv["x"], d, 0), (sv["mix"], d, 0), (dx1, d, 0)], [wl["ln1_g"], wl["ln1_b"]],
        [(d, F32), (d, BF16)], [(1, d), (1, d)], name=nm("b_ln1"))
    wgrad("w_out", sv["om"], dmix)
    dom = mm(dmix, wl["w_out"], dims="nt", name=nm("b_om"))

    nq = t // min(ATT_TILE, t)
    dqa, delta = _attn_bwd_dq(sv["qa"], sv["ka"], sv["kt"], _blocked_transpose(sv["va"], HEAD_DIM), sv["o_mla"], sv["lse"],
                              dom, hw // HEAD_DIM, name=nm("b_attn_dq"))
    lse_row = sv["lse"].reshape(N_HEADS, nq, 1, t // nq)
    delta_row = delta.reshape(N_HEADS, nq, 1, t // nq)
    dot = _blocked_transpose(dom[:, hw:].astype(BF16), HEAD_DIM)
    dka, dva = _attn_bwd_dkv(sv["qa"], _blocked_transpose(sv["qa"], QK_DIM), sv["ka"], sv["va"], lse_row, delta_row, dom, dot,
                             hw // HEAD_DIM, name=nm("b_attn_dkv"))
    scale = (HEAD_DIM + ROPE_DIM) ** -0.5
    qk_fn = functools.partial(_mla_qk, scale)

    def qk_bwd(qm, kvm, misc, cs, sn, g_q, g_k, g_v):
        _, pull = jax.vjp(lambda a, b, c_: qk_fn(a, b, c_, cs, sn), qm, kvm, misc)
        return pull((g_q, g_k, g_v))

    dqm, dkvm, dmisc_rope = _rowwise(
        qk_bwd, [(sv["qm"], N_HEADS * QK_DIM, 0), (sv["kvm"], 2 * hw, 0), (sv["h"], LANES, misc_cb), (cos_t, LANES, 0),
                 (sin_t, LANES, 0), (dqa, N_HEADS * QK_DIM, 0), (dka, N_HEADS * QK_DIM, 0), (dva, hw, 0)], [],
        [(N_HEADS * QK_DIM, BF16), (2 * hw, BF16), (LANES, F32)], name=nm("b_mla_qk"))
    wgrad("w_uq", sv["cqn"], dqm)
    wgrad("w_ukv", sv["ckvn"], dkvm)
    dcqn = mm(dqm, wl["w_uq"], dims="nt", name=nm("b_cqn"))
    dckvn = mm(dkvm, wl["w_ukv"], dims="nt", name=nm("b_ckvn"))

    def norm_bwd(ckv, cq, g_q, g_kv, kvg, qg):
        _, pull = jax.vjp(_mla_norm, ckv, cq, kvg, qg)
        return pull((g_q, g_kv))

    dckv, dcq, gr["kvg_row"], gr["qg_row"] = _rowwise(
        norm_bwd, [(sv["h"], dm.kv_lora, dm.c_kv0 // dm.kv_lora), (sv["h"], dm.q_lora, dm.c_q0 // dm.q_lora),
                   (dcqn, dm.q_lora, 0), (dckvn, dm.kv_lora, 0)], [wl["kvg_row"], wl["qg_row"]],
        [(dm.kv_lora, BF16), (dm.q_lora, BF16)], [(1, dm.kv_lora), (1, dm.q_lora)], name=nm("b_mla_norm"))

    def gout_bwd(o, z, g_o, gn):
        _, pull = jax.vjp(_gdn_out, o, z, gn)
        return pull(g_o)

    do_gdn, dz, gr["gn_row"] = _rowwise(gout_bwd, [(sv["o_gdn"], hw, 0), (sv["h"], hw, 3), (dom, hw, 0)], [wl["gn_row"]],
                                        [(hw, F32), (hw, BF16)], [(1, HEAD_DIM)], name=nm("b_gdn_out"))
    dprep = _gdn_seq_bwd(sv["prep"], sv["s_all"], do_gdn, name=nm("b_gdn_seq"))
    dqn, dkn, dvg, dgb = _gdn_prep_bwd(sv["qn"], sv["kn"], sv["vg"], sv["gb"], sv["tinv"], dprep, name=nm("b_gdn_prep"))

    def act_bwd(u, misc, g_q, g_k, g_v, g_gb, g_rope, alog, dtb):
        _, pull = jax.vjp(_gdn_act, u, misc, alog, dtb)
        du_, dmisc_, dalog_, ddtb_ = pull((g_q, g_k, g_v, g_gb))
        return du_, dmisc_ + g_rope, dalog_, ddtb_

    du, dmisc, gr["alog_row"], gr["dtb_row"] = _rowwise(
        act_bwd, [(sv["u"], 3 * hw, 0), (sv["h"], LANES, misc_cb), (dqn, hw, 0), (dkn, hw, 0), (dvg, hw, 0),
                  (dgb, LANES, 0), (dmisc_rope, LANES, 0)], [wl["alog_row"], wl["dtb_row"]],
        [(3 * hw, F32), (LANES, BF16)], [(1, LANES), (1, LANES)], name=nm("b_gdn_act"))
    dqkv, dconv = _conv_bwd(du, sv["h"], wl["conv_w"], 3 * hw, name=nm("b_conv"))
    gr["conv_w"] = dconv[:CONV_W]

    dh = jnp.concatenate([dqkv, dz, dckv, dcq, dmisc], axis=1)
    wgrad("w_in", sv["xb"], dh)
    dx = mm(dh, wl["w_in"], dims="nt", c=dx_a, name=nm("b_x"), tk=1408)
    return dx, gbuf, gr


LOCAL_MATRICES = ("w_in", "w_uq", "w_ukv", "w_out", "w_gate_up", "w_down", "w_ple", "w_ple_gate")


def _layer_weights(mats, vecs, layer):
    wl = {n: mats[n] for n in LOCAL_MATRICES}
    wl["layer"] = layer
    wl["conv_w"] = mats["conv_w"][layer]
    wl["alog_row"] = _lane_row(vecs["a_log"][layer], MISC_A0)
    wl["dtb_row"] = _lane_row(vecs["dt_bias"][layer], MISC_A0)
    wl["gn_row"] = vecs["gdn_norm_g"][layer][None, :]
    wl["qg_row"] = vecs["q_norm_g"][layer][None, :]
    wl["kvg_row"] = vecs["kv_norm_g"][layer][None, :]
    for n in ("ln1_g", "ln1_b", "ln2_g", "ln2_b"):
        wl[n] = vecs[n][layer][None, :]
    return wl


def _vector_grads(gr):
    out = {"a_log": gr["alog_row"][0, MISC_A0:MISC_A0 + N_HEADS], "dt_bias": gr["dtb_row"][0, MISC_A0:MISC_A0 + N_HEADS],
           "gdn_norm_g": gr["gn_row"][0], "q_norm_g": gr["qg_row"][0], "kv_norm_g": gr["kvg_row"][0]}
    for n in ("ln1_g", "ln1_b", "ln2_g", "ln2_b"):
        out[n] = gr[n][0]
    return out


def _local_step(dm, x, p, positions, target, mats, vecs):
    depth = p.shape[0]
    alpha = (2.0 * depth) ** 0.25
    freq = ROPE_THETA ** (-jnp.arange(0, ROPE_DIM, 2, dtype=F32) / ROPE_DIM)
    inv_freq_row = _lane_row(jnp.concatenate([freq, freq]), 0)
    cos_t, sin_t = _rope_tables(positions.reshape(-1, 1), inv_freq_row, name="rope_tables")

    wls = [_layer_weights(mats, vecs, i) for i in range(depth)]
    saved = []
    cur, cur_b = x, x
    for i in range(depth):
        cur, cur_b, sv = _layer_fwd(dm, alpha, cur, cur_b, p[i], cos_t, sin_t, wls[i], f"l{i}")
        saved.append(sv)
    dy, loss_blk = _loss_head(cur, target)
    gbuf = {n: depth for n in LOCAL_MATRICES}
    conv_g, vec_g = [None] * depth, [None] * depth
    for i in reversed(range(depth)):
        dy, gbuf, gr = _layer_bwd(dm, alpha, dy, saved[i], cos_t, sin_t, wls[i], gbuf, f"l{i}")
        conv_g[i] = gr["conv_w"]
        vec_g[i] = _vector_grads(gr)
    vec_grads = {n: jnp.stack([vec_g[i][n] for i in range(depth)]) for n in VECTORS}
    return loss_blk[0, 0], dy, gbuf, jnp.stack(conv_g), vec_grads


def kernel(x, p, positions, w_in, conv_w, a_log, dt_bias, gdn_norm_g, q_norm_g, w_uq, kv_norm_g, w_ukv, w_out, ln1_g, ln1_b, w_gate_up, w_down, ln2_g, ln2_b, w_ple, w_ple_gate, loss_target, m_w_in, m_conv_w, m_a_log, m_dt_bias, m_gdn_norm_g, m_q_norm_g, m_w_uq, m_kv_norm_g, m_w_ukv, m_w_out, m_ln1_g, m_ln1_b, m_w_gate_up, m_w_down, m_ln2_g, m_ln2_b, m_w_ple, m_w_ple_gate, v_w_in, v_conv_w, v_a_log, v_dt_bias, v_gdn_norm_g, v_q_norm_g, v_w_uq, v_kv_norm_g, v_w_ukv, v_w_out, v_ln1_g, v_ln1_b, v_w_gate_up, v_w_down, v_ln2_g, v_ln2_b, v_w_ple, v_w_ple_gate):
    w = dict(w_in=w_in, conv_w=conv_w, a_log=a_log, dt_bias=dt_bias, gdn_norm_g=gdn_norm_g, q_norm_g=q_norm_g, w_uq=w_uq,
             kv_norm_g=kv_norm_g, w_ukv=w_ukv, w_out=w_out, ln1_g=ln1_g, ln1_b=ln1_b, w_gate_up=w_gate_up, w_down=w_down,
             ln2_g=ln2_g, ln2_b=ln2_b, w_ple=w_ple, w_ple_gate=w_ple_gate)
    m = dict(w_in=m_w_in, conv_w=m_conv_w, a_log=m_a_log, dt_bias=m_dt_bias, gdn_norm_g=m_gdn_norm_g, q_norm_g=m_q_norm_g,
             w_uq=m_w_uq, kv_norm_g=m_kv_norm_g, w_ukv=m_w_ukv, w_out=m_w_out, ln1_g=m_ln1_g, ln1_b=m_ln1_b,
             w_gate_up=m_w_gate_up, w_down=m_w_down, ln2_g=m_ln2_g, ln2_b=m_ln2_b, w_ple=m_w_ple, w_ple_gate=m_w_ple_gate)
    v = dict(w_in=v_w_in, conv_w=v_conv_w, a_log=v_a_log, dt_bias=v_dt_bias, gdn_norm_g=v_gdn_norm_g, q_norm_g=v_q_norm_g,
             w_uq=v_w_uq, kv_norm_g=v_kv_norm_g, w_ukv=v_w_ukv, w_out=v_w_out, ln1_g=v_ln1_g, ln1_b=v_ln1_b,
             w_gate_up=v_w_gate_up, w_down=v_w_down, ln2_g=v_ln2_g, ln2_b=v_ln2_b, w_ple=v_w_ple, w_ple_gate=v_w_ple_gate)
    depth = w_in.shape[0]
    assert depth % 2 == 0
    hd = depth // 2
    dm = _Dims(x.shape[2], N_CHIPS * w_in.shape[2], w_uq.shape[1], w_ukv.shape[1], N_CHIPS * w_gate_up.shape[2], p.shape[3])
    cx, cy, cc = lax.axis_index("x"), lax.axis_index("y"), lax.axis_index("c")
    chip = 2 * cx + cy

    g_streams = [_stream_of(n, w[n].shape[1:]) for n in MATRICES]
    shards = [w[n] if n == "conv_w" else w[n].astype(BF16) for n in MATRICES]
    shards = [_pad_lanes(s) if st.kind == "piece" else s for s, st in zip(shards, g_streams)]
    g_shapes = []
    for s, st in zip(shards, g_streams):
        if st.kind == "piece":
            shape = (N_CHIPS,) + s.shape
        elif st.kind == "rows":
            shape = (depth, N_CHIPS * s.shape[1], s.shape[2])
        else:
            shape = (depth, s.shape[1], N_CHIPS * s.shape[2])
        g_shapes.append(jax.ShapeDtypeStruct(shape, s.dtype))
    mats = dict(zip(MATRICES, _gather_chips(shards, g_streams, g_shapes, name="gather_weights")))
    for n, to_local in (("w_in", dm.w_in_local), ("w_uq", dm.w_uq_local)):
        pieces = jnp.moveaxis(mats[n][..., :w[n].shape[2]], 0, 2)
        mats[n] = to_local(pieces.reshape(pieces.shape[:2] + (-1,)))
    vecs = {n: w[n] for n in VECTORS}

    loss_local, grad_x, gbuf, conv_g, vec_g = _local_step(dm, x[0], p[:, 0], positions[0], loss_target[0], mats, vecs)
    loss = lax.psum(loss_local, ("x", "y", "c"))

    names = list(LOCAL_MATRICES) + ["conv_w", "vectors"]
    gs = [gbuf[n] for n in LOCAL_MATRICES] + [conv_g, _pack_vectors(vec_g, depth)]
    wire = [BF16] * len(LOCAL_MATRICES) + [F32, F32]
    r_streams = [_stream_of(n, w[n].shape[1:]) for n in LOCAL_MATRICES]
    r_streams += [_stream_of("conv_w", w["conv_w"].shape[1:]), _Stream("whole")]
    shard_shapes = [(hd, w[n].shape[1], _lane_padded(w[n].shape[2])) if st.kind == "piece" else (hd,) + w[n].shape[1:]
                    for n, st in zip(LOCAL_MATRICES, r_streams)]
    shard_shapes += [(hd,) + w["conv_w"].shape[1:], (hd, VEC_ROWS, LANES)]
    c_idx = cc.reshape(1).astype(jnp.int32)
    place = (chip.reshape(1).astype(jnp.int32), c_idx)
    from_sibling = _sibling_take_other_half(gs, name="reduce_sibling")
    chip_sum = [_add_own_half(g, a, c_idx, dt, name=f"reduce_add_{n}")
                for g, a, dt, n in zip(gs, from_sibling, wire, names)]
    for i, n in enumerate(names):
        if r_streams[i].kind == "piece":
            glob = dm.w_in_global(chip_sum[i]) if n == "w_in" else dm.w_uq_global(chip_sum[i])
            glob = glob.reshape(glob.shape[:2] + (N_CHIPS, glob.shape[2] // N_CHIPS))
            chip_sum[i] = jnp.moveaxis(_pad_lanes(glob), 2, 0)
    from_chips = _chips_exchange(chip_sum, r_streams, shard_shapes, name="reduce_chips")
    halves = [_sum_chips(ps, got, place, st, name=f"reduce_sum_{n}")
              for ps, got, st, n in zip(chip_sum, from_chips, r_streams, names)]
    joined = dict(zip(names, _sibling_join_halves(halves, name="reduce_join")))
    joined.update(_unpack_vectors(joined.pop("vectors"), {n: w[n].shape for n in VECTORS}))

    grad_w, delta_w, new_m, new_v = {}, {}, {}, {}
    for n in WEIGHTS:
        grad_w[n] = joined[n][..., :w[n].shape[-1]]
        delta_w[n], new_m[n], new_v[n] = _adamw(w[n], grad_w[n], m[n], v[n], name=f"adamw_{n}")
    return (loss, grad_x[None], *[grad_w[n] for n in WEIGHTS], *[delta_w[n] for n in WEIGHTS],
            *[new_m[n] for n in WEIGHTS], *[new_v[n] for n in WEIGHTS])
```

```python
import functools

import jax
import jax.numpy as jnp
from jax import lax
from jax.experimental import pallas as pl
from jax.experimental.pallas import tpu as pltpu

F32 = jnp.float32
BF16 = jnp.bfloat16
HIGH = lax.Precision.HIGH
MESH = pl.DeviceIdType.MESH

CHUNK = 64
N_HEADS = 4
HEAD_DIM = 128
ROPE_DIM = 64
ROPE_THETA = 10000.0
LN_EPS = 1e-5
RMS_EPS = 1e-6
ADAM_LR, ADAM_B1, ADAM_B2, ADAM_EPS, ADAM_WD, ADAM_STEP = 0.001, 0.9, 0.999, 1e-08, 0.01, 10

LANES = 128
VMEM_LIMIT = 48 * 1024 * 1024
ROW_TILE = 512
WIDE_ROW_TILE = 256
WIDE_COLS = 2048
SUB_ROWS = 16
MAX_SUB_ROWS = 64
VREG_FILE_ELEMS = 64 * 8 * LANES

MISC_BETA0 = ROPE_DIM
MISC_A0 = ROPE_DIM + N_HEADS

NN = (((1,), (0,)), ((), ()))
NT = (((1,), (1,)), ((), ()))
TN = (((0,), (0,)), ((), ()))


def _params(sem=None):
    return pltpu.CompilerParams(dimension_semantics=sem, vmem_limit_bytes=VMEM_LIMIT)


def _divisor_tile(dim, target, unit):
    best = None
    t = unit
    while t <= min(dim, target):
        if dim % t == 0:
            best = t
        t += unit
    return best if best is not None else dim


BATCHED = {NN: (((2,), (1,)), ((0,), (0,))), NT: (((2,), (2,)), ((0,), (0,))), TN: (((1,), (1,)), ((0,), (0,)))}


def _make_dots(high_precision):
    def raw(a, b, dims):
        if a.ndim == 3:
            dims = BATCHED[dims]
        if high_precision:
            return lax.dot_general(a, b, dims, precision=HIGH, preferred_element_type=F32)
        return lax.dot_general(a.astype(BF16), b.astype(BF16), dims, preferred_element_type=F32)

    @jax.custom_vjp
    def nn(a, b):
        return raw(a, b, NN)

    @jax.custom_vjp
    def nt(a, b):
        return raw(a, b, NT)

    @jax.custom_vjp
    def tn(a, b):
        return raw(a, b, TN)

    nn.defvjp(lambda a, b: (raw(a, b, NN), (a, b)), lambda r, g: (nt(g, r[1]), tn(r[0], g)))
    nt.defvjp(lambda a, b: (raw(a, b, NT), (a, b)), lambda r, g: (nn(g, r[1]), tn(g, r[0])))
    tn.defvjp(lambda a, b: (raw(a, b, TN), (a, b)), lambda r, g: (nt(r[1], g), nn(r[0], g)))
    return nn, nt, tn


_nn, _nt, _tn = _make_dots(False)
_hnn, _hnt, _htn = _make_dots(True)


def _matmul(a, b, *, dims, name, c=None, out_dtype=F32, tm=1024, tn=1408, tk=1408, layer=None, into=None):
    b_shape = b.shape[-2:]
    if dims == "nn":
        (m, k), (k2, n) = a.shape, b_shape
    elif dims == "nt":
        (m, k), (n, k2) = a.shape, b_shape
    else:
        (k, m), (k2, n) = a.shape, b_shape
    assert k == k2, (a.shape, b.shape, dims)
    tm = _divisor_tile(m, tm, LANES)
    tn = _divisor_tile(n, tn, LANES)
    tk = _divisor_tile(k, tk, LANES)
    nk = k // tk
    dn = {"nn": NN, "nt": NT, "tn": TN}[dims]
    if dims == "tn":
        a_spec = pl.BlockSpec((tk, tm), lambda i, j, kk: (kk, i))
    else:
        a_spec = pl.BlockSpec((tm, tk), lambda i, j, kk: (i, kk))
    b_blk, b_idx = ((tn, tk), lambda i, j, kk: (j, kk)) if dims == "nt" else ((tk, tn), lambda i, j, kk: (kk, j))
    if b.ndim == 3:
        b_spec = pl.BlockSpec((None,) + b_blk, lambda i, j, kk: (layer,) + b_idx(i, j, kk))
    else:
        b_spec = pl.BlockSpec(b_blk, b_idx)
    c_spec = pl.BlockSpec((tm, tn), lambda i, j, kk: (i, j))
    if isinstance(into, int):
        o_spec = pl.BlockSpec((None, tm, tn), lambda i, j, kk: (layer, i, j))
        out_shape = jax.ShapeDtypeStruct((into, m, n), out_dtype)
        into = None
    elif into is not None:
        assert into.shape[1:] == (m, n) and into.dtype == out_dtype
        o_spec = pl.BlockSpec((None, tm, tn), lambda i, j, kk: (layer, i, j))
        out_shape = jax.ShapeDtypeStruct(into.shape, into.dtype)
    else:
        o_spec = c_spec
        out_shape = jax.ShapeDtypeStruct((m, n), out_dtype)
    has_c = c is not None

    def body(*refs):
        a_ref, b_ref = refs[:2]
        c_ref = refs[2] if has_c else None
        o_ref, acc_ref = refs[-2:]
        kk = pl.program_id(2)

        @pl.when(kk == 0)
        def _():
            if has_c:
                acc_ref[...] = c_ref[...].astype(F32)
            else:
                acc_ref[...] = jnp.zeros_like(acc_ref)

        acc_ref[...] += lax.dot_general(a_ref[...].astype(BF16), b_ref[...].astype(BF16), dn,
                                        preferred_element_type=F32)

        @pl.when(kk == nk - 1)
        def _():
            o_ref[...] = acc_ref[...].astype(o_ref.dtype)

    ins = [a, b] + ([c] if has_c else [])
    specs = [a_spec, b_spec] + ([c_spec] if has_c else [])
    aliases = {}
    if into is not None:
        aliases = {len(ins): 0}
        ins.append(into)
        specs.append(pl.BlockSpec(memory_space=pl.ANY))
    return pl.pallas_call(
        body, name=name, grid=(m // tm, n // tn, nk), in_specs=specs, out_specs=o_spec, out_shape=out_shape,
        scratch_shapes=[pltpu.VMEM((tm, tn), F32)], input_output_aliases=aliases,
        compiler_params=_params(("arbitrary", "arbitrary", "arbitrary")),
    )(*ins)


def _rowwise(fn, rows, params, outs, accs=(), *, name):
    t = rows[0][0].shape[0]
    widest = max([w for _, w, _ in rows] + [w for w, _ in outs])
    tm = min(WIDE_ROW_TILE if widest > WIDE_COLS else ROW_TILE, t)
    sub = SUB_ROWS
    while sub < MAX_SUB_ROWS and 2 * sub * widest <= VREG_FILE_ELEMS:
        sub *= 2
    assert t % tm == 0 and tm % sub == 0
    n_rows, n_par, n_out, n_acc = len(rows), len(params), len(outs), len(accs)

    def body(*refs):
        row_refs = refs[:n_rows]
        par_refs = refs[n_rows:n_rows + n_par]
        out_refs = refs[n_rows + n_par:n_rows + n_par + n_out]
        acc_refs = refs[n_rows + n_par + n_out:]
        if n_acc:
            @pl.when(pl.program_id(0) == 0)
            def _():
                for a_ref in acc_refs:
                    a_ref[...] = jnp.zeros_like(a_ref)

        def step(r, carry):
            sl = pl.ds(pl.multiple_of(r * sub, sub), sub)
            vals = [ref[sl, :].astype(F32) for ref in row_refs] + [ref[...] for ref in par_refs]
            res = fn(*vals)
            for o_ref, val in zip(out_refs, res[:n_out]):
                o_ref[sl, :] = val.astype(o_ref.dtype)
            for a_ref, val in zip(acc_refs, res[n_out:]):
                a_ref[...] += val
            return carry

        lax.fori_loop(0, tm // sub, step, 0)

    in_specs = [pl.BlockSpec((tm, w), functools.partial(lambda i, cb: (i, cb), cb=cb)) for _, w, cb in rows]
    in_specs += [pl.BlockSpec(p.shape, lambda i: (0, 0)) for p in params]
    out_specs = [pl.BlockSpec((tm, w), lambda i: (i, 0)) for w, _ in outs]
    out_specs += [pl.BlockSpec(s, lambda i: (0, 0)) for s in accs]
    out_shape = [jax.ShapeDtypeStruct((t, w), d) for w, d in outs]
    out_shape += [jax.ShapeDtypeStruct(s, F32) for s in accs]
    return pl.pallas_call(
        body, name=name, grid=(t // tm,), in_specs=in_specs, out_specs=out_specs, out_shape=out_shape,
        compiler_params=_params(("arbitrary",)),
    )(*[r[0] for r in rows], *params)


def _vjp_fn(fn, n_in, n_out):
    def bwd(*args):
        ins, cts = args[:n_in], args[n_in:]
        _, pull = jax.vjp(fn, *ins)
        return pull(tuple(cts) if n_out > 1 else cts[0])
    return bwd


def _lane(shape):
    return lax.broadcasted_iota(jnp.int32, shape, 1)


def _silu(x):
    return x * jax.nn.sigmoid(x)


def _softplus(x):
    return jnp.maximum(x, 0.0) + jnp.log1p(jnp.exp(-jnp.abs(x)))


def _heads(x, width=HEAD_DIM):
    return [x[:, h * width:(h + 1) * width] for h in range(N_HEADS)]


def _layer_norm(z, g, b):
    mu = jnp.mean(z, -1, keepdims=True)
    zc = z - mu
    var = jnp.mean(zc * zc, -1, keepdims=True)
    return zc * lax.rsqrt(var + LN_EPS) * g + b


def _gdn_act(u, misc, alog_row, dtb_row):
    s = _silu(u)
    w = N_HEADS * HEAD_DIM
    q = jnp.concatenate([t * lax.rsqrt(jnp.sum(t * t, -1, keepdims=True) + RMS_EPS) * HEAD_DIM ** -0.5
                         for t in _heads(s[:, :w])], axis=1)
    k = jnp.concatenate([t * lax.rsqrt(jnp.sum(t * t, -1, keepdims=True) + RMS_EPS)
                         for t in _heads(s[:, w:2 * w])], axis=1)
    v = s[:, 2 * w:]
    lane = _lane(misc.shape)
    beta = jax.nn.sigmoid(misc)
    g = -jnp.exp(alog_row) * _softplus(misc + dtb_row)
    is_beta = (lane >= MISC_BETA0) & (lane < MISC_BETA0 + N_HEADS)
    is_g = (lane >= MISC_A0) & (lane < MISC_A0 + N_HEADS)
    gb = jnp.where(is_beta, beta, jnp.where(is_g, g, 0.0))
    return q, k, v, gb


def _gdn_out(o, z, gn_row):
    outs = []
    for oh, zh in zip(_heads(o), _heads(z)):
        r = oh * lax.rsqrt(jnp.mean(oh * oh, -1, keepdims=True) + RMS_EPS) * gn_row
        outs.append(r * _silu(zh))
    return jnp.concatenate(outs, axis=1)


def _mla_norm(ckv, cq, kvg_row, qg_row):
    cqn = cq * lax.rsqrt(jnp.mean(cq * cq, -1, keepdims=True) + RMS_EPS) * qg_row
    ckvn = ckv * lax.rsqrt(jnp.mean(ckv * ckv, -1, keepdims=True) + RMS_EPS) * kvg_row
    return cqn, ckvn


def _swap_halves(x):
    half = ROPE_DIM // 2
    return jnp.where(_lane(x.shape) < half, pltpu.roll(x, LANES - half, 1), pltpu.roll(x, half, 1))


@jax.custom_vjp
def _rope(x, cos_t, sin_t):
    return x * cos_t + _swap_halves(x) * sin_t


def _rope_fwd(x, cos_t, sin_t):
    return _rope(x, cos_t, sin_t), (cos_t, sin_t)


def _rope_bwd(res, g):
    cos_t, sin_t = res
    return g * cos_t - _swap_halves(g) * sin_t, jnp.zeros_like(cos_t), jnp.zeros_like(sin_t)


_rope.defvjp(_rope_fwd, _rope_bwd)


def _mla_qk(scale, qm, kv, misc, cos_t, sin_t):
    krope = _rope(misc, cos_t, sin_t)
    qs, ks = [], []
    for h in range(N_HEADS):
        base = 2 * HEAD_DIM * h
        qs += [qm[:, base:base + HEAD_DIM], _rope(qm[:, base + HEAD_DIM:base + 2 * HEAD_DIM], cos_t, sin_t)]
        ks += [kv[:, HEAD_DIM * h:HEAD_DIM * (h + 1)], krope]
    return jnp.concatenate(qs, axis=1) * scale, jnp.concatenate(ks, axis=1), kv[:, N_HEADS * HEAD_DIM:]


def _swiglu(gu):
    f = gu.shape[1] // 2
    return _silu(gu[:, :f]) * gu[:, f:]


def _ple_out(x2, pg, pe):
    return x2 + jax.nn.sigmoid(pg) * pe


CONV_W = 4
HALO = 8
CONV_STRIP = 512


def _conv_fwd(h, conv_w, width, *, name, tm=ROW_TILE, sub=32):
    t = h.shape[0]
    tm = min(tm, t)
    nb = tm // HALO

    def body(x_ref, halo_ref, w_ref, u_ref, buf):
        i = pl.program_id(0)
        buf[pl.ds(0, HALO), :] = jnp.where(i > 0, halo_ref[...], 0.0)
        buf[pl.ds(HALO, tm), :] = x_ref[...]
        for c0 in range(0, width, CONV_STRIP):
            cols = pl.ds(c0, CONV_STRIP)
            w = w_ref[:, cols]
            for r0 in range(0, tm, sub):
                acc = jnp.zeros((sub, CONV_STRIP), F32)
                for j in range(CONV_W):
                    acc = acc + w[j:j + 1, :] * buf[pl.ds(HALO + r0 - (CONV_W - 1) + j, sub), cols]
                u_ref[pl.ds(r0, sub), cols] = acc

    return pl.pallas_call(
        body, name=name, grid=(t // tm,),
        in_specs=[pl.BlockSpec((tm, width), lambda i: (i, 0)),
                  pl.BlockSpec((HALO, width), lambda i: (jnp.maximum(i * nb - 1, 0), 0)),
                  pl.BlockSpec(conv_w.shape, lambda i: (0, 0))],
        out_specs=pl.BlockSpec((tm, width), lambda i: (i, 0)),
        out_shape=jax.ShapeDtypeStruct((t, width), F32),
        scratch_shapes=[pltpu.VMEM((tm + HALO, width), F32)],
        compiler_params=_params(("arbitrary",)),
    )(h, h, conv_w)


def _conv_bwd(du, h, conv_w, width, *, name, tm=ROW_TILE, sub=32):
    t = h.shape[0]
    tm = min(tm, t)
    nb = tm // HALO
    n_tiles = t // tm

    def body(du_ref, du_halo, x_ref, x_halo, w_ref, dx_ref, dw_ref, dbuf, xbuf):
        i = pl.program_id(0)

        @pl.when(i == 0)
        def _():
            dw_ref[...] = jnp.zeros_like(dw_ref)

        dbuf[pl.ds(0, tm), :] = du_ref[...]
        dbuf[pl.ds(tm, HALO), :] = jnp.where(i < n_tiles - 1, du_halo[...], 0.0)
        xbuf[pl.ds(0, HALO), :] = jnp.where(i > 0, x_halo[...], 0.0)
        xbuf[pl.ds(HALO, tm), :] = x_ref[...]
        for c0 in range(0, width, CONV_STRIP):
            cols = pl.ds(c0, CONV_STRIP)
            w = w_ref[:, cols]
            dws = [jnp.zeros((HALO, CONV_STRIP), F32) for _ in range(CONV_W)]
            for r0 in range(0, tm, sub):
                acc = jnp.zeros((sub, CONV_STRIP), F32)
                d_here = dbuf[pl.ds(r0, sub), cols]
                for j in range(CONV_W):
                    acc = acc + w[j:j + 1, :] * dbuf[pl.ds(r0 + (CONV_W - 1) - j, sub), cols]
                    prod = d_here * xbuf[pl.ds(HALO + r0 - (CONV_W - 1) + j, sub), cols]
                    for g0 in range(0, sub, HALO):
                        dws[j] = dws[j] + prod[g0:g0 + HALO, :]
                dx_ref[pl.ds(r0, sub), cols] = acc.astype(dx_ref.dtype)
            for j in range(CONV_W):
                dw_ref[pl.ds(j, 1), cols] += jnp.sum(dws[j], axis=0, keepdims=True)

    return pl.pallas_call(
        body, name=name, grid=(n_tiles,),
        in_specs=[pl.BlockSpec((tm, width), lambda i: (i, 0)),
                  pl.BlockSpec((HALO, width), lambda i: (jnp.minimum((i + 1) * nb, t // HALO - 1), 0)),
                  pl.BlockSpec((tm, width), lambda i: (i, 0)),
                  pl.BlockSpec((HALO, width), lambda i: (jnp.maximum(i * nb - 1, 0), 0)),
                  pl.BlockSpec(conv_w.shape, lambda i: (0, 0))],
        out_specs=[pl.BlockSpec((tm, width), lambda i: (i, 0)),
                   pl.BlockSpec((HALO, width), lambda i: (0, 0))],
        out_shape=[jax.ShapeDtypeStruct((t, width), BF16), jax.ShapeDtypeStruct((HALO, width), F32)],
        scratch_shapes=[pltpu.VMEM((tm + HALO, width), F32), pltpu.VMEM((tm + HALO, width), F32)],
        compiler_params=_params(("arbitrary",)),
    )(du, du, h, h, conv_w)


@jax.custom_vjp
def _inv_unit_lower(low):
    n = low.shape[-1]
    eye = (lax.broadcasted_iota(jnp.int32, (n, n), 0) == lax.broadcasted_iota(jnp.int32, (n, n), 1)).astype(F32)
    x = eye - low
    p = low
    span = 2
    while span < n:
        p = _hnn(p, p)
        x = x + _hnn(x, p)
        span *= 2
    return x


def _inv_fwd(low):
    x = _inv_unit_lower(low)
    return x, x


def _inv_bwd(x, g):
    return (-_htn(x, _hnt(g, x)),)


_inv_unit_lower.defvjp(_inv_fwd, _inv_bwd)


@jax.custom_vjp
def _inv_known(low, inverse):
    return inverse


_inv_known.defvjp(lambda low, inverse: (inverse, inverse), lambda x, g: (_inv_bwd(x, g)[0], jnp.zeros_like(x)))


def _gdn_prep(q, k, v, gb, known_inverse=None):
    c = CHUNK
    n = q.shape[0] // c
    pairs = [(g, h) for g in range(n) for h in range(N_HEADS)]
    row = lax.broadcasted_iota(jnp.int32, (c, c), 0)
    col = lax.broadcasted_iota(jnp.int32, (c, c), 1)
    tri_incl = row >= col
    tri_strict = row > col
    lane = _lane((c, LANES))
    sub = lax.broadcasted_iota(jnp.int32, (LANES, c), 0)
    last = lax.broadcasted_iota(jnp.int32, (c, 1), 0) == c - 1

    def split(x):
        return jnp.stack([x[g * c:(g + 1) * c, h * HEAD_DIM:(h + 1) * HEAD_DIM] for g, h in pairs])

    gbs = [gb[g * c:(g + 1) * c, :] for g in range(n)]
    gbts = [x.T for x in gbs]
    g_col = jnp.stack([jnp.sum(jnp.where(lane == MISC_A0 + h, gbs[g], 0.0), axis=1, keepdims=True) for g, h in pairs])
    b_col = jnp.stack([jnp.sum(jnp.where(lane == MISC_BETA0 + h, gbs[g], 0.0), axis=1, keepdims=True) for g, h in pairs])
    g_row = jnp.stack([jnp.sum(jnp.where(sub == MISC_A0 + h, gbts[g], 0.0), axis=0, keepdims=True) for g, h in pairs])
    gc_col = jnp.sum(jnp.where(tri_incl, g_row, 0.0), axis=2, keepdims=True)
    gc_row = jnp.sum(jnp.where(row <= col, g_col, 0.0), axis=1, keepdims=True)
    decay = jnp.where(tri_incl, jnp.exp(jnp.where(tri_incl, gc_col - gc_row, 0.0)), 0.0)
    g_last = jnp.sum(jnp.where(last, gc_col, 0.0), axis=1, keepdims=True)
    qs, ks, vs = split(q), split(k), split(v)
    kb = ks * b_col
    low = jnp.where(tri_strict, _nt(kb, ks) * decay, 0.0)
    if known_inverse is None:
        tinv = _inv_unit_lower(low)
    else:
        tinv = _inv_known(low, jnp.stack([known_inverse[g * c:(g + 1) * c, h * c:(h + 1) * c] for g, h in pairs]))
    eg = jnp.exp(gc_col)
    sol = _hnn(tinv, jnp.concatenate([vs * b_col, kb * eg], axis=2))
    attn = jnp.where(tri_incl, _nt(qs, ks) * decay, 0.0)
    qd = qs * eg
    kd = ks * jnp.exp(g_last - gc_col)

    def merge(x):
        return jnp.concatenate([jnp.concatenate([x[g * N_HEADS + h] for h in range(N_HEADS)], axis=1)
                                for g in range(n)], axis=0)

    glb = jnp.concatenate([sum(jnp.where(lane == h, g_last[g * N_HEADS + h], 0.0) for h in range(N_HEADS))
                           for g in range(n)], axis=0)
    outs = (merge(sol[:, :, :HEAD_DIM]), merge(sol[:, :, HEAD_DIM:]), merge(qd), merge(kd), merge(attn), glb)
    return outs, merge(tinv)


def _gdn_seq(state, u, w, qd, kd, attn, glb):
    c = u.shape[0]
    first = lax.broadcasted_iota(jnp.int32, glb.shape, 0) == 0
    lane = _lane(glb.shape)
    heads = lambda x: jnp.stack([x[:, h * HEAD_DIM:(h + 1) * HEAD_DIM] for h in range(N_HEADS)])
    g_last = jnp.stack([jnp.sum(jnp.sum(jnp.where(first & (lane == h), glb, 0.0), axis=1, keepdims=True),
                                axis=0, keepdims=True) for h in range(N_HEADS)])
    s = jnp.stack([state[h * HEAD_DIM:(h + 1) * HEAD_DIM, :] for h in range(N_HEADS)])
    at = jnp.stack([attn[:, h * c:(h + 1) * c] for h in range(N_HEADS)])
    v_new = heads(u) - _nn(heads(w), s)
    o = _nn(heads(qd), s) + _nn(at, v_new)
    s_new = s * jnp.exp(g_last) + _tn(heads(kd), v_new)
    return (jnp.concatenate([o[h] for h in range(N_HEADS)], axis=1),
            jnp.concatenate([s_new[h] for h in range(N_HEADS)], axis=0))


PREP_CHUNKS = 2
SEQ_CHUNKS = 8


def _gdn_prep_fwd(q, k, v, gb, *, name):
    t, w = q.shape
    rows = min(PREP_CHUNKS * CHUNK, t)

    def body(q_ref, k_ref, v_ref, gb_ref, *out_refs):
        outs, inverse = _gdn_prep(q_ref[...], k_ref[...], v_ref[...], gb_ref[...])
        for o_ref, val in zip(out_refs, outs + (inverse,)):
            o_ref[...] = val

    spec = lambda width: pl.BlockSpec((rows, width), lambda i: (i, 0))
    widths = [w, w, w, w, N_HEADS * CHUNK, LANES, N_HEADS * CHUNK]
    res = pl.pallas_call(
        body, name=name, grid=(t // rows,),
        in_specs=[spec(w), spec(w), spec(w), spec(LANES)],
        out_specs=[spec(x) for x in widths],
        out_shape=[jax.ShapeDtypeStruct((t, x), F32) for x in widths],
        compiler_params=_params(("arbitrary",)),
    )(q, k, v, gb)
    return tuple(res[:6]), res[6]


def _gdn_prep_bwd(q, k, v, gb, inverse, cts, *, name):
    t, w = q.shape
    rows = min(PREP_CHUNKS * CHUNK, t)

    def body(q_ref, k_ref, v_ref, gb_ref, inv_ref, du, dw, dqd, dkd, dattn, dglb, dq_ref, dk_ref, dv_ref, dgb_ref):
        known = inv_ref[...]
        _, pull = jax.vjp(lambda a, b, c_, d_: _gdn_prep(a, b, c_, d_, known)[0],
                          q_ref[...], k_ref[...], v_ref[...], gb_ref[...])
        dq, dk, dv, dgb = pull(tuple(r[...] for r in (du, dw, dqd, dkd, dattn, dglb)))
        dq_ref[...] = dq
        dk_ref[...] = dk
        dv_ref[...] = dv
        dgb_ref[...] = dgb

    spec = lambda width: pl.BlockSpec((rows, width), lambda i: (i, 0))
    widths = [w, w, w, w, N_HEADS * CHUNK, LANES]
    return pl.pallas_call(
        body, name=name, grid=(t // rows,),
        in_specs=[spec(w), spec(w), spec(w), spec(LANES), spec(N_HEADS * CHUNK)] + [spec(x) for x in widths],
        out_specs=[spec(w), spec(w), spec(w), spec(LANES)],
        out_shape=[jax.ShapeDtypeStruct((t, w), F32)] * 3 + [jax.ShapeDtypeStruct((t, LANES), F32)],
        compiler_params=_params(("arbitrary",)),
    )(q, k, v, gb, inverse, *cts)


def _gdn_seq_fwd(prep, *, name):
    t, w = prep[0].shape
    rows = min(SEQ_CHUNKS * CHUNK, t)
    per = rows // CHUNK

    def body(u_ref, w_ref, qd_ref, kd_ref, at_ref, gl_ref, o_ref, sall_ref, s_scr):
        @pl.when(pl.program_id(0) == 0)
        def _():
            s_scr[...] = jnp.zeros_like(s_scr)

        def step(j, carry):
            sl = pl.ds(pl.multiple_of(j * CHUNK, CHUNK), CHUNK)
            s = s_scr[...]
            sall_ref[j] = s
            o, s_new = _gdn_seq(s, u_ref[sl, :], w_ref[sl, :], qd_ref[sl, :], kd_ref[sl, :], at_ref[sl, :], gl_ref[sl, :])
            o_ref[sl, :] = o
            s_scr[...] = s_new
            return carry

        lax.fori_loop(0, per, step, 0)

    spec = lambda width: pl.BlockSpec((rows, width), lambda i: (i, 0))
    widths = [w, w, w, w, N_HEADS * CHUNK, LANES]
    return pl.pallas_call(
        body, name=name, grid=(t // rows,),
        in_specs=[spec(x) for x in widths],
        out_specs=[spec(w), pl.BlockSpec((per, w, HEAD_DIM), lambda i: (i, 0, 0))],
        out_shape=[jax.ShapeDtypeStruct((t, w), F32), jax.ShapeDtypeStruct((t // CHUNK, w, HEAD_DIM), F32)],
        scratch_shapes=[pltpu.VMEM((w, HEAD_DIM), F32)],
        compiler_params=_params(("arbitrary",)),
    )(*prep)


def _gdn_seq_bwd(prep, s_all, do, *, name):
    t, w = prep[0].shape
    rows = min(SEQ_CHUNKS * CHUNK, t)
    per = rows // CHUNK
    n = t // rows

    def body(u_ref, w_ref, qd_ref, kd_ref, at_ref, gl_ref, sall_ref, do_ref, du, dw, dqd, dkd, dat, dgl, ds_scr):
        @pl.when(pl.program_id(0) == 0)
        def _():
            ds_scr[...] = jnp.zeros_like(ds_scr)

        def step(jj, carry):
            j = per - 1 - jj
            sl = pl.ds(pl.multiple_of(j * CHUNK, CHUNK), CHUNK)
            _, pull = jax.vjp(_gdn_seq, sall_ref[j], u_ref[sl, :], w_ref[sl, :], qd_ref[sl, :], kd_ref[sl, :],
                              at_ref[sl, :], gl_ref[sl, :])
            res = pull((do_ref[sl, :], ds_scr[...]))
            ds_scr[...] = res[0]
            for o_ref, val in zip((du, dw, dqd, dkd, dat, dgl), res[1:]):
                o_ref[sl, :] = val
            return carry

        lax.fori_loop(0, per, step, 0)

    spec = lambda width: pl.BlockSpec((rows, width), lambda i: (n - 1 - i, 0))
    widths = [w, w, w, w, N_HEADS * CHUNK, LANES]
    return pl.pallas_call(
        body, name=name, grid=(n,),
        in_specs=[spec(x) for x in widths] + [pl.BlockSpec((per, w, HEAD_DIM), lambda i: (n - 1 - i, 0, 0)), spec(w)],
        out_specs=[spec(x) for x in widths],
        out_shape=[jax.ShapeDtypeStruct((t, x), F32) for x in widths],
        scratch_shapes=[pltpu.VMEM((w, HEAD_DIM), F32)],
        compiler_params=_params(("arbitrary",)),
    )(*prep, s_all, do)


QK_DIM = 2 * HEAD_DIM
ATT_TILE = 1024
NEG = -1e30


ATT_SPLIT = 4


def _chunk_mask(n_rows, n_cols, key_major, query_offset):
    r = lax.broadcasted_iota(jnp.int32, (n_rows, n_cols), 0)
    c = lax.broadcasted_iota(jnp.int32, (n_rows, n_cols), 1)
    if key_major:
        return r // CHUNK <= (c + query_offset) // CHUNK
    return c // CHUNK <= (r + query_offset) // CHUNK


def _visible_keys(tile, diagonal):
    hq = tile // ATT_SPLIT
    return [(a + 1) * hq if diagonal else tile for a in range(ATT_SPLIT)]


def _dot_nn(a, b):
    return lax.dot_general(a, b, NN, preferred_element_type=F32)


def _blocked_transpose(x, width):
    t = x.shape[0]
    tile = min(ATT_TILE, t)
    return x.reshape(t // tile, tile, N_HEADS, width).transpose(2, 0, 3, 1)


def _attn_fwd(q, kt, v1, *, name):
    t = q.shape[0]
    tq = min(ATT_TILE, t)
    nq = t // tq

    def body(q_ref, kt_ref, v_ref, o_ref, lse_ref, m_scr, acc_scr):
        qi = pl.program_id(1)
        m_scr[...] = jnp.full_like(m_scr, NEG)
        acc_scr[...] = jnp.zeros_like(acc_scr)
        hq = tq // ATT_SPLIT
        parts = [pl.ds(a * hq, hq) for a in range(ATT_SPLIT)]
        qs = [q_ref[sl, :] for sl in parts]

        def step(kj, masked):
            rows = pl.ds(pl.multiple_of(kj * tq, tq), tq)
            kt_blk, vv = kt_ref[kj], v_ref[rows, :]
            seen = _visible_keys(tq, masked)
            ss = [_dot_nn(qv, kt_blk[:, :w]) for qv, w in zip(qs, seen)]
            for a, sl in enumerate(parts):
                s = ss[a]
                if masked:
                    s = jnp.where(_chunk_mask(hq, seen[a], False, a * hq), s, NEG)
                m_old = m_scr[sl, :]
                m_new = jnp.maximum(m_old, jnp.max(s, axis=1, keepdims=True))
                p = jnp.exp(s - m_new)
                acc_scr[sl, :] = jnp.exp(m_old - m_new) * acc_scr[sl, :] + _dot_nn(p.astype(BF16), vv[:seen[a], :])
                m_scr[sl, :] = m_new

        def loop_body(kj, carry):
            step(kj, False)
            return carry

        lax.fori_loop(0, qi, loop_body, 0)
        step(qi, True)
        acc = acc_scr[...]
        o_ref[...] = (acc[:, :HEAD_DIM] / acc[:, HEAD_DIM:]).astype(o_ref.dtype)
        lse_ref[...] = m_scr[...] + jnp.log(acc[:, HEAD_DIM:HEAD_DIM + 1])

    return pl.pallas_call(
        body, name=name, grid=(N_HEADS, nq),
        in_specs=[pl.BlockSpec((tq, QK_DIM), lambda h, i: (i, h)),
                  pl.BlockSpec((None, nq, QK_DIM, tq), lambda h, i: (h, 0, 0, 0)),
                  pl.BlockSpec((t, 2 * HEAD_DIM), lambda h, i: (0, h))],
        out_specs=[pl.BlockSpec((tq, HEAD_DIM), lambda h, i: (i, h)),
                   pl.BlockSpec((None, tq, 1), lambda h, i: (h, i, 0))],
        out_shape=[jax.ShapeDtypeStruct((t, N_HEADS * HEAD_DIM), BF16),
                   jax.ShapeDtypeStruct((N_HEADS, t, 1), F32)],
        scratch_shapes=[pltpu.VMEM((tq, 1), F32), pltpu.VMEM((tq, 2 * HEAD_DIM), F32)],
        compiler_params=_params(("arbitrary", "arbitrary")),
    )(q, kt, v1)


def _attn_delta(dom, o, *, name):
    hw = o.shape[1]

    def fn(do, ov):
        lane = _lane((do.shape[0], LANES))
        out = jnp.zeros((do.shape[0], LANES), F32)
        for h, (a, b) in enumerate(zip(_heads(do), _heads(ov))):
            out = out + jnp.where(lane == h, jnp.sum(a * b, axis=1, keepdims=True), 0.0)
        return (out,)

    return _rowwise(fn, [(dom, hw, 1), (o, hw, 0)], [], [(LANES, F32)], name=name)[0]


def _attn_bwd(q, qt, k, v, lse_row, delta_row, do, dot, *, name):
    t = q.shape[0]
    tk = min(ATT_TILE, t)
    nk = t // tk

    def body(q_ref, qt_ref, k_ref, v_ref, lse_ref, delta_ref, do_ref, dot_ref, dk_ref, dv_ref, dq_ref, dk_scr, dv_scr):
        kj = pl.program_id(1)

        @pl.when(kj == 0)
        def _():
            dq_ref[...] = jnp.zeros_like(dq_ref)

        dk_scr[...] = jnp.zeros_like(dk_scr)
        dv_scr[...] = jnp.zeros_like(dv_scr)
        kv_ = k_ref[...]
        vv = v_ref[...]
        hq = tk // ATT_SPLIT

        def step(qi, masked):
            lse_v, delta_v = lse_ref[qi], delta_ref[qi]
            qt_blk, dot_blk = qt_ref[qi], dot_ref[qi]
            rows = [pl.ds(pl.multiple_of(qi * tk + a * hq, hq), hq) for a in range(ATT_SPLIT)]
            qs = [q_ref[r, :] for r in rows]
            dos = [do_ref[r, :] for r in rows]
            seen = _visible_keys(tk, masked)
            ss = [_dot_nn(kv_[:seen[a], :], qt_blk[:, a * hq:(a + 1) * hq]) for a in range(ATT_SPLIT)]
            dps = [_dot_nn(vv[:seen[a], :], dot_blk[:, a * hq:(a + 1) * hq]) for a in range(ATT_SPLIT)]
            for a in range(ATT_SPLIT):
                cols = slice(a * hq, (a + 1) * hq)
                keys = pl.ds(0, seen[a])
                p = jnp.exp(ss[a] - lse_v[:, cols])
                if masked:
                    p = jnp.where(_chunk_mask(seen[a], hq, True, a * hq), p, 0.0)
                dv_scr[keys, :] += _dot_nn(p.astype(BF16), dos[a])
                ds = (p * (dps[a] - delta_v[:, cols])).astype(BF16)
                dk_scr[keys, :] += _dot_nn(ds, qs[a])
                dq_ref[rows[a], :] += lax.dot_general(ds, kv_[:seen[a], :], TN, preferred_element_type=F32)

        step(kj, True)

        def loop_body(qi, carry):
            step(qi, False)
            return carry

        lax.fori_loop(kj + 1, nk, loop_body, 0)
        dk_ref[...] = dk_scr[...].astype(dk_ref.dtype)
        dv_ref[...] = dv_scr[...].astype(dv_ref.dtype)

    once = dict(pipeline_mode=pl.Buffered(1))
    stat = pl.BlockSpec((None, nk, 1, tk), lambda h, j: (h, 0, 0, 0))
    return pl.pallas_call(
        body, name=name, grid=(N_HEADS, nk),
        in_specs=[pl.BlockSpec((t, QK_DIM), lambda h, j: (0, h), **once),
                  pl.BlockSpec((None, nk, QK_DIM, tk), lambda h, j: (h, 0, 0, 0), **once),
                  pl.BlockSpec((tk, QK_DIM), lambda h, j: (j, h)),
                  pl.BlockSpec((tk, HEAD_DIM), lambda h, j: (j, h)),
                  stat, stat,
                  pl.BlockSpec((t, HEAD_DIM), lambda h, j: (0, h), **once),
                  pl.BlockSpec((None, nk, HEAD_DIM, tk), lambda h, j: (h, 0, 0, 0), **once)],
        out_specs=[pl.BlockSpec((tk, QK_DIM), lambda h, j: (j, h)),
                   pl.BlockSpec((tk, HEAD_DIM), lambda h, j: (j, h)),
                   pl.BlockSpec((t, QK_DIM), lambda h, j: (0, h))],
        out_shape=[jax.ShapeDtypeStruct((t, N_HEADS * QK_DIM), BF16),
                   jax.ShapeDtypeStruct((t, N_HEADS * HEAD_DIM), BF16),
                   jax.ShapeDtypeStruct((t, N_HEADS * QK_DIM), F32)],
        scratch_shapes=[pltpu.VMEM((tk, QK_DIM), F32), pltpu.VMEM((tk, HEAD_DIM), F32)],
        compiler_params=_params(("arbitrary", "arbitrary")),
    )(q, qt, k, v, lse_row, delta_row, do, dot)


def _rope_tables(pos_col, inv_freq_row, *, name):
    t = pos_col.shape[0]
    tm = min(ROW_TILE, t)

    def body(p_ref, f_ref, c_ref, s_ref):
        ang = p_ref[...].astype(F32) * f_ref[...]
        lane = _lane(ang.shape)
        c_ref[...] = jnp.where(lane < ROPE_DIM, jnp.cos(ang), 0.0)
        sn = jnp.sin(ang)
        s_ref[...] = jnp.where(lane < ROPE_DIM // 2, -sn, jnp.where(lane < ROPE_DIM, sn, 0.0))

    out = pl.BlockSpec((tm, LANES), lambda i: (i, 0))
    return pl.pallas_call(
        body, name=name, grid=(t // tm,),
        in_specs=[pl.BlockSpec((tm, 1), lambda i: (i, 0)), pl.BlockSpec((1, LANES), lambda i: (0, 0))],
        out_specs=[out, out], out_shape=[jax.ShapeDtypeStruct((t, LANES), F32)] * 2,
        compiler_params=_params(("arbitrary",)),
    )(pos_col, inv_freq_row)


def _loss_head(y, target):
    width = y.shape[1]

    def fn(yv, tv):
        e = yv - tv
        part = 0.5 * jnp.sum(jnp.mean(e * e, axis=1, keepdims=True), axis=0, keepdims=True)
        return e * (1.0 / width), jnp.broadcast_to(part, (HALO, LANES))

    return _rowwise(fn, [(y, width, 0), (target, width, 0)], [], [(width, F32)], [(HALO, LANES)], name="loss_head")


def _adamw(w, g, m, v, *, name):
    shape = w.shape
    w2, g2, m2, v2 = (a.reshape(-1, shape[-1]) for a in (w, g, m, v))
    rows, width = w2.shape
    tr = _divisor_tile(rows, max(8, (1 << 19) // max(width, 1)), 8)
    bc1 = 1.0 - ADAM_B1 ** ADAM_STEP
    bc2 = 1.0 - ADAM_B2 ** ADAM_STEP

    def body(w_ref, g_ref, m_ref, v_ref, d_ref, mo_ref, vo_ref):
        gv = g_ref[...]
        mn = ADAM_B1 * m_ref[...] + (1.0 - ADAM_B1) * gv
        vn = ADAM_B2 * v_ref[...] + (1.0 - ADAM_B2) * (gv * gv)
        d_ref[...] = -ADAM_LR * ((mn / bc1) / (jnp.sqrt(vn / bc2) + ADAM_EPS) + ADAM_WD * w_ref[...])
        mo_ref[...] = mn
        vo_ref[...] = vn

    spec = pl.BlockSpec((tr, width), lambda i: (i, 0))
    outs = pl.pallas_call(
        body, name=name, grid=(rows // tr,), in_specs=[spec] * 4, out_specs=[spec] * 3,
        out_shape=[jax.ShapeDtypeStruct((rows, width), F32)] * 3,
        compiler_params=_params(("arbitrary",)),
    )(w2, g2, m2, v2)
    return tuple(o.reshape(shape) for o in outs)


HBM_SPEC = pl.BlockSpec(memory_space=pltpu.HBM)


def _position():
    return lax.axis_index("x"), lax.axis_index("y"), lax.axis_index("c")


def _other_chips(x, y):
    return [(1 - x, y), (x, 1 - y), (1 - x, 1 - y)]


class _Stream:
    def __init__(self, kind, size=0):
        self.kind, self.size = kind, size
        self.parts = 2 if kind == "heads" else 1

    def local(self, ref, k, part):
        if self.kind == "rows":
            return ref.at[:, pl.ds(k * self.size, self.size), :]
        if self.kind == "cols":
            return ref.at[:, :, pl.ds(k * self.size, self.size)]
        if self.kind == "heads":
            return ref.at[:, :, pl.ds(part * N_HEADS * HEAD_DIM + k * HEAD_DIM, HEAD_DIM)]
        if self.kind == "piece":
            return ref.at[k]
        return ref

    def shard(self, ref, part):
        if self.kind == "heads":
            return ref.at[:, :, pl.ds(part * HEAD_DIM, HEAD_DIM)]
        return ref

    def half_local(self, ref, k, part, cc, hd):
        if self.kind == "piece":
            return ref.at[k, pl.ds(cc * hd, hd)]
        return self.local(ref.at[pl.ds(cc * hd, hd)], k, part)


def _remote(src, dst, send_sems, recv_sems, idx, to):
    return pltpu.make_async_remote_copy(src_ref=src, dst_ref=dst, send_sem=send_sems.at[idx],
                                        recv_sem=recv_sems.at[idx], device_id=to, device_id_type=MESH)


def _comm_call(body, ins, out_shapes, n_remote, n_local, *, name):
    scratch = [pltpu.SemaphoreType.DMA((n_remote,)), pltpu.SemaphoreType.DMA((n_remote,))]
    if n_local:
        scratch.append(pltpu.SemaphoreType.DMA((n_local,)))
    return pl.pallas_call(
        body, name=name, in_specs=[HBM_SPEC] * len(ins), out_specs=[HBM_SPEC] * len(out_shapes), out_shape=out_shapes,
        scratch_shapes=scratch, compiler_params=pltpu.CompilerParams(has_side_effects=True),
    )(*ins)


def _gather_chips(shards, streams, out_shapes, *, name):
    n = len(shards)
    hd = shards[0].shape[0] // 2
    flat = [(t, part) for t in range(n) for part in range(streams[t].parts)]
    ns = len(flat)

    def body(*refs):
        s_refs, o_refs = refs[:n], refs[n:2 * n]
        send_sems, recv_sems = refs[2 * n:]
        x, y, c = _position()
        sibling = (x, y, 1 - c)
        chips = _other_chips(x, y)
        me = 2 * x + y
        sent = []
        for s, (t, part) in enumerate(flat):
            st = streams[t]
            sent.append(_remote(st.shard(s_refs[t], part), st.local(o_refs[t], me, part), send_sems, recv_sems,
                                6 * ns + s, sibling))
            sent[-1].start()
            src = st.shard(s_refs[t].at[pl.ds(c * hd, hd)], part)
            for j, (cx, cy) in enumerate(chips):
                sent.append(_remote(src, st.half_local(o_refs[t], me, part, c, hd), send_sems, recv_sems,
                                    3 * s + j, (cx, cy, c)))
                sent[-1].start()
        for s, (t, part) in enumerate(flat):
            st = streams[t]
            for j, (cx, cy) in enumerate(chips):
                blk = st.half_local(o_refs[t], 2 * cx + cy, part, c, hd)
                _remote(blk, blk, send_sems, recv_sems, 3 * s + j, (x, y, c)).wait_recv()
                sent.append(_remote(blk, blk, send_sems, recv_sems, 3 * ns + 3 * s + j, sibling))
                sent[-1].start()
        for s, (t, part) in enumerate(flat):
            st = streams[t]
            for j, (cx, cy) in enumerate(chips):
                blk = st.half_local(o_refs[t], 2 * cx + cy, part, 1 - c, hd)
                _remote(blk, blk, send_sems, recv_sems, 3 * ns + 3 * s + j, (x, y, c)).wait_recv()
            own = st.local(o_refs[t], me, part)
            _remote(own, own, send_sems, recv_sems, 6 * ns + s, (x, y, c)).wait_recv()
        for cp in sent:
            cp.wait_send()

    return _comm_call(body, shards, out_shapes, 7 * ns, 0, name=name)


def _sibling_take_other_half(gs, *, name):
    n = len(gs)
    hd = gs[0].shape[0] // 2

    def body(*refs):
        g_refs, o_refs = refs[:n], refs[n:2 * n]
        send_sems, recv_sems = refs[2 * n:]
        x, y, c = _position()
        copies = [_remote(g_refs[t].at[pl.ds((1 - c) * hd, hd)], o_refs[t], send_sems, recv_sems, t, (x, y, 1 - c))
                  for t in range(n)]
        for cp in copies:
            cp.start()
        for cp in copies:
            cp.wait()

    outs = [jax.ShapeDtypeStruct((hd,) + g.shape[1:], g.dtype) for g in gs]
    return _comm_call(body, gs, outs, n, 0, name=name)


def _chips_exchange(ps, streams, shard_shapes, *, name):
    n = len(ps)
    flat = [(t, part) for t in range(n) for part in range(streams[t].parts)]

    def body(*refs):
        p_refs, o_refs = refs[:n], refs[n:2 * n]
        send_sems, recv_sems = refs[2 * n:]
        x, y, c = _position()
        copies = []
        for s, (t, part) in enumerate(flat):
            st = streams[t]
            for j, (cx, cy) in enumerate(_other_chips(x, y)):
                copies.append(_remote(st.local(p_refs[t], 2 * cx + cy, part), st.shard(o_refs[t].at[j], part),
                                      send_sems, recv_sems, 3 * s + j, (cx, cy, c)))
        for cp in copies:
            cp.start()
        for cp in copies:
            cp.wait()

    outs = [jax.ShapeDtypeStruct((3,) + tuple(shp), p.dtype) for p, shp in zip(ps, shard_shapes)]
    return _comm_call(body, ps, outs, 3 * len(flat), 0, name=name)


def _sibling_join_halves(bufs, *, name):
    n = len(bufs)
    hd = bufs[0].shape[0] // 2

    def body(*refs):
        o_refs = refs[n:2 * n]
        send_sems, recv_sems = refs[2 * n:]
        x, y, c = _position()
        sent = []
        for t in range(n):
            mine = o_refs[t].at[pl.ds(c * hd, hd)]
            sent.append(_remote(mine, mine, send_sems, recv_sems, t, (x, y, 1 - c)))
            sent[-1].start()
        for t in range(n):
            theirs = o_refs[t].at[pl.ds((1 - c) * hd, hd)]
            _remote(theirs, theirs, send_sems, recv_sems, t, (x, y, c)).wait_recv()
        for cp in sent:
            cp.wait_send()

    return pl.pallas_call(
        body, name=name, in_specs=[HBM_SPEC] * n, out_specs=[HBM_SPEC] * n,
        out_shape=[jax.ShapeDtypeStruct(b.shape, b.dtype) for b in bufs],
        scratch_shapes=[pltpu.SemaphoreType.DMA((n,)), pltpu.SemaphoreType.DMA((n,))],
        input_output_aliases={t: t for t in range(n)},
        compiler_params=pltpu.CompilerParams(has_side_effects=True),
    )(*bufs)


def _row_tile(rows, width):
    return _divisor_tile(rows, max(16, (1 << 19) // width), 16)


def _add_own_half(g, got, c_idx, out_dtype, *, name):
    hd, r, w = got.shape
    tr = _row_tile(r, w)

    def body(c_ref, g_ref, a_ref, o_ref):
        o_ref[...] = (g_ref[...] + a_ref[...]).astype(o_ref.dtype)

    return pl.pallas_call(
        body, name=name,
        grid_spec=pltpu.PrefetchScalarGridSpec(
            num_scalar_prefetch=1, grid=(hd, r // tr),
            in_specs=[pl.BlockSpec((None, None, tr, w), lambda l, i, c_ref: (c_ref[0], l, i, 0)),
                      pl.BlockSpec((None, tr, w), lambda l, i, c_ref: (l, i, 0))],
            out_specs=pl.BlockSpec((None, tr, w), lambda l, i, c_ref: (l, i, 0))),
        out_shape=jax.ShapeDtypeStruct((hd, r, w), out_dtype),
        compiler_params=_params(("arbitrary", "arbitrary")),
    )(c_idx, g.reshape((2, hd) + g.shape[1:]), got)


def _sum_chips(p, got, place, stream, *, name):
    _, hd, rs, cs = got.shape
    wb = HEAD_DIM if stream.kind == "heads" else cs
    tr = _row_tile(rs, wb)
    kind, size = stream.kind, stream.size

    def own_index(l, i, g, k_ref, c_ref):
        k = k_ref[0]
        if kind == "rows":
            return (l, k * (size // tr) + i, 0)
        if kind == "cols":
            return (l, i, k)
        if kind == "heads":
            return (l, i, g * N_HEADS + k)
        if kind == "piece":
            return (k, l, i, 0)
        return (l, i, 0)

    own_blk = (None, None, tr, wb) if kind == "piece" else (None, tr, wb)

    def body(k_ref, c_ref, p_ref, fx_ref, fy_ref, fxy_ref, o_ref):
        f = lambda r: r[...].astype(F32)
        o_ref[...] = (f(p_ref) + f(fy_ref)) + (f(fx_ref) + f(fxy_ref))

    def rel(j):
        return pl.BlockSpec((None, None, tr, wb), functools.partial(lambda l, i, g, k_ref, c_ref, j: (j, l, i, g), j=j))

    return pl.pallas_call(
        body, name=name,
        grid_spec=pltpu.PrefetchScalarGridSpec(
            num_scalar_prefetch=2, grid=(hd, rs // tr, stream.parts),
            in_specs=[pl.BlockSpec(own_blk, own_index), rel(0), rel(1), rel(2)],
            out_specs=pl.BlockSpec((None, tr, wb), lambda l, i, g, k_ref, c_ref: (c_ref[0] * hd + l, i, g))),
        out_shape=jax.ShapeDtypeStruct((2 * hd, rs, cs), F32),
        compiler_params=_params(("arbitrary", "arbitrary", "arbitrary")),
    )(place[0], place[1], p, got, got, got)


MATRICES = ("w_in", "w_uq", "w_ukv", "w_out", "w_gate_up", "w_down", "w_ple", "w_ple_gate", "conv_w")
VECTORS = ("a_log", "dt_bias", "gdn_norm_g", "q_norm_g", "kv_norm_g", "ln1_g", "ln1_b", "ln2_g", "ln2_b")
WEIGHTS = ("w_in", "conv_w", "a_log", "dt_bias", "gdn_norm_g", "q_norm_g", "w_uq", "kv_norm_g", "w_ukv", "w_out",
           "ln1_g", "ln1_b", "w_gate_up", "w_down", "ln2_g", "ln2_b", "w_ple", "w_ple_gate")
ROW_SHARDED = ("w_out", "w_down", "w_ple_gate")
N_CHIPS = 4


def _stream_of(name, shard_shape):
    if name in ("w_in", "w_uq"):
        return _Stream("piece")
    if name == "w_ukv":
        return _Stream("heads")
    if name in ROW_SHARDED:
        return _Stream("rows", shard_shape[0])
    return _Stream("cols", shard_shape[1])


def _pack_vectors(vecs, depth):
    flat = jnp.concatenate([vecs[n].reshape(depth, -1) for n in VECTORS], axis=1)
    pad = jnp.zeros((depth, VEC_ROWS * LANES - flat.shape[1]), F32)
    return jnp.concatenate([flat, pad], axis=1).reshape(depth, VEC_ROWS, LANES)


def _unpack_vectors(packed, shapes):
    depth = packed.shape[0]
    flat = packed.reshape(depth, VEC_ROWS * LANES)
    out, off = {}, 0
    for n in VECTORS:
        out[n] = flat[:, off:off + shapes[n][1]]
        off += shapes[n][1]
    return out


VEC_ROWS = 40


class _Dims:
    def __init__(self, d_model, in_width, q_lora, kv_lora, d_ff2, ple_dim):
        self.d = d_model
        self.hw = N_HEADS * HEAD_DIM
        self.in_width = in_width
        self.q_lora, self.kv_lora = q_lora, kv_lora
        self.ff2 = d_ff2
        self.ple = ple_dim
        self.c_kv0 = 4 * self.hw
        self.c_q0 = self.c_kv0 + kv_lora
        self.misc0 = self.c_q0 + q_lora
        self.h_width = self.misc0 + LANES
        assert self.c_kv0 % kv_lora == 0 and self.c_q0 % q_lora == 0 and self.misc0 % LANES == 0
        self.g_beta = 4 * self.hw
        self.g_a = self.g_beta + N_HEADS
        self.g_cq = self.g_a + N_HEADS
        self.g_ckv = self.g_cq + q_lora
        self.g_kr = self.g_ckv + kv_lora
        assert self.g_kr + ROPE_DIM == in_width

    def w_in_local(self, w):
        pad = jnp.zeros(w.shape[:-1] + (self.h_width - self.in_width,), w.dtype)
        return jnp.concatenate([w[..., :self.g_beta], w[..., self.g_ckv:self.g_kr], w[..., self.g_cq:self.g_ckv],
                                w[..., self.g_kr:], w[..., self.g_beta:self.g_cq], pad], axis=-1)

    def w_in_global(self, d):
        m = self.misc0
        return jnp.concatenate([d[..., :self.c_kv0], d[..., m + MISC_BETA0:m + MISC_A0 + N_HEADS],
                                d[..., self.c_q0:self.misc0], d[..., self.c_kv0:self.c_q0], d[..., m:m + ROPE_DIM]],
                               axis=-1)

    def w_uq_local(self, w):
        r = w.reshape(w.shape[:-1] + (N_HEADS, HEAD_DIM + ROPE_DIM))
        r = jnp.pad(r, [(0, 0)] * (r.ndim - 1) + [(0, QK_DIM - HEAD_DIM - ROPE_DIM)])
        return r.reshape(w.shape[:-1] + (N_HEADS * QK_DIM,))

    def w_uq_global(self, d):
        r = d.reshape(d.shape[:-1] + (N_HEADS, QK_DIM))[..., :HEAD_DIM + ROPE_DIM]
        return r.reshape(d.shape[:-1] + (N_HEADS * (HEAD_DIM + ROPE_DIM),))


def _lane_padded(n):
    return -(-n // LANES) * LANES


def _pad_lanes(a):
    pad = _lane_padded(a.shape[-1]) - a.shape[-1]
    return a if pad == 0 else jnp.pad(a, [(0, 0)] * (a.ndim - 1) + [(0, pad)])


def _lane_row(vec, lane0):
    pad = LANES - lane0 - vec.shape[0]
    return jnp.concatenate([jnp.zeros((lane0,), F32), vec.astype(F32), jnp.zeros((pad,), F32)])[None, :]


def _layer_fwd(dm, alpha, x, xb, p_i, cos_t, sin_t, wl, tag):
    d, hw = dm.d, dm.hw
    nm = lambda s: f"{s}_{tag}"
    mm = functools.partial(_matmul, layer=wl["layer"])
    h = mm(xb, wl["w_in"], dims="nn", name=nm("f_in"))
    misc_cb = dm.misc0 // LANES

    u = _conv_fwd(h, wl["conv_w"], 3 * hw, name=nm("f_conv"))
    qn, kn, vg, gb = _rowwise(_gdn_act, [(u, 3 * hw, 0), (h, LANES, misc_cb)], [wl["alog_row"], wl["dtb_row"]],
                              [(hw, F32), (hw, F32), (hw, F32), (LANES, F32)], name=nm("f_gdn_act"))
    prep, tinv = _gdn_prep_fwd(qn, kn, vg, gb, name=nm("f_gdn_prep"))
    o_gdn, s_all = _gdn_seq_fwd(prep, name=nm("f_gdn_seq"))
    (og,) = _rowwise(lambda o, z, g: (_gdn_out(o, z, g),), [(o_gdn, hw, 0), (h, hw, 3)], [wl["gn_row"]],
                     [(hw, BF16)], name=nm("f_gdn_out"))

    cqn, ckvn = _rowwise(_mla_norm, [(h, dm.kv_lora, dm.c_kv0 // dm.kv_lora), (h, dm.q_lora, dm.c_q0 // dm.q_lora)],
                         [wl["kvg_row"], wl["qg_row"]], [(dm.q_lora, BF16), (dm.kv_lora, BF16)], name=nm("f_mla_norm"))
    qm = mm(cqn, wl["w_uq"], dims="nn", name=nm("f_uq"))
    kvm = mm(ckvn, wl["w_ukv"], dims="nn", name=nm("f_ukv"))
    scale = (HEAD_DIM + ROPE_DIM) ** -0.5
    qk_fn = functools.partial(_mla_qk, scale)
    qa, ka, va = _rowwise(qk_fn, [(qm, N_HEADS * QK_DIM, 0), (kvm, 2 * hw, 0), (h, LANES, misc_cb),
                                  (cos_t, LANES, 0), (sin_t, LANES, 0)], [],
                          [(N_HEADS * QK_DIM, BF16), (N_HEADS * QK_DIM, BF16), (hw, BF16)], name=nm("f_mla_qk"))
    kt = _blocked_transpose(ka, QK_DIM)
    v_heads = va.reshape(va.shape[0], N_HEADS, HEAD_DIM)
    v1 = jnp.concatenate([v_heads, jnp.ones_like(v_heads)], axis=2).reshape(va.shape[0], 2 * hw)
    o_mla, lse = _attn_fwd(qa, kt, v1, name=nm("f_attn"))

    om = jnp.concatenate([og, o_mla], axis=1)
    mix = mm(om, wl["w_out"], dims="nn", name=nm("f_out"))
    ln1 = lambda xv, yv, g, b: (_layer_norm(alpha * xv + yv, g, b),) * 2
    x1, x1b = _rowwise(ln1, [(x, d, 0), (mix, d, 0)], [wl["ln1_g"], wl["ln1_b"]], [(d, F32), (d, BF16)], name=nm("f_ln1"))

    gu = mm(x1b, wl["w_gate_up"], dims="nn", name=nm("f_gate_up"), out_dtype=BF16)
    (act,) = _rowwise(lambda g_: (_swiglu(g_),), [(gu, dm.ff2, 0)], [], [(dm.ff2 // 2, BF16)], name=nm("f_swiglu"))
    dn = mm(act, wl["w_down"], dims="nn", name=nm("f_down"))
    x2, x2b = _rowwise(ln1, [(x1, d, 0), (dn, d, 0)], [wl["ln2_g"], wl["ln2_b"]], [(d, F32), (d, BF16)], name=nm("f_ln2"))

    pg = mm(x2b, wl["w_ple_gate"], dims="nn", name=nm("f_ple_gate"))
    pe = mm(p_i, wl["w_ple"], dims="nn", name=nm("f_ple"))
    out, outb = _rowwise(lambda a, b, c_: (_ple_out(a, b, c_),) * 2, [(x2, d, 0), (pg, d, 0), (pe, d, 0)], [],
                         [(d, F32), (d, BF16)], name=nm("f_ple_out"))
    saved = dict(x=x, xb=xb, p_i=p_i, h=h, u=u, qn=qn, kn=kn, vg=vg, gb=gb, prep=prep, tinv=tinv, s_all=s_all, o_gdn=o_gdn, cqn=cqn, ckvn=ckvn,
                 qm=qm, kvm=kvm, qa=qa, ka=ka, va=va, o_mla=o_mla, lse=lse, om=om, mix=mix, x1=x1, x1b=x1b, gu=gu,
                 act=act, dn=dn, x2=x2, x2b=x2b, pg=pg, pe=pe)
    return out, outb, saved


def _layer_bwd(dm, alpha, dout, sv, cos_t, sin_t, wl, gbuf, tag):
    d, hw = dm.d, dm.hw
    t = dout.shape[0]
    nm = lambda s: f"{s}_{tag}"
    gr = {}
    gbuf = dict(gbuf)
    misc_cb = dm.misc0 // LANES
    mm = functools.partial(_matmul, layer=wl["layer"])

    def wgrad(name_, a, g):
        gbuf[name_] = mm(a, g, dims="tn", name=nm("b_" + name_), into=gbuf[name_], tm=1408, tn=1408, tk=1024)

    dx2_a, dpg, dpe = _rowwise(_vjp_fn(_ple_out, 3, 1), [(sv["x2"], d, 0), (sv["pg"], d, 0), (sv["pe"], d, 0), (dout, d, 0)],
                               [], [(d, F32), (d, BF16), (d, BF16)], name=nm("b_ple_out"))
    wgrad("w_ple", sv["p_i"], dpe)
    wgrad("w_ple_gate", sv["x2b"], dpg)
    dx2 = mm(dpg, wl["w_ple_gate"], dims="nt", c=dx2_a, name=nm("b_x2"))

    def ln_bwd(xv, yv, ct, g, b):
        _, pull = jax.vjp(lambda a_, b_, c_, d_: _layer_norm(alpha * a_ + b_, c_, d_), xv, yv, g, b)
        return pull(ct)

    dx1_a, ddn, gr["ln2_g"], gr["ln2_b"] = _rowwise(
        ln_bwd, [(sv["x1"], d, 0), (sv["dn"], d, 0), (dx2, d, 0)], [wl["ln2_g"], wl["ln2_b"]],
        [(d, F32), (d, BF16)], [(1, d), (1, d)], name=nm("b_ln2"))
    wgrad("w_down", sv["act"], ddn)
    dact = mm(ddn, wl["w_down"], dims="nt", name=nm("b_act"), out_dtype=BF16)
    (dgu,) = _rowwise(_vjp_fn(_swiglu, 1, 1), [(sv["gu"], dm.ff2, 0), (dact, dm.ff2 // 2, 0)], [], [(dm.ff2, BF16)],
                      name=nm("b_swiglu"))
    wgrad("w_gate_up", sv["x1b"], dgu)
    dx1 = mm(dgu, wl["w_gate_up"], dims="nt", c=dx1_a, name=nm("b_x1"))

    dx_a, dmix, gr["ln1_g"], gr["ln1_b"] = _rowwise(
        ln_bwd, [(sv["x"], d, 0), (sv["mix"], d, 0), (dx1, d, 0)], [wl["ln1_g"], wl["ln1_b"]],
        [(d, F32), (d, BF16)], [(1, d), (1, d)], name=nm("b_ln1"))
    wgrad("w_out", sv["om"], dmix)
    dom = mm(dmix, wl["w_out"], dims="nt", name=nm("b_om"))

    nq = t // min(ATT_TILE, t)
    delta = _attn_delta(dom, sv["o_mla"], name=nm("b_attn_delta"))
    lse_row = sv["lse"].reshape(N_HEADS, nq, 1, t // nq)
    delta_row = delta[:, :N_HEADS].T.reshape(N_HEADS, nq, 1, t // nq)
    do_b = dom[:, hw:].astype(BF16)
    dka, dva, dqa = _attn_bwd(sv["qa"], _blocked_transpose(sv["qa"], QK_DIM), sv["ka"], sv["va"], lse_row, delta_row,
                              do_b, _blocked_transpose(do_b, HEAD_DIM), name=nm("b_attn"))
    scale = (HEAD_DIM + ROPE_DIM) ** -0.5
    qk_fn = functools.partial(_mla_qk, scale)

    def qk_bwd(qm, kvm, misc, cs, sn, g_q, g_k, g_v):
        _, pull = jax.vjp(lambda a, b, c_: qk_fn(a, b, c_, cs, sn), qm, kvm, misc)
        return pull((g_q, g_k, g_v))

    dqm, dkvm, dmisc_rope = _rowwise(
        qk_bwd, [(sv["qm"], N_HEADS * QK_DIM, 0), (sv["kvm"], 2 * hw, 0), (sv["h"], LANES, misc_cb), (cos_t, LANES, 0),
                 (sin_t, LANES, 0), (dqa, N_HEADS * QK_DIM, 0), (dka, N_HEADS * QK_DIM, 0), (dva, hw, 0)], [],
        [(N_HEADS * QK_DIM, BF16), (2 * hw, BF16), (LANES, F32)], name=nm("b_mla_qk"))
    wgrad("w_uq", sv["cqn"], dqm)
    wgrad("w_ukv", sv["ckvn"], dkvm)
    dcqn = mm(dqm, wl["w_uq"], dims="nt", name=nm("b_cqn"))
    dckvn = mm(dkvm, wl["w_ukv"], dims="nt", name=nm("b_ckvn"))

    def norm_bwd(ckv, cq, g_q, g_kv, kvg, qg):
        _, pull = jax.vjp(_mla_norm, ckv, cq, kvg, qg)
        return pull((g_q, g_kv))

    dckv, dcq, gr["kvg_row"], gr["qg_row"] = _rowwise(
        norm_bwd, [(sv["h"], dm.kv_lora, dm.c_kv0 // dm.kv_lora), (sv["h"], dm.q_lora, dm.c_q0 // dm.q_lora),
                   (dcqn, dm.q_lora, 0), (dckvn, dm.kv_lora, 0)], [wl["kvg_row"], wl["qg_row"]],
        [(dm.kv_lora, BF16), (dm.q_lora, BF16)], [(1, dm.kv_lora), (1, dm.q_lora)], name=nm("b_mla_norm"))

    def gout_bwd(o, z, g_o, gn):
        _, pull = jax.vjp(_gdn_out, o, z, gn)
        return pull(g_o)

    do_gdn, dz, gr["gn_row"] = _rowwise(gout_bwd, [(sv["o_gdn"], hw, 0), (sv["h"], hw, 3), (dom, hw, 0)], [wl["gn_row"]],
                                        [(hw, F32), (hw, BF16)], [(1, HEAD_DIM)], name=nm("b_gdn_out"))
    dprep = _gdn_seq_bwd(sv["prep"], sv["s_all"], do_gdn, name=nm("b_gdn_seq"))
    dqn, dkn, dvg, dgb = _gdn_prep_bwd(sv["qn"], sv["kn"], sv["vg"], sv["gb"], sv["tinv"], dprep, name=nm("b_gdn_prep"))

    def act_bwd(u, misc, g_q, g_k, g_v, g_gb, g_rope, alog, dtb):
        _, pull = jax.vjp(_gdn_act, u, misc, alog, dtb)
        du_, dmisc_, dalog_, ddtb_ = pull((g_q, g_k, g_v, g_gb))
        return du_, dmisc_ + g_rope, dalog_, ddtb_

    du, dmisc, gr["alog_row"], gr["dtb_row"] = _rowwise(
        act_bwd, [(sv["u"], 3 * hw, 0), (sv["h"], LANES, misc_cb), (dqn, hw, 0), (dkn, hw, 0), (dvg, hw, 0),
                  (dgb, LANES, 0), (dmisc_rope, LANES, 0)], [wl["alog_row"], wl["dtb_row"]],
        [(3 * hw, F32), (LANES, BF16)], [(1, LANES), (1, LANES)], name=nm("b_gdn_act"))
    dqkv, dconv = _conv_bwd(du, sv["h"], wl["conv_w"], 3 * hw, name=nm("b_conv"))
    gr["conv_w"] = dconv[:CONV_W]

    dh = jnp.concatenate([dqkv, dz, dckv, dcq, dmisc], axis=1)
    wgrad("w_in", sv["xb"], dh)
    dx = mm(dh, wl["w_in"], dims="nt", c=dx_a, name=nm("b_x"), tk=1408)
    return dx, gbuf, gr


LOCAL_MATRICES = ("w_in", "w_uq", "w_ukv", "w_out", "w_gate_up", "w_down", "w_ple", "w_ple_gate")


def _layer_weights(mats, vecs, layer):
    wl = {n: mats[n] for n in LOCAL_MATRICES}
    wl["layer"] = layer
    wl["conv_w"] = mats["conv_w"][layer]
    wl["alog_row"] = _lane_row(vecs["a_log"][layer], MISC_A0)
    wl["dtb_row"] = _lane_row(vecs["dt_bias"][layer], MISC_A0)
    wl["gn_row"] = vecs["gdn_norm_g"][layer][None, :]
    wl["qg_row"] = vecs["q_norm_g"][layer][None, :]
    wl["kvg_row"] = vecs["kv_norm_g"][layer][None, :]
    for n in ("ln1_g", "ln1_b", "ln2_g", "ln2_b"):
        wl[n] = vecs[n][layer][None, :]
    return wl


def _vector_grads(gr):
    out = {"a_log": gr["alog_row"][0, MISC_A0:MISC_A0 + N_HEADS], "dt_bias": gr["dtb_row"][0, MISC_A0:MISC_A0 + N_HEADS],
           "gdn_norm_g": gr["gn_row"][0], "q_norm_g": gr["qg_row"][0], "kv_norm_g": gr["kvg_row"][0]}
    for n in ("ln1_g", "ln1_b", "ln2_g", "ln2_b"):
        out[n] = gr[n][0]
    return out


def _local_step(dm, x, p, positions, target, mats, vecs):
    depth = p.shape[0]
    alpha = (2.0 * depth) ** 0.25
    freq = ROPE_THETA ** (-jnp.arange(0, ROPE_DIM, 2, dtype=F32) / ROPE_DIM)
    inv_freq_row = _lane_row(jnp.concatenate([freq, freq]), 0)
    cos_t, sin_t = _rope_tables(positions.reshape(-1, 1), inv_freq_row, name="rope_tables")

    wls = [_layer_weights(mats, vecs, i) for i in range(depth)]
    saved = []
    cur, cur_b = x, x
    for i in range(depth):
        cur, cur_b, sv = _layer_fwd(dm, alpha, cur, cur_b, p[i], cos_t, sin_t, wls[i], f"l{i}")
        saved.append(sv)
    dy, loss_blk = _loss_head(cur, target)
    gbuf = {n: depth for n in LOCAL_MATRICES}
    conv_g, vec_g = [None] * depth, [None] * depth
    for i in reversed(range(depth)):
        dy, gbuf, gr = _layer_bwd(dm, alpha, dy, saved[i], cos_t, sin_t, wls[i], gbuf, f"l{i}")
        conv_g[i] = gr["conv_w"]
        vec_g[i] = _vector_grads(gr)
    vec_grads = {n: jnp.stack([vec_g[i][n] for i in range(depth)]) for n in VECTORS}
    return loss_blk[0, 0], dy, gbuf, jnp.stack(conv_g), vec_grads


def kernel(x, p, positions, w_in, conv_w, a_log, dt_bias, gdn_norm_g, q_norm_g, w_uq, kv_norm_g, w_ukv, w_out, ln1_g, ln1_b, w_gate_up, w_down, ln2_g, ln2_b, w_ple, w_ple_gate, loss_target, m_w_in, m_conv_w, m_a_log, m_dt_bias, m_gdn_norm_g, m_q_norm_g, m_w_uq, m_kv_norm_g, m_w_ukv, m_w_out, m_ln1_g, m_ln1_b, m_w_gate_up, m_w_down, m_ln2_g, m_ln2_b, m_w_ple, m_w_ple_gate, v_w_in, v_conv_w, v_a_log, v_dt_bias, v_gdn_norm_g, v_q_norm_g, v_w_uq, v_kv_norm_g, v_w_ukv, v_w_out, v_ln1_g, v_ln1_b, v_w_gate_up, v_w_down, v_ln2_g, v_ln2_b, v_w_ple, v_w_ple_gate):
    w = dict(w_in=w_in, conv_w=conv_w, a_log=a_log, dt_bias=dt_bias, gdn_norm_g=gdn_norm_g, q_norm_g=q_norm_g, w_uq=w_uq,
             kv_norm_g=kv_norm_g, w_ukv=w_ukv, w_out=w_out, ln1_g=ln1_g, ln1_b=ln1_b, w_gate_up=w_gate_up, w_down=w_down,
             ln2_g=ln2_g, ln2_b=ln2_b, w_ple=w_ple, w_ple_gate=w_ple_gate)
    m = dict(w_in=m_w_in, conv_w=m_conv_w, a_log=m_a_log, dt_bias=m_dt_bias, gdn_norm_g=m_gdn_norm_g, q_norm_g=m_q_norm_g,
             w_uq=m_w_uq, kv_norm_g=m_kv_norm_g, w_ukv=m_w_ukv, w_out=m_w_out, ln1_g=m_ln1_g, ln1_b=m_ln1_b,
             w_gate_up=m_w_gate_up, w_down=m_w_down, ln2_g=m_ln2_g, ln2_b=m_ln2_b, w_ple=m_w_ple, w_ple_gate=m_w_ple_gate)
    v = dict(w_in=v_w_in, conv_w=v_conv_w, a_log=v_a_log, dt_bias=v_dt_bias, gdn_norm_g=v_gdn_norm_g, q_norm_g=v_q_norm_g,
             w_uq=v_w_uq, kv_norm_g=v_kv_norm_g, w_ukv=v_w_ukv, w_out=v_w_out, ln1_g=v_ln1_g, ln1_b=v_ln1_b,
             w_gate_up=v_w_gate_up, w_down=v_w_down, ln2_g=v_ln2_g, ln2_b=v_ln2_b, w_ple=v_w_ple, w_ple_gate=v_w_ple_gate)
    depth = w_in.shape[0]
    assert depth % 2 == 0
    hd = depth // 2
    dm = _Dims(x.shape[2], N_CHIPS * w_in.shape[2], w_uq.shape[1], w_ukv.shape[1], N_CHIPS * w_gate_up.shape[2], p.shape[3])
    cx, cy, cc = lax.axis_index("x"), lax.axis_index("y"), lax.axis_index("c")
    chip = 2 * cx + cy

    g_streams = [_stream_of(n, w[n].shape[1:]) for n in MATRICES]
    shards = [w[n] if n == "conv_w" else w[n].astype(BF16) for n in MATRICES]
    shards = [_pad_lanes(s) if st.kind == "piece" else s for s, st in zip(shards, g_streams)]
    g_shapes = []
    for s, st in zip(shards, g_streams):
        if st.kind == "piece":
            shape = (N_CHIPS,) + s.shape
        elif st.kind == "rows":
            shape = (depth, N_CHIPS * s.shape[1], s.shape[2])
        else:
            shape = (depth, s.shape[1], N_CHIPS * s.shape[2])
        g_shapes.append(jax.ShapeDtypeStruct(shape, s.dtype))
    mats = dict(zip(MATRICES, _gather_chips(shards, g_streams, g_shapes, name="gather_weights")))
    for n, to_local in (("w_in", dm.w_in_local), ("w_uq", dm.w_uq_local)):
        pieces = jnp.moveaxis(mats[n][..., :w[n].shape[2]], 0, 2)
        mats[n] = to_local(pieces.reshape(pieces.shape[:2] + (-1,)))
    vecs = {n: w[n] for n in VECTORS}

    loss_local, grad_x, gbuf, conv_g, vec_g = _local_step(dm, x[0], p[:, 0], positions[0], loss_target[0], mats, vecs)
    loss = lax.psum(loss_local, ("x", "y", "c"))

    names = list(LOCAL_MATRICES) + ["conv_w", "vectors"]
    gs = [gbuf[n] for n in LOCAL_MATRICES] + [conv_g, _pack_vectors(vec_g, depth)]
    wire = [BF16] * len(LOCAL_MATRICES) + [F32, F32]
    r_streams = [_stream_of(n, w[n].shape[1:]) for n in LOCAL_MATRICES]
    r_streams += [_stream_of("conv_w", w["conv_w"].shape[1:]), _Stream("whole")]
    shard_shapes = [(hd, w[n].shape[1], _lane_padded(w[n].shape[2])) if st.kind == "piece" else (hd,) + w[n].shape[1:]
                    for n, st in zip(LOCAL_MATRICES, r_streams)]
    shard_shapes += [(hd,) + w["conv_w"].shape[1:], (hd, VEC_ROWS, LANES)]
    c_idx = cc.reshape(1).astype(jnp.int32)
    place = (chip.reshape(1).astype(jnp.int32), c_idx)
    from_sibling = _sibling_take_other_half(gs, name="reduce_sibling")
    chip_sum = [_add_own_half(g, a, c_idx, dt, name=f"reduce_add_{n}")
                for g, a, dt, n in zip(gs, from_sibling, wire, names)]
    for i, n in enumerate(names):
        if r_streams[i].kind == "piece":
            glob = dm.w_in_global(chip_sum[i]) if n == "w_in" else dm.w_uq_global(chip_sum[i])
            glob = glob.reshape(glob.shape[:2] + (N_CHIPS, glob.shape[2] // N_CHIPS))
            chip_sum[i] = jnp.moveaxis(_pad_lanes(glob), 2, 0)
    from_chips = _chips_exchange(chip_sum, r_streams, shard_shapes, name="reduce_chips")
    halves = [_sum_chips(ps, got, place, st, name=f"reduce_sum_{n}")
              for ps, got, st, n in zip(chip_sum, from_chips, r_streams, names)]
    joined = dict(zip(names, _sibling_join_halves(halves, name="reduce_join")))
    joined.update(_unpack_vectors(joined.pop("vectors"), {n: w[n].shape for n in VECTORS}))

    grad_w, delta_w, new_m, new_v = {}, {}, {}, {}
    for n in WEIGHTS:
        grad_w[n] = joined[n][..., :w[n].shape[-1]]
        delta_w[n], new_m[n], new_v[n] = _adamw(w[n], grad_w[n], m[n], v[n], name=f"adamw_{n}")
    return (loss, grad_x[None], *[grad_w[n] for n in WEIGHTS], *[delta_w[n] for n in WEIGHTS],
            *[new_m[n] for n in WEIGHTS], *[new_v[n] for n in WEIGHTS])
```

```python
import functools

import jax
import jax.numpy as jnp
from jax import lax
from jax.experimental import pallas as pl
from jax.experimental.pallas import tpu as pltpu

F32 = jnp.float32
BF16 = jnp.bfloat16
MESH = pl.DeviceIdType.MESH

CHUNK = 64
N_HEADS = 4
HEAD_DIM = 128
ROPE_DIM = 64
ROPE_THETA = 10000.0
LN_EPS = 1e-5
RMS_EPS = 1e-6
ADAM_LR, ADAM_B1, ADAM_B2, ADAM_EPS, ADAM_WD, ADAM_STEP = 0.001, 0.9, 0.999, 1e-08, 0.01, 10

LANES = 128
VMEM_LIMIT = 48 * 1024 * 1024
ROW_TILE = 512
WIDE_ROW_TILE = 256
WIDE_COLS = 2048
SUB_ROWS = 16
MAX_SUB_ROWS = 64
VREG_FILE_ELEMS = 64 * 8 * LANES

MISC_BETA0 = ROPE_DIM
MISC_A0 = ROPE_DIM + N_HEADS

NN = (((1,), (0,)), ((), ()))
NT = (((1,), (1,)), ((), ()))
TN = (((0,), (0,)), ((), ()))


def _params(sem=None):
    return pltpu.CompilerParams(dimension_semantics=sem, vmem_limit_bytes=VMEM_LIMIT)


def _divisor_tile(dim, target, unit):
    best = None
    t = unit
    while t <= min(dim, target):
        if dim % t == 0:
            best = t
        t += unit
    return best if best is not None else dim


BATCHED = {NN: (((2,), (1,)), ((0,), (0,))), NT: (((2,), (2,)), ((0,), (0,))), TN: (((1,), (1,)), ((0,), (0,)))}


def _make_dots():
    def raw(a, b, dims):
        if a.ndim == 3:
            dims = BATCHED[dims]
        return lax.dot_general(a.astype(BF16), b.astype(BF16), dims, preferred_element_type=F32)

    @jax.custom_vjp
    def nn(a, b):
        return raw(a, b, NN)

    @jax.custom_vjp
    def nt(a, b):
        return raw(a, b, NT)

    @jax.custom_vjp
    def tn(a, b):
        return raw(a, b, TN)

    nn.defvjp(lambda a, b: (raw(a, b, NN), (a, b)), lambda r, g: (nt(g, r[1]), tn(r[0], g)))
    nt.defvjp(lambda a, b: (raw(a, b, NT), (a, b)), lambda r, g: (nn(g, r[1]), tn(g, r[0])))
    tn.defvjp(lambda a, b: (raw(a, b, TN), (a, b)), lambda r, g: (nt(r[1], g), nn(r[0], g)))
    return nn, nt, tn


_nn, _nt, _tn = _make_dots()


def _matmul(a, b, *, dims, name, c=None, out_dtype=F32, tm=1024, tn=1408, tk=1408, layer=None, into=None):
    b_shape = b.shape[-2:]
    if dims == "nn":
        (m, k), (k2, n) = a.shape, b_shape
    elif dims == "nt":
        (m, k), (n, k2) = a.shape, b_shape
    else:
        (k, m), (k2, n) = a.shape, b_shape
    assert k == k2, (a.shape, b.shape, dims)
    tm = _divisor_tile(m, tm, LANES)
    tn = _divisor_tile(n, tn, LANES)
    tk = _divisor_tile(k, tk, LANES)
    nk = k // tk
    dn = {"nn": NN, "nt": NT, "tn": TN}[dims]
    if dims == "tn":
        a_spec = pl.BlockSpec((tk, tm), lambda i, j, kk: (kk, i))
    else:
        a_spec = pl.BlockSpec((tm, tk), lambda i, j, kk: (i, kk))
    b_blk, b_idx = ((tn, tk), lambda i, j, kk: (j, kk)) if dims == "nt" else ((tk, tn), lambda i, j, kk: (kk, j))
    if b.ndim == 3:
        b_spec = pl.BlockSpec((None,) + b_blk, lambda i, j, kk: (layer,) + b_idx(i, j, kk))
    else:
        b_spec = pl.BlockSpec(b_blk, b_idx)
    c_spec = pl.BlockSpec((tm, tn), lambda i, j, kk: (i, j))
    if isinstance(into, int):
        o_spec = pl.BlockSpec((None, tm, tn), lambda i, j, kk: (layer, i, j))
        out_shape = jax.ShapeDtypeStruct((into, m, n), out_dtype)
        into = None
    elif into is not None:
        assert into.shape[1:] == (m, n) and into.dtype == out_dtype
        o_spec = pl.BlockSpec((None, tm, tn), lambda i, j, kk: (layer, i, j))
        out_shape = jax.ShapeDtypeStruct(into.shape, into.dtype)
    else:
        o_spec = c_spec
        out_shape = jax.ShapeDtypeStruct((m, n), out_dtype)
    has_c = c is not None

    def body(*refs):
        a_ref, b_ref = refs[:2]
        c_ref = refs[2] if has_c else None
        o_ref, acc_ref = refs[-2:]
        kk = pl.program_id(2)

        @pl.when(kk == 0)
        def _():
            if has_c:
                acc_ref[...] = c_ref[...].astype(F32)
            else:
                acc_ref[...] = jnp.zeros_like(acc_ref)

        acc_ref[...] += lax.dot_general(a_ref[...].astype(BF16), b_ref[...].astype(BF16), dn,
                                        preferred_element_type=F32)

        @pl.when(kk == nk - 1)
        def _():
            o_ref[...] = acc_ref[...].astype(o_ref.dtype)

    ins = [a, b] + ([c] if has_c else [])
    specs = [a_spec, b_spec] + ([c_spec] if has_c else [])
    aliases = {}
    if into is not None:
        aliases = {len(ins): 0}
        ins.append(into)
        specs.append(pl.BlockSpec(memory_space=pl.ANY))
    return pl.pallas_call(
        body, name=name, grid=(m // tm, n // tn, nk), in_specs=specs, out_specs=o_spec, out_shape=out_shape,
        scratch_shapes=[pltpu.VMEM((tm, tn), F32)], input_output_aliases=aliases,
        compiler_params=_params(("arbitrary", "arbitrary", "arbitrary")),
    )(*ins)


def _rowwise(fn, rows, params, outs, accs=(), *, name):
    t = rows[0][0].shape[0]
    widest = max([w for _, w, _ in rows] + [w for w, _ in outs])
    tm = min(WIDE_ROW_TILE if widest > WIDE_COLS else ROW_TILE, t)
    sub = SUB_ROWS
    while sub < MAX_SUB_ROWS and 2 * sub * widest <= VREG_FILE_ELEMS:
        sub *= 2
    assert t % tm == 0 and tm % sub == 0
    n_rows, n_par, n_out, n_acc = len(rows), len(params), len(outs), len(accs)

    def body(*refs):
        row_refs = refs[:n_rows]
        par_refs = refs[n_rows:n_rows + n_par]
        out_refs = refs[n_rows + n_par:n_rows + n_par + n_out]
        acc_refs = refs[n_rows + n_par + n_out:]
        if n_acc:
            @pl.when(pl.program_id(0) == 0)
            def _():
                for a_ref in acc_refs:
                    a_ref[...] = jnp.zeros_like(a_ref)

        def step(r, carry):
            sl = pl.ds(pl.multiple_of(r * sub, sub), sub)
            vals = [ref[sl, :].astype(F32) for ref in row_refs] + [ref[...] for ref in par_refs]
            res = fn(*vals)
            for o_ref, val in zip(out_refs, res[:n_out]):
                o_ref[sl, :] = val.astype(o_ref.dtype)
            for a_ref, val in zip(acc_refs, res[n_out:]):
                a_ref[...] += val
            return carry

        lax.fori_loop(0, tm // sub, step, 0)

    in_specs = [pl.BlockSpec((tm, w), functools.partial(lambda i, cb: (i, cb), cb=cb)) for _, w, cb in rows]
    in_specs += [pl.BlockSpec(p.shape, lambda i: (0, 0)) for p in params]
    out_specs = [pl.BlockSpec((tm, w), lambda i: (i, 0)) for w, _ in outs]
    out_specs += [pl.BlockSpec(s, lambda i: (0, 0)) for s in accs]
    out_shape = [jax.ShapeDtypeStruct((t, w), d) for w, d in outs]
    out_shape += [jax.ShapeDtypeStruct(s, F32) for s in accs]
    return pl.pallas_call(
        body, name=name, grid=(t // tm,), in_specs=in_specs, out_specs=out_specs, out_shape=out_shape,
        compiler_params=_params(("arbitrary",)),
    )(*[r[0] for r in rows], *params)


def _vjp_fn(fn, n_in, n_out):
    def bwd(*args):
        ins, cts = args[:n_in], args[n_in:]
        _, pull = jax.vjp(fn, *ins)
        return pull(tuple(cts) if n_out > 1 else cts[0])
    return bwd


def _lane(shape):
    return lax.broadcasted_iota(jnp.int32, shape, 1)


def _silu(x):
    return x * jax.nn.sigmoid(x)


def _softplus(x):
    return jnp.maximum(x, 0.0) + jnp.log1p(jnp.exp(-jnp.abs(x)))


def _heads(x, width=HEAD_DIM):
    return [x[:, h * width:(h + 1) * width] for h in range(N_HEADS)]


def _layer_norm(z, g, b):
    mu = jnp.mean(z, -1, keepdims=True)
    zc = z - mu
    var = jnp.mean(zc * zc, -1, keepdims=True)
    return zc * lax.rsqrt(var + LN_EPS) * g + b


def _gdn_act(u, misc, alog_row, dtb_row):
    s = _silu(u)
    w = N_HEADS * HEAD_DIM
    q = jnp.concatenate([t * lax.rsqrt(jnp.sum(t * t, -1, keepdims=True) + RMS_EPS) * HEAD_DIM ** -0.5
                         for t in _heads(s[:, :w])], axis=1)
    k = jnp.concatenate([t * lax.rsqrt(jnp.sum(t * t, -1, keepdims=True) + RMS_EPS)
                         for t in _heads(s[:, w:2 * w])], axis=1)
    v = s[:, 2 * w:]
    lane = _lane(misc.shape)
    beta = jax.nn.sigmoid(misc)
    g = -jnp.exp(alog_row) * _softplus(misc + dtb_row)
    is_beta = (lane >= MISC_BETA0) & (lane < MISC_BETA0 + N_HEADS)
    is_g = (lane >= MISC_A0) & (lane < MISC_A0 + N_HEADS)
    gb = jnp.where(is_beta, beta, jnp.where(is_g, g, 0.0))
    return q, k, v, gb


def _gdn_out(o, z, gn_row):
    outs = []
    for oh, zh in zip(_heads(o), _heads(z)):
        r = oh * lax.rsqrt(jnp.mean(oh * oh, -1, keepdims=True) + RMS_EPS) * gn_row
        outs.append(r * _silu(zh))
    return jnp.concatenate(outs, axis=1)


def _mla_norm(ckv, cq, kvg_row, qg_row):
    cqn = cq * lax.rsqrt(jnp.mean(cq * cq, -1, keepdims=True) + RMS_EPS) * qg_row
    ckvn = ckv * lax.rsqrt(jnp.mean(ckv * ckv, -1, keepdims=True) + RMS_EPS) * kvg_row
    return cqn, ckvn


def _swap_halves(x):
    half = ROPE_DIM // 2
    return jnp.where(_lane(x.shape) < half, pltpu.roll(x, LANES - half, 1), pltpu.roll(x, half, 1))


@jax.custom_vjp
def _rope(x, cos_t, sin_t):
    return x * cos_t + _swap_halves(x) * sin_t


def _rope_fwd(x, cos_t, sin_t):
    return _rope(x, cos_t, sin_t), (cos_t, sin_t)


def _rope_bwd(res, g):
    cos_t, sin_t = res
    return g * cos_t - _swap_halves(g) * sin_t, jnp.zeros_like(cos_t), jnp.zeros_like(sin_t)


_rope.defvjp(_rope_fwd, _rope_bwd)


def _mla_qk(scale, qm, kv, misc, cos_t, sin_t):
    krope = _rope(misc, cos_t, sin_t)
    qs, ks = [], []
    for h in range(N_HEADS):
        base = 2 * HEAD_DIM * h
        qs += [qm[:, base:base + HEAD_DIM], _rope(qm[:, base + HEAD_DIM:base + 2 * HEAD_DIM], cos_t, sin_t)]
        ks += [kv[:, HEAD_DIM * h:HEAD_DIM * (h + 1)], krope]
    return jnp.concatenate(qs, axis=1) * scale, jnp.concatenate(ks, axis=1), kv[:, N_HEADS * HEAD_DIM:]


def _swiglu(gu):
    f = gu.shape[1] // 2
    return _silu(gu[:, :f]) * gu[:, f:]


def _ple_out(x2, pg, pe):
    return x2 + jax.nn.sigmoid(pg) * pe


CONV_W = 4
HALO = 8
CONV_STRIP = 512


def _conv_fwd(h, conv_w, width, *, name, tm=ROW_TILE, sub=32):
    t = h.shape[0]
    tm = min(tm, t)
    nb = tm // HALO

    def body(x_ref, halo_ref, w_ref, u_ref, buf):
        i = pl.program_id(0)
        buf[pl.ds(0, HALO), :] = jnp.where(i > 0, halo_ref[...], 0.0)
        buf[pl.ds(HALO, tm), :] = x_ref[...]
        for c0 in range(0, width, CONV_STRIP):
            cols = pl.ds(c0, CONV_STRIP)
            w = w_ref[:, cols]
            for r0 in range(0, tm, sub):
                acc = jnp.zeros((sub, CONV_STRIP), F32)
                for j in range(CONV_W):
                    acc = acc + w[j:j + 1, :] * buf[pl.ds(HALO + r0 - (CONV_W - 1) + j, sub), cols]
                u_ref[pl.ds(r0, sub), cols] = acc

    return pl.pallas_call(
        body, name=name, grid=(t // tm,),
        in_specs=[pl.BlockSpec((tm, width), lambda i: (i, 0)),
                  pl.BlockSpec((HALO, width), lambda i: (jnp.maximum(i * nb - 1, 0), 0)),
                  pl.BlockSpec(conv_w.shape, lambda i: (0, 0))],
        out_specs=pl.BlockSpec((tm, width), lambda i: (i, 0)),
        out_shape=jax.ShapeDtypeStruct((t, width), F32),
        scratch_shapes=[pltpu.VMEM((tm + HALO, width), F32)],
        compiler_params=_params(("arbitrary",)),
    )(h, h, conv_w)


def _conv_bwd(du, h, conv_w, width, *, name, tm=ROW_TILE, sub=32):
    t = h.shape[0]
    tm = min(tm, t)
    nb = tm // HALO
    n_tiles = t // tm

    def body(du_ref, du_halo, x_ref, x_halo, w_ref, dx_ref, dw_ref, dbuf, xbuf):
        i = pl.program_id(0)

        @pl.when(i == 0)
        def _():
            dw_ref[...] = jnp.zeros_like(dw_ref)

        dbuf[pl.ds(0, tm), :] = du_ref[...]
        dbuf[pl.ds(tm, HALO), :] = jnp.where(i < n_tiles - 1, du_halo[...], 0.0)
        xbuf[pl.ds(0, HALO), :] = jnp.where(i > 0, x_halo[...], 0.0)
        xbuf[pl.ds(HALO, tm), :] = x_ref[...]
        for c0 in range(0, width, CONV_STRIP):
            cols = pl.ds(c0, CONV_STRIP)
            w = w_ref[:, cols]
            dws = [jnp.zeros((HALO, CONV_STRIP), F32) for _ in range(CONV_W)]
            for r0 in range(0, tm, sub):
                acc = jnp.zeros((sub, CONV_STRIP), F32)
                d_here = dbuf[pl.ds(r0, sub), cols]
                for j in range(CONV_W):
                    acc = acc + w[j:j + 1, :] * dbuf[pl.ds(r0 + (CONV_W - 1) - j, sub), cols]
                    prod = d_here * xbuf[pl.ds(HALO + r0 - (CONV_W - 1) + j, sub), cols]
                    for g0 in range(0, sub, HALO):
                        dws[j] = dws[j] + prod[g0:g0 + HALO, :]
                dx_ref[pl.ds(r0, sub), cols] = acc.astype(dx_ref.dtype)
            for j in range(CONV_W):
                dw_ref[pl.ds(j, 1), cols] += jnp.sum(dws[j], axis=0, keepdims=True)

    return pl.pallas_call(
        body, name=name, grid=(n_tiles,),
        in_specs=[pl.BlockSpec((tm, width), lambda i: (i, 0)),
                  pl.BlockSpec((HALO, width), lambda i: (jnp.minimum((i + 1) * nb, t // HALO - 1), 0)),
                  pl.BlockSpec((tm, width), lambda i: (i, 0)),
                  pl.BlockSpec((HALO, width), lambda i: (jnp.maximum(i * nb - 1, 0), 0)),
                  pl.BlockSpec(conv_w.shape, lambda i: (0, 0))],
        out_specs=[pl.BlockSpec((tm, width), lambda i: (i, 0)),
                   pl.BlockSpec((HALO, width), lambda i: (0, 0))],
        out_shape=[jax.ShapeDtypeStruct((t, width), BF16), jax.ShapeDtypeStruct((HALO, width), F32)],
        scratch_shapes=[pltpu.VMEM((tm + HALO, width), F32), pltpu.VMEM((tm + HALO, width), F32)],
        compiler_params=_params(("arbitrary",)),
    )(du, du, h, h, conv_w)


@jax.custom_vjp
def _inv_unit_lower(low):
    n = low.shape[-1]
    eye = (lax.broadcasted_iota(jnp.int32, (n, n), 0) == lax.broadcasted_iota(jnp.int32, (n, n), 1)).astype(F32)
    x = eye - low
    p = low
    span = 2
    while span < n:
        p = _nn(p, p)
        x = x + _nn(x, p)
        span *= 2
    return x


def _inv_fwd(low):
    x = _inv_unit_lower(low)
    return x, x


def _inv_bwd(x, g):
    return (-_tn(x, _nt(g, x)),)


_inv_unit_lower.defvjp(_inv_fwd, _inv_bwd)


@jax.custom_vjp
def _inv_known(low, inverse):
    return inverse


_inv_known.defvjp(lambda low, inverse: (inverse, inverse), lambda x, g: (_inv_bwd(x, g)[0], jnp.zeros_like(x)))


def _gdn_prep(q, k, v, gb, known_inverse=None):
    c = CHUNK
    n = q.shape[0] // c
    pairs = [(g, h) for g in range(n) for h in range(N_HEADS)]
    row = lax.broadcasted_iota(jnp.int32, (c, c), 0)
    col = lax.broadcasted_iota(jnp.int32, (c, c), 1)
    tri_incl = row >= col
    tri_strict = row > col
    lane = _lane((c, LANES))
    sub = lax.broadcasted_iota(jnp.int32, (LANES, c), 0)
    last = lax.broadcasted_iota(jnp.int32, (c, 1), 0) == c - 1

    def split(x):
        return jnp.stack([x[g * c:(g + 1) * c, h * HEAD_DIM:(h + 1) * HEAD_DIM] for g, h in pairs])

    gbs = [gb[g * c:(g + 1) * c, :] for g in range(n)]
    gbts = [x.T for x in gbs]
    g_col = jnp.stack([jnp.sum(jnp.where(lane == MISC_A0 + h, gbs[g], 0.0), axis=1, keepdims=True) for g, h in pairs])
    b_col = jnp.stack([jnp.sum(jnp.where(lane == MISC_BETA0 + h, gbs[g], 0.0), axis=1, keepdims=True) for g, h in pairs])
    g_row = jnp.stack([jnp.sum(jnp.where(sub == MISC_A0 + h, gbts[g], 0.0), axis=0, keepdims=True) for g, h in pairs])
    gc_col = jnp.sum(jnp.where(tri_incl, g_row, 0.0), axis=2, keepdims=True)
    gc_row = jnp.sum(jnp.where(row <= col, g_col, 0.0), axis=1, keepdims=True)
    decay = jnp.where(tri_incl, jnp.exp(jnp.where(tri_incl, gc_col - gc_row, 0.0)), 0.0)
    g_last = jnp.sum(jnp.where(last, gc_col, 0.0), axis=1, keepdims=True)
    qs, ks, vs = split(q), split(k), split(v)
    kb = ks * b_col
    low = jnp.where(tri_strict, _nt(kb, ks) * decay, 0.0)
    if known_inverse is None:
        tinv = _inv_unit_lower(low)
    else:
        tinv = _inv_known(low, jnp.stack([known_inverse[g * c:(g + 1) * c, h * c:(h + 1) * c] for g, h in pairs]))
    eg = jnp.exp(gc_col)
    sol = _nn(tinv, jnp.concatenate([vs * b_col, kb * eg], axis=2))
    attn = jnp.where(tri_incl, _nt(qs, ks) * decay, 0.0)
    qd = qs * eg
    kd = ks * jnp.exp(g_last - gc_col)

    def merge(x):
        return jnp.concatenate([jnp.concatenate([x[g * N_HEADS + h] for h in range(N_HEADS)], axis=1)
                                for g in range(n)], axis=0)

    glb = jnp.concatenate([sum(jnp.where(lane == h, g_last[g * N_HEADS + h], 0.0) for h in range(N_HEADS))
                           for g in range(n)], axis=0)
    outs = (merge(sol[:, :, :HEAD_DIM]), merge(sol[:, :, HEAD_DIM:]), merge(qd), merge(kd), merge(attn), glb)
    return outs, merge(tinv)


def _gdn_seq(state, u, w, qd, kd, attn, glb):
    c = u.shape[0]
    first = lax.broadcasted_iota(jnp.int32, glb.shape, 0) == 0
    lane = _lane(glb.shape)
    heads = lambda x: jnp.stack([x[:, h * HEAD_DIM:(h + 1) * HEAD_DIM] for h in range(N_HEADS)])
    g_last = jnp.stack([jnp.sum(jnp.sum(jnp.where(first & (lane == h), glb, 0.0), axis=1, keepdims=True),
                                axis=0, keepdims=True) for h in range(N_HEADS)])
    s = jnp.stack([state[h * HEAD_DIM:(h + 1) * HEAD_DIM, :] for h in range(N_HEADS)])
    at = jnp.stack([attn[:, h * c:(h + 1) * c] for h in range(N_HEADS)])
    v_new = heads(u) - _nn(heads(w), s)
    o = _nn(heads(qd), s) + _nn(at, v_new)
    s_new = s * jnp.exp(g_last) + _tn(heads(kd), v_new)
    return (jnp.concatenate([o[h] for h in range(N_HEADS)], axis=1),
            jnp.concatenate([s_new[h] for h in range(N_HEADS)], axis=0))


PREP_CHUNKS = 8
PREP_CHUNKS_BWD = 4
SEQ_CHUNKS = 8


def _gdn_prep_fwd(q, k, v, gb, *, name):
    t, w = q.shape
    rows = min(PREP_CHUNKS * CHUNK, t)

    def body(q_ref, k_ref, v_ref, gb_ref, *out_refs):
        outs, inverse = _gdn_prep(q_ref[...], k_ref[...], v_ref[...], gb_ref[...])
        for o_ref, val in zip(out_refs, outs + (inverse,)):
            o_ref[...] = val

    spec = lambda width: pl.BlockSpec((rows, width), lambda i: (i, 0))
    widths = [w, w, w, w, N_HEADS * CHUNK, LANES, N_HEADS * CHUNK]
    res = pl.pallas_call(
        body, name=name, grid=(t // rows,),
        in_specs=[spec(w), spec(w), spec(w), spec(LANES)],
        out_specs=[spec(x) for x in widths],
        out_shape=[jax.ShapeDtypeStruct((t, x), F32) for x in widths],
        compiler_params=_params(("arbitrary",)),
    )(q, k, v, gb)
    return tuple(res[:6]), res[6]


def _gdn_prep_bwd(q, k, v, gb, inverse, cts, *, name):
    t, w = q.shape
    rows = min(PREP_CHUNKS_BWD * CHUNK, t)

    def body(q_ref, k_ref, v_ref, gb_ref, inv_ref, du, dw, dqd, dkd, dattn, dglb, dq_ref, dk_ref, dv_ref, dgb_ref):
        known = inv_ref[...]
        _, pull = jax.vjp(lambda a, b, c_, d_: _gdn_prep(a, b, c_, d_, known)[0],
                          q_ref[...], k_ref[...], v_ref[...], gb_ref[...])
        dq, dk, dv, dgb = pull(tuple(r[...] for r in (du, dw, dqd, dkd, dattn, dglb)))
        dq_ref[...] = dq
        dk_ref[...] = dk
        dv_ref[...] = dv
        dgb_ref[...] = dgb

    spec = lambda width: pl.BlockSpec((rows, width), lambda i: (i, 0))
    widths = [w, w, w, w, N_HEADS * CHUNK, LANES]
    return pl.pallas_call(
        body, name=name, grid=(t // rows,),
        in_specs=[spec(w), spec(w), spec(w), spec(LANES), spec(N_HEADS * CHUNK)] + [spec(x) for x in widths],
        out_specs=[spec(w), spec(w), spec(w), spec(LANES)],
        out_shape=[jax.ShapeDtypeStruct((t, w), F32)] * 3 + [jax.ShapeDtypeStruct((t, LANES), F32)],
        compiler_params=_params(("arbitrary",)),
    )(q, k, v, gb, inverse, *cts)


def _gdn_seq_fwd(prep, *, name):
    t, w = prep[0].shape
    rows = min(SEQ_CHUNKS * CHUNK, t)
    per = rows // CHUNK

    def body(u_ref, w_ref, qd_ref, kd_ref, at_ref, gl_ref, o_ref, sall_ref, s_scr):
        @pl.when(pl.program_id(0) == 0)
        def _():
            s_scr[...] = jnp.zeros_like(s_scr)

        def step(j, carry):
            sl = pl.ds(pl.multiple_of(j * CHUNK, CHUNK), CHUNK)
            s = s_scr[...]
            sall_ref[j] = s
            o, s_new = _gdn_seq(s, u_ref[sl, :], w_ref[sl, :], qd_ref[sl, :], kd_ref[sl, :], at_ref[sl, :], gl_ref[sl, :])
            o_ref[sl, :] = o
            s_scr[...] = s_new
            return carry

        lax.fori_loop(0, per, step, 0)

    spec = lambda width: pl.BlockSpec((rows, width), lambda i: (i, 0))
    widths = [w, w, w, w, N_HEADS * CHUNK, LANES]
    return pl.pallas_call(
        body, name=name, grid=(t // rows,),
        in_specs=[spec(x) for x in widths],
        out_specs=[spec(w), pl.BlockSpec((per, w, HEAD_DIM), lambda i: (i, 0, 0))],
        out_shape=[jax.ShapeDtypeStruct((t, w), F32), jax.ShapeDtypeStruct((t // CHUNK, w, HEAD_DIM), F32)],
        scratch_shapes=[pltpu.VMEM((w, HEAD_DIM), F32)],
        compiler_params=_params(("arbitrary",)),
    )(*prep)


def _gdn_seq_bwd(prep, s_all, do, *, name):
    t, w = prep[0].shape
    rows = min(SEQ_CHUNKS * CHUNK, t)
    per = rows // CHUNK
    n = t // rows

    def body(u_ref, w_ref, qd_ref, kd_ref, at_ref, gl_ref, sall_ref, do_ref, du, dw, dqd, dkd, dat, dgl, ds_scr):
        @pl.when(pl.program_id(0) == 0)
        def _():
            ds_scr[...] = jnp.zeros_like(ds_scr)

        def step(jj, carry):
            j = per - 1 - jj
            sl = pl.ds(pl.multiple_of(j * CHUNK, CHUNK), CHUNK)
            _, pull = jax.vjp(_gdn_seq, sall_ref[j], u_ref[sl, :], w_ref[sl, :], qd_ref[sl, :], kd_ref[sl, :],
                              at_ref[sl, :], gl_ref[sl, :])
            res = pull((do_ref[sl, :], ds_scr[...]))
            ds_scr[...] = res[0]
            for o_ref, val in zip((du, dw, dqd, dkd, dat, dgl), res[1:]):
                o_ref[sl, :] = val
            return carry

        lax.fori_loop(0, per, step, 0)

    spec = lambda width: pl.BlockSpec((rows, width), lambda i: (n - 1 - i, 0))
    widths = [w, w, w, w, N_HEADS * CHUNK, LANES]
    return pl.pallas_call(
        body, name=name, grid=(n,),
        in_specs=[spec(x) for x in widths] + [pl.BlockSpec((per, w, HEAD_DIM), lambda i: (n - 1 - i, 0, 0)), spec(w)],
        out_specs=[spec(x) for x in widths],
        out_shape=[jax.ShapeDtypeStruct((t, x), F32) for x in widths],
        scratch_shapes=[pltpu.VMEM((w, HEAD_DIM), F32)],
        compiler_params=_params(("arbitrary",)),
    )(*prep, s_all, do)


QK_DIM = 2 * HEAD_DIM
ATT_TILE = 1024
NEG = -1e30


ATT_SPLIT = 4


def _chunk_mask(n_rows, n_cols, key_major, query_offset):
    r = lax.broadcasted_iota(jnp.int32, (n_rows, n_cols), 0)
    c = lax.broadcasted_iota(jnp.int32, (n_rows, n_cols), 1)
    if key_major:
        return r // CHUNK <= (c + query_offset) // CHUNK
    return c // CHUNK <= (r + query_offset) // CHUNK


def _visible_keys(tile, diagonal):
    hq = tile // ATT_SPLIT
    return [(a + 1) * hq if diagonal else tile for a in range(ATT_SPLIT)]


def _dot_nn(a, b):
    return lax.dot_general(a, b, NN, preferred_element_type=F32)


def _blocked_transpose(x, width):
    t = x.shape[0]
    tile = min(ATT_TILE, t)
    return x.reshape(t // tile, tile, N_HEADS, width).transpose(2, 0, 3, 1)


def _attn_fwd(q, kt, v1, *, name):
    t = q.shape[0]
    tq = min(ATT_TILE, t)
    nq = t // tq

    def body(q_ref, kt_ref, v_ref, o_ref, lse_ref, m_scr, acc_scr):
        qi = pl.program_id(1)
        m_scr[...] = jnp.full_like(m_scr, NEG)
        acc_scr[...] = jnp.zeros_like(acc_scr)
        hq = tq // ATT_SPLIT
        parts = [pl.ds(a * hq, hq) for a in range(ATT_SPLIT)]
        qs = [q_ref[sl, :] for sl in parts]

        def step(kj, masked):
            rows = pl.ds(pl.multiple_of(kj * tq, tq), tq)
            kt_blk, vv = kt_ref[kj], v_ref[rows, :]
            seen = _visible_keys(tq, masked)
            ss = [_dot_nn(qv, kt_blk[:, :w]) for qv, w in zip(qs, seen)]
            for a, sl in enumerate(parts):
                s = ss[a]
                if masked:
                    s = jnp.where(_chunk_mask(hq, seen[a], False, a * hq), s, NEG)
                m_old = m_scr[sl, :]
                m_new = jnp.maximum(m_old, jnp.max(s, axis=1, keepdims=True))
                p = jnp.exp(s - m_new)
                acc_scr[sl, :] = jnp.exp(m_old - m_new) * acc_scr[sl, :] + _dot_nn(p.astype(BF16), vv[:seen[a], :])
                m_scr[sl, :] = m_new

        def loop_body(kj, carry):
            step(kj, False)
            return carry

        lax.fori_loop(0, qi, loop_body, 0)
        step(qi, True)
        acc = acc_scr[...]
        o_ref[...] = (acc[:, :HEAD_DIM] / acc[:, HEAD_DIM:]).astype(o_ref.dtype)
        lse_ref[...] = m_scr[...] + jnp.log(acc[:, HEAD_DIM:HEAD_DIM + 1])

    return pl.pallas_call(
        body, name=name, grid=(N_HEADS, nq),
        in_specs=[pl.BlockSpec((tq, QK_DIM), lambda h, i: (i, h)),
                  pl.BlockSpec((None, nq, QK_DIM, tq), lambda h, i: (h, 0, 0, 0)),
                  pl.BlockSpec((t, 2 * HEAD_DIM), lambda h, i: (0, h))],
        out_specs=[pl.BlockSpec((tq, HEAD_DIM), lambda h, i: (i, h)),
                   pl.BlockSpec((None, tq, 1), lambda h, i: (h, i, 0))],
        out_shape=[jax.ShapeDtypeStruct((t, N_HEADS * HEAD_DIM), BF16),
                   jax.ShapeDtypeStruct((N_HEADS, t, 1), F32)],
        scratch_shapes=[pltpu.VMEM((tq, 1), F32), pltpu.VMEM((tq, 2 * HEAD_DIM), F32)],
        compiler_params=_params(("arbitrary", "arbitrary")),
    )(q, kt, v1)


def _attn_delta(dom, o, *, name):
    hw = o.shape[1]

    def fn(do, ov):
        lane = _lane((do.shape[0], LANES))
        out = jnp.zeros((do.shape[0], LANES), F32)
        for h, (a, b) in enumerate(zip(_heads(do), _heads(ov))):
            out = out + jnp.where(lane == h, jnp.sum(a * b, axis=1, keepdims=True), 0.0)
        return (out,)

    return _rowwise(fn, [(dom, hw, 1), (o, hw, 0)], [], [(LANES, F32)], name=name)[0]


def _attn_bwd(q, qt, k, v, lse_row, delta_row, do, dot, *, name):
    t = q.shape[0]
    tk = min(ATT_TILE, t)
    nk = t // tk

    def body(q_ref, qt_ref, k_ref, v_ref, lse_ref, delta_ref, do_ref, dot_ref, dk_ref, dv_ref, dq_ref, dk_scr, dv_scr):
        kj = pl.program_id(1)

        @pl.when(kj == 0)
        def _():
            dq_ref[...] = jnp.zeros_like(dq_ref)

        dk_scr[...] = jnp.zeros_like(dk_scr)
        dv_scr[...] = jnp.zeros_like(dv_scr)
        kv_ = k_ref[...]
        vv = v_ref[...]
        hq = tk // ATT_SPLIT

        def step(qi, masked):
            lse_v, delta_v = lse_ref[qi], delta_ref[qi]
            qt_blk, dot_blk = qt_ref[qi], dot_ref[qi]
            rows = [pl.ds(pl.multiple_of(qi * tk + a * hq, hq), hq) for a in range(ATT_SPLIT)]
            qs = [q_ref[r, :] for r in rows]
            dos = [do_ref[r, :] for r in rows]
            seen = _visible_keys(tk, masked)
            ss = [_dot_nn(kv_[:seen[a], :], qt_blk[:, a * hq:(a + 1) * hq]) for a in range(ATT_SPLIT)]
            dps = [_dot_nn(vv[:seen[a], :], dot_blk[:, a * hq:(a + 1) * hq]) for a in range(ATT_SPLIT)]
            for a in range(ATT_SPLIT):
                cols = slice(a * hq, (a + 1) * hq)
                keys = pl.ds(0, seen[a])
                p = jnp.exp(ss[a] - lse_v[:, cols])
                if masked:
                    p = jnp.where(_chunk_mask(seen[a], hq, True, a * hq), p, 0.0)
                dv_scr[keys, :] += _dot_nn(p.astype(BF16), dos[a])
                ds = (p * (dps[a] - delta_v[:, cols])).astype(BF16)
                dk_scr[keys, :] += _dot_nn(ds, qs[a])
                dq_ref[rows[a], :] += lax.dot_general(ds, kv_[:seen[a], :], TN, preferred_element_type=F32)

        step(kj, True)

        def loop_body(qi, carry):
            step(qi, False)
            return carry

        lax.fori_loop(kj + 1, nk, loop_body, 0)
        dk_ref[...] = dk_scr[...].astype(dk_ref.dtype)
        dv_ref[...] = dv_scr[...].astype(dv_ref.dtype)

    once = dict(pipeline_mode=pl.Buffered(1))
    stat = pl.BlockSpec((None, nk, 1, tk), lambda h, j: (h, 0, 0, 0))
    return pl.pallas_call(
        body, name=name, grid=(N_HEADS, nk),
        in_specs=[pl.BlockSpec((t, QK_DIM), lambda h, j: (0, h), **once),
                  pl.BlockSpec((None, nk, QK_DIM, tk), lambda h, j: (h, 0, 0, 0), **once),
                  pl.BlockSpec((tk, QK_DIM), lambda h, j: (j, h)),
                  pl.BlockSpec((tk, HEAD_DIM), lambda h, j: (j, h)),
                  stat, stat,
                  pl.BlockSpec((t, HEAD_DIM), lambda h, j: (0, h), **once),
                  pl.BlockSpec((None, nk, HEAD_DIM, tk), lambda h, j: (h, 0, 0, 0), **once)],
        out_specs=[pl.BlockSpec((tk, QK_DIM), lambda h, j: (j, h)),
                   pl.BlockSpec((tk, HEAD_DIM), lambda h, j: (j, h)),
                   pl.BlockSpec((t, QK_DIM), lambda h, j: (0, h))],
        out_shape=[jax.ShapeDtypeStruct((t, N_HEADS * QK_DIM), BF16),
                   jax.ShapeDtypeStruct((t, N_HEADS * HEAD_DIM), BF16),
                   jax.ShapeDtypeStruct((t, N_HEADS * QK_DIM), F32)],
        scratch_shapes=[pltpu.VMEM((tk, QK_DIM), F32), pltpu.VMEM((tk, HEAD_DIM), F32)],
        compiler_params=_params(("arbitrary", "arbitrary")),
    )(q, qt, k, v, lse_row, delta_row, do, dot)


def _rope_tables(pos_col, inv_freq_row, *, name):
    t = pos_col.shape[0]
    tm = min(ROW_TILE, t)

    def body(p_ref, f_ref, c_ref, s_ref):
        ang = p_ref[...].astype(F32) * f_ref[...]
        lane = _lane(ang.shape)
        c_ref[...] = jnp.where(lane < ROPE_DIM, jnp.cos(ang), 0.0)
        sn = jnp.sin(ang)
        s_ref[...] = jnp.where(lane < ROPE_DIM // 2, -sn, jnp.where(lane < ROPE_DIM, sn, 0.0))

    out = pl.BlockSpec((tm, LANES), lambda i: (i, 0))
    return pl.pallas_call(
        body, name=name, grid=(t // tm,),
        in_specs=[pl.BlockSpec((tm, 1), lambda i: (i, 0)), pl.BlockSpec((1, LANES), lambda i: (0, 0))],
        out_specs=[out, out], out_shape=[jax.ShapeDtypeStruct((t, LANES), F32)] * 2,
        compiler_params=_params(("arbitrary",)),
    )(pos_col, inv_freq_row)


def _loss_head(y, target):
    width = y.shape[1]

    def fn(yv, tv):
        e = yv - tv
        part = 0.5 * jnp.sum(jnp.mean(e * e, axis=1, keepdims=True), axis=0, keepdims=True)
        return e * (1.0 / width), jnp.broadcast_to(part, (HALO, LANES))

    return _rowwise(fn, [(y, width, 0), (target, width, 0)], [], [(width, F32)], [(HALO, LANES)], name="loss_head")


def _adamw(w, g, m, v, *, name):
    shape = w.shape
    w2, g2, m2, v2 = (a.reshape(-1, shape[-1]) for a in (w, g, m, v))
    rows, width = w2.shape
    tr = _divisor_tile(rows, max(8, (1 << 19) // max(width, 1)), 8)
    bc1 = 1.0 - ADAM_B1 ** ADAM_STEP
    bc2 = 1.0 - ADAM_B2 ** ADAM_STEP

    def body(w_ref, g_ref, m_ref, v_ref, d_ref, mo_ref, vo_ref):
        gv = g_ref[...]
        mn = ADAM_B1 * m_ref[...] + (1.0 - ADAM_B1) * gv
        vn = ADAM_B2 * v_ref[...] + (1.0 - ADAM_B2) * (gv * gv)
        d_ref[...] = -ADAM_LR * ((mn / bc1) / (jnp.sqrt(vn / bc2) + ADAM_EPS) + ADAM_WD * w_ref[...])
        mo_ref[...] = mn
        vo_ref[...] = vn

    spec = pl.BlockSpec((tr, width), lambda i: (i, 0))
    outs = pl.pallas_call(
        body, name=name, grid=(rows // tr,), in_specs=[spec] * 4, out_specs=[spec] * 3,
        out_shape=[jax.ShapeDtypeStruct((rows, width), F32)] * 3,
        compiler_params=_params(("arbitrary",)),
    )(w2, g2, m2, v2)
    return tuple(o.reshape(shape) for o in outs)


HBM_SPEC = pl.BlockSpec(memory_space=pltpu.HBM)


def _position():
    return lax.axis_index("x"), lax.axis_index("y"), lax.axis_index("c")


def _other_chips(x, y):
    return [(1 - x, y), (x, 1 - y), (1 - x, 1 - y)]


class _Stream:
    def __init__(self, kind, size=0):
        self.kind, self.size = kind, size
        self.parts = 2 if kind == "heads" else 1

    def local(self, ref, k, part):
        if self.kind == "rows":
            return ref.at[:, pl.ds(k * self.size, self.size), :]
        if self.kind == "cols":
            return ref.at[:, :, pl.ds(k * self.size, self.size)]
        if self.kind == "heads":
            return ref.at[:, :, pl.ds(part * N_HEADS * HEAD_DIM + k * HEAD_DIM, HEAD_DIM)]
        if self.kind == "piece":
            return ref.at[k]
        return ref

    def shard(self, ref, part):
        if self.kind == "heads":
            return ref.at[:, :, pl.ds(part * HEAD_DIM, HEAD_DIM)]
        return ref

    def half_local(self, ref, k, part, cc, hd):
        if self.kind == "piece":
            return ref.at[k, pl.ds(cc * hd, hd)]
        return self.local(ref.at[pl.ds(cc * hd, hd)], k, part)


def _remote(src, dst, send_sems, recv_sems, idx, to):
    return pltpu.make_async_remote_copy(src_ref=src, dst_ref=dst, send_sem=send_sems.at[idx],
                                        recv_sem=recv_sems.at[idx], device_id=to, device_id_type=MESH)


def _comm_call(body, ins, out_shapes, n_remote, n_local, *, name):
    scratch = [pltpu.SemaphoreType.DMA((n_remote,)), pltpu.SemaphoreType.DMA((n_remote,))]
    if n_local:
        scratch.append(pltpu.SemaphoreType.DMA((n_local,)))
    return pl.pallas_call(
        body, name=name, in_specs=[HBM_SPEC] * len(ins), out_specs=[HBM_SPEC] * len(out_shapes), out_shape=out_shapes,
        scratch_shapes=scratch, compiler_params=pltpu.CompilerParams(has_side_effects=True),
    )(*ins)


def _gather_chips(shards, streams, out_shapes, *, name):
    n = len(shards)
    hd = shards[0].shape[0] // 2
    flat = [(t, part) for t in range(n) for part in range(streams[t].parts)]
    ns = len(flat)

    def body(*refs):
        s_refs, o_refs = refs[:n], refs[n:2 * n]
        send_sems, recv_sems = refs[2 * n:]
        x, y, c = _position()
        sibling = (x, y, 1 - c)
        chips = _other_chips(x, y)
        me = 2 * x + y
        sent = []
        for s, (t, part) in enumerate(flat):
            st = streams[t]
            sent.append(_remote(st.shard(s_refs[t], part), st.local(o_refs[t], me, part), send_sems, recv_sems,
                                6 * ns + s, sibling))
            sent[-1].start()
            src = st.shard(s_refs[t].at[pl.ds(c * hd, hd)], part)
            for j, (cx, cy) in enumerate(chips):
                sent.append(_remote(src, st.half_local(o_refs[t], me, part, c, hd), send_sems, recv_sems,
                                    3 * s + j, (cx, cy, c)))
                sent[-1].start()
        for s, (t, part) in enumerate(flat):
            st = streams[t]
            for j, (cx, cy) in enumerate(chips):
                blk = st.half_local(o_refs[t], 2 * cx + cy, part, c, hd)
                _remote(blk, blk, send_sems, recv_sems, 3 * s + j, (x, y, c)).wait_recv()
                sent.append(_remote(blk, blk, send_sems, recv_sems, 3 * ns + 3 * s + j, sibling))
                sent[-1].start()
        for s, (t, part) in enumerate(flat):
            st = streams[t]
            for j, (cx, cy) in enumerate(chips):
                blk = st.half_local(o_refs[t], 2 * cx + cy, part, 1 - c, hd)
                _remote(blk, blk, send_sems, recv_sems, 3 * ns + 3 * s + j, (x, y, c)).wait_recv()
            own = st.local(o_refs[t], me, part)
            _remote(own, own, send_sems, recv_sems, 6 * ns + s, (x, y, c)).wait_recv()
        for cp in sent:
            cp.wait_send()

    return _comm_call(body, shards, out_shapes, 7 * ns, 0, name=name)


def _sibling_take_other_half(gs, *, name):
    n = len(gs)
    hd = gs[0].shape[0] // 2

    def body(*refs):
        g_refs, o_refs = refs[:n], refs[n:2 * n]
        send_sems, recv_sems = refs[2 * n:]
        x, y, c = _position()
        copies = [_remote(g_refs[t].at[pl.ds((1 - c) * hd, hd)], o_refs[t], send_sems, recv_sems, t, (x, y, 1 - c))
                  for t in range(n)]
        for cp in copies:
            cp.start()
        for cp in copies:
            cp.wait()

    outs = [jax.ShapeDtypeStruct((hd,) + g.shape[1:], g.dtype) for g in gs]
    return _comm_call(body, gs, outs, n, 0, name=name)


def _chips_exchange(ps, streams, shard_shapes, *, name):
    n = len(ps)
    flat = [(t, part) for t in range(n) for part in range(streams[t].parts)]

    def body(*refs):
        p_refs, o_refs = refs[:n], refs[n:2 * n]
        send_sems, recv_sems = refs[2 * n:]
        x, y, c = _position()
        copies = []
        for s, (t, part) in enumerate(flat):
            st = streams[t]
            for j, (cx, cy) in enumerate(_other_chips(x, y)):
                copies.append(_remote(st.local(p_refs[t], 2 * cx + cy, part), st.shard(o_refs[t].at[j], part),
                                      send_sems, recv_sems, 3 * s + j, (cx, cy, c)))
        for cp in copies:
            cp.start()
        for cp in copies:
            cp.wait()

    outs = [jax.ShapeDtypeStruct((3,) + tuple(shp), p.dtype) for p, shp in zip(ps, shard_shapes)]
    return _comm_call(body, ps, outs, 3 * len(flat), 0, name=name)


def _sibling_join_halves(bufs, *, name):
    n = len(bufs)
    hd = bufs[0].shape[0] // 2

    def body(*refs):
        o_refs = refs[n:2 * n]
        send_sems, recv_sems = refs[2 * n:]
        x, y, c = _position()
        sent = []
        for t in range(n):
            mine = o_refs[t].at[pl.ds(c * hd, hd)]
            sent.append(_remote(mine, mine, send_sems, recv_sems, t, (x, y, 1 - c)))
            sent[-1].start()
        for t in range(n):
            theirs = o_refs[t].at[pl.ds((1 - c) * hd, hd)]
            _remote(theirs, theirs, send_sems, recv_sems, t, (x, y, c)).wait_recv()
        for cp in sent:
            cp.wait_send()

    return pl.pallas_call(
        body, name=name, in_specs=[HBM_SPEC] * n, out_specs=[HBM_SPEC] * n,
        out_shape=[jax.ShapeDtypeStruct(b.shape, b.dtype) for b in bufs],
        scratch_shapes=[pltpu.SemaphoreType.DMA((n,)), pltpu.SemaphoreType.DMA((n,))],
        input_output_aliases={t: t for t in range(n)},
        compiler_params=pltpu.CompilerParams(has_side_effects=True),
    )(*bufs)


def _row_tile(rows, width):
    return _divisor_tile(rows, max(16, (1 << 19) // width), 16)


def _add_own_half(g, got, c_idx, out_dtype, *, name):
    hd, r, w = got.shape
    tr = _row_tile(r, w)

    def body(c_ref, g_ref, a_ref, o_ref):
        o_ref[...] = (g_ref[...] + a_ref[...]).astype(o_ref.dtype)

    return pl.pallas_call(
        body, name=name,
        grid_spec=pltpu.PrefetchScalarGridSpec(
            num_scalar_prefetch=1, grid=(hd, r // tr),
            in_specs=[pl.BlockSpec((None, None, tr, w), lambda l, i, c_ref: (c_ref[0], l, i, 0)),
                      pl.BlockSpec((None, tr, w), lambda l, i, c_ref: (l, i, 0))],
            out_specs=pl.BlockSpec((None, tr, w), lambda l, i, c_ref: (l, i, 0))),
        out_shape=jax.ShapeDtypeStruct((hd, r, w), out_dtype),
        compiler_params=_params(("arbitrary", "arbitrary")),
    )(c_idx, g.reshape((2, hd) + g.shape[1:]), got)


def _sum_chips(p, got, place, stream, *, name):
    _, hd, rs, cs = got.shape
    wb = HEAD_DIM if stream.kind == "heads" else cs
    tr = _row_tile(rs, wb)
    kind, size = stream.kind, stream.size

    def own_index(l, i, g, k_ref, c_ref):
        k = k_ref[0]
        if kind == "rows":
            return (l, k * (size // tr) + i, 0)
        if kind == "cols":
            return (l, i, k)
        if kind == "heads":
            return (l, i, g * N_HEADS + k)
        if kind == "piece":
            return (k, l, i, 0)
        return (l, i, 0)

    own_blk = (None, None, tr, wb) if kind == "piece" else (None, tr, wb)

    def body(k_ref, c_ref, p_ref, fx_ref, fy_ref, fxy_ref, o_ref):
        f = lambda r: r[...].astype(F32)
        o_ref[...] = (f(p_ref) + f(fy_ref)) + (f(fx_ref) + f(fxy_ref))

    def rel(j):
        return pl.BlockSpec((None, None, tr, wb), functools.partial(lambda l, i, g, k_ref, c_ref, j: (j, l, i, g), j=j))

    return pl.pallas_call(
        body, name=name,
        grid_spec=pltpu.PrefetchScalarGridSpec(
            num_scalar_prefetch=2, grid=(hd, rs // tr, stream.parts),
            in_specs=[pl.BlockSpec(own_blk, own_index), rel(0), rel(1), rel(2)],
            out_specs=pl.BlockSpec((None, tr, wb), lambda l, i, g, k_ref, c_ref: (c_ref[0] * hd + l, i, g))),
        out_shape=jax.ShapeDtypeStruct((2 * hd, rs, cs), F32),
        compiler_params=_params(("arbitrary", "arbitrary", "arbitrary")),
    )(place[0], place[1], p, got, got, got)


MATRICES = ("w_in", "w_uq", "w_ukv", "w_out", "w_gate_up", "w_down", "w_ple", "w_ple_gate", "conv_w")
VECTORS = ("a_log", "dt_bias", "gdn_norm_g", "q_norm_g", "kv_norm_g", "ln1_g", "ln1_b", "ln2_g", "ln2_b")
WEIGHTS = ("w_in", "conv_w", "a_log", "dt_bias", "gdn_norm_g", "q_norm_g", "w_uq", "kv_norm_g", "w_ukv", "w_out",
           "ln1_g", "ln1_b", "w_gate_up", "w_down", "ln2_g", "ln2_b", "w_ple", "w_ple_gate")
ROW_SHARDED = ("w_out", "w_down", "w_ple_gate")
N_CHIPS = 4


def _stream_of(name, shard_shape):
    if name in ("w_in", "w_uq"):
        return _Stream("piece")
    if name == "w_ukv":
        return _Stream("heads")
    if name in ROW_SHARDED:
        return _Stream("rows", shard_shape[0])
    return _Stream("cols", shard_shape[1])


def _pack_vectors(vecs, depth):
    flat = jnp.concatenate([vecs[n].reshape(depth, -1) for n in VECTORS], axis=1)
    pad = jnp.zeros((depth, VEC_ROWS * LANES - flat.shape[1]), F32)
    return jnp.concatenate([flat, pad], axis=1).reshape(depth, VEC_ROWS, LANES)


def _unpack_vectors(packed, shapes):
    depth = packed.shape[0]
    flat = packed.reshape(depth, VEC_ROWS * LANES)
    out, off = {}, 0
    for n in VECTORS:
        out[n] = flat[:, off:off + shapes[n][1]]
        off += shapes[n][1]
    return out


VEC_ROWS = 40


class _Dims:
    def __init__(self, d_model, in_width, q_lora, kv_lora, d_ff2, ple_dim):
        self.d = d_model
        self.hw = N_HEADS * HEAD_DIM
        self.in_width = in_width
        self.q_lora, self.kv_lora = q_lora, kv_lora
        self.ff2 = d_ff2
        self.ple = ple_dim
        self.c_kv0 = 4 * self.hw
        self.c_q0 = self.c_kv0 + kv_lora
        self.misc0 = self.c_q0 + q_lora
        self.h_width = self.misc0 + LANES
        assert self.c_kv0 % kv_lora == 0 and self.c_q0 % q_lora == 0 and self.misc0 % LANES == 0
        self.g_beta = 4 * self.hw
        self.g_a = self.g_beta + N_HEADS
        self.g_cq = self.g_a + N_HEADS
        self.g_ckv = self.g_cq + q_lora
        self.g_kr = self.g_ckv + kv_lora
        assert self.g_kr + ROPE_DIM == in_width

    def w_in_local(self, w):
        pad = jnp.zeros(w.shape[:-1] + (self.h_width - self.in_width,), w.dtype)
        return jnp.concatenate([w[..., :self.g_beta], w[..., self.g_ckv:self.g_kr], w[..., self.g_cq:self.g_ckv],
                                w[..., self.g_kr:], w[..., self.g_beta:self.g_cq], pad], axis=-1)

    def w_in_global(self, d):
        m = self.misc0
        return jnp.concatenate([d[..., :self.c_kv0], d[..., m + MISC_BETA0:m + MISC_A0 + N_HEADS],
                                d[..., self.c_q0:self.misc0], d[..., self.c_kv0:self.c_q0], d[..., m:m + ROPE_DIM]],
                               axis=-1)

    def w_uq_local(self, w):
        r = w.reshape(w.shape[:-1] + (N_HEADS, HEAD_DIM + ROPE_DIM))
        r = jnp.pad(r, [(0, 0)] * (r.ndim - 1) + [(0, QK_DIM - HEAD_DIM - ROPE_DIM)])
        return r.reshape(w.shape[:-1] + (N_HEADS * QK_DIM,))

    def w_uq_global(self, d):
        r = d.reshape(d.shape[:-1] + (N_HEADS, QK_DIM))[..., :HEAD_DIM + ROPE_DIM]
        return r.reshape(d.shape[:-1] + (N_HEADS * (HEAD_DIM + ROPE_DIM),))


def _lane_padded(n):
    return -(-n // LANES) * LANES


def _pad_lanes(a):
    pad = _lane_padded(a.shape[-1]) - a.shape[-1]
    return a if pad == 0 else jnp.pad(a, [(0, 0)] * (a.ndim - 1) + [(0, pad)])


def _lane_row(vec, lane0):
    pad = LANES - lane0 - vec.shape[0]
    return jnp.concatenate([jnp.zeros((lane0,), F32), vec.astype(F32), jnp.zeros((pad,), F32)])[None, :]


def _layer_fwd(dm, alpha, x, xb, p_i, cos_t, sin_t, wl, tag):
    d, hw = dm.d, dm.hw
    nm = lambda s: f"{s}_{tag}"
    mm = functools.partial(_matmul, layer=wl["layer"])
    h = mm(xb, wl["w_in"], dims="nn", name=nm("f_in"))
    misc_cb = dm.misc0 // LANES

    u = _conv_fwd(h, wl["conv_w"], 3 * hw, name=nm("f_conv"))
    qn, kn, vg, gb = _rowwise(_gdn_act, [(u, 3 * hw, 0), (h, LANES, misc_cb)], [wl["alog_row"], wl["dtb_row"]],
                              [(hw, F32), (hw, F32), (hw, F32), (LANES, F32)], name=nm("f_gdn_act"))
    prep, tinv = _gdn_prep_fwd(qn, kn, vg, gb, name=nm("f_gdn_prep"))
    o_gdn, s_all = _gdn_seq_fwd(prep, name=nm("f_gdn_seq"))
    (og,) = _rowwise(lambda o, z, g: (_gdn_out(o, z, g),), [(o_gdn, hw, 0), (h, hw, 3)], [wl["gn_row"]],
                     [(hw, BF16)], name=nm("f_gdn_out"))

    cqn, ckvn = _rowwise(_mla_norm, [(h, dm.kv_lora, dm.c_kv0 // dm.kv_lora), (h, dm.q_lora, dm.c_q0 // dm.q_lora)],
                         [wl["kvg_row"], wl["qg_row"]], [(dm.q_lora, BF16), (dm.kv_lora, BF16)], name=nm("f_mla_norm"))
    qm = mm(cqn, wl["w_uq"], dims="nn", name=nm("f_uq"))
    kvm = mm(ckvn, wl["w_ukv"], dims="nn", name=nm("f_ukv"))
    scale = (HEAD_DIM + ROPE_DIM) ** -0.5
    qk_fn = functools.partial(_mla_qk, scale)
    qa, ka, va = _rowwise(qk_fn, [(qm, N_HEADS * QK_DIM, 0), (kvm, 2 * hw, 0), (h, LANES, misc_cb),
                                  (cos_t, LANES, 0), (sin_t, LANES, 0)], [],
                          [(N_HEADS * QK_DIM, BF16), (N_HEADS * QK_DIM, BF16), (hw, BF16)], name=nm("f_mla_qk"))
    kt = _blocked_transpose(ka, QK_DIM)
    v_heads = va.reshape(va.shape[0], N_HEADS, HEAD_DIM)
    v1 = jnp.concatenate([v_heads, jnp.ones_like(v_heads)], axis=2).reshape(va.shape[0], 2 * hw)
    o_mla, lse = _attn_fwd(qa, kt, v1, name=nm("f_attn"))

    om = jnp.concatenate([og, o_mla], axis=1)
    mix = mm(om, wl["w_out"], dims="nn", name=nm("f_out"))
    ln1 = lambda xv, yv, g, b: (_layer_norm(alpha * xv + yv, g, b),) * 2
    x1, x1b = _rowwise(ln1, [(x, d, 0), (mix, d, 0)], [wl["ln1_g"], wl["ln1_b"]], [(d, F32), (d, BF16)], name=nm("f_ln1"))

    gu = mm(x1b, wl["w_gate_up"], dims="nn", name=nm("f_gate_up"), out_dtype=BF16)
    (act,) = _rowwise(lambda g_: (_swiglu(g_),), [(gu, dm.ff2, 0)], [], [(dm.ff2 // 2, BF16)], name=nm("f_swiglu"))
    dn = mm(act, wl["w_down"], dims="nn", name=nm("f_down"))
    x2, x2b = _rowwise(ln1, [(x1, d, 0), (dn, d, 0)], [wl["ln2_g"], wl["ln2_b"]], [(d, F32), (d, BF16)], name=nm("f_ln2"))

    pg = mm(x2b, wl["w_ple_gate"], dims="nn", name=nm("f_ple_gate"))
    pe = mm(p_i, wl["w_ple"], dims="nn", name=nm("f_ple"))
    out, outb = _rowwise(lambda a, b, c_: (_ple_out(a, b, c_),) * 2, [(x2, d, 0), (pg, d, 0), (pe, d, 0)], [],
                         [(d, F32), (d, BF16)], name=nm("f_ple_out"))
    saved = dict(x=x, xb=xb, p_i=p_i, h=h, u=u, qn=qn, kn=kn, vg=vg, gb=gb, prep=prep, tinv=tinv, s_all=s_all, o_gdn=o_gdn, cqn=cqn, ckvn=ckvn,
                 qm=qm, kvm=kvm, qa=qa, ka=ka, va=va, o_mla=o_mla, lse=lse, om=om, mix=mix, x1=x1, x1b=x1b, gu=gu,
                 act=act, dn=dn, x2=x2, x2b=x2b, pg=pg, pe=pe)
    return out, outb, saved


def _layer_bwd(dm, alpha, dout, sv, cos_t, sin_t, wl, gbuf, tag):
    d, hw = dm.d, dm.hw
    t = dout.shape[0]
    nm = lambda s: f"{s}_{tag}"
    gr = {}
    gbuf = dict(gbuf)
    misc_cb = dm.misc0 // LANES
    mm = functools.partial(_matmul, layer=wl["layer"])

    def wgrad(name_, a, g):
        gbuf[name_] = mm(a, g, dims="tn", name=nm("b_" + name_), into=gbuf[name_], tm=1408, tn=1408, tk=1024)

    dx2_a, dpg, dpe = _rowwise(_vjp_fn(_ple_out, 3, 1), [(sv["x2"], d, 0), (sv["pg"], d, 0), (sv["pe"], d, 0), (dout, d, 0)],
                               [], [(d, F32), (d, BF16), (d, BF16)], name=nm("b_ple_out"))
    wgrad("w_ple", sv["p_i"], dpe)
    wgrad("w_ple_gate", sv["x2b"], dpg)
    dx2 = mm(dpg, wl["w_ple_gate"], dims="nt", c=dx2_a, name=nm("b_x2"))

    def ln_bwd(xv, yv, ct, g, b):
        _, pull = jax.vjp(lambda a_, b_, c_, d_: _layer_norm(alpha * a_ + b_, c_, d_), xv, yv, g, b)
        return pull(ct)

    dx1_a, ddn, gr["ln2_g"], gr["ln2_b"] = _rowwise(
        ln_bwd, [(sv["x1"], d, 0), (sv["dn"], d, 0), (dx2, d, 0)], [wl["ln2_g"], wl["ln2_b"]],
        [(d, F32), (d, BF16)], [(1, d), (1, d)], name=nm("b_ln2"))
    wgrad("w_down", sv["act"], ddn)
    dact = mm(ddn, wl["w_down"], dims="nt", name=nm("b_act"), out_dtype=BF16)
    (dgu,) = _rowwise(_vjp_fn(_swiglu, 1, 1), [(sv["gu"], dm.ff2, 0), (dact, dm.ff2 // 2, 0)], [], [(dm.ff2, BF16)],
                      name=nm("b_swiglu"))
    wgrad("w_gate_up", sv["x1b"], dgu)
    dx1 = mm(dgu, wl["w_gate_up"], dims="nt", c=dx1_a, name=nm("b_x1"))

    dx_a, dmix, gr["ln1_g"], gr["ln1_b"] = _rowwise(
        ln_bwd, [(sv["x"], d, 0), (sv["mix"], d, 0), (dx1, d, 0)], [wl["ln1_g"], wl["ln1_b"]],
        [(d, F32), (d, BF16)], [(1, d), (1, d)], name=nm("b_ln1"))
    wgrad("w_out", sv["om"], dmix)
    dom = mm(dmix, wl["w_out"], dims="nt", name=nm("b_om"))

    nq = t // min(ATT_TILE, t)
    delta = _attn_delta(dom, sv["o_mla"], name=nm("b_attn_delta"))
    lse_row = sv["lse"].reshape(N_HEADS, nq, 1, t // nq)
    delta_row = delta[:, :N_HEADS].T.reshape(N_HEADS, nq, 1, t // nq)
    do_b = dom[:, hw:].astype(BF16)
    dka, dva, dqa = _attn_bwd(sv["qa"], _blocked_transpose(sv["qa"], QK_DIM), sv["ka"], sv["va"], lse_row, delta_row,
                              do_b, _blocked_transpose(do_b, HEAD_DIM), name=nm("b_attn"))
    scale = (HEAD_DIM + ROPE_DIM) ** -0.5
    qk_fn = functools.partial(_mla_qk, scale)

    def qk_bwd(qm, kvm, misc, cs, sn, g_q, g_k, g_v):
        _, pull = jax.vjp(lambda a, b, c_: qk_fn(a, b, c_, cs, sn), qm, kvm, misc)
        return pull((g_q, g_k, g_v))

    dqm, dkvm, dmisc_rope = _rowwise(
        qk_bwd, [(sv["qm"], N_HEADS * QK_DIM, 0), (sv["kvm"], 2 * hw, 0), (sv["h"], LANES, misc_cb), (cos_t, LANES, 0),
                 (sin_t, LANES, 0), (dqa, N_HEADS * QK_DIM, 0), (dka, N_HEADS * QK_DIM, 0), (dva, hw, 0)], [],
        [(N_HEADS * QK_DIM, BF16), (2 * hw, BF16), (LANES, F32)], name=nm("b_mla_qk"))
    wgrad("w_uq", sv["cqn"], dqm)
    wgrad("w_ukv", sv["ckvn"], dkvm)
    dcqn = mm(dqm, wl["w_uq"], dims="nt", name=nm("b_cqn"))
    dckvn = mm(dkvm, wl["w_ukv"], dims="nt", name=nm("b_ckvn"))

    def norm_bwd(ckv, cq, g_q, g_kv, kvg, qg):
        _, pull = jax.vjp(_mla_norm, ckv, cq, kvg, qg)
        return pull((g_q, g_kv))

    dckv, dcq, gr["kvg_row"], gr["qg_row"] = _rowwise(
        norm_bwd, [(sv["h"], dm.kv_lora, dm.c_kv0 // dm.kv_lora), (sv["h"], dm.q_lora, dm.c_q0 // dm.q_lora),
                   (dcqn, dm.q_lora, 0), (dckvn, dm.kv_lora, 0)], [wl["kvg_row"], wl["qg_row"]],
        [(dm.kv_lora, BF16), (dm.q_lora, BF16)], [(1, dm.kv_lora), (1, dm.q_lora)], name=nm("b_mla_norm"))

    def gout_bwd(o, z, g_o, gn):
        _, pull = jax.vjp(_gdn_out, o, z, gn)
        return pull(g_o)

    do_gdn, dz, gr["gn_row"] = _rowwise(gout_bwd, [(sv["o_gdn"], hw, 0), (sv["h"], hw, 3), (dom, hw, 0)], [wl["gn_row"]],
                                        [(hw, F32), (hw, BF16)], [(1, HEAD_DIM)], name=nm("b_gdn_out"))
    dprep = _gdn_seq_bwd(sv["prep"], sv["s_all"], do_gdn, name=nm("b_gdn_seq"))
    dqn, dkn, dvg, dgb = _gdn_prep_bwd(sv["qn"], sv["kn"], sv["vg"], sv["gb"], sv["tinv"], dprep, name=nm("b_gdn_prep"))

    def act_bwd(u, misc, g_q, g_k, g_v, g_gb, g_rope, alog, dtb):
        _, pull = jax.vjp(_gdn_act, u, misc, alog, dtb)
        du_, dmisc_, dalog_, ddtb_ = pull((g_q, g_k, g_v, g_gb))
        return du_, dmisc_ + g_rope, dalog_, ddtb_

    du, dmisc, gr["alog_row"], gr["dtb_row"] = _rowwise(
        act_bwd, [(sv["u"], 3 * hw, 0), (sv["h"], LANES, misc_cb), (dqn, hw, 0), (dkn, hw, 0), (dvg, hw, 0),
                  (dgb, LANES, 0), (dmisc_rope, LANES, 0)], [wl["alog_row"], wl["dtb_row"]],
        [(3 * hw, F32), (LANES, BF16)], [(1, LANES), (1, LANES)], name=nm("b_gdn_act"))
    dqkv, dconv = _conv_bwd(du, sv["h"], wl["conv_w"], 3 * hw, name=nm("b_conv"))
    gr["conv_w"] = dconv[:CONV_W]

    dh = jnp.concatenate([dqkv, dz, dckv, dcq, dmisc], axis=1)
    wgrad("w_in", sv["xb"], dh)
    dx = mm(dh, wl["w_in"], dims="nt", c=dx_a, name=nm("b_x"), tk=1408)
    return dx, gbuf, gr


LOCAL_MATRICES = ("w_in", "w_uq", "w_ukv", "w_out", "w_gate_up", "w_down", "w_ple", "w_ple_gate")


def _layer_weights(mats, vecs, layer):
    wl = {n: mats[n] for n in LOCAL_MATRICES}
    wl["layer"] = layer
    wl["conv_w"] = mats["conv_w"][layer]
    wl["alog_row"] = _lane_row(vecs["a_log"][layer], MISC_A0)
    wl["dtb_row"] = _lane_row(vecs["dt_bias"][layer], MISC_A0)
    wl["gn_row"] = vecs["gdn_norm_g"][layer][None, :]
    wl["qg_row"] = vecs["q_norm_g"][layer][None, :]
    wl["kvg_row"] = vecs["kv_norm_g"][layer][None, :]
    for n in ("ln1_g", "ln1_b", "ln2_g", "ln2_b"):
        wl[n] = vecs[n][layer][None, :]
    return wl


def _vector_grads(gr):
    out = {"a_log": gr["alog_row"][0, MISC_A0:MISC_A0 + N_HEADS], "dt_bias": gr["dtb_row"][0, MISC_A0:MISC_A0 + N_HEADS],
           "gdn_norm_g": gr["gn_row"][0], "q_norm_g": gr["qg_row"][0], "kv_norm_g": gr["kvg_row"][0]}
    for n in ("ln1_g", "ln1_b", "ln2_g", "ln2_b"):
        out[n] = gr[n][0]
    return out


def _local_step(dm, x, p, positions, target, mats, vecs):
    depth = p.shape[0]
    alpha = (2.0 * depth) ** 0.25
    freq = ROPE_THETA ** (-jnp.arange(0, ROPE_DIM, 2, dtype=F32) / ROPE_DIM)
    inv_freq_row = _lane_row(jnp.concatenate([freq, freq]), 0)
    cos_t, sin_t = _rope_tables(positions.reshape(-1, 1), inv_freq_row, name="rope_tables")

    wls = [_layer_weights(mats, vecs, i) for i in range(depth)]
    saved = []
    cur, cur_b = x, x
    for i in range(depth):
        cur, cur_b, sv = _layer_fwd(dm, alpha, cur, cur_b, p[i], cos_t, sin_t, wls[i], f"l{i}")
        saved.append(sv)
    dy, loss_blk = _loss_head(cur, target)
    gbuf = {n: depth for n in LOCAL_MATRICES}
    conv_g, vec_g = [None] * depth, [None] * depth
    for i in reversed(range(depth)):
        dy, gbuf, gr = _layer_bwd(dm, alpha, dy, saved[i], cos_t, sin_t, wls[i], gbuf, f"l{i}")
        conv_g[i] = gr["conv_w"]
        vec_g[i] = _vector_grads(gr)
    vec_grads = {n: jnp.stack([vec_g[i][n] for i in range(depth)]) for n in VECTORS}
    return loss_blk[0, 0], dy, gbuf, jnp.stack(conv_g), vec_grads


def kernel(x, p, positions, w_in, conv_w, a_log, dt_bias, gdn_norm_g, q_norm_g, w_uq, kv_norm_g, w_ukv, w_out, ln1_g, ln1_b, w_gate_up, w_down, ln2_g, ln2_b, w_ple, w_ple_gate, loss_target, m_w_in, m_conv_w, m_a_log, m_dt_bias, m_gdn_norm_g, m_q_norm_g, m_w_uq, m_kv_norm_g, m_w_ukv, m_w_out, m_ln1_g, m_ln1_b, m_w_gate_up, m_w_down, m_ln2_g, m_ln2_b, m_w_ple, m_w_ple_gate, v_w_in, v_conv_w, v_a_log, v_dt_bias, v_gdn_norm_g, v_q_norm_g, v_w_uq, v_kv_norm_g, v_w_ukv, v_w_out, v_ln1_g, v_ln1_b, v_w_gate_up, v_w_down, v_ln2_g, v_ln2_b, v_w_ple, v_w_ple_gate):
    w = dict(w_in=w_in, conv_w=conv_w, a_log=a_log, dt_bias=dt_bias, gdn_norm_g=gdn_norm_g, q_norm_g=q_norm_g, w_uq=w_uq,
             kv_norm_g=kv_norm_g, w_ukv=w_ukv, w_out=w_out, ln1_g=ln1_g, ln1_b=ln1_b, w_gate_up=w_gate_up, w_down=w_down,
             ln2_g=ln2_g, ln2_b=ln2_b, w_ple=w_ple, w_ple_gate=w_ple_gate)
    m = dict(w_in=m_w_in, conv_w=m_conv_w, a_log=m_a_log, dt_bias=m_dt_bias, gdn_norm_g=m_gdn_norm_g, q_norm_g=m_q_norm_g,
             w_uq=m_w_uq, kv_norm_g=m_kv_norm_g, w_ukv=m_w_ukv, w_out=m_w_out, ln1_g=m_ln1_g, ln1_b=m_ln1_b,
             w_gate_up=m_w_gate_up, w_down=m_w_down, ln2_g=m_ln2_g, ln2_b=m_ln2_b, w_ple=m_w_ple, w_ple_gate=m_w_ple_gate)
    v = dict(w_in=v_w_in, conv_w=v_conv_w, a_log=v_a_log, dt_bias=v_dt_bias, gdn_norm_g=v_gdn_norm_g, q_norm_g=v_q_norm_g,
             w_uq=v_w_uq, kv_norm_g=v_kv_norm_g, w_ukv=v_w_ukv, w_out=v_w_out, ln1_g=v_ln1_g, ln1_b=v_ln1_b,
             w_gate_up=v_w_gate_up, w_down=v_w_down, ln2_g=v_ln2_g, ln2_b=v_ln2_b, w_ple=v_w_ple, w_ple_gate=v_w_ple_gate)
    depth = w_in.shape[0]
    assert depth % 2 == 0
    hd = depth // 2
    dm = _Dims(x.shape[2], N_CHIPS * w_in.shape[2], w_uq.shape[1], w_ukv.shape[1], N_CHIPS * w_gate_up.shape[2], p.shape[3])
    cx, cy, cc = lax.axis_index("x"), lax.axis_index("y"), lax.axis_index("c")
    chip = 2 * cx + cy

    g_streams = [_stream_of(n, w[n].shape[1:]) for n in MATRICES]
    shards = [w[n] if n == "conv_w" else w[n].astype(BF16) for n in MATRICES]
    shards = [_pad_lanes(s) if st.kind == "piece" else s for s, st in zip(shards, g_streams)]
    g_shapes = []
    for s, st in zip(shards, g_streams):
        if st.kind == "piece":
            shape = (N_CHIPS,) + s.shape
        elif st.kind == "rows":
            shape = (depth, N_CHIPS * s.shape[1], s.shape[2])
        else:
            shape = (depth, s.shape[1], N_CHIPS * s.shape[2])
        g_shapes.append(jax.ShapeDtypeStruct(shape, s.dtype))
    mats = dict(zip(MATRICES, _gather_chips(shards, g_streams, g_shapes, name="gather_weights")))
    for n, to_local in (("w_in", dm.w_in_local), ("w_uq", dm.w_uq_local)):
        pieces = jnp.moveaxis(mats[n][..., :w[n].shape[2]], 0, 2)
        mats[n] = to_local(pieces.reshape(pieces.shape[:2] + (-1,)))
    vecs = {n: w[n] for n in VECTORS}

    loss_local, grad_x, gbuf, conv_g, vec_g = _local_step(dm, x[0], p[:, 0], positions[0], loss_target[0], mats, vecs)
    loss = lax.psum(loss_local, ("x", "y", "c"))

    names = list(LOCAL_MATRICES) + ["conv_w", "vectors"]
    gs = [gbuf[n] for n in LOCAL_MATRICES] + [conv_g, _pack_vectors(vec_g, depth)]
    wire = [BF16] * len(LOCAL_MATRICES) + [F32, F32]
    r_streams = [_stream_of(n, w[n].shape[1:]) for n in LOCAL_MATRICES]
    r_streams += [_stream_of("conv_w", w["conv_w"].shape[1:]), _Stream("whole")]
    shard_shapes = [(hd, w[n].shape[1], _lane_padded(w[n].shape[2])) if st.kind == "piece" else (hd,) + w[n].shape[1:]
                    for n, st in zip(LOCAL_MATRICES, r_streams)]
    shard_shapes += [(hd,) + w["conv_w"].shape[1:], (hd, VEC_ROWS, LANES)]
    c_idx = cc.reshape(1).astype(jnp.int32)
    place = (chip.reshape(1).astype(jnp.int32), c_idx)
    from_sibling = _sibling_take_other_half(gs, name="reduce_sibling")
    chip_sum = [_add_own_half(g, a, c_idx, dt, name=f"reduce_add_{n}")
                for g, a, dt, n in zip(gs, from_sibling, wire, names)]
    for i, n in enumerate(names):
        if r_streams[i].kind == "piece":
            glob = dm.w_in_global(chip_sum[i]) if n == "w_in" else dm.w_uq_global(chip_sum[i])
            glob = glob.reshape(glob.shape[:2] + (N_CHIPS, glob.shape[2] // N_CHIPS))
            chip_sum[i] = jnp.moveaxis(_pad_lanes(glob), 2, 0)
    from_chips = _chips_exchange(chip_sum, r_streams, shard_shapes, name="reduce_chips")
    halves = [_sum_chips(ps, got, place, st, name=f"reduce_sum_{n}")
              for ps, got, st, n in zip(chip_sum, from_chips, r_streams, names)]
    joined = dict(zip(names, _sibling_join_halves(halves, name="reduce_join")))
    joined.update(_unpack_vectors(joined.pop("vectors"), {n: w[n].shape for n in VECTORS}))

    grad_w, delta_w, new_m, new_v = {}, {}, {}, {}
    for n in WEIGHTS:
        grad_w[n] = joined[n][..., :w[n].shape[-1]]
        delta_w[n], new_m[n], new_v[n] = _adamw(w[n], grad_w[n], m[n], v[n], name=f"adamw_{n}")
    return (loss, grad_x[None], *[grad_w[n] for n in WEIGHTS], *[delta_w[n] for n in WEIGHTS],
            *[new_m[n] for n in WEIGHTS], *[new_v[n] for n in WEIGHTS])
```

```python
import functools

import jax
import jax.numpy as jnp
from jax import lax
from jax.experimental import pallas as pl
from jax.experimental.pallas import tpu as pltpu

F32 = jnp.float32
BF16 = jnp.bfloat16
MESH = pl.DeviceIdType.MESH

CHUNK = 64
N_HEADS = 4
HEAD_DIM = 128
ROPE_DIM = 64
ROPE_THETA = 10000.0
LN_EPS = 1e-5
RMS_EPS = 1e-6
ADAM_LR, ADAM_B1, ADAM_B2, ADAM_EPS, ADAM_WD, ADAM_STEP = 0.001, 0.9, 0.999, 1e-08, 0.01, 10

LANES = 128
VMEM_LIMIT = 48 * 1024 * 1024
ROW_TILE = 512
WIDE_ROW_TILE = 256
WIDE_COLS = 2048
GATED_TILE = 512
SUB_ROWS = 16
MAX_SUB_ROWS = 64
VREG_FILE_ELEMS = 64 * 8 * LANES

MISC_BETA0 = ROPE_DIM
MISC_A0 = ROPE_DIM + N_HEADS

NN = (((1,), (0,)), ((), ()))
NT = (((1,), (1,)), ((), ()))
TN = (((0,), (0,)), ((), ()))


def _params(sem=None):
    return pltpu.CompilerParams(dimension_semantics=sem, vmem_limit_bytes=VMEM_LIMIT)


def _divisor_tile(dim, target, unit):
    best = None
    t = unit
    while t <= min(dim, target):
        if dim % t == 0:
            best = t
        t += unit
    return best if best is not None else dim


BATCHED = {NN: (((2,), (1,)), ((0,), (0,))), NT: (((2,), (2,)), ((0,), (0,))), TN: (((1,), (1,)), ((0,), (0,)))}


def _make_dots():
    def raw(a, b, dims):
        if a.ndim == 3:
            dims = BATCHED[dims]
        return lax.dot_general(a.astype(BF16), b.astype(BF16), dims, preferred_element_type=F32)

    @jax.custom_vjp
    def nn(a, b):
        return raw(a, b, NN)

    @jax.custom_vjp
    def nt(a, b):
        return raw(a, b, NT)

    @jax.custom_vjp
    def tn(a, b):
        return raw(a, b, TN)

    nn.defvjp(lambda a, b: (raw(a, b, NN), (a, b)), lambda r, g: (nt(g, r[1]), tn(r[0], g)))
    nt.defvjp(lambda a, b: (raw(a, b, NT), (a, b)), lambda r, g: (nn(g, r[1]), tn(g, r[0])))
    tn.defvjp(lambda a, b: (raw(a, b, TN), (a, b)), lambda r, g: (nt(r[1], g), nn(r[0], g)))
    return nn, nt, tn


_nn, _nt, _tn = _make_dots()


def _matmul(a, b, *, dims, name, c=None, out_dtype=F32, tm=1024, tn=1408, tk=1408, layer=None, into=None,
            a_gated=False):
    b_shape = b.shape[-2:]
    a_shape = (a.shape[0], a.shape[1] // 2) if a_gated else a.shape
    if dims == "nn":
        (m, k), (k2, n) = a_shape, b_shape
    elif dims == "nt":
        (m, k), (n, k2) = a_shape, b_shape
    else:
        (k, m), (k2, n) = a_shape, b_shape
    assert k == k2, (a.shape, b.shape, dims)
    tm = _divisor_tile(m, tm, LANES)
    tn = _divisor_tile(n, tn, LANES)
    tk = _divisor_tile(k, tk, LANES)
    nk = k // tk
    dn = {"nn": NN, "nt": NT, "tn": TN}[dims]
    if dims == "tn":
        a_blk, a_idx, up_off = (tk, tm), (lambda i, j, kk: (kk, i)), m // tm
    else:
        a_blk, a_idx, up_off = (tm, tk), (lambda i, j, kk: (i, kk)), k // tk
    a_spec = pl.BlockSpec(a_blk, a_idx)
    up_spec = pl.BlockSpec(a_blk, lambda i, j, kk: (a_idx(i, j, kk)[0], a_idx(i, j, kk)[1] + up_off))
    b_blk, b_idx = ((tn, tk), lambda i, j, kk: (j, kk)) if dims == "nt" else ((tk, tn), lambda i, j, kk: (kk, j))
    if b.ndim == 3:
        b_spec = pl.BlockSpec((None,) + b_blk, lambda i, j, kk: (layer,) + b_idx(i, j, kk))
    else:
        b_spec = pl.BlockSpec(b_blk, b_idx)
    c_spec = pl.BlockSpec((tm, tn), lambda i, j, kk: (i, j))
    if isinstance(into, int):
        o_spec = pl.BlockSpec((None, tm, tn), lambda i, j, kk: (layer, i, j))
        out_shape = jax.ShapeDtypeStruct((into, m, n), out_dtype)
        into = None
    elif into is not None:
        assert into.shape[1:] == (m, n) and into.dtype == out_dtype
        o_spec = pl.BlockSpec((None, tm, tn), lambda i, j, kk: (layer, i, j))
        out_shape = jax.ShapeDtypeStruct(into.shape, into.dtype)
    else:
        o_spec = c_spec
        out_shape = jax.ShapeDtypeStruct((m, n), out_dtype)
    has_c = c is not None

    n_a = 2 if a_gated else 1

    def body(*refs):
        b_ref = refs[n_a]
        c_ref = refs[n_a + 1] if has_c else None
        acc_ref = refs[-1]
        o_ref = refs[-3] if a_gated else refs[-2]
        kk = pl.program_id(2)

        @pl.when(kk == 0)
        def _():
            if has_c:
                acc_ref[...] = c_ref[...].astype(F32)
            else:
                acc_ref[...] = jnp.zeros_like(acc_ref)

        if a_gated:
            a_val = (_silu(refs[0][...].astype(F32)) * refs[1][...].astype(F32)).astype(BF16)
            refs[-2][...] = a_val
        else:
            a_val = refs[0][...].astype(BF16)
        acc_ref[...] += lax.dot_general(a_val, b_ref[...].astype(BF16), dn, preferred_element_type=F32)

        @pl.when(kk == nk - 1)
        def _():
            o_ref[...] = acc_ref[...].astype(o_ref.dtype)

    ins = ([a, a] if a_gated else [a]) + [b] + ([c] if has_c else [])
    specs = ([a_spec, up_spec] if a_gated else [a_spec]) + [b_spec] + ([c_spec] if has_c else [])
    aliases = {}
    if into is not None:
        aliases = {len(ins): 0}
        ins.append(into)
        specs.append(pl.BlockSpec(memory_space=pl.ANY))
    if a_gated:
        assert dims == "nn" and n == tn
        o_spec, out_shape = [o_spec, a_spec], [out_shape, jax.ShapeDtypeStruct((m, k), BF16)]
    return pl.pallas_call(
        body, name=name, grid=(m // tm, n // tn, nk), in_specs=specs, out_specs=o_spec, out_shape=out_shape,
        scratch_shapes=[pltpu.VMEM((tm, tn), F32)], input_output_aliases=aliases,
        compiler_params=_params(("arbitrary", "arbitrary", "arbitrary")),
    )(*ins)


def _rowwise(fn, rows, params, outs, accs=(), *, name):
    t = rows[0][0].shape[0]
    widest = max([w for _, w, _ in rows] + [w for w, _ in outs])
    tm = min(WIDE_ROW_TILE if widest > WIDE_COLS else ROW_TILE, t)
    sub = SUB_ROWS
    while sub < MAX_SUB_ROWS and 2 * sub * widest <= VREG_FILE_ELEMS:
        sub *= 2
    assert t % tm == 0 and tm % sub == 0
    n_rows, n_par, n_out, n_acc = len(rows), len(params), len(outs), len(accs)

    def body(*refs):
        row_refs = refs[:n_rows]
        par_refs = refs[n_rows:n_rows + n_par]
        out_refs = refs[n_rows + n_par:n_rows + n_par + n_out]
        acc_refs = refs[n_rows + n_par + n_out:]
        if n_acc:
            @pl.when(pl.program_id(0) == 0)
            def _():
                for a_ref in acc_refs:
                    a_ref[...] = jnp.zeros_like(a_ref)

        def step(r, carry):
            sl = pl.ds(pl.multiple_of(r * sub, sub), sub)
            vals = [ref[sl, :].astype(F32) for ref in row_refs] + [ref[...] for ref in par_refs]
            res = fn(*vals)
            for o_ref, val in zip(out_refs, res[:n_out]):
                o_ref[sl, :] = val.astype(o_ref.dtype)
            for a_ref, val in zip(acc_refs, res[n_out:]):
                a_ref[...] += val
            return carry

        lax.fori_loop(0, tm // sub, step, 0)

    in_specs = [pl.BlockSpec((tm, w), functools.partial(lambda i, cb: (i, cb), cb=cb)) for _, w, cb in rows]
    in_specs += [pl.BlockSpec(p.shape, lambda i: (0, 0)) for p in params]
    out_specs = [pl.BlockSpec((tm, w), lambda i: (i, 0)) for w, _ in outs]
    out_specs += [pl.BlockSpec(s, lambda i: (0, 0)) for s in accs]
    out_shape = [jax.ShapeDtypeStruct((t, w), d) for w, d in outs]
    out_shape += [jax.ShapeDtypeStruct(s, F32) for s in accs]
    return pl.pallas_call(
        body, name=name, grid=(t // tm,), in_specs=in_specs, out_specs=out_specs, out_shape=out_shape,
        compiler_params=_params(("arbitrary",)),
    )(*[r[0] for r in rows], *params)


def _vjp_fn(fn, n_in, n_out):
    def bwd(*args):
        ins, cts = args[:n_in], args[n_in:]
        _, pull = jax.vjp(fn, *ins)
        return pull(tuple(cts) if n_out > 1 else cts[0])
    return bwd


def _lane(shape):
    return lax.broadcasted_iota(jnp.int32, shape, 1)


def _silu(x):
    return x * jax.nn.sigmoid(x)


def _softplus(x):
    return jnp.maximum(x, 0.0) + jnp.log1p(jnp.exp(-jnp.abs(x)))


def _heads(x, width=HEAD_DIM):
    return [x[:, h * width:(h + 1) * width] for h in range(N_HEADS)]


def _layer_norm(z, g, b):
    mu = jnp.mean(z, -1, keepdims=True)
    zc = z - mu
    var = jnp.mean(zc * zc, -1, keepdims=True)
    return zc * lax.rsqrt(var + LN_EPS) * g + b


def _gdn_act(u, misc, alog_row, dtb_row):
    s = _silu(u)
    w = N_HEADS * HEAD_DIM
    q = jnp.concatenate([t * lax.rsqrt(jnp.sum(t * t, -1, keepdims=True) + RMS_EPS) * HEAD_DIM ** -0.5
                         for t in _heads(s[:, :w])], axis=1)
    k = jnp.concatenate([t * lax.rsqrt(jnp.sum(t * t, -1, keepdims=True) + RMS_EPS)
                         for t in _heads(s[:, w:2 * w])], axis=1)
    v = s[:, 2 * w:]
    lane = _lane(misc.shape)
    beta = jax.nn.sigmoid(misc)
    g = -jnp.exp(alog_row) * _softplus(misc + dtb_row)
    is_beta = (lane >= MISC_BETA0) & (lane < MISC_BETA0 + N_HEADS)
    is_g = (lane >= MISC_A0) & (lane < MISC_A0 + N_HEADS)
    gb = jnp.where(is_beta, beta, jnp.where(is_g, g, 0.0))
    return q, k, v, gb


def _gdn_out(o, z, gn_row):
    outs = []
    for oh, zh in zip(_heads(o), _heads(z)):
        r = oh * lax.rsqrt(jnp.mean(oh * oh, -1, keepdims=True) + RMS_EPS) * gn_row
        outs.append(r * _silu(zh))
    return jnp.concatenate(outs, axis=1)


def _mla_norm(ckv, cq, kvg_row, qg_row):
    cqn = cq * lax.rsqrt(jnp.mean(cq * cq, -1, keepdims=True) + RMS_EPS) * qg_row
    ckvn = ckv * lax.rsqrt(jnp.mean(ckv * ckv, -1, keepdims=True) + RMS_EPS) * kvg_row
    return cqn, ckvn


def _swap_halves(x):
    half = ROPE_DIM // 2
    return jnp.where(_lane(x.shape) < half, pltpu.roll(x, LANES - half, 1), pltpu.roll(x, half, 1))


@jax.custom_vjp
def _rope(x, cos_t, sin_t):
    return x * cos_t + _swap_halves(x) * sin_t


def _rope_fwd(x, cos_t, sin_t):
    return _rope(x, cos_t, sin_t), (cos_t, sin_t)


def _rope_bwd(res, g):
    cos_t, sin_t = res
    return g * cos_t - _swap_halves(g) * sin_t, jnp.zeros_like(cos_t), jnp.zeros_like(sin_t)


_rope.defvjp(_rope_fwd, _rope_bwd)


def _mla_qk(scale, qm, kv, misc, cos_t, sin_t):
    krope = _rope(misc, cos_t, sin_t)
    qs, ks = [], []
    for h in range(N_HEADS):
        base = 2 * HEAD_DIM * h
        qs += [qm[:, base:base + HEAD_DIM], _rope(qm[:, base + HEAD_DIM:base + 2 * HEAD_DIM], cos_t, sin_t)]
        ks += [kv[:, HEAD_DIM * h:HEAD_DIM * (h + 1)], krope]
    return jnp.concatenate(qs, axis=1) * scale, jnp.concatenate(ks, axis=1), kv[:, N_HEADS * HEAD_DIM:]


def _swiglu(gu):
    f = gu.shape[1] // 2
    return _silu(gu[:, :f]) * gu[:, f:]


def _ple_out(x2, pg, pe):
    return x2 + jax.nn.sigmoid(pg) * pe


CONV_W = 4
HALO = 8
CONV_STRIP = 512


def _conv_fwd(h, conv_w, width, *, name, tm=ROW_TILE, sub=32):
    t = h.shape[0]
    tm = min(tm, t)
    nb = tm // HALO

    def body(x_ref, halo_ref, w_ref, u_ref, buf):
        i = pl.program_id(0)
        buf[pl.ds(0, HALO), :] = jnp.where(i > 0, halo_ref[...], 0.0)
        buf[pl.ds(HALO, tm), :] = x_ref[...]
        for c0 in range(0, width, CONV_STRIP):
            cols = pl.ds(c0, CONV_STRIP)
            w = w_ref[:, cols]
            for r0 in range(0, tm, sub):
                acc = jnp.zeros((sub, CONV_STRIP), F32)
                for j in range(CONV_W):
                    acc = acc + w[j:j + 1, :] * buf[pl.ds(HALO + r0 - (CONV_W - 1) + j, sub), cols]
                u_ref[pl.ds(r0, sub), cols] = acc

    return pl.pallas_call(
        body, name=name, grid=(t // tm,),
        in_specs=[pl.BlockSpec((tm, width), lambda i: (i, 0)),
                  pl.BlockSpec((HALO, width), lambda i: (jnp.maximum(i * nb - 1, 0), 0)),
                  pl.BlockSpec(conv_w.shape, lambda i: (0, 0))],
        out_specs=pl.BlockSpec((tm, width), lambda i: (i, 0)),
        out_shape=jax.ShapeDtypeStruct((t, width), F32),
        scratch_shapes=[pltpu.VMEM((tm + HALO, width), F32)],
        compiler_params=_params(("arbitrary",)),
    )(h, h, conv_w)


def _conv_bwd(du, h, conv_w, width, *, name, tm=ROW_TILE, sub=32):
    t = h.shape[0]
    tm = min(tm, t)
    nb = tm // HALO
    n_tiles = t // tm

    def body(du_ref, du_halo, x_ref, x_halo, w_ref, dx_ref, dw_ref, dbuf, xbuf):
        i = pl.program_id(0)

        @pl.when(i == 0)
        def _():
            dw_ref[...] = jnp.zeros_like(dw_ref)

        dbuf[pl.ds(0, tm), :] = du_ref[...]
        dbuf[pl.ds(tm, HALO), :] = jnp.where(i < n_tiles - 1, du_halo[...], 0.0)
        xbuf[pl.ds(0, HALO), :] = jnp.where(i > 0, x_halo[...], 0.0)
        xbuf[pl.ds(HALO, tm), :] = x_ref[...]
        for c0 in range(0, width, CONV_STRIP):
            cols = pl.ds(c0, CONV_STRIP)
            w = w_ref[:, cols]
            dws = [jnp.zeros((HALO, CONV_STRIP), F32) for _ in range(CONV_W)]
            for r0 in range(0, tm, sub):
                acc = jnp.zeros((sub, CONV_STRIP), F32)
                d_here = dbuf[pl.ds(r0, sub), cols]
                for j in range(CONV_W):
                    acc = acc + w[j:j + 1, :] * dbuf[pl.ds(r0 + (CONV_W - 1) - j, sub), cols]
                    prod = d_here * xbuf[pl.ds(HALO + r0 - (CONV_W - 1) + j, sub), cols]
                    for g0 in range(0, sub, HALO):
                        dws[j] = dws[j] + prod[g0:g0 + HALO, :]
                dx_ref[pl.ds(r0, sub), cols] = acc.astype(dx_ref.dtype)
            for j in range(CONV_W):
                dw_ref[pl.ds(j, 1), cols] += jnp.sum(dws[j], axis=0, keepdims=True)

    return pl.pallas_call(
        body, name=name, grid=(n_tiles,),
        in_specs=[pl.BlockSpec((tm, width), lambda i: (i, 0)),
                  pl.BlockSpec((HALO, width), lambda i: (jnp.minimum((i + 1) * nb, t // HALO - 1), 0)),
                  pl.BlockSpec((tm, width), lambda i: (i, 0)),
                  pl.BlockSpec((HALO, width), lambda i: (jnp.maximum(i * nb - 1, 0), 0)),
                  pl.BlockSpec(conv_w.shape, lambda i: (0, 0))],
        out_specs=[pl.BlockSpec((tm, width), lambda i: (i, 0)),
                   pl.BlockSpec((HALO, width), lambda i: (0, 0))],
        out_shape=[jax.ShapeDtypeStruct((t, width), BF16), jax.ShapeDtypeStruct((HALO, width), F32)],
        scratch_shapes=[pltpu.VMEM((tm + HALO, width), F32), pltpu.VMEM((tm + HALO, width), F32)],
        compiler_params=_params(("arbitrary",)),
    )(du, du, h, h, conv_w)


@jax.custom_vjp
def _inv_unit_lower(low):
    n = low.shape[-1]
    eye = (lax.broadcasted_iota(jnp.int32, (n, n), 0) == lax.broadcasted_iota(jnp.int32, (n, n), 1)).astype(F32)
    x = eye - low
    p = low
    span = 2
    while span < n:
        p = _nn(p, p)
        x = x + _nn(x, p)
        span *= 2
    return x


def _inv_fwd(low):
    x = _inv_unit_lower(low)
    return x, x


def _inv_bwd(x, g):
    return (-_tn(x, _nt(g, x)),)


_inv_unit_lower.defvjp(_inv_fwd, _inv_bwd)


@jax.custom_vjp
def _inv_known(low, inverse):
    return inverse


_inv_known.defvjp(lambda low, inverse: (inverse, inverse), lambda x, g: (_inv_bwd(x, g)[0], jnp.zeros_like(x)))


def _gdn_prep(q, k, v, gb, known_inverse=None):
    c = CHUNK
    n = q.shape[0] // c
    pairs = [(g, h) for g in range(n) for h in range(N_HEADS)]
    row = lax.broadcasted_iota(jnp.int32, (c, c), 0)
    col = lax.broadcasted_iota(jnp.int32, (c, c), 1)
    tri_incl = row >= col
    tri_strict = row > col
    lane = _lane((c, LANES))
    sub = lax.broadcasted_iota(jnp.int32, (LANES, c), 0)
    last = lax.broadcasted_iota(jnp.int32, (c, 1), 0) == c - 1

    def split(x):
        return jnp.stack([x[g * c:(g + 1) * c, h * HEAD_DIM:(h + 1) * HEAD_DIM] for g, h in pairs])

    gbs = [gb[g * c:(g + 1) * c, :] for g in range(n)]
    gbts = [x.T for x in gbs]
    g_col = jnp.stack([jnp.sum(jnp.where(lane == MISC_A0 + h, gbs[g], 0.0), axis=1, keepdims=True) for g, h in pairs])
    b_col = jnp.stack([jnp.sum(jnp.where(lane == MISC_BETA0 + h, gbs[g], 0.0), axis=1, keepdims=True) for g, h in pairs])
    g_row = jnp.stack([jnp.sum(jnp.where(sub == MISC_A0 + h, gbts[g], 0.0), axis=0, keepdims=True) for g, h in pairs])
    gc_col = jnp.sum(jnp.where(tri_incl, g_row, 0.0), axis=2, keepdims=True)
    gc_row = jnp.sum(jnp.where(row <= col, g_col, 0.0), axis=1, keepdims=True)
    decay = jnp.where(tri_incl, jnp.exp(jnp.where(tri_incl, gc_col - gc_row, 0.0)), 0.0)
    g_last = jnp.sum(jnp.where(last, gc_col, 0.0), axis=1, keepdims=True)
    qs, ks, vs = split(q), split(k), split(v)
    kb = ks * b_col
    low = jnp.where(tri_strict, _nt(kb, ks) * decay, 0.0)
    if known_inverse is None:
        tinv = _inv_unit_lower(low)
    else:
        tinv = _inv_known(low, jnp.stack([known_inverse[g * c:(g + 1) * c, h * c:(h + 1) * c] for g, h in pairs]))
    eg = jnp.exp(gc_col)
    sol = _nn(tinv, jnp.concatenate([vs * b_col, kb * eg], axis=2))
    attn = jnp.where(tri_incl, _nt(qs, ks) * decay, 0.0)
    qd = qs * eg
    kd = ks * jnp.exp(g_last - gc_col)

    def merge(x):
        return jnp.concatenate([jnp.concatenate([x[g * N_HEADS + h] for h in range(N_HEADS)], axis=1)
                                for g in range(n)], axis=0)

    glb = jnp.concatenate([sum(jnp.where(lane == h, g_last[g * N_HEADS + h], 0.0) for h in range(N_HEADS))
                           for g in range(n)], axis=0)
    outs = (merge(sol[:, :, :HEAD_DIM]), merge(sol[:, :, HEAD_DIM:]), merge(qd), merge(kd), merge(attn), glb)
    return outs, merge(tinv)


def _gdn_seq(state, u, w, qd, kd, attn, glb):
    c = u.shape[0]
    first = lax.broadcasted_iota(jnp.int32, glb.shape, 0) == 0
    lane = _lane(glb.shape)
    heads = lambda x: jnp.stack([x[:, h * HEAD_DIM:(h + 1) * HEAD_DIM] for h in range(N_HEADS)])
    g_last = jnp.stack([jnp.sum(jnp.sum(jnp.where(first & (lane == h), glb, 0.0), axis=1, keepdims=True),
                                axis=0, keepdims=True) for h in range(N_HEADS)])
    s = jnp.stack([state[h * HEAD_DIM:(h + 1) * HEAD_DIM, :] for h in range(N_HEADS)])
    at = jnp.stack([attn[:, h * c:(h + 1) * c] for h in range(N_HEADS)])
    v_new = heads(u) - _nn(heads(w), s)
    o = _nn(heads(qd), s) + _nn(at, v_new)
    s_new = s * jnp.exp(g_last) + _tn(heads(kd), v_new)
    return (jnp.concatenate([o[h] for h in range(N_HEADS)], axis=1),
            jnp.concatenate([s_new[h] for h in range(N_HEADS)], axis=0))


PREP_CHUNKS = 8
PREP_CHUNKS_BWD = 4
SEQ_CHUNKS = 8


def _gdn_prep_fwd(q, k, v, gb, *, name):
    t, w = q.shape
    rows = min(PREP_CHUNKS * CHUNK, t)

    def body(q_ref, k_ref, v_ref, gb_ref, *out_refs):
        outs, inverse = _gdn_prep(q_ref[...], k_ref[...], v_ref[...], gb_ref[...])
        for o_ref, val in zip(out_refs, outs + (inverse,)):
            o_ref[...] = val

    spec = lambda width: pl.BlockSpec((rows, width), lambda i: (i, 0))
    widths = [w, w, w, w, N_HEADS * CHUNK, LANES, N_HEADS * CHUNK]
    res = pl.pallas_call(
        body, name=name, grid=(t // rows,),
        in_specs=[spec(w), spec(w), spec(w), spec(LANES)],
        out_specs=[spec(x) for x in widths],
        out_shape=[jax.ShapeDtypeStruct((t, x), F32) for x in widths],
        compiler_params=_params(("arbitrary",)),
    )(q, k, v, gb)
    return tuple(res[:6]), res[6]


def _gdn_prep_bwd(q, k, v, gb, inverse, cts, *, name):
    t, w = q.shape
    rows = min(PREP_CHUNKS_BWD * CHUNK, t)

    def body(q_ref, k_ref, v_ref, gb_ref, inv_ref, du, dw, dqd, dkd, dattn, dglb, dq_ref, dk_ref, dv_ref, dgb_ref):
        known = inv_ref[...]
        _, pull = jax.vjp(lambda a, b, c_, d_: _gdn_prep(a, b, c_, d_, known)[0],
                          q_ref[...], k_ref[...], v_ref[...], gb_ref[...])
        dq, dk, dv, dgb = pull(tuple(r[...] for r in (du, dw, dqd, dkd, dattn, dglb)))
        dq_ref[...] = dq
        dk_ref[...] = dk
        dv_ref[...] = dv
        dgb_ref[...] = dgb

    spec = lambda width: pl.BlockSpec((rows, width), lambda i: (i, 0))
    widths = [w, w, w, w, N_HEADS * CHUNK, LANES]
    return pl.pallas_call(
        body, name=name, grid=(t // rows,),
        in_specs=[spec(w), spec(w), spec(w), spec(LANES), spec(N_HEADS * CHUNK)] + [spec(x) for x in widths],
        out_specs=[spec(w), spec(w), spec(w), spec(LANES)],
        out_shape=[jax.ShapeDtypeStruct((t, w), F32)] * 3 + [jax.ShapeDtypeStruct((t, LANES), F32)],
        compiler_params=_params(("arbitrary",)),
    )(q, k, v, gb, inverse, *cts)


def _gdn_seq_fwd(prep, *, name):
    t, w = prep[0].shape
    rows = min(SEQ_CHUNKS * CHUNK, t)
    per = rows // CHUNK

    def body(u_ref, w_ref, qd_ref, kd_ref, at_ref, gl_ref, o_ref, sall_ref, s_scr):
        @pl.when(pl.program_id(0) == 0)
        def _():
            s_scr[...] = jnp.zeros_like(s_scr)

        def step(j, carry):
            sl = pl.ds(pl.multiple_of(j * CHUNK, CHUNK), CHUNK)
            s = s_scr[...]
            sall_ref[j] = s
            o, s_new = _gdn_seq(s, u_ref[sl, :], w_ref[sl, :], qd_ref[sl, :], kd_ref[sl, :], at_ref[sl, :], gl_ref[sl, :])
            o_ref[sl, :] = o
            s_scr[...] = s_new
            return carry

        lax.fori_loop(0, per, step, 0)

    spec = lambda width: pl.BlockSpec((rows, width), lambda i: (i, 0))
    widths = [w, w, w, w, N_HEADS * CHUNK, LANES]
    return pl.pallas_call(
        body, name=name, grid=(t // rows,),
        in_specs=[spec(x) for x in widths],
        out_specs=[spec(w), pl.BlockSpec((per, w, HEAD_DIM), lambda i: (i, 0, 0))],
        out_shape=[jax.ShapeDtypeStruct((t, w), F32), jax.ShapeDtypeStruct((t // CHUNK, w, HEAD_DIM), F32)],
        scratch_shapes=[pltpu.VMEM((w, HEAD_DIM), F32)],
        compiler_params=_params(("arbitrary",)),
    )(*prep)


def _gdn_seq_bwd(prep, s_all, do, *, name):
    t, w = prep[0].shape
    rows = min(SEQ_CHUNKS * CHUNK, t)
    per = rows // CHUNK
    n = t // rows

    def body(u_ref, w_ref, qd_ref, kd_ref, at_ref, gl_ref, sall_ref, do_ref, du, dw, dqd, dkd, dat, dgl, ds_scr):
        @pl.when(pl.program_id(0) == 0)
        def _():
            ds_scr[...] = jnp.zeros_like(ds_scr)

        def step(jj, carry):
            j = per - 1 - jj
            sl = pl.ds(pl.multiple_of(j * CHUNK, CHUNK), CHUNK)
            _, pull = jax.vjp(_gdn_seq, sall_ref[j], u_ref[sl, :], w_ref[sl, :], qd_ref[sl, :], kd_ref[sl, :],
                              at_ref[sl, :], gl_ref[sl, :])
            res = pull((do_ref[sl, :], ds_scr[...]))
            ds_scr[...] = res[0]
            for o_ref, val in zip((du, dw, dqd, dkd, dat, dgl), res[1:]):
                o_ref[sl, :] = val
            return carry

        lax.fori_loop(0, per, step, 0)

    spec = lambda width: pl.BlockSpec((rows, width), lambda i: (n - 1 - i, 0))
    widths = [w, w, w, w, N_HEADS * CHUNK, LANES]
    return pl.pallas_call(
        body, name=name, grid=(n,),
        in_specs=[spec(x) for x in widths] + [pl.BlockSpec((per, w, HEAD_DIM), lambda i: (n - 1 - i, 0, 0)), spec(w)],
        out_specs=[spec(x) for x in widths],
        out_shape=[jax.ShapeDtypeStruct((t, x), F32) for x in widths],
        scratch_shapes=[pltpu.VMEM((w, HEAD_DIM), F32)],
        compiler_params=_params(("arbitrary",)),
    )(*prep, s_all, do)


QK_DIM = 2 * HEAD_DIM
ATT_TILE = 1024
NEG = -1e30


ATT_SPLIT = 4


def _chunk_mask(n_rows, n_cols, key_major, query_offset):
    r = lax.broadcasted_iota(jnp.int32, (n_rows, n_cols), 0)
    c = lax.broadcasted_iota(jnp.int32, (n_rows, n_cols), 1)
    if key_major:
        return r // CHUNK <= (c + query_offset) // CHUNK
    return c // CHUNK <= (r + query_offset) // CHUNK


def _visible_keys(tile, diagonal):
    hq = tile // ATT_SPLIT
    return [(a + 1) * hq if diagonal else tile for a in range(ATT_SPLIT)]


def _dot_nn(a, b):
    return lax.dot_general(a, b, NN, preferred_element_type=F32)


def _blocked_transpose(x, width):
    t = x.shape[0]
    tile = min(ATT_TILE, t)
    return x.reshape(t // tile, tile, N_HEADS, width).transpose(2, 0, 3, 1)


def _attn_fwd(q, kt, v1, *, name):
    t = q.shape[0]
    tq = min(ATT_TILE, t)
    nq = t // tq

    def body(q_ref, kt_ref, v_ref, o_ref, lse_ref, m_scr, acc_scr):
        qi = pl.program_id(1)
        m_scr[...] = jnp.full_like(m_scr, NEG)
        acc_scr[...] = jnp.zeros_like(acc_scr)
        hq = tq // ATT_SPLIT
        parts = [pl.ds(a * hq, hq) for a in range(ATT_SPLIT)]
        qs = [q_ref[sl, :] for sl in parts]

        def step(kj, masked):
            rows = pl.ds(pl.multiple_of(kj * tq, tq), tq)
            kt_blk, vv = kt_ref[kj], v_ref[rows, :]
            seen = _visible_keys(tq, masked)
            ss = [_dot_nn(qv, kt_blk[:, :w]) for qv, w in zip(qs, seen)]
            for a, sl in enumerate(parts):
                s = ss[a]
                if masked:
                    s = jnp.where(_chunk_mask(hq, seen[a], False, a * hq), s, NEG)
                m_old = m_scr[sl, :]
                m_new = jnp.maximum(m_old, jnp.max(s, axis=1, keepdims=True))
                p = jnp.exp(s - m_new)
                acc_scr[sl, :] = jnp.exp(m_old - m_new) * acc_scr[sl, :] + _dot_nn(p.astype(BF16), vv[:seen[a], :])
                m_scr[sl, :] = m_new

        def loop_body(kj, carry):
            step(kj, False)
            return carry

        lax.fori_loop(0, qi, loop_body, 0)
        step(qi, True)
        acc = acc_scr[...]
        o_ref[...] = (acc[:, :HEAD_DIM] / acc[:, HEAD_DIM:]).astype(o_ref.dtype)
        lse_ref[...] = m_scr[...] + jnp.log(acc[:, HEAD_DIM:HEAD_DIM + 1])

    return pl.pallas_call(
        body, name=name, grid=(N_HEADS, nq),
        in_specs=[pl.BlockSpec((tq, QK_DIM), lambda h, i: (i, h)),
                  pl.BlockSpec((None, nq, QK_DIM, tq), lambda h, i: (h, 0, 0, 0)),
                  pl.BlockSpec((t, 2 * HEAD_DIM), lambda h, i: (0, h))],
        out_specs=[pl.BlockSpec((tq, HEAD_DIM), lambda h, i: (i, h)),
                   pl.BlockSpec((None, tq, 1), lambda h, i: (h, i, 0))],
        out_shape=[jax.ShapeDtypeStruct((t, N_HEADS * HEAD_DIM), BF16),
                   jax.ShapeDtypeStruct((N_HEADS, t, 1), F32)],
        scratch_shapes=[pltpu.VMEM((tq, 1), F32), pltpu.VMEM((tq, 2 * HEAD_DIM), F32)],
        compiler_params=_params(("arbitrary", "arbitrary")),
    )(q, kt, v1)


def _attn_delta(dom, o, *, name):
    hw = o.shape[1]

    def fn(do, ov):
        lane = _lane((do.shape[0], LANES))
        out = jnp.zeros((do.shape[0], LANES), F32)
        for h, (a, b) in enumerate(zip(_heads(do), _heads(ov))):
            out = out + jnp.where(lane == h, jnp.sum(a * b, axis=1, keepdims=True), 0.0)
        return (out,)

    return _rowwise(fn, [(dom, hw, 1), (o, hw, 0)], [], [(LANES, F32)], name=name)[0]


def _attn_bwd(q, qt, k, v, lse_row, delta_row, do, dot, *, name):
    t = q.shape[0]
    tk = min(ATT_TILE, t)
    nk = t // tk

    def body(q_ref, qt_ref, k_ref, v_ref, lse_ref, delta_ref, do_ref, dot_ref, dk_ref, dv_ref, dq_ref, dk_scr, dv_scr):
        kj = pl.program_id(1)

        @pl.when(kj == 0)
        def _():
            dq_ref[...] = jnp.zeros_like(dq_ref)

        dk_scr[...] = jnp.zeros_like(dk_scr)
        dv_scr[...] = jnp.zeros_like(dv_scr)
        kv_ = k_ref[...]
        vv = v_ref[...]
        hq = tk // ATT_SPLIT

        def step(qi, masked):
            lse_v, delta_v = lse_ref[qi], delta_ref[qi]
            qt_blk, dot_blk = qt_ref[qi], dot_ref[qi]
            rows = [pl.ds(pl.multiple_of(qi * tk + a * hq, hq), hq) for a in range(ATT_SPLIT)]
            qs = [q_ref[r, :] for r in rows]
            dos = [do_ref[r, :] for r in rows]
            seen = _visible_keys(tk, masked)
            ss = [_dot_nn(kv_[:seen[a], :], qt_blk[:, a * hq:(a + 1) * hq]) for a in range(ATT_SPLIT)]
            dps = [_dot_nn(vv[:seen[a], :], dot_blk[:, a * hq:(a + 1) * hq]) for a in range(ATT_SPLIT)]
            for a in range(ATT_SPLIT):
                cols = slice(a * hq, (a + 1) * hq)
                keys = pl.ds(0, seen[a])
                p = jnp.exp(ss[a] - lse_v[:, cols])
                if masked:
                    p = jnp.where(_chunk_mask(seen[a], hq, True, a * hq), p, 0.0)
                dv_scr[keys, :] += _dot_nn(p.astype(BF16), dos[a])
                ds = (p * (dps[a] - delta_v[:, cols])).astype(BF16)
                dk_scr[keys, :] += _dot_nn(ds, qs[a])
                dq_ref[rows[a], :] += lax.dot_general(ds, kv_[:seen[a], :], TN, preferred_element_type=F32)

        step(kj, True)

        def loop_body(qi, carry):
            step(qi, False)
            return carry

        lax.fori_loop(kj + 1, nk, loop_body, 0)
        dk_ref[...] = dk_scr[...].astype(dk_ref.dtype)
        dv_ref[...] = dv_scr[...].astype(dv_ref.dtype)

    once = dict(pipeline_mode=pl.Buffered(1))
    stat = pl.BlockSpec((None, nk, 1, tk), lambda h, j: (h, 0, 0, 0))
    return pl.pallas_call(
        body, name=name, grid=(N_HEADS, nk),
        in_specs=[pl.BlockSpec((t, QK_DIM), lambda h, j: (0, h), **once),
                  pl.BlockSpec((None, nk, QK_DIM, tk), lambda h, j: (h, 0, 0, 0), **once),
                  pl.BlockSpec((tk, QK_DIM), lambda h, j: (j, h)),
                  pl.BlockSpec((tk, HEAD_DIM), lambda h, j: (j, h)),
                  stat, stat,
                  pl.BlockSpec((t, HEAD_DIM), lambda h, j: (0, h), **once),
                  pl.BlockSpec((None, nk, HEAD_DIM, tk), lambda h, j: (h, 0, 0, 0), **once)],
        out_specs=[pl.BlockSpec((tk, QK_DIM), lambda h, j: (j, h)),
                   pl.BlockSpec((tk, HEAD_DIM), lambda h, j: (j, h)),
                   pl.BlockSpec((t, QK_DIM), lambda h, j: (0, h))],
        out_shape=[jax.ShapeDtypeStruct((t, N_HEADS * QK_DIM), BF16),
                   jax.ShapeDtypeStruct((t, N_HEADS * HEAD_DIM), BF16),
                   jax.ShapeDtypeStruct((t, N_HEADS * QK_DIM), F32)],
        scratch_shapes=[pltpu.VMEM((tk, QK_DIM), F32), pltpu.VMEM((tk, HEAD_DIM), F32)],
        compiler_params=_params(("arbitrary", "arbitrary")),
    )(q, qt, k, v, lse_row, delta_row, do, dot)


def _rope_tables(pos_col, inv_freq_row, *, name):
    t = pos_col.shape[0]
    tm = min(ROW_TILE, t)

    def body(p_ref, f_ref, c_ref, s_ref):
        ang = p_ref[...].astype(F32) * f_ref[...]
        lane = _lane(ang.shape)
        c_ref[...] = jnp.where(lane < ROPE_DIM, jnp.cos(ang), 0.0)
        sn = jnp.sin(ang)
        s_ref[...] = jnp.where(lane < ROPE_DIM // 2, -sn, jnp.where(lane < ROPE_DIM, sn, 0.0))

    out = pl.BlockSpec((tm, LANES), lambda i: (i, 0))
    return pl.pallas_call(
        body, name=name, grid=(t // tm,),
        in_specs=[pl.BlockSpec((tm, 1), lambda i: (i, 0)), pl.BlockSpec((1, LANES), lambda i: (0, 0))],
        out_specs=[out, out], out_shape=[jax.ShapeDtypeStruct((t, LANES), F32)] * 2,
        compiler_params=_params(("arbitrary",)),
    )(pos_col, inv_freq_row)


def _loss_head(y, target):
    width = y.shape[1]

    def fn(yv, tv):
        e = yv - tv
        part = 0.5 * jnp.sum(jnp.mean(e * e, axis=1, keepdims=True), axis=0, keepdims=True)
        return e * (1.0 / width), jnp.broadcast_to(part, (HALO, LANES))

    return _rowwise(fn, [(y, width, 0), (target, width, 0)], [], [(width, F32)], [(HALO, LANES)], name="loss_head")


def _adamw(w, g, m, v, *, name):
    shape = w.shape
    w2, g2, m2, v2 = (a.reshape(-1, shape[-1]) for a in (w, g, m, v))
    rows, width = w2.shape
    tr = _divisor_tile(rows, max(8, (1 << 19) // max(width, 1)), 8)
    bc1 = 1.0 - ADAM_B1 ** ADAM_STEP
    bc2 = 1.0 - ADAM_B2 ** ADAM_STEP

    def body(w_ref, g_ref, m_ref, v_ref, d_ref, mo_ref, vo_ref):
        gv = g_ref[...]
        mn = ADAM_B1 * m_ref[...] + (1.0 - ADAM_B1) * gv
        vn = ADAM_B2 * v_ref[...] + (1.0 - ADAM_B2) * (gv * gv)
        d_ref[...] = -ADAM_LR * ((mn / bc1) / (jnp.sqrt(vn / bc2) + ADAM_EPS) + ADAM_WD * w_ref[...])
        mo_ref[...] = mn
        vo_ref[...] = vn

    spec = pl.BlockSpec((tr, width), lambda i: (i, 0))
    outs = pl.pallas_call(
        body, name=name, grid=(rows // tr,), in_specs=[spec] * 4, out_specs=[spec] * 3,
        out_shape=[jax.ShapeDtypeStruct((rows, width), F32)] * 3,
        compiler_params=_params(("arbitrary",)),
    )(w2, g2, m2, v2)
    return tuple(o.reshape(shape) for o in outs)


HBM_SPEC = pl.BlockSpec(memory_space=pltpu.HBM)


def _position():
    return lax.axis_index("x"), lax.axis_index("y"), lax.axis_index("c")


def _other_chips(x, y):
    return [(1 - x, y), (x, 1 - y), (1 - x, 1 - y)]


class _Stream:
    def __init__(self, kind, size=0):
        self.kind, self.size = kind, size
        self.parts = 2 if kind == "heads" else 1

    def local(self, ref, k, part):
        if self.kind == "rows":
            return ref.at[:, pl.ds(k * self.size, self.size), :]
        if self.kind == "cols":
            return ref.at[:, :, pl.ds(k * self.size, self.size)]
        if self.kind == "heads":
            return ref.at[:, :, pl.ds(part * N_HEADS * HEAD_DIM + k * HEAD_DIM, HEAD_DIM)]
        if self.kind == "piece":
            return ref.at[k]
        return ref

    def shard(self, ref, part):
        if self.kind == "heads":
            return ref.at[:, :, pl.ds(part * HEAD_DIM, HEAD_DIM)]
        return ref

    def half_local(self, ref, k, part, cc, hd):
        if self.kind == "piece":
            return ref.at[k, pl.ds(cc * hd, hd)]
        return self.local(ref.at[pl.ds(cc * hd, hd)], k, part)


def _remote(src, dst, send_sems, recv_sems, idx, to):
    return pltpu.make_async_remote_copy(src_ref=src, dst_ref=dst, send_sem=send_sems.at[idx],
                                        recv_sem=recv_sems.at[idx], device_id=to, device_id_type=MESH)


def _comm_call(body, ins, out_shapes, n_remote, n_local, *, name):
    scratch = [pltpu.SemaphoreType.DMA((n_remote,)), pltpu.SemaphoreType.DMA((n_remote,))]
    if n_local:
        scratch.append(pltpu.SemaphoreType.DMA((n_local,)))
    return pl.pallas_call(
        body, name=name, in_specs=[HBM_SPEC] * len(ins), out_specs=[HBM_SPEC] * len(out_shapes), out_shape=out_shapes,
        scratch_shapes=scratch, compiler_params=pltpu.CompilerParams(has_side_effects=True),
    )(*ins)


def _gather_chips(shards, streams, out_shapes, *, name):
    n = len(shards)
    hd = shards[0].shape[0] // 2
    flat = [(t, part) for t in range(n) for part in range(streams[t].parts)]
    ns = len(flat)

    def body(*refs):
        s_refs, o_refs = refs[:n], refs[n:2 * n]
        send_sems, recv_sems = refs[2 * n:]
        x, y, c = _position()
        sibling = (x, y, 1 - c)
        chips = _other_chips(x, y)
        me = 2 * x + y
        sent = []
        for s, (t, part) in enumerate(flat):
            st = streams[t]
            sent.append(_remote(st.shard(s_refs[t], part), st.local(o_refs[t], me, part), send_sems, recv_sems,
                                6 * ns + s, sibling))
            sent[-1].start()
            src = st.shard(s_refs[t].at[pl.ds(c * hd, hd)], part)
            for j, (cx, cy) in enumerate(chips):
                sent.append(_remote(src, st.half_local(o_refs[t], me, part, c, hd), send_sems, recv_sems,
                                    3 * s + j, (cx, cy, c)))
                sent[-1].start()
        for s, (t, part) in enumerate(flat):
            st = streams[t]
            for j, (cx, cy) in enumerate(chips):
                blk = st.half_local(o_refs[t], 2 * cx + cy, part, c, hd)
                _remote(blk, blk, send_sems, recv_sems, 3 * s + j, (x, y, c)).wait_recv()
                sent.append(_remote(blk, blk, send_sems, recv_sems, 3 * ns + 3 * s + j, sibling))
                sent[-1].start()
        for s, (t, part) in enumerate(flat):
            st = streams[t]
            for j, (cx, cy) in enumerate(chips):
                blk = st.half_local(o_refs[t], 2 * cx + cy, part, 1 - c, hd)
                _remote(blk, blk, send_sems, recv_sems, 3 * ns + 3 * s + j, (x, y, c)).wait_recv()
            own = st.local(o_refs[t], me, part)
            _remote(own, own, send_sems, recv_sems, 6 * ns + s, (x, y, c)).wait_recv()
        for cp in sent:
            cp.wait_send()

    return _comm_call(body, shards, out_shapes, 7 * ns, 0, name=name)


def _sibling_take_other_half(gs, *, name):
    n = len(gs)
    hd = gs[0].shape[0] // 2

    def body(*refs):
        g_refs, o_refs = refs[:n], refs[n:2 * n]
        send_sems, recv_sems = refs[2 * n:]
        x, y, c = _position()
        copies = [_remote(g_refs[t].at[pl.ds((1 - c) * hd, hd)], o_refs[t], send_sems, recv_sems, t, (x, y, 1 - c))
                  for t in range(n)]
        for cp in copies:
            cp.start()
        for cp in copies:
            cp.wait()

    outs = [jax.ShapeDtypeStruct((hd,) + g.shape[1:], g.dtype) for g in gs]
    return _comm_call(body, gs, outs, n, 0, name=name)


def _chips_exchange(ps, streams, shard_shapes, *, name):
    n = len(ps)
    flat = [(t, part) for t in range(n) for part in range(streams[t].parts)]

    def body(*refs):
        p_refs, o_refs = refs[:n], refs[n:2 * n]
        send_sems, recv_sems = refs[2 * n:]
        x, y, c = _position()
        copies = []
        for s, (t, part) in enumerate(flat):
            st = streams[t]
            for j, (cx, cy) in enumerate(_other_chips(x, y)):
                copies.append(_remote(st.local(p_refs[t], 2 * cx + cy, part), st.shard(o_refs[t].at[j], part),
                                      send_sems, recv_sems, 3 * s + j, (cx, cy, c)))
        for cp in copies:
            cp.start()
        for cp in copies:
            cp.wait()

    outs = [jax.ShapeDtypeStruct((3,) + tuple(shp), p.dtype) for p, shp in zip(ps, shard_shapes)]
    return _comm_call(body, ps, outs, 3 * len(flat), 0, name=name)


def _sibling_join_halves(bufs, *, name):
    n = len(bufs)
    hd = bufs[0].shape[0] // 2

    def body(*refs):
        o_refs = refs[n:2 * n]
        send_sems, recv_sems = refs[2 * n:]
        x, y, c = _position()
        sent = []
        for t in range(n):
            mine = o_refs[t].at[pl.ds(c * hd, hd)]
            sent.append(_remote(mine, mine, send_sems, recv_sems, t, (x, y, 1 - c)))
            sent[-1].start()
        for t in range(n):
            theirs = o_refs[t].at[pl.ds((1 - c) * hd, hd)]
            _remote(theirs, theirs, send_sems, recv_sems, t, (x, y, c)).wait_recv()
        for cp in sent:
            cp.wait_send()

    return pl.pallas_call(
        body, name=name, in_specs=[HBM_SPEC] * n, out_specs=[HBM_SPEC] * n,
        out_shape=[jax.ShapeDtypeStruct(b.shape, b.dtype) for b in bufs],
        scratch_shapes=[pltpu.SemaphoreType.DMA((n,)), pltpu.SemaphoreType.DMA((n,))],
        input_output_aliases={t: t for t in range(n)},
        compiler_params=pltpu.CompilerParams(has_side_effects=True),
    )(*bufs)


def _row_tile(rows, width):
    return _divisor_tile(rows, max(16, (1 << 19) // width), 16)


def _add_own_half(g, got, c_idx, out_dtype, *, name):
    hd, r, w = got.shape
    tr = _row_tile(r, w)

    def body(c_ref, g_ref, a_ref, o_ref):
        o_ref[...] = (g_ref[...] + a_ref[...]).astype(o_ref.dtype)

    return pl.pallas_call(
        body, name=name,
        grid_spec=pltpu.PrefetchScalarGridSpec(
            num_scalar_prefetch=1, grid=(hd, r // tr),
            in_specs=[pl.BlockSpec((None, None, tr, w), lambda l, i, c_ref: (c_ref[0], l, i, 0)),
                      pl.BlockSpec((None, tr, w), lambda l, i, c_ref: (l, i, 0))],
            out_specs=pl.BlockSpec((None, tr, w), lambda l, i, c_ref: (l, i, 0))),
        out_shape=jax.ShapeDtypeStruct((hd, r, w), out_dtype),
        compiler_params=_params(("arbitrary", "arbitrary")),
    )(c_idx, g.reshape((2, hd) + g.shape[1:]), got)


def _sum_chips(p, got, place, stream, *, name):
    _, hd, rs, cs = got.shape
    wb = HEAD_DIM if stream.kind == "heads" else cs
    tr = _row_tile(rs, wb)
    kind, size = stream.kind, stream.size

    def own_index(l, i, g, k_ref, c_ref):
        k = k_ref[0]
        if kind == "rows":
            return (l, k * (size // tr) + i, 0)
        if kind == "cols":
            return (l, i, k)
        if kind == "heads":
            return (l, i, g * N_HEADS + k)
        if kind == "piece":
            return (k, l, i, 0)
        return (l, i, 0)

    own_blk = (None, None, tr, wb) if kind == "piece" else (None, tr, wb)

    def body(k_ref, c_ref, p_ref, fx_ref, fy_ref, fxy_ref, o_ref):
        f = lambda r: r[...].astype(F32)
        o_ref[...] = (f(p_ref) + f(fy_ref)) + (f(fx_ref) + f(fxy_ref))

    def rel(j):
        return pl.BlockSpec((None, None, tr, wb), functools.partial(lambda l, i, g, k_ref, c_ref, j: (j, l, i, g), j=j))

    return pl.pallas_call(
        body, name=name,
        grid_spec=pltpu.PrefetchScalarGridSpec(
            num_scalar_prefetch=2, grid=(hd, rs // tr, stream.parts),
            in_specs=[pl.BlockSpec(own_blk, own_index), rel(0), rel(1), rel(2)],
            out_specs=pl.BlockSpec((None, tr, wb), lambda l, i, g, k_ref, c_ref: (c_ref[0] * hd + l, i, g))),
        out_shape=jax.ShapeDtypeStruct((2 * hd, rs, cs), F32),
        compiler_params=_params(("arbitrary", "arbitrary", "arbitrary")),
    )(place[0], place[1], p, got, got, got)


MATRICES = ("w_in", "w_uq", "w_ukv", "w_out", "w_gate_up", "w_down", "w_ple", "w_ple_gate", "conv_w")
VECTORS = ("a_log", "dt_bias", "gdn_norm_g", "q_norm_g", "kv_norm_g", "ln1_g", "ln1_b", "ln2_g", "ln2_b")
WEIGHTS = ("w_in", "conv_w", "a_log", "dt_bias", "gdn_norm_g", "q_norm_g", "w_uq", "kv_norm_g", "w_ukv", "w_out",
           "ln1_g", "ln1_b", "w_gate_up", "w_down", "ln2_g", "ln2_b", "w_ple", "w_ple_gate")
ROW_SHARDED = ("w_out", "w_down", "w_ple_gate")
N_CHIPS = 4


def _stream_of(name, shard_shape):
    if name in ("w_in", "w_uq"):
        return _Stream("piece")
    if name == "w_ukv":
        return _Stream("heads")
    if name in ROW_SHARDED:
        return _Stream("rows", shard_shape[0])
    return _Stream("cols", shard_shape[1])


def _pack_vectors(vecs, depth):
    flat = jnp.concatenate([vecs[n].reshape(depth, -1) for n in VECTORS], axis=1)
    pad = jnp.zeros((depth, VEC_ROWS * LANES - flat.shape[1]), F32)
    return jnp.concatenate([flat, pad], axis=1).reshape(depth, VEC_ROWS, LANES)


def _unpack_vectors(packed, shapes):
    depth = packed.shape[0]
    flat = packed.reshape(depth, VEC_ROWS * LANES)
    out, off = {}, 0
    for n in VECTORS:
        out[n] = flat[:, off:off + shapes[n][1]]
        off += shapes[n][1]
    return out


VEC_ROWS = 40


class _Dims:
    def __init__(self, d_model, in_width, q_lora, kv_lora, d_ff2, ple_dim):
        self.d = d_model
        self.hw = N_HEADS * HEAD_DIM
        self.in_width = in_width
        self.q_lora, self.kv_lora = q_lora, kv_lora
        self.ff2 = d_ff2
        self.ple = ple_dim
        self.c_kv0 = 4 * self.hw
        self.c_q0 = self.c_kv0 + kv_lora
        self.misc0 = self.c_q0 + q_lora
        self.h_width = self.misc0 + LANES
        assert self.c_kv0 % kv_lora == 0 and self.c_q0 % q_lora == 0 and self.misc0 % LANES == 0
        self.g_beta = 4 * self.hw
        self.g_a = self.g_beta + N_HEADS
        self.g_cq = self.g_a + N_HEADS
        self.g_ckv = self.g_cq + q_lora
        self.g_kr = self.g_ckv + kv_lora
        assert self.g_kr + ROPE_DIM == in_width

    def w_in_local(self, w):
        pad = jnp.zeros(w.shape[:-1] + (self.h_width - self.in_width,), w.dtype)
        return jnp.concatenate([w[..., :self.g_beta], w[..., self.g_ckv:self.g_kr], w[..., self.g_cq:self.g_ckv],
                                w[..., self.g_kr:], w[..., self.g_beta:self.g_cq], pad], axis=-1)

    def w_in_global(self, d):
        m = self.misc0
        return jnp.concatenate([d[..., :self.c_kv0], d[..., m + MISC_BETA0:m + MISC_A0 + N_HEADS],
                                d[..., self.c_q0:self.misc0], d[..., self.c_kv0:self.c_q0], d[..., m:m + ROPE_DIM]],
                               axis=-1)

    def w_uq_local(self, w):
        r = w.reshape(w.shape[:-1] + (N_HEADS, HEAD_DIM + ROPE_DIM))
        r = jnp.pad(r, [(0, 0)] * (r.ndim - 1) + [(0, QK_DIM - HEAD_DIM - ROPE_DIM)])
        return r.reshape(w.shape[:-1] + (N_HEADS * QK_DIM,))

    def w_uq_global(self, d):
        r = d.reshape(d.shape[:-1] + (N_HEADS, QK_DIM))[..., :HEAD_DIM + ROPE_DIM]
        return r.reshape(d.shape[:-1] + (N_HEADS * (HEAD_DIM + ROPE_DIM),))


def _lane_padded(n):
    return -(-n // LANES) * LANES


def _pad_lanes(a):
    pad = _lane_padded(a.shape[-1]) - a.shape[-1]
    return a if pad == 0 else jnp.pad(a, [(0, 0)] * (a.ndim - 1) + [(0, pad)])


def _lane_row(vec, lane0):
    pad = LANES - lane0 - vec.shape[0]
    return jnp.concatenate([jnp.zeros((lane0,), F32), vec.astype(F32), jnp.zeros((pad,), F32)])[None, :]


def _layer_fwd(dm, alpha, x, xb, p_i, cos_t, sin_t, wl, tag):
    d, hw = dm.d, dm.hw
    nm = lambda s: f"{s}_{tag}"
    mm = functools.partial(_matmul, layer=wl["layer"])
    h = mm(xb, wl["w_in"], dims="nn", name=nm("f_in"))
    misc_cb = dm.misc0 // LANES

    u = _conv_fwd(h, wl["conv_w"], 3 * hw, name=nm("f_conv"))
    qn, kn, vg, gb = _rowwise(_gdn_act, [(u, 3 * hw, 0), (h, LANES, misc_cb)], [wl["alog_row"], wl["dtb_row"]],
                              [(hw, F32), (hw, F32), (hw, F32), (LANES, F32)], name=nm("f_gdn_act"))
    prep, tinv = _gdn_prep_fwd(qn, kn, vg, gb, name=nm("f_gdn_prep"))
    o_gdn, s_all = _gdn_seq_fwd(prep, name=nm("f_gdn_seq"))
    (og,) = _rowwise(lambda o, z, g: (_gdn_out(o, z, g),), [(o_gdn, hw, 0), (h, hw, 3)], [wl["gn_row"]],
                     [(hw, BF16)], name=nm("f_gdn_out"))

    cqn, ckvn = _rowwise(_mla_norm, [(h, dm.kv_lora, dm.c_kv0 // dm.kv_lora), (h, dm.q_lora, dm.c_q0 // dm.q_lora)],
                         [wl["kvg_row"], wl["qg_row"]], [(dm.q_lora, BF16), (dm.kv_lora, BF16)], name=nm("f_mla_norm"))
    qm = mm(cqn, wl["w_uq"], dims="nn", name=nm("f_uq"))
    kvm = mm(ckvn, wl["w_ukv"], dims="nn", name=nm("f_ukv"))
    scale = (HEAD_DIM + ROPE_DIM) ** -0.5
    qk_fn = functools.partial(_mla_qk, scale)
    qa, ka, va = _rowwise(qk_fn, [(qm, N_HEADS * QK_DIM, 0), (kvm, 2 * hw, 0), (h, LANES, misc_cb),
                                  (cos_t, LANES, 0), (sin_t, LANES, 0)], [],
                          [(N_HEADS * QK_DIM, BF16), (N_HEADS * QK_DIM, BF16), (hw, BF16)], name=nm("f_mla_qk"))
    kt = _blocked_transpose(ka, QK_DIM)
    v_heads = va.reshape(va.shape[0], N_HEADS, HEAD_DIM)
    v1 = jnp.concatenate([v_heads, jnp.ones_like(v_heads)], axis=2).reshape(va.shape[0], 2 * hw)
    o_mla, lse = _attn_fwd(qa, kt, v1, name=nm("f_attn"))

    om = jnp.concatenate([og, o_mla], axis=1)
    mix = mm(om, wl["w_out"], dims="nn", name=nm("f_out"))
    ln1 = lambda xv, yv, g, b: (_layer_norm(alpha * xv + yv, g, b),) * 2
    x1, x1b = _rowwise(ln1, [(x, d, 0), (mix, d, 0)], [wl["ln1_g"], wl["ln1_b"]], [(d, F32), (d, BF16)], name=nm("f_ln1"))

    gu = mm(x1b, wl["w_gate_up"], dims="nn", name=nm("f_gate_up"), out_dtype=BF16)
    dn, act = mm(gu, wl["w_down"], dims="nn", name=nm("f_down"), a_gated=True, tm=GATED_TILE)
    x2, x2b = _rowwise(ln1, [(x1, d, 0), (dn, d, 0)], [wl["ln2_g"], wl["ln2_b"]], [(d, F32), (d, BF16)], name=nm("f_ln2"))

    pg = mm(x2b, wl["w_ple_gate"], dims="nn", name=nm("f_ple_gate"))
    pe = mm(p_i, wl["w_ple"], dims="nn", name=nm("f_ple"))
    out, outb = _rowwise(lambda a, b, c_: (_ple_out(a, b, c_),) * 2, [(x2, d, 0), (pg, d, 0), (pe, d, 0)], [],
                         [(d, F32), (d, BF16)], name=nm("f_ple_out"))
    saved = dict(x=x, xb=xb, p_i=p_i, h=h, u=u, qn=qn, kn=kn, vg=vg, gb=gb, prep=prep, tinv=tinv, s_all=s_all, o_gdn=o_gdn, cqn=cqn, ckvn=ckvn,
                 qm=qm, kvm=kvm, qa=qa, ka=ka, va=va, o_mla=o_mla, lse=lse, om=om, mix=mix, x1=x1, x1b=x1b, gu=gu,
                 act=act, dn=dn, x2=x2, x2b=x2b, pg=pg, pe=pe)
    return out, outb, saved


def _layer_bwd(dm, alpha, dout, sv, cos_t, sin_t, wl, gbuf, tag):
    d, hw = dm.d, dm.hw
    t = dout.shape[0]
    nm = lambda s: f"{s}_{tag}"
    gr = {}
    gbuf = dict(gbuf)
    misc_cb = dm.misc0 // LANES
    mm = functools.partial(_matmul, layer=wl["layer"])

    def wgrad(name_, a, g):
        gbuf[name_] = mm(a, g, dims="tn", name=nm("b_" + name_), into=gbuf[name_], tm=1408, tn=1408, tk=1024)

    dx2_a, dpg, dpe = _rowwise(_vjp_fn(_ple_out, 3, 1), [(sv["x2"], d, 0), (sv["pg"], d, 0), (sv["pe"], d, 0), (dout, d, 0)],
                               [], [(d, F32), (d, BF16), (d, BF16)], name=nm("b_ple_out"))
    wgrad("w_ple", sv["p_i"], dpe)
    wgrad("w_ple_gate", sv["x2b"], dpg)
    dx2 = mm(dpg, wl["w_ple_gate"], dims="nt", c=dx2_a, name=nm("b_x2"))

    def ln_bwd(xv, yv, ct, g, b):
        _, pull = jax.vjp(lambda a_, b_, c_, d_: _layer_norm(alpha * a_ + b_, c_, d_), xv, yv, g, b)
        return pull(ct)

    dx1_a, ddn, gr["ln2_g"], gr["ln2_b"] = _rowwise(
        ln_bwd, [(sv["x1"], d, 0), (sv["dn"], d, 0), (dx2, d, 0)], [wl["ln2_g"], wl["ln2_b"]],
        [(d, F32), (d, BF16)], [(1, d), (1, d)], name=nm("b_ln2"))
    wgrad("w_down", sv["act"], ddn)
    dact = mm(ddn, wl["w_down"], dims="nt", name=nm("b_act"), out_dtype=BF16)
    (dgu,) = _rowwise(_vjp_fn(_swiglu, 1, 1), [(sv["gu"], dm.ff2, 0), (dact, dm.ff2 // 2, 0)], [], [(dm.ff2, BF16)],
                      name=nm("b_swiglu"))
    wgrad("w_gate_up", sv["x1b"], dgu)
    dx1 = mm(dgu, wl["w_gate_up"], dims="nt", c=dx1_a, name=nm("b_x1"))

    dx_a, dmix, gr["ln1_g"], gr["ln1_b"] = _rowwise(
        ln_bwd, [(sv["x"], d, 0), (sv["mix"], d, 0), (dx1, d, 0)], [wl["ln1_g"], wl["ln1_b"]],
        [(d, F32), (d, BF16)], [(1, d), (1, d)], name=nm("b_ln1"))
    wgrad("w_out", sv["om"], dmix)
    dom = mm(dmix, wl["w_out"], dims="nt", name=nm("b_om"))

    nq = t // min(ATT_TILE, t)
    delta = _attn_delta(dom, sv["o_mla"], name=nm("b_attn_delta"))
    lse_row = sv["lse"].reshape(N_HEADS, nq, 1, t // nq)
    delta_row = delta[:, :N_HEADS].T.reshape(N_HEADS, nq, 1, t // nq)
    do_b = dom[:, hw:].astype(BF16)
    dka, dva, dqa = _attn_bwd(sv["qa"], _blocked_transpose(sv["qa"], QK_DIM), sv["ka"], sv["va"], lse_row, delta_row,
                              do_b, _blocked_transpose(do_b, HEAD_DIM), name=nm("b_attn"))
    scale = (HEAD_DIM + ROPE_DIM) ** -0.5
    qk_fn = functools.partial(_mla_qk, scale)

    def qk_bwd(qm, kvm, misc, cs, sn, g_q, g_k, g_v):
        _, pull = jax.vjp(lambda a, b, c_: qk_fn(a, b, c_, cs, sn), qm, kvm, misc)
        return pull((g_q, g_k, g_v))

    dqm, dkvm, dmisc_rope = _rowwise(
        qk_bwd, [(sv["qm"], N_HEADS * QK_DIM, 0), (sv["kvm"], 2 * hw, 0), (sv["h"], LANES, misc_cb), (cos_t, LANES, 0),
                 (sin_t, LANES, 0), (dqa, N_HEADS * QK_DIM, 0), (dka, N_HEADS * QK_DIM, 0), (dva, hw, 0)], [],
        [(N_HEADS * QK_DIM, BF16), (2 * hw, BF16), (LANES, F32)], name=nm("b_mla_qk"))
    wgrad("w_uq", sv["cqn"], dqm)
    wgrad("w_ukv", sv["ckvn"], dkvm)
    dcqn = mm(dqm, wl["w_uq"], dims="nt", name=nm("b_cqn"))
    dckvn = mm(dkvm, wl["w_ukv"], dims="nt", name=nm("b_ckvn"))

    def norm_bwd(ckv, cq, g_q, g_kv, kvg, qg):
        _, pull = jax.vjp(_mla_norm, ckv, cq, kvg, qg)
        return pull((g_q, g_kv))

    dckv, dcq, gr["kvg_row"], gr["qg_row"] = _rowwise(
        norm_bwd, [(sv["h"], dm.kv_lora, dm.c_kv0 // dm.kv_lora), (sv["h"], dm.q_lora, dm.c_q0 // dm.q_lora),
                   (dcqn, dm.q_lora, 0), (dckvn, dm.kv_lora, 0)], [wl["kvg_row"], wl["qg_row"]],
        [(dm.kv_lora, BF16), (dm.q_lora, BF16)], [(1, dm.kv_lora), (1, dm.q_lora)], name=nm("b_mla_norm"))

    def gout_bwd(o, z, g_o, gn):
        _, pull = jax.vjp(_gdn_out, o, z, gn)
        return pull(g_o)

    do_gdn, dz, gr["gn_row"] = _rowwise(gout_bwd, [(sv["o_gdn"], hw, 0), (sv["h"], hw, 3), (dom, hw, 0)], [wl["gn_row"]],
                                        [(hw, F32), (hw, BF16)], [(1, HEAD_DIM)], name=nm("b_gdn_out"))
    dprep = _gdn_seq_bwd(sv["prep"], sv["s_all"], do_gdn, name=nm("b_gdn_seq"))
    dqn, dkn, dvg, dgb = _gdn_prep_bwd(sv["qn"], sv["kn"], sv["vg"], sv["gb"], sv["tinv"], dprep, name=nm("b_gdn_prep"))

    def act_bwd(u, misc, g_q, g_k, g_v, g_gb, g_rope, alog, dtb):
        _, pull = jax.vjp(_gdn_act, u, misc, alog, dtb)
        du_, dmisc_, dalog_, ddtb_ = pull((g_q, g_k, g_v, g_gb))
        return du_, dmisc_ + g_rope, dalog_, ddtb_

    du, dmisc, gr["alog_row"], gr["dtb_row"] = _rowwise(
        act_bwd, [(sv["u"], 3 * hw, 0), (sv["h"], LANES, misc_cb), (dqn, hw, 0), (dkn, hw, 0), (dvg, hw, 0),
                  (dgb, LANES, 0), (dmisc_rope, LANES, 0)], [wl["alog_row"], wl["dtb_row"]],
        [(3 * hw, F32), (LANES, BF16)], [(1, LANES), (1, LANES)], name=nm("b_gdn_act"))
    dqkv, dconv = _conv_bwd(du, sv["h"], wl["conv_w"], 3 * hw, name=nm("b_conv"))
    gr["conv_w"] = dconv[:CONV_W]

    dh = jnp.concatenate([dqkv, dz, dckv, dcq, dmisc], axis=1)
    wgrad("w_in", sv["xb"], dh)
    dx = mm(dh, wl["w_in"], dims="nt", c=dx_a, name=nm("b_x"), tk=1408)
    return dx, gbuf, gr


LOCAL_MATRICES = ("w_in", "w_uq", "w_ukv", "w_out", "w_gate_up", "w_down", "w_ple", "w_ple_gate")


def _layer_weights(mats, vecs, layer):
    wl = {n: mats[n] for n in LOCAL_MATRICES}
    wl["layer"] = layer
    wl["conv_w"] = mats["conv_w"][layer]
    wl["alog_row"] = _lane_row(vecs["a_log"][layer], MISC_A0)
    wl["dtb_row"] = _lane_row(vecs["dt_bias"][layer], MISC_A0)
    wl["gn_row"] = vecs["gdn_norm_g"][layer][None, :]
    wl["qg_row"] = vecs["q_norm_g"][layer][None, :]
    wl["kvg_row"] = vecs["kv_norm_g"][layer][None, :]
    for n in ("ln1_g", "ln1_b", "ln2_g", "ln2_b"):
        wl[n] = vecs[n][layer][None, :]
    return wl


def _vector_grads(gr):
    out = {"a_log": gr["alog_row"][0, MISC_A0:MISC_A0 + N_HEADS], "dt_bias": gr["dtb_row"][0, MISC_A0:MISC_A0 + N_HEADS],
           "gdn_norm_g": gr["gn_row"][0], "q_norm_g": gr["qg_row"][0], "kv_norm_g": gr["kvg_row"][0]}
    for n in ("ln1_g", "ln1_b", "ln2_g", "ln2_b"):
        out[n] = gr[n][0]
    return out


def _local_step(dm, x, p, positions, target, mats, vecs):
    depth = p.shape[0]
    alpha = (2.0 * depth) ** 0.25
    freq = ROPE_THETA ** (-jnp.arange(0, ROPE_DIM, 2, dtype=F32) / ROPE_DIM)
    inv_freq_row = _lane_row(jnp.concatenate([freq, freq]), 0)
    cos_t, sin_t = _rope_tables(positions.reshape(-1, 1), inv_freq_row, name="rope_tables")

    wls = [_layer_weights(mats, vecs, i) for i in range(depth)]
    saved = []
    cur, cur_b = x, x
    for i in range(depth):
        cur, cur_b, sv = _layer_fwd(dm, alpha, cur, cur_b, p[i], cos_t, sin_t, wls[i], f"l{i}")
        saved.append(sv)
    dy, loss_blk = _loss_head(cur, target)
    gbuf = {n: depth for n in LOCAL_MATRICES}
    conv_g, vec_g = [None] * depth, [None] * depth
    for i in reversed(range(depth)):
        dy, gbuf, gr = _layer_bwd(dm, alpha, dy, saved[i], cos_t, sin_t, wls[i], gbuf, f"l{i}")
        conv_g[i] = gr["conv_w"]
        vec_g[i] = _vector_grads(gr)
    vec_grads = {n: jnp.stack([vec_g[i][n] for i in range(depth)]) for n in VECTORS}
    return loss_blk[0, 0], dy, gbuf, jnp.stack(conv_g), vec_grads


def kernel(x, p, positions, w_in, conv_w, a_log, dt_bias, gdn_norm_g, q_norm_g, w_uq, kv_norm_g, w_ukv, w_out, ln1_g, ln1_b, w_gate_up, w_down, ln2_g, ln2_b, w_ple, w_ple_gate, loss_target, m_w_in, m_conv_w, m_a_log, m_dt_bias, m_gdn_norm_g, m_q_norm_g, m_w_uq, m_kv_norm_g, m_w_ukv, m_w_out, m_ln1_g, m_ln1_b, m_w_gate_up, m_w_down, m_ln2_g, m_ln2_b, m_w_ple, m_w_ple_gate, v_w_in, v_conv_w, v_a_log, v_dt_bias, v_gdn_norm_g, v_q_norm_g, v_w_uq, v_kv_norm_g, v_w_ukv, v_w_out, v_ln1_g, v_ln1_b, v_w_gate_up, v_w_down, v_ln2_g, v_ln2_b, v_w_ple, v_w_ple_gate):
    w = dict(w_in=w_in, conv_w=conv_w, a_log=a_log, dt_bias=dt_bias, gdn_norm_g=gdn_norm_g, q_norm_g=q_norm_g, w_uq=w_uq,
             kv_norm_g=kv_norm_g, w_ukv=w_ukv, w_out=w_out, ln1_g=ln1_g, ln1_b=ln1_b, w_gate_up=w_gate_up, w_down=w_down,
             ln2_g=ln2_g, ln2_b=ln2_b, w_ple=w_ple, w_ple_gate=w_ple_gate)
    m = dict(w_in=m_w_in, conv_w=m_conv_w, a_log=m_a_log, dt_bias=m_dt_bias, gdn_norm_g=m_gdn_norm_g, q_norm_g=m_q_norm_g,
             w_uq=m_w_uq, kv_norm_g=m_kv_norm_g, w_ukv=m_w_ukv, w_out=m_w_out, ln1_g=m_ln1_g, ln1_b=m_ln1_b,
             w_gate_up=m_w_gate_up, w_down=m_w_down, ln2_g=m_ln2_g, ln2_b=m_ln2_b, w_ple=m_w_ple, w_ple_gate=m_w_ple_gate)
    v = dict(w_in=v_w_in, conv_w=v_conv_w, a_log=v_a_log, dt_bias=v_dt_bias, gdn_norm_g=v_gdn_norm_g, q_norm_g=v_q_norm_g,
             w_uq=v_w_uq, kv_norm_g=v_kv_norm_g, w_ukv=v_w_ukv, w_out=v_w_out, ln1_g=v_ln1_g, ln1_b=v_ln1_b,
             w_gate_up=v_w_gate_up, w_down=v_w_down, ln2_g=v_ln2_g, ln2_b=v_ln2_b, w_ple=v_w_ple, w_ple_gate=v_w_ple_gate)
    depth = w_in.shape[0]
    assert depth % 2 == 0
    hd = depth // 2
    dm = _Dims(x.shape[2], N_CHIPS * w_in.shape[2], w_uq.shape[1], w_ukv.shape[1], N_CHIPS * w_gate_up.shape[2], p.shape[3])
    cx, cy, cc = lax.axis_index("x"), lax.axis_index("y"), lax.axis_index("c")
    chip = 2 * cx + cy

    g_streams = [_stream_of(n, w[n].shape[1:]) for n in MATRICES]
    shards = [w[n] if n == "conv_w" else w[n].astype(BF16) for n in MATRICES]
    shards = [_pad_lanes(s) if st.kind == "piece" else s for s, st in zip(shards, g_streams)]
    g_shapes = []
    for s, st in zip(shards, g_streams):
        if st.kind == "piece":
            shape = (N_CHIPS,) + s.shape
        elif st.kind == "rows":
            shape = (depth, N_CHIPS * s.shape[1], s.shape[2])
        else:
            shape = (depth, s.shape[1], N_CHIPS * s.shape[2])
        g_shapes.append(jax.ShapeDtypeStruct(shape, s.dtype))
    mats = dict(zip(MATRICES, _gather_chips(shards, g_streams, g_shapes, name="gather_weights")))
    for n, to_local in (("w_in", dm.w_in_local), ("w_uq", dm.w_uq_local)):
        pieces = jnp.moveaxis(mats[n][..., :w[n].shape[2]], 0, 2)
        mats[n] = to_local(pieces.reshape(pieces.shape[:2] + (-1,)))
    vecs = {n: w[n] for n in VECTORS}

    loss_local, grad_x, gbuf, conv_g, vec_g = _local_step(dm, x[0], p[:, 0], positions[0], loss_target[0], mats, vecs)
    loss = lax.psum(loss_local, ("x", "y", "c"))

    names = list(LOCAL_MATRICES) + ["conv_w", "vectors"]
    gs = [gbuf[n] for n in LOCAL_MATRICES] + [conv_g, _pack_vectors(vec_g, depth)]
    wire = [BF16] * len(LOCAL_MATRICES) + [F32, F32]
    r_streams = [_stream_of(n, w[n].shape[1:]) for n in LOCAL_MATRICES]
    r_streams += [_stream_of("conv_w", w["conv_w"].shape[1:]), _Stream("whole")]
    shard_shapes = [(hd, w[n].shape[1], _lane_padded(w[n].shape[2])) if st.kind == "piece" else (hd,) + w[n].shape[1:]
                    for n, st in zip(LOCAL_MATRICES, r_streams)]
    shard_shapes += [(hd,) + w["conv_w"].shape[1:], (hd, VEC_ROWS, LANES)]
    c_idx = cc.reshape(1).astype(jnp.int32)
    place = (chip.reshape(1).astype(jnp.int32), c_idx)
    from_sibling = _sibling_take_other_half(gs, name="reduce_sibling")
    chip_sum = [_add_own_half(g, a, c_idx, dt, name=f"reduce_add_{n}")
                for g, a, dt, n in zip(gs, from_sibling, wire, names)]
    for i, n in enumerate(names):
        if r_streams[i].kind == "piece":
            glob = dm.w_in_global(chip_sum[i]) if n == "w_in" else dm.w_uq_global(chip_sum[i])
            glob = glob.reshape(glob.shape[:2] + (N_CHIPS, glob.shape[2] // N_CHIPS))
            chip_sum[i] = jnp.moveaxis(_pad_lanes(glob), 2, 0)
    from_chips = _chips_exchange(chip_sum, r_streams, shard_shapes, name="reduce_chips")
    halves = [_sum_chips(ps, got, place, st, name=f"reduce_sum_{n}")
              for ps, got, st, n in zip(chip_sum, from_chips, r_streams, names)]
    joined = dict(zip(names, _sibling_join_halves(halves, name="reduce_join")))
    joined.update(_unpack_vectors(joined.pop("vectors"), {n: w[n].shape for n in VECTORS}))

    grad_w, delta_w, new_m, new_v = {}, {}, {}, {}
    for n in WEIGHTS:
        grad_w[n] = joined[n][..., :w[n].shape[-1]]
        delta_w[n], new_m[n], new_v[n] = _adamw(w[n], grad_w[n], m[n], v[n], name=f"adamw_{n}")
    return (loss, grad_x[None], *[grad_w[n] for n in WEIGHTS], *[delta_w[n] for n in WEIGHTS],
            *[new_m[n] for n in WEIGHTS], *[new_v[n] for n in WEIGHTS])
```

```python
import functools

import jax
import jax.numpy as jnp
from jax import lax
from jax.experimental import pallas as pl
from jax.experimental.pallas import tpu as pltpu

F32 = jnp.float32
BF16 = jnp.bfloat16
MESH = pl.DeviceIdType.MESH

CHUNK = 64
N_HEADS = 4
HEAD_DIM = 128
ROPE_DIM = 64
ROPE_THETA = 10000.0
LN_EPS = 1e-5
RMS_EPS = 1e-6
ADAM_LR, ADAM_B1, ADAM_B2, ADAM_EPS, ADAM_WD, ADAM_STEP = 0.001, 0.9, 0.999, 1e-08, 0.01, 10

LANES = 128
VMEM_LIMIT = 48 * 1024 * 1024
ROW_TILE = 512
WIDE_ROW_TILE = 256
WIDE_COLS = 2048
GATED_TILE = 512
SUB_ROWS = 16
MAX_SUB_ROWS = 64
VREG_FILE_ELEMS = 64 * 8 * LANES

MISC_BETA0 = ROPE_DIM
MISC_A0 = ROPE_DIM + N_HEADS

NN = (((1,), (0,)), ((), ()))
NT = (((1,), (1,)), ((), ()))
TN = (((0,), (0,)), ((), ()))


def _params(sem=None):
    return pltpu.CompilerParams(dimension_semantics=sem, vmem_limit_bytes=VMEM_LIMIT)


def _divisor_tile(dim, target, unit):
    best = None
    t = unit
    while t <= min(dim, target):
        if dim % t == 0:
            best = t
        t += unit
    return best if best is not None else dim


BATCHED = {NN: (((2,), (1,)), ((0,), (0,))), NT: (((2,), (2,)), ((0,), (0,))), TN: (((1,), (1,)), ((0,), (0,)))}


def _make_dots():
    def raw(a, b, dims):
        if a.ndim == 3:
            dims = BATCHED[dims]
        return lax.dot_general(a.astype(BF16), b.astype(BF16), dims, preferred_element_type=F32)

    @jax.custom_vjp
    def nn(a, b):
        return raw(a, b, NN)

    @jax.custom_vjp
    def nt(a, b):
        return raw(a, b, NT)

    @jax.custom_vjp
    def tn(a, b):
        return raw(a, b, TN)

    nn.defvjp(lambda a, b: (raw(a, b, NN), (a, b)), lambda r, g: (nt(g, r[1]), tn(r[0], g)))
    nt.defvjp(lambda a, b: (raw(a, b, NT), (a, b)), lambda r, g: (nn(g, r[1]), tn(g, r[0])))
    tn.defvjp(lambda a, b: (raw(a, b, TN), (a, b)), lambda r, g: (nt(r[1], g), nn(r[0], g)))
    return nn, nt, tn


_nn, _nt, _tn = _make_dots()


def _matmul(a, b, *, dims, name, c=None, out_dtype=F32, tm=1024, tn=1408, tk=1408, layer=None, into=None,
            a_gated=False):
    b_shape = b.shape[-2:]
    a_shape = (a.shape[0], a.shape[1] // 2) if a_gated else a.shape
    if dims == "nn":
        (m, k), (k2, n) = a_shape, b_shape
    elif dims == "nt":
        (m, k), (n, k2) = a_shape, b_shape
    else:
        (k, m), (k2, n) = a_shape, b_shape
    assert k == k2, (a.shape, b.shape, dims)
    tm = _divisor_tile(m, tm, LANES)
    tn = _divisor_tile(n, tn, LANES)
    tk = _divisor_tile(k, tk, LANES)
    nk = k // tk
    dn = {"nn": NN, "nt": NT, "tn": TN}[dims]
    if dims == "tn":
        a_blk, a_idx, up_off = (tk, tm), (lambda i, j, kk: (kk, i)), m // tm
    else:
        a_blk, a_idx, up_off = (tm, tk), (lambda i, j, kk: (i, kk)), k // tk
    a_spec = pl.BlockSpec(a_blk, a_idx)
    up_spec = pl.BlockSpec(a_blk, lambda i, j, kk: (a_idx(i, j, kk)[0], a_idx(i, j, kk)[1] + up_off))
    b_blk, b_idx = ((tn, tk), lambda i, j, kk: (j, kk)) if dims == "nt" else ((tk, tn), lambda i, j, kk: (kk, j))
    if b.ndim == 3:
        b_spec = pl.BlockSpec((None,) + b_blk, lambda i, j, kk: (layer,) + b_idx(i, j, kk))
    else:
        b_spec = pl.BlockSpec(b_blk, b_idx)
    c_spec = pl.BlockSpec((tm, tn), lambda i, j, kk: (i, j))
    if isinstance(into, int):
        o_spec = pl.BlockSpec((None, tm, tn), lambda i, j, kk: (layer, i, j))
        out_shape = jax.ShapeDtypeStruct((into, m, n), out_dtype)
        into = None
    elif into is not None:
        assert into.shape[1:] == (m, n) and into.dtype == out_dtype
        o_spec = pl.BlockSpec((None, tm, tn), lambda i, j, kk: (layer, i, j))
        out_shape = jax.ShapeDtypeStruct(into.shape, into.dtype)
    else:
        o_spec = c_spec
        out_shape = jax.ShapeDtypeStruct((m, n), out_dtype)
    has_c = c is not None

    n_a = 2 if a_gated else 1

    def body(*refs):
        b_ref = refs[n_a]
        c_ref = refs[n_a + 1] if has_c else None
        acc_ref = refs[-1]
        o_ref = refs[-3] if a_gated else refs[-2]
        kk = pl.program_id(2)

        @pl.when(kk == 0)
        def _():
            if has_c:
                acc_ref[...] = c_ref[...].astype(F32)
            else:
                acc_ref[...] = jnp.zeros_like(acc_ref)

        if a_gated:
            a_val = (_silu(refs[0][...].astype(F32)) * refs[1][...].astype(F32)).astype(BF16)
            refs[-2][...] = a_val
        else:
            a_val = refs[0][...].astype(BF16)
        acc_ref[...] += lax.dot_general(a_val, b_ref[...].astype(BF16), dn, preferred_element_type=F32)

        @pl.when(kk == nk - 1)
        def _():
            o_ref[...] = acc_ref[...].astype(o_ref.dtype)

    ins = ([a, a] if a_gated else [a]) + [b] + ([c] if has_c else [])
    specs = ([a_spec, up_spec] if a_gated else [a_spec]) + [b_spec] + ([c_spec] if has_c else [])
    aliases = {}
    if into is not None:
        aliases = {len(ins): 0}
        ins.append(into)
        specs.append(pl.BlockSpec(memory_space=pl.ANY))
    if a_gated:
        assert dims == "nn" and n == tn
        o_spec, out_shape = [o_spec, a_spec], [out_shape, jax.ShapeDtypeStruct((m, k), BF16)]
    return pl.pallas_call(
        body, name=name, grid=(m // tm, n // tn, nk), in_specs=specs, out_specs=o_spec, out_shape=out_shape,
        scratch_shapes=[pltpu.VMEM((tm, tn), F32)], input_output_aliases=aliases,
        compiler_params=_params(("arbitrary", "arbitrary", "arbitrary")),
    )(*ins)


def _rowwise(fn, rows, params, outs, accs=(), *, name):
    t = rows[0][0].shape[0]
    widest = max([w for _, w, _ in rows] + [w for w, _ in outs])
    tm = min(WIDE_ROW_TILE if widest > WIDE_COLS else ROW_TILE, t)
    sub = SUB_ROWS
    while sub < MAX_SUB_ROWS and 2 * sub * widest <= VREG_FILE_ELEMS:
        sub *= 2
    assert t % tm == 0 and tm % sub == 0
    n_rows, n_par, n_out, n_acc = len(rows), len(params), len(outs), len(accs)

    def body(*refs):
        row_refs = refs[:n_rows]
        par_refs = refs[n_rows:n_rows + n_par]
        out_refs = refs[n_rows + n_par:n_rows + n_par + n_out]
        acc_refs = refs[n_rows + n_par + n_out:]
        if n_acc:
            @pl.when(pl.program_id(0) == 0)
            def _():
                for a_ref in acc_refs:
                    a_ref[...] = jnp.zeros_like(a_ref)

        def step(r, carry):
            sl = pl.ds(pl.multiple_of(r * sub, sub), sub)
            vals = [ref[sl, :].astype(F32) for ref in row_refs] + [ref[...] for ref in par_refs]
            res = fn(*vals)
            for o_ref, val in zip(out_refs, res[:n_out]):
                o_ref[sl, :] = val.astype(o_ref.dtype)
            for a_ref, val in zip(acc_refs, res[n_out:]):
                a_ref[...] += val
            return carry

        lax.fori_loop(0, tm // sub, step, 0)

    in_specs = [pl.BlockSpec((tm, w), functools.partial(lambda i, cb: (i, cb), cb=cb)) for _, w, cb in rows]
    in_specs += [pl.BlockSpec(p.shape, lambda i: (0, 0)) for p in params]
    out_specs = [pl.BlockSpec((tm, w), lambda i: (i, 0)) for w, _ in outs]
    out_specs += [pl.BlockSpec(s, lambda i: (0, 0)) for s in accs]
    out_shape = [jax.ShapeDtypeStruct((t, w), d) for w, d in outs]
    out_shape += [jax.ShapeDtypeStruct(s, F32) for s in accs]
    return pl.pallas_call(
        body, name=name, grid=(t // tm,), in_specs=in_specs, out_specs=out_specs, out_shape=out_shape,
        compiler_params=_params(("arbitrary",)),
    )(*[r[0] for r in rows], *params)


def _vjp_fn(fn, n_in, n_out):
    def bwd(*args):
        ins, cts = args[:n_in], args[n_in:]
        _, pull = jax.vjp(fn, *ins)
        return pull(tuple(cts) if n_out > 1 else cts[0])
    return bwd


def _lane(shape):
    return lax.broadcasted_iota(jnp.int32, shape, 1)


def _silu(x):
    return x * jax.nn.sigmoid(x)


def _softplus(x):
    return jnp.maximum(x, 0.0) + jnp.log1p(jnp.exp(-jnp.abs(x)))


def _heads(x, width=HEAD_DIM):
    return [x[:, h * width:(h + 1) * width] for h in range(N_HEADS)]


def _layer_norm(z, g, b):
    mu = jnp.mean(z, -1, keepdims=True)
    zc = z - mu
    var = jnp.mean(zc * zc, -1, keepdims=True)
    return zc * lax.rsqrt(var + LN_EPS) * g + b


def _gdn_act(u, misc, alog_row, dtb_row):
    s = _silu(u)
    w = N_HEADS * HEAD_DIM
    q = jnp.concatenate([t * lax.rsqrt(jnp.sum(t * t, -1, keepdims=True) + RMS_EPS) * HEAD_DIM ** -0.5
                         for t in _heads(s[:, :w])], axis=1)
    k = jnp.concatenate([t * lax.rsqrt(jnp.sum(t * t, -1, keepdims=True) + RMS_EPS)
                         for t in _heads(s[:, w:2 * w])], axis=1)
    v = s[:, 2 * w:]
    lane = _lane(misc.shape)
    beta = jax.nn.sigmoid(misc)
    g = -jnp.exp(alog_row) * _softplus(misc + dtb_row)
    is_beta = (lane >= MISC_BETA0) & (lane < MISC_BETA0 + N_HEADS)
    is_g = (lane >= MISC_A0) & (lane < MISC_A0 + N_HEADS)
    gb = jnp.where(is_beta, beta, jnp.where(is_g, g, 0.0))
    return q, k, v, gb


def _gdn_out(o, z, gn_row):
    outs = []
    for oh, zh in zip(_heads(o), _heads(z)):
        r = oh * lax.rsqrt(jnp.mean(oh * oh, -1, keepdims=True) + RMS_EPS) * gn_row
        outs.append(r * _silu(zh))
    return jnp.concatenate(outs, axis=1)


def _mla_norm(ckv, cq, kvg_row, qg_row):
    cqn = cq * lax.rsqrt(jnp.mean(cq * cq, -1, keepdims=True) + RMS_EPS) * qg_row
    ckvn = ckv * lax.rsqrt(jnp.mean(ckv * ckv, -1, keepdims=True) + RMS_EPS) * kvg_row
    return cqn, ckvn


def _swap_halves(x):
    half = ROPE_DIM // 2
    return jnp.where(_lane(x.shape) < half, pltpu.roll(x, LANES - half, 1), pltpu.roll(x, half, 1))


@jax.custom_vjp
def _rope(x, cos_t, sin_t):
    return x * cos_t + _swap_halves(x) * sin_t


def _rope_fwd(x, cos_t, sin_t):
    return _rope(x, cos_t, sin_t), (cos_t, sin_t)


def _rope_bwd(res, g):
    cos_t, sin_t = res
    return g * cos_t - _swap_halves(g) * sin_t, jnp.zeros_like(cos_t), jnp.zeros_like(sin_t)


_rope.defvjp(_rope_fwd, _rope_bwd)


def _mla_qk(scale, qm, kv, misc, cos_t, sin_t):
    krope = _rope(misc, cos_t, sin_t)
    qs, ks = [], []
    for h in range(N_HEADS):
        base = 2 * HEAD_DIM * h
        qs += [qm[:, base:base + HEAD_DIM], _rope(qm[:, base + HEAD_DIM:base + 2 * HEAD_DIM], cos_t, sin_t)]
        ks += [kv[:, HEAD_DIM * h:HEAD_DIM * (h + 1)], krope]
    return jnp.concatenate(qs, axis=1) * scale, jnp.concatenate(ks, axis=1), kv[:, N_HEADS * HEAD_DIM:]


def _swiglu(gu):
    f = gu.shape[1] // 2
    return _silu(gu[:, :f]) * gu[:, f:]


def _ple_out(x2, pg, pe):
    return x2 + jax.nn.sigmoid(pg) * pe


CONV_W = 4
HALO = 8
CONV_STRIP = 512


def _conv_fwd(h, conv_w, width, *, name, tm=ROW_TILE, sub=32):
    t = h.shape[0]
    tm = min(tm, t)
    nb = tm // HALO

    def body(x_ref, halo_ref, w_ref, u_ref, buf):
        i = pl.program_id(0)
        buf[pl.ds(0, HALO), :] = jnp.where(i > 0, halo_ref[...], 0.0)
        buf[pl.ds(HALO, tm), :] = x_ref[...]
        for c0 in range(0, width, CONV_STRIP):
            cols = pl.ds(c0, CONV_STRIP)
            w = w_ref[:, cols]
            for r0 in range(0, tm, sub):
                acc = jnp.zeros((sub, CONV_STRIP), F32)
                for j in range(CONV_W):
                    acc = acc + w[j:j + 1, :] * buf[pl.ds(HALO + r0 - (CONV_W - 1) + j, sub), cols]
                u_ref[pl.ds(r0, sub), cols] = acc

    return pl.pallas_call(
        body, name=name, grid=(t // tm,),
        in_specs=[pl.BlockSpec((tm, width), lambda i: (i, 0)),
                  pl.BlockSpec((HALO, width), lambda i: (jnp.maximum(i * nb - 1, 0), 0)),
                  pl.BlockSpec(conv_w.shape, lambda i: (0, 0))],
        out_specs=pl.BlockSpec((tm, width), lambda i: (i, 0)),
        out_shape=jax.ShapeDtypeStruct((t, width), F32),
        scratch_shapes=[pltpu.VMEM((tm + HALO, width), F32)],
        compiler_params=_params(("arbitrary",)),
    )(h, h, conv_w)


def _conv_bwd(du, h, conv_w, width, *, name, tm=ROW_TILE, sub=32):
    t = h.shape[0]
    tm = min(tm, t)
    nb = tm // HALO
    n_tiles = t // tm

    def body(du_ref, du_halo, x_ref, x_halo, w_ref, dx_ref, dw_ref, dbuf, xbuf):
        i = pl.program_id(0)

        @pl.when(i == 0)
        def _():
            dw_ref[...] = jnp.zeros_like(dw_ref)

        dbuf[pl.ds(0, tm), :] = du_ref[...]
        dbuf[pl.ds(tm, HALO), :] = jnp.where(i < n_tiles - 1, du_halo[...], 0.0)
        xbuf[pl.ds(0, HALO), :] = jnp.where(i > 0, x_halo[...], 0.0)
        xbuf[pl.ds(HALO, tm), :] = x_ref[...]
        for c0 in range(0, width, CONV_STRIP):
            cols = pl.ds(c0, CONV_STRIP)
            w = w_ref[:, cols]
            dws = [jnp.zeros((HALO, CONV_STRIP), F32) for _ in range(CONV_W)]
            for r0 in range(0, tm, sub):
                acc = jnp.zeros((sub, CONV_STRIP), F32)
                d_here = dbuf[pl.ds(r0, sub), cols]
                for j in range(CONV_W):
                    acc = acc + w[j:j + 1, :] * dbuf[pl.ds(r0 + (CONV_W - 1) - j, sub), cols]
                    prod = d_here * xbuf[pl.ds(HALO + r0 - (CONV_W - 1) + j, sub), cols]
                    for g0 in range(0, sub, HALO):
                        dws[j] = dws[j] + prod[g0:g0 + HALO, :]
                dx_ref[pl.ds(r0, sub), cols] = acc.astype(dx_ref.dtype)
            for j in range(CONV_W):
                dw_ref[pl.ds(j, 1), cols] += jnp.sum(dws[j], axis=0, keepdims=True)

    return pl.pallas_call(
        body, name=name, grid=(n_tiles,),
        in_specs=[pl.BlockSpec((tm, width), lambda i: (i, 0)),
                  pl.BlockSpec((HALO, width), lambda i: (jnp.minimum((i + 1) * nb, t // HALO - 1), 0)),
                  pl.BlockSpec((tm, width), lambda i: (i, 0)),
                  pl.BlockSpec((HALO, width), lambda i: (jnp.maximum(i * nb - 1, 0), 0)),
                  pl.BlockSpec(conv_w.shape, lambda i: (0, 0))],
        out_specs=[pl.BlockSpec((tm, width), lambda i: (i, 0)),
                   pl.BlockSpec((HALO, width), lambda i: (0, 0))],
        out_shape=[jax.ShapeDtypeStruct((t, width), BF16), jax.ShapeDtypeStruct((HALO, width), F32)],
        scratch_shapes=[pltpu.VMEM((tm + HALO, width), F32), pltpu.VMEM((tm + HALO, width), F32)],
        compiler_params=_params(("arbitrary",)),
    )(du, du, h, h, conv_w)


@jax.custom_vjp
def _inv_unit_lower(low):
    n = low.shape[-1]
    eye = (lax.broadcasted_iota(jnp.int32, (n, n), 0) == lax.broadcasted_iota(jnp.int32, (n, n), 1)).astype(F32)
    x = eye - low
    p = low
    span = 2
    while span < n:
        p = _nn(p, p)
        x = x + _nn(x, p)
        span *= 2
    return x


def _inv_fwd(low):
    x = _inv_unit_lower(low)
    return x, x


def _inv_bwd(x, g):
    return (-_tn(x, _nt(g, x)),)


_inv_unit_lower.defvjp(_inv_fwd, _inv_bwd)


@jax.custom_vjp
def _inv_known(low, inverse):
    return inverse


_inv_known.defvjp(lambda low, inverse: (inverse, inverse), lambda x, g: (_inv_bwd(x, g)[0], jnp.zeros_like(x)))


def _gdn_prep(q, k, v, gb, known_inverse=None):
    c = CHUNK
    n = q.shape[0] // c
    pairs = [(g, h) for g in range(n) for h in range(N_HEADS)]
    row = lax.broadcasted_iota(jnp.int32, (c, c), 0)
    col = lax.broadcasted_iota(jnp.int32, (c, c), 1)
    tri_incl = row >= col
    tri_strict = row > col
    lane = _lane((c, LANES))
    sub = lax.broadcasted_iota(jnp.int32, (LANES, c), 0)
    last = lax.broadcasted_iota(jnp.int32, (c, 1), 0) == c - 1

    def split(x):
        return jnp.stack([x[g * c:(g + 1) * c, h * HEAD_DIM:(h + 1) * HEAD_DIM] for g, h in pairs])

    gbs = [gb[g * c:(g + 1) * c, :] for g in range(n)]
    gbts = [x.T for x in gbs]
    g_col = jnp.stack([jnp.sum(jnp.where(lane == MISC_A0 + h, gbs[g], 0.0), axis=1, keepdims=True) for g, h in pairs])
    b_col = jnp.stack([jnp.sum(jnp.where(lane == MISC_BETA0 + h, gbs[g], 0.0), axis=1, keepdims=True) for g, h in pairs])
    g_row = jnp.stack([jnp.sum(jnp.where(sub == MISC_A0 + h, gbts[g], 0.0), axis=0, keepdims=True) for g, h in pairs])
    gc_col = jnp.sum(jnp.where(tri_incl, g_row, 0.0), axis=2, keepdims=True)
    gc_row = jnp.sum(jnp.where(row <= col, g_col, 0.0), axis=1, keepdims=True)
    decay = jnp.where(tri_incl, jnp.exp(jnp.where(tri_incl, gc_col - gc_row, 0.0)), 0.0)
    g_last = jnp.sum(jnp.where(last, gc_col, 0.0), axis=1, keepdims=True)
    qs, ks, vs = split(q), split(k), split(v)
    kb = ks * b_col
    low = jnp.where(tri_strict, _nt(kb, ks) * decay, 0.0)
    if known_inverse is None:
        tinv = _inv_unit_lower(low)
    else:
        tinv = _inv_known(low, jnp.stack([known_inverse[g * c:(g + 1) * c, h * c:(h + 1) * c] for g, h in pairs]))
    eg = jnp.exp(gc_col)
    sol = _nn(tinv, jnp.concatenate([vs * b_col, kb * eg], axis=2))
    attn = jnp.where(tri_incl, _nt(qs, ks) * decay, 0.0)
    qd = qs * eg
    kd = ks * jnp.exp(g_last - gc_col)

    def merge(x):
        return jnp.concatenate([jnp.concatenate([x[g * N_HEADS + h] for h in range(N_HEADS)], axis=1)
                                for g in range(n)], axis=0)

    glb = jnp.concatenate([sum(jnp.where(lane == h, g_last[g * N_HEADS + h], 0.0) for h in range(N_HEADS))
                           for g in range(n)], axis=0)
    outs = (merge(sol[:, :, :HEAD_DIM]), merge(sol[:, :, HEAD_DIM:]), merge(qd), merge(kd), merge(attn), glb)
    return outs, merge(tinv)


def _gdn_seq(state, u, w, qd, kd, attn, glb):
    c = u.shape[0]
    first = lax.broadcasted_iota(jnp.int32, glb.shape, 0) == 0
    lane = _lane(glb.shape)
    heads = lambda x: jnp.stack([x[:, h * HEAD_DIM:(h + 1) * HEAD_DIM] for h in range(N_HEADS)])
    g_last = jnp.stack([jnp.sum(jnp.sum(jnp.where(first & (lane == h), glb, 0.0), axis=1, keepdims=True),
                                axis=0, keepdims=True) for h in range(N_HEADS)])
    s = jnp.stack([state[h * HEAD_DIM:(h + 1) * HEAD_DIM, :] for h in range(N_HEADS)])
    at = jnp.stack([attn[:, h * c:(h + 1) * c] for h in range(N_HEADS)])
    v_new = heads(u) - _nn(heads(w), s)
    o = _nn(heads(qd), s) + _nn(at, v_new)
    s_new = s * jnp.exp(g_last) + _tn(heads(kd), v_new)
    return (jnp.concatenate([o[h] for h in range(N_HEADS)], axis=1),
            jnp.concatenate([s_new[h] for h in range(N_HEADS)], axis=0))


PREP_CHUNKS = 8
PREP_CHUNKS_BWD = 4
SEQ_CHUNKS = 8


def _gdn_prep_fwd(q, k, v, gb, *, name):
    t, w = q.shape
    rows = min(PREP_CHUNKS * CHUNK, t)

    def body(q_ref, k_ref, v_ref, gb_ref, *out_refs):
        outs, inverse = _gdn_prep(q_ref[...], k_ref[...], v_ref[...], gb_ref[...])
        for o_ref, val in zip(out_refs, outs + (inverse,)):
            o_ref[...] = val

    spec = lambda width: pl.BlockSpec((rows, width), lambda i: (i, 0))
    widths = [w, w, w, w, N_HEADS * CHUNK, LANES, N_HEADS * CHUNK]
    res = pl.pallas_call(
        body, name=name, grid=(t // rows,),
        in_specs=[spec(w), spec(w), spec(w), spec(LANES)],
        out_specs=[spec(x) for x in widths],
        out_shape=[jax.ShapeDtypeStruct((t, x), F32) for x in widths],
        compiler_params=_params(("arbitrary",)),
    )(q, k, v, gb)
    return tuple(res[:6]), res[6]


def _gdn_prep_bwd(q, k, v, gb, inverse, cts, *, name):
    t, w = q.shape
    rows = min(PREP_CHUNKS_BWD * CHUNK, t)

    def body(q_ref, k_ref, v_ref, gb_ref, inv_ref, du, dw, dqd, dkd, dattn, dglb, dq_ref, dk_ref, dv_ref, dgb_ref):
        known = inv_ref[...]
        _, pull = jax.vjp(lambda a, b, c_, d_: _gdn_prep(a, b, c_, d_, known)[0],
                          q_ref[...], k_ref[...], v_ref[...], gb_ref[...])
        dq, dk, dv, dgb = pull(tuple(r[...] for r in (du, dw, dqd, dkd, dattn, dglb)))
        dq_ref[...] = dq
        dk_ref[...] = dk
        dv_ref[...] = dv
        dgb_ref[...] = dgb

    spec = lambda width: pl.BlockSpec((rows, width), lambda i: (i, 0))
    widths = [w, w, w, w, N_HEADS * CHUNK, LANES]
    return pl.pallas_call(
        body, name=name, grid=(t // rows,),
        in_specs=[spec(w), spec(w), spec(w), spec(LANES), spec(N_HEADS * CHUNK)] + [spec(x) for x in widths],
        out_specs=[spec(w), spec(w), spec(w), spec(LANES)],
        out_shape=[jax.ShapeDtypeStruct((t, w), F32)] * 3 + [jax.ShapeDtypeStruct((t, LANES), F32)],
        compiler_params=_params(("arbitrary",)),
    )(q, k, v, gb, inverse, *cts)


def _gdn_seq_fwd(prep, *, name):
    t, w = prep[0].shape
    rows = min(SEQ_CHUNKS * CHUNK, t)
    per = rows // CHUNK

    def body(u_ref, w_ref, qd_ref, kd_ref, at_ref, gl_ref, o_ref, sall_ref, s_scr):
        @pl.when(pl.program_id(0) == 0)
        def _():
            s_scr[...] = jnp.zeros_like(s_scr)

        def step(j, carry):
            sl = pl.ds(pl.multiple_of(j * CHUNK, CHUNK), CHUNK)
            s = s_scr[...]
            sall_ref[j] = s
            o, s_new = _gdn_seq(s, u_ref[sl, :], w_ref[sl, :], qd_ref[sl, :], kd_ref[sl, :], at_ref[sl, :], gl_ref[sl, :])
            o_ref[sl, :] = o
            s_scr[...] = s_new
            return carry

        lax.fori_loop(0, per, step, 0)

    spec = lambda width: pl.BlockSpec((rows, width), lambda i: (i, 0))
    widths = [w, w, w, w, N_HEADS * CHUNK, LANES]
    return pl.pallas_call(
        body, name=name, grid=(t // rows,),
        in_specs=[spec(x) for x in widths],
        out_specs=[spec(w), pl.BlockSpec((per, w, HEAD_DIM), lambda i: (i, 0, 0))],
        out_shape=[jax.ShapeDtypeStruct((t, w), F32), jax.ShapeDtypeStruct((t // CHUNK, w, HEAD_DIM), F32)],
        scratch_shapes=[pltpu.VMEM((w, HEAD_DIM), F32)],
        compiler_params=_params(("arbitrary",)),
    )(*prep)


def _gdn_seq_bwd(prep, s_all, do, *, name):
    t, w = prep[0].shape
    rows = min(SEQ_CHUNKS * CHUNK, t)
    per = rows // CHUNK
    n = t // rows

    def body(u_ref, w_ref, qd_ref, kd_ref, at_ref, gl_ref, sall_ref, do_ref, du, dw, dqd, dkd, dat, dgl, ds_scr):
        @pl.when(pl.program_id(0) == 0)
        def _():
            ds_scr[...] = jnp.zeros_like(ds_scr)

        def step(jj, carry):
            j = per - 1 - jj
            sl = pl.ds(pl.multiple_of(j * CHUNK, CHUNK), CHUNK)
            _, pull = jax.vjp(_gdn_seq, sall_ref[j], u_ref[sl, :], w_ref[sl, :], qd_ref[sl, :], kd_ref[sl, :],
                              at_ref[sl, :], gl_ref[sl, :])
            res = pull((do_ref[sl, :], ds_scr[...]))
            ds_scr[...] = res[0]
            for o_ref, val in zip((du, dw, dqd, dkd, dat, dgl), res[1:]):
                o_ref[sl, :] = val
            return carry

        lax.fori_loop(0, per, step, 0)

    spec = lambda width: pl.BlockSpec((rows, width), lambda i: (n - 1 - i, 0))
    widths = [w, w, w, w, N_HEADS * CHUNK, LANES]
    return pl.pallas_call(
        body, name=name, grid=(n,),
        in_specs=[spec(x) for x in widths] + [pl.BlockSpec((per, w, HEAD_DIM), lambda i: (n - 1 - i, 0, 0)), spec(w)],
        out_specs=[spec(x) for x in widths],
        out_shape=[jax.ShapeDtypeStruct((t, x), F32) for x in widths],
        scratch_shapes=[pltpu.VMEM((w, HEAD_DIM), F32)],
        compiler_params=_params(("arbitrary",)),
    )(*prep, s_all, do)


QK_DIM = 2 * HEAD_DIM
ATT_TILE = 1024
NEG = -1e30


ATT_SPLIT = 4


def _chunk_mask(n_rows, n_cols, key_major, query_offset):
    r = lax.broadcasted_iota(jnp.int32, (n_rows, n_cols), 0)
    c = lax.broadcasted_iota(jnp.int32, (n_rows, n_cols), 1)
    if key_major:
        return r // CHUNK <= (c + query_offset) // CHUNK
    return c // CHUNK <= (r + query_offset) // CHUNK


def _visible_keys(tile, diagonal):
    hq = tile // ATT_SPLIT
    return [(a + 1) * hq if diagonal else tile for a in range(ATT_SPLIT)]


def _dot_nn(a, b):
    return lax.dot_general(a, b, NN, preferred_element_type=F32)


def _blocked_transpose(x, width):
    t = x.shape[0]
    tile = min(ATT_TILE, t)
    return x.reshape(t // tile, tile, N_HEADS * width).transpose(0, 2, 1).reshape(t // tile, N_HEADS, width, tile)


def _attn_fwd(q, kt, v1, *, name):
    t = q.shape[0]
    tq = min(ATT_TILE, t)
    nq = t // tq

    def body(q_ref, kt_ref, v_ref, o_ref, lse_ref, m_scr, acc_scr):
        qi = pl.program_id(1)
        m_scr[...] = jnp.full_like(m_scr, NEG)
        acc_scr[...] = jnp.zeros_like(acc_scr)
        hq = tq // ATT_SPLIT
        parts = [pl.ds(a * hq, hq) for a in range(ATT_SPLIT)]
        qs = [q_ref[sl, :] for sl in parts]

        def step(kj, masked):
            rows = pl.ds(pl.multiple_of(kj * tq, tq), tq)
            kt_blk, vv = kt_ref[kj], v_ref[rows, :]
            seen = _visible_keys(tq, masked)
            ss = [_dot_nn(qv, kt_blk[:, :w]) for qv, w in zip(qs, seen)]
            for a, sl in enumerate(parts):
                s = ss[a]
                if masked:
                    s = jnp.where(_chunk_mask(hq, seen[a], False, a * hq), s, NEG)
                m_old = m_scr[sl, :]
                m_new = jnp.maximum(m_old, jnp.max(s, axis=1, keepdims=True))
                p = jnp.exp(s - m_new)
                acc_scr[sl, :] = jnp.exp(m_old - m_new) * acc_scr[sl, :] + _dot_nn(p.astype(BF16), vv[:seen[a], :])
                m_scr[sl, :] = m_new

        def loop_body(kj, carry):
            step(kj, False)
            return carry

        lax.fori_loop(0, qi, loop_body, 0)
        step(qi, True)
        acc = acc_scr[...]
        o_ref[...] = (acc[:, :HEAD_DIM] / acc[:, HEAD_DIM:]).astype(o_ref.dtype)
        lse_ref[...] = m_scr[...] + jnp.log(acc[:, HEAD_DIM:HEAD_DIM + 1])

    return pl.pallas_call(
        body, name=name, grid=(N_HEADS, nq),
        in_specs=[pl.BlockSpec((tq, QK_DIM), lambda h, i: (i, h)),
                  pl.BlockSpec((nq, None, QK_DIM, tq), lambda h, i: (0, h, 0, 0)),
                  pl.BlockSpec((t, 2 * HEAD_DIM), lambda h, i: (0, h))],
        out_specs=[pl.BlockSpec((tq, HEAD_DIM), lambda h, i: (i, h)),
                   pl.BlockSpec((None, tq, 1), lambda h, i: (h, i, 0))],
        out_shape=[jax.ShapeDtypeStruct((t, N_HEADS * HEAD_DIM), BF16),
                   jax.ShapeDtypeStruct((N_HEADS, t, 1), F32)],
        scratch_shapes=[pltpu.VMEM((tq, 1), F32), pltpu.VMEM((tq, 2 * HEAD_DIM), F32)],
        compiler_params=_params(("arbitrary", "arbitrary")),
    )(q, kt, v1)


def _attn_delta(dom, o, *, name):
    hw = o.shape[1]

    def fn(do, ov):
        lane = _lane((do.shape[0], LANES))
        out = jnp.zeros((do.shape[0], LANES), F32)
        for h, (a, b) in enumerate(zip(_heads(do), _heads(ov))):
            out = out + jnp.where(lane == h, jnp.sum(a * b, axis=1, keepdims=True), 0.0)
        return (out,)

    return _rowwise(fn, [(dom, hw, 1), (o, hw, 0)], [], [(LANES, F32)], name=name)[0]


def _attn_bwd(q, qt, k, v, lse_row, delta_row, do, dot, *, name):
    t = q.shape[0]
    tk = min(ATT_TILE, t)
    nk = t // tk

    def body(q_ref, qt_ref, k_ref, v_ref, lse_ref, delta_ref, do_ref, dot_ref, dk_ref, dv_ref, dq_ref, dk_scr, dv_scr):
        kj = pl.program_id(1)

        @pl.when(kj == 0)
        def _():
            dq_ref[...] = jnp.zeros_like(dq_ref)

        dk_scr[...] = jnp.zeros_like(dk_scr)
        dv_scr[...] = jnp.zeros_like(dv_scr)
        kv_ = k_ref[...]
        vv = v_ref[...]
        hq = tk // ATT_SPLIT

        def step(qi, masked):
            lse_v, delta_v = lse_ref[qi], delta_ref[qi]
            qt_blk, dot_blk = qt_ref[qi], dot_ref[qi]
            rows = [pl.ds(pl.multiple_of(qi * tk + a * hq, hq), hq) for a in range(ATT_SPLIT)]
            qs = [q_ref[r, :] for r in rows]
            dos = [do_ref[r, :] for r in rows]
            seen = _visible_keys(tk, masked)
            ss = [_dot_nn(kv_[:seen[a], :], qt_blk[:, a * hq:(a + 1) * hq]) for a in range(ATT_SPLIT)]
            dps = [_dot_nn(vv[:seen[a], :], dot_blk[:, a * hq:(a + 1) * hq]) for a in range(ATT_SPLIT)]
            for a in range(ATT_SPLIT):
                cols = slice(a * hq, (a + 1) * hq)
                keys = pl.ds(0, seen[a])
                p = jnp.exp(ss[a] - lse_v[:, cols])
                if masked:
                    p = jnp.where(_chunk_mask(seen[a], hq, True, a * hq), p, 0.0)
                dv_scr[keys, :] += _dot_nn(p.astype(BF16), dos[a])
                ds = (p * (dps[a] - delta_v[:, cols])).astype(BF16)
                dk_scr[keys, :] += _dot_nn(ds, qs[a])
                dq_ref[rows[a], :] += lax.dot_general(ds, kv_[:seen[a], :], TN, preferred_element_type=F32)

        step(kj, True)

        def loop_body(qi, carry):
            step(qi, False)
            return carry

        lax.fori_loop(kj + 1, nk, loop_body, 0)
        dk_ref[...] = dk_scr[...].astype(dk_ref.dtype)
        dv_ref[...] = dv_scr[...].astype(dv_ref.dtype)

    once = dict(pipeline_mode=pl.Buffered(1))
    stat = pl.BlockSpec((None, nk, 1, tk), lambda h, j: (h, 0, 0, 0))
    return pl.pallas_call(
        body, name=name, grid=(N_HEADS, nk),
        in_specs=[pl.BlockSpec((t, QK_DIM), lambda h, j: (0, h), **once),
                  pl.BlockSpec((nk, None, QK_DIM, tk), lambda h, j: (0, h, 0, 0), **once),
                  pl.BlockSpec((tk, QK_DIM), lambda h, j: (j, h)),
                  pl.BlockSpec((tk, HEAD_DIM), lambda h, j: (j, h)),
                  stat, stat,
                  pl.BlockSpec((t, HEAD_DIM), lambda h, j: (0, h), **once),
                  pl.BlockSpec((nk, None, HEAD_DIM, tk), lambda h, j: (0, h, 0, 0), **once)],
        out_specs=[pl.BlockSpec((tk, QK_DIM), lambda h, j: (j, h)),
                   pl.BlockSpec((tk, HEAD_DIM), lambda h, j: (j, h)),
                   pl.BlockSpec((t, QK_DIM), lambda h, j: (0, h))],
        out_shape=[jax.ShapeDtypeStruct((t, N_HEADS * QK_DIM), BF16),
                   jax.ShapeDtypeStruct((t, N_HEADS * HEAD_DIM), BF16),
                   jax.ShapeDtypeStruct((t, N_HEADS * QK_DIM), F32)],
        scratch_shapes=[pltpu.VMEM((tk, QK_DIM), F32), pltpu.VMEM((tk, HEAD_DIM), F32)],
        compiler_params=_params(("arbitrary", "arbitrary")),
    )(q, qt, k, v, lse_row, delta_row, do, dot)


def _rope_tables(pos_col, inv_freq_row, *, name):
    t = pos_col.shape[0]
    tm = min(ROW_TILE, t)

    def body(p_ref, f_ref, c_ref, s_ref):
        ang = p_ref[...].astype(F32) * f_ref[...]
        lane = _lane(ang.shape)
        c_ref[...] = jnp.where(lane < ROPE_DIM, jnp.cos(ang), 0.0)
        sn = jnp.sin(ang)
        s_ref[...] = jnp.where(lane < ROPE_DIM // 2, -sn, jnp.where(lane < ROPE_DIM, sn, 0.0))

    out = pl.BlockSpec((tm, LANES), lambda i: (i, 0))
    return pl.pallas_call(
        body, name=name, grid=(t // tm,),
        in_specs=[pl.BlockSpec((tm, 1), lambda i: (i, 0)), pl.BlockSpec((1, LANES), lambda i: (0, 0))],
        out_specs=[out, out], out_shape=[jax.ShapeDtypeStruct((t, LANES), F32)] * 2,
        compiler_params=_params(("arbitrary",)),
    )(pos_col, inv_freq_row)


def _loss_head(y, target):
    width = y.shape[1]

    def fn(yv, tv):
        e = yv - tv
        part = 0.5 * jnp.sum(jnp.mean(e * e, axis=1, keepdims=True), axis=0, keepdims=True)
        return e * (1.0 / width), jnp.broadcast_to(part, (HALO, LANES))

    return _rowwise(fn, [(y, width, 0), (target, width, 0)], [], [(width, F32)], [(HALO, LANES)], name="loss_head")


def _adamw(w, g, m, v, *, name):
    shape = w.shape
    w2, g2, m2, v2 = (a.reshape(-1, shape[-1]) for a in (w, g, m, v))
    rows, width = w2.shape
    tr = _divisor_tile(rows, max(8, (1 << 19) // max(width, 1)), 8)
    bc1 = 1.0 - ADAM_B1 ** ADAM_STEP
    bc2 = 1.0 - ADAM_B2 ** ADAM_STEP

    def body(w_ref, g_ref, m_ref, v_ref, d_ref, mo_ref, vo_ref):
        gv = g_ref[...]
        mn = ADAM_B1 * m_ref[...] + (1.0 - ADAM_B1) * gv
        vn = ADAM_B2 * v_ref[...] + (1.0 - ADAM_B2) * (gv * gv)
        d_ref[...] = -ADAM_LR * ((mn / bc1) / (jnp.sqrt(vn / bc2) + ADAM_EPS) + ADAM_WD * w_ref[...])
        mo_ref[...] = mn
        vo_ref[...] = vn

    spec = pl.BlockSpec((tr, width), lambda i: (i, 0))
    outs = pl.pallas_call(
        body, name=name, grid=(rows // tr,), in_specs=[spec] * 4, out_specs=[spec] * 3,
        out_shape=[jax.ShapeDtypeStruct((rows, width), F32)] * 3,
        compiler_params=_params(("arbitrary",)),
    )(w2, g2, m2, v2)
    return tuple(o.reshape(shape) for o in outs)


HBM_SPEC = pl.BlockSpec(memory_space=pltpu.HBM)


def _position():
    return lax.axis_index("x"), lax.axis_index("y"), lax.axis_index("c")


def _other_chips(x, y):
    return [(1 - x, y), (x, 1 - y), (1 - x, 1 - y)]


class _Stream:
    def __init__(self, kind, size=0):
        self.kind, self.size = kind, size
        self.parts = 2 if kind == "heads" else 1

    def local(self, ref, k, part):
        if self.kind == "rows":
            return ref.at[:, pl.ds(k * self.size, self.size), :]
        if self.kind == "cols":
            return ref.at[:, :, pl.ds(k * self.size, self.size)]
        if self.kind == "heads":
            return ref.at[:, :, pl.ds(part * N_HEADS * HEAD_DIM + k * HEAD_DIM, HEAD_DIM)]
        if self.kind == "piece":
            return ref.at[k]
        return ref

    def shard(self, ref, part):
        if self.kind == "heads":
            return ref.at[:, :, pl.ds(part * HEAD_DIM, HEAD_DIM)]
        return ref

    def half_local(self, ref, k, part, cc, hd):
        if self.kind == "piece":
            return ref.at[k, pl.ds(cc * hd, hd)]
        return self.local(ref.at[pl.ds(cc * hd, hd)], k, part)


def _remote(src, dst, send_sems, recv_sems, idx, to):
    return pltpu.make_async_remote_copy(src_ref=src, dst_ref=dst, send_sem=send_sems.at[idx],
                                        recv_sem=recv_sems.at[idx], device_id=to, device_id_type=MESH)


def _comm_call(body, ins, out_shapes, n_remote, n_local, *, name):
    scratch = [pltpu.SemaphoreType.DMA((n_remote,)), pltpu.SemaphoreType.DMA((n_remote,))]
    if n_local:
        scratch.append(pltpu.SemaphoreType.DMA((n_local,)))
    return pl.pallas_call(
        body, name=name, in_specs=[HBM_SPEC] * len(ins), out_specs=[HBM_SPEC] * len(out_shapes), out_shape=out_shapes,
        scratch_shapes=scratch, compiler_params=pltpu.CompilerParams(has_side_effects=True),
    )(*ins)


def _gather_chips(shards, streams, out_shapes, *, name):
    n = len(shards)
    hd = shards[0].shape[0] // 2
    flat = [(t, part) for t in range(n) for part in range(streams[t].parts)]
    ns = len(flat)

    def body(*refs):
        s_refs, o_refs = refs[:n], refs[n:2 * n]
        send_sems, recv_sems = refs[2 * n:]
        x, y, c = _position()
        sibling = (x, y, 1 - c)
        chips = _other_chips(x, y)
        me = 2 * x + y
        sent = []
        for s, (t, part) in enumerate(flat):
            st = streams[t]
            sent.append(_remote(st.shard(s_refs[t], part), st.local(o_refs[t], me, part), send_sems, recv_sems,
                                6 * ns + s, sibling))
            sent[-1].start()
            src = st.shard(s_refs[t].at[pl.ds(c * hd, hd)], part)
            for j, (cx, cy) in enumerate(chips):
                sent.append(_remote(src, st.half_local(o_refs[t], me, part, c, hd), send_sems, recv_sems,
                                    3 * s + j, (cx, cy, c)))
                sent[-1].start()
        for s, (t, part) in enumerate(flat):
            st = streams[t]
            for j, (cx, cy) in enumerate(chips):
                blk = st.half_local(o_refs[t], 2 * cx + cy, part, c, hd)
                _remote(blk, blk, send_sems, recv_sems, 3 * s + j, (x, y, c)).wait_recv()
                sent.append(_remote(blk, blk, send_sems, recv_sems, 3 * ns + 3 * s + j, sibling))
                sent[-1].start()
        for s, (t, part) in enumerate(flat):
            st = streams[t]
            for j, (cx, cy) in enumerate(chips):
                blk = st.half_local(o_refs[t], 2 * cx + cy, part, 1 - c, hd)
                _remote(blk, blk, send_sems, recv_sems, 3 * ns + 3 * s + j, (x, y, c)).wait_recv()
            own = st.local(o_refs[t], me, part)
            _remote(own, own, send_sems, recv_sems, 6 * ns + s, (x, y, c)).wait_recv()
        for cp in sent:
            cp.wait_send()

    return _comm_call(body, shards, out_shapes, 7 * ns, 0, name=name)


def _sibling_take_other_half(gs, *, name):
    n = len(gs)
    hd = gs[0].shape[0] // 2

    def body(*refs):
        g_refs, o_refs = refs[:n], refs[n:2 * n]
        send_sems, recv_sems = refs[2 * n:]
        x, y, c = _position()
        copies = [_remote(g_refs[t].at[pl.ds((1 - c) * hd, hd)], o_refs[t], send_sems, recv_sems, t, (x, y, 1 - c))
                  for t in range(n)]
        for cp in copies:
            cp.start()
        for cp in copies:
            cp.wait()

    outs = [jax.ShapeDtypeStruct((hd,) + g.shape[1:], g.dtype) for g in gs]
    return _comm_call(body, gs, outs, n, 0, name=name)


def _chips_exchange(ps, streams, shard_shapes, *, name):
    n = len(ps)
    flat = [(t, part) for t in range(n) for part in range(streams[t].parts)]

    def body(*refs):
        p_refs, o_refs = refs[:n], refs[n:2 * n]
        send_sems, recv_sems = refs[2 * n:]
        x, y, c = _position()
        copies = []
        for s, (t, part) in enumerate(flat):
            st = streams[t]
            for j, (cx, cy) in enumerate(_other_chips(x, y)):
                copies.append(_remote(st.local(p_refs[t], 2 * cx + cy, part), st.shard(o_refs[t].at[j], part),
                                      send_sems, recv_sems, 3 * s + j, (cx, cy, c)))
        for cp in copies:
            cp.start()
        for cp in copies:
            cp.wait()

    outs = [jax.ShapeDtypeStruct((3,) + tuple(shp), p.dtype) for p, shp in zip(ps, shard_shapes)]
    return _comm_call(body, ps, outs, 3 * len(flat), 0, name=name)


def _sibling_join_halves(bufs, *, name):
    n = len(bufs)
    hd = bufs[0].shape[0] // 2

    def body(*refs):
        o_refs = refs[n:2 * n]
        send_sems, recv_sems = refs[2 * n:]
        x, y, c = _position()
        sent = []
        for t in range(n):
            mine = o_refs[t].at[pl.ds(c * hd, hd)]
            sent.append(_remote(mine, mine, send_sems, recv_sems, t, (x, y, 1 - c)))
            sent[-1].start()
        for t in range(n):
            theirs = o_refs[t].at[pl.ds((1 - c) * hd, hd)]
            _remote(theirs, theirs, send_sems, recv_sems, t, (x, y, c)).wait_recv()
        for cp in sent:
            cp.wait_send()

    return pl.pallas_call(
        body, name=name, in_specs=[HBM_SPEC] * n, out_specs=[HBM_SPEC] * n,
        out_shape=[jax.ShapeDtypeStruct(b.shape, b.dtype) for b in bufs],
        scratch_shapes=[pltpu.SemaphoreType.DMA((n,)), pltpu.SemaphoreType.DMA((n,))],
        input_output_aliases={t: t for t in range(n)},
        compiler_params=pltpu.CompilerParams(has_side_effects=True),
    )(*bufs)


def _row_tile(rows, width):
    return _divisor_tile(rows, max(16, (1 << 19) // width), 16)


def _add_own_half(g, got, c_idx, out_dtype, *, name):
    hd, r, w = got.shape
    tr = _row_tile(r, w)

    def body(c_ref, g_ref, a_ref, o_ref):
        o_ref[...] = (g_ref[...] + a_ref[...]).astype(o_ref.dtype)

    return pl.pallas_call(
        body, name=name,
        grid_spec=pltpu.PrefetchScalarGridSpec(
            num_scalar_prefetch=1, grid=(hd, r // tr),
            in_specs=[pl.BlockSpec((None, None, tr, w), lambda l, i, c_ref: (c_ref[0], l, i, 0)),
                      pl.BlockSpec((None, tr, w), lambda l, i, c_ref: (l, i, 0))],
            out_specs=pl.BlockSpec((None, tr, w), lambda l, i, c_ref: (l, i, 0))),
        out_shape=jax.ShapeDtypeStruct((hd, r, w), out_dtype),
        compiler_params=_params(("arbitrary", "arbitrary")),
    )(c_idx, g.reshape((2, hd) + g.shape[1:]), got)


def _sum_chips(p, got, place, stream, *, name):
    _, hd, rs, cs = got.shape
    wb = HEAD_DIM if stream.kind == "heads" else cs
    tr = _row_tile(rs, wb)
    kind, size = stream.kind, stream.size

    def own_index(l, i, g, k_ref, c_ref):
        k = k_ref[0]
        if kind == "rows":
            return (l, k * (size // tr) + i, 0)
        if kind == "cols":
            return (l, i, k)
        if kind == "heads":
            return (l, i, g * N_HEADS + k)
        if kind == "piece":
            return (k, l, i, 0)
        return (l, i, 0)

    own_blk = (None, None, tr, wb) if kind == "piece" else (None, tr, wb)

    def body(k_ref, c_ref, p_ref, fx_ref, fy_ref, fxy_ref, o_ref):
        f = lambda r: r[...].astype(F32)
        o_ref[...] = (f(p_ref) + f(fy_ref)) + (f(fx_ref) + f(fxy_ref))

    def rel(j):
        return pl.BlockSpec((None, None, tr, wb), functools.partial(lambda l, i, g, k_ref, c_ref, j: (j, l, i, g), j=j))

    return pl.pallas_call(
        body, name=name,
        grid_spec=pltpu.PrefetchScalarGridSpec(
            num_scalar_prefetch=2, grid=(hd, rs // tr, stream.parts),
            in_specs=[pl.BlockSpec(own_blk, own_index), rel(0), rel(1), rel(2)],
            out_specs=pl.BlockSpec((None, tr, wb), lambda l, i, g, k_ref, c_ref: (c_ref[0] * hd + l, i, g))),
        out_shape=jax.ShapeDtypeStruct((2 * hd, rs, cs), F32),
        compiler_params=_params(("arbitrary", "arbitrary", "arbitrary")),
    )(place[0], place[1], p, got, got, got)


MATRICES = ("w_in", "w_uq", "w_ukv", "w_out", "w_gate_up", "w_down", "w_ple", "w_ple_gate", "conv_w")
VECTORS = ("a_log", "dt_bias", "gdn_norm_g", "q_norm_g", "kv_norm_g", "ln1_g", "ln1_b", "ln2_g", "ln2_b")
WEIGHTS = ("w_in", "conv_w", "a_log", "dt_bias", "gdn_norm_g", "q_norm_g", "w_uq", "kv_norm_g", "w_ukv", "w_out",
           "ln1_g", "ln1_b", "w_gate_up", "w_down", "ln2_g", "ln2_b", "w_ple", "w_ple_gate")
ROW_SHARDED = ("w_out", "w_down", "w_ple_gate")
N_CHIPS = 4


def _stream_of(name, shard_shape):
    if name in ("w_in", "w_uq"):
        return _Stream("piece")
    if name == "w_ukv":
        return _Stream("heads")
    if name in ROW_SHARDED:
        return _Stream("rows", shard_shape[0])
    return _Stream("cols", shard_shape[1])


def _pack_vectors(vecs, depth):
    flat = jnp.concatenate([vecs[n].reshape(depth, -1) for n in VECTORS], axis=1)
    pad = jnp.zeros((depth, VEC_ROWS * LANES - flat.shape[1]), F32)
    return jnp.concatenate([flat, pad], axis=1).reshape(depth, VEC_ROWS, LANES)


def _unpack_vectors(packed, shapes):
    depth = packed.shape[0]
    flat = packed.reshape(depth, VEC_ROWS * LANES)
    out, off = {}, 0
    for n in VECTORS:
        out[n] = flat[:, off:off + shapes[n][1]]
        off += shapes[n][1]
    return out


VEC_ROWS = 40


class _Dims:
    def __init__(self, d_model, in_width, q_lora, kv_lora, d_ff2, ple_dim):
        self.d = d_model
        self.hw = N_HEADS * HEAD_DIM
        self.in_width = in_width
        self.q_lora, self.kv_lora = q_lora, kv_lora
        self.ff2 = d_ff2
        self.ple = ple_dim
        self.c_kv0 = 4 * self.hw
        self.c_q0 = self.c_kv0 + kv_lora
        self.misc0 = self.c_q0 + q_lora
        self.h_width = self.misc0 + LANES
        assert self.c_kv0 % kv_lora == 0 and self.c_q0 % q_lora == 0 and self.misc0 % LANES == 0
        self.g_beta = 4 * self.hw
        self.g_a = self.g_beta + N_HEADS
        self.g_cq = self.g_a + N_HEADS
        self.g_ckv = self.g_cq + q_lora
        self.g_kr = self.g_ckv + kv_lora
        assert self.g_kr + ROPE_DIM == in_width

    def w_in_local(self, w):
        pad = jnp.zeros(w.shape[:-1] + (self.h_width - self.in_width,), w.dtype)
        return jnp.concatenate([w[..., :self.g_beta], w[..., self.g_ckv:self.g_kr], w[..., self.g_cq:self.g_ckv],
                                w[..., self.g_kr:], w[..., self.g_beta:self.g_cq], pad], axis=-1)

    def w_in_global(self, d):
        m = self.misc0
        return jnp.concatenate([d[..., :self.c_kv0], d[..., m + MISC_BETA0:m + MISC_A0 + N_HEADS],
                                d[..., self.c_q0:self.misc0], d[..., self.c_kv0:self.c_q0], d[..., m:m + ROPE_DIM]],
                               axis=-1)

    def w_uq_local(self, w):
        r = w.reshape(w.shape[:-1] + (N_HEADS, HEAD_DIM + ROPE_DIM))
        r = jnp.pad(r, [(0, 0)] * (r.ndim - 1) + [(0, QK_DIM - HEAD_DIM - ROPE_DIM)])
        return r.reshape(w.shape[:-1] + (N_HEADS * QK_DIM,))

    def w_uq_global(self, d):
        r = d.reshape(d.shape[:-1] + (N_HEADS, QK_DIM))[..., :HEAD_DIM + ROPE_DIM]
        return r.reshape(d.shape[:-1] + (N_HEADS * (HEAD_DIM + ROPE_DIM),))


def _lane_padded(n):
    return -(-n // LANES) * LANES


def _pad_lanes(a):
    pad = _lane_padded(a.shape[-1]) - a.shape[-1]
    return a if pad == 0 else jnp.pad(a, [(0, 0)] * (a.ndim - 1) + [(0, pad)])


def _lane_row(vec, lane0):
    pad = LANES - lane0 - vec.shape[0]
    return jnp.concatenate([jnp.zeros((lane0,), F32), vec.astype(F32), jnp.zeros((pad,), F32)])[None, :]


def _layer_fwd(dm, alpha, x, xb, p_i, cos_t, sin_t, wl, tag):
    d, hw = dm.d, dm.hw
    nm = lambda s: f"{s}_{tag}"
    mm = functools.partial(_matmul, layer=wl["layer"])
    h = mm(xb, wl["w_in"], dims="nn", name=nm("f_in"))
    misc_cb = dm.misc0 // LANES

    u = _conv_fwd(h, wl["conv_w"], 3 * hw, name=nm("f_conv"))
    qn, kn, vg, gb = _rowwise(_gdn_act, [(u, 3 * hw, 0), (h, LANES, misc_cb)], [wl["alog_row"], wl["dtb_row"]],
                              [(hw, F32), (hw, F32), (hw, F32), (LANES, F32)], name=nm("f_gdn_act"))
    prep, tinv = _gdn_prep_fwd(qn, kn, vg, gb, name=nm("f_gdn_prep"))
    o_gdn, s_all = _gdn_seq_fwd(prep, name=nm("f_gdn_seq"))
    (og,) = _rowwise(lambda o, z, g: (_gdn_out(o, z, g),), [(o_gdn, hw, 0), (h, hw, 3)], [wl["gn_row"]],
                     [(hw, BF16)], name=nm("f_gdn_out"))

    cqn, ckvn = _rowwise(_mla_norm, [(h, dm.kv_lora, dm.c_kv0 // dm.kv_lora), (h, dm.q_lora, dm.c_q0 // dm.q_lora)],
                         [wl["kvg_row"], wl["qg_row"]], [(dm.q_lora, BF16), (dm.kv_lora, BF16)], name=nm("f_mla_norm"))
    qm = mm(cqn, wl["w_uq"], dims="nn", name=nm("f_uq"))
    kvm = mm(ckvn, wl["w_ukv"], dims="nn", name=nm("f_ukv"))
    scale = (HEAD_DIM + ROPE_DIM) ** -0.5
    qk_fn = functools.partial(_mla_qk, scale)
    qa, ka, va = _rowwise(qk_fn, [(qm, N_HEADS * QK_DIM, 0), (kvm, 2 * hw, 0), (h, LANES, misc_cb),
                                  (cos_t, LANES, 0), (sin_t, LANES, 0)], [],
                          [(N_HEADS * QK_DIM, BF16), (N_HEADS * QK_DIM, BF16), (hw, BF16)], name=nm("f_mla_qk"))
    kt = _blocked_transpose(ka, QK_DIM)
    ones = jnp.ones((va.shape[0], HEAD_DIM), va.dtype)
    v1 = jnp.concatenate([part for vh in _heads(va) for part in (vh, ones)], axis=1)
    o_mla, lse = _attn_fwd(qa, kt, v1, name=nm("f_attn"))

    om = jnp.concatenate([og, o_mla], axis=1)
    mix = mm(om, wl["w_out"], dims="nn", name=nm("f_out"))
    ln1 = lambda xv, yv, g, b: (_layer_norm(alpha * xv + yv, g, b),) * 2
    x1, x1b = _rowwise(ln1, [(x, d, 0), (mix, d, 0)], [wl["ln1_g"], wl["ln1_b"]], [(d, F32), (d, BF16)], name=nm("f_ln1"))

    gu = mm(x1b, wl["w_gate_up"], dims="nn", name=nm("f_gate_up"), out_dtype=BF16)
    dn, act = mm(gu, wl["w_down"], dims="nn", name=nm("f_down"), a_gated=True, tm=GATED_TILE)
    x2, x2b = _rowwise(ln1, [(x1, d, 0), (dn, d, 0)], [wl["ln2_g"], wl["ln2_b"]], [(d, F32), (d, BF16)], name=nm("f_ln2"))

    pg = mm(x2b, wl["w_ple_gate"], dims="nn", name=nm("f_ple_gate"))
    pe = mm(p_i, wl["w_ple"], dims="nn", name=nm("f_ple"))
    out, outb = _rowwise(lambda a, b, c_: (_ple_out(a, b, c_),) * 2, [(x2, d, 0), (pg, d, 0), (pe, d, 0)], [],
                         [(d, F32), (d, BF16)], name=nm("f_ple_out"))
    saved = dict(x=x, xb=xb, p_i=p_i, h=h, u=u, qn=qn, kn=kn, vg=vg, gb=gb, prep=prep, tinv=tinv, s_all=s_all, o_gdn=o_gdn, cqn=cqn, ckvn=ckvn,
                 qm=qm, kvm=kvm, qa=qa, ka=ka, va=va, o_mla=o_mla, lse=lse, om=om, mix=mix, x1=x1, x1b=x1b, gu=gu,
                 act=act, dn=dn, x2=x2, x2b=x2b, pg=pg, pe=pe)
    return out, outb, saved


def _layer_bwd(dm, alpha, dout, sv, cos_t, sin_t, wl, gbuf, tag):
    d, hw = dm.d, dm.hw
    t = dout.shape[0]
    nm = lambda s: f"{s}_{tag}"
    gr = {}
    gbuf = dict(gbuf)
    misc_cb = dm.misc0 // LANES
    mm = functools.partial(_matmul, layer=wl["layer"])

    def wgrad(name_, a, g):
        gbuf[name_] = mm(a, g, dims="tn", name=nm("b_" + name_), into=gbuf[name_], tm=1408, tn=1408, tk=1024)

    dx2_a, dpg, dpe = _rowwise(_vjp_fn(_ple_out, 3, 1), [(sv["x2"], d, 0), (sv["pg"], d, 0), (sv["pe"], d, 0), (dout, d, 0)],
                               [], [(d, F32), (d, BF16), (d, BF16)], name=nm("b_ple_out"))
    wgrad("w_ple", sv["p_i"], dpe)
    wgrad("w_ple_gate", sv["x2b"], dpg)
    dx2 = mm(dpg, wl["w_ple_gate"], dims="nt", c=dx2_a, name=nm("b_x2"))

    def ln_bwd(xv, yv, ct, g, b):
        _, pull = jax.vjp(lambda a_, b_, c_, d_: _layer_norm(alpha * a_ + b_, c_, d_), xv, yv, g, b)
        return pull(ct)

    dx1_a, ddn, gr["ln2_g"], gr["ln2_b"] = _rowwise(
        ln_bwd, [(sv["x1"], d, 0), (sv["dn"], d, 0), (dx2, d, 0)], [wl["ln2_g"], wl["ln2_b"]],
        [(d, F32), (d, BF16)], [(1, d), (1, d)], name=nm("b_ln2"))
    wgrad("w_down", sv["act"], ddn)
    dact = mm(ddn, wl["w_down"], dims="nt", name=nm("b_act"), out_dtype=BF16)
    (dgu,) = _rowwise(_vjp_fn(_swiglu, 1, 1), [(sv["gu"], dm.ff2, 0), (dact, dm.ff2 // 2, 0)], [], [(dm.ff2, BF16)],
                      name=nm("b_swiglu"))
    wgrad("w_gate_up", sv["x1b"], dgu)
    dx1 = mm(dgu, wl["w_gate_up"], dims="nt", c=dx1_a, name=nm("b_x1"))

    dx_a, dmix, gr["ln1_g"], gr["ln1_b"] = _rowwise(
        ln_bwd, [(sv["x"], d, 0), (sv["mix"], d, 0), (dx1, d, 0)], [wl["ln1_g"], wl["ln1_b"]],
        [(d, F32), (d, BF16)], [(1, d), (1, d)], name=nm("b_ln1"))
    wgrad("w_out", sv["om"], dmix)
    dom = mm(dmix, wl["w_out"], dims="nt", name=nm("b_om"))

    nq = t // min(ATT_TILE, t)
    delta = _attn_delta(dom, sv["o_mla"], name=nm("b_attn_delta"))
    lse_row = sv["lse"].reshape(N_HEADS, nq, 1, t // nq)
    delta_row = delta[:, :N_HEADS].T.reshape(N_HEADS, nq, 1, t // nq)
    do_b = dom[:, hw:].astype(BF16)
    dka, dva, dqa = _attn_bwd(sv["qa"], _blocked_transpose(sv["qa"], QK_DIM), sv["ka"], sv["va"], lse_row, delta_row,
                              do_b, _blocked_transpose(do_b, HEAD_DIM), name=nm("b_attn"))
    scale = (HEAD_DIM + ROPE_DIM) ** -0.5
    qk_fn = functools.partial(_mla_qk, scale)

    def qk_bwd(qm, kvm, misc, cs, sn, g_q, g_k, g_v):
        _, pull = jax.vjp(lambda a, b, c_: qk_fn(a, b, c_, cs, sn), qm, kvm, misc)
        return pull((g_q, g_k, g_v))

    dqm, dkvm, dmisc_rope = _rowwise(
        qk_bwd, [(sv["qm"], N_HEADS * QK_DIM, 0), (sv["kvm"], 2 * hw, 0), (sv["h"], LANES, misc_cb), (cos_t, LANES, 0),
                 (sin_t, LANES, 0), (dqa, N_HEADS * QK_DIM, 0), (dka, N_HEADS * QK_DIM, 0), (dva, hw, 0)], [],
        [(N_HEADS * QK_DIM, BF16), (2 * hw, BF16), (LANES, F32)], name=nm("b_mla_qk"))
    wgrad("w_uq", sv["cqn"], dqm)
    wgrad("w_ukv", sv["ckvn"], dkvm)
    dcqn = mm(dqm, wl["w_uq"], dims="nt", name=nm("b_cqn"))
    dckvn = mm(dkvm, wl["w_ukv"], dims="nt", name=nm("b_ckvn"))

    def norm_bwd(ckv, cq, g_q, g_kv, kvg, qg):
        _, pull = jax.vjp(_mla_norm, ckv, cq, kvg, qg)
        return pull((g_q, g_kv))

    dckv, dcq, gr["kvg_row"], gr["qg_row"] = _rowwise(
        norm_bwd, [(sv["h"], dm.kv_lora, dm.c_kv0 // dm.kv_lora), (sv["h"], dm.q_lora, dm.c_q0 // dm.q_lora),
                   (dcqn, dm.q_lora, 0), (dckvn, dm.kv_lora, 0)], [wl["kvg_row"], wl["qg_row"]],
        [(dm.kv_lora, BF16), (dm.q_lora, BF16)], [(1, dm.kv_lora), (1, dm.q_lora)], name=nm("b_mla_norm"))

    def gout_bwd(o, z, g_o, gn):
        _, pull = jax.vjp(_gdn_out, o, z, gn)
        return pull(g_o)

    do_gdn, dz, gr["gn_row"] = _rowwise(gout_bwd, [(sv["o_gdn"], hw, 0), (sv["h"], hw, 3), (dom, hw, 0)], [wl["gn_row"]],
                                        [(hw, F32), (hw, BF16)], [(1, HEAD_DIM)], name=nm("b_gdn_out"))
    dprep = _gdn_seq_bwd(sv["prep"], sv["s_all"], do_gdn, name=nm("b_gdn_seq"))
    dqn, dkn, dvg, dgb = _gdn_prep_bwd(sv["qn"], sv["kn"], sv["vg"], sv["gb"], sv["tinv"], dprep, name=nm("b_gdn_prep"))

    def act_bwd(u, misc, g_q, g_k, g_v, g_gb, g_rope, alog, dtb):
        _, pull = jax.vjp(_gdn_act, u, misc, alog, dtb)
        du_, dmisc_, dalog_, ddtb_ = pull((g_q, g_k, g_v, g_gb))
        return du_, dmisc_ + g_rope, dalog_, ddtb_

    du, dmisc, gr["alog_row"], gr["dtb_row"] = _rowwise(
        act_bwd, [(sv["u"], 3 * hw, 0), (sv["h"], LANES, misc_cb), (dqn, hw, 0), (dkn, hw, 0), (dvg, hw, 0),
                  (dgb, LANES, 0), (dmisc_rope, LANES, 0)], [wl["alog_row"], wl["dtb_row"]],
        [(3 * hw, F32), (LANES, BF16)], [(1, LANES), (1, LANES)], name=nm("b_gdn_act"))
    dqkv, dconv = _conv_bwd(du, sv["h"], wl["conv_w"], 3 * hw, name=nm("b_conv"))
    gr["conv_w"] = dconv[:CONV_W]

    dh = jnp.concatenate([dqkv, dz, dckv, dcq, dmisc], axis=1)
    wgrad("w_in", sv["xb"], dh)
    dx = mm(dh, wl["w_in"], dims="nt", c=dx_a, name=nm("b_x"), tk=1408)
    return dx, gbuf, gr


LOCAL_MATRICES = ("w_in", "w_uq", "w_ukv", "w_out", "w_gate_up", "w_down", "w_ple", "w_ple_gate")


def _layer_weights(mats, vecs, layer):
    wl = {n: mats[n] for n in LOCAL_MATRICES}
    wl["layer"] = layer
    wl["conv_w"] = mats["conv_w"][layer]
    wl["alog_row"] = _lane_row(vecs["a_log"][layer], MISC_A0)
    wl["dtb_row"] = _lane_row(vecs["dt_bias"][layer], MISC_A0)
    wl["gn_row"] = vecs["gdn_norm_g"][layer][None, :]
    wl["qg_row"] = vecs["q_norm_g"][layer][None, :]
    wl["kvg_row"] = vecs["kv_norm_g"][layer][None, :]
    for n in ("ln1_g", "ln1_b", "ln2_g", "ln2_b"):
        wl[n] = vecs[n][layer][None, :]
    return wl


def _vector_grads(gr):
    out = {"a_log": gr["alog_row"][0, MISC_A0:MISC_A0 + N_HEADS], "dt_bias": gr["dtb_row"][0, MISC_A0:MISC_A0 + N_HEADS],
           "gdn_norm_g": gr["gn_row"][0], "q_norm_g": gr["qg_row"][0], "kv_norm_g": gr["kvg_row"][0]}
    for n in ("ln1_g", "ln1_b", "ln2_g", "ln2_b"):
        out[n] = gr[n][0]
    return out


def _local_step(dm, x, p, positions, target, mats, vecs):
    depth = p.shape[0]
    alpha = (2.0 * depth) ** 0.25
    freq = ROPE_THETA ** (-jnp.arange(0, ROPE_DIM, 2, dtype=F32) / ROPE_DIM)
    inv_freq_row = _lane_row(jnp.concatenate([freq, freq]), 0)
    cos_t, sin_t = _rope_tables(positions.reshape(-1, 1), inv_freq_row, name="rope_tables")

    wls = [_layer_weights(mats, vecs, i) for i in range(depth)]
    saved = []
    cur, cur_b = x, x
    for i in range(depth):
        cur, cur_b, sv = _layer_fwd(dm, alpha, cur, cur_b, p[i], cos_t, sin_t, wls[i], f"l{i}")
        saved.append(sv)
    dy, loss_blk = _loss_head(cur, target)
    gbuf = {n: depth for n in LOCAL_MATRICES}
    conv_g, vec_g = [None] * depth, [None] * depth
    for i in reversed(range(depth)):
        dy, gbuf, gr = _layer_bwd(dm, alpha, dy, saved[i], cos_t, sin_t, wls[i], gbuf, f"l{i}")
        conv_g[i] = gr["conv_w"]
        vec_g[i] = _vector_grads(gr)
    vec_grads = {n: jnp.stack([vec_g[i][n] for i in range(depth)]) for n in VECTORS}
    return loss_blk[0, 0], dy, gbuf, jnp.stack(conv_g), vec_grads


def kernel(x, p, positions, w_in, conv_w, a_log, dt_bias, gdn_norm_g, q_norm_g, w_uq, kv_norm_g, w_ukv, w_out, ln1_g, ln1_b, w_gate_up, w_down, ln2_g, ln2_b, w_ple, w_ple_gate, loss_target, m_w_in, m_conv_w, m_a_log, m_dt_bias, m_gdn_norm_g, m_q_norm_g, m_w_uq, m_kv_norm_g, m_w_ukv, m_w_out, m_ln1_g, m_ln1_b, m_w_gate_up, m_w_down, m_ln2_g, m_ln2_b, m_w_ple, m_w_ple_gate, v_w_in, v_conv_w, v_a_log, v_dt_bias, v_gdn_norm_g, v_q_norm_g, v_w_uq, v_kv_norm_g, v_w_ukv, v_w_out, v_ln1_g, v_ln1_b, v_w_gate_up, v_w_down, v_ln2_g, v_ln2_b, v_w_ple, v_w_ple_gate):
    w = dict(w_in=w_in, conv_w=conv_w, a_log=a_log, dt_bias=dt_bias, gdn_norm_g=gdn_norm_g, q_norm_g=q_norm_g, w_uq=w_uq,
             kv_norm_g=kv_norm_g, w_ukv=w_ukv, w_out=w_out, ln1_g=ln1_g, ln1_b=ln1_b, w_gate_up=w_gate_up, w_down=w_down,
             ln2_g=ln2_g, ln2_b=ln2_b, w_ple=w_ple, w_ple_gate=w_ple_gate)
    m = dict(w_in=m_w_in, conv_w=m_conv_w, a_log=m_a_log, dt_bias=m_dt_bias, gdn_norm_g=m_gdn_norm_g, q_norm_g=m_q_norm_g,
             w_uq=m_w_uq, kv_norm_g=m_kv_norm_g, w_ukv=m_w_ukv, w_out=m_w_out, ln1_g=m_ln1_g, ln1_b=m_ln1_b,
             w_gate_up=m_w_gate_up, w_down=m_w_down, ln2_g=m_ln2_g, ln2_b=m_ln2_b, w_ple=m_w_ple, w_ple_gate=m_w_ple_gate)
    v = dict(w_in=v_w_in, conv_w=v_conv_w, a_log=v_a_log, dt_bias=v_dt_bias, gdn_norm_g=v_gdn_norm_g, q_norm_g=v_q_norm_g,
             w_uq=v_w_uq, kv_norm_g=v_kv_norm_g, w_ukv=v_w_ukv, w_out=v_w_out, ln1_g=v_ln1_g, ln1_b=v_ln1_b,
             w_gate_up=v_w_gate_up, w_down=v_w_down, ln2_g=v_ln2_g, ln2_b=v_ln2_b, w_ple=v_w_ple, w_ple_gate=v_w_ple_gate)
    depth = w_in.shape[0]
    assert depth % 2 == 0
    hd = depth // 2
    dm = _Dims(x.shape[2], N_CHIPS * w_in.shape[2], w_uq.shape[1], w_ukv.shape[1], N_CHIPS * w_gate_up.shape[2], p.shape[3])
    cx, cy, cc = lax.axis_index("x"), lax.axis_index("y"), lax.axis_index("c")
    chip = 2 * cx + cy

    g_streams = [_stream_of(n, w[n].shape[1:]) for n in MATRICES]
    shards = [w[n] if n == "conv_w" else w[n].astype(BF16) for n in MATRICES]
    shards = [_pad_lanes(s) if st.kind == "piece" else s for s, st in zip(shards, g_streams)]
    g_shapes = []
    for s, st in zip(shards, g_streams):
        if st.kind == "piece":
            shape = (N_CHIPS,) + s.shape
        elif st.kind == "rows":
            shape = (depth, N_CHIPS * s.shape[1], s.shape[2])
        else:
            shape = (depth, s.shape[1], N_CHIPS * s.shape[2])
        g_shapes.append(jax.ShapeDtypeStruct(shape, s.dtype))
    mats = dict(zip(MATRICES, _gather_chips(shards, g_streams, g_shapes, name="gather_weights")))
    for n, to_local in (("w_in", dm.w_in_local), ("w_uq", dm.w_uq_local)):
        pieces = jnp.moveaxis(mats[n][..., :w[n].shape[2]], 0, 2)
        mats[n] = to_local(pieces.reshape(pieces.shape[:2] + (-1,)))
    vecs = {n: w[n] for n in VECTORS}

    loss_local, grad_x, gbuf, conv_g, vec_g = _local_step(dm, x[0], p[:, 0], positions[0], loss_target[0], mats, vecs)
    loss = lax.psum(loss_local, ("x", "y", "c"))

    names = list(LOCAL_MATRICES) + ["conv_w", "vectors"]
    gs = [gbuf[n] for n in LOCAL_MATRICES] + [conv_g, _pack_vectors(vec_g, depth)]
    wire = [BF16] * len(LOCAL_MATRICES) + [F32, F32]
    r_streams = [_stream_of(n, w[n].shape[1:]) for n in LOCAL_MATRICES]
    r_streams += [_stream_of("conv_w", w["conv_w"].shape[1:]), _Stream("whole")]
    shard_shapes = [(hd, w[n].shape[1], _lane_padded(w[n].shape[2])) if st.kind == "piece" else (hd,) + w[n].shape[1:]
                    for n, st in zip(LOCAL_MATRICES, r_streams)]
    shard_shapes += [(hd,) + w["conv_w"].shape[1:], (hd, VEC_ROWS, LANES)]
    c_idx = cc.reshape(1).astype(jnp.int32)
    place = (chip.reshape(1).astype(jnp.int32), c_idx)
    from_sibling = _sibling_take_other_half(gs, name="reduce_sibling")
    chip_sum = [_add_own_half(g, a, c_idx, dt, name=f"reduce_add_{n}")
                for g, a, dt, n in zip(gs, from_sibling, wire, names)]
    for i, n in enumerate(names):
        if r_streams[i].kind == "piece":
            glob = dm.w_in_global(chip_sum[i]) if n == "w_in" else dm.w_uq_global(chip_sum[i])
            glob = glob.reshape(glob.shape[:2] + (N_CHIPS, glob.shape[2] // N_CHIPS))
            chip_sum[i] = jnp.moveaxis(_pad_lanes(glob), 2, 0)
    from_chips = _chips_exchange(chip_sum, r_streams, shard_shapes, name="reduce_chips")
    halves = [_sum_chips(ps, got, place, st, name=f"reduce_sum_{n}")
              for ps, got, st, n in zip(chip_sum, from_chips, r_streams, names)]
    joined = dict(zip(names, _sibling_join_halves(halves, name="reduce_join")))
    joined.update(_unpack_vectors(joined.pop("vectors"), {n: w[n].shape for n in VECTORS}))

    grad_w, delta_w, new_m, new_v = {}, {}, {}, {}
    for n in WEIGHTS:
        grad_w[n] = joined[n][..., :w[n].shape[-1]]
        delta_w[n], new_m[n], new_v[n] = _adamw(w[n], grad_w[n], m[n], v[n], name=f"adamw_{n}")
    return (loss, grad_x[None], *[grad_w[n] for n in WEIGHTS], *[delta_w[n] for n in WEIGHTS],
            *[new_m[n] for n in WEIGHTS], *[new_v[n] for n in WEIGHTS])
```

```python
import functools

import jax
import jax.numpy as jnp
from jax import lax
from jax.experimental import pallas as pl
from jax.experimental.pallas import tpu as pltpu

F32 = jnp.float32
BF16 = jnp.bfloat16
MESH = pl.DeviceIdType.MESH

CHUNK = 64
N_HEADS = 4
HEAD_DIM = 128
ROPE_DIM = 64
ROPE_THETA = 10000.0
LN_EPS = 1e-5
RMS_EPS = 1e-6
ADAM_LR, ADAM_B1, ADAM_B2, ADAM_EPS, ADAM_WD, ADAM_STEP = 0.001, 0.9, 0.999, 1e-08, 0.01, 10

LANES = 128
VMEM_LIMIT = 48 * 1024 * 1024
ROW_TILE = 512
WIDE_ROW_TILE = 256
WIDE_COLS = 2048
GATED_TILE = 512
SUB_ROWS = 16
MAX_SUB_ROWS = 64
VREG_FILE_ELEMS = 64 * 8 * LANES

MISC_BETA0 = ROPE_DIM
MISC_A0 = ROPE_DIM + N_HEADS

NN = (((1,), (0,)), ((), ()))
NT = (((1,), (1,)), ((), ()))
TN = (((0,), (0,)), ((), ()))


def _params(sem=None):
    return pltpu.CompilerParams(dimension_semantics=sem, vmem_limit_bytes=VMEM_LIMIT)


def _divisor_tile(dim, target, unit):
    best = None
    t = unit
    while t <= min(dim, target):
        if dim % t == 0:
            best = t
        t += unit
    return best if best is not None else dim


BATCHED = {NN: (((2,), (1,)), ((0,), (0,))), NT: (((2,), (2,)), ((0,), (0,))), TN: (((1,), (1,)), ((0,), (0,)))}


def _make_dots():
    def raw(a, b, dims):
        if a.ndim == 3:
            dims = BATCHED[dims]
        return lax.dot_general(a.astype(BF16), b.astype(BF16), dims, preferred_element_type=F32)

    @jax.custom_vjp
    def nn(a, b):
        return raw(a, b, NN)

    @jax.custom_vjp
    def nt(a, b):
        return raw(a, b, NT)

    @jax.custom_vjp
    def tn(a, b):
        return raw(a, b, TN)

    nn.defvjp(lambda a, b: (raw(a, b, NN), (a, b)), lambda r, g: (nt(g, r[1]), tn(r[0], g)))
    nt.defvjp(lambda a, b: (raw(a, b, NT), (a, b)), lambda r, g: (nn(g, r[1]), tn(g, r[0])))
    tn.defvjp(lambda a, b: (raw(a, b, TN), (a, b)), lambda r, g: (nt(r[1], g), nn(r[0], g)))
    return nn, nt, tn


_nn, _nt, _tn = _make_dots()


def _matmul(a, b, *, dims, name, c=None, out_dtype=F32, tm=1024, tn=1408, tk=1408, layer=None, into=None,
            a_gated=False):
    b_shape = b.shape[-2:]
    a_shape = (a.shape[0], a.shape[1] // 2) if a_gated else a.shape
    if dims == "nn":
        (m, k), (k2, n) = a_shape, b_shape
    elif dims == "nt":
        (m, k), (n, k2) = a_shape, b_shape
    else:
        (k, m), (k2, n) = a_shape, b_shape
    assert k == k2, (a.shape, b.shape, dims)
    tm = _divisor_tile(m, tm, LANES)
    tn = _divisor_tile(n, tn, LANES)
    tk = _divisor_tile(k, tk, LANES)
    nk = k // tk
    dn = {"nn": NN, "nt": NT, "tn": TN}[dims]
    if dims == "tn":
        a_blk, a_idx, up_off = (tk, tm), (lambda i, j, kk: (kk, i)), m // tm
    else:
        a_blk, a_idx, up_off = (tm, tk), (lambda i, j, kk: (i, kk)), k // tk
    a_spec = pl.BlockSpec(a_blk, a_idx)
    up_spec = pl.BlockSpec(a_blk, lambda i, j, kk: (a_idx(i, j, kk)[0], a_idx(i, j, kk)[1] + up_off))
    b_blk, b_idx = ((tn, tk), lambda i, j, kk: (j, kk)) if dims == "nt" else ((tk, tn), lambda i, j, kk: (kk, j))
    if b.ndim == 3:
        b_spec = pl.BlockSpec((None,) + b_blk, lambda i, j, kk: (layer,) + b_idx(i, j, kk))
    else:
        b_spec = pl.BlockSpec(b_blk, b_idx)
    c_spec = pl.BlockSpec((tm, tn), lambda i, j, kk: (i, j))
    if isinstance(into, int):
        o_spec = pl.BlockSpec((None, tm, tn), lambda i, j, kk: (layer, i, j))
        out_shape = jax.ShapeDtypeStruct((into, m, n), out_dtype)
        into = None
    elif into is not None:
        assert into.shape[1:] == (m, n) and into.dtype == out_dtype
        o_spec = pl.BlockSpec((None, tm, tn), lambda i, j, kk: (layer, i, j))
        out_shape = jax.ShapeDtypeStruct(into.shape, into.dtype)
    else:
        o_spec = c_spec
        out_shape = jax.ShapeDtypeStruct((m, n), out_dtype)
    has_c = c is not None

    n_a = 2 if a_gated else 1

    def body(*refs):
        b_ref = refs[n_a]
        c_ref = refs[n_a + 1] if has_c else None
        acc_ref = refs[-1]
        o_ref = refs[-3] if a_gated else refs[-2]
        kk = pl.program_id(2)

        @pl.when(kk == 0)
        def _():
            if has_c:
                acc_ref[...] = c_ref[...].astype(F32)
            else:
                acc_ref[...] = jnp.zeros_like(acc_ref)

        if a_gated:
            a_val = (_silu(refs[0][...].astype(F32)) * refs[1][...].astype(F32)).astype(BF16)
            refs[-2][...] = a_val
        else:
            a_val = refs[0][...].astype(BF16)
        acc_ref[...] += lax.dot_general(a_val, b_ref[...].astype(BF16), dn, preferred_element_type=F32)

        @pl.when(kk == nk - 1)
        def _():
            o_ref[...] = acc_ref[...].astype(o_ref.dtype)

    ins = ([a, a] if a_gated else [a]) + [b] + ([c] if has_c else [])
    specs = ([a_spec, up_spec] if a_gated else [a_spec]) + [b_spec] + ([c_spec] if has_c else [])
    aliases = {}
    if into is not None:
        aliases = {len(ins): 0}
        ins.append(into)
        specs.append(pl.BlockSpec(memory_space=pl.ANY))
    if a_gated:
        assert dims == "nn" and n == tn
        o_spec, out_shape = [o_spec, a_spec], [out_shape, jax.ShapeDtypeStruct((m, k), BF16)]
    return pl.pallas_call(
        body, name=name, grid=(m // tm, n // tn, nk), in_specs=specs, out_specs=o_spec, out_shape=out_shape,
        scratch_shapes=[pltpu.VMEM((tm, tn), F32)], input_output_aliases=aliases,
        compiler_params=_params(("arbitrary", "arbitrary", "arbitrary")),
    )(*ins)


def _rowwise(fn, rows, params, outs, accs=(), *, name):
    t = rows[0][0].shape[0]
    widest = max([w for _, w, _ in rows] + [w for w, _ in outs])
    tm = min(WIDE_ROW_TILE if widest > WIDE_COLS else ROW_TILE, t)
    sub = SUB_ROWS
    while sub < MAX_SUB_ROWS and 2 * sub * widest <= VREG_FILE_ELEMS:
        sub *= 2
    assert t % tm == 0 and tm % sub == 0
    n_rows, n_par, n_out, n_acc = len(rows), len(params), len(outs), len(accs)

    def body(*refs):
        row_refs = refs[:n_rows]
        par_refs = refs[n_rows:n_rows + n_par]
        out_refs = refs[n_rows + n_par:n_rows + n_par + n_out]
        acc_refs = refs[n_rows + n_par + n_out:]
        if n_acc:
            @pl.when(pl.program_id(0) == 0)
            def _():
                for a_ref in acc_refs:
                    a_ref[...] = jnp.zeros_like(a_ref)

        def step(r, carry):
            sl = pl.ds(pl.multiple_of(r * sub, sub), sub)
            vals = [ref[sl, :].astype(F32) for ref in row_refs] + [ref[...] for ref in par_refs]
            res = fn(*vals)
            for o_ref, val in zip(out_refs, res[:n_out]):
                o_ref[sl, :] = val.astype(o_ref.dtype)
            for a_ref, val in zip(acc_refs, res[n_out:]):
                a_ref[...] += val
            return carry

        lax.fori_loop(0, tm // sub, step, 0)

    in_specs = [pl.BlockSpec((tm, w), functools.partial(lambda i, cb: (i, cb), cb=cb)) for _, w, cb in rows]
    in_specs += [pl.BlockSpec(p.shape, lambda i: (0, 0)) for p in params]
    out_specs = [pl.BlockSpec((tm, w), lambda i: (i, 0)) for w, _ in outs]
    out_specs += [pl.BlockSpec(s, lambda i: (0, 0)) for s in accs]
    out_shape = [jax.ShapeDtypeStruct((t, w), d) for w, d in outs]
    out_shape += [jax.ShapeDtypeStruct(s, F32) for s in accs]
    return pl.pallas_call(
        body, name=name, grid=(t // tm,), in_specs=in_specs, out_specs=out_specs, out_shape=out_shape,
        compiler_params=_params(("arbitrary",)),
    )(*[r[0] for r in rows], *params)


def _vjp_fn(fn, n_in, n_out):
    def bwd(*args):
        ins, cts = args[:n_in], args[n_in:]
        _, pull = jax.vjp(fn, *ins)
        return pull(tuple(cts) if n_out > 1 else cts[0])
    return bwd


def _lane(shape):
    return lax.broadcasted_iota(jnp.int32, shape, 1)


def _silu(x):
    return x * jax.nn.sigmoid(x)


def _softplus(x):
    return jnp.maximum(x, 0.0) + jnp.log1p(jnp.exp(-jnp.abs(x)))


def _heads(x, width=HEAD_DIM):
    return [x[:, h * width:(h + 1) * width] for h in range(N_HEADS)]


def _layer_norm(z, g, b):
    mu = jnp.mean(z, -1, keepdims=True)
    zc = z - mu
    var = jnp.mean(zc * zc, -1, keepdims=True)
    return zc * lax.rsqrt(var + LN_EPS) * g + b


def _gdn_act(u, misc, alog_row, dtb_row):
    s = _silu(u)
    w = N_HEADS * HEAD_DIM
    q = jnp.concatenate([t * lax.rsqrt(jnp.sum(t * t, -1, keepdims=True) + RMS_EPS) * HEAD_DIM ** -0.5
                         for t in _heads(s[:, :w])], axis=1)
    k = jnp.concatenate([t * lax.rsqrt(jnp.sum(t * t, -1, keepdims=True) + RMS_EPS)
                         for t in _heads(s[:, w:2 * w])], axis=1)
    v = s[:, 2 * w:]
    lane = _lane(misc.shape)
    beta = jax.nn.sigmoid(misc)
    g = -jnp.exp(alog_row) * _softplus(misc + dtb_row)
    is_beta = (lane >= MISC_BETA0) & (lane < MISC_BETA0 + N_HEADS)
    is_g = (lane >= MISC_A0) & (lane < MISC_A0 + N_HEADS)
    gb = jnp.where(is_beta, beta, jnp.where(is_g, g, 0.0))
    return q, k, v, gb


def _gdn_out(o, z, gn_row):
    outs = []
    for oh, zh in zip(_heads(o), _heads(z)):
        r = oh * lax.rsqrt(jnp.mean(oh * oh, -1, keepdims=True) + RMS_EPS) * gn_row
        outs.append(r * _silu(zh))
    return jnp.concatenate(outs, axis=1)


def _mla_norm(ckv, cq, kvg_row, qg_row):
    cqn = cq * lax.rsqrt(jnp.mean(cq * cq, -1, keepdims=True) + RMS_EPS) * qg_row
    ckvn = ckv * lax.rsqrt(jnp.mean(ckv * ckv, -1, keepdims=True) + RMS_EPS) * kvg_row
    return cqn, ckvn


def _swap_halves(x):
    half = ROPE_DIM // 2
    return jnp.where(_lane(x.shape) < half, pltpu.roll(x, LANES - half, 1), pltpu.roll(x, half, 1))


@jax.custom_vjp
def _rope(x, cos_t, sin_t):
    return x * cos_t + _swap_halves(x) * sin_t


def _rope_fwd(x, cos_t, sin_t):
    return _rope(x, cos_t, sin_t), (cos_t, sin_t)


def _rope_bwd(res, g):
    cos_t, sin_t = res
    return g * cos_t - _swap_halves(g) * sin_t, jnp.zeros_like(cos_t), jnp.zeros_like(sin_t)


_rope.defvjp(_rope_fwd, _rope_bwd)


def _mla_qk(scale, qm, kv, misc, cos_t, sin_t):
    krope = _rope(misc, cos_t, sin_t)
    qs, ks = [], []
    for h in range(N_HEADS):
        base = 2 * HEAD_DIM * h
        qs += [qm[:, base:base + HEAD_DIM], _rope(qm[:, base + HEAD_DIM:base + 2 * HEAD_DIM], cos_t, sin_t)]
        ks += [kv[:, HEAD_DIM * h:HEAD_DIM * (h + 1)], krope]
    return jnp.concatenate(qs, axis=1) * scale, jnp.concatenate(ks, axis=1), kv[:, N_HEADS * HEAD_DIM:]


def _swiglu(gu):
    f = gu.shape[1] // 2
    return _silu(gu[:, :f]) * gu[:, f:]


def _ple_out(x2, pg, pe):
    return x2 + jax.nn.sigmoid(pg) * pe


CONV_W = 4
HALO = 8
CONV_STRIP = 512


def _conv_fwd(h, conv_w, width, *, name, tm=ROW_TILE, sub=32):
    t = h.shape[0]
    tm = min(tm, t)
    nb = tm // HALO

    def body(x_ref, halo_ref, w_ref, u_ref, buf):
        i = pl.program_id(0)
        buf[pl.ds(0, HALO), :] = jnp.where(i > 0, halo_ref[...], 0.0)
        buf[pl.ds(HALO, tm), :] = x_ref[...]
        for c0 in range(0, width, CONV_STRIP):
            cols = pl.ds(c0, CONV_STRIP)
            w = w_ref[:, cols]
            for r0 in range(0, tm, sub):
                acc = jnp.zeros((sub, CONV_STRIP), F32)
                for j in range(CONV_W):
                    acc = acc + w[j:j + 1, :] * buf[pl.ds(HALO + r0 - (CONV_W - 1) + j, sub), cols]
                u_ref[pl.ds(r0, sub), cols] = acc

    return pl.pallas_call(
        body, name=name, grid=(t // tm,),
        in_specs=[pl.BlockSpec((tm, width), lambda i: (i, 0)),
                  pl.BlockSpec((HALO, width), lambda i: (jnp.maximum(i * nb - 1, 0), 0)),
                  pl.BlockSpec(conv_w.shape, lambda i: (0, 0))],
        out_specs=pl.BlockSpec((tm, width), lambda i: (i, 0)),
        out_shape=jax.ShapeDtypeStruct((t, width), F32),
        scratch_shapes=[pltpu.VMEM((tm + HALO, width), F32)],
        compiler_params=_params(("arbitrary",)),
    )(h, h, conv_w)


def _conv_bwd(du, h, conv_w, width, *, name, tm=ROW_TILE, sub=32):
    t = h.shape[0]
    tm = min(tm, t)
    nb = tm // HALO
    n_tiles = t // tm

    def body(du_ref, du_halo, x_ref, x_halo, w_ref, dx_ref, dw_ref, dbuf, xbuf):
        i = pl.program_id(0)

        @pl.when(i == 0)
        def _():
            dw_ref[...] = jnp.zeros_like(dw_ref)

        dbuf[pl.ds(0, tm), :] = du_ref[...]
        dbuf[pl.ds(tm, HALO), :] = jnp.where(i < n_tiles - 1, du_halo[...], 0.0)
        xbuf[pl.ds(0, HALO), :] = jnp.where(i > 0, x_halo[...], 0.0)
        xbuf[pl.ds(HALO, tm), :] = x_ref[...]
        for c0 in range(0, width, CONV_STRIP):
            cols = pl.ds(c0, CONV_STRIP)
            w = w_ref[:, cols]
            dws = [jnp.zeros((HALO, CONV_STRIP), F32) for _ in range(CONV_W)]
            for r0 in range(0, tm, sub):
                acc = jnp.zeros((sub, CONV_STRIP), F32)
                d_here = dbuf[pl.ds(r0, sub), cols]
                for j in range(CONV_W):
                    acc = acc + w[j:j + 1, :] * dbuf[pl.ds(r0 + (CONV_W - 1) - j, sub), cols]
                    prod = d_here * xbuf[pl.ds(HALO + r0 - (CONV_W - 1) + j, sub), cols]
                    for g0 in range(0, sub, HALO):
                        dws[j] = dws[j] + prod[g0:g0 + HALO, :]
                dx_ref[pl.ds(r0, sub), cols] = acc.astype(dx_ref.dtype)
            for j in range(CONV_W):
                dw_ref[pl.ds(j, 1), cols] += jnp.sum(dws[j], axis=0, keepdims=True)

    return pl.pallas_call(
        body, name=name, grid=(n_tiles,),
        in_specs=[pl.BlockSpec((tm, width), lambda i: (i, 0)),
                  pl.BlockSpec((HALO, width), lambda i: (jnp.minimum((i + 1) * nb, t // HALO - 1), 0)),
                  pl.BlockSpec((tm, width), lambda i: (i, 0)),
                  pl.BlockSpec((HALO, width), lambda i: (jnp.maximum(i * nb - 1, 0), 0)),
                  pl.BlockSpec(conv_w.shape, lambda i: (0, 0))],
        out_specs=[pl.BlockSpec((tm, width), lambda i: (i, 0)),
                   pl.BlockSpec((HALO, width), lambda i: (0, 0))],
        out_shape=[jax.ShapeDtypeStruct((t, width), BF16), jax.ShapeDtypeStruct((HALO, width), F32)],
        scratch_shapes=[pltpu.VMEM((tm + HALO, width), F32), pltpu.VMEM((tm + HALO, width), F32)],
        compiler_params=_params(("arbitrary",)),
    )(du, du, h, h, conv_w)


@jax.custom_vjp
def _inv_unit_lower(low):
    n = low.shape[-1]
    eye = (lax.broadcasted_iota(jnp.int32, (n, n), 0) == lax.broadcasted_iota(jnp.int32, (n, n), 1)).astype(F32)
    x = eye - low
    p = low
    span = 2
    while span < n:
        p = _nn(p, p)
        x = x + _nn(x, p)
        span *= 2
    return x


def _inv_fwd(low):
    x = _inv_unit_lower(low)
    return x, x


def _inv_bwd(x, g):
    return (-_tn(x, _nt(g, x)),)


_inv_unit_lower.defvjp(_inv_fwd, _inv_bwd)


@jax.custom_vjp
def _inv_known(low, inverse):
    return inverse


_inv_known.defvjp(lambda low, inverse: (inverse, inverse), lambda x, g: (_inv_bwd(x, g)[0], jnp.zeros_like(x)))


def _gdn_prep(q, k, v, gb, known_inverse=None):
    c = CHUNK
    n = q.shape[0] // c
    pairs = [(g, h) for g in range(n) for h in range(N_HEADS)]
    row = lax.broadcasted_iota(jnp.int32, (c, c), 0)
    col = lax.broadcasted_iota(jnp.int32, (c, c), 1)
    tri_incl = row >= col
    tri_strict = row > col
    lane = _lane((c, LANES))
    sub = lax.broadcasted_iota(jnp.int32, (LANES, c), 0)
    last = lax.broadcasted_iota(jnp.int32, (c, 1), 0) == c - 1

    def split(x):
        return jnp.stack([x[g * c:(g + 1) * c, h * HEAD_DIM:(h + 1) * HEAD_DIM] for g, h in pairs])

    gbs = [gb[g * c:(g + 1) * c, :] for g in range(n)]
    gbts = [x.T for x in gbs]
    g_col = jnp.stack([jnp.sum(jnp.where(lane == MISC_A0 + h, gbs[g], 0.0), axis=1, keepdims=True) for g, h in pairs])
    b_col = jnp.stack([jnp.sum(jnp.where(lane == MISC_BETA0 + h, gbs[g], 0.0), axis=1, keepdims=True) for g, h in pairs])
    g_row = jnp.stack([jnp.sum(jnp.where(sub == MISC_A0 + h, gbts[g], 0.0), axis=0, keepdims=True) for g, h in pairs])
    gc_col = jnp.sum(jnp.where(tri_incl, g_row, 0.0), axis=2, keepdims=True)
    gc_row = jnp.sum(jnp.where(row <= col, g_col, 0.0), axis=1, keepdims=True)
    decay = jnp.where(tri_incl, jnp.exp(jnp.where(tri_incl, gc_col - gc_row, 0.0)), 0.0)
    g_last = jnp.sum(jnp.where(last, gc_col, 0.0), axis=1, keepdims=True)
    qs, ks, vs = split(q), split(k), split(v)
    kb = ks * b_col
    low = jnp.where(tri_strict, _nt(kb, ks) * decay, 0.0)
    if known_inverse is None:
        tinv = _inv_unit_lower(low)
    else:
        tinv = _inv_known(low, jnp.stack([known_inverse[g * c:(g + 1) * c, h * c:(h + 1) * c] for g, h in pairs]))
    eg = jnp.exp(gc_col)
    sol = _nn(tinv, jnp.concatenate([vs * b_col, kb * eg], axis=2))
    attn = jnp.where(tri_incl, _nt(qs, ks) * decay, 0.0)
    qd = qs * eg
    kd = ks * jnp.exp(g_last - gc_col)

    def merge(x):
        return jnp.concatenate([jnp.concatenate([x[g * N_HEADS + h] for h in range(N_HEADS)], axis=1)
                                for g in range(n)], axis=0)

    glb = jnp.concatenate([sum(jnp.where(lane == h, g_last[g * N_HEADS + h], 0.0) for h in range(N_HEADS))
                           for g in range(n)], axis=0)
    outs = (merge(sol[:, :, :HEAD_DIM]), merge(sol[:, :, HEAD_DIM:]), merge(qd), merge(kd), merge(attn), glb)
    return outs, merge(tinv)


def _gdn_seq(state, u, w, qd, kd, attn, glb):
    c = u.shape[0]
    first = lax.broadcasted_iota(jnp.int32, glb.shape, 0) == 0
    lane = _lane(glb.shape)
    heads = lambda x: jnp.stack([x[:, h * HEAD_DIM:(h + 1) * HEAD_DIM] for h in range(N_HEADS)])
    g_last = jnp.stack([jnp.sum(jnp.sum(jnp.where(first & (lane == h), glb, 0.0), axis=1, keepdims=True),
                                axis=0, keepdims=True) for h in range(N_HEADS)])
    s = jnp.stack([state[h * HEAD_DIM:(h + 1) * HEAD_DIM, :] for h in range(N_HEADS)])
    at = jnp.stack([attn[:, h * c:(h + 1) * c] for h in range(N_HEADS)])
    v_new = heads(u) - _nn(heads(w), s)
    o = _nn(heads(qd), s) + _nn(at, v_new)
    s_new = s * jnp.exp(g_last) + _tn(heads(kd), v_new)
    return (jnp.concatenate([o[h] for h in range(N_HEADS)], axis=1),
            jnp.concatenate([s_new[h] for h in range(N_HEADS)], axis=0))


PREP_CHUNKS = 8
PREP_CHUNKS_BWD = 4
SEQ_CHUNKS = 8


def _gdn_prep_fwd(q, k, v, gb, *, name):
    t, w = q.shape
    rows = min(PREP_CHUNKS * CHUNK, t)

    def body(q_ref, k_ref, v_ref, gb_ref, *out_refs):
        outs, inverse = _gdn_prep(q_ref[...], k_ref[...], v_ref[...], gb_ref[...])
        for o_ref, val in zip(out_refs, outs + (inverse,)):
            o_ref[...] = val

    spec = lambda width: pl.BlockSpec((rows, width), lambda i: (i, 0))
    widths = [w, w, w, w, N_HEADS * CHUNK, LANES, N_HEADS * CHUNK]
    res = pl.pallas_call(
        body, name=name, grid=(t // rows,),
        in_specs=[spec(w), spec(w), spec(w), spec(LANES)],
        out_specs=[spec(x) for x in widths],
        out_shape=[jax.ShapeDtypeStruct((t, x), F32) for x in widths],
        compiler_params=_params(("arbitrary",)),
    )(q, k, v, gb)
    return tuple(res[:6]), res[6]


def _gdn_prep_bwd(q, k, v, gb, inverse, cts, *, name):
    t, w = q.shape
    rows = min(PREP_CHUNKS_BWD * CHUNK, t)

    def body(q_ref, k_ref, v_ref, gb_ref, inv_ref, du, dw, dqd, dkd, dattn, dglb, dq_ref, dk_ref, dv_ref, dgb_ref):
        known = inv_ref[...]
        _, pull = jax.vjp(lambda a, b, c_, d_: _gdn_prep(a, b, c_, d_, known)[0],
                          q_ref[...], k_ref[...], v_ref[...], gb_ref[...])
        dq, dk, dv, dgb = pull(tuple(r[...] for r in (du, dw, dqd, dkd, dattn, dglb)))
        dq_ref[...] = dq
        dk_ref[...] = dk
        dv_ref[...] = dv
        dgb_ref[...] = dgb

    spec = lambda width: pl.BlockSpec((rows, width), lambda i: (i, 0))
    widths = [w, w, w, w, N_HEADS * CHUNK, LANES]
    return pl.pallas_call(
        body, name=name, grid=(t // rows,),
        in_specs=[spec(w), spec(w), spec(w), spec(LANES), spec(N_HEADS * CHUNK)] + [spec(x) for x in widths],
        out_specs=[spec(w), spec(w), spec(w), spec(LANES)],
        out_shape=[jax.ShapeDtypeStruct((t, w), F32)] * 3 + [jax.ShapeDtypeStruct((t, LANES), F32)],
        compiler_params=_params(("arbitrary",)),
    )(q, k, v, gb, inverse, *cts)


def _gdn_seq_fwd(prep, *, name):
    t, w = prep[0].shape
    rows = min(SEQ_CHUNKS * CHUNK, t)
    per = rows // CHUNK

    def body(u_ref, w_ref, qd_ref, kd_ref, at_ref, gl_ref, o_ref, sall_ref, s_scr):
        @pl.when(pl.program_id(0) == 0)
        def _():
            s_scr[...] = jnp.zeros_like(s_scr)

        def step(j, carry):
            sl = pl.ds(pl.multiple_of(j * CHUNK, CHUNK), CHUNK)
            s = s_scr[...]
            sall_ref[j] = s
            o, s_new = _gdn_seq(s, u_ref[sl, :], w_ref[sl, :], qd_ref[sl, :], kd_ref[sl, :], at_ref[sl, :], gl_ref[sl, :])
            o_ref[sl, :] = o
            s_scr[...] = s_new
            return carry

        lax.fori_loop(0, per, step, 0)

    spec = lambda width: pl.BlockSpec((rows, width), lambda i: (i, 0))
    widths = [w, w, w, w, N_HEADS * CHUNK, LANES]
    return pl.pallas_call(
        body, name=name, grid=(t // rows,),
        in_specs=[spec(x) for x in widths],
        out_specs=[spec(w), pl.BlockSpec((per, w, HEAD_DIM), lambda i: (i, 0, 0))],
        out_shape=[jax.ShapeDtypeStruct((t, w), F32), jax.ShapeDtypeStruct((t // CHUNK, w, HEAD_DIM), F32)],
        scratch_shapes=[pltpu.VMEM((w, HEAD_DIM), F32)],
        compiler_params=_params(("arbitrary",)),
    )(*prep)


def _gdn_seq_bwd(prep, s_all, do, *, name):
    t, w = prep[0].shape
    rows = min(SEQ_CHUNKS * CHUNK, t)
    per = rows // CHUNK
    n = t // rows

    def body(u_ref, w_ref, qd_ref, kd_ref, at_ref, gl_ref, sall_ref, do_ref, du, dw, dqd, dkd, dat, dgl, ds_scr):
        @pl.when(pl.program_id(0) == 0)
        def _():
            ds_scr[...] = jnp.zeros_like(ds_scr)

        def step(jj, carry):
            j = per - 1 - jj
            sl = pl.ds(pl.multiple_of(j * CHUNK, CHUNK), CHUNK)
            _, pull = jax.vjp(_gdn_seq, sall_ref[j], u_ref[sl, :], w_ref[sl, :], qd_ref[sl, :], kd_ref[sl, :],
                              at_ref[sl, :], gl_ref[sl, :])
            res = pull((do_ref[sl, :], ds_scr[...]))
            ds_scr[...] = res[0]
            for o_ref, val in zip((du, dw, dqd, dkd, dat, dgl), res[1:]):
                o_ref[sl, :] = val
            return carry

        lax.fori_loop(0, per, step, 0)

    spec = lambda width: pl.BlockSpec((rows, width), lambda i: (n - 1 - i, 0))
    widths = [w, w, w, w, N_HEADS * CHUNK, LANES]
    return pl.pallas_call(
        body, name=name, grid=(n,),
        in_specs=[spec(x) for x in widths] + [pl.BlockSpec((per, w, HEAD_DIM), lambda i: (n - 1 - i, 0, 0)), spec(w)],
        out_specs=[spec(x) for x in widths],
        out_shape=[jax.ShapeDtypeStruct((t, x), F32) for x in widths],
        scratch_shapes=[pltpu.VMEM((w, HEAD_DIM), F32)],
        compiler_params=_params(("arbitrary",)),
    )(*prep, s_all, do)


QK_DIM = 2 * HEAD_DIM
ATT_TILE = 1024
NEG = -1e30


ATT_SPLIT = 4


def _chunk_mask(n_rows, n_cols, key_major, query_offset):
    r = lax.broadcasted_iota(jnp.int32, (n_rows, n_cols), 0)
    c = lax.broadcasted_iota(jnp.int32, (n_rows, n_cols), 1)
    if key_major:
        return r // CHUNK <= (c + query_offset) // CHUNK
    return c // CHUNK <= (r + query_offset) // CHUNK


def _visible_keys(tile, diagonal):
    hq = tile // ATT_SPLIT
    return [(a + 1) * hq if diagonal else tile for a in range(ATT_SPLIT)]


def _dot_nn(a, b):
    return lax.dot_general(a, b, NN, preferred_element_type=F32)


def _blocked_transpose(x, width):
    t = x.shape[0]
    tile = min(ATT_TILE, t)
    return x.reshape(t // tile, tile, N_HEADS * width).transpose(0, 2, 1).reshape(t // tile, N_HEADS, width, tile)


def _attn_fwd(q, kt, v1, *, name):
    t = q.shape[0]
    tq = min(ATT_TILE, t)
    nq = t // tq

    def body(q_ref, kt_ref, v_ref, o_ref, lse_ref, m_scr, acc_scr):
        qi = pl.program_id(1)
        m_scr[...] = jnp.full_like(m_scr, NEG)
        acc_scr[...] = jnp.zeros_like(acc_scr)
        hq = tq // ATT_SPLIT
        parts = [pl.ds(a * hq, hq) for a in range(ATT_SPLIT)]
        qs = [q_ref[sl, :] for sl in parts]

        def step(kj, masked):
            rows = pl.ds(pl.multiple_of(kj * tq, tq), tq)
            kt_blk, vv = kt_ref[kj], v_ref[rows, :]
            seen = _visible_keys(tq, masked)
            ss = [_dot_nn(qv, kt_blk[:, :w]) for qv, w in zip(qs, seen)]
            for a, sl in enumerate(parts):
                s = ss[a]
                if masked:
                    s = jnp.where(_chunk_mask(hq, seen[a], False, a * hq), s, NEG)
                m_old = m_scr[sl, :]
                m_new = jnp.maximum(m_old, jnp.max(s, axis=1, keepdims=True))
                p = jnp.exp(s - m_new)
                acc_scr[sl, :] = jnp.exp(m_old - m_new) * acc_scr[sl, :] + _dot_nn(p.astype(BF16), vv[:seen[a], :])
                m_scr[sl, :] = m_new

        def loop_body(kj, carry):
            step(kj, False)
            return carry

        lax.fori_loop(0, qi, loop_body, 0)
        step(qi, True)
        acc = acc_scr[...]
        o_ref[...] = (acc[:, :HEAD_DIM] / acc[:, HEAD_DIM:]).astype(o_ref.dtype)
        lse_ref[...] = m_scr[...] + jnp.log(acc[:, HEAD_DIM:HEAD_DIM + 1])

    return pl.pallas_call(
        body, name=name, grid=(N_HEADS, nq),
        in_specs=[pl.BlockSpec((tq, QK_DIM), lambda h, i: (i, h)),
                  pl.BlockSpec((nq, None, QK_DIM, tq), lambda h, i: (0, h, 0, 0)),
                  pl.BlockSpec((t, 2 * HEAD_DIM), lambda h, i: (0, h))],
        out_specs=[pl.BlockSpec((tq, HEAD_DIM), lambda h, i: (i, h)),
                   pl.BlockSpec((None, tq, 1), lambda h, i: (h, i, 0))],
        out_shape=[jax.ShapeDtypeStruct((t, N_HEADS * HEAD_DIM), BF16),
                   jax.ShapeDtypeStruct((N_HEADS, t, 1), F32)],
        scratch_shapes=[pltpu.VMEM((tq, 1), F32), pltpu.VMEM((tq, 2 * HEAD_DIM), F32)],
        compiler_params=_params(("arbitrary", "arbitrary")),
    )(q, kt, v1)


def _attn_delta(dom, o, *, name):
    hw = o.shape[1]

    def fn(do, ov):
        lane = _lane((do.shape[0], LANES))
        out = jnp.zeros((do.shape[0], LANES), F32)
        for h, (a, b) in enumerate(zip(_heads(do), _heads(ov))):
            out = out + jnp.where(lane == h, jnp.sum(a * b, axis=1, keepdims=True), 0.0)
        return (out,)

    return _rowwise(fn, [(dom, hw, 1), (o, hw, 0)], [], [(LANES, F32)], name=name)[0]


def _attn_bwd(q, qt, k, v, lse_row, delta_row, do, dot, *, name):
    t = q.shape[0]
    tk = min(ATT_TILE, t)
    nk = t // tk

    def body(q_ref, qt_ref, k_ref, v_ref, lse_ref, delta_ref, do_ref, dot_ref, dk_ref, dv_ref, dq_ref, dk_scr, dv_scr):
        kj = pl.program_id(1)

        @pl.when(kj == 0)
        def _():
            dq_ref[...] = jnp.zeros_like(dq_ref)

        dk_scr[...] = jnp.zeros_like(dk_scr)
        dv_scr[...] = jnp.zeros_like(dv_scr)
        kv_ = k_ref[...]
        vv = v_ref[...]
        hq = tk // ATT_SPLIT

        def step(qi, masked):
            lse_v, delta_v = lse_ref[qi], delta_ref[qi]
            qt_blk, dot_blk = qt_ref[qi], dot_ref[qi]
            rows = [pl.ds(pl.multiple_of(qi * tk + a * hq, hq), hq) for a in range(ATT_SPLIT)]
            qs = [q_ref[r, :] for r in rows]
            dos = [do_ref[r, :] for r in rows]
            seen = _visible_keys(tk, masked)
            ss = [_dot_nn(kv_[:seen[a], :], qt_blk[:, a * hq:(a + 1) * hq]) for a in range(ATT_SPLIT)]
            dps = [_dot_nn(vv[:seen[a], :], dot_blk[:, a * hq:(a + 1) * hq]) for a in range(ATT_SPLIT)]
            for a in range(ATT_SPLIT):
                cols = slice(a * hq, (a + 1) * hq)
                keys = pl.ds(0, seen[a])
                p = jnp.exp(ss[a] - lse_v[:, cols])
                if masked:
                    p = jnp.where(_chunk_mask(seen[a], hq, True, a * hq), p, 0.0)
                dv_scr[keys, :] += _dot_nn(p.astype(BF16), dos[a])
                ds = (p * (dps[a] - delta_v[:, cols])).astype(BF16)
                dk_scr[keys, :] += _dot_nn(ds, qs[a])
                dq_ref[rows[a], :] += lax.dot_general(ds, kv_[:seen[a], :], TN, preferred_element_type=F32)

        step(kj, True)

        def loop_body(qi, carry):
            step(qi, False)
            return carry

        lax.fori_loop(kj + 1, nk, loop_body, 0)
        dk_ref[...] = dk_scr[...].astype(dk_ref.dtype)
        dv_ref[...] = dv_scr[...].astype(dv_ref.dtype)

    once = dict(pipeline_mode=pl.Buffered(1))
    stat = pl.BlockSpec((None, nk, 1, tk), lambda h, j: (h, 0, 0, 0))
    return pl.pallas_call(
        body, name=name, grid=(N_HEADS, nk),
        in_specs=[pl.BlockSpec((t, QK_DIM), lambda h, j: (0, h), **once),
                  pl.BlockSpec((nk, None, QK_DIM, tk), lambda h, j: (0, h, 0, 0), **once),
                  pl.BlockSpec((tk, QK_DIM), lambda h, j: (j, h)),
                  pl.BlockSpec((tk, HEAD_DIM), lambda h, j: (j, h)),
                  stat, stat,
                  pl.BlockSpec((t, HEAD_DIM), lambda h, j: (0, h), **once),
                  pl.BlockSpec((nk, None, HEAD_DIM, tk), lambda h, j: (0, h, 0, 0), **once)],
        out_specs=[pl.BlockSpec((tk, QK_DIM), lambda h, j: (j, h)),
                   pl.BlockSpec((tk, HEAD_DIM), lambda h, j: (j, h)),
                   pl.BlockSpec((t, QK_DIM), lambda h, j: (0, h))],
        out_shape=[jax.ShapeDtypeStruct((t, N_HEADS * QK_DIM), BF16),
                   jax.ShapeDtypeStruct((t, N_HEADS * HEAD_DIM), BF16),
                   jax.ShapeDtypeStruct((t, N_HEADS * QK_DIM), F32)],
        scratch_shapes=[pltpu.VMEM((tk, QK_DIM), F32), pltpu.VMEM((tk, HEAD_DIM), F32)],
        compiler_params=_params(("arbitrary", "arbitrary")),
    )(q, qt, k, v, lse_row, delta_row, do, dot)


def _rope_tables(pos_col, inv_freq_row, *, name):
    t = pos_col.shape[0]
    tm = min(ROW_TILE, t)

    def body(p_ref, f_ref, c_ref, s_ref):
        ang = p_ref[...].astype(F32) * f_ref[...]
        lane = _lane(ang.shape)
        c_ref[...] = jnp.where(lane < ROPE_DIM, jnp.cos(ang), 0.0)
        sn = jnp.sin(ang)
        s_ref[...] = jnp.where(lane < ROPE_DIM // 2, -sn, jnp.where(lane < ROPE_DIM, sn, 0.0))

    out = pl.BlockSpec((tm, LANES), lambda i: (i, 0))
    return pl.pallas_call(
        body, name=name, grid=(t // tm,),
        in_specs=[pl.BlockSpec((tm, 1), lambda i: (i, 0)), pl.BlockSpec((1, LANES), lambda i: (0, 0))],
        out_specs=[out, out], out_shape=[jax.ShapeDtypeStruct((t, LANES), F32)] * 2,
        compiler_params=_params(("arbitrary",)),
    )(pos_col, inv_freq_row)


def _loss_head(y, target):
    width = y.shape[1]

    def fn(yv, tv):
        e = yv - tv
        part = 0.5 * jnp.sum(jnp.mean(e * e, axis=1, keepdims=True), axis=0, keepdims=True)
        return e * (1.0 / width), jnp.broadcast_to(part, (HALO, LANES))

    return _rowwise(fn, [(y, width, 0), (target, width, 0)], [], [(width, F32)], [(HALO, LANES)], name="loss_head")


def _adamw(w, g, m, v, *, name):
    shape = w.shape
    w2, g2, m2, v2 = (a.reshape(-1, shape[-1]) for a in (w, g, m, v))
    rows, width = w2.shape
    tr = _divisor_tile(rows, max(8, (1 << 19) // max(width, 1)), 8)
    bc1 = 1.0 - ADAM_B1 ** ADAM_STEP
    bc2 = 1.0 - ADAM_B2 ** ADAM_STEP

    def body(w_ref, g_ref, m_ref, v_ref, d_ref, mo_ref, vo_ref):
        gv = g_ref[...]
        mn = ADAM_B1 * m_ref[...] + (1.0 - ADAM_B1) * gv
        vn = ADAM_B2 * v_ref[...] + (1.0 - ADAM_B2) * (gv * gv)
        d_ref[...] = -ADAM_LR * ((mn / bc1) / (jnp.sqrt(vn / bc2) + ADAM_EPS) + ADAM_WD * w_ref[...])
        mo_ref[...] = mn
        vo_ref[...] = vn

    spec = pl.BlockSpec((tr, width), lambda i: (i, 0))
    outs = pl.pallas_call(
        body, name=name, grid=(rows // tr,), in_specs=[spec] * 4, out_specs=[spec] * 3,
        out_shape=[jax.ShapeDtypeStruct((rows, width), F32)] * 3,
        compiler_params=_params(("arbitrary",)),
    )(w2, g2, m2, v2)
    return tuple(o.reshape(shape) for o in outs)


HBM_SPEC = pl.BlockSpec(memory_space=pltpu.HBM)


def _position():
    return lax.axis_index("x"), lax.axis_index("y"), lax.axis_index("c")


def _other_chips(x, y):
    return [(1 - x, y), (x, 1 - y), (1 - x, 1 - y)]


class _Stream:
    def __init__(self, kind, size=0):
        self.kind, self.size = kind, size
        self.parts = 2 if kind == "heads" else 1

    def local(self, ref, k, part):
        if self.kind == "rows":
            return ref.at[:, pl.ds(k * self.size, self.size), :]
        if self.kind == "cols":
            return ref.at[:, :, pl.ds(k * self.size, self.size)]
        if self.kind == "heads":
            return ref.at[:, :, pl.ds(part * N_HEADS * HEAD_DIM + k * HEAD_DIM, HEAD_DIM)]
        if self.kind == "piece":
            return ref.at[k]
        return ref

    def shard(self, ref, part):
        if self.kind == "heads":
            return ref.at[:, :, pl.ds(part * HEAD_DIM, HEAD_DIM)]
        return ref

    def half_local(self, ref, k, part, cc, hd):
        if self.kind == "piece":
            return ref.at[k, pl.ds(cc * hd, hd)]
        return self.local(ref.at[pl.ds(cc * hd, hd)], k, part)


def _remote(src, dst, send_sems, recv_sems, idx, to):
    return pltpu.make_async_remote_copy(src_ref=src, dst_ref=dst, send_sem=send_sems.at[idx],
                                        recv_sem=recv_sems.at[idx], device_id=to, device_id_type=MESH)


def _comm_call(body, ins, out_shapes, n_remote, n_local, *, name):
    scratch = [pltpu.SemaphoreType.DMA((n_remote,)), pltpu.SemaphoreType.DMA((n_remote,))]
    if n_local:
        scratch.append(pltpu.SemaphoreType.DMA((n_local,)))
    return pl.pallas_call(
        body, name=name, in_specs=[HBM_SPEC] * len(ins), out_specs=[HBM_SPEC] * len(out_shapes), out_shape=out_shapes,
        scratch_shapes=scratch, compiler_params=pltpu.CompilerParams(has_side_effects=True),
    )(*ins)


def _gather_chips(shards, streams, out_shapes, *, name):
    n = len(shards)
    hd = shards[0].shape[0] // 2
    flat = [(t, part) for t in range(n) for part in range(streams[t].parts)]
    ns = len(flat)

    def body(*refs):
        s_refs, o_refs = refs[:n], refs[n:2 * n]
        send_sems, recv_sems = refs[2 * n:]
        x, y, c = _position()
        sibling = (x, y, 1 - c)
        chips = _other_chips(x, y)
        me = 2 * x + y
        sent = []
        for s, (t, part) in enumerate(flat):
            st = streams[t]
            sent.append(_remote(st.shard(s_refs[t], part), st.local(o_refs[t], me, part), send_sems, recv_sems,
                                6 * ns + s, sibling))
            sent[-1].start()
            src = st.shard(s_refs[t].at[pl.ds(c * hd, hd)], part)
            for j, (cx, cy) in enumerate(chips[:2]):
                sent.append(_remote(src, st.half_local(o_refs[t], me, part, c, hd), send_sems, recv_sems,
                                    3 * s + j, (cx, cy, c)))
                sent[-1].start()
        for s, (t, part) in enumerate(flat):
            st = streams[t]
            for j, (cx, cy) in enumerate(chips[:2]):
                blk = st.half_local(o_refs[t], 2 * cx + cy, part, c, hd)
                _remote(blk, blk, send_sems, recv_sems, 3 * s + j, (x, y, c)).wait_recv()
                sent.append(_remote(blk, blk, send_sems, recv_sems, 3 * ns + 3 * s + j, sibling))
                sent[-1].start()
                if s % 2 == j:
                    ox, oy = chips[1 - j]
                    sent.append(_remote(blk, blk, send_sems, recv_sems, 3 * s + 2, (ox, oy, c)))
                    sent[-1].start()
        for s, (t, part) in enumerate(flat):
            st = streams[t]
            cx, cy = chips[2]
            blk = st.half_local(o_refs[t], 2 * cx + cy, part, c, hd)
            _remote(blk, blk, send_sems, recv_sems, 3 * s + 2, (x, y, c)).wait_recv()
            sent.append(_remote(blk, blk, send_sems, recv_sems, 3 * ns + 3 * s + 2, sibling))
            sent[-1].start()
        for s, (t, part) in enumerate(flat):
            st = streams[t]
            for j, (cx, cy) in enumerate(chips):
                blk = st.half_local(o_refs[t], 2 * cx + cy, part, 1 - c, hd)
                _remote(blk, blk, send_sems, recv_sems, 3 * ns + 3 * s + j, (x, y, c)).wait_recv()
            own = st.local(o_refs[t], me, part)
            _remote(own, own, send_sems, recv_sems, 6 * ns + s, (x, y, c)).wait_recv()
        for cp in sent:
            cp.wait_send()

    return _comm_call(body, shards, out_shapes, 7 * ns, 0, name=name)


def _sibling_take_other_half(gs, *, name):
    n = len(gs)
    hd = gs[0].shape[0] // 2

    def body(*refs):
        g_refs, o_refs = refs[:n], refs[n:2 * n]
        send_sems, recv_sems = refs[2 * n:]
        x, y, c = _position()
        copies = [_remote(g_refs[t].at[pl.ds((1 - c) * hd, hd)], o_refs[t], send_sems, recv_sems, t, (x, y, 1 - c))
                  for t in range(n)]
        for cp in copies:
            cp.start()
        for cp in copies:
            cp.wait()

    outs = [jax.ShapeDtypeStruct((hd,) + g.shape[1:], g.dtype) for g in gs]
    return _comm_call(body, gs, outs, n, 0, name=name)


def _chips_exchange(ps, streams, shard_shapes, *, name):
    n = len(ps)
    flat = [(t, part) for t in range(n) for part in range(streams[t].parts)]

    def body(*refs):
        p_refs, o_refs = refs[:n], refs[n:2 * n]
        send_sems, recv_sems = refs[2 * n:]
        x, y, c = _position()
        copies = []
        for s, (t, part) in enumerate(flat):
            st = streams[t]
            for j, (cx, cy) in enumerate(_other_chips(x, y)):
                copies.append(_remote(st.local(p_refs[t], 2 * cx + cy, part), st.shard(o_refs[t].at[j], part),
                                      send_sems, recv_sems, 3 * s + j, (cx, cy, c)))
        for cp in copies:
            cp.start()
        for cp in copies:
            cp.wait()

    outs = [jax.ShapeDtypeStruct((3,) + tuple(shp), p.dtype) for p, shp in zip(ps, shard_shapes)]
    return _comm_call(body, ps, outs, 3 * len(flat), 0, name=name)


def _sibling_join_halves(bufs, *, name):
    n = len(bufs)
    hd = bufs[0].shape[0] // 2

    def body(*refs):
        o_refs = refs[n:2 * n]
        send_sems, recv_sems = refs[2 * n:]
        x, y, c = _position()
        sent = []
        for t in range(n):
            mine = o_refs[t].at[pl.ds(c * hd, hd)]
            sent.append(_remote(mine, mine, send_sems, recv_sems, t, (x, y, 1 - c)))
            sent[-1].start()
        for t in range(n):
            theirs = o_refs[t].at[pl.ds((1 - c) * hd, hd)]
            _remote(theirs, theirs, send_sems, recv_sems, t, (x, y, c)).wait_recv()
        for cp in sent:
            cp.wait_send()

    return pl.pallas_call(
        body, name=name, in_specs=[HBM_SPEC] * n, out_specs=[HBM_SPEC] * n,
        out_shape=[jax.ShapeDtypeStruct(b.shape, b.dtype) for b in bufs],
        scratch_shapes=[pltpu.SemaphoreType.DMA((n,)), pltpu.SemaphoreType.DMA((n,))],
        input_output_aliases={t: t for t in range(n)},
        compiler_params=pltpu.CompilerParams(has_side_effects=True),
    )(*bufs)


def _row_tile(rows, width):
    return _divisor_tile(rows, max(16, (1 << 19) // width), 16)


def _add_own_half(g, got, c_idx, out_dtype, *, name):
    hd, r, w = got.shape
    tr = _row_tile(r, w)

    def body(c_ref, g_ref, a_ref, o_ref):
        o_ref[...] = (g_ref[...] + a_ref[...]).astype(o_ref.dtype)

    return pl.pallas_call(
        body, name=name,
        grid_spec=pltpu.PrefetchScalarGridSpec(
            num_scalar_prefetch=1, grid=(hd, r // tr),
            in_specs=[pl.BlockSpec((None, None, tr, w), lambda l, i, c_ref: (c_ref[0], l, i, 0)),
                      pl.BlockSpec((None, tr, w), lambda l, i, c_ref: (l, i, 0))],
            out_specs=pl.BlockSpec((None, tr, w), lambda l, i, c_ref: (l, i, 0))),
        out_shape=jax.ShapeDtypeStruct((hd, r, w), out_dtype),
        compiler_params=_params(("arbitrary", "arbitrary")),
    )(c_idx, g.reshape((2, hd) + g.shape[1:]), got)


def _sum_chips(p, got, place, stream, *, name):
    _, hd, rs, cs = got.shape
    wb = HEAD_DIM if stream.kind == "heads" else cs
    tr = _row_tile(rs, wb)
    kind, size = stream.kind, stream.size

    def own_index(l, i, g, k_ref, c_ref):
        k = k_ref[0]
        if kind == "rows":
            return (l, k * (size // tr) + i, 0)
        if kind == "cols":
            return (l, i, k)
        if kind == "heads":
            return (l, i, g * N_HEADS + k)
        if kind == "piece":
            return (k, l, i, 0)
        return (l, i, 0)

    own_blk = (None, None, tr, wb) if kind == "piece" else (None, tr, wb)

    def body(k_ref, c_ref, p_ref, fx_ref, fy_ref, fxy_ref, o_ref):
        f = lambda r: r[...].astype(F32)
        o_ref[...] = (f(p_ref) + f(fy_ref)) + (f(fx_ref) + f(fxy_ref))

    def rel(j):
        return pl.BlockSpec((None, None, tr, wb), functools.partial(lambda l, i, g, k_ref, c_ref, j: (j, l, i, g), j=j))

    return pl.pallas_call(
        body, name=name,
        grid_spec=pltpu.PrefetchScalarGridSpec(
            num_scalar_prefetch=2, grid=(hd, rs // tr, stream.parts),
            in_specs=[pl.BlockSpec(own_blk, own_index), rel(0), rel(1), rel(2)],
            out_specs=pl.BlockSpec((None, tr, wb), lambda l, i, g, k_ref, c_ref: (c_ref[0] * hd + l, i, g))),
        out_shape=jax.ShapeDtypeStruct((2 * hd, rs, cs), F32),
        compiler_params=_params(("arbitrary", "arbitrary", "arbitrary")),
    )(place[0], place[1], p, got, got, got)


MATRICES = ("w_in", "w_uq", "w_ukv", "w_out", "w_gate_up", "w_down", "w_ple", "w_ple_gate", "conv_w")
VECTORS = ("a_log", "dt_bias", "gdn_norm_g", "q_norm_g", "kv_norm_g", "ln1_g", "ln1_b", "ln2_g", "ln2_b")
WEIGHTS = ("w_in", "conv_w", "a_log", "dt_bias", "gdn_norm_g", "q_norm_g", "w_uq", "kv_norm_g", "w_ukv", "w_out",
           "ln1_g", "ln1_b", "w_gate_up", "w_down", "ln2_g", "ln2_b", "w_ple", "w_ple_gate")
ROW_SHARDED = ("w_out", "w_down", "w_ple_gate")
N_CHIPS = 4


def _stream_of(name, shard_shape):
    if name in ("w_in", "w_uq"):
        return _Stream("piece")
    if name == "w_ukv":
        return _Stream("heads")
    if name in ROW_SHARDED:
        return _Stream("rows", shard_shape[0])
    return _Stream("cols", shard_shape[1])


def _pack_vectors(vecs, depth):
    flat = jnp.concatenate([vecs[n].reshape(depth, -1) for n in VECTORS], axis=1)
    pad = jnp.zeros((depth, VEC_ROWS * LANES - flat.shape[1]), F32)
    return jnp.concatenate([flat, pad], axis=1).reshape(depth, VEC_ROWS, LANES)


def _unpack_vectors(packed, shapes):
    depth = packed.shape[0]
    flat = packed.reshape(depth, VEC_ROWS * LANES)
    out, off = {}, 0
    for n in VECTORS:
        out[n] = flat[:, off:off + shapes[n][1]]
        off += shapes[n][1]
    return out


VEC_ROWS = 40


class _Dims:
    def __init__(self, d_model, in_width, q_lora, kv_lora, d_ff2, ple_dim):
        self.d = d_model
        self.hw = N_HEADS * HEAD_DIM
        self.in_width = in_width
        self.q_lora, self.kv_lora = q_lora, kv_lora
        self.ff2 = d_ff2
        self.ple = ple_dim
        self.c_kv0 = 4 * self.hw
        self.c_q0 = self.c_kv0 + kv_lora
        self.misc0 = self.c_q0 + q_lora
        self.h_width = self.misc0 + LANES
        assert self.c_kv0 % kv_lora == 0 and self.c_q0 % q_lora == 0 and self.misc0 % LANES == 0
        self.g_beta = 4 * self.hw
        self.g_a = self.g_beta + N_HEADS
        self.g_cq = self.g_a + N_HEADS
        self.g_ckv = self.g_cq + q_lora
        self.g_kr = self.g_ckv + kv_lora
        assert self.g_kr + ROPE_DIM == in_width

    def w_in_local(self, w):
        pad = jnp.zeros(w.shape[:-1] + (self.h_width - self.in_width,), w.dtype)
        return jnp.concatenate([w[..., :self.g_beta], w[..., self.g_ckv:self.g_kr], w[..., self.g_cq:self.g_ckv],
                                w[..., self.g_kr:], w[..., self.g_beta:self.g_cq], pad], axis=-1)

    def w_in_global(self, d):
        m = self.misc0
        return jnp.concatenate([d[..., :self.c_kv0], d[..., m + MISC_BETA0:m + MISC_A0 + N_HEADS],
                                d[..., self.c_q0:self.misc0], d[..., self.c_kv0:self.c_q0], d[..., m:m + ROPE_DIM]],
                               axis=-1)

    def w_uq_local(self, w):
        r = w.reshape(w.shape[:-1] + (N_HEADS, HEAD_DIM + ROPE_DIM))
        r = jnp.pad(r, [(0, 0)] * (r.ndim - 1) + [(0, QK_DIM - HEAD_DIM - ROPE_DIM)])
        return r.reshape(w.shape[:-1] + (N_HEADS * QK_DIM,))

    def w_uq_global(self, d):
        r = d.reshape(d.shape[:-1] + (N_HEADS, QK_DIM))[..., :HEAD_DIM + ROPE_DIM]
        return r.reshape(d.shape[:-1] + (N_HEADS * (HEAD_DIM + ROPE_DIM),))


def _lane_padded(n):
    return -(-n // LANES) * LANES


def _pad_lanes(a):
    pad = _lane_padded(a.shape[-1]) - a.shape[-1]
    return a if pad == 0 else jnp.pad(a, [(0, 0)] * (a.ndim - 1) + [(0, pad)])


def _lane_row(vec, lane0):
    pad = LANES - lane0 - vec.shape[0]
    return jnp.concatenate([jnp.zeros((lane0,), F32), vec.astype(F32), jnp.zeros((pad,), F32)])[None, :]


def _layer_fwd(dm, alpha, x, xb, p_i, cos_t, sin_t, wl, tag):
    d, hw = dm.d, dm.hw
    nm = lambda s: f"{s}_{tag}"
    mm = functools.partial(_matmul, layer=wl["layer"])
    h = mm(xb, wl["w_in"], dims="nn", name=nm("f_in"))
    misc_cb = dm.misc0 // LANES

    u = _conv_fwd(h, wl["conv_w"], 3 * hw, name=nm("f_conv"))
    qn, kn, vg, gb = _rowwise(_gdn_act, [(u, 3 * hw, 0), (h, LANES, misc_cb)], [wl["alog_row"], wl["dtb_row"]],
                              [(hw, F32), (hw, F32), (hw, F32), (LANES, F32)], name=nm("f_gdn_act"))
    prep, tinv = _gdn_prep_fwd(qn, kn, vg, gb, name=nm("f_gdn_prep"))
    o_gdn, s_all = _gdn_seq_fwd(prep, name=nm("f_gdn_seq"))
    (og,) = _rowwise(lambda o, z, g: (_gdn_out(o, z, g),), [(o_gdn, hw, 0), (h, hw, 3)], [wl["gn_row"]],
                     [(hw, BF16)], name=nm("f_gdn_out"))

    cqn, ckvn = _rowwise(_mla_norm, [(h, dm.kv_lora, dm.c_kv0 // dm.kv_lora), (h, dm.q_lora, dm.c_q0 // dm.q_lora)],
                         [wl["kvg_row"], wl["qg_row"]], [(dm.q_lora, BF16), (dm.kv_lora, BF16)], name=nm("f_mla_norm"))
    qm = mm(cqn, wl["w_uq"], dims="nn", name=nm("f_uq"))
    kvm = mm(ckvn, wl["w_ukv"], dims="nn", name=nm("f_ukv"))
    scale = (HEAD_DIM + ROPE_DIM) ** -0.5
    qk_fn = functools.partial(_mla_qk, scale)
    qa, ka, va = _rowwise(qk_fn, [(qm, N_HEADS * QK_DIM, 0), (kvm, 2 * hw, 0), (h, LANES, misc_cb),
                                  (cos_t, LANES, 0), (sin_t, LANES, 0)], [],
                          [(N_HEADS * QK_DIM, BF16), (N_HEADS * QK_DIM, BF16), (hw, BF16)], name=nm("f_mla_qk"))
    kt = _blocked_transpose(ka, QK_DIM)
    ones = jnp.ones((va.shape[0], HEAD_DIM), va.dtype)
    v1 = jnp.concatenate([part for vh in _heads(va) for part in (vh, ones)], axis=1)
    o_mla, lse = _attn_fwd(qa, kt, v1, name=nm("f_attn"))

    om = jnp.concatenate([og, o_mla], axis=1)
    mix = mm(om, wl["w_out"], dims="nn", name=nm("f_out"))
    ln1 = lambda xv, yv, g, b: (_layer_norm(alpha * xv + yv, g, b),) * 2
    x1, x1b = _rowwise(ln1, [(x, d, 0), (mix, d, 0)], [wl["ln1_g"], wl["ln1_b"]], [(d, F32), (d, BF16)], name=nm("f_ln1"))

    gu = mm(x1b, wl["w_gate_up"], dims="nn", name=nm("f_gate_up"), out_dtype=BF16)
    dn, act = mm(gu, wl["w_down"], dims="nn", name=nm("f_down"), a_gated=True, tm=GATED_TILE)
    x2, x2b = _rowwise(ln1, [(x1, d, 0), (dn, d, 0)], [wl["ln2_g"], wl["ln2_b"]], [(d, F32), (d, BF16)], name=nm("f_ln2"))

    pg = mm(x2b, wl["w_ple_gate"], dims="nn", name=nm("f_ple_gate"))
    pe = mm(p_i, wl["w_ple"], dims="nn", name=nm("f_ple"))
    out, outb = _rowwise(lambda a, b, c_: (_ple_out(a, b, c_),) * 2, [(x2, d, 0), (pg, d, 0), (pe, d, 0)], [],
                         [(d, F32), (d, BF16)], name=nm("f_ple_out"))
    saved = dict(x=x, xb=xb, p_i=p_i, h=h, u=u, qn=qn, kn=kn, vg=vg, gb=gb, prep=prep, tinv=tinv, s_all=s_all, o_gdn=o_gdn, cqn=cqn, ckvn=ckvn,
                 qm=qm, kvm=kvm, qa=qa, ka=ka, va=va, o_mla=o_mla, lse=lse, om=om, mix=mix, x1=x1, x1b=x1b, gu=gu,
                 act=act, dn=dn, x2=x2, x2b=x2b, pg=pg, pe=pe)
    return out, outb, saved


def _layer_bwd(dm, alpha, dout, sv, cos_t, sin_t, wl, gbuf, tag):
    d, hw = dm.d, dm.hw
    t = dout.shape[0]
    nm = lambda s: f"{s}_{tag}"
    gr = {}
    gbuf = dict(gbuf)
    misc_cb = dm.misc0 // LANES
    mm = functools.partial(_matmul, layer=wl["layer"])

    def wgrad(name_, a, g):
        gbuf[name_] = mm(a, g, dims="tn", name=nm("b_" + name_), into=gbuf[name_], tm=1408, tn=1408, tk=1024)

    dx2_a, dpg, dpe = _rowwise(_vjp_fn(_ple_out, 3, 1), [(sv["x2"], d, 0), (sv["pg"], d, 0), (sv["pe"], d, 0), (dout, d, 0)],
                               [], [(d, F32), (d, BF16), (d, BF16)], name=nm("b_ple_out"))
    wgrad("w_ple", sv["p_i"], dpe)
    wgrad("w_ple_gate", sv["x2b"], dpg)
    dx2 = mm(dpg, wl["w_ple_gate"], dims="nt", c=dx2_a, name=nm("b_x2"))

    def ln_bwd(xv, yv, ct, g, b):
        _, pull = jax.vjp(lambda a_, b_, c_, d_: _layer_norm(alpha * a_ + b_, c_, d_), xv, yv, g, b)
        return pull(ct)

    dx1_a, ddn, gr["ln2_g"], gr["ln2_b"] = _rowwise(
        ln_bwd, [(sv["x1"], d, 0), (sv["dn"], d, 0), (dx2, d, 0)], [wl["ln2_g"], wl["ln2_b"]],
        [(d, F32), (d, BF16)], [(1, d), (1, d)], name=nm("b_ln2"))
    wgrad("w_down", sv["act"], ddn)
    dact = mm(ddn, wl["w_down"], dims="nt", name=nm("b_act"), out_dtype=BF16)
    (dgu,) = _rowwise(_vjp_fn(_swiglu, 1, 1), [(sv["gu"], dm.ff2, 0), (dact, dm.ff2 // 2, 0)], [], [(dm.ff2, BF16)],
                      name=nm("b_swiglu"))
    wgrad("w_gate_up", sv["x1b"], dgu)
    dx1 = mm(dgu, wl["w_gate_up"], dims="nt", c=dx1_a, name=nm("b_x1"))

    dx_a, dmix, gr["ln1_g"], gr["ln1_b"] = _rowwise(
        ln_bwd, [(sv["x"], d, 0), (sv["mix"], d, 0), (dx1, d, 0)], [wl["ln1_g"], wl["ln1_b"]],
        [(d, F32), (d, BF16)], [(1, d), (1, d)], name=nm("b_ln1"))
    wgrad("w_out", sv["om"], dmix)
    dom = mm(dmix, wl["w_out"], dims="nt", name=nm("b_om"))

    nq = t // min(ATT_TILE, t)
    delta = _attn_delta(dom, sv["o_mla"], name=nm("b_attn_delta"))
    lse_row = sv["lse"].reshape(N_HEADS, nq, 1, t // nq)
    delta_row = delta[:, :N_HEADS].T.reshape(N_HEADS, nq, 1, t // nq)
    do_b = dom[:, hw:].astype(BF16)
    dka, dva, dqa = _attn_bwd(sv["qa"], _blocked_transpose(sv["qa"], QK_DIM), sv["ka"], sv["va"], lse_row, delta_row,
                              do_b, _blocked_transpose(do_b, HEAD_DIM), name=nm("b_attn"))
    scale = (HEAD_DIM + ROPE_DIM) ** -0.5
    qk_fn = functools.partial(_mla_qk, scale)

    def qk_bwd(qm, kvm, misc, cs, sn, g_q, g_k, g_v):
        _, pull = jax.vjp(lambda a, b, c_: qk_fn(a, b, c_, cs, sn), qm, kvm, misc)
        return pull((g_q, g_k, g_v))

    dqm, dkvm, dmisc_rope = _rowwise(
        qk_bwd, [(sv["qm"], N_HEADS * QK_DIM, 0), (sv["kvm"], 2 * hw, 0), (sv["h"], LANES, misc_cb), (cos_t, LANES, 0),
                 (sin_t, LANES, 0), (dqa, N_HEADS * QK_DIM, 0), (dka, N_HEADS * QK_DIM, 0), (dva, hw, 0)], [],
        [(N_HEADS * QK_DIM, BF16), (2 * hw, BF16), (LANES, F32)], name=nm("b_mla_qk"))
    wgrad("w_uq", sv["cqn"], dqm)
    wgrad("w_ukv", sv["ckvn"], dkvm)
    dcqn = mm(dqm, wl["w_uq"], dims="nt", name=nm("b_cqn"))
    dckvn = mm(dkvm, wl["w_ukv"], dims="nt", name=nm("b_ckvn"))

    def norm_bwd(ckv, cq, g_q, g_kv, kvg, qg):
        _, pull = jax.vjp(_mla_norm, ckv, cq, kvg, qg)
        return pull((g_q, g_kv))

    dckv, dcq, gr["kvg_row"], gr["qg_row"] = _rowwise(
        norm_bwd, [(sv["h"], dm.kv_lora, dm.c_kv0 // dm.kv_lora), (sv["h"], dm.q_lora, dm.c_q0 // dm.q_lora),
                   (dcqn, dm.q_lora, 0), (dckvn, dm.kv_lora, 0)], [wl["kvg_row"], wl["qg_row"]],
        [(dm.kv_lora, BF16), (dm.q_lora, BF16)], [(1, dm.kv_lora), (1, dm.q_lora)], name=nm("b_mla_norm"))

    def gout_bwd(o, z, g_o, gn):
        _, pull = jax.vjp(_gdn_out, o, z, gn)
        return pull(g_o)

    do_gdn, dz, gr["gn_row"] = _rowwise(gout_bwd, [(sv["o_gdn"], hw, 0), (sv["h"], hw, 3), (dom, hw, 0)], [wl["gn_row"]],
                                        [(hw, F32), (hw, BF16)], [(1, HEAD_DIM)], name=nm("b_gdn_out"))
    dprep = _gdn_seq_bwd(sv["prep"], sv["s_all"], do_gdn, name=nm("b_gdn_seq"))
    dqn, dkn, dvg, dgb = _gdn_prep_bwd(sv["qn"], sv["kn"], sv["vg"], sv["gb"], sv["tinv"], dprep, name=nm("b_gdn_prep"))

    def act_bwd(u, misc, g_q, g_k, g_v, g_gb, g_rope, alog, dtb):
        _, pull = jax.vjp(_gdn_act, u, misc, alog, dtb)
        du_, dmisc_, dalog_, ddtb_ = pull((g_q, g_k, g_v, g_gb))
        return du_, dmisc_ + g_rope, dalog_, ddtb_

    du, dmisc, gr["alog_row"], gr["dtb_row"] = _rowwise(
        act_bwd, [(sv["u"], 3 * hw, 0), (sv["h"], LANES, misc_cb), (dqn, hw, 0), (dkn, hw, 0), (dvg, hw, 0),
                  (dgb, LANES, 0), (dmisc_rope, LANES, 0)], [wl["alog_row"], wl["dtb_row"]],
        [(3 * hw, F32), (LANES, BF16)], [(1, LANES), (1, LANES)], name=nm("b_gdn_act"))
    dqkv, dconv = _conv_bwd(du, sv["h"], wl["conv_w"], 3 * hw, name=nm("b_conv"))
    gr["conv_w"] = dconv[:CONV_W]

    dh = jnp.concatenate([dqkv, dz, dckv, dcq, dmisc], axis=1)
    wgrad("w_in", sv["xb"], dh)
    dx = mm(dh, wl["w_in"], dims="nt", c=dx_a, name=nm("b_x"), tk=1408)
    return dx, gbuf, gr


LOCAL_MATRICES = ("w_in", "w_uq", "w_ukv", "w_out", "w_gate_up", "w_down", "w_ple", "w_ple_gate")


def _layer_weights(mats, vecs, layer):
    wl = {n: mats[n] for n in LOCAL_MATRICES}
    wl["layer"] = layer
    wl["conv_w"] = mats["conv_w"][layer]
    wl["alog_row"] = _lane_row(vecs["a_log"][layer], MISC_A0)
    wl["dtb_row"] = _lane_row(vecs["dt_bias"][layer], MISC_A0)
    wl["gn_row"] = vecs["gdn_norm_g"][layer][None, :]
    wl["qg_row"] = vecs["q_norm_g"][layer][None, :]
    wl["kvg_row"] = vecs["kv_norm_g"][layer][None, :]
    for n in ("ln1_g", "ln1_b", "ln2_g", "ln2_b"):
        wl[n] = vecs[n][layer][None, :]
    return wl


def _vector_grads(gr):
    out = {"a_log": gr["alog_row"][0, MISC_A0:MISC_A0 + N_HEADS], "dt_bias": gr["dtb_row"][0, MISC_A0:MISC_A0 + N_HEADS],
           "gdn_norm_g": gr["gn_row"][0], "q_norm_g": gr["qg_row"][0], "kv_norm_g": gr["kvg_row"][0]}
    for n in ("ln1_g", "ln1_b", "ln2_g", "ln2_b"):
        out[n] = gr[n][0]
    return out


def _local_step(dm, x, p, positions, target, mats, vecs):
    depth = p.shape[0]
    alpha = (2.0 * depth) ** 0.25
    freq = ROPE_THETA ** (-jnp.arange(0, ROPE_DIM, 2, dtype=F32) / ROPE_DIM)
    inv_freq_row = _lane_row(jnp.concatenate([freq, freq]), 0)
    cos_t, sin_t = _rope_tables(positions.reshape(-1, 1), inv_freq_row, name="rope_tables")

    wls = [_layer_weights(mats, vecs, i) for i in range(depth)]
    saved = []
    cur, cur_b = x, x
    for i in range(depth):
        cur, cur_b, sv = _layer_fwd(dm, alpha, cur, cur_b, p[i], cos_t, sin_t, wls[i], f"l{i}")
        saved.append(sv)
    dy, loss_blk = _loss_head(cur, target)
    gbuf = {n: depth for n in LOCAL_MATRICES}
    conv_g, vec_g = [None] * depth, [None] * depth
    for i in reversed(range(depth)):
        dy, gbuf, gr = _layer_bwd(dm, alpha, dy, saved[i], cos_t, sin_t, wls[i], gbuf, f"l{i}")
        conv_g[i] = gr["conv_w"]
        vec_g[i] = _vector_grads(gr)
    vec_grads = {n: jnp.stack([vec_g[i][n] for i in range(depth)]) for n in VECTORS}
    return loss_blk[0, 0], dy, gbuf, jnp.stack(conv_g), vec_grads


def kernel(x, p, positions, w_in, conv_w, a_log, dt_bias, gdn_norm_g, q_norm_g, w_uq, kv_norm_g, w_ukv, w_out, ln1_g, ln1_b, w_gate_up, w_down, ln2_g, ln2_b, w_ple, w_ple_gate, loss_target, m_w_in, m_conv_w, m_a_log, m_dt_bias, m_gdn_norm_g, m_q_norm_g, m_w_uq, m_kv_norm_g, m_w_ukv, m_w_out, m_ln1_g, m_ln1_b, m_w_gate_up, m_w_down, m_ln2_g, m_ln2_b, m_w_ple, m_w_ple_gate, v_w_in, v_conv_w, v_a_log, v_dt_bias, v_gdn_norm_g, v_q_norm_g, v_w_uq, v_kv_norm_g, v_w_ukv, v_w_out, v_ln1_g, v_ln1_b, v_w_gate_up, v_w_down, v_ln2_g, v_ln2_b, v_w_ple, v_w_ple_gate):
    w = dict(w_in=w_in, conv_w=conv_w, a_log=a_log, dt_bias=dt_bias, gdn_norm_g=gdn_norm_g, q_norm_g=q_norm_g, w_uq=w_uq,
             kv_norm_g=kv_norm_g, w_ukv=w_ukv, w_out=w_out, ln1_g=ln1_g, ln1_b=ln1_b, w_gate_up=w_gate_up, w_down=w_down,
             ln2_g=ln2_g, ln2_b=ln2_b, w_ple=w_ple, w_ple_gate=w_ple_gate)
    m = dict(w_in=m_w_in, conv_w=m_conv_w, a_log=m_a_log, dt_bias=m_dt_bias, gdn_norm_g=m_gdn_norm_g, q_norm_g=m_q_norm_g,
             w_uq=m_w_uq, kv_norm_g=m_kv_norm_g, w_ukv=m_w_ukv, w_out=m_w_out, ln1_g=m_ln1_g, ln1_b=m_ln1_b,
             w_gate_up=m_w_gate_up, w_down=m_w_down, ln2_g=m_ln2_g, ln2_b=m_ln2_b, w_ple=m_w_ple, w_ple_gate=m_w_ple_gate)
    v = dict(w_in=v_w_in, conv_w=v_conv_w, a_log=v_a_log, dt_bias=v_dt_bias, gdn_norm_g=v_gdn_norm_g, q_norm_g=v_q_norm_g,
             w_uq=v_w_uq, kv_norm_g=v_kv_norm_g, w_ukv=v_w_ukv, w_out=v_w_out, ln1_g=v_ln1_g, ln1_b=v_ln1_b,
             w_gate_up=v_w_gate_up, w_down=v_w_down, ln2_g=v_ln2_g, ln2_b=v_ln2_b, w_ple=v_w_ple, w_ple_gate=v_w_ple_gate)
    depth = w_in.shape[0]
    assert depth % 2 == 0
    hd = depth // 2
    dm = _Dims(x.shape[2], N_CHIPS * w_in.shape[2], w_uq.shape[1], w_ukv.shape[1], N_CHIPS * w_gate_up.shape[2], p.shape[3])
    cx, cy, cc = lax.axis_index("x"), lax.axis_index("y"), lax.axis_index("c")
    chip = 2 * cx + cy

    g_streams = [_stream_of(n, w[n].shape[1:]) for n in MATRICES]
    shards = [w[n] if n == "conv_w" else w[n].astype(BF16) for n in MATRICES]
    shards = [_pad_lanes(s) if st.kind == "piece" else s for s, st in zip(shards, g_streams)]
    g_shapes = []
    for s, st in zip(shards, g_streams):
        if st.kind == "piece":
            shape = (N_CHIPS,) + s.shape
        elif st.kind == "rows":
            shape = (depth, N_CHIPS * s.shape[1], s.shape[2])
        else:
            shape = (depth, s.shape[1], N_CHIPS * s.shape[2])
        g_shapes.append(jax.ShapeDtypeStruct(shape, s.dtype))
    mats = dict(zip(MATRICES, _gather_chips(shards, g_streams, g_shapes, name="gather_weights")))
    for n, to_local in (("w_in", dm.w_in_local), ("w_uq", dm.w_uq_local)):
        pieces = jnp.moveaxis(mats[n][..., :w[n].shape[2]], 0, 2)
        mats[n] = to_local(pieces.reshape(pieces.shape[:2] + (-1,)))
    vecs = {n: w[n] for n in VECTORS}

    loss_local, grad_x, gbuf, conv_g, vec_g = _local_step(dm, x[0], p[:, 0], positions[0], loss_target[0], mats, vecs)
    loss = lax.psum(loss_local, ("x", "y", "c"))

    names = list(LOCAL_MATRICES) + ["conv_w", "vectors"]
    gs = [gbuf[n] for n in LOCAL_MATRICES] + [conv_g, _pack_vectors(vec_g, depth)]
    wire = [BF16] * len(LOCAL_MATRICES) + [F32, F32]
    r_streams = [_stream_of(n, w[n].shape[1:]) for n in LOCAL_MATRICES]
    r_streams += [_stream_of("conv_w", w["conv_w"].shape[1:]), _Stream("whole")]
    shard_shapes = [(hd, w[n].shape[1], _lane_padded(w[n].shape[2])) if st.kind == "piece" else (hd,) + w[n].shape[1:]
                    for n, st in zip(LOCAL_MATRICES, r_streams)]
    shard_shapes += [(hd,) + w["conv_w"].shape[1:], (hd, VEC_ROWS, LANES)]
    c_idx = cc.reshape(1).astype(jnp.int32)
    place = (chip.reshape(1).astype(jnp.int32), c_idx)
    from_sibling = _sibling_take_other_half(gs, name="reduce_sibling")
    chip_sum = [_add_own_half(g, a, c_idx, dt, name=f"reduce_add_{n}")
                for g, a, dt, n in zip(gs, from_sibling, wire, names)]
    for i, n in enumerate(names):
        if r_streams[i].kind == "piece":
            glob = dm.w_in_global(chip_sum[i]) if n == "w_in" else dm.w_uq_global(chip_sum[i])
            glob = glob.reshape(glob.shape[:2] + (N_CHIPS, glob.shape[2] // N_CHIPS))
            chip_sum[i] = jnp.moveaxis(_pad_lanes(glob), 2, 0)
    from_chips = _chips_exchange(chip_sum, r_streams, shard_shapes, name="reduce_chips")
    halves = [_sum_chips(ps, got, place, st, name=f"reduce_sum_{n}")
              for ps, got, st, n in zip(chip_sum, from_chips, r_streams, names)]
    joined = dict(zip(names, _sibling_join_halves(halves, name="reduce_join")))
    joined.update(_unpack_vectors(joined.pop("vectors"), {n: w[n].shape for n in VECTORS}))

    grad_w, delta_w, new_m, new_v = {}, {}, {}, {}
    for n in WEIGHTS:
        grad_w[n] = joined[n][..., :w[n].shape[-1]]
        delta_w[n], new_m[n], new_v[n] = _adamw(w[n], grad_w[n], m[n], v[n], name=f"adamw_{n}")
    return (loss, grad_x[None], *[grad_w[n] for n in WEIGHTS], *[delta_w[n] for n in WEIGHTS],
            *[new_m[n] for n in WEIGHTS], *[new_v[n] for n in WEIGHTS])
```

```python
import functools

import jax
import jax.numpy as jnp
from jax import lax
from jax.experimental import pallas as pl
from jax.experimental.pallas import tpu as pltpu

F32 = jnp.float32
BF16 = jnp.bfloat16
MESH = pl.DeviceIdType.MESH

CHUNK = 64
N_HEADS = 4
HEAD_DIM = 128
ROPE_DIM = 64
ROPE_THETA = 10000.0
LN_EPS = 1e-5
RMS_EPS = 1e-6
ADAM_LR, ADAM_B1, ADAM_B2, ADAM_EPS, ADAM_WD, ADAM_STEP = 0.001, 0.9, 0.999, 1e-08, 0.01, 10

LANES = 128
VMEM_LIMIT = 48 * 1024 * 1024
ROW_TILE = 512
WIDE_ROW_TILE = 256
WIDE_COLS = 2048
GATED_TILE = 512
SUB_ROWS = 16
MAX_SUB_ROWS = 64
VREG_FILE_ELEMS = 64 * 8 * LANES

MISC_BETA0 = ROPE_DIM
MISC_A0 = ROPE_DIM + N_HEADS

NN = (((1,), (0,)), ((), ()))
NT = (((1,), (1,)), ((), ()))
TN = (((0,), (0,)), ((), ()))


def _params(sem=None):
    return pltpu.CompilerParams(dimension_semantics=sem, vmem_limit_bytes=VMEM_LIMIT)


def _divisor_tile(dim, target, unit):
    best = None
    t = unit
    while t <= min(dim, target):
        if dim % t == 0:
            best = t
        t += unit
    return best if best is not None else dim


BATCHED = {NN: (((2,), (1,)), ((0,), (0,))), NT: (((2,), (2,)), ((0,), (0,))), TN: (((1,), (1,)), ((0,), (0,)))}


def _make_dots():
    def raw(a, b, dims):
        if a.ndim == 3:
            dims = BATCHED[dims]
        return lax.dot_general(a.astype(BF16), b.astype(BF16), dims, preferred_element_type=F32)

    @jax.custom_vjp
    def nn(a, b):
        return raw(a, b, NN)

    @jax.custom_vjp
    def nt(a, b):
        return raw(a, b, NT)

    @jax.custom_vjp
    def tn(a, b):
        return raw(a, b, TN)

    nn.defvjp(lambda a, b: (raw(a, b, NN), (a, b)), lambda r, g: (nt(g, r[1]), tn(r[0], g)))
    nt.defvjp(lambda a, b: (raw(a, b, NT), (a, b)), lambda r, g: (nn(g, r[1]), tn(g, r[0])))
    tn.defvjp(lambda a, b: (raw(a, b, TN), (a, b)), lambda r, g: (nt(r[1], g), nn(r[0], g)))
    return nn, nt, tn


_nn, _nt, _tn = _make_dots()


def _matmul(a, b, *, dims, name, c=None, out_dtype=F32, tm=1024, tn=1408, tk=1408, layer=None, into=None,
            a_gated=False):
    b_shape = b.shape[-2:]
    a_shape = (a.shape[0], a.shape[1] // 2) if a_gated else a.shape
    if dims == "nn":
        (m, k), (k2, n) = a_shape, b_shape
    elif dims == "nt":
        (m, k), (n, k2) = a_shape, b_shape
    else:
        (k, m), (k2, n) = a_shape, b_shape
    assert k == k2, (a.shape, b.shape, dims)
    tm = _divisor_tile(m, tm, LANES)
    tn = _divisor_tile(n, tn, LANES)
    tk = _divisor_tile(k, tk, LANES)
    nk = k // tk
    dn = {"nn": NN, "nt": NT, "tn": TN}[dims]
    if dims == "tn":
        a_blk, a_idx, up_off = (tk, tm), (lambda i, j, kk: (kk, i)), m // tm
    else:
        a_blk, a_idx, up_off = (tm, tk), (lambda i, j, kk: (i, kk)), k // tk
    a_spec = pl.BlockSpec(a_blk, a_idx)
    up_spec = pl.BlockSpec(a_blk, lambda i, j, kk: (a_idx(i, j, kk)[0], a_idx(i, j, kk)[1] + up_off))
    b_blk, b_idx = ((tn, tk), lambda i, j, kk: (j, kk)) if dims == "nt" else ((tk, tn), lambda i, j, kk: (kk, j))
    if b.ndim == 3:
        b_spec = pl.BlockSpec((None,) + b_blk, lambda i, j, kk: (layer,) + b_idx(i, j, kk))
    else:
        b_spec = pl.BlockSpec(b_blk, b_idx)
    c_spec = pl.BlockSpec((tm, tn), lambda i, j, kk: (i, j))
    if isinstance(into, int):
        o_spec = pl.BlockSpec((None, tm, tn), lambda i, j, kk: (layer, i, j))
        out_shape = jax.ShapeDtypeStruct((into, m, n), out_dtype)
        into = None
    elif into is not None:
        assert into.shape[1:] == (m, n) and into.dtype == out_dtype
        o_spec = pl.BlockSpec((None, tm, tn), lambda i, j, kk: (layer, i, j))
        out_shape = jax.ShapeDtypeStruct(into.shape, into.dtype)
    else:
        o_spec = c_spec
        out_shape = jax.ShapeDtypeStruct((m, n), out_dtype)
    has_c = c is not None

    n_a = 2 if a_gated else 1

    def body(*refs):
        b_ref = refs[n_a]
        c_ref = refs[n_a + 1] if has_c else None
        acc_ref = refs[-1]
        o_ref = refs[-3] if a_gated else refs[-2]
        kk = pl.program_id(2)

        @pl.when(kk == 0)
        def _():
            if has_c:
                acc_ref[...] = c_ref[...].astype(F32)
            else:
                acc_ref[...] = jnp.zeros_like(acc_ref)

        if a_gated:
            a_val = (_silu(refs[0][...].astype(F32)) * refs[1][...].astype(F32)).astype(BF16)
            refs[-2][...] = a_val
        else:
            a_val = refs[0][...].astype(BF16)
        acc_ref[...] += lax.dot_general(a_val, b_ref[...].astype(BF16), dn, preferred_element_type=F32)

        @pl.when(kk == nk - 1)
        def _():
            o_ref[...] = acc_ref[...].astype(o_ref.dtype)

    ins = ([a, a] if a_gated else [a]) + [b] + ([c] if has_c else [])
    specs = ([a_spec, up_spec] if a_gated else [a_spec]) + [b_spec] + ([c_spec] if has_c else [])
    aliases = {}
    if into is not None:
        aliases = {len(ins): 0}
        ins.append(into)
        specs.append(pl.BlockSpec(memory_space=pl.ANY))
    if a_gated:
        assert dims == "nn" and n == tn
        o_spec, out_shape = [o_spec, a_spec], [out_shape, jax.ShapeDtypeStruct((m, k), BF16)]
    return pl.pallas_call(
        body, name=name, grid=(m // tm, n // tn, nk), in_specs=specs, out_specs=o_spec, out_shape=out_shape,
        scratch_shapes=[pltpu.VMEM((tm, tn), F32)], input_output_aliases=aliases,
        compiler_params=_params(("arbitrary", "arbitrary", "arbitrary")),
    )(*ins)


def _rowwise(fn, rows, params, outs, accs=(), *, name):
    t = rows[0][0].shape[0]
    widest = max([w for _, w, _ in rows] + [w for w, _ in outs])
    tm = min(WIDE_ROW_TILE if widest > WIDE_COLS else ROW_TILE, t)
    sub = SUB_ROWS
    while sub < MAX_SUB_ROWS and 2 * sub * widest <= VREG_FILE_ELEMS:
        sub *= 2
    assert t % tm == 0 and tm % sub == 0
    n_rows, n_par, n_out, n_acc = len(rows), len(params), len(outs), len(accs)

    def body(*refs):
        row_refs = refs[:n_rows]
        par_refs = refs[n_rows:n_rows + n_par]
        out_refs = refs[n_rows + n_par:n_rows + n_par + n_out]
        acc_refs = refs[n_rows + n_par + n_out:]
        if n_acc:
            @pl.when(pl.program_id(0) == 0)
            def _():
                for a_ref in acc_refs:
                    a_ref[...] = jnp.zeros_like(a_ref)

        def step(r, carry):
            sl = pl.ds(pl.multiple_of(r * sub, sub), sub)
            vals = [ref[sl, :].astype(F32) for ref in row_refs] + [ref[...] for ref in par_refs]
            res = fn(*vals)
            for o_ref, val in zip(out_refs, res[:n_out]):
                o_ref[sl, :] = val.astype(o_ref.dtype)
            for a_ref, val in zip(acc_refs, res[n_out:]):
                a_ref[...] += val
            return carry

        lax.fori_loop(0, tm // sub, step, 0)

    in_specs = [pl.BlockSpec((tm, w), functools.partial(lambda i, cb: (i, cb), cb=cb)) for _, w, cb in rows]
    in_specs += [pl.BlockSpec(p.shape, lambda i: (0, 0)) for p in params]
    out_specs = [pl.BlockSpec((tm, w), lambda i: (i, 0)) for w, _ in outs]
    out_specs += [pl.BlockSpec(s, lambda i: (0, 0)) for s in accs]
    out_shape = [jax.ShapeDtypeStruct((t, w), d) for w, d in outs]
    out_shape += [jax.ShapeDtypeStruct(s, F32) for s in accs]
    return pl.pallas_call(
        body, name=name, grid=(t // tm,), in_specs=in_specs, out_specs=out_specs, out_shape=out_shape,
        compiler_params=_params(("arbitrary",)),
    )(*[r[0] for r in rows], *params)


def _vjp_fn(fn, n_in, n_out):
    def bwd(*args):
        ins, cts = args[:n_in], args[n_in:]
        _, pull = jax.vjp(fn, *ins)
        return pull(tuple(cts) if n_out > 1 else cts[0])
    return bwd


def _lane(shape):
    return lax.broadcasted_iota(jnp.int32, shape, 1)


def _silu(x):
    return x * jax.nn.sigmoid(x)


def _softplus(x):
    return jnp.maximum(x, 0.0) + jnp.log1p(jnp.exp(-jnp.abs(x)))


def _heads(x, width=HEAD_DIM):
    return [x[:, h * width:(h + 1) * width] for h in range(N_HEADS)]


def _layer_norm(z, g, b):
    mu = jnp.mean(z, -1, keepdims=True)
    zc = z - mu
    var = jnp.mean(zc * zc, -1, keepdims=True)
    return zc * lax.rsqrt(var + LN_EPS) * g + b


def _gdn_act(u, misc, alog_row, dtb_row):
    s = _silu(u)
    w = N_HEADS * HEAD_DIM
    q = jnp.concatenate([t * lax.rsqrt(jnp.sum(t * t, -1, keepdims=True) + RMS_EPS) * HEAD_DIM ** -0.5
                         for t in _heads(s[:, :w])], axis=1)
    k = jnp.concatenate([t * lax.rsqrt(jnp.sum(t * t, -1, keepdims=True) + RMS_EPS)
                         for t in _heads(s[:, w:2 * w])], axis=1)
    v = s[:, 2 * w:]
    lane = _lane(misc.shape)
    beta = jax.nn.sigmoid(misc)
    g = -jnp.exp(alog_row) * _softplus(misc + dtb_row)
    is_beta = (lane >= MISC_BETA0) & (lane < MISC_BETA0 + N_HEADS)
    is_g = (lane >= MISC_A0) & (lane < MISC_A0 + N_HEADS)
    gb = jnp.where(is_beta, beta, jnp.where(is_g, g, 0.0))
    return q, k, v, gb


def _gdn_out(o, z, gn_row):
    outs = []
    for oh, zh in zip(_heads(o), _heads(z)):
        r = oh * lax.rsqrt(jnp.mean(oh * oh, -1, keepdims=True) + RMS_EPS) * gn_row
        outs.append(r * _silu(zh))
    return jnp.concatenate(outs, axis=1)


def _mla_norm(ckv, cq, kvg_row, qg_row):
    cqn = cq * lax.rsqrt(jnp.mean(cq * cq, -1, keepdims=True) + RMS_EPS) * qg_row
    ckvn = ckv * lax.rsqrt(jnp.mean(ckv * ckv, -1, keepdims=True) + RMS_EPS) * kvg_row
    return cqn, ckvn


def _swap_halves(x):
    half = ROPE_DIM // 2
    return jnp.where(_lane(x.shape) < half, pltpu.roll(x, LANES - half, 1), pltpu.roll(x, half, 1))


@jax.custom_vjp
def _rope(x, cos_t, sin_t):
    return x * cos_t + _swap_halves(x) * sin_t


def _rope_fwd(x, cos_t, sin_t):
    return _rope(x, cos_t, sin_t), (cos_t, sin_t)


def _rope_bwd(res, g):
    cos_t, sin_t = res
    return g * cos_t - _swap_halves(g) * sin_t, jnp.zeros_like(cos_t), jnp.zeros_like(sin_t)


_rope.defvjp(_rope_fwd, _rope_bwd)


def _mla_qk(scale, qm, kv, misc, cos_t, sin_t):
    krope = _rope(misc, cos_t, sin_t)
    qs, ks = [], []
    for h in range(N_HEADS):
        base = 2 * HEAD_DIM * h
        qs += [qm[:, base:base + HEAD_DIM], _rope(qm[:, base + HEAD_DIM:base + 2 * HEAD_DIM], cos_t, sin_t)]
        ks += [kv[:, HEAD_DIM * h:HEAD_DIM * (h + 1)], krope]
    return jnp.concatenate(qs, axis=1) * scale, jnp.concatenate(ks, axis=1), kv[:, N_HEADS * HEAD_DIM:]


def _swiglu(gu):
    f = gu.shape[1] // 2
    return _silu(gu[:, :f]) * gu[:, f:]


def _ple_out(x2, pg, pe):
    return x2 + jax.nn.sigmoid(pg) * pe


CONV_W = 4
HALO = 8
CONV_STRIP = 512


def _conv_fwd(h, conv_w, width, *, name, tm=ROW_TILE, sub=32):
    t = h.shape[0]
    tm = min(tm, t)
    nb = tm // HALO

    def body(x_ref, halo_ref, w_ref, u_ref, buf):
        i = pl.program_id(0)
        buf[pl.ds(0, HALO), :] = jnp.where(i > 0, halo_ref[...], 0.0)
        buf[pl.ds(HALO, tm), :] = x_ref[...]
        for c0 in range(0, width, CONV_STRIP):
            cols = pl.ds(c0, CONV_STRIP)
            w = w_ref[:, cols]
            for r0 in range(0, tm, sub):
                acc = jnp.zeros((sub, CONV_STRIP), F32)
                for j in range(CONV_W):
                    acc = acc + w[j:j + 1, :] * buf[pl.ds(HALO + r0 - (CONV_W - 1) + j, sub), cols]
                u_ref[pl.ds(r0, sub), cols] = acc

    return pl.pallas_call(
        body, name=name, grid=(t // tm,),
        in_specs=[pl.BlockSpec((tm, width), lambda i: (i, 0)),
                  pl.BlockSpec((HALO, width), lambda i: (jnp.maximum(i * nb - 1, 0), 0)),
                  pl.BlockSpec(conv_w.shape, lambda i: (0, 0))],
        out_specs=pl.BlockSpec((tm, width), lambda i: (i, 0)),
        out_shape=jax.ShapeDtypeStruct((t, width), F32),
        scratch_shapes=[pltpu.VMEM((tm + HALO, width), F32)],
        compiler_params=_params(("arbitrary",)),
    )(h, h, conv_w)


def _conv_bwd(du, h, conv_w, width, *, name, tm=ROW_TILE, sub=32):
    t = h.shape[0]
    tm = min(tm, t)
    nb = tm // HALO
    n_tiles = t // tm

    def body(du_ref, du_halo, x_ref, x_halo, w_ref, dx_ref, dw_ref, dbuf, xbuf):
        i = pl.program_id(0)

        @pl.when(i == 0)
        def _():
            dw_ref[...] = jnp.zeros_like(dw_ref)

        dbuf[pl.ds(0, tm), :] = du_ref[...]
        dbuf[pl.ds(tm, HALO), :] = jnp.where(i < n_tiles - 1, du_halo[...], 0.0)
        xbuf[pl.ds(0, HALO), :] = jnp.where(i > 0, x_halo[...], 0.0)
        xbuf[pl.ds(HALO, tm), :] = x_ref[...]
        for c0 in range(0, width, CONV_STRIP):
            cols = pl.ds(c0, CONV_STRIP)
            w = w_ref[:, cols]
            dws = [jnp.zeros((HALO, CONV_STRIP), F32) for _ in range(CONV_W)]
            for r0 in range(0, tm, sub):
                acc = jnp.zeros((sub, CONV_STRIP), F32)
                d_here = dbuf[pl.ds(r0, sub), cols]
                for j in range(CONV_W):
                    acc = acc + w[j:j + 1, :] * dbuf[pl.ds(r0 + (CONV_W - 1) - j, sub), cols]
                    prod = d_here * xbuf[pl.ds(HALO + r0 - (CONV_W - 1) + j, sub), cols]
                    for g0 in range(0, sub, HALO):
                        dws[j] = dws[j] + prod[g0:g0 + HALO, :]
                dx_ref[pl.ds(r0, sub), cols] = acc.astype(dx_ref.dtype)
            for j in range(CONV_W):
                dw_ref[pl.ds(j, 1), cols] += jnp.sum(dws[j], axis=0, keepdims=True)

    return pl.pallas_call(
        body, name=name, grid=(n_tiles,),
        in_specs=[pl.BlockSpec((tm, width), lambda i: (i, 0)),
                  pl.BlockSpec((HALO, width), lambda i: (jnp.minimum((i + 1) * nb, t // HALO - 1), 0)),
                  pl.BlockSpec((tm, width), lambda i: (i, 0)),
                  pl.BlockSpec((HALO, width), lambda i: (jnp.maximum(i * nb - 1, 0), 0)),
                  pl.BlockSpec(conv_w.shape, lambda i: (0, 0))],
        out_specs=[pl.BlockSpec((tm, width), lambda i: (i, 0)),
                   pl.BlockSpec((HALO, width), lambda i: (0, 0))],
        out_shape=[jax.ShapeDtypeStruct((t, width), BF16), jax.ShapeDtypeStruct((HALO, width), F32)],
        scratch_shapes=[pltpu.VMEM((tm + HALO, width), F32), pltpu.VMEM((tm + HALO, width), F32)],
        compiler_params=_params(("arbitrary",)),
    )(du, du, h, h, conv_w)


@jax.custom_vjp
def _inv_unit_lower(low):
    n = low.shape[-1]
    eye = (lax.broadcasted_iota(jnp.int32, (n, n), 0) == lax.broadcasted_iota(jnp.int32, (n, n), 1)).astype(F32)
    x = eye - low
    p = low
    span = 2
    while span < n:
        p = _nn(p, p)
        x = x + _nn(x, p)
        span *= 2
    return x


def _inv_fwd(low):
    x = _inv_unit_lower(low)
    return x, x


def _inv_bwd(x, g):
    return (-_tn(x, _nt(g, x)),)


_inv_unit_lower.defvjp(_inv_fwd, _inv_bwd)


@jax.custom_vjp
def _inv_known(low, inverse):
    return inverse


_inv_known.defvjp(lambda low, inverse: (inverse, inverse), lambda x, g: (_inv_bwd(x, g)[0], jnp.zeros_like(x)))


def _gdn_prep(q, k, v, gb, known_inverse=None):
    c = CHUNK
    n = q.shape[0] // c
    pairs = [(g, h) for g in range(n) for h in range(N_HEADS)]
    row = lax.broadcasted_iota(jnp.int32, (c, c), 0)
    col = lax.broadcasted_iota(jnp.int32, (c, c), 1)
    tri_incl = row >= col
    tri_strict = row > col
    lane = _lane((c, LANES))
    sub = lax.broadcasted_iota(jnp.int32, (LANES, c), 0)
    last = lax.broadcasted_iota(jnp.int32, (c, 1), 0) == c - 1

    def split(x):
        return jnp.stack([x[g * c:(g + 1) * c, h * HEAD_DIM:(h + 1) * HEAD_DIM] for g, h in pairs])

    gbs = [gb[g * c:(g + 1) * c, :] for g in range(n)]
    gbts = [x.T for x in gbs]
    g_col = jnp.stack([jnp.sum(jnp.where(lane == MISC_A0 + h, gbs[g], 0.0), axis=1, keepdims=True) for g, h in pairs])
    b_col = jnp.stack([jnp.sum(jnp.where(lane == MISC_BETA0 + h, gbs[g], 0.0), axis=1, keepdims=True) for g, h in pairs])
    g_row = jnp.stack([jnp.sum(jnp.where(sub == MISC_A0 + h, gbts[g], 0.0), axis=0, keepdims=True) for g, h in pairs])
    gc_col = jnp.sum(jnp.where(tri_incl, g_row, 0.0), axis=2, keepdims=True)
    gc_row = jnp.sum(jnp.where(row <= col, g_col, 0.0), axis=1, keepdims=True)
    decay = jnp.where(tri_incl, jnp.exp(jnp.where(tri_incl, gc_col - gc_row, 0.0)), 0.0)
    g_last = jnp.sum(jnp.where(last, gc_col, 0.0), axis=1, keepdims=True)
    qs, ks, vs = split(q), split(k), split(v)
    kb = ks * b_col
    low = jnp.where(tri_strict, _nt(kb, ks) * decay, 0.0)
    if known_inverse is None:
        tinv = _inv_unit_lower(low)
    else:
        tinv = _inv_known(low, jnp.stack([known_inverse[g * c:(g + 1) * c, h * c:(h + 1) * c] for g, h in pairs]))
    eg = jnp.exp(gc_col)
    sol = _nn(tinv, jnp.concatenate([vs * b_col, kb * eg], axis=2))
    attn = jnp.where(tri_incl, _nt(qs, ks) * decay, 0.0)
    qd = qs * eg
    kd = ks * jnp.exp(g_last - gc_col)

    def merge(x):
        return jnp.concatenate([jnp.concatenate([x[g * N_HEADS + h] for h in range(N_HEADS)], axis=1)
                                for g in range(n)], axis=0)

    glb = jnp.concatenate([sum(jnp.where(lane == h, g_last[g * N_HEADS + h], 0.0) for h in range(N_HEADS))
                           for g in range(n)], axis=0)
    outs = (merge(sol[:, :, :HEAD_DIM]), merge(sol[:, :, HEAD_DIM:]), merge(qd), merge(kd), merge(attn), glb)
    return outs, merge(tinv)


def _gdn_seq(state, u, w, qd, kd, attn, glb):
    c = u.shape[0]
    first = lax.broadcasted_iota(jnp.int32, glb.shape, 0) == 0
    lane = _lane(glb.shape)
    heads = lambda x: jnp.stack([x[:, h * HEAD_DIM:(h + 1) * HEAD_DIM] for h in range(N_HEADS)])
    g_last = jnp.stack([jnp.sum(jnp.sum(jnp.where(first & (lane == h), glb, 0.0), axis=1, keepdims=True),
                                axis=0, keepdims=True) for h in range(N_HEADS)])
    s = jnp.stack([state[h * HEAD_DIM:(h + 1) * HEAD_DIM, :] for h in range(N_HEADS)])
    at = jnp.stack([attn[:, h * c:(h + 1) * c] for h in range(N_HEADS)])
    v_new = heads(u) - _nn(heads(w), s)
    o = _nn(heads(qd), s) + _nn(at, v_new)
    s_new = s * jnp.exp(g_last) + _tn(heads(kd), v_new)
    return (jnp.concatenate([o[h] for h in range(N_HEADS)], axis=1),
            jnp.concatenate([s_new[h] for h in range(N_HEADS)], axis=0))


PREP_CHUNKS = 8
PREP_CHUNKS_BWD = 4
SEQ_CHUNKS = 8


def _gdn_prep_fwd(q, k, v, gb, *, name):
    t, w = q.shape
    rows = min(PREP_CHUNKS * CHUNK, t)

    def body(q_ref, k_ref, v_ref, gb_ref, *out_refs):
        outs, inverse = _gdn_prep(q_ref[...], k_ref[...], v_ref[...], gb_ref[...])
        for o_ref, val in zip(out_refs, outs + (inverse,)):
            o_ref[...] = val

    spec = lambda width: pl.BlockSpec((rows, width), lambda i: (i, 0))
    widths = [w, w, w, w, N_HEADS * CHUNK, LANES, N_HEADS * CHUNK]
    res = pl.pallas_call(
        body, name=name, grid=(t // rows,),
        in_specs=[spec(w), spec(w), spec(w), spec(LANES)],
        out_specs=[spec(x) for x in widths],
        out_shape=[jax.ShapeDtypeStruct((t, x), F32) for x in widths],
        compiler_params=_params(("arbitrary",)),
    )(q, k, v, gb)
    return tuple(res[:6]), res[6]


def _gdn_prep_bwd(q, k, v, gb, inverse, cts, *, name):
    t, w = q.shape
    rows = min(PREP_CHUNKS_BWD * CHUNK, t)

    def body(q_ref, k_ref, v_ref, gb_ref, inv_ref, du, dw, dqd, dkd, dattn, dglb, dq_ref, dk_ref, dv_ref, dgb_ref):
        known = inv_ref[...]
        _, pull = jax.vjp(lambda a, b, c_, d_: _gdn_prep(a, b, c_, d_, known)[0],
                          q_ref[...], k_ref[...], v_ref[...], gb_ref[...])
        dq, dk, dv, dgb = pull(tuple(r[...] for r in (du, dw, dqd, dkd, dattn, dglb)))
        dq_ref[...] = dq
        dk_ref[...] = dk
        dv_ref[...] = dv
        dgb_ref[...] = dgb

    spec = lambda width: pl.BlockSpec((rows, width), lambda i: (i, 0))
    widths = [w, w, w, w, N_HEADS * CHUNK, LANES]
    return pl.pallas_call(
        body, name=name, grid=(t // rows,),
        in_specs=[spec(w), spec(w), spec(w), spec(LANES), spec(N_HEADS * CHUNK)] + [spec(x) for x in widths],
        out_specs=[spec(w), spec(w), spec(w), spec(LANES)],
        out_shape=[jax.ShapeDtypeStruct((t, w), F32)] * 3 + [jax.ShapeDtypeStruct((t, LANES), F32)],
        compiler_params=_params(("arbitrary",)),
    )(q, k, v, gb, inverse, *cts)


def _gdn_seq_fwd(prep, *, name):
    t, w = prep[0].shape
    rows = min(SEQ_CHUNKS * CHUNK, t)
    per = rows // CHUNK

    def body(u_ref, w_ref, qd_ref, kd_ref, at_ref, gl_ref, o_ref, sall_ref, s_scr):
        @pl.when(pl.program_id(0) == 0)
        def _():
            s_scr[...] = jnp.zeros_like(s_scr)

        def step(j, carry):
            sl = pl.ds(pl.multiple_of(j * CHUNK, CHUNK), CHUNK)
            s = s_scr[...]
            sall_ref[j] = s
            o, s_new = _gdn_seq(s, u_ref[sl, :], w_ref[sl, :], qd_ref[sl, :], kd_ref[sl, :], at_ref[sl, :], gl_ref[sl, :])
            o_ref[sl, :] = o
            s_scr[...] = s_new
            return carry

        lax.fori_loop(0, per, step, 0)

    spec = lambda width: pl.BlockSpec((rows, width), lambda i: (i, 0))
    widths = [w, w, w, w, N_HEADS * CHUNK, LANES]
    return pl.pallas_call(
        body, name=name, grid=(t // rows,),
        in_specs=[spec(x) for x in widths],
        out_specs=[spec(w), pl.BlockSpec((per, w, HEAD_DIM), lambda i: (i, 0, 0))],
        out_shape=[jax.ShapeDtypeStruct((t, w), F32), jax.ShapeDtypeStruct((t // CHUNK, w, HEAD_DIM), F32)],
        scratch_shapes=[pltpu.VMEM((w, HEAD_DIM), F32)],
        compiler_params=_params(("arbitrary",)),
    )(*prep)


def _gdn_seq_bwd(prep, s_all, do, *, name):
    t, w = prep[0].shape
    rows = min(SEQ_CHUNKS * CHUNK, t)
    per = rows // CHUNK
    n = t // rows

    def body(u_ref, w_ref, qd_ref, kd_ref, at_ref, gl_ref, sall_ref, do_ref, du, dw, dqd, dkd, dat, dgl, ds_scr):
        @pl.when(pl.program_id(0) == 0)
        def _():
            ds_scr[...] = jnp.zeros_like(ds_scr)

        def step(jj, carry):
            j = per - 1 - jj
            sl = pl.ds(pl.multiple_of(j * CHUNK, CHUNK), CHUNK)
            _, pull = jax.vjp(_gdn_seq, sall_ref[j], u_ref[sl, :], w_ref[sl, :], qd_ref[sl, :], kd_ref[sl, :],
                              at_ref[sl, :], gl_ref[sl, :])
            res = pull((do_ref[sl, :], ds_scr[...]))
            ds_scr[...] = res[0]
            for o_ref, val in zip((du, dw, dqd, dkd, dat, dgl), res[1:]):
                o_ref[sl, :] = val
            return carry

        lax.fori_loop(0, per, step, 0)

    spec = lambda width: pl.BlockSpec((rows, width), lambda i: (n - 1 - i, 0))
    widths = [w, w, w, w, N_HEADS * CHUNK, LANES]
    return pl.pallas_call(
        body, name=name, grid=(n,),
        in_specs=[spec(x) for x in widths] + [pl.BlockSpec((per, w, HEAD_DIM), lambda i: (n - 1 - i, 0, 0)), spec(w)],
        out_specs=[spec(x) for x in widths],
        out_shape=[jax.ShapeDtypeStruct((t, x), F32) for x in widths],
        scratch_shapes=[pltpu.VMEM((w, HEAD_DIM), F32)],
        compiler_params=_params(("arbitrary",)),
    )(*prep, s_all, do)


QK_DIM = 2 * HEAD_DIM
ATT_TILE = 1024
NEG = -1e30


ATT_SPLIT = 4


def _chunk_mask(n_rows, n_cols, key_major, query_offset):
    r = lax.broadcasted_iota(jnp.int32, (n_rows, n_cols), 0)
    c = lax.broadcasted_iota(jnp.int32, (n_rows, n_cols), 1)
    if key_major:
        return r // CHUNK <= (c + query_offset) // CHUNK
    return c // CHUNK <= (r + query_offset) // CHUNK


def _visible_keys(tile, diagonal):
    hq = tile // ATT_SPLIT
    return [(a + 1) * hq if diagonal else tile for a in range(ATT_SPLIT)]


def _dot_nn(a, b):
    return lax.dot_general(a, b, NN, preferred_element_type=F32)


def _blocked_transpose(x, width):
    t = x.shape[0]
    tile = min(ATT_TILE, t)
    return x.reshape(t // tile, tile, N_HEADS * width).transpose(0, 2, 1).reshape(t // tile, N_HEADS, width, tile)


def _attn_fwd(q, kt, v1, *, name):
    t = q.shape[0]
    tq = min(ATT_TILE, t)
    nq = t // tq

    def body(q_ref, kt_ref, v_ref, o_ref, lse_ref, m_scr, acc_scr):
        qi = pl.program_id(1)
        m_scr[...] = jnp.full_like(m_scr, NEG)
        acc_scr[...] = jnp.zeros_like(acc_scr)
        hq = tq // ATT_SPLIT
        parts = [pl.ds(a * hq, hq) for a in range(ATT_SPLIT)]
        qs = [q_ref[sl, :] for sl in parts]

        def step(kj, masked):
            rows = pl.ds(pl.multiple_of(kj * tq, tq), tq)
            kt_blk, vv = kt_ref[kj], v_ref[rows, :]
            seen = _visible_keys(tq, masked)
            ss = [_dot_nn(qv, kt_blk[:, :w]) for qv, w in zip(qs, seen)]
            for a, sl in enumerate(parts):
                s = ss[a]
                if masked:
                    s = jnp.where(_chunk_mask(hq, seen[a], False, a * hq), s, NEG)
                m_old = m_scr[sl, :]
                m_new = jnp.maximum(m_old, jnp.max(s, axis=1, keepdims=True))
                p = jnp.exp(s - m_new)
                acc_scr[sl, :] = jnp.exp(m_old - m_new) * acc_scr[sl, :] + _dot_nn(p.astype(BF16), vv[:seen[a], :])
                m_scr[sl, :] = m_new

        def loop_body(kj, carry):
            step(kj, False)
            return carry

        lax.fori_loop(0, qi, loop_body, 0)
        step(qi, True)
        acc = acc_scr[...]
        o_ref[...] = (acc[:, :HEAD_DIM] / acc[:, HEAD_DIM:]).astype(o_ref.dtype)
        lse_ref[...] = m_scr[...] + jnp.log(acc[:, HEAD_DIM:HEAD_DIM + 1])

    return pl.pallas_call(
        body, name=name, grid=(N_HEADS, nq),
        in_specs=[pl.BlockSpec((tq, QK_DIM), lambda h, i: (i, h)),
                  pl.BlockSpec((nq, None, QK_DIM, tq), lambda h, i: (0, h, 0, 0)),
                  pl.BlockSpec((t, 2 * HEAD_DIM), lambda h, i: (0, h))],
        out_specs=[pl.BlockSpec((tq, HEAD_DIM), lambda h, i: (i, h)),
                   pl.BlockSpec((None, tq, 1), lambda h, i: (h, i, 0))],
        out_shape=[jax.ShapeDtypeStruct((t, N_HEADS * HEAD_DIM), BF16),
                   jax.ShapeDtypeStruct((N_HEADS, t, 1), F32)],
        scratch_shapes=[pltpu.VMEM((tq, 1), F32), pltpu.VMEM((tq, 2 * HEAD_DIM), F32)],
        compiler_params=_params(("arbitrary", "arbitrary")),
    )(q, kt, v1)


def _attn_delta(dom, o, *, name):
    hw = o.shape[1]

    def fn(do, ov):
        lane = _lane((do.shape[0], LANES))
        out = jnp.zeros((do.shape[0], LANES), F32)
        for h, (a, b) in enumerate(zip(_heads(do), _heads(ov))):
            out = out + jnp.where(lane == h, jnp.sum(a * b, axis=1, keepdims=True), 0.0)
        return (out,)

    return _rowwise(fn, [(dom, hw, 1), (o, hw, 0)], [], [(LANES, F32)], name=name)[0]


def _attn_bwd(q, qt, k, v, lse_row, delta_row, do, dot, *, name):
    t = q.shape[0]
    tk = min(ATT_TILE, t)
    nk = t // tk

    def body(q_ref, qt_ref, k_ref, v_ref, lse_ref, delta_ref, do_ref, dot_ref, dk_ref, dv_ref, dq_ref, dk_scr, dv_scr):
        kj = pl.program_id(1)

        @pl.when(kj == 0)
        def _():
            dq_ref[...] = jnp.zeros_like(dq_ref)

        dk_scr[...] = jnp.zeros_like(dk_scr)
        dv_scr[...] = jnp.zeros_like(dv_scr)
        kv_ = k_ref[...]
        vv = v_ref[...]
        hq = tk // ATT_SPLIT

        def step(qi, masked):
            lse_v, delta_v = lse_ref[qi], delta_ref[qi]
            qt_blk, dot_blk = qt_ref[qi], dot_ref[qi]
            rows = [pl.ds(pl.multiple_of(qi * tk + a * hq, hq), hq) for a in range(ATT_SPLIT)]
            qs = [q_ref[r, :] for r in rows]
            dos = [do_ref[r, :] for r in rows]
            seen = _visible_keys(tk, masked)
            ss = [_dot_nn(kv_[:seen[a], :], qt_blk[:, a * hq:(a + 1) * hq]) for a in range(ATT_SPLIT)]
            dps = [_dot_nn(vv[:seen[a], :], dot_blk[:, a * hq:(a + 1) * hq]) for a in range(ATT_SPLIT)]
            for a in range(ATT_SPLIT):
                cols = slice(a * hq, (a + 1) * hq)
                keys = pl.ds(0, seen[a])
                p = jnp.exp(ss[a] - lse_v[:, cols])
                if masked:
                    p = jnp.where(_chunk_mask(seen[a], hq, True, a * hq), p, 0.0)
                dv_scr[keys, :] += _dot_nn(p.astype(BF16), dos[a])
                ds = (p * (dps[a] - delta_v[:, cols])).astype(BF16)
                dk_scr[keys, :] += _dot_nn(ds, qs[a])
                dq_ref[rows[a], :] += lax.dot_general(ds, kv_[:seen[a], :], TN, preferred_element_type=F32)

        step(kj, True)

        def loop_body(qi, carry):
            step(qi, False)
            return carry

        lax.fori_loop(kj + 1, nk, loop_body, 0)
        dk_ref[...] = dk_scr[...].astype(dk_ref.dtype)
        dv_ref[...] = dv_scr[...].astype(dv_ref.dtype)

    once = dict(pipeline_mode=pl.Buffered(1))
    stat = pl.BlockSpec((None, nk, 1, tk), lambda h, j: (h, 0, 0, 0))
    return pl.pallas_call(
        body, name=name, grid=(N_HEADS, nk),
        in_specs=[pl.BlockSpec((t, QK_DIM), lambda h, j: (0, h), **once),
                  pl.BlockSpec((nk, None, QK_DIM, tk), lambda h, j: (0, h, 0, 0), **once),
                  pl.BlockSpec((tk, QK_DIM), lambda h, j: (j, h)),
                  pl.BlockSpec((tk, HEAD_DIM), lambda h, j: (j, h)),
                  stat, stat,
                  pl.BlockSpec((t, HEAD_DIM), lambda h, j: (0, h), **once),
                  pl.BlockSpec((nk, None, HEAD_DIM, tk), lambda h, j: (0, h, 0, 0), **once)],
        out_specs=[pl.BlockSpec((tk, QK_DIM), lambda h, j: (j, h)),
                   pl.BlockSpec((tk, HEAD_DIM), lambda h, j: (j, h)),
                   pl.BlockSpec((t, QK_DIM), lambda h, j: (0, h))],
        out_shape=[jax.ShapeDtypeStruct((t, N_HEADS * QK_DIM), BF16),
                   jax.ShapeDtypeStruct((t, N_HEADS * HEAD_DIM), BF16),
                   jax.ShapeDtypeStruct((t, N_HEADS * QK_DIM), F32)],
        scratch_shapes=[pltpu.VMEM((tk, QK_DIM), F32), pltpu.VMEM((tk, HEAD_DIM), F32)],
        compiler_params=_params(("arbitrary", "arbitrary")),
    )(q, qt, k, v, lse_row, delta_row, do, dot)


def _rope_tables(pos_col, inv_freq_row, *, name):
    t = pos_col.shape[0]
    tm = min(ROW_TILE, t)

    def body(p_ref, f_ref, c_ref, s_ref):
        ang = p_ref[...].astype(F32) * f_ref[...]
        lane = _lane(ang.shape)
        c_ref[...] = jnp.where(lane < ROPE_DIM, jnp.cos(ang), 0.0)
        sn = jnp.sin(ang)
        s_ref[...] = jnp.where(lane < ROPE_DIM // 2, -sn, jnp.where(lane < ROPE_DIM, sn, 0.0))

    out = pl.BlockSpec((tm, LANES), lambda i: (i, 0))
    return pl.pallas_call(
        body, name=name, grid=(t // tm,),
        in_specs=[pl.BlockSpec((tm, 1), lambda i: (i, 0)), pl.BlockSpec((1, LANES), lambda i: (0, 0))],
        out_specs=[out, out], out_shape=[jax.ShapeDtypeStruct((t, LANES), F32)] * 2,
        compiler_params=_params(("arbitrary",)),
    )(pos_col, inv_freq_row)


def _loss_head(y, target):
    width = y.shape[1]

    def fn(yv, tv):
        e = yv - tv
        part = 0.5 * jnp.sum(jnp.mean(e * e, axis=1, keepdims=True), axis=0, keepdims=True)
        return e * (1.0 / width), jnp.broadcast_to(part, (HALO, LANES))

    return _rowwise(fn, [(y, width, 0), (target, width, 0)], [], [(width, F32)], [(HALO, LANES)], name="loss_head")


def _adamw(w, g, m, v, *, name):
    shape = w.shape
    w2, g2, m2, v2 = (a.reshape(-1, shape[-1]) for a in (w, g, m, v))
    rows, width = w2.shape
    tr = _divisor_tile(rows, max(8, (1 << 19) // max(width, 1)), 8)
    bc1 = 1.0 - ADAM_B1 ** ADAM_STEP
    bc2 = 1.0 - ADAM_B2 ** ADAM_STEP

    def body(w_ref, g_ref, m_ref, v_ref, d_ref, mo_ref, vo_ref):
        gv = g_ref[...]
        mn = ADAM_B1 * m_ref[...] + (1.0 - ADAM_B1) * gv
        vn = ADAM_B2 * v_ref[...] + (1.0 - ADAM_B2) * (gv * gv)
        d_ref[...] = -ADAM_LR * ((mn / bc1) / (jnp.sqrt(vn / bc2) + ADAM_EPS) + ADAM_WD * w_ref[...])
        mo_ref[...] = mn
        vo_ref[...] = vn

    spec = pl.BlockSpec((tr, width), lambda i: (i, 0))
    outs = pl.pallas_call(
        body, name=name, grid=(rows // tr,), in_specs=[spec] * 4, out_specs=[spec] * 3,
        out_shape=[jax.ShapeDtypeStruct((rows, width), F32)] * 3,
        compiler_params=_params(("arbitrary",)),
    )(w2, g2, m2, v2)
    return tuple(o.reshape(shape) for o in outs)


HBM_SPEC = pl.BlockSpec(memory_space=pltpu.HBM)


def _position():
    return lax.axis_index("x"), lax.axis_index("y"), lax.axis_index("c")


def _other_chips(x, y):
    return [(1 - x, y), (x, 1 - y), (1 - x, 1 - y)]


class _Stream:
    def __init__(self, kind, size=0):
        self.kind, self.size = kind, size
        self.parts = 2 if kind == "heads" else 1

    def local(self, ref, k, part):
        if self.kind == "rows":
            return ref.at[:, pl.ds(k * self.size, self.size), :]
        if self.kind == "cols":
            return ref.at[:, :, pl.ds(k * self.size, self.size)]
        if self.kind == "heads":
            return ref.at[:, :, pl.ds(part * N_HEADS * HEAD_DIM + k * HEAD_DIM, HEAD_DIM)]
        if self.kind == "piece":
            return ref.at[k]
        return ref

    def shard(self, ref, part):
        if self.kind == "heads":
            return ref.at[:, :, pl.ds(part * HEAD_DIM, HEAD_DIM)]
        return ref

    def half_local(self, ref, k, part, cc, hd):
        if self.kind == "piece":
            return ref.at[k, pl.ds(cc * hd, hd)]
        return self.local(ref.at[pl.ds(cc * hd, hd)], k, part)


def _remote(src, dst, send_sems, recv_sems, idx, to):
    return pltpu.make_async_remote_copy(src_ref=src, dst_ref=dst, send_sem=send_sems.at[idx],
                                        recv_sem=recv_sems.at[idx], device_id=to, device_id_type=MESH)


def _comm_call(body, ins, out_shapes, n_remote, n_local, *, name):
    scratch = [pltpu.SemaphoreType.DMA((n_remote,)), pltpu.SemaphoreType.DMA((n_remote,))]
    if n_local:
        scratch.append(pltpu.SemaphoreType.DMA((n_local,)))
    return pl.pallas_call(
        body, name=name, in_specs=[HBM_SPEC] * len(ins), out_specs=[HBM_SPEC] * len(out_shapes), out_shape=out_shapes,
        scratch_shapes=scratch, compiler_params=pltpu.CompilerParams(has_side_effects=True),
    )(*ins)


def _gather_chips(shards, streams, out_shapes, *, name):
    n = len(shards)
    hd = shards[0].shape[0] // 2
    flat = [(t, part) for t in range(n) for part in range(streams[t].parts)]
    ns = len(flat)

    def body(*refs):
        s_refs, o_refs = refs[:n], refs[n:2 * n]
        send_sems, recv_sems = refs[2 * n:]
        x, y, c = _position()
        sibling = (x, y, 1 - c)
        chips = _other_chips(x, y)
        me = 2 * x + y
        sent = []
        for s, (t, part) in enumerate(flat):
            st = streams[t]
            sent.append(_remote(st.shard(s_refs[t], part), st.local(o_refs[t], me, part), send_sems, recv_sems,
                                6 * ns + s, sibling))
            sent[-1].start()
            src = st.shard(s_refs[t].at[pl.ds(c * hd, hd)], part)
            for j, (cx, cy) in enumerate(chips[:2]):
                sent.append(_remote(src, st.half_local(o_refs[t], me, part, c, hd), send_sems, recv_sems,
                                    3 * s + j, (cx, cy, c)))
                sent[-1].start()
        for s, (t, part) in enumerate(flat):
            st = streams[t]
            for j, (cx, cy) in enumerate(chips[:2]):
                blk = st.half_local(o_refs[t], 2 * cx + cy, part, c, hd)
                _remote(blk, blk, send_sems, recv_sems, 3 * s + j, (x, y, c)).wait_recv()
                sent.append(_remote(blk, blk, send_sems, recv_sems, 3 * ns + 3 * s + j, sibling))
                sent[-1].start()
                if s % 2 == j:
                    ox, oy = chips[1 - j]
                    sent.append(_remote(blk, blk, send_sems, recv_sems, 3 * s + 2, (ox, oy, c)))
                    sent[-1].start()
        for s, (t, part) in enumerate(flat):
            st = streams[t]
            cx, cy = chips[2]
            blk = st.half_local(o_refs[t], 2 * cx + cy, part, c, hd)
            _remote(blk, blk, send_sems, recv_sems, 3 * s + 2, (x, y, c)).wait_recv()
            sent.append(_remote(blk, blk, send_sems, recv_sems, 3 * ns + 3 * s + 2, sibling))
            sent[-1].start()
        for s, (t, part) in enumerate(flat):
            st = streams[t]
            for j, (cx, cy) in enumerate(chips):
                blk = st.half_local(o_refs[t], 2 * cx + cy, part, 1 - c, hd)
                _remote(blk, blk, send_sems, recv_sems, 3 * ns + 3 * s + j, (x, y, c)).wait_recv()
            own = st.local(o_refs[t], me, part)
            _remote(own, own, send_sems, recv_sems, 6 * ns + s, (x, y, c)).wait_recv()
        for cp in sent:
            cp.wait_send()

    return _comm_call(body, shards, out_shapes, 7 * ns, 0, name=name)


def _sibling_take_other_half(gs, *, name):
    n = len(gs)
    hd = gs[0].shape[0] // 2

    def body(*refs):
        g_refs, o_refs = refs[:n], refs[n:2 * n]
        send_sems, recv_sems = refs[2 * n:]
        x, y, c = _position()
        copies = [_remote(g_refs[t].at[pl.ds((1 - c) * hd, hd)], o_refs[t], send_sems, recv_sems, t, (x, y, 1 - c))
                  for t in range(n)]
        for cp in copies:
            cp.start()
        for cp in copies:
            cp.wait()

    outs = [jax.ShapeDtypeStruct((hd,) + g.shape[1:], g.dtype) for g in gs]
    return _comm_call(body, gs, outs, n, 0, name=name)


def _chips_exchange(ps, streams, shard_shapes, *, name):
    n = len(ps)
    flat = [(t, part) for t in range(n) for part in range(streams[t].parts)]

    def body(*refs):
        p_refs, o_refs = refs[:n], refs[n:2 * n]
        send_sems, recv_sems = refs[2 * n:]
        x, y, c = _position()
        copies = []
        for s, (t, part) in enumerate(flat):
            st = streams[t]
            for j, (cx, cy) in enumerate(_other_chips(x, y)):
                copies.append(_remote(st.local(p_refs[t], 2 * cx + cy, part), st.shard(o_refs[t].at[j], part),
                                      send_sems, recv_sems, 3 * s + j, (cx, cy, c)))
        for cp in copies:
            cp.start()
        for cp in copies:
            cp.wait()

    outs = [jax.ShapeDtypeStruct((3,) + tuple(shp), p.dtype) for p, shp in zip(ps, shard_shapes)]
    return _comm_call(body, ps, outs, 3 * len(flat), 0, name=name)


def _sibling_join_halves(bufs, *, name):
    n = len(bufs)
    hd = bufs[0].shape[0] // 2

    def body(*refs):
        o_refs = refs[n:2 * n]
        send_sems, recv_sems = refs[2 * n:]
        x, y, c = _position()
        sent = []
        for t in range(n):
            mine = o_refs[t].at[pl.ds(c * hd, hd)]
            sent.append(_remote(mine, mine, send_sems, recv_sems, t, (x, y, 1 - c)))
            sent[-1].start()
        for t in range(n):
            theirs = o_refs[t].at[pl.ds((1 - c) * hd, hd)]
            _remote(theirs, theirs, send_sems, recv_sems, t, (x, y, c)).wait_recv()
        for cp in sent:
            cp.wait_send()

    return pl.pallas_call(
        body, name=name, in_specs=[HBM_SPEC] * n, out_specs=[HBM_SPEC] * n,
        out_shape=[jax.ShapeDtypeStruct(b.shape, b.dtype) for b in bufs],
        scratch_shapes=[pltpu.SemaphoreType.DMA((n,)), pltpu.SemaphoreType.DMA((n,))],
        input_output_aliases={t: t for t in range(n)},
        compiler_params=pltpu.CompilerParams(has_side_effects=True),
    )(*bufs)


def _row_tile(rows, width):
    return _divisor_tile(rows, max(16, (1 << 19) // width), 16)


def _add_own_half(g, got, c_idx, out_dtype, *, name):
    hd, r, w = got.shape
    tr = _row_tile(r, w)

    def body(c_ref, g_ref, a_ref, o_ref):
        o_ref[...] = (g_ref[...].astype(F32) + a_ref[...].astype(F32)).astype(o_ref.dtype)

    return pl.pallas_call(
        body, name=name,
        grid_spec=pltpu.PrefetchScalarGridSpec(
            num_scalar_prefetch=1, grid=(hd, r // tr),
            in_specs=[pl.BlockSpec((None, None, tr, w), lambda l, i, c_ref: (c_ref[0], l, i, 0)),
                      pl.BlockSpec((None, tr, w), lambda l, i, c_ref: (l, i, 0))],
            out_specs=pl.BlockSpec((None, tr, w), lambda l, i, c_ref: (l, i, 0))),
        out_shape=jax.ShapeDtypeStruct((hd, r, w), out_dtype),
        compiler_params=_params(("arbitrary", "arbitrary")),
    )(c_idx, g.reshape((2, hd) + g.shape[1:]), got)


def _sum_chips(p, got, place, stream, *, name):
    _, hd, rs, cs = got.shape
    wb = HEAD_DIM if stream.kind == "heads" else cs
    tr = _row_tile(rs, wb)
    kind, size = stream.kind, stream.size

    def own_index(l, i, g, k_ref, c_ref):
        k = k_ref[0]
        if kind == "rows":
            return (l, k * (size // tr) + i, 0)
        if kind == "cols":
            return (l, i, k)
        if kind == "heads":
            return (l, i, g * N_HEADS + k)
        if kind == "piece":
            return (k, l, i, 0)
        return (l, i, 0)

    own_blk = (None, None, tr, wb) if kind == "piece" else (None, tr, wb)

    def body(k_ref, c_ref, p_ref, fx_ref, fy_ref, fxy_ref, o_ref):
        f = lambda r: r[...].astype(F32)
        o_ref[...] = (f(p_ref) + f(fy_ref)) + (f(fx_ref) + f(fxy_ref))

    def rel(j):
        return pl.BlockSpec((None, None, tr, wb), functools.partial(lambda l, i, g, k_ref, c_ref, j: (j, l, i, g), j=j))

    return pl.pallas_call(
        body, name=name,
        grid_spec=pltpu.PrefetchScalarGridSpec(
            num_scalar_prefetch=2, grid=(hd, rs // tr, stream.parts),
            in_specs=[pl.BlockSpec(own_blk, own_index), rel(0), rel(1), rel(2)],
            out_specs=pl.BlockSpec((None, tr, wb), lambda l, i, g, k_ref, c_ref: (c_ref[0] * hd + l, i, g))),
        out_shape=jax.ShapeDtypeStruct((2 * hd, rs, cs), F32),
        compiler_params=_params(("arbitrary", "arbitrary", "arbitrary")),
    )(place[0], place[1], p, got, got, got)


MATRICES = ("w_in", "w_uq", "w_ukv", "w_out", "w_gate_up", "w_down", "w_ple", "w_ple_gate", "conv_w")
VECTORS = ("a_log", "dt_bias", "gdn_norm_g", "q_norm_g", "kv_norm_g", "ln1_g", "ln1_b", "ln2_g", "ln2_b")
WEIGHTS = ("w_in", "conv_w", "a_log", "dt_bias", "gdn_norm_g", "q_norm_g", "w_uq", "kv_norm_g", "w_ukv", "w_out",
           "ln1_g", "ln1_b", "w_gate_up", "w_down", "ln2_g", "ln2_b", "w_ple", "w_ple_gate")
ROW_SHARDED = ("w_out", "w_down", "w_ple_gate")
N_CHIPS = 4


def _stream_of(name, shard_shape):
    if name in ("w_in", "w_uq"):
        return _Stream("piece")
    if name == "w_ukv":
        return _Stream("heads")
    if name in ROW_SHARDED:
        return _Stream("rows", shard_shape[0])
    return _Stream("cols", shard_shape[1])


def _pack_vectors(vecs, depth):
    flat = jnp.concatenate([vecs[n].reshape(depth, -1) for n in VECTORS], axis=1)
    pad = jnp.zeros((depth, VEC_ROWS * LANES - flat.shape[1]), F32)
    return jnp.concatenate([flat, pad], axis=1).reshape(depth, VEC_ROWS, LANES)


def _unpack_vectors(packed, shapes):
    depth = packed.shape[0]
    flat = packed.reshape(depth, VEC_ROWS * LANES)
    out, off = {}, 0
    for n in VECTORS:
        out[n] = flat[:, off:off + shapes[n][1]]
        off += shapes[n][1]
    return out


VEC_ROWS = 40


class _Dims:
    def __init__(self, d_model, in_width, q_lora, kv_lora, d_ff2, ple_dim):
        self.d = d_model
        self.hw = N_HEADS * HEAD_DIM
        self.in_width = in_width
        self.q_lora, self.kv_lora = q_lora, kv_lora
        self.ff2 = d_ff2
        self.ple = ple_dim
        self.c_kv0 = 4 * self.hw
        self.c_q0 = self.c_kv0 + kv_lora
        self.misc0 = self.c_q0 + q_lora
        self.h_width = self.misc0 + LANES
        assert self.c_kv0 % kv_lora == 0 and self.c_q0 % q_lora == 0 and self.misc0 % LANES == 0
        self.g_beta = 4 * self.hw
        self.g_a = self.g_beta + N_HEADS
        self.g_cq = self.g_a + N_HEADS
        self.g_ckv = self.g_cq + q_lora
        self.g_kr = self.g_ckv + kv_lora
        assert self.g_kr + ROPE_DIM == in_width

    def w_in_local(self, w):
        pad = jnp.zeros(w.shape[:-1] + (self.h_width - self.in_width,), w.dtype)
        return jnp.concatenate([w[..., :self.g_beta], w[..., self.g_ckv:self.g_kr], w[..., self.g_cq:self.g_ckv],
                                w[..., self.g_kr:], w[..., self.g_beta:self.g_cq], pad], axis=-1)

    def w_in_global(self, d):
        m = self.misc0
        return jnp.concatenate([d[..., :self.c_kv0], d[..., m + MISC_BETA0:m + MISC_A0 + N_HEADS],
                                d[..., self.c_q0:self.misc0], d[..., self.c_kv0:self.c_q0], d[..., m:m + ROPE_DIM]],
                               axis=-1)

    def w_uq_local(self, w):
        r = w.reshape(w.shape[:-1] + (N_HEADS, HEAD_DIM + ROPE_DIM))
        r = jnp.pad(r, [(0, 0)] * (r.ndim - 1) + [(0, QK_DIM - HEAD_DIM - ROPE_DIM)])
        return r.reshape(w.shape[:-1] + (N_HEADS * QK_DIM,))

    def w_uq_global(self, d):
        r = d.reshape(d.shape[:-1] + (N_HEADS, QK_DIM))[..., :HEAD_DIM + ROPE_DIM]
        return r.reshape(d.shape[:-1] + (N_HEADS * (HEAD_DIM + ROPE_DIM),))


def _lane_padded(n):
    return -(-n // LANES) * LANES


def _pad_lanes(a):
    pad = _lane_padded(a.shape[-1]) - a.shape[-1]
    return a if pad == 0 else jnp.pad(a, [(0, 0)] * (a.ndim - 1) + [(0, pad)])


def _lane_row(vec, lane0):
    pad = LANES - lane0 - vec.shape[0]
    return jnp.concatenate([jnp.zeros((lane0,), F32), vec.astype(F32), jnp.zeros((pad,), F32)])[None, :]


def _layer_fwd(dm, alpha, x, xb, p_i, cos_t, sin_t, wl, tag):
    d, hw = dm.d, dm.hw
    nm = lambda s: f"{s}_{tag}"
    mm = functools.partial(_matmul, layer=wl["layer"])
    h = mm(xb, wl["w_in"], dims="nn", name=nm("f_in"))
    misc_cb = dm.misc0 // LANES

    u = _conv_fwd(h, wl["conv_w"], 3 * hw, name=nm("f_conv"))
    qn, kn, vg, gb = _rowwise(_gdn_act, [(u, 3 * hw, 0), (h, LANES, misc_cb)], [wl["alog_row"], wl["dtb_row"]],
                              [(hw, F32), (hw, F32), (hw, F32), (LANES, F32)], name=nm("f_gdn_act"))
    prep, tinv = _gdn_prep_fwd(qn, kn, vg, gb, name=nm("f_gdn_prep"))
    o_gdn, s_all = _gdn_seq_fwd(prep, name=nm("f_gdn_seq"))
    (og,) = _rowwise(lambda o, z, g: (_gdn_out(o, z, g),), [(o_gdn, hw, 0), (h, hw, 3)], [wl["gn_row"]],
                     [(hw, BF16)], name=nm("f_gdn_out"))

    cqn, ckvn = _rowwise(_mla_norm, [(h, dm.kv_lora, dm.c_kv0 // dm.kv_lora), (h, dm.q_lora, dm.c_q0 // dm.q_lora)],
                         [wl["kvg_row"], wl["qg_row"]], [(dm.q_lora, BF16), (dm.kv_lora, BF16)], name=nm("f_mla_norm"))
    qm = mm(cqn, wl["w_uq"], dims="nn", name=nm("f_uq"))
    kvm = mm(ckvn, wl["w_ukv"], dims="nn", name=nm("f_ukv"))
    scale = (HEAD_DIM + ROPE_DIM) ** -0.5
    qk_fn = functools.partial(_mla_qk, scale)
    qa, ka, va = _rowwise(qk_fn, [(qm, N_HEADS * QK_DIM, 0), (kvm, 2 * hw, 0), (h, LANES, misc_cb),
                                  (cos_t, LANES, 0), (sin_t, LANES, 0)], [],
                          [(N_HEADS * QK_DIM, BF16), (N_HEADS * QK_DIM, BF16), (hw, BF16)], name=nm("f_mla_qk"))
    kt = _blocked_transpose(ka, QK_DIM)
    ones = jnp.ones((va.shape[0], HEAD_DIM), va.dtype)
    v1 = jnp.concatenate([part for vh in _heads(va) for part in (vh, ones)], axis=1)
    o_mla, lse = _attn_fwd(qa, kt, v1, name=nm("f_attn"))

    om = jnp.concatenate([og, o_mla], axis=1)
    mix = mm(om, wl["w_out"], dims="nn", name=nm("f_out"))
    ln1 = lambda xv, yv, g, b: (_layer_norm(alpha * xv + yv, g, b),) * 2
    x1, x1b = _rowwise(ln1, [(x, d, 0), (mix, d, 0)], [wl["ln1_g"], wl["ln1_b"]], [(d, F32), (d, BF16)], name=nm("f_ln1"))

    gu = mm(x1b, wl["w_gate_up"], dims="nn", name=nm("f_gate_up"), out_dtype=BF16)
    dn, act = mm(gu, wl["w_down"], dims="nn", name=nm("f_down"), a_gated=True, tm=GATED_TILE)
    x2, x2b = _rowwise(ln1, [(x1, d, 0), (dn, d, 0)], [wl["ln2_g"], wl["ln2_b"]], [(d, F32), (d, BF16)], name=nm("f_ln2"))

    pg = mm(x2b, wl["w_ple_gate"], dims="nn", name=nm("f_ple_gate"))
    pe = mm(p_i, wl["w_ple"], dims="nn", name=nm("f_ple"))
    out, outb = _rowwise(lambda a, b, c_: (_ple_out(a, b, c_),) * 2, [(x2, d, 0), (pg, d, 0), (pe, d, 0)], [],
                         [(d, F32), (d, BF16)], name=nm("f_ple_out"))
    saved = dict(x=x, xb=xb, p_i=p_i, h=h, u=u, qn=qn, kn=kn, vg=vg, gb=gb, prep=prep, tinv=tinv, s_all=s_all, o_gdn=o_gdn, cqn=cqn, ckvn=ckvn,
                 qm=qm, kvm=kvm, qa=qa, ka=ka, va=va, o_mla=o_mla, lse=lse, om=om, mix=mix, x1=x1, x1b=x1b, gu=gu,
                 act=act, dn=dn, x2=x2, x2b=x2b, pg=pg, pe=pe)
    return out, outb, saved


def _layer_bwd(dm, alpha, dout, sv, cos_t, sin_t, wl, gbuf, tag):
    d, hw = dm.d, dm.hw
    t = dout.shape[0]
    nm = lambda s: f"{s}_{tag}"
    gr = {}
    gbuf = dict(gbuf)
    misc_cb = dm.misc0 // LANES
    mm = functools.partial(_matmul, layer=wl["layer"])

    def wgrad(name_, a, g):
        gbuf[name_] = mm(a, g, dims="tn", name=nm("b_" + name_), into=gbuf[name_], tm=1408, tn=1408, tk=1024,
                         out_dtype=BF16)

    dx2_a, dpg, dpe = _rowwise(_vjp_fn(_ple_out, 3, 1), [(sv["x2"], d, 0), (sv["pg"], d, 0), (sv["pe"], d, 0), (dout, d, 0)],
                               [], [(d, F32), (d, BF16), (d, BF16)], name=nm("b_ple_out"))
    wgrad("w_ple", sv["p_i"], dpe)
    wgrad("w_ple_gate", sv["x2b"], dpg)
    dx2 = mm(dpg, wl["w_ple_gate"], dims="nt", c=dx2_a, name=nm("b_x2"))

    def ln_bwd(xv, yv, ct, g, b):
        _, pull = jax.vjp(lambda a_, b_, c_, d_: _layer_norm(alpha * a_ + b_, c_, d_), xv, yv, g, b)
        return pull(ct)

    dx1_a, ddn, gr["ln2_g"], gr["ln2_b"] = _rowwise(
        ln_bwd, [(sv["x1"], d, 0), (sv["dn"], d, 0), (dx2, d, 0)], [wl["ln2_g"], wl["ln2_b"]],
        [(d, F32), (d, BF16)], [(1, d), (1, d)], name=nm("b_ln2"))
    wgrad("w_down", sv["act"], ddn)
    dact = mm(ddn, wl["w_down"], dims="nt", name=nm("b_act"), out_dtype=BF16)
    (dgu,) = _rowwise(_vjp_fn(_swiglu, 1, 1), [(sv["gu"], dm.ff2, 0), (dact, dm.ff2 // 2, 0)], [], [(dm.ff2, BF16)],
                      name=nm("b_swiglu"))
    wgrad("w_gate_up", sv["x1b"], dgu)
    dx1 = mm(dgu, wl["w_gate_up"], dims="nt", c=dx1_a, name=nm("b_x1"))

    dx_a, dmix, gr["ln1_g"], gr["ln1_b"] = _rowwise(
        ln_bwd, [(sv["x"], d, 0), (sv["mix"], d, 0), (dx1, d, 0)], [wl["ln1_g"], wl["ln1_b"]],
        [(d, F32), (d, BF16)], [(1, d), (1, d)], name=nm("b_ln1"))
    wgrad("w_out", sv["om"], dmix)
    dom = mm(dmix, wl["w_out"], dims="nt", name=nm("b_om"))

    nq = t // min(ATT_TILE, t)
    delta = _attn_delta(dom, sv["o_mla"], name=nm("b_attn_delta"))
    lse_row = sv["lse"].reshape(N_HEADS, nq, 1, t // nq)
    delta_row = delta[:, :N_HEADS].T.reshape(N_HEADS, nq, 1, t // nq)
    do_b = dom[:, hw:].astype(BF16)
    dka, dva, dqa = _attn_bwd(sv["qa"], _blocked_transpose(sv["qa"], QK_DIM), sv["ka"], sv["va"], lse_row, delta_row,
                              do_b, _blocked_transpose(do_b, HEAD_DIM), name=nm("b_attn"))
    scale = (HEAD_DIM + ROPE_DIM) ** -0.5
    qk_fn = functools.partial(_mla_qk, scale)

    def qk_bwd(qm, kvm, misc, cs, sn, g_q, g_k, g_v):
        _, pull = jax.vjp(lambda a, b, c_: qk_fn(a, b, c_, cs, sn), qm, kvm, misc)
        return pull((g_q, g_k, g_v))

    dqm, dkvm, dmisc_rope = _rowwise(
        qk_bwd, [(sv["qm"], N_HEADS * QK_DIM, 0), (sv["kvm"], 2 * hw, 0), (sv["h"], LANES, misc_cb), (cos_t, LANES, 0),
                 (sin_t, LANES, 0), (dqa, N_HEADS * QK_DIM, 0), (dka, N_HEADS * QK_DIM, 0), (dva, hw, 0)], [],
        [(N_HEADS * QK_DIM, BF16), (2 * hw, BF16), (LANES, F32)], name=nm("b_mla_qk"))
    wgrad("w_uq", sv["cqn"], dqm)
    wgrad("w_ukv", sv["ckvn"], dkvm)
    dcqn = mm(dqm, wl["w_uq"], dims="nt", name=nm("b_cqn"))
    dckvn = mm(dkvm, wl["w_ukv"], dims="nt", name=nm("b_ckvn"))

    def norm_bwd(ckv, cq, g_q, g_kv, kvg, qg):
        _, pull = jax.vjp(_mla_norm, ckv, cq, kvg, qg)
        return pull((g_q, g_kv))

    dckv, dcq, gr["kvg_row"], gr["qg_row"] = _rowwise(
        norm_bwd, [(sv["h"], dm.kv_lora, dm.c_kv0 // dm.kv_lora), (sv["h"], dm.q_lora, dm.c_q0 // dm.q_lora),
                   (dcqn, dm.q_lora, 0), (dckvn, dm.kv_lora, 0)], [wl["kvg_row"], wl["qg_row"]],
        [(dm.kv_lora, BF16), (dm.q_lora, BF16)], [(1, dm.kv_lora), (1, dm.q_lora)], name=nm("b_mla_norm"))

    def gout_bwd(o, z, g_o, gn):
        _, pull = jax.vjp(_gdn_out, o, z, gn)
        return pull(g_o)

    do_gdn, dz, gr["gn_row"] = _rowwise(gout_bwd, [(sv["o_gdn"], hw, 0), (sv["h"], hw, 3), (dom, hw, 0)], [wl["gn_row"]],
                                        [(hw, F32), (hw, BF16)], [(1, HEAD_DIM)], name=nm("b_gdn_out"))
    dprep = _gdn_seq_bwd(sv["prep"], sv["s_all"], do_gdn, name=nm("b_gdn_seq"))
    dqn, dkn, dvg, dgb = _gdn_prep_bwd(sv["qn"], sv["kn"], sv["vg"], sv["gb"], sv["tinv"], dprep, name=nm("b_gdn_prep"))

    def act_bwd(u, misc, g_q, g_k, g_v, g_gb, g_rope, alog, dtb):
        _, pull = jax.vjp(_gdn_act, u, misc, alog, dtb)
        du_, dmisc_, dalog_, ddtb_ = pull((g_q, g_k, g_v, g_gb))
        return du_, dmisc_ + g_rope, dalog_, ddtb_

    du, dmisc, gr["alog_row"], gr["dtb_row"] = _rowwise(
        act_bwd, [(sv["u"], 3 * hw, 0), (sv["h"], LANES, misc_cb), (dqn, hw, 0), (dkn, hw, 0), (dvg, hw, 0),
                  (dgb, LANES, 0), (dmisc_rope, LANES, 0)], [wl["alog_row"], wl["dtb_row"]],
        [(3 * hw, F32), (LANES, BF16)], [(1, LANES), (1, LANES)], name=nm("b_gdn_act"))
    dqkv, dconv = _conv_bwd(du, sv["h"], wl["conv_w"], 3 * hw, name=nm("b_conv"))
    gr["conv_w"] = dconv[:CONV_W]

    dh = jnp.concatenate([dqkv, dz, dckv, dcq, dmisc], axis=1)
    wgrad("w_in", sv["xb"], dh)
    dx = mm(dh, wl["w_in"], dims="nt", c=dx_a, name=nm("b_x"), tk=1408)
    return dx, gbuf, gr


LOCAL_MATRICES = ("w_in", "w_uq", "w_ukv", "w_out", "w_gate_up", "w_down", "w_ple", "w_ple_gate")


def _layer_weights(mats, vecs, layer):
    wl = {n: mats[n] for n in LOCAL_MATRICES}
    wl["layer"] = layer
    wl["conv_w"] = mats["conv_w"][layer]
    wl["alog_row"] = _lane_row(vecs["a_log"][layer], MISC_A0)
    wl["dtb_row"] = _lane_row(vecs["dt_bias"][layer], MISC_A0)
    wl["gn_row"] = vecs["gdn_norm_g"][layer][None, :]
    wl["qg_row"] = vecs["q_norm_g"][layer][None, :]
    wl["kvg_row"] = vecs["kv_norm_g"][layer][None, :]
    for n in ("ln1_g", "ln1_b", "ln2_g", "ln2_b"):
        wl[n] = vecs[n][layer][None, :]
    return wl


def _vector_grads(gr):
    out = {"a_log": gr["alog_row"][0, MISC_A0:MISC_A0 + N_HEADS], "dt_bias": gr["dtb_row"][0, MISC_A0:MISC_A0 + N_HEADS],
           "gdn_norm_g": gr["gn_row"][0], "q_norm_g": gr["qg_row"][0], "kv_norm_g": gr["kvg_row"][0]}
    for n in ("ln1_g", "ln1_b", "ln2_g", "ln2_b"):
        out[n] = gr[n][0]
    return out


def _local_step(dm, x, p, positions, target, mats, vecs):
    depth = p.shape[0]
    alpha = (2.0 * depth) ** 0.25
    freq = ROPE_THETA ** (-jnp.arange(0, ROPE_DIM, 2, dtype=F32) / ROPE_DIM)
    inv_freq_row = _lane_row(jnp.concatenate([freq, freq]), 0)
    cos_t, sin_t = _rope_tables(positions.reshape(-1, 1), inv_freq_row, name="rope_tables")

    wls = [_layer_weights(mats, vecs, i) for i in range(depth)]
    saved = []
    cur, cur_b = x, x
    for i in range(depth):
        cur, cur_b, sv = _layer_fwd(dm, alpha, cur, cur_b, p[i], cos_t, sin_t, wls[i], f"l{i}")
        saved.append(sv)
    dy, loss_blk = _loss_head(cur, target)
    gbuf = {n: depth for n in LOCAL_MATRICES}
    conv_g, vec_g = [None] * depth, [None] * depth
    for i in reversed(range(depth)):
        dy, gbuf, gr = _layer_bwd(dm, alpha, dy, saved[i], cos_t, sin_t, wls[i], gbuf, f"l{i}")
        conv_g[i] = gr["conv_w"]
        vec_g[i] = _vector_grads(gr)
    vec_grads = {n: jnp.stack([vec_g[i][n] for i in range(depth)]) for n in VECTORS}
    return loss_blk[0, 0], dy, gbuf, jnp.stack(conv_g), vec_grads


def kernel(x, p, positions, w_in, conv_w, a_log, dt_bias, gdn_norm_g, q_norm_g, w_uq, kv_norm_g, w_ukv, w_out, ln1_g, ln1_b, w_gate_up, w_down, ln2_g, ln2_b, w_ple, w_ple_gate, loss_target, m_w_in, m_conv_w, m_a_log, m_dt_bias, m_gdn_norm_g, m_q_norm_g, m_w_uq, m_kv_norm_g, m_w_ukv, m_w_out, m_ln1_g, m_ln1_b, m_w_gate_up, m_w_down, m_ln2_g, m_ln2_b, m_w_ple, m_w_ple_gate, v_w_in, v_conv_w, v_a_log, v_dt_bias, v_gdn_norm_g, v_q_norm_g, v_w_uq, v_kv_norm_g, v_w_ukv, v_w_out, v_ln1_g, v_ln1_b, v_w_gate_up, v_w_down, v_ln2_g, v_ln2_b, v_w_ple, v_w_ple_gate):
    w = dict(w_in=w_in, conv_w=conv_w, a_log=a_log, dt_bias=dt_bias, gdn_norm_g=gdn_norm_g, q_norm_g=q_norm_g, w_uq=w_uq,
             kv_norm_g=kv_norm_g, w_ukv=w_ukv, w_out=w_out, ln1_g=ln1_g, ln1_b=ln1_b, w_gate_up=w_gate_up, w_down=w_down,
             ln2_g=ln2_g, ln2_b=ln2_b, w_ple=w_ple, w_ple_gate=w_ple_gate)
    m = dict(w_in=m_w_in, conv_w=m_conv_w, a_log=m_a_log, dt_bias=m_dt_bias, gdn_norm_g=m_gdn_norm_g, q_norm_g=m_q_norm_g,
             w_uq=m_w_uq, kv_norm_g=m_kv_norm_g, w_ukv=m_w_ukv, w_out=m_w_out, ln1_g=m_ln1_g, ln1_b=m_ln1_b,
             w_gate_up=m_w_gate_up, w_down=m_w_down, ln2_g=m_ln2_g, ln2_b=m_ln2_b, w_ple=m_w_ple, w_ple_gate=m_w_ple_gate)
    v = dict(w_in=v_w_in, conv_w=v_conv_w, a_log=v_a_log, dt_bias=v_dt_bias, gdn_norm_g=v_gdn_norm_g, q_norm_g=v_q_norm_g,
             w_uq=v_w_uq, kv_norm_g=v_kv_norm_g, w_ukv=v_w_ukv, w_out=v_w_out, ln1_g=v_ln1_g, ln1_b=v_ln1_b,
             w_gate_up=v_w_gate_up, w_down=v_w_down, ln2_g=v_ln2_g, ln2_b=v_ln2_b, w_ple=v_w_ple, w_ple_gate=v_w_ple_gate)
    depth = w_in.shape[0]
    assert depth % 2 == 0
    hd = depth // 2
    dm = _Dims(x.shape[2], N_CHIPS * w_in.shape[2], w_uq.shape[1], w_ukv.shape[1], N_CHIPS * w_gate_up.shape[2], p.shape[3])
    cx, cy, cc = lax.axis_index("x"), lax.axis_index("y"), lax.axis_index("c")
    chip = 2 * cx + cy

    g_streams = [_stream_of(n, w[n].shape[1:]) for n in MATRICES]
    shards = [w[n] if n == "conv_w" else w[n].astype(BF16) for n in MATRICES]
    shards = [_pad_lanes(s) if st.kind == "piece" else s for s, st in zip(shards, g_streams)]
    g_shapes = []
    for s, st in zip(shards, g_streams):
        if st.kind == "piece":
            shape = (N_CHIPS,) + s.shape
        elif st.kind == "rows":
            shape = (depth, N_CHIPS * s.shape[1], s.shape[2])
        else:
            shape = (depth, s.shape[1], N_CHIPS * s.shape[2])
        g_shapes.append(jax.ShapeDtypeStruct(shape, s.dtype))
    mats = dict(zip(MATRICES, _gather_chips(shards, g_streams, g_shapes, name="gather_weights")))
    for n, to_local in (("w_in", dm.w_in_local), ("w_uq", dm.w_uq_local)):
        pieces = jnp.moveaxis(mats[n][..., :w[n].shape[2]], 0, 2)
        mats[n] = to_local(pieces.reshape(pieces.shape[:2] + (-1,)))
    vecs = {n: w[n] for n in VECTORS}

    loss_local, grad_x, gbuf, conv_g, vec_g = _local_step(dm, x[0], p[:, 0], positions[0], loss_target[0], mats, vecs)
    loss = lax.psum(loss_local, ("x", "y", "c"))

    names = list(LOCAL_MATRICES) + ["conv_w", "vectors"]
    gs = [gbuf[n] for n in LOCAL_MATRICES] + [conv_g, _pack_vectors(vec_g, depth)]
    wire = [BF16] * len(LOCAL_MATRICES) + [F32, F32]
    r_streams = [_stream_of(n, w[n].shape[1:]) for n in LOCAL_MATRICES]
    r_streams += [_stream_of("conv_w", w["conv_w"].shape[1:]), _Stream("whole")]
    shard_shapes = [(hd, w[n].shape[1], _lane_padded(w[n].shape[2])) if st.kind == "piece" else (hd,) + w[n].shape[1:]
                    for n, st in zip(LOCAL_MATRICES, r_streams)]
    shard_shapes += [(hd,) + w["conv_w"].shape[1:], (hd, VEC_ROWS, LANES)]
    c_idx = cc.reshape(1).astype(jnp.int32)
    place = (chip.reshape(1).astype(jnp.int32), c_idx)
    from_sibling = _sibling_take_other_half(gs, name="reduce_sibling")
    chip_sum = [_add_own_half(g, a, c_idx, dt, name=f"reduce_add_{n}")
                for g, a, dt, n in zip(gs, from_sibling, wire, names)]
    for i, n in enumerate(names):
        if r_streams[i].kind == "piece":
            glob = dm.w_in_global(chip_sum[i]) if n == "w_in" else dm.w_uq_global(chip_sum[i])
            glob = glob.reshape(glob.shape[:2] + (N_CHIPS, glob.shape[2] // N_CHIPS))
            chip_sum[i] = jnp.moveaxis(_pad_lanes(glob), 2, 0)
    from_chips = _chips_exchange(chip_sum, r_streams, shard_shapes, name="reduce_chips")
    halves = [_sum_chips(ps, got, place, st, name=f"reduce_sum_{n}")
              for ps, got, st, n in zip(chip_sum, from_chips, r_streams, names)]
    joined = dict(zip(names, _sibling_join_halves(halves, name="reduce_join")))
    joined.update(_unpack_vectors(joined.pop("vectors"), {n: w[n].shape for n in VECTORS}))

    grad_w, delta_w, new_m, new_v = {}, {}, {}, {}
    for n in WEIGHTS:
        grad_w[n] = joined[n][..., :w[n].shape[-1]]
        delta_w[n], new_m[n], new_v[n] = _adamw(w[n], grad_w[n], m[n], v[n], name=f"adamw_{n}")
    return (loss, grad_x[None], *[grad_w[n] for n in WEIGHTS], *[delta_w[n] for n in WEIGHTS],
            *[new_m[n] for n in WEIGHTS], *[new_v[n] for n in WEIGHTS])
```

```python
import functools

import jax
import jax.numpy as jnp
from jax import lax
from jax.experimental import pallas as pl
from jax.experimental.pallas import tpu as pltpu

F32 = jnp.float32
BF16 = jnp.bfloat16
MESH = pl.DeviceIdType.MESH

CHUNK = 64
N_HEADS = 4
HEAD_DIM = 128
ROPE_DIM = 64
ROPE_THETA = 10000.0
LN_EPS = 1e-5
RMS_EPS = 1e-6
ADAM_LR, ADAM_B1, ADAM_B2, ADAM_EPS, ADAM_WD, ADAM_STEP = 0.001, 0.9, 0.999, 1e-08, 0.01, 10

LANES = 128
VMEM_LIMIT = 48 * 1024 * 1024
ROW_TILE = 512
WIDE_ROW_TILE = 256
WIDE_COLS = 2048
NARROW_ROW_TILE = 1024
NARROW_VMEM_BYTES = 20 * 1024 * 1024
GATED_TILE = 512
SUB_ROWS = 16
MAX_SUB_ROWS = 64
VREG_FILE_ELEMS = 64 * 8 * LANES

MISC_BETA0 = ROPE_DIM
MISC_A0 = ROPE_DIM + N_HEADS

NN = (((1,), (0,)), ((), ()))
NT = (((1,), (1,)), ((), ()))
TN = (((0,), (0,)), ((), ()))


def _params(sem=None):
    return pltpu.CompilerParams(dimension_semantics=sem, vmem_limit_bytes=VMEM_LIMIT)


def _divisor_tile(dim, target, unit):
    best = None
    t = unit
    while t <= min(dim, target):
        if dim % t == 0:
            best = t
        t += unit
    return best if best is not None else dim


BATCHED = {NN: (((2,), (1,)), ((0,), (0,))), NT: (((2,), (2,)), ((0,), (0,))), TN: (((1,), (1,)), ((0,), (0,)))}


def _make_dots():
    def raw(a, b, dims):
        if a.ndim == 3:
            dims = BATCHED[dims]
        return lax.dot_general(a.astype(BF16), b.astype(BF16), dims, preferred_element_type=F32)

    @jax.custom_vjp
    def nn(a, b):
        return raw(a, b, NN)

    @jax.custom_vjp
    def nt(a, b):
        return raw(a, b, NT)

    @jax.custom_vjp
    def tn(a, b):
        return raw(a, b, TN)

    nn.defvjp(lambda a, b: (raw(a, b, NN), (a, b)), lambda r, g: (nt(g, r[1]), tn(r[0], g)))
    nt.defvjp(lambda a, b: (raw(a, b, NT), (a, b)), lambda r, g: (nn(g, r[1]), tn(g, r[0])))
    tn.defvjp(lambda a, b: (raw(a, b, TN), (a, b)), lambda r, g: (nt(r[1], g), nn(r[0], g)))
    return nn, nt, tn


_nn, _nt, _tn = _make_dots()


def _matmul(a, b, *, dims, name, c=None, out_dtype=F32, tm=1024, tn=1408, tk=1408, layer=None, into=None,
            a_gated=False):
    b_shape = b.shape[-2:]
    a_shape = (a.shape[0], a.shape[1] // 2) if a_gated else a.shape
    if dims == "nn":
        (m, k), (k2, n) = a_shape, b_shape
    elif dims == "nt":
        (m, k), (n, k2) = a_shape, b_shape
    else:
        (k, m), (k2, n) = a_shape, b_shape
    assert k == k2, (a.shape, b.shape, dims)
    tm = _divisor_tile(m, tm, LANES)
    tn = _divisor_tile(n, tn, LANES)
    tk = _divisor_tile(k, tk, LANES)
    nk = k // tk
    dn = {"nn": NN, "nt": NT, "tn": TN}[dims]
    if dims == "tn":
        a_blk, a_idx, up_off = (tk, tm), (lambda i, j, kk: (kk, i)), m // tm
    else:
        a_blk, a_idx, up_off = (tm, tk), (lambda i, j, kk: (i, kk)), k // tk
    a_spec = pl.BlockSpec(a_blk, a_idx)
    up_spec = pl.BlockSpec(a_blk, lambda i, j, kk: (a_idx(i, j, kk)[0], a_idx(i, j, kk)[1] + up_off))
    b_blk, b_idx = ((tn, tk), lambda i, j, kk: (j, kk)) if dims == "nt" else ((tk, tn), lambda i, j, kk: (kk, j))
    if b.ndim == 3:
        b_spec = pl.BlockSpec((None,) + b_blk, lambda i, j, kk: (layer,) + b_idx(i, j, kk))
    else:
        b_spec = pl.BlockSpec(b_blk, b_idx)
    c_spec = pl.BlockSpec((tm, tn), lambda i, j, kk: (i, j))
    if isinstance(into, int):
        o_spec = pl.BlockSpec((None, tm, tn), lambda i, j, kk: (layer, i, j))
        out_shape = jax.ShapeDtypeStruct((into, m, n), out_dtype)
        into = None
    elif into is not None:
        assert into.shape[1:] == (m, n) and into.dtype == out_dtype
        o_spec = pl.BlockSpec((None, tm, tn), lambda i, j, kk: (layer, i, j))
        out_shape = jax.ShapeDtypeStruct(into.shape, into.dtype)
    else:
        o_spec = c_spec
        out_shape = jax.ShapeDtypeStruct((m, n), out_dtype)
    has_c = c is not None

    n_a = 2 if a_gated else 1

    def body(*refs):
        b_ref = refs[n_a]
        c_ref = refs[n_a + 1] if has_c else None
        acc_ref = refs[-1]
        o_ref = refs[-3] if a_gated else refs[-2]
        kk = pl.program_id(2)

        @pl.when(kk == 0)
        def _():
            if has_c:
                acc_ref[...] = c_ref[...].astype(F32)
            else:
                acc_ref[...] = jnp.zeros_like(acc_ref)

        if a_gated:
            a_val = (_silu(refs[0][...].astype(F32)) * refs[1][...].astype(F32)).astype(BF16)
            refs[-2][...] = a_val
        else:
            a_val = refs[0][...].astype(BF16)
        acc_ref[...] += lax.dot_general(a_val, b_ref[...].astype(BF16), dn, preferred_element_type=F32)

        @pl.when(kk == nk - 1)
        def _():
            o_ref[...] = acc_ref[...].astype(o_ref.dtype)

    ins = ([a, a] if a_gated else [a]) + [b] + ([c] if has_c else [])
    specs = ([a_spec, up_spec] if a_gated else [a_spec]) + [b_spec] + ([c_spec] if has_c else [])
    aliases = {}
    if into is not None:
        aliases = {len(ins): 0}
        ins.append(into)
        specs.append(pl.BlockSpec(memory_space=pl.ANY))
    if a_gated:
        assert dims == "nn" and n == tn
        o_spec, out_shape = [o_spec, a_spec], [out_shape, jax.ShapeDtypeStruct((m, k), BF16)]
    return pl.pallas_call(
        body, name=name, grid=(m // tm, n // tn, nk), in_specs=specs, out_specs=o_spec, out_shape=out_shape,
        scratch_shapes=[pltpu.VMEM((tm, tn), F32)], input_output_aliases=aliases,
        compiler_params=_params(("arbitrary", "arbitrary", "arbitrary")),
    )(*ins)


def _rowwise(fn, rows, params, outs, accs=(), *, name):
    t = rows[0][0].shape[0]
    widest = max([w for _, w, _ in rows] + [w for w, _ in outs])
    row_bytes = sum(w * a.dtype.itemsize for a, w, _ in rows) + sum(w * jnp.dtype(d).itemsize for w, d in outs)
    if 2 * NARROW_ROW_TILE * row_bytes <= NARROW_VMEM_BYTES:
        tm = min(NARROW_ROW_TILE, t)
    else:
        tm = min(WIDE_ROW_TILE if widest > WIDE_COLS else ROW_TILE, t)
    sub = SUB_ROWS
    while sub < MAX_SUB_ROWS and 2 * sub * widest <= VREG_FILE_ELEMS:
        sub *= 2
    assert t % tm == 0 and tm % sub == 0
    n_rows, n_par, n_out, n_acc = len(rows), len(params), len(outs), len(accs)

    def body(*refs):
        row_refs = refs[:n_rows]
        par_refs = refs[n_rows:n_rows + n_par]
        out_refs = refs[n_rows + n_par:n_rows + n_par + n_out]
        acc_refs = refs[n_rows + n_par + n_out:]
        if n_acc:
            @pl.when(pl.program_id(0) == 0)
            def _():
                for a_ref in acc_refs:
                    a_ref[...] = jnp.zeros_like(a_ref)

        def step(r, carry):
            sl = pl.ds(pl.multiple_of(r * sub, sub), sub)
            vals = [ref[sl, :].astype(F32) for ref in row_refs] + [ref[...] for ref in par_refs]
            res = fn(*vals)
            for o_ref, val in zip(out_refs, res[:n_out]):
                o_ref[sl, :] = val.astype(o_ref.dtype)
            for a_ref, val in zip(acc_refs, res[n_out:]):
                a_ref[...] += val
            return carry

        lax.fori_loop(0, tm // sub, step, 0)

    in_specs = [pl.BlockSpec((tm, w), functools.partial(lambda i, cb: (i, cb), cb=cb)) for _, w, cb in rows]
    in_specs += [pl.BlockSpec(p.shape, lambda i: (0, 0)) for p in params]
    out_specs = [pl.BlockSpec((tm, w), lambda i: (i, 0)) for w, _ in outs]
    out_specs += [pl.BlockSpec(s, lambda i: (0, 0)) for s in accs]
    out_shape = [jax.ShapeDtypeStruct((t, w), d) for w, d in outs]
    out_shape += [jax.ShapeDtypeStruct(s, F32) for s in accs]
    return pl.pallas_call(
        body, name=name, grid=(t // tm,), in_specs=in_specs, out_specs=out_specs, out_shape=out_shape,
        compiler_params=_params(("arbitrary",)),
    )(*[r[0] for r in rows], *params)


def _vjp_fn(fn, n_in, n_out):
    def bwd(*args):
        ins, cts = args[:n_in], args[n_in:]
        _, pull = jax.vjp(fn, *ins)
        return pull(tuple(cts) if n_out > 1 else cts[0])
    return bwd


def _lane(shape):
    return lax.broadcasted_iota(jnp.int32, shape, 1)


def _silu(x):
    return x * jax.nn.sigmoid(x)


def _softplus(x):
    return jnp.maximum(x, 0.0) + jnp.log1p(jnp.exp(-jnp.abs(x)))


def _heads(x, width=HEAD_DIM):
    return [x[:, h * width:(h + 1) * width] for h in range(N_HEADS)]


def _layer_norm(z, g, b):
    mu = jnp.mean(z, -1, keepdims=True)
    zc = z - mu
    var = jnp.mean(zc * zc, -1, keepdims=True)
    return zc * lax.rsqrt(var + LN_EPS) * g + b


def _gdn_act(u, misc, alog_row, dtb_row):
    s = _silu(u)
    w = N_HEADS * HEAD_DIM
    q = jnp.concatenate([t * lax.rsqrt(jnp.sum(t * t, -1, keepdims=True) + RMS_EPS) * HEAD_DIM ** -0.5
                         for t in _heads(s[:, :w])], axis=1)
    k = jnp.concatenate([t * lax.rsqrt(jnp.sum(t * t, -1, keepdims=True) + RMS_EPS)
                         for t in _heads(s[:, w:2 * w])], axis=1)
    v = s[:, 2 * w:]
    lane = _lane(misc.shape)
    beta = jax.nn.sigmoid(misc)
    g = -jnp.exp(alog_row) * _softplus(misc + dtb_row)
    is_beta = (lane >= MISC_BETA0) & (lane < MISC_BETA0 + N_HEADS)
    is_g = (lane >= MISC_A0) & (lane < MISC_A0 + N_HEADS)
    gb = jnp.where(is_beta, beta, jnp.where(is_g, g, 0.0))
    return q, k, v, gb


def _gdn_out(o, z, gn_row):
    outs = []
    for oh, zh in zip(_heads(o), _heads(z)):
        r = oh * lax.rsqrt(jnp.mean(oh * oh, -1, keepdims=True) + RMS_EPS) * gn_row
        outs.append(r * _silu(zh))
    return jnp.concatenate(outs, axis=1)


def _mla_norm(ckv, cq, kvg_row, qg_row):
    cqn = cq * lax.rsqrt(jnp.mean(cq * cq, -1, keepdims=True) + RMS_EPS) * qg_row
    ckvn = ckv * lax.rsqrt(jnp.mean(ckv * ckv, -1, keepdims=True) + RMS_EPS) * kvg_row
    return cqn, ckvn


def _swap_halves(x):
    half = ROPE_DIM // 2
    return jnp.where(_lane(x.shape) < half, pltpu.roll(x, LANES - half, 1), pltpu.roll(x, half, 1))


@jax.custom_vjp
def _rope(x, cos_t, sin_t):
    return x * cos_t + _swap_halves(x) * sin_t


def _rope_fwd(x, cos_t, sin_t):
    return _rope(x, cos_t, sin_t), (cos_t, sin_t)


def _rope_bwd(res, g):
    cos_t, sin_t = res
    return g * cos_t - _swap_halves(g) * sin_t, jnp.zeros_like(cos_t), jnp.zeros_like(sin_t)


_rope.defvjp(_rope_fwd, _rope_bwd)


def _mla_qk(scale, qm, kv, misc, cos_t, sin_t):
    krope = _rope(misc, cos_t, sin_t)
    qs, ks = [], []
    for h in range(N_HEADS):
        base = 2 * HEAD_DIM * h
        qs += [qm[:, base:base + HEAD_DIM], _rope(qm[:, base + HEAD_DIM:base + 2 * HEAD_DIM], cos_t, sin_t)]
        ks += [kv[:, HEAD_DIM * h:HEAD_DIM * (h + 1)], krope]
    return jnp.concatenate(qs, axis=1) * scale, jnp.concatenate(ks, axis=1), kv[:, N_HEADS * HEAD_DIM:]


def _swiglu(gu):
    f = gu.shape[1] // 2
    return _silu(gu[:, :f]) * gu[:, f:]


def _ple_out(x2, pg, pe):
    return x2 + jax.nn.sigmoid(pg) * pe


CONV_W = 4
HALO = 8
CONV_STRIP = 512


def _conv_fwd(h, conv_w, width, *, name, tm=ROW_TILE, sub=32):
    t = h.shape[0]
    tm = min(tm, t)
    nb = tm // HALO

    def body(x_ref, halo_ref, w_ref, u_ref, buf):
        i = pl.program_id(0)
        buf[pl.ds(0, HALO), :] = jnp.where(i > 0, halo_ref[...], 0.0)
        buf[pl.ds(HALO, tm), :] = x_ref[...]
        for c0 in range(0, width, CONV_STRIP):
            cols = pl.ds(c0, CONV_STRIP)
            w = w_ref[:, cols]
            for r0 in range(0, tm, sub):
                acc = jnp.zeros((sub, CONV_STRIP), F32)
                for j in range(CONV_W):
                    acc = acc + w[j:j + 1, :] * buf[pl.ds(HALO + r0 - (CONV_W - 1) + j, sub), cols]
                u_ref[pl.ds(r0, sub), cols] = acc

    return pl.pallas_call(
        body, name=name, grid=(t // tm,),
        in_specs=[pl.BlockSpec((tm, width), lambda i: (i, 0)),
                  pl.BlockSpec((HALO, width), lambda i: (jnp.maximum(i * nb - 1, 0), 0)),
                  pl.BlockSpec(conv_w.shape, lambda i: (0, 0))],
        out_specs=pl.BlockSpec((tm, width), lambda i: (i, 0)),
        out_shape=jax.ShapeDtypeStruct((t, width), F32),
        scratch_shapes=[pltpu.VMEM((tm + HALO, width), F32)],
        compiler_params=_params(("arbitrary",)),
    )(h, h, conv_w)


def _conv_bwd(du, h, conv_w, width, *, name, tm=ROW_TILE, sub=32):
    t = h.shape[0]
    tm = min(tm, t)
    nb = tm // HALO
    n_tiles = t // tm

    def body(du_ref, du_halo, x_ref, x_halo, w_ref, dx_ref, dw_ref, dbuf, xbuf):
        i = pl.program_id(0)

        @pl.when(i == 0)
        def _():
            dw_ref[...] = jnp.zeros_like(dw_ref)

        dbuf[pl.ds(0, tm), :] = du_ref[...]
        dbuf[pl.ds(tm, HALO), :] = jnp.where(i < n_tiles - 1, du_halo[...], 0.0)
        xbuf[pl.ds(0, HALO), :] = jnp.where(i > 0, x_halo[...], 0.0)
        xbuf[pl.ds(HALO, tm), :] = x_ref[...]
        for c0 in range(0, width, CONV_STRIP):
            cols = pl.ds(c0, CONV_STRIP)
            w = w_ref[:, cols]
            dws = [jnp.zeros((HALO, CONV_STRIP), F32) for _ in range(CONV_W)]
            for r0 in range(0, tm, sub):
                acc = jnp.zeros((sub, CONV_STRIP), F32)
                d_here = dbuf[pl.ds(r0, sub), cols]
                for j in range(CONV_W):
                    acc = acc + w[j:j + 1, :] * dbuf[pl.ds(r0 + (CONV_W - 1) - j, sub), cols]
                    prod = d_here * xbuf[pl.ds(HALO + r0 - (CONV_W - 1) + j, sub), cols]
                    for g0 in range(0, sub, HALO):
                        dws[j] = dws[j] + prod[g0:g0 + HALO, :]
                dx_ref[pl.ds(r0, sub), cols] = acc.astype(dx_ref.dtype)
            for j in range(CONV_W):
                dw_ref[pl.ds(j, 1), cols] += jnp.sum(dws[j], axis=0, keepdims=True)

    return pl.pallas_call(
        body, name=name, grid=(n_tiles,),
        in_specs=[pl.BlockSpec((tm, width), lambda i: (i, 0)),
                  pl.BlockSpec((HALO, width), lambda i: (jnp.minimum((i + 1) * nb, t // HALO - 1), 0)),
                  pl.BlockSpec((tm, width), lambda i: (i, 0)),
                  pl.BlockSpec((HALO, width), lambda i: (jnp.maximum(i * nb - 1, 0), 0)),
                  pl.BlockSpec(conv_w.shape, lambda i: (0, 0))],
        out_specs=[pl.BlockSpec((tm, width), lambda i: (i, 0)),
                   pl.BlockSpec((HALO, width), lambda i: (0, 0))],
        out_shape=[jax.ShapeDtypeStruct((t, width), BF16), jax.ShapeDtypeStruct((HALO, width), F32)],
        scratch_shapes=[pltpu.VMEM((tm + HALO, width), F32), pltpu.VMEM((tm + HALO, width), F32)],
        compiler_params=_params(("arbitrary",)),
    )(du, du, h, h, conv_w)


@jax.custom_vjp
def _inv_unit_lower(low):
    n = low.shape[-1]
    eye = (lax.broadcasted_iota(jnp.int32, (n, n), 0) == lax.broadcasted_iota(jnp.int32, (n, n), 1)).astype(F32)
    x = eye - low
    p = low
    span = 2
    while span < n:
        p = _nn(p, p)
        x = x + _nn(x, p)
        span *= 2
    return x


def _inv_fwd(low):
    x = _inv_unit_lower(low)
    return x, x


def _inv_bwd(x, g):
    return (-_tn(x, _nt(g, x)),)


_inv_unit_lower.defvjp(_inv_fwd, _inv_bwd)


@jax.custom_vjp
def _inv_known(low, inverse):
    return inverse


_inv_known.defvjp(lambda low, inverse: (inverse, inverse), lambda x, g: (_inv_bwd(x, g)[0], jnp.zeros_like(x)))


def _gdn_prep(q, k, v, gb, known_inverse=None):
    c = CHUNK
    n = q.shape[0] // c
    pairs = [(g, h) for g in range(n) for h in range(N_HEADS)]
    row = lax.broadcasted_iota(jnp.int32, (c, c), 0)
    col = lax.broadcasted_iota(jnp.int32, (c, c), 1)
    tri_incl = row >= col
    tri_strict = row > col
    lane = _lane((c, LANES))
    sub = lax.broadcasted_iota(jnp.int32, (LANES, c), 0)
    last = lax.broadcasted_iota(jnp.int32, (c, 1), 0) == c - 1

    def split(x):
        return jnp.stack([x[g * c:(g + 1) * c, h * HEAD_DIM:(h + 1) * HEAD_DIM] for g, h in pairs])

    gbs = [gb[g * c:(g + 1) * c, :] for g in range(n)]
    gbts = [x.T for x in gbs]
    g_col = jnp.stack([jnp.sum(jnp.where(lane == MISC_A0 + h, gbs[g], 0.0), axis=1, keepdims=True) for g, h in pairs])
    b_col = jnp.stack([jnp.sum(jnp.where(lane == MISC_BETA0 + h, gbs[g], 0.0), axis=1, keepdims=True) for g, h in pairs])
    g_row = jnp.stack([jnp.sum(jnp.where(sub == MISC_A0 + h, gbts[g], 0.0), axis=0, keepdims=True) for g, h in pairs])
    gc_col = jnp.sum(jnp.where(tri_incl, g_row, 0.0), axis=2, keepdims=True)
    gc_row = jnp.sum(jnp.where(row <= col, g_col, 0.0), axis=1, keepdims=True)
    decay = jnp.where(tri_incl, jnp.exp(jnp.where(tri_incl, gc_col - gc_row, 0.0)), 0.0)
    g_last = jnp.sum(jnp.where(last, gc_col, 0.0), axis=1, keepdims=True)
    qs, ks, vs = split(q), split(k), split(v)
    kb = ks * b_col
    low = jnp.where(tri_strict, _nt(kb, ks) * decay, 0.0)
    if known_inverse is None:
        tinv = _inv_unit_lower(low)
    else:
        tinv = _inv_known(low, jnp.stack([known_inverse[g * c:(g + 1) * c, h * c:(h + 1) * c] for g, h in pairs]))
    eg = jnp.exp(gc_col)
    sol = _nn(tinv, jnp.concatenate([vs * b_col, kb * eg], axis=2))
    attn = jnp.where(tri_incl, _nt(qs, ks) * decay, 0.0)
    qd = qs * eg
    kd = ks * jnp.exp(g_last - gc_col)

    def merge(x):
        return jnp.concatenate([jnp.concatenate([x[g * N_HEADS + h] for h in range(N_HEADS)], axis=1)
                                for g in range(n)], axis=0)

    glb = jnp.concatenate([sum(jnp.where(lane == h, g_last[g * N_HEADS + h], 0.0) for h in range(N_HEADS))
                           for g in range(n)], axis=0)
    outs = (merge(sol[:, :, :HEAD_DIM]), merge(sol[:, :, HEAD_DIM:]), merge(qd), merge(kd), merge(attn), glb)
    return outs, merge(tinv)


def _gdn_seq(state, u, w, qd, kd, attn, glb):
    c = u.shape[0]
    first = lax.broadcasted_iota(jnp.int32, glb.shape, 0) == 0
    lane = _lane(glb.shape)
    heads = lambda x: jnp.stack([x[:, h * HEAD_DIM:(h + 1) * HEAD_DIM] for h in range(N_HEADS)])
    g_last = jnp.stack([jnp.sum(jnp.sum(jnp.where(first & (lane == h), glb, 0.0), axis=1, keepdims=True),
                                axis=0, keepdims=True) for h in range(N_HEADS)])
    s = jnp.stack([state[h * HEAD_DIM:(h + 1) * HEAD_DIM, :] for h in range(N_HEADS)])
    at = jnp.stack([attn[:, h * c:(h + 1) * c] for h in range(N_HEADS)])
    v_new = heads(u) - _nn(heads(w), s)
    o = _nn(heads(qd), s) + _nn(at, v_new)
    s_new = s * jnp.exp(g_last) + _tn(heads(kd), v_new)
    return (jnp.concatenate([o[h] for h in range(N_HEADS)], axis=1),
            jnp.concatenate([s_new[h] for h in range(N_HEADS)], axis=0))


PREP_CHUNKS = 8
PREP_CHUNKS_BWD = 4
SEQ_CHUNKS = 8


def _gdn_prep_fwd(q, k, v, gb, *, name):
    t, w = q.shape
    rows = min(PREP_CHUNKS * CHUNK, t)

    def body(q_ref, k_ref, v_ref, gb_ref, *out_refs):
        outs, inverse = _gdn_prep(q_ref[...], k_ref[...], v_ref[...], gb_ref[...])
        for o_ref, val in zip(out_refs, outs + (inverse,)):
            o_ref[...] = val

    spec = lambda width: pl.BlockSpec((rows, width), lambda i: (i, 0))
    widths = [w, w, w, w, N_HEADS * CHUNK, LANES, N_HEADS * CHUNK]
    res = pl.pallas_call(
        body, name=name, grid=(t // rows,),
        in_specs=[spec(w), spec(w), spec(w), spec(LANES)],
        out_specs=[spec(x) for x in widths],
        out_shape=[jax.ShapeDtypeStruct((t, x), F32) for x in widths],
        compiler_params=_params(("arbitrary",)),
    )(q, k, v, gb)
    return tuple(res[:6]), res[6]


def _gdn_prep_bwd(q, k, v, gb, inverse, cts, *, name):
    t, w = q.shape
    rows = min(PREP_CHUNKS_BWD * CHUNK, t)

    def body(q_ref, k_ref, v_ref, gb_ref, inv_ref, du, dw, dqd, dkd, dattn, dglb, dq_ref, dk_ref, dv_ref, dgb_ref):
        known = inv_ref[...]
        _, pull = jax.vjp(lambda a, b, c_, d_: _gdn_prep(a, b, c_, d_, known)[0],
                          q_ref[...], k_ref[...], v_ref[...], gb_ref[...])
        dq, dk, dv, dgb = pull(tuple(r[...] for r in (du, dw, dqd, dkd, dattn, dglb)))
        dq_ref[...] = dq
        dk_ref[...] = dk
        dv_ref[...] = dv
        dgb_ref[...] = dgb

    spec = lambda width: pl.BlockSpec((rows, width), lambda i: (i, 0))
    widths = [w, w, w, w, N_HEADS * CHUNK, LANES]
    return pl.pallas_call(
        body, name=name, grid=(t // rows,),
        in_specs=[spec(w), spec(w), spec(w), spec(LANES), spec(N_HEADS * CHUNK)] + [spec(x) for x in widths],
        out_specs=[spec(w), spec(w), spec(w), spec(LANES)],
        out_shape=[jax.ShapeDtypeStruct((t, w), F32)] * 3 + [jax.ShapeDtypeStruct((t, LANES), F32)],
        compiler_params=_params(("arbitrary",)),
    )(q, k, v, gb, inverse, *cts)


def _gdn_seq_fwd(prep, *, name):
    t, w = prep[0].shape
    rows = min(SEQ_CHUNKS * CHUNK, t)
    per = rows // CHUNK

    def body(u_ref, w_ref, qd_ref, kd_ref, at_ref, gl_ref, o_ref, sall_ref, s_scr):
        @pl.when(pl.program_id(0) == 0)
        def _():
            s_scr[...] = jnp.zeros_like(s_scr)

        def step(j, carry):
            sl = pl.ds(pl.multiple_of(j * CHUNK, CHUNK), CHUNK)
            s = s_scr[...]
            sall_ref[j] = s
            o, s_new = _gdn_seq(s, u_ref[sl, :], w_ref[sl, :], qd_ref[sl, :], kd_ref[sl, :], at_ref[sl, :], gl_ref[sl, :])
            o_ref[sl, :] = o
            s_scr[...] = s_new
            return carry

        lax.fori_loop(0, per, step, 0)

    spec = lambda width: pl.BlockSpec((rows, width), lambda i: (i, 0))
    widths = [w, w, w, w, N_HEADS * CHUNK, LANES]
    return pl.pallas_call(
        body, name=name, grid=(t // rows,),
        in_specs=[spec(x) for x in widths],
        out_specs=[spec(w), pl.BlockSpec((per, w, HEAD_DIM), lambda i: (i, 0, 0))],
        out_shape=[jax.ShapeDtypeStruct((t, w), F32), jax.ShapeDtypeStruct((t // CHUNK, w, HEAD_DIM), F32)],
        scratch_shapes=[pltpu.VMEM((w, HEAD_DIM), F32)],
        compiler_params=_params(("arbitrary",)),
    )(*prep)


def _gdn_seq_bwd(prep, s_all, do, *, name):
    t, w = prep[0].shape
    rows = min(SEQ_CHUNKS * CHUNK, t)
    per = rows // CHUNK
    n = t // rows

    def body(u_ref, w_ref, qd_ref, kd_ref, at_ref, gl_ref, sall_ref, do_ref, du, dw, dqd, dkd, dat, dgl, ds_scr):
        @pl.when(pl.program_id(0) == 0)
        def _():
            ds_scr[...] = jnp.zeros_like(ds_scr)

        def step(jj, carry):
            j = per - 1 - jj
            sl = pl.ds(pl.multiple_of(j * CHUNK, CHUNK), CHUNK)
            _, pull = jax.vjp(_gdn_seq, sall_ref[j], u_ref[sl, :], w_ref[sl, :], qd_ref[sl, :], kd_ref[sl, :],
                              at_ref[sl, :], gl_ref[sl, :])
            res = pull((do_ref[sl, :], ds_scr[...]))
            ds_scr[...] = res[0]
            for o_ref, val in zip((du, dw, dqd, dkd, dat, dgl), res[1:]):
                o_ref[sl, :] = val
            return carry

        lax.fori_loop(0, per, step, 0)

    spec = lambda width: pl.BlockSpec((rows, width), lambda i: (n - 1 - i, 0))
    widths = [w, w, w, w, N_HEADS * CHUNK, LANES]
    return pl.pallas_call(
        body, name=name, grid=(n,),
        in_specs=[spec(x) for x in widths] + [pl.BlockSpec((per, w, HEAD_DIM), lambda i: (n - 1 - i, 0, 0)), spec(w)],
        out_specs=[spec(x) for x in widths],
        out_shape=[jax.ShapeDtypeStruct((t, x), F32) for x in widths],
        scratch_shapes=[pltpu.VMEM((w, HEAD_DIM), F32)],
        compiler_params=_params(("arbitrary",)),
    )(*prep, s_all, do)


QK_DIM = 2 * HEAD_DIM
ATT_TILE = 1024
NEG = -1e30


ATT_SPLIT = 4


def _chunk_mask(n_rows, n_cols, key_major, query_offset):
    r = lax.broadcasted_iota(jnp.int32, (n_rows, n_cols), 0)
    c = lax.broadcasted_iota(jnp.int32, (n_rows, n_cols), 1)
    if key_major:
        return r // CHUNK <= (c + query_offset) // CHUNK
    return c // CHUNK <= (r + query_offset) // CHUNK


def _visible_keys(tile, diagonal):
    hq = tile // ATT_SPLIT
    return [(a + 1) * hq if diagonal else tile for a in range(ATT_SPLIT)]


def _dot_nn(a, b):
    return lax.dot_general(a, b, NN, preferred_element_type=F32)


def _blocked_transpose(x, width):
    t = x.shape[0]
    tile = min(ATT_TILE, t)
    return x.reshape(t // tile, tile, N_HEADS * width).transpose(0, 2, 1).reshape(t // tile, N_HEADS, width, tile)


def _attn_fwd(q, kt, v1, *, name):
    t = q.shape[0]
    tq = min(ATT_TILE, t)
    nq = t // tq

    def body(q_ref, kt_ref, v_ref, o_ref, lse_ref, m_scr, acc_scr):
        qi = pl.program_id(1)
        m_scr[...] = jnp.full_like(m_scr, NEG)
        acc_scr[...] = jnp.zeros_like(acc_scr)
        hq = tq // ATT_SPLIT
        parts = [pl.ds(a * hq, hq) for a in range(ATT_SPLIT)]
        qs = [q_ref[sl, :] for sl in parts]

        def step(kj, masked):
            rows = pl.ds(pl.multiple_of(kj * tq, tq), tq)
            kt_blk, vv = kt_ref[kj], v_ref[rows, :]
            seen = _visible_keys(tq, masked)
            ss = [_dot_nn(qv, kt_blk[:, :w]) for qv, w in zip(qs, seen)]
            for a, sl in enumerate(parts):
                s = ss[a]
                if masked:
                    s = jnp.where(_chunk_mask(hq, seen[a], False, a * hq), s, NEG)
                m_old = m_scr[sl, :]
                m_new = jnp.maximum(m_old, jnp.max(s, axis=1, keepdims=True))
                p = jnp.exp(s - m_new)
                acc_scr[sl, :] = jnp.exp(m_old - m_new) * acc_scr[sl, :] + _dot_nn(p.astype(BF16), vv[:seen[a], :])
                m_scr[sl, :] = m_new

        def loop_body(kj, carry):
            step(kj, False)
            return carry

        lax.fori_loop(0, qi, loop_body, 0)
        step(qi, True)
        acc = acc_scr[...]
        o_ref[...] = (acc[:, :HEAD_DIM] / acc[:, HEAD_DIM:]).astype(o_ref.dtype)
        lse_ref[...] = m_scr[...] + jnp.log(acc[:, HEAD_DIM:HEAD_DIM + 1])

    return pl.pallas_call(
        body, name=name, grid=(N_HEADS, nq),
        in_specs=[pl.BlockSpec((tq, QK_DIM), lambda h, i: (i, h)),
                  pl.BlockSpec((nq, None, QK_DIM, tq), lambda h, i: (0, h, 0, 0)),
                  pl.BlockSpec((t, 2 * HEAD_DIM), lambda h, i: (0, h))],
        out_specs=[pl.BlockSpec((tq, HEAD_DIM), lambda h, i: (i, h)),
                   pl.BlockSpec((None, tq, 1), lambda h, i: (h, i, 0))],
        out_shape=[jax.ShapeDtypeStruct((t, N_HEADS * HEAD_DIM), BF16),
                   jax.ShapeDtypeStruct((N_HEADS, t, 1), F32)],
        scratch_shapes=[pltpu.VMEM((tq, 1), F32), pltpu.VMEM((tq, 2 * HEAD_DIM), F32)],
        compiler_params=_params(("arbitrary", "arbitrary")),
    )(q, kt, v1)


def _attn_delta(dom, o, *, name):
    hw = o.shape[1]

    def fn(do, ov):
        lane = _lane((do.shape[0], LANES))
        out = jnp.zeros((do.shape[0], LANES), F32)
        for h, (a, b) in enumerate(zip(_heads(do), _heads(ov))):
            out = out + jnp.where(lane == h, jnp.sum(a * b, axis=1, keepdims=True), 0.0)
        return (out,)

    return _rowwise(fn, [(dom, hw, 1), (o, hw, 0)], [], [(LANES, F32)], name=name)[0]


def _attn_bwd(q, qt, k, v, lse_row, delta_row, do, dot, *, name):
    t = q.shape[0]
    tk = min(ATT_TILE, t)
    nk = t // tk

    def body(q_ref, qt_ref, k_ref, v_ref, lse_ref, delta_ref, do_ref, dot_ref, dk_ref, dv_ref, dq_ref, dk_scr, dv_scr):
        kj = pl.program_id(1)

        @pl.when(kj == 0)
        def _():
            dq_ref[...] = jnp.zeros_like(dq_ref)

        dk_scr[...] = jnp.zeros_like(dk_scr)
        dv_scr[...] = jnp.zeros_like(dv_scr)
        kv_ = k_ref[...]
        vv = v_ref[...]
        hq = tk // ATT_SPLIT

        def step(qi, masked):
            lse_v, delta_v = lse_ref[qi], delta_ref[qi]
            qt_blk, dot_blk = qt_ref[qi], dot_ref[qi]
            rows = [pl.ds(pl.multiple_of(qi * tk + a * hq, hq), hq) for a in range(ATT_SPLIT)]
            qs = [q_ref[r, :] for r in rows]
            dos = [do_ref[r, :] for r in rows]
            seen = _visible_keys(tk, masked)
            ss = [_dot_nn(kv_[:seen[a], :], qt_blk[:, a * hq:(a + 1) * hq]) for a in range(ATT_SPLIT)]
            dps = [_dot_nn(vv[:seen[a], :], dot_blk[:, a * hq:(a + 1) * hq]) for a in range(ATT_SPLIT)]
            for a in range(ATT_SPLIT):
                cols = slice(a * hq, (a + 1) * hq)
                keys = pl.ds(0, seen[a])
                p = jnp.exp(ss[a] - lse_v[:, cols])
                if masked:
                    p = jnp.where(_chunk_mask(seen[a], hq, True, a * hq), p, 0.0)
                dv_scr[keys, :] += _dot_nn(p.astype(BF16), dos[a])
                ds = (p * (dps[a] - delta_v[:, cols])).astype(BF16)
                dk_scr[keys, :] += _dot_nn(ds, qs[a])
                dq_ref[rows[a], :] += lax.dot_general(ds, kv_[:seen[a], :], TN, preferred_element_type=F32)

        step(kj, True)

        def loop_body(qi, carry):
            step(qi, False)
            return carry

        lax.fori_loop(kj + 1, nk, loop_body, 0)
        dk_ref[...] = dk_scr[...].astype(dk_ref.dtype)
        dv_ref[...] = dv_scr[...].astype(dv_ref.dtype)

    once = dict(pipeline_mode=pl.Buffered(1))
    stat = pl.BlockSpec((None, nk, 1, tk), lambda h, j: (h, 0, 0, 0))
    return pl.pallas_call(
        body, name=name, grid=(N_HEADS, nk),
        in_specs=[pl.BlockSpec((t, QK_DIM), lambda h, j: (0, h), **once),
                  pl.BlockSpec((nk, None, QK_DIM, tk), lambda h, j: (0, h, 0, 0), **once),
                  pl.BlockSpec((tk, QK_DIM), lambda h, j: (j, h)),
                  pl.BlockSpec((tk, HEAD_DIM), lambda h, j: (j, h)),
                  stat, stat,
                  pl.BlockSpec((t, HEAD_DIM), lambda h, j: (0, h), **once),
                  pl.BlockSpec((nk, None, HEAD_DIM, tk), lambda h, j: (0, h, 0, 0), **once)],
        out_specs=[pl.BlockSpec((tk, QK_DIM), lambda h, j: (j, h)),
                   pl.BlockSpec((tk, HEAD_DIM), lambda h, j: (j, h)),
                   pl.BlockSpec((t, QK_DIM), lambda h, j: (0, h))],
        out_shape=[jax.ShapeDtypeStruct((t, N_HEADS * QK_DIM), BF16),
                   jax.ShapeDtypeStruct((t, N_HEADS * HEAD_DIM), BF16),
                   jax.ShapeDtypeStruct((t, N_HEADS * QK_DIM), F32)],
        scratch_shapes=[pltpu.VMEM((tk, QK_DIM), F32), pltpu.VMEM((tk, HEAD_DIM), F32)],
        compiler_params=_params(("arbitrary", "arbitrary")),
    )(q, qt, k, v, lse_row, delta_row, do, dot)


def _rope_tables(pos_col, inv_freq_row, *, name):
    t = pos_col.shape[0]
    tm = min(ROW_TILE, t)

    def body(p_ref, f_ref, c_ref, s_ref):
        ang = p_ref[...].astype(F32) * f_ref[...]
        lane = _lane(ang.shape)
        c_ref[...] = jnp.where(lane < ROPE_DIM, jnp.cos(ang), 0.0)
        sn = jnp.sin(ang)
        s_ref[...] = jnp.where(lane < ROPE_DIM // 2, -sn, jnp.where(lane < ROPE_DIM, sn, 0.0))

    out = pl.BlockSpec((tm, LANES), lambda i: (i, 0))
    return pl.pallas_call(
        body, name=name, grid=(t // tm,),
        in_specs=[pl.BlockSpec((tm, 1), lambda i: (i, 0)), pl.BlockSpec((1, LANES), lambda i: (0, 0))],
        out_specs=[out, out], out_shape=[jax.ShapeDtypeStruct((t, LANES), F32)] * 2,
        compiler_params=_params(("arbitrary",)),
    )(pos_col, inv_freq_row)


def _loss_head(y, target):
    width = y.shape[1]

    def fn(yv, tv):
        e = yv - tv
        part = 0.5 * jnp.sum(jnp.mean(e * e, axis=1, keepdims=True), axis=0, keepdims=True)
        return e * (1.0 / width), jnp.broadcast_to(part, (HALO, LANES))

    return _rowwise(fn, [(y, width, 0), (target, width, 0)], [], [(width, F32)], [(HALO, LANES)], name="loss_head")


def _adamw(w, g, m, v, *, name):
    shape = w.shape
    w2, g2, m2, v2 = (a.reshape(-1, shape[-1]) for a in (w, g, m, v))
    rows, width = w2.shape
    tr = _divisor_tile(rows, max(8, (1 << 19) // max(width, 1)), 8)
    bc1 = 1.0 - ADAM_B1 ** ADAM_STEP
    bc2 = 1.0 - ADAM_B2 ** ADAM_STEP

    def body(w_ref, g_ref, m_ref, v_ref, d_ref, mo_ref, vo_ref):
        gv = g_ref[...]
        mn = ADAM_B1 * m_ref[...] + (1.0 - ADAM_B1) * gv
        vn = ADAM_B2 * v_ref[...] + (1.0 - ADAM_B2) * (gv * gv)
        d_ref[...] = -ADAM_LR * ((mn / bc1) / (jnp.sqrt(vn / bc2) + ADAM_EPS) + ADAM_WD * w_ref[...])
        mo_ref[...] = mn
        vo_ref[...] = vn

    spec = pl.BlockSpec((tr, width), lambda i: (i, 0))
    outs = pl.pallas_call(
        body, name=name, grid=(rows // tr,), in_specs=[spec] * 4, out_specs=[spec] * 3,
        out_shape=[jax.ShapeDtypeStruct((rows, width), F32)] * 3,
        compiler_params=_params(("arbitrary",)),
    )(w2, g2, m2, v2)
    return tuple(o.reshape(shape) for o in outs)


HBM_SPEC = pl.BlockSpec(memory_space=pltpu.HBM)


def _position():
    return lax.axis_index("x"), lax.axis_index("y"), lax.axis_index("c")


def _other_chips(x, y):
    return [(1 - x, y), (x, 1 - y), (1 - x, 1 - y)]


class _Stream:
    def __init__(self, kind, size=0):
        self.kind, self.size = kind, size
        self.parts = 2 if kind == "heads" else 1

    def local(self, ref, k, part):
        if self.kind == "rows":
            return ref.at[:, pl.ds(k * self.size, self.size), :]
        if self.kind == "cols":
            return ref.at[:, :, pl.ds(k * self.size, self.size)]
        if self.kind == "heads":
            return ref.at[:, :, pl.ds(part * N_HEADS * HEAD_DIM + k * HEAD_DIM, HEAD_DIM)]
        if self.kind == "piece":
            return ref.at[k]
        return ref

    def shard(self, ref, part):
        if self.kind == "heads":
            return ref.at[:, :, pl.ds(part * HEAD_DIM, HEAD_DIM)]
        return ref

    def half_local(self, ref, k, part, cc, hd):
        if self.kind == "piece":
            return ref.at[k, pl.ds(cc * hd, hd)]
        return self.local(ref.at[pl.ds(cc * hd, hd)], k, part)


def _remote(src, dst, send_sems, recv_sems, idx, to):
    return pltpu.make_async_remote_copy(src_ref=src, dst_ref=dst, send_sem=send_sems.at[idx],
                                        recv_sem=recv_sems.at[idx], device_id=to, device_id_type=MESH)


def _comm_call(body, ins, out_shapes, n_remote, n_local, *, name):
    scratch = [pltpu.SemaphoreType.DMA((n_remote,)), pltpu.SemaphoreType.DMA((n_remote,))]
    if n_local:
        scratch.append(pltpu.SemaphoreType.DMA((n_local,)))
    return pl.pallas_call(
        body, name=name, in_specs=[HBM_SPEC] * len(ins), out_specs=[HBM_SPEC] * len(out_shapes), out_shape=out_shapes,
        scratch_shapes=scratch, compiler_params=pltpu.CompilerParams(has_side_effects=True),
    )(*ins)


def _gather_chips(shards, streams, out_shapes, *, name):
    n = len(shards)
    hd = shards[0].shape[0] // 2
    flat = [(t, part) for t in range(n) for part in range(streams[t].parts)]
    ns = len(flat)

    def body(*refs):
        s_refs, o_refs = refs[:n], refs[n:2 * n]
        send_sems, recv_sems = refs[2 * n:]
        x, y, c = _position()
        sibling = (x, y, 1 - c)
        chips = _other_chips(x, y)
        me = 2 * x + y
        sent = []
        for s, (t, part) in enumerate(flat):
            st = streams[t]
            sent.append(_remote(st.shard(s_refs[t], part), st.local(o_refs[t], me, part), send_sems, recv_sems,
                                6 * ns + s, sibling))
            sent[-1].start()
            src = st.shard(s_refs[t].at[pl.ds(c * hd, hd)], part)
            for j, (cx, cy) in enumerate(chips[:2]):
                sent.append(_remote(src, st.half_local(o_refs[t], me, part, c, hd), send_sems, recv_sems,
                                    3 * s + j, (cx, cy, c)))
                sent[-1].start()
        for s, (t, part) in enumerate(flat):
            st = streams[t]
            for j, (cx, cy) in enumerate(chips[:2]):
                blk = st.half_local(o_refs[t], 2 * cx + cy, part, c, hd)
                _remote(blk, blk, send_sems, recv_sems, 3 * s + j, (x, y, c)).wait_recv()
                sent.append(_remote(blk, blk, send_sems, recv_sems, 3 * ns + 3 * s + j, sibling))
                sent[-1].start()
                if s % 2 == j:
                    ox, oy = chips[1 - j]
                    sent.append(_remote(blk, blk, send_sems, recv_sems, 3 * s + 2, (ox, oy, c)))
                    sent[-1].start()
        for s, (t, part) in enumerate(flat):
            st = streams[t]
            cx, cy = chips[2]
            blk = st.half_local(o_refs[t], 2 * cx + cy, part, c, hd)
            _remote(blk, blk, send_sems, recv_sems, 3 * s + 2, (x, y, c)).wait_recv()
            sent.append(_remote(blk, blk, send_sems, recv_sems, 3 * ns + 3 * s + 2, sibling))
            sent[-1].start()
        for s, (t, part) in enumerate(flat):
            st = streams[t]
            for j, (cx, cy) in enumerate(chips):
                blk = st.half_local(o_refs[t], 2 * cx + cy, part, 1 - c, hd)
                _remote(blk, blk, send_sems, recv_sems, 3 * ns + 3 * s + j, (x, y, c)).wait_recv()
            own = st.local(o_refs[t], me, part)
            _remote(own, own, send_sems, recv_sems, 6 * ns + s, (x, y, c)).wait_recv()
        for cp in sent:
            cp.wait_send()

    return _comm_call(body, shards, out_shapes, 7 * ns, 0, name=name)


def _sibling_take_other_half(gs, *, name):
    n = len(gs)
    hd = gs[0].shape[0] // 2

    def body(*refs):
        g_refs, o_refs = refs[:n], refs[n:2 * n]
        send_sems, recv_sems = refs[2 * n:]
        x, y, c = _position()
        copies = [_remote(g_refs[t].at[pl.ds((1 - c) * hd, hd)], o_refs[t], send_sems, recv_sems, t, (x, y, 1 - c))
                  for t in range(n)]
        for cp in copies:
            cp.start()
        for cp in copies:
            cp.wait()

    outs = [jax.ShapeDtypeStruct((hd,) + g.shape[1:], g.dtype) for g in gs]
    return _comm_call(body, gs, outs, n, 0, name=name)


def _chips_exchange(ps, streams, shard_shapes, *, name):
    n = len(ps)
    flat = [(t, part) for t in range(n) for part in range(streams[t].parts)]

    def body(*refs):
        p_refs, o_refs = refs[:n], refs[n:2 * n]
        send_sems, recv_sems = refs[2 * n:]
        x, y, c = _position()
        copies = []
        for s, (t, part) in enumerate(flat):
            st = streams[t]
            for j, (cx, cy) in enumerate(_other_chips(x, y)):
                copies.append(_remote(st.local(p_refs[t], 2 * cx + cy, part), st.shard(o_refs[t].at[j], part),
                                      send_sems, recv_sems, 3 * s + j, (cx, cy, c)))
        for cp in copies:
            cp.start()
        for cp in copies:
            cp.wait()

    outs = [jax.ShapeDtypeStruct((3,) + tuple(shp), p.dtype) for p, shp in zip(ps, shard_shapes)]
    return _comm_call(body, ps, outs, 3 * len(flat), 0, name=name)


def _sibling_join_halves(bufs, *, name):
    n = len(bufs)
    hd = bufs[0].shape[0] // 2

    def body(*refs):
        o_refs = refs[n:2 * n]
        send_sems, recv_sems = refs[2 * n:]
        x, y, c = _position()
        sent = []
        for t in range(n):
            mine = o_refs[t].at[pl.ds(c * hd, hd)]
            sent.append(_remote(mine, mine, send_sems, recv_sems, t, (x, y, 1 - c)))
            sent[-1].start()
        for t in range(n):
            theirs = o_refs[t].at[pl.ds((1 - c) * hd, hd)]
            _remote(theirs, theirs, send_sems, recv_sems, t, (x, y, c)).wait_recv()
        for cp in sent:
            cp.wait_send()

    return pl.pallas_call(
        body, name=name, in_specs=[HBM_SPEC] * n, out_specs=[HBM_SPEC] * n,
        out_shape=[jax.ShapeDtypeStruct(b.shape, b.dtype) for b in bufs],
        scratch_shapes=[pltpu.SemaphoreType.DMA((n,)), pltpu.SemaphoreType.DMA((n,))],
        input_output_aliases={t: t for t in range(n)},
        compiler_params=pltpu.CompilerParams(has_side_effects=True),
    )(*bufs)


def _row_tile(rows, width):
    return _divisor_tile(rows, max(16, (1 << 19) // width), 16)


def _add_own_half(g, got, c_idx, out_dtype, *, name):
    hd, r, w = got.shape
    tr = _row_tile(r, w)

    def body(c_ref, g_ref, a_ref, o_ref):
        o_ref[...] = (g_ref[...].astype(F32) + a_ref[...].astype(F32)).astype(o_ref.dtype)

    return pl.pallas_call(
        body, name=name,
        grid_spec=pltpu.PrefetchScalarGridSpec(
            num_scalar_prefetch=1, grid=(hd, r // tr),
            in_specs=[pl.BlockSpec((None, None, tr, w), lambda l, i, c_ref: (c_ref[0], l, i, 0)),
                      pl.BlockSpec((None, tr, w), lambda l, i, c_ref: (l, i, 0))],
            out_specs=pl.BlockSpec((None, tr, w), lambda l, i, c_ref: (l, i, 0))),
        out_shape=jax.ShapeDtypeStruct((hd, r, w), out_dtype),
        compiler_params=_params(("arbitrary", "arbitrary")),
    )(c_idx, g.reshape((2, hd) + g.shape[1:]), got)


def _sum_chips(p, got, place, stream, *, name):
    _, hd, rs, cs = got.shape
    wb = HEAD_DIM if stream.kind == "heads" else cs
    tr = _row_tile(rs, wb)
    kind, size = stream.kind, stream.size

    def own_index(l, i, g, k_ref, c_ref):
        k = k_ref[0]
        if kind == "rows":
            return (l, k * (size // tr) + i, 0)
        if kind == "cols":
            return (l, i, k)
        if kind == "heads":
            return (l, i, g * N_HEADS + k)
        if kind == "piece":
            return (k, l, i, 0)
        return (l, i, 0)

    own_blk = (None, None, tr, wb) if kind == "piece" else (None, tr, wb)

    def body(k_ref, c_ref, p_ref, fx_ref, fy_ref, fxy_ref, o_ref):
        f = lambda r: r[...].astype(F32)
        o_ref[...] = (f(p_ref) + f(fy_ref)) + (f(fx_ref) + f(fxy_ref))

    def rel(j):
        return pl.BlockSpec((None, None, tr, wb), functools.partial(lambda l, i, g, k_ref, c_ref, j: (j, l, i, g), j=j))

    return pl.pallas_call(
        body, name=name,
        grid_spec=pltpu.PrefetchScalarGridSpec(
            num_scalar_prefetch=2, grid=(hd, rs // tr, stream.parts),
            in_specs=[pl.BlockSpec(own_blk, own_index), rel(0), rel(1), rel(2)],
            out_specs=pl.BlockSpec((None, tr, wb), lambda l, i, g, k_ref, c_ref: (c_ref[0] * hd + l, i, g))),
        out_shape=jax.ShapeDtypeStruct((2 * hd, rs, cs), F32),
        compiler_params=_params(("arbitrary", "arbitrary", "arbitrary")),
    )(place[0], place[1], p, got, got, got)


MATRICES = ("w_in", "w_uq", "w_ukv", "w_out", "w_gate_up", "w_down", "w_ple", "w_ple_gate", "conv_w")
VECTORS = ("a_log", "dt_bias", "gdn_norm_g", "q_norm_g", "kv_norm_g", "ln1_g", "ln1_b", "ln2_g", "ln2_b")
WEIGHTS = ("w_in", "conv_w", "a_log", "dt_bias", "gdn_norm_g", "q_norm_g", "w_uq", "kv_norm_g", "w_ukv", "w_out",
           "ln1_g", "ln1_b", "w_gate_up", "w_down", "ln2_g", "ln2_b", "w_ple", "w_ple_gate")
ROW_SHARDED = ("w_out", "w_down", "w_ple_gate")
N_CHIPS = 4


def _stream_of(name, shard_shape):
    if name in ("w_in", "w_uq"):
        return _Stream("piece")
    if name == "w_ukv":
        return _Stream("heads")
    if name in ROW_SHARDED:
        return _Stream("rows", shard_shape[0])
    return _Stream("cols", shard_shape[1])


def _pack_vectors(vecs, depth):
    flat = jnp.concatenate([vecs[n].reshape(depth, -1) for n in VECTORS], axis=1)
    pad = jnp.zeros((depth, VEC_ROWS * LANES - flat.shape[1]), F32)
    return jnp.concatenate([flat, pad], axis=1).reshape(depth, VEC_ROWS, LANES)


def _unpack_vectors(packed, shapes):
    depth = packed.shape[0]
    flat = packed.reshape(depth, VEC_ROWS * LANES)
    out, off = {}, 0
    for n in VECTORS:
        out[n] = flat[:, off:off + shapes[n][1]]
        off += shapes[n][1]
    return out


VEC_ROWS = 40


class _Dims:
    def __init__(self, d_model, in_width, q_lora, kv_lora, d_ff2, ple_dim):
        self.d = d_model
        self.hw = N_HEADS * HEAD_DIM
        self.in_width = in_width
        self.q_lora, self.kv_lora = q_lora, kv_lora
        self.ff2 = d_ff2
        self.ple = ple_dim
        self.c_kv0 = 4 * self.hw
        self.c_q0 = self.c_kv0 + kv_lora
        self.misc0 = self.c_q0 + q_lora
        self.h_width = self.misc0 + LANES
        assert self.c_kv0 % kv_lora == 0 and self.c_q0 % q_lora == 0 and self.misc0 % LANES == 0
        self.g_beta = 4 * self.hw
        self.g_a = self.g_beta + N_HEADS
        self.g_cq = self.g_a + N_HEADS
        self.g_ckv = self.g_cq + q_lora
        self.g_kr = self.g_ckv + kv_lora
        assert self.g_kr + ROPE_DIM == in_width

    def w_in_local(self, w):
        pad = jnp.zeros(w.shape[:-1] + (self.h_width - self.in_width,), w.dtype)
        return jnp.concatenate([w[..., :self.g_beta], w[..., self.g_ckv:self.g_kr], w[..., self.g_cq:self.g_ckv],
                                w[..., self.g_kr:], w[..., self.g_beta:self.g_cq], pad], axis=-1)

    def w_in_global(self, d):
        m = self.misc0
        return jnp.concatenate([d[..., :self.c_kv0], d[..., m + MISC_BETA0:m + MISC_A0 + N_HEADS],
                                d[..., self.c_q0:self.misc0], d[..., self.c_kv0:self.c_q0], d[..., m:m + ROPE_DIM]],
                               axis=-1)

    def w_uq_local(self, w):
        r = w.reshape(w.shape[:-1] + (N_HEADS, HEAD_DIM + ROPE_DIM))
        r = jnp.pad(r, [(0, 0)] * (r.ndim - 1) + [(0, QK_DIM - HEAD_DIM - ROPE_DIM)])
        return r.reshape(w.shape[:-1] + (N_HEADS * QK_DIM,))

    def w_uq_global(self, d):
        r = d.reshape(d.shape[:-1] + (N_HEADS, QK_DIM))[..., :HEAD_DIM + ROPE_DIM]
        return r.reshape(d.shape[:-1] + (N_HEADS * (HEAD_DIM + ROPE_DIM),))


def _lane_padded(n):
    return -(-n // LANES) * LANES


def _pad_lanes(a):
    pad = _lane_padded(a.shape[-1]) - a.shape[-1]
    return a if pad == 0 else jnp.pad(a, [(0, 0)] * (a.ndim - 1) + [(0, pad)])


def _lane_row(vec, lane0):
    pad = LANES - lane0 - vec.shape[0]
    return jnp.concatenate([jnp.zeros((lane0,), F32), vec.astype(F32), jnp.zeros((pad,), F32)])[None, :]


def _layer_fwd(dm, alpha, x, xb, p_i, cos_t, sin_t, wl, tag):
    d, hw = dm.d, dm.hw
    nm = lambda s: f"{s}_{tag}"
    mm = functools.partial(_matmul, layer=wl["layer"])
    h = mm(xb, wl["w_in"], dims="nn", name=nm("f_in"))
    misc_cb = dm.misc0 // LANES

    u = _conv_fwd(h, wl["conv_w"], 3 * hw, name=nm("f_conv"))
    qn, kn, vg, gb = _rowwise(_gdn_act, [(u, 3 * hw, 0), (h, LANES, misc_cb)], [wl["alog_row"], wl["dtb_row"]],
                              [(hw, F32), (hw, F32), (hw, F32), (LANES, F32)], name=nm("f_gdn_act"))
    prep, tinv = _gdn_prep_fwd(qn, kn, vg, gb, name=nm("f_gdn_prep"))
    o_gdn, s_all = _gdn_seq_fwd(prep, name=nm("f_gdn_seq"))
    (og,) = _rowwise(lambda o, z, g: (_gdn_out(o, z, g),), [(o_gdn, hw, 0), (h, hw, 3)], [wl["gn_row"]],
                     [(hw, BF16)], name=nm("f_gdn_out"))

    cqn, ckvn = _rowwise(_mla_norm, [(h, dm.kv_lora, dm.c_kv0 // dm.kv_lora), (h, dm.q_lora, dm.c_q0 // dm.q_lora)],
                         [wl["kvg_row"], wl["qg_row"]], [(dm.q_lora, BF16), (dm.kv_lora, BF16)], name=nm("f_mla_norm"))
    qm = mm(cqn, wl["w_uq"], dims="nn", name=nm("f_uq"))
    kvm = mm(ckvn, wl["w_ukv"], dims="nn", name=nm("f_ukv"))
    scale = (HEAD_DIM + ROPE_DIM) ** -0.5
    qk_fn = functools.partial(_mla_qk, scale)
    qa, ka, va = _rowwise(qk_fn, [(qm, N_HEADS * QK_DIM, 0), (kvm, 2 * hw, 0), (h, LANES, misc_cb),
                                  (cos_t, LANES, 0), (sin_t, LANES, 0)], [],
                          [(N_HEADS * QK_DIM, BF16), (N_HEADS * QK_DIM, BF16), (hw, BF16)], name=nm("f_mla_qk"))
    kt = _blocked_transpose(ka, QK_DIM)
    ones = jnp.ones((va.shape[0], HEAD_DIM), va.dtype)
    v1 = jnp.concatenate([part for vh in _heads(va) for part in (vh, ones)], axis=1)
    o_mla, lse = _attn_fwd(qa, kt, v1, name=nm("f_attn"))

    om = jnp.concatenate([og, o_mla], axis=1)
    mix = mm(om, wl["w_out"], dims="nn", name=nm("f_out"))
    ln1 = lambda xv, yv, g, b: (_layer_norm(alpha * xv + yv, g, b),) * 2
    x1, x1b = _rowwise(ln1, [(x, d, 0), (mix, d, 0)], [wl["ln1_g"], wl["ln1_b"]], [(d, F32), (d, BF16)], name=nm("f_ln1"))

    gu = mm(x1b, wl["w_gate_up"], dims="nn", name=nm("f_gate_up"), out_dtype=BF16)
    dn, act = mm(gu, wl["w_down"], dims="nn", name=nm("f_down"), a_gated=True, tm=GATED_TILE)
    x2, x2b = _rowwise(ln1, [(x1, d, 0), (dn, d, 0)], [wl["ln2_g"], wl["ln2_b"]], [(d, F32), (d, BF16)], name=nm("f_ln2"))

    pg = mm(x2b, wl["w_ple_gate"], dims="nn", name=nm("f_ple_gate"))
    pe = mm(p_i, wl["w_ple"], dims="nn", name=nm("f_ple"))
    out, outb = _rowwise(lambda a, b, c_: (_ple_out(a, b, c_),) * 2, [(x2, d, 0), (pg, d, 0), (pe, d, 0)], [],
                         [(d, F32), (d, BF16)], name=nm("f_ple_out"))
    saved = dict(x=x, xb=xb, p_i=p_i, h=h, u=u, qn=qn, kn=kn, vg=vg, gb=gb, prep=prep, tinv=tinv, s_all=s_all, o_gdn=o_gdn, cqn=cqn, ckvn=ckvn,
                 qm=qm, kvm=kvm, qa=qa, ka=ka, va=va, o_mla=o_mla, lse=lse, om=om, mix=mix, x1=x1, x1b=x1b, gu=gu,
                 act=act, dn=dn, x2=x2, x2b=x2b, pg=pg, pe=pe)
    return out, outb, saved


def _layer_bwd(dm, alpha, dout, sv, cos_t, sin_t, wl, gbuf, tag):
    d, hw = dm.d, dm.hw
    t = dout.shape[0]
    nm = lambda s: f"{s}_{tag}"
    gr = {}
    gbuf = dict(gbuf)
    misc_cb = dm.misc0 // LANES
    mm = functools.partial(_matmul, layer=wl["layer"])

    def wgrad(name_, a, g):
        gbuf[name_] = mm(a, g, dims="tn", name=nm("b_" + name_), into=gbuf[name_], tm=1408, tn=1408, tk=1024,
                         out_dtype=BF16)

    dx2_a, dpg, dpe = _rowwise(_vjp_fn(_ple_out, 3, 1), [(sv["x2"], d, 0), (sv["pg"], d, 0), (sv["pe"], d, 0), (dout, d, 0)],
                               [], [(d, F32), (d, BF16), (d, BF16)], name=nm("b_ple_out"))
    wgrad("w_ple", sv["p_i"], dpe)
    wgrad("w_ple_gate", sv["x2b"], dpg)
    dx2 = mm(dpg, wl["w_ple_gate"], dims="nt", c=dx2_a, name=nm("b_x2"))

    def ln_bwd(xv, yv, ct, g, b):
        _, pull = jax.vjp(lambda a_, b_, c_, d_: _layer_norm(alpha * a_ + b_, c_, d_), xv, yv, g, b)
        return pull(ct)

    dx1_a, ddn, gr["ln2_g"], gr["ln2_b"] = _rowwise(
        ln_bwd, [(sv["x1"], d, 0), (sv["dn"], d, 0), (dx2, d, 0)], [wl["ln2_g"], wl["ln2_b"]],
        [(d, F32), (d, BF16)], [(1, d), (1, d)], name=nm("b_ln2"))
    wgrad("w_down", sv["act"], ddn)
    dact = mm(ddn, wl["w_down"], dims="nt", name=nm("b_act"), out_dtype=BF16)
    (dgu,) = _rowwise(_vjp_fn(_swiglu, 1, 1), [(sv["gu"], dm.ff2, 0), (dact, dm.ff2 // 2, 0)], [], [(dm.ff2, BF16)],
                      name=nm("b_swiglu"))
    wgrad("w_gate_up", sv["x1b"], dgu)
    dx1 = mm(dgu, wl["w_gate_up"], dims="nt", c=dx1_a, name=nm("b_x1"))

    dx_a, dmix, gr["ln1_g"], gr["ln1_b"] = _rowwise(
        ln_bwd, [(sv["x"], d, 0), (sv["mix"], d, 0), (dx1, d, 0)], [wl["ln1_g"], wl["ln1_b"]],
        [(d, F32), (d, BF16)], [(1, d), (1, d)], name=nm("b_ln1"))
    wgrad("w_out", sv["om"], dmix)
    dom = mm(dmix, wl["w_out"], dims="nt", name=nm("b_om"))

    nq = t // min(ATT_TILE, t)
    delta = _attn_delta(dom, sv["o_mla"], name=nm("b_attn_delta"))
    lse_row = sv["lse"].reshape(N_HEADS, nq, 1, t // nq)
    delta_row = delta[:, :N_HEADS].T.reshape(N_HEADS, nq, 1, t // nq)
    do_b = dom[:, hw:].astype(BF16)
    dka, dva, dqa = _attn_bwd(sv["qa"], _blocked_transpose(sv["qa"], QK_DIM), sv["ka"], sv["va"], lse_row, delta_row,
                              do_b, _blocked_transpose(do_b, HEAD_DIM), name=nm("b_attn"))
    scale = (HEAD_DIM + ROPE_DIM) ** -0.5
    qk_fn = functools.partial(_mla_qk, scale)

    def qk_bwd(qm, kvm, misc, cs, sn, g_q, g_k, g_v):
        _, pull = jax.vjp(lambda a, b, c_: qk_fn(a, b, c_, cs, sn), qm, kvm, misc)
        return pull((g_q, g_k, g_v))

    dqm, dkvm, dmisc_rope = _rowwise(
        qk_bwd, [(sv["qm"], N_HEADS * QK_DIM, 0), (sv["kvm"], 2 * hw, 0), (sv["h"], LANES, misc_cb), (cos_t, LANES, 0),
                 (sin_t, LANES, 0), (dqa, N_HEADS * QK_DIM, 0), (dka, N_HEADS * QK_DIM, 0), (dva, hw, 0)], [],
        [(N_HEADS * QK_DIM, BF16), (2 * hw, BF16), (LANES, F32)], name=nm("b_mla_qk"))
    wgrad("w_uq", sv["cqn"], dqm)
    wgrad("w_ukv", sv["ckvn"], dkvm)
    dcqn = mm(dqm, wl["w_uq"], dims="nt", name=nm("b_cqn"))
    dckvn = mm(dkvm, wl["w_ukv"], dims="nt", name=nm("b_ckvn"))

    def norm_bwd(ckv, cq, g_q, g_kv, kvg, qg):
        _, pull = jax.vjp(_mla_norm, ckv, cq, kvg, qg)
        return pull((g_q, g_kv))

    dckv, dcq, gr["kvg_row"], gr["qg_row"] = _rowwise(
        norm_bwd, [(sv["h"], dm.kv_lora, dm.c_kv0 // dm.kv_lora), (sv["h"], dm.q_lora, dm.c_q0 // dm.q_lora),
                   (dcqn, dm.q_lora, 0), (dckvn, dm.kv_lora, 0)], [wl["kvg_row"], wl["qg_row"]],
        [(dm.kv_lora, BF16), (dm.q_lora, BF16)], [(1, dm.kv_lora), (1, dm.q_lora)], name=nm("b_mla_norm"))

    def gout_bwd(o, z, g_o, gn):
        _, pull = jax.vjp(_gdn_out, o, z, gn)
        return pull(g_o)

    do_gdn, dz, gr["gn_row"] = _rowwise(gout_bwd, [(sv["o_gdn"], hw, 0), (sv["h"], hw, 3), (dom, hw, 0)], [wl["gn_row"]],
                                        [(hw, F32), (hw, BF16)], [(1, HEAD_DIM)], name=nm("b_gdn_out"))
    dprep = _gdn_seq_bwd(sv["prep"], sv["s_all"], do_gdn, name=nm("b_gdn_seq"))
    dqn, dkn, dvg, dgb = _gdn_prep_bwd(sv["qn"], sv["kn"], sv["vg"], sv["gb"], sv["tinv"], dprep, name=nm("b_gdn_prep"))

    def act_bwd(u, misc, g_q, g_k, g_v, g_gb, g_rope, alog, dtb):
        _, pull = jax.vjp(_gdn_act, u, misc, alog, dtb)
        du_, dmisc_, dalog_, ddtb_ = pull((g_q, g_k, g_v, g_gb))
        return du_, dmisc_ + g_rope, dalog_, ddtb_

    du, dmisc, gr["alog_row"], gr["dtb_row"] = _rowwise(
        act_bwd, [(sv["u"], 3 * hw, 0), (sv["h"], LANES, misc_cb), (dqn, hw, 0), (dkn, hw, 0), (dvg, hw, 0),
                  (dgb, LANES, 0), (dmisc_rope, LANES, 0)], [wl["alog_row"], wl["dtb_row"]],
        [(3 * hw, F32), (LANES, BF16)], [(1, LANES), (1, LANES)], name=nm("b_gdn_act"))
    dqkv, dconv = _conv_bwd(du, sv["h"], wl["conv_w"], 3 * hw, name=nm("b_conv"))
    gr["conv_w"] = dconv[:CONV_W]

    dh = jnp.concatenate([dqkv, dz, dckv, dcq, dmisc], axis=1)
    wgrad("w_in", sv["xb"], dh)
    dx = mm(dh, wl["w_in"], dims="nt", c=dx_a, name=nm("b_x"), tk=1408)
    return dx, gbuf, gr


LOCAL_MATRICES = ("w_in", "w_uq", "w_ukv", "w_out", "w_gate_up", "w_down", "w_ple", "w_ple_gate")


def _layer_weights(mats, vecs, layer):
    wl = {n: mats[n] for n in LOCAL_MATRICES}
    wl["layer"] = layer
    wl["conv_w"] = mats["conv_w"][layer]
    wl["alog_row"] = _lane_row(vecs["a_log"][layer], MISC_A0)
    wl["dtb_row"] = _lane_row(vecs["dt_bias"][layer], MISC_A0)
    wl["gn_row"] = vecs["gdn_norm_g"][layer][None, :]
    wl["qg_row"] = vecs["q_norm_g"][layer][None, :]
    wl["kvg_row"] = vecs["kv_norm_g"][layer][None, :]
    for n in ("ln1_g", "ln1_b", "ln2_g", "ln2_b"):
        wl[n] = vecs[n][layer][None, :]
    return wl


def _vector_grads(gr):
    out = {"a_log": gr["alog_row"][0, MISC_A0:MISC_A0 + N_HEADS], "dt_bias": gr["dtb_row"][0, MISC_A0:MISC_A0 + N_HEADS],
           "gdn_norm_g": gr["gn_row"][0], "q_norm_g": gr["qg_row"][0], "kv_norm_g": gr["kvg_row"][0]}
    for n in ("ln1_g", "ln1_b", "ln2_g", "ln2_b"):
        out[n] = gr[n][0]
    return out


def _local_step(dm, x, p, positions, target, mats, vecs):
    depth = p.shape[0]
    alpha = (2.0 * depth) ** 0.25
    freq = ROPE_THETA ** (-jnp.arange(0, ROPE_DIM, 2, dtype=F32) / ROPE_DIM)
    inv_freq_row = _lane_row(jnp.concatenate([freq, freq]), 0)
    cos_t, sin_t = _rope_tables(positions.reshape(-1, 1), inv_freq_row, name="rope_tables")

    wls = [_layer_weights(mats, vecs, i) for i in range(depth)]
    saved = []
    cur, cur_b = x, x
    for i in range(depth):
        cur, cur_b, sv = _layer_fwd(dm, alpha, cur, cur_b, p[i], cos_t, sin_t, wls[i], f"l{i}")
        saved.append(sv)
    dy, loss_blk = _loss_head(cur, target)
    gbuf = {n: depth for n in LOCAL_MATRICES}
    conv_g, vec_g = [None] * depth, [None] * depth
    for i in reversed(range(depth)):
        dy, gbuf, gr = _layer_bwd(dm, alpha, dy, saved[i], cos_t, sin_t, wls[i], gbuf, f"l{i}")
        conv_g[i] = gr["conv_w"]
        vec_g[i] = _vector_grads(gr)
    vec_grads = {n: jnp.stack([vec_g[i][n] for i in range(depth)]) for n in VECTORS}
    return loss_blk[0, 0], dy, gbuf, jnp.stack(conv_g), vec_grads


def kernel(x, p, positions, w_in, conv_w, a_log, dt_bias, gdn_norm_g, q_norm_g, w_uq, kv_norm_g, w_ukv, w_out, ln1_g, ln1_b, w_gate_up, w_down, ln2_g, ln2_b, w_ple, w_ple_gate, loss_target, m_w_in, m_conv_w, m_a_log, m_dt_bias, m_gdn_norm_g, m_q_norm_g, m_w_uq, m_kv_norm_g, m_w_ukv, m_w_out, m_ln1_g, m_ln1_b, m_w_gate_up, m_w_down, m_ln2_g, m_ln2_b, m_w_ple, m_w_ple_gate, v_w_in, v_conv_w, v_a_log, v_dt_bias, v_gdn_norm_g, v_q_norm_g, v_w_uq, v_kv_norm_g, v_w_ukv, v_w_out, v_ln1_g, v_ln1_b, v_w_gate_up, v_w_down, v_ln2_g, v_ln2_b, v_w_ple, v_w_ple_gate):
    w = dict(w_in=w_in, conv_w=conv_w, a_log=a_log, dt_bias=dt_bias, gdn_norm_g=gdn_norm_g, q_norm_g=q_norm_g, w_uq=w_uq,
             kv_norm_g=kv_norm_g, w_ukv=w_ukv, w_out=w_out, ln1_g=ln1_g, ln1_b=ln1_b, w_gate_up=w_gate_up, w_down=w_down,
             ln2_g=ln2_g, ln2_b=ln2_b, w_ple=w_ple, w_ple_gate=w_ple_gate)
    m = dict(w_in=m_w_in, conv_w=m_conv_w, a_log=m_a_log, dt_bias=m_dt_bias, gdn_norm_g=m_gdn_norm_g, q_norm_g=m_q_norm_g,
             w_uq=m_w_uq, kv_norm_g=m_kv_norm_g, w_ukv=m_w_ukv, w_out=m_w_out, ln1_g=m_ln1_g, ln1_b=m_ln1_b,
             w_gate_up=m_w_gate_up, w_down=m_w_down, ln2_g=m_ln2_g, ln2_b=m_ln2_b, w_ple=m_w_ple, w_ple_gate=m_w_ple_gate)
    v = dict(w_in=v_w_in, conv_w=v_conv_w, a_log=v_a_log, dt_bias=v_dt_bias, gdn_norm_g=v_gdn_norm_g, q_norm_g=v_q_norm_g,
             w_uq=v_w_uq, kv_norm_g=v_kv_norm_g, w_ukv=v_w_ukv, w_out=v_w_out, ln1_g=v_ln1_g, ln1_b=v_ln1_b,
             w_gate_up=v_w_gate_up, w_down=v_w_down, ln2_g=v_ln2_g, ln2_b=v_ln2_b, w_ple=v_w_ple, w_ple_gate=v_w_ple_gate)
    depth = w_in.shape[0]
    assert depth % 2 == 0
    hd = depth // 2
    dm = _Dims(x.shape[2], N_CHIPS * w_in.shape[2], w_uq.shape[1], w_ukv.shape[1], N_CHIPS * w_gate_up.shape[2], p.shape[3])
    cx, cy, cc = lax.axis_index("x"), lax.axis_index("y"), lax.axis_index("c")
    chip = 2 * cx + cy

    g_streams = [_stream_of(n, w[n].shape[1:]) for n in MATRICES]
    shards = [w[n] if n == "conv_w" else w[n].astype(BF16) for n in MATRICES]
    shards = [_pad_lanes(s) if st.kind == "piece" else s for s, st in zip(shards, g_streams)]
    g_shapes = []
    for s, st in zip(shards, g_streams):
        if st.kind == "piece":
            shape = (N_CHIPS,) + s.shape
        elif st.kind == "rows":
            shape = (depth, N_CHIPS * s.shape[1], s.shape[2])
        else:
            shape = (depth, s.shape[1], N_CHIPS * s.shape[2])
        g_shapes.append(jax.ShapeDtypeStruct(shape, s.dtype))
    mats = dict(zip(MATRICES, _gather_chips(shards, g_streams, g_shapes, name="gather_weights")))
    for n, to_local in (("w_in", dm.w_in_local), ("w_uq", dm.w_uq_local)):
        pieces = jnp.moveaxis(mats[n][..., :w[n].shape[2]], 0, 2)
        mats[n] = to_local(pieces.reshape(pieces.shape[:2] + (-1,)))
    vecs = {n: w[n] for n in VECTORS}

    loss_local, grad_x, gbuf, conv_g, vec_g = _local_step(dm, x[0], p[:, 0], positions[0], loss_target[0], mats, vecs)
    loss = lax.psum(loss_local, ("x", "y", "c"))

    names = list(LOCAL_MATRICES) + ["conv_w", "vectors"]
    gs = [gbuf[n] for n in LOCAL_MATRICES] + [conv_g, _pack_vectors(vec_g, depth)]
    wire = [BF16] * len(LOCAL_MATRICES) + [F32, F32]
    r_streams = [_stream_of(n, w[n].shape[1:]) for n in LOCAL_MATRICES]
    r_streams += [_stream_of("conv_w", w["conv_w"].shape[1:]), _Stream("whole")]
    shard_shapes = [(hd, w[n].shape[1], _lane_padded(w[n].shape[2])) if st.kind == "piece" else (hd,) + w[n].shape[1:]
                    for n, st in zip(LOCAL_MATRICES, r_streams)]
    shard_shapes += [(hd,) + w["conv_w"].shape[1:], (hd, VEC_ROWS, LANES)]
    c_idx = cc.reshape(1).astype(jnp.int32)
    place = (chip.reshape(1).astype(jnp.int32), c_idx)
    from_sibling = _sibling_take_other_half(gs, name="reduce_sibling")
    chip_sum = [_add_own_half(g, a, c_idx, dt, name=f"reduce_add_{n}")
                for g, a, dt, n in zip(gs, from_sibling, wire, names)]
    for i, n in enumerate(names):
        if r_streams[i].kind == "piece":
            glob = dm.w_in_global(chip_sum[i]) if n == "w_in" else dm.w_uq_global(chip_sum[i])
            glob = glob.reshape(glob.shape[:2] + (N_CHIPS, glob.shape[2] // N_CHIPS))
            chip_sum[i] = jnp.moveaxis(_pad_lanes(glob), 2, 0)
    from_chips = _chips_exchange(chip_sum, r_streams, shard_shapes, name="reduce_chips")
    halves = [_sum_chips(ps, got, place, st, name=f"reduce_sum_{n}")
              for ps, got, st, n in zip(chip_sum, from_chips, r_streams, names)]
    joined = dict(zip(names, _sibling_join_halves(halves, name="reduce_join")))
    joined.update(_unpack_vectors(joined.pop("vectors"), {n: w[n].shape for n in VECTORS}))

    grad_w, delta_w, new_m, new_v = {}, {}, {}, {}
    for n in WEIGHTS:
        grad_w[n] = joined[n][..., :w[n].shape[-1]]
        delta_w[n], new_m[n], new_v[n] = _adamw(w[n], grad_w[n], m[n], v[n], name=f"adamw_{n}")
    return (loss, grad_x[None], *[grad_w[n] for n in WEIGHTS], *[delta_w[n] for n in WEIGHTS],
            *[new_m[n] for n in WEIGHTS], *[new_v[n] for n in WEIGHTS])
```
